```python
import numpy as np
import jax
import jax.numpy as jnp
from jax import lax

D_MODEL = 1024
BATCH = 16
SEQ = 2048
DEPTH = 1

GRID_W = 64
CTX_LEN = 256

RET_HEAD_DIM = 128
RET_WIDTH = D_MODEL // 2
RET_HEADS = RET_WIDTH // RET_HEAD_DIM
RET_CHUNK = 128
RET_EPS = 1e-5
ROPE_BASE = 10000.0

RWKV_HEAD_DIM = 64
RWKV_WIDTH = D_MODEL // 2
RWKV_HEADS = RWKV_WIDTH // RWKV_HEAD_DIM
DECAY_LORA = 64
ICLR_LORA = 64
GATE_LORA = 128
RWKV_EPS = 64e-5

N_EXPERTS = 16
EXPERT_FF = 1024
CAPACITY_FACTOR = 2

N_MOD = 6
NORM_EPS = 1e-6

STATE_SIZES = (RET_WIDTH, RET_WIDTH, RWKV_WIDTH, RWKV_WIDTH, DECAY_LORA, ICLR_LORA)
QUERY_SIZES = (RET_WIDTH, RET_WIDTH, RWKV_WIDTH, GATE_LORA, 2 * D_MODEL)
STATE_COLS = sum(STATE_SIZES)
IN_COLS = STATE_COLS + sum(QUERY_SIZES)
SHIFT_STATE_COLS = 2 * RWKV_WIDTH + DECAY_LORA + ICLR_LORA
SHIFT_QUERY_COLS = RWKV_WIDTH + GATE_LORA
SHIFT_COLS = SHIFT_STATE_COLS + SHIFT_QUERY_COLS

kernel_name = 'hybrid_retention_rwkv7_ecmoe_dit_layer'


def _split(z, sizes):
    offs = np.cumsum((0,) + tuple(sizes))
    return [z[..., int(offs[i]):int(offs[i + 1])] for i in range(len(sizes))]


def _ident(a):
    return a


def _flip(a):
    return a[:, ::-1]


def _heads(a, dim):
    return a.reshape(a.shape[:-1] + (a.shape[-1] // dim, dim))


def _rmsnorm(x, g):
    xf = x.astype(jnp.float32)
    y = xf * lax.rsqrt(jnp.mean(xf * xf, axis=-1, keepdims=True) + NORM_EPS)
    return (y * g.astype(jnp.float32)).astype(x.dtype)


def _head_norm(x, eps):
    xf = x.astype(jnp.float32)
    mu = jnp.mean(xf, axis=-1, keepdims=True)
    var = jnp.mean(jnp.square(xf - mu), axis=-1, keepdims=True)
    y = (xf - mu) * lax.rsqrt(var + eps)
    return y.reshape(y.shape[:-2] + (-1,))


def _rope_2d(x):
    n = x.shape[1]
    t = jnp.arange(n)
    half = x.shape[-1] // 2
    nfreq = half // 2
    inv = ROPE_BASE ** (-jnp.arange(nfreq, dtype=jnp.float32) / nfreq)
    ang = jnp.concatenate([(t // GRID_W).astype(jnp.float32)[:, None] * inv,
                           (t % GRID_W).astype(jnp.float32)[:, None] * inv], axis=-1)
    cos = jnp.cos(ang)[None, :, None, :].astype(x.dtype)
    sin = jnp.sin(ang)[None, :, None, :].astype(x.dtype)
    x1, x2 = x[..., :half], x[..., half:]
    return jnp.concatenate([x1 * cos - x2 * sin, x1 * sin + x2 * cos], axis=-1)


def _centred_shift(p, mu):
    prev = jnp.pad(p[:, :-1], ((0, 0), (1, 0), (0, 0)))
    nxt = jnp.pad(p[:, 1:], ((0, 0), (0, 1), (0, 0)))
    return p + mu[0] * (prev - p) + mu[1] * (nxt - p)


def _retention_chunkwise(q, k, v, log_gamma, r0):
    b, t, h, _ = q.shape
    dv = v.shape[-1]
    n_chunks = t // RET_CHUNK
    lg = log_gamma.astype(jnp.float32)
    pos = jnp.arange(RET_CHUNK, dtype=jnp.float32)
    diff = pos[:, None] - pos[None, :]
    intra = jnp.where(diff >= 0, jnp.exp(lg[:, None, None] * jnp.maximum(diff, 0.0)), 0.0)
    cross = jnp.exp(lg[None, :] * (pos[:, None] + 1.0))
    tail = jnp.exp(lg[None, :] * (RET_CHUNK - 1.0 - pos[:, None]))
    chunk_decay = jnp.exp(lg * RET_CHUNK)

    def to_chunks(a):
        a = a.astype(jnp.float32).reshape(b, n_chunks, RET_CHUNK, h, a.shape[-1])
        return jnp.moveaxis(a, 1, 0)

    def step(r, xs):
        qc, kc, vc = xs
        s = jnp.einsum('bihd,bjhd->bhij', qc, kc) * intra
        o = (jnp.einsum('bhij,bjhe->bihe', s, vc)
             + jnp.einsum('bihd,bhde->bihe', qc, r) * cross[None, :, :, None])
        r = r * chunk_decay[None, :, None, None] + jnp.einsum('bjhd,bjhe->bhde', kc * tail[None, :, :, None], vc)
        return r, o

    r_final, o = lax.scan(step, r0.astype(jnp.float32), (to_chunks(q), to_chunks(k), to_chunks(v)))
    o = jnp.moveaxis(o, 0, 1).reshape(b, t, h, dv)
    return o.astype(v.dtype), r_final


def _retention_final_state(k, v, log_gamma):
    t = k.shape[1]
    w = jnp.exp(log_gamma[None, :] * (t - 1.0 - jnp.arange(t, dtype=jnp.float32)[:, None]))
    return jnp.einsum('bjhd,bjhe->bhde', k.astype(jnp.float32) * w[None, :, :, None], v.astype(jnp.float32))


def _rwkv7_scan(s0, w, k, v, kk, a, r=None):
    seq = [w, k, v, kk, a] + ([] if r is None else [r])
    xs = tuple(jnp.moveaxis(t.astype(jnp.float32), 1, 0) for t in seq)

    def step(s, xt):
        wt, kt, vt, kkt, at = xt[:5]
        sa = jnp.einsum('bhvk,bhk->bhv', s, -kkt)
        s = (s * wt[:, :, None, :] + sa[..., None] * (kkt * at)[:, :, None, :]
             + vt[..., None] * kt[:, :, None, :])
        if r is None:
            return s, None
        return s, jnp.einsum('bhvk,bhk->bhv', s, xt[5])

    s, ys = lax.scan(step, s0, xs)
    if r is None:
        return s, None
    return s, jnp.moveaxis(ys, 0, 1).astype(v.dtype)


def _side_state(h, lp, latent):
    z = h @ lp['w_in'][:, :STATE_COLS]
    ret_k, ret_v = _split(z[..., :2 * RET_WIDTH], (RET_WIDTH, RET_WIDTH))
    zr = _centred_shift(z[..., 2 * RET_WIDTH:], lp['rwkv_mu'][:, :SHIFT_STATE_COLS])
    k, v, wd, ad = _split(zr, (RWKV_WIDTH, RWKV_WIDTH, DECAY_LORA, ICLR_LORA))
    ret_k = _heads(ret_k, RET_HEAD_DIM) * (RET_HEAD_DIM ** -0.5)
    if latent:
        ret_k = _rope_2d(ret_k)
    kk = _heads(k * lp['rwkv_k_k'], RWKV_HEAD_DIM).astype(jnp.float32)
    kk = kk * lax.rsqrt(jnp.sum(kk * kk, axis=-1, keepdims=True) + 1e-12)
    u = lp['rwkv_w0'][:, None, None, :] + jnp.einsum('btr,drc->dbtc', jnp.tanh(wd), lp['rwkv_w2'])
    w = jnp.exp(-jnp.exp(-jax.nn.softplus(-u.astype(jnp.float32)) - 0.5))
    a = jax.nn.sigmoid(lp['rwkv_a0'][:, None, None, :] + jnp.einsum('btr,drc->dbtc', ad, lp['rwkv_a2']))
    k_dir = k[None] * (1.0 + (a - 1.0) * lp['rwkv_k_a'])
    return {'ret_k': ret_k, 'ret_v': _heads(ret_v, RET_HEAD_DIM), 'kk': kk,
            'w': _heads(w, RWKV_HEAD_DIM), 'a': _heads(a, RWKV_HEAD_DIM),
            'k': _heads(k_dir, RWKV_HEAD_DIM), 'v': _heads(v, RWKV_HEAD_DIM)}


def _side_query(h, lp, latent):
    z = h @ lp['w_in'][:, STATE_COLS:]
    ret_q, ret_g = _split(z[..., :2 * RET_WIDTH], (RET_WIDTH, RET_WIDTH))
    zr = _centred_shift(z[..., 2 * RET_WIDTH:2 * RET_WIDTH + SHIFT_QUERY_COLS], lp['rwkv_mu'][:, SHIFT_STATE_COLS:])
    r, gd = _split(zr, (RWKV_WIDTH, GATE_LORA))
    merge = z[..., 2 * RET_WIDTH + SHIFT_QUERY_COLS:]
    ret_q = _heads(ret_q, RET_HEAD_DIM)
    if latent:
        ret_q = _rope_2d(ret_q)
    return {'ret_q': ret_q, 'ret_g': ret_g, 'r': _heads(r, RWKV_HEAD_DIM),
            'rwkv_g': jax.nn.sigmoid(gd) @ lp['rwkv_g2'], 'merge': jax.nn.sigmoid(merge)}


def _branch_merge(ret_o, rwkv_o, st, qu, lp):
    dt = qu['ret_g'].dtype
    y_ret = (jax.nn.silu(qu['ret_g']) * (_head_norm(ret_o, RET_EPS) * lp['ret_gn_g']).astype(dt)) @ lp['w_br_ret']
    r_k = lp['rwkv_r_k'].reshape(RWKV_HEADS, RWKV_HEAD_DIM)
    bonus = jnp.sum(qu['r'][None] * st['k'] * r_k, axis=-1, keepdims=True).sum(0) * st['v']
    y_rw = (_head_norm(rwkv_o, RWKV_EPS) * lp['rwkv_ln_g'] + lp['rwkv_ln_b']).astype(dt)
    y_rw = ((y_rw + bonus.reshape(bonus.shape[:-2] + (-1,))) * qu['rwkv_g']) @ lp['w_br_rwkv']
    g_ret, g_rw = jnp.split(qu['merge'], 2, axis=-1)
    return (g_ret * y_ret + g_rw * y_rw) @ lp['w_out']


def _token_mixer(h_lat, h_ctx, lp, ctx_out):
    st_l = _side_state(h_lat, lp, True)
    qu_l = _side_query(h_lat, lp, True)
    st_c = _side_state(h_ctx, lp, False)
    qu_c = _side_query(h_ctx, lp, False) if ctx_out else None
    log_gamma = -jnp.exp(lp['ret_log_decay'].astype(jnp.float32))
    b = h_lat.shape[0]
    zero_ret = jnp.zeros((b, RET_HEADS, RET_HEAD_DIM, RET_HEAD_DIM), jnp.float32)
    zero_rw = jnp.zeros((b, RWKV_HEADS, RWKV_HEAD_DIM, RWKV_HEAD_DIM), jnp.float32)
    ret_l, rw_l, ret_c, rw_c = [], [], [], []
    for d, f in enumerate((_ident, _flip)):
        if ctx_out:
            o, r_state = _retention_chunkwise(f(qu_c['ret_q']), f(st_c['ret_k']), f(st_c['ret_v']), log_gamma[d], zero_ret)
            ret_c.append(f(o))
        else:
            r_state = _retention_final_state(f(st_c['ret_k']), f(st_c['ret_v']), log_gamma[d])
        o, _ = _retention_chunkwise(f(qu_l['ret_q']), f(st_l['ret_k']), f(st_l['ret_v']), log_gamma[d], r_state)
        ret_l.append(f(o))
        s_state, o = _rwkv7_scan(zero_rw, f(st_c['w'][d]), f(st_c['k'][d]), f(st_c['v']), f(st_c['kk']),
                                 f(st_c['a'][d]), f(qu_c['r']) if ctx_out else None)
        if ctx_out:
            rw_c.append(f(o))
        _, o = _rwkv7_scan(s_state, f(st_l['w'][d]), f(st_l['k'][d]), f(st_l['v']), f(st_l['kk']),
                           f(st_l['a'][d]), f(qu_l['r']))
        rw_l.append(f(o))
    y_lat = _branch_merge(ret_l[0] + ret_l[1], rw_l[0] + rw_l[1], st_l, qu_l, lp)
    y_ctx = _branch_merge(ret_c[0] + ret_c[1], rw_c[0] + rw_c[1], st_c, qu_c, lp) if ctx_out else None
    return y_lat, y_ctx


def _expert_choice_ffn(h, w_router, w_gate, w_up, w_down):
    b, n, d = h.shape
    cap = CAPACITY_FACTOR * n // N_EXPERTS
    aff = jax.nn.softmax((h @ w_router).astype(jnp.float32), axis=-1)
    gate, idx = lax.top_k(jnp.swapaxes(aff, 1, 2), cap)
    xg = jax.vmap(lambda hb, ib: hb[ib])(h, idx)
    hid = jax.nn.silu(jnp.einsum('becd,edf->becf', xg, w_gate)) * jnp.einsum('becd,edf->becf', xg, w_up)
    out = jnp.einsum('becf,efd->becd', hid, w_down) * gate[..., None].astype(h.dtype)
    return jax.vmap(lambda ob, ib: jnp.zeros((n, d), ob.dtype).at[ib.reshape(-1)].add(ob.reshape(-1, d)))(out, idx)


def setup_inputs(seed: int = 0) -> dict:
    key = jax.random.key(seed)
    ks = iter(jax.random.split(key, 32))
    f32 = jnp.float32

    def nrm(shape, scale):
        return scale * jax.random.normal(next(ks), shape, f32)

    L, D = DEPTH, D_MODEL
    ret_init = jnp.log(-jnp.log1p(-(2.0 ** (-5.0 - jnp.arange(RET_HEADS, dtype=f32)))))
    return {
        'x': nrm((BATCH, SEQ, D), 1.0),
        'c': nrm((BATCH, D), 1.0),
        'ctx': nrm((BATCH, CTX_LEN, D), 1.0),
        'c_ctx': nrm((D,), 1.0),
        'w_mod': nrm((L, D, N_MOD * D), 0.5 * D ** -0.5),
        'b_mod': nrm((L, N_MOD * D), 0.02),
        'norm_g': 1.0 + nrm((L, 4, D), 0.05),
        'w_in': nrm((L, D, IN_COLS), D ** -0.5),
        'ret_log_decay': ret_init[None, None, :] + nrm((L, 2, RET_HEADS), 0.1),
        'ret_gn_g': 1.0 + nrm((L, RET_WIDTH), 0.05),
        'rwkv_mu': jax.random.uniform(next(ks), (L, 2, SHIFT_COLS), f32, 0.0, 0.5),
        'rwkv_k_k': 0.85 + nrm((L, RWKV_WIDTH), 0.05),
        'rwkv_k_a': 1.0 + nrm((L, RWKV_WIDTH), 0.05),
        'rwkv_r_k': nrm((L, RWKV_WIDTH), 0.1),
        'rwkv_w0': jax.random.uniform(next(ks), (L, 2, RWKV_WIDTH), f32, -6.0, 0.0),
        'rwkv_w2': nrm((L, 2, DECAY_LORA, RWKV_WIDTH), 0.5 * DECAY_LORA ** -0.5),
        'rwkv_a0': nrm((L, 2, RWKV_WIDTH), 0.1),
        'rwkv_a2': nrm((L, 2, ICLR_LORA, RWKV_WIDTH), ICLR_LORA ** -0.5),
        'rwkv_g2': nrm((L, GATE_LORA, RWKV_WIDTH), GATE_LORA ** -0.5),
        'rwkv_ln_g': 1.0 + nrm((L, RWKV_WIDTH), 0.05),
        'rwkv_ln_b': nrm((L, RWKV_WIDTH), 0.02),
        'w_br_ret': nrm((L, RET_WIDTH, D), RET_WIDTH ** -0.5),
        'w_br_rwkv': nrm((L, RWKV_WIDTH, D), RWKV_WIDTH ** -0.5),
        'w_out': nrm((L, D, D), D ** -0.5),
        'w_router': nrm((L, D, N_EXPERTS), D ** -0.5),
        'w_gate': nrm((L, N_EXPERTS, D, EXPERT_FF), D ** -0.5),
        'w_up': nrm((L, N_EXPERTS, D, EXPERT_FF), D ** -0.5),
        'w_down': nrm((L, N_EXPERTS, EXPERT_FF, D), EXPERT_FF ** -0.5),
    }


def reference(x, c, ctx, c_ctx, w_mod, b_mod, norm_g, w_in, ret_log_decay, ret_gn_g, rwkv_mu,
              rwkv_k_k, rwkv_k_a, rwkv_r_k, rwkv_w0, rwkv_w2, rwkv_a0, rwkv_a2, rwkv_g2,
              rwkv_ln_g, rwkv_ln_b, w_br_ret, w_br_rwkv, w_out, w_router, w_gate, w_up, w_down):
    for l in range(DEPTH):
        last = l == DEPTH - 1
        lp = {'w_in': w_in[l], 'ret_log_decay': ret_log_decay[l], 'ret_gn_g': ret_gn_g[l],
              'rwkv_mu': rwkv_mu[l], 'rwkv_k_k': rwkv_k_k[l], 'rwkv_k_a': rwkv_k_a[l], 'rwkv_r_k': rwkv_r_k[l],
              'rwkv_w0': rwkv_w0[l], 'rwkv_w2': rwkv_w2[l], 'rwkv_a0': rwkv_a0[l], 'rwkv_a2': rwkv_a2[l],
              'rwkv_g2': rwkv_g2[l], 'rwkv_ln_g': rwkv_ln_g[l], 'rwkv_ln_b': rwkv_ln_b[l],
              'w_br_ret': w_br_ret[l], 'w_br_rwkv': w_br_rwkv[l], 'w_out': w_out[l]}
        ng = norm_g[l]
        sh1, sc1, g1, sh2, sc2, g2 = jnp.split((jax.nn.silu(c) @ w_mod[l] + b_mod[l])[:, None, :], N_MOD, axis=-1)
        csh1, csc1, cg1, csh2, csc2, cg2 = jnp.split(jax.nn.silu(c_ctx) @ w_mod[l] + b_mod[l], N_MOD, axis=-1)
        h_lat = _rmsnorm(x, ng[0]) * (1.0 + sc1) + sh1
        h_ctx = _rmsnorm(ctx, ng[0]) * (1.0 + csc1) + csh1
        y_lat, y_ctx = _token_mixer(h_lat, h_ctx, lp, not last)
        x = x + g1 * _rmsnorm(y_lat, ng[1])
        h2 = _rmsnorm(x, ng[2]) * (1.0 + sc2) + sh2
        x = x + g2 * _rmsnorm(_expert_choice_ffn(h2, w_router[l], w_gate[l], w_up[l], w_down[l]), ng[3])
        if not last:
            ctx = ctx + cg1 * _rmsnorm(y_ctx, ng[1])
            hc2 = _rmsnorm(ctx, ng[2]) * (1.0 + csc2) + csh2
            ctx = ctx + cg2 * _rmsnorm(_expert_choice_ffn(hc2, w_router[l], w_gate[l], w_up[l], w_down[l]), ng[3])
    return x
```

```python
import functools

import numpy as np
import jax
import jax.numpy as jnp
from jax import lax
from jax.experimental import pallas as pl
from jax.experimental.pallas import tpu as pltpu

f32 = jnp.float32
bf16 = jnp.bfloat16

D_MODEL = 1024
GRID_W = 64
RET_HEAD_DIM = 128
RET_WIDTH = 512
RET_HEADS = 4
RET_CHUNK = 128
RET_EPS = 1e-5
ROPE_BASE = 10000.0
RWKV_HEAD_DIM = 64
RWKV_WIDTH = 512
RWKV_PAIRS = 4
RWKV_CHUNK = 64
DECAY_LORA = 64
ICLR_LORA = 64
GATE_LORA = 128
RWKV_EPS = 64e-5
N_EXPERTS = 16
EXPERT_FF = 1024
CAPACITY_FACTOR = 2
N_MOD = 6
NORM_EPS = 1e-6

C_MERGE, C_RETK, C_RETV, C_RETQ, C_RETG = 0, 2048, 2560, 3072, 3584
C_RWK, C_RWV, C_RWR, C_LORA = 4096, 4608, 5120, 5632
IN_COLS = 5888
TN = 256
ROWS = 256

ROUTE_MAX_EXP = 126.0
ROUTE_EXP_STEPS = 8
ROUTE_VAL_STEPS = 26

VMEM_LIMIT = 56 * 1024 * 1024


def _dot(a, b):
    return jnp.dot(a, b, preferred_element_type=f32)


def _dot_nt(a, b):
    return lax.dot_general(a, b, (((1,), (1,)), ((), ())), preferred_element_type=f32)


def _dot_tn(a, b):
    return lax.dot_general(a, b, (((0,), (0,)), ((), ())), preferred_element_type=f32)


def _bdot(a, b):
    return _dot(a.astype(bf16), b.astype(bf16))


def _split_dot(x, w):
    hi = x.astype(bf16)
    lo = (x - hi.astype(f32)).astype(bf16)
    return _dot(hi, w) + _dot(lo, w)


def _params(sem, limit=VMEM_LIMIT):
    return pltpu.CompilerParams(dimension_semantics=sem, vmem_limit_bytes=limit)


def _mod_kernel(c_ref, w_ref, b_ref, o_ref):
    c = c_ref[...]
    s = c * jax.nn.sigmoid(c)
    o_ref[...] = _bdot(s, w_ref[...]) + b_ref[...]


def _modulation(cc, w_mod, b_mod):
    m, d = cc.shape
    n = w_mod.shape[1]
    tn = 512
    return pl.pallas_call(
        _mod_kernel,
        grid=(n // tn,),
        in_specs=[pl.BlockSpec((m, d), lambda j: (0, 0)),
                  pl.BlockSpec((d, tn), lambda j: (0, j)),
                  pl.BlockSpec((1, tn), lambda j: (0, j))],
        out_specs=pl.BlockSpec((m, tn), lambda j: (0, j)),
        out_shape=jax.ShapeDtypeStruct((m, n), f32),
        compiler_params=_params(("arbitrary",)),
        name="mod",
    )(cc, w_mod, b_mod.reshape(1, n))


def _inproj_kernel(x_ref, mod_ref, g_ref, w_ref, mu_ref, cos_ref, sin_ref, o_ref, h_ref, z_ref, *, ctx):
    n = pl.program_id(1)
    L = x_ref.shape[1]
    nchunk = L // ROWS
    PAD = 8

    @pl.when(n == 0)
    def _():
        z_ref[0:PAD, :] = jnp.zeros((PAD, TN), f32)
        z_ref[PAD + L:PAD + L + PAD, :] = jnp.zeros((PAD, TN), f32)
        for i in range(nchunk):
            r0 = i * ROWS
            xb = x_ref[0, r0:r0 + ROWS, :]
            y = xb * lax.rsqrt(jnp.mean(xb * xb, axis=-1, keepdims=True) + NORM_EPS) * g_ref[...]
            o = 2 if r0 < ctx else 0
            sh = mod_ref[0, o:o + 1, :]
            sc = mod_ref[0, o + 1:o + 2, :]
            h_ref[r0:r0 + ROWS, :] = (y * (1.0 + sc) + sh).astype(bf16)

    for i in range(nchunk):
        r0 = i * ROWS
        z_ref[PAD + r0:PAD + r0 + ROWS, :] = _dot(h_ref[r0:r0 + ROWS, :], w_ref[...])

    def emit(fn):
        for i in range(nchunk):
            r0 = i * ROWS
            o_ref[0, r0:r0 + ROWS, :] = fn(z_ref[PAD + r0:PAD + r0 + ROWS, :], r0).astype(bf16)

    def rope(z, r0, scale):
        if scale != 1.0:
            z = z * scale
        if r0 < ctx:
            return z
        t0 = r0 - ctx
        cs = cos_ref[t0:t0 + ROWS, :]
        sn = sin_ref[t0:t0 + ROWS, :]
        parts = []
        for hh in range(TN // RET_HEAD_DIM):
            zh = z[:, hh * RET_HEAD_DIM:(hh + 1) * RET_HEAD_DIM]
            parts.append(zh * cs + pltpu.roll(zh, RET_HEAD_DIM // 2, 1) * sn)
        return jnp.concatenate(parts, axis=1)

    def shift(z, r0):
        prev = z_ref[PAD + r0 - 1:PAD + r0 - 1 + ROWS, :]
        nxt = z_ref[PAD + r0 + 1:PAD + r0 + 1 + ROWS, :]
        rid = r0 + lax.broadcasted_iota(jnp.int32, (ROWS, 1), 0)
        prev = jnp.where(rid == ctx, 0.0, prev)
        nxt = jnp.where(rid == ctx - 1, 0.0, nxt)
        return z + mu_ref[0:1, :] * (prev - z) + mu_ref[1:2, :] * (nxt - z)

    def lora_act(z, r0):
        zs = shift(z, r0)
        lane = lax.broadcasted_iota(jnp.int32, (1, TN), 1)
        return jnp.where(lane < DECAY_LORA, jnp.tanh(zs),
                         jnp.where(lane < DECAY_LORA + ICLR_LORA, zs, jax.nn.sigmoid(zs)))

    t = lambda c: c // TN

    @pl.when(n < t(C_RETK))
    def _():
        emit(lambda z, r0: jax.nn.sigmoid(z))

    @pl.when((n >= t(C_RETK)) & (n < t(C_RETV)))
    def _():
        emit(lambda z, r0: rope(z, r0, RET_HEAD_DIM ** -0.5))

    @pl.when((n >= t(C_RETV)) & (n < t(C_RETQ)))
    def _():
        emit(lambda z, r0: z)

    @pl.when((n >= t(C_RETQ)) & (n < t(C_RETG)))
    def _():
        emit(lambda z, r0: rope(z, r0, 1.0))

    @pl.when((n >= t(C_RETG)) & (n < t(C_RWK)))
    def _():
        emit(lambda z, r0: z * jax.nn.sigmoid(z))

    @pl.when((n >= t(C_RWK)) & (n < t(C_LORA)))
    def _():
        emit(shift)

    @pl.when(n >= t(C_LORA))
    def _():
        emit(lora_act)


def _inproj(xall, modrows, g0, w_perm, mu_full, cosf, sinf, ctx):
    B, L, D = xall.shape
    T = L - ctx
    nt = IN_COLS // TN
    return pl.pallas_call(
        functools.partial(_inproj_kernel, ctx=ctx),
        grid=(B, nt),
        in_specs=[pl.BlockSpec((1, L, D), lambda b, n: (b, 0, 0)),
                  pl.BlockSpec((1, 4, D), lambda b, n: (b, 0, 0)),
                  pl.BlockSpec((1, D), lambda b, n: (0, 0)),
                  pl.BlockSpec((D, TN), lambda b, n: (0, n)),
                  pl.BlockSpec((2, TN), lambda b, n: (0, n)),
                  pl.BlockSpec((T, RET_HEAD_DIM), lambda b, n: (0, 0)),
                  pl.BlockSpec((T, RET_HEAD_DIM), lambda b, n: (0, 0))],
        out_specs=pl.BlockSpec((1, L, TN), lambda b, n: (b, 0, n)),
        out_shape=jax.ShapeDtypeStruct((B, L, IN_COLS), bf16),
        scratch_shapes=[pltpu.VMEM((L, D), bf16), pltpu.VMEM((L + 16, TN), f32)],
        compiler_params=_params(("arbitrary", "arbitrary")),
        name="inproj",
    )(xall, modrows, g0, w_perm, mu_full, cosf, sinf)


def _ret_kernel(lg_ref, q_ref, k_ref, v_ref, o_ref, *, ctx):
    h = pl.program_id(1)
    L = q_ref.shape[1]
    Cc = RET_CHUNK
    nc = ctx // Cc
    nl = (L - ctx) // Cc
    ii = lax.broadcasted_iota(jnp.int32, (Cc, Cc), 0).astype(f32)
    jj = lax.broadcasted_iota(jnp.int32, (Cc, Cc), 1).astype(f32)

    for d in (0, 1):
        lg = lg_ref[d, h]
        if d == 0:
            diff = ii - jj
            cross = jnp.exp(lg * (ii + 1.0))
            tailw = jnp.exp(lg * (Cc - 1.0 - ii))
        else:
            diff = jj - ii
            cross = jnp.exp(lg * (Cc - ii))
            tailw = jnp.exp(lg * ii)
        intra = jnp.where(diff >= 0, jnp.exp(lg * jnp.maximum(diff, 0.0)), 0.0)
        cdv = jnp.exp(jnp.zeros((Cc, Cc), f32) + lg * Cc)

        def update(r, kc, vc):
            ks = (kc.astype(f32) * tailw).astype(bf16)
            return r * cdv + _dot_tn(ks, vc)

        def ctx_step(s, r):
            row0 = s * Cc if d == 0 else (nc - 1 - s) * Cc
            row0 = pl.multiple_of(row0, Cc)
            return update(r, k_ref[0, pl.ds(row0, Cc), :], v_ref[0, pl.ds(row0, Cc), :])

        def lat_step(s, r):
            t0 = s * Cc if d == 0 else (nl - 1 - s) * Cc
            t0 = pl.multiple_of(t0, Cc)
            row0 = pl.multiple_of(t0 + ctx, Cc)
            qc = q_ref[0, pl.ds(row0, Cc), :]
            kc = k_ref[0, pl.ds(row0, Cc), :]
            vc = v_ref[0, pl.ds(row0, Cc), :]
            sc = _dot_nt(qc, kc) * intra
            o = _dot(sc.astype(bf16), vc) + _dot(qc, r.astype(bf16)) * cross
            if d == 0:
                o_ref[0, pl.ds(t0, Cc), :] = o
            else:
                o_ref[0, pl.ds(t0, Cc), :] += o
            return update(r, kc, vc)

        r = lax.fori_loop(0, nc, ctx_step, jnp.zeros((Cc, Cc), f32))
        lax.fori_loop(0, nl, lat_step, r)


def _retention(lg, z, ctx):
    B, L, _ = z.shape
    T = L - ctx
    hd = RET_HEAD_DIM
    blk = lambda c0: pl.BlockSpec((1, L, hd), lambda b, h: (b, 0, c0 // hd + h))
    return pl.pallas_call(
        functools.partial(_ret_kernel, ctx=ctx),
        grid=(B, RET_HEADS),
        in_specs=[pl.BlockSpec(memory_space=pltpu.SMEM), blk(C_RETQ), blk(C_RETK), blk(C_RETV)],
        out_specs=pl.BlockSpec((1, T, hd), lambda b, h: (b, 0, h)),
        out_shape=jax.ShapeDtypeStruct((B, T, RET_WIDTH), f32),
        compiler_params=_params(("arbitrary", "arbitrary")),
        name="ret",
    )(lg, z, z, z)


def _rwkv_kernel(k_ref, v_ref, r_ref, lo_ref, w0_ref, a0_ref, w2_ref, a2_ref, kk_ref, ka_ref, rk_ref,
                 o_ref, bo_ref, s_ref, *, nc):
    d = pl.program_id(1)
    s = pl.program_id(2)
    C = RWKV_CHUNK
    G = 2 * RWKV_HEAD_DIM
    rev = d == 1

    @pl.when(s == 0)
    def _():
        s_ref[...] = jnp.zeros(s_ref.shape, f32)

    ii = lax.broadcasted_iota(jnp.int32, (C, C), 0)
    jj = lax.broadcasted_iota(jnp.int32, (C, C), 1)
    dlt = jnp.where(rev, ii - jj, jj - ii)
    strict = dlt < 0
    incl = dlt <= 0
    eye = (ii == jj).astype(f32)
    gi = lax.broadcasted_iota(jnp.int32, (G, G), 0)
    gj = lax.broadcasted_iota(jnp.int32, (G, G), 1)
    blockdiag = (gi >= RWKV_HEAD_DIM) == (gj >= RWKV_HEAD_DIM)
    ones_bd = blockdiag.astype(bf16)
    lane = lax.broadcasted_iota(jnp.int32, (1, G), 1)
    hmask = [(lane < RWKV_HEAD_DIM).astype(f32), (lane >= RWKV_HEAD_DIM).astype(f32)]

    def segsum(x):
        return jnp.concatenate(
            [_split_dot(x[:, p * G:(p + 1) * G], ones_bd) for p in range(RWKV_PAIRS)], axis=1)

    kx = k_ref[0].astype(f32)
    vx = v_ref[0].astype(f32)
    rx = r_ref[0].astype(f32)
    lo = lo_ref[0, :, 0:DECAY_LORA + ICLR_LORA]

    u = w0_ref[0] + _dot(lo, w2_ref[0])
    softplus = jnp.maximum(-u, 0.0) + jnp.log1p(jnp.exp(-jnp.abs(u)))
    lw = -jnp.exp(-softplus - 0.5)
    a = jax.nn.sigmoid(a0_ref[0] + _dot(lo, a2_ref[0]))
    kkr = kx * kk_ref[...]
    kk = kkr * lax.rsqrt(segsum(kkr * kkr) + 1e-12)
    kd = kx * (1.0 + (a - 1.0) * ka_ref[...])
    al = -kk
    be = kk * a
    bonus = segsum(rx * kd * rk_ref[...]) * vx

    tri = incl.astype(bf16)
    cum = _split_dot_left(tri, lw)
    tot = jnp.where(rev, cum[0:1, :], cum[C - 1:C, :])
    alb = al * jnp.exp(cum - lw)
    rb = rx * jnp.exp(cum)
    gneg = jnp.exp(-cum)
    beb = be * gneg
    kb = kd * gneg
    gh = jnp.exp(tot - cum)
    beh = be * gh
    kh = kd * gh
    etot = jnp.exp(tot)

    base = (ii >> 1) == (jj >> 1)
    offs = []
    for lv in range(1, 6):
        offs.append(((ii >> (lv + 1)) == (jj >> (lv + 1))) & ((ii >> lv) != (jj >> lv)))

    def unit_inverse(Lm):
        T = eye + jnp.where(base, Lm, 0.0)
        for off in offs:
            X = _bdot(jnp.where(off, Lm, 0.0), T)
            T = T + _bdot(T, X)
        return T

    outs = []
    for p in range(RWKV_PAIRS):
        sl = slice(p * G, (p + 1) * G)
        S = s_ref[p]
        Sb = S.astype(bf16)
        X = alb[:, sl]
        Rb = rb[:, sl]
        Yb = beb[:, sl].astype(bf16)
        Yk = kb[:, sl].astype(bf16)
        V = vx[:, sl]
        Bm = _dot_nt(X.astype(bf16), Sb)
        Om = _dot_nt(Rb.astype(bf16), Sb)
        Ts, Pab, Pak = [], [], []
        for hh in range(2):
            m = hmask[hh]
            Xh = (X * m).astype(bf16)
            Rh = (Rb * m).astype(bf16)
            Aab = jnp.where(strict, _dot_nt(Xh, Yb), 0.0)
            Aak = jnp.where(strict, _dot_nt(Xh, Yk), 0.0)
            Ts.append(unit_inverse(Aab))
            Pab.append(jnp.where(incl, _dot_nt(Rh, Yb), 0.0))
            Pak.append(jnp.where(incl, _dot_nt(Rh, Yk), 0.0))
            Vh = V * m
            Bm = Bm + _bdot(Aak, Vh)
            Om = Om + _bdot(Pak[hh], Vh)
        U = _bdot(Ts[0], Bm * hmask[0]) + _bdot(Ts[1], Bm * hmask[1])
        Om = Om + _bdot(Pab[0], U * hmask[0]) + _bdot(Pab[1], U * hmask[1])
        outs.append(Om)
        upd = _dot_tn(U.astype(bf16), beh[:, sl].astype(bf16)) + _dot_tn(V.astype(bf16), kh[:, sl].astype(bf16))
        s_ref[p] = jnp.where(blockdiag, S * etot[:, sl] + upd, 0.0)

    @pl.when(s >= nc)
    def _():
        o_ref[0, 0] = jnp.concatenate(outs, axis=1)
        bo_ref[0, 0] = bonus


def _split_dot_left(w, x):
    hi = x.astype(bf16)
    lo = (x - hi.astype(f32)).astype(bf16)
    return _dot(w, hi) + _dot(w, lo)


def _rwkv(z, w0, a0, w2p, a2p, k_k, k_a, r_k, ctx):
    B, L, _ = z.shape
    T = L - ctx
    C = RWKV_CHUNK
    nc, nl = ctx // C, T // C
    W = RWKV_WIDTH

    def chunk(d, s):
        fwd = s
        bwd = jnp.where(s < nc, nc - 1 - s, nc + nl - 1 - (s - nc))
        return jnp.where(d == 0, fwd, bwd)

    def ochunk(d, s):
        sl = jnp.maximum(s - nc, 0)
        return jnp.where(d == 0, sl, nl - 1 - sl)

    zblk = lambda c0, w: pl.BlockSpec((1, C, w), lambda b, d, s: (b, chunk(d, s), c0 // w))
    dpar = lambda r: pl.BlockSpec((1, r, W), lambda b, d, s: (d, 0, 0))
    par = pl.BlockSpec((1, W), lambda b, d, s: (0, 0))
    oblk = pl.BlockSpec((1, 1, C, W), lambda b, d, s: (d, b, ochunk(d, s), 0))
    G = 2 * RWKV_HEAD_DIM
    return pl.pallas_call(
        functools.partial(_rwkv_kernel, nc=nc),
        grid=(B, 2, nc + nl),
        in_specs=[zblk(C_RWK, W), zblk(C_RWV, W), zblk(C_RWR, W), zblk(C_LORA, 256),
                  dpar(1), dpar(1), dpar(G), dpar(G), par, par, par],
        out_specs=[oblk, oblk],
        out_shape=[jax.ShapeDtypeStruct((2, B, T, W), f32), jax.ShapeDtypeStruct((2, B, T, W), f32)],
        scratch_shapes=[pltpu.VMEM((RWKV_PAIRS, G, G), f32)],
        compiler_params=_params(("arbitrary", "arbitrary", "arbitrary")),
        name="rwkv",
    )(z, z, z, z, w0, a0, w2p, a2p, k_k, k_a, r_k)


def _merge_kernel(x_ref, ret_ref, rw0_ref, rw1_ref, b0_ref, b1_ref, gate_ref, retg_ref, gd_ref, mod_ref,
                  ng_ref, gn_ref, lng_ref, lnb_ref, g2_ref, wbr_ref, wbw_ref, wout_ref, wr_ref,
                  x1_ref, h2_ref, lt_ref):
    hd = RET_HEAD_DIM
    ret = ret_ref[0]
    parts = []
    for hh in range(RET_HEADS):
        xh = ret[:, hh * hd:(hh + 1) * hd]
        mu = jnp.mean(xh, axis=-1, keepdims=True)
        dv = xh - mu
        var = jnp.mean(dv * dv, axis=-1, keepdims=True)
        parts.append(dv * lax.rsqrt(var + RET_EPS))
    yr = retg_ref[0].astype(f32) * (jnp.concatenate(parts, axis=1) * gn_ref[...])
    y_ret = _bdot(yr, wbr_ref[...])
    W = RWKV_WIDTH
    gi = lax.broadcasted_iota(jnp.int32, (W, W), 0)
    gj = lax.broadcasted_iota(jnp.int32, (W, W), 1)
    ones_bd = ((gi >> 6) == (gj >> 6)).astype(bf16)
    o = rw0_ref[0, 0] + rw1_ref[0, 0]
    mu = _split_dot(o, ones_bd) * (1.0 / RWKV_HEAD_DIM)
    dv = o - mu
    var = _split_dot(dv * dv, ones_bd) * (1.0 / RWKV_HEAD_DIM)
    yw = dv * lax.rsqrt(var + RWKV_EPS) * lng_ref[...] + lnb_ref[...]
    gate = _dot(gd_ref[0], g2_ref[...])
    yw = (yw + b0_ref[0, 0] + b1_ref[0, 0]) * gate
    y_rw = _bdot(yw, wbw_ref[...])
    D = y_ret.shape[1]
    g = gate_ref[0].astype(f32)
    m = g[:, :D] * y_ret + g[:, D:] * y_rw
    y = _bdot(m, wout_ref[...])

    def rms(v, gg):
        return v * lax.rsqrt(jnp.mean(v * v, axis=-1, keepdims=True) + NORM_EPS) * gg

    x1 = x_ref[0] + mod_ref[0, 0:1, :] * rms(y, ng_ref[0:1, :])
    x1_ref[0] = x1
    h2 = rms(x1, ng_ref[1:2, :]) * (1.0 + mod_ref[0, 2:3, :]) + mod_ref[0, 1:2, :]
    h2_ref[0] = h2.astype(bf16)
    lt_ref[0] = lax.dot_general(wr_ref[...], h2, (((1,), (1,)), ((), ())),
                                preferred_element_type=f32, precision=lax.Precision.HIGHEST)


def _merge(x, ret_o, rw_o, bonus, z, mod2, ng12, gn, lng, lnb, g2, wbr, wbw, wout, wrt, ctx):
    B, T, D = x.shape
    tm = 256
    co = ctx // tm
    W = RWKV_WIDTH
    row = lambda w: pl.BlockSpec((1, tm, w), lambda b, i: (b, i, 0))
    dblk = lambda dd: pl.BlockSpec((1, 1, tm, W), lambda b, i: (dd, b, i, 0))
    zblk = lambda c0, w: pl.BlockSpec((1, tm, w), lambda b, i: (b, co + i, c0 // w))
    full = lambda a: pl.BlockSpec(a.shape, lambda b, i: (0,) * a.ndim)
    return pl.pallas_call(
        _merge_kernel,
        grid=(B, T // tm),
        in_specs=[row(D), row(W), dblk(0), dblk(1), dblk(0), dblk(1),
                  zblk(C_MERGE, 2 * D), zblk(C_RETG, W), zblk(C_LORA + 128, 128),
                  pl.BlockSpec((1, 3, D), lambda b, i: (b, 0, 0)),
                  full(ng12), full(gn), full(lng), full(lnb), full(g2), full(wbr), full(wbw), full(wout),
                  full(wrt)],
        out_specs=[row(D), row(D), pl.BlockSpec((1, N_EXPERTS, tm), lambda b, i: (b, 0, i))],
        out_shape=[jax.ShapeDtypeStruct((B, T, D), f32), jax.ShapeDtypeStruct((B, T, D), bf16),
                   jax.ShapeDtypeStruct((B, N_EXPERTS, T), f32)],
        compiler_params=_params(("arbitrary", "arbitrary")),
        name="merge",
    )(x, ret_o, rw_o, rw_o, bonus, bonus, z, z, z, mod2, ng12, gn, lng, lnb, g2, wbr, wbw, wout, wrt)


def _route_kernel(lt_ref, tri_ref, slot_ref, rt_ref, *, cap):
    lg = lt_ref[0]
    E, T = lg.shape
    mx = jnp.max(lg, axis=0, keepdims=True)
    ex = jnp.exp(lg - mx)
    aff = ex / jnp.sum(ex, axis=0, keepdims=True)

    def count_ge(cand):
        return jnp.sum((aff >= cand).astype(f32), axis=1, keepdims=True)

    def exp_step(_, kk):
        k_lo, k_hi = kk
        km = jnp.floor((k_lo + k_hi) * 0.5)
        ok = count_ge(jnp.exp2(-km)) >= cap
        return jnp.where(ok, k_lo, km), jnp.where(ok, km, k_hi)

    k_lo, k_hi = lax.fori_loop(0, ROUTE_EXP_STEPS, exp_step,
                               (jnp.full((E, 1), -1.0, f32), jnp.full((E, 1), ROUTE_MAX_EXP, f32)))
    lo0 = jnp.where(k_hi >= ROUTE_MAX_EXP, 0.0, jnp.exp2(-k_hi))
    hi0 = jnp.exp2(-k_lo)

    def val_step(_, lh):
        lo, hi = lh
        mid = (lo + hi) * 0.5
        ok = count_ge(mid) >= cap
        return jnp.where(ok, mid, lo), jnp.where(ok, hi, mid)

    lo, hi = lax.fori_loop(0, ROUTE_VAL_STEPS, val_step, (lo0, hi0))
    gt = aff >= hi
    eq = (aff >= lo) & (aff < hi)
    need = cap - jnp.sum(gt.astype(f32), axis=1, keepdims=True)
    tri = tri_ref[...]
    eq_before = _dot(eq.astype(bf16), tri)
    sel = gt | (eq & (eq_before < need))
    slot = _dot(sel.astype(bf16), tri)
    slot_f = jnp.where(sel, slot, -1.0)
    slot_ref[0] = slot_f.astype(jnp.int32)
    gate = jnp.where(sel, aff, 0.0)
    packed = jnp.concatenate([slot_f, gate, jnp.zeros((128 - 2 * E, T), f32)], axis=0)
    rt_ref[0] = packed.T


def _route(lt, tri, cap):
    B, E, T = lt.shape
    return pl.pallas_call(
        functools.partial(_route_kernel, cap=cap),
        grid=(B,),
        in_specs=[pl.BlockSpec((1, E, T), lambda b: (b, 0, 0)),
                  pl.BlockSpec((T, T), lambda b: (0, 0))],
        out_specs=[pl.BlockSpec((1, E, T), lambda b: (b, 0, 0)),
                   pl.BlockSpec((1, T, 128), lambda b: (b, 0, 0))],
        out_shape=[jax.ShapeDtypeStruct((B, E, T), jnp.int32), jax.ShapeDtypeStruct((B, T, 128), f32)],
        compiler_params=_params(("arbitrary",)),
        name="route",
    )(lt, tri)


def _ffn_kernel(slot_ref, h_ref, wg_ref, wu_ref, wd_ref, o_ref, *, cap):
    T = h_ref.shape[1]
    ci = lax.broadcasted_iota(jnp.int32, (cap, T), 0)
    onehot = (slot_ref[0, 0] == ci).astype(bf16)
    xg = _dot(onehot, h_ref[0]).astype(bf16)
    hg = _dot(xg, wg_ref[0])
    hu = _dot(xg, wu_ref[0])
    hid = (hg * jax.nn.sigmoid(hg) * hu).astype(bf16)
    o_ref[0, 0] = _dot(hid, wd_ref[0]).astype(bf16)


def _ffn(slot4, h2, wg, wu, wd, cap):
    B, T, D = h2.shape
    E, _, F = wg.shape
    return pl.pallas_call(
        functools.partial(_ffn_kernel, cap=cap),
        grid=(E, B),
        in_specs=[pl.BlockSpec((1, 1, 1, T), lambda e, b: (b, e, 0, 0)),
                  pl.BlockSpec((1, T, D), lambda e, b: (b, 0, 0)),
                  pl.BlockSpec((1, D, F), lambda e, b: (e, 0, 0)),
                  pl.BlockSpec((1, D, F), lambda e, b: (e, 0, 0)),
                  pl.BlockSpec((1, F, D), lambda e, b: (e, 0, 0))],
        out_specs=pl.BlockSpec((1, 1, cap, D), lambda e, b: (b, e, 0, 0)),
        out_shape=jax.ShapeDtypeStruct((B, E, cap, D), bf16),
        compiler_params=_params(("arbitrary", "arbitrary")),
        name="ffn",
    )(slot4, h2, wg, wu, wd)


def _combine_kernel(rt_ref, eo_ref, x1_ref, mod_ref, ng_ref, o_ref, *, cap):
    rt = rt_ref[0]
    tm = rt.shape[0]
    E = eo_ref.shape[1]
    ci = lax.broadcasted_iota(jnp.int32, (tm, cap), 1).astype(f32)
    y = jnp.zeros((tm, eo_ref.shape[3]), f32)
    for e in range(E):
        p = jnp.where(rt[:, e:e + 1] == ci, rt[:, E + e:E + e + 1], 0.0).astype(bf16)
        y = y + _dot(p, eo_ref[0, e])
    yn = y * lax.rsqrt(jnp.mean(y * y, axis=-1, keepdims=True) + NORM_EPS) * ng_ref[...]
    o_ref[0] = x1_ref[0] + mod_ref[0] * yn


def _combine(rt, eo, x1, g2mod, ng3, cap):
    B, T, D = x1.shape
    E = eo.shape[1]
    tm = 512 if T % 512 == 0 else T
    return pl.pallas_call(
        functools.partial(_combine_kernel, cap=cap),
        grid=(B, T // tm),
        in_specs=[pl.BlockSpec((1, tm, 128), lambda b, i: (b, i, 0)),
                  pl.BlockSpec((1, E, cap, D), lambda b, i: (b, 0, 0, 0)),
                  pl.BlockSpec((1, tm, D), lambda b, i: (b, i, 0)),
                  pl.BlockSpec((1, 1, D), lambda b, i: (b, 0, 0)),
                  pl.BlockSpec((1, D), lambda b, i: (0, 0))],
        out_specs=pl.BlockSpec((1, tm, D), lambda b, i: (b, i, 0)),
        out_shape=jax.ShapeDtypeStruct((B, T, D), f32),
        compiler_params=_params(("arbitrary", "arbitrary")),
        name="combine",
    )(rt, eo, x1, g2mod, ng3)


def _column_perm():
    sk, sv, rk, rv, wd, ad = 0, 512, 1024, 1536, 2048, 2112
    q0 = 2176
    rq, rg, rr, gd, mg = q0, q0 + 512, q0 + 1024, q0 + 1536, q0 + 1664
    rng = lambda a, n: np.arange(a, a + n)
    return np.concatenate([rng(mg, 2048), rng(sk, 512), rng(sv, 512), rng(rq, 512), rng(rg, 512),
                           rng(rk, 512), rng(rv, 512), rng(rr, 512), rng(wd, 64), rng(ad, 64), rng(gd, 128)])


def _rope_tables(T):
    t = jnp.arange(T)
    nfreq = RET_HEAD_DIM // 4
    inv = ROPE_BASE ** (-jnp.arange(nfreq, dtype=f32) / nfreq)
    ang = jnp.concatenate([(t // GRID_W).astype(f32)[:, None] * inv,
                           (t % GRID_W).astype(f32)[:, None] * inv], axis=-1)
    cos, sin = jnp.cos(ang), jnp.sin(ang)
    return jnp.concatenate([cos, cos], axis=1), jnp.concatenate([-sin, sin], axis=1)


def kernel(x, c, ctx, c_ctx, w_mod, b_mod, norm_g, w_in, ret_log_decay, ret_gn_g, rwkv_mu, rwkv_k_k, rwkv_k_a,
           rwkv_r_k, rwkv_w0, rwkv_w2, rwkv_a0, rwkv_a2, rwkv_g2, rwkv_ln_g, rwkv_ln_b, w_br_ret, w_br_rwkv,
           w_out, w_router, w_gate, w_up, w_down):
    B, T, D = x.shape
    CT = ctx.shape[1]
    assert w_mod.shape[0] == 1 and D == D_MODEL
    assert CT % ROWS == 0 and T % ROWS == 0 and T % GRID_W == 0
    cap = CAPACITY_FACTOR * T // N_EXPERTS
    assert cap % 8 == 0

    mrows = -(-(B + 1) // 8) * 8
    cc = jnp.zeros((mrows, D), f32).at[:B].set(c).at[B].set(c_ctx)
    mod = _modulation(cc, w_mod[0], b_mod[0])
    lat = mod[:B].reshape(B, N_MOD, D)
    cm = jnp.broadcast_to(mod[B].reshape(1, N_MOD, D), (B, N_MOD, D))
    modrows = jnp.concatenate([lat[:, 0:2], cm[:, 0:2]], axis=1)

    perm = _column_perm()
    w_perm = w_in[0][:, perm].astype(bf16)
    mu = rwkv_mu[0]
    ss = 2 * RWKV_WIDTH + DECAY_LORA + ICLR_LORA
    mu_full = jnp.zeros((2, IN_COLS), f32)
    mu_full = mu_full.at[:, C_RWK:C_RWK + 1024].set(mu[:, 0:1024])
    mu_full = mu_full.at[:, C_RWR:C_RWR + 512].set(mu[:, ss:ss + 512])
    mu_full = mu_full.at[:, C_LORA:C_LORA + 128].set(mu[:, 1024:ss])
    mu_full = mu_full.at[:, C_LORA + 128:].set(mu[:, ss + 512:])
    cosf, sinf = _rope_tables(T)
    xall = jnp.concatenate([ctx, x], axis=1)
    z = _inproj(xall, modrows, norm_g[0, 0:1], w_perm, mu_full, cosf, sinf, CT)

    lg = -jnp.exp(ret_log_decay[0].astype(f32))
    ret_o = _retention(lg, z, CT)

    G = 2 * RWKV_HEAD_DIM
    w2p = jnp.zeros((2, G, RWKV_WIDTH), f32).at[:, :DECAY_LORA].set(rwkv_w2[0]).astype(bf16)
    a2p = jnp.zeros((2, G, RWKV_WIDTH), f32).at[:, DECAY_LORA:].set(rwkv_a2[0]).astype(bf16)
    rw_o, bonus = _rwkv(z, rwkv_w0[0][:, None, :], rwkv_a0[0][:, None, :], w2p, a2p,
                        rwkv_k_k[0][None], rwkv_k_a[0][None], rwkv_r_k[0][None], CT)

    mod2 = jnp.stack([lat[:, 2], lat[:, 3], lat[:, 4]], axis=1)
    x1, h2, lt = _merge(x, ret_o, rw_o, bonus, z, mod2, norm_g[0, 1:3], ret_gn_g[0][None], rwkv_ln_g[0][None],
                        rwkv_ln_b[0][None], rwkv_g2[0].astype(bf16), w_br_ret[0].astype(bf16),
                        w_br_rwkv[0].astype(bf16), w_out[0].astype(bf16), w_router[0].T, CT)

    ti = jnp.arange(T)
    tri = (ti[:, None] < ti[None, :]).astype(bf16)
    slot, rt = _route(lt, tri, cap)
    eo = _ffn(slot.reshape(B, N_EXPERTS, 1, T), h2, w_gate[0].astype(bf16), w_up[0].astype(bf16),
              w_down[0].astype(bf16), cap)
    return _combine(rt, eo, x1, lat[:, 5:6], norm_g[0, 3:4], cap)
```

```python
import functools

import numpy as np
import jax
import jax.numpy as jnp
from jax import lax
from jax.experimental import pallas as pl
from jax.experimental.pallas import tpu as pltpu

f32 = jnp.float32
bf16 = jnp.bfloat16

D_MODEL = 1024
GRID_W = 64
RET_HEAD_DIM = 128
RET_WIDTH = 512
RET_HEADS = 4
RET_CHUNK = 128
RET_EPS = 1e-5
ROPE_BASE = 10000.0
RWKV_HEAD_DIM = 64
RWKV_WIDTH = 512
RWKV_PAIRS = 4
RWKV_CHUNK = 64
RWKV_SAMPLES_PER_STEP = 4
DECAY_LORA = 64
ICLR_LORA = 64
GATE_LORA = 128
RWKV_EPS = 64e-5
N_EXPERTS = 16
EXPERT_FF = 1024
CAPACITY_FACTOR = 2
N_MOD = 6
NORM_EPS = 1e-6

C_MERGE, C_RETK, C_RETV, C_RETQ, C_RETG = 0, 2048, 2560, 3072, 3584
C_RWK, C_RWV, C_RWR, C_LORA = 4096, 4608, 5120, 5632
IN_COLS = 5888
TN = 256
ROWS = 256

ROUTE_MAX_EXP = 126.0
ROUTE_EXP_STEPS = 8
ROUTE_VAL_STEPS = 26

VMEM_LIMIT = 56 * 1024 * 1024


def _dot(a, b):
    return jnp.dot(a, b, preferred_element_type=f32)


def _dot_nt(a, b):
    return lax.dot_general(a, b, (((1,), (1,)), ((), ())), preferred_element_type=f32)


def _dot_tn(a, b):
    return lax.dot_general(a, b, (((0,), (0,)), ((), ())), preferred_element_type=f32)


def _bdot(a, b):
    return _dot(a.astype(bf16), b.astype(bf16))


def _split_dot(x, w):
    hi = x.astype(bf16)
    lo = (x - hi.astype(f32)).astype(bf16)
    return _dot(hi, w) + _dot(lo, w)


def _params(sem, limit=VMEM_LIMIT):
    return pltpu.CompilerParams(dimension_semantics=sem, vmem_limit_bytes=limit)


def _mod_kernel(c_ref, w_ref, b_ref, o_ref):
    c = c_ref[...]
    s = c * jax.nn.sigmoid(c)
    o_ref[...] = _bdot(s, w_ref[...]) + b_ref[...]


def _modulation(cc, w_mod, b_mod):
    m, d = cc.shape
    n = w_mod.shape[1]
    tn = 512
    return pl.pallas_call(
        _mod_kernel,
        grid=(n // tn,),
        in_specs=[pl.BlockSpec((m, d), lambda j: (0, 0)),
                  pl.BlockSpec((d, tn), lambda j: (0, j)),
                  pl.BlockSpec((1, tn), lambda j: (0, j))],
        out_specs=pl.BlockSpec((m, tn), lambda j: (0, j)),
        out_shape=jax.ShapeDtypeStruct((m, n), f32),
        compiler_params=_params(("arbitrary",)),
        name="mod",
    )(cc, w_mod, b_mod.reshape(1, n))


def _inproj_kernel(x_ref, mod_ref, g_ref, w_ref, mu_ref, cos_ref, sin_ref, o_ref, h_ref, z_ref, *, ctx):
    n = pl.program_id(1)
    L = x_ref.shape[1]
    nchunk = L // ROWS
    PAD = 8

    @pl.when(n == 0)
    def _():
        z_ref[0:PAD, :] = jnp.zeros((PAD, TN), f32)
        z_ref[PAD + L:PAD + L + PAD, :] = jnp.zeros((PAD, TN), f32)
        for i in range(nchunk):
            r0 = i * ROWS
            xb = x_ref[0, r0:r0 + ROWS, :]
            y = xb * lax.rsqrt(jnp.mean(xb * xb, axis=-1, keepdims=True) + NORM_EPS) * g_ref[...]
            o = 2 if r0 < ctx else 0
            sh = mod_ref[0, o:o + 1, :]
            sc = mod_ref[0, o + 1:o + 2, :]
            h_ref[r0:r0 + ROWS, :] = (y * (1.0 + sc) + sh).astype(bf16)

    for i in range(nchunk):
        r0 = i * ROWS
        z_ref[PAD + r0:PAD + r0 + ROWS, :] = _dot(h_ref[r0:r0 + ROWS, :], w_ref[...])

    def emit(fn):
        for i in range(nchunk):
            r0 = i * ROWS
            o_ref[0, r0:r0 + ROWS, :] = fn(z_ref[PAD + r0:PAD + r0 + ROWS, :], r0).astype(bf16)

    def rope(z, r0, scale):
        if scale != 1.0:
            z = z * scale
        if r0 < ctx:
            return z
        t0 = r0 - ctx
        cs = cos_ref[t0:t0 + ROWS, :]
        sn = sin_ref[t0:t0 + ROWS, :]
        parts = []
        for hh in range(TN // RET_HEAD_DIM):
            zh = z[:, hh * RET_HEAD_DIM:(hh + 1) * RET_HEAD_DIM]
            parts.append(zh * cs + pltpu.roll(zh, RET_HEAD_DIM // 2, 1) * sn)
        return jnp.concatenate(parts, axis=1)

    def shift(z, r0):
        prev = z_ref[PAD + r0 - 1:PAD + r0 - 1 + ROWS, :]
        nxt = z_ref[PAD + r0 + 1:PAD + r0 + 1 + ROWS, :]
        rid = r0 + lax.broadcasted_iota(jnp.int32, (ROWS, 1), 0)
        prev = jnp.where(rid == ctx, 0.0, prev)
        nxt = jnp.where(rid == ctx - 1, 0.0, nxt)
        return z + mu_ref[0:1, :] * (prev - z) + mu_ref[1:2, :] * (nxt - z)

    def lora_act(z, r0):
        zs = shift(z, r0)
        lane = lax.broadcasted_iota(jnp.int32, (1, TN), 1)
        return jnp.where(lane < DECAY_LORA, jnp.tanh(zs),
                         jnp.where(lane < DECAY_LORA + ICLR_LORA, zs, jax.nn.sigmoid(zs)))

    t = lambda c: c // TN

    @pl.when(n < t(C_RETK))
    def _():
        emit(lambda z, r0: jax.nn.sigmoid(z))

    @pl.when((n >= t(C_RETK)) & (n < t(C_RETV)))
    def _():
        emit(lambda z, r0: rope(z, r0, RET_HEAD_DIM ** -0.5))

    @pl.when((n >= t(C_RETV)) & (n < t(C_RETQ)))
    def _():
        emit(lambda z, r0: z)

    @pl.when((n >= t(C_RETQ)) & (n < t(C_RETG)))
    def _():
        emit(lambda z, r0: rope(z, r0, 1.0))

    @pl.when((n >= t(C_RETG)) & (n < t(C_RWK)))
    def _():
        emit(lambda z, r0: z * jax.nn.sigmoid(z))

    @pl.when((n >= t(C_RWK)) & (n < t(C_LORA)))
    def _():
        emit(shift)

    @pl.when(n >= t(C_LORA))
    def _():
        emit(lora_act)


def _inproj(xall, modrows, g0, w_perm, mu_full, cosf, sinf, ctx):
    B, L, D = xall.shape
    T = L - ctx
    nt = IN_COLS // TN
    return pl.pallas_call(
        functools.partial(_inproj_kernel, ctx=ctx),
        grid=(B, nt),
        in_specs=[pl.BlockSpec((1, L, D), lambda b, n: (b, 0, 0)),
                  pl.BlockSpec((1, 4, D), lambda b, n: (b, 0, 0)),
                  pl.BlockSpec((1, D), lambda b, n: (0, 0)),
                  pl.BlockSpec((D, TN), lambda b, n: (0, n)),
                  pl.BlockSpec((2, TN), lambda b, n: (0, n)),
                  pl.BlockSpec((T, RET_HEAD_DIM), lambda b, n: (0, 0)),
                  pl.BlockSpec((T, RET_HEAD_DIM), lambda b, n: (0, 0))],
        out_specs=pl.BlockSpec((1, L, TN), lambda b, n: (b, 0, n)),
        out_shape=jax.ShapeDtypeStruct((B, L, IN_COLS), bf16),
        scratch_shapes=[pltpu.VMEM((L, D), bf16), pltpu.VMEM((L + 16, TN), f32)],
        compiler_params=_params(("arbitrary", "arbitrary")),
        name="inproj",
    )(xall, modrows, g0, w_perm, mu_full, cosf, sinf)


def _ret_kernel(lg_ref, q_ref, k_ref, v_ref, o_ref, *, ctx):
    h = pl.program_id(1)
    L = q_ref.shape[1]
    Cc = RET_CHUNK
    nc = ctx // Cc
    nl = (L - ctx) // Cc
    ii = lax.broadcasted_iota(jnp.int32, (Cc, Cc), 0).astype(f32)
    jj = lax.broadcasted_iota(jnp.int32, (Cc, Cc), 1).astype(f32)

    for d in (0, 1):
        lg = lg_ref[d, h]
        if d == 0:
            diff = ii - jj
            cross = jnp.exp(lg * (ii + 1.0))
            tailw = jnp.exp(lg * (Cc - 1.0 - ii))
        else:
            diff = jj - ii
            cross = jnp.exp(lg * (Cc - ii))
            tailw = jnp.exp(lg * ii)
        intra = jnp.where(diff >= 0, jnp.exp(lg * jnp.maximum(diff, 0.0)), 0.0)
        cdv = jnp.exp(jnp.zeros((Cc, Cc), f32) + lg * Cc)

        def update(r, kc, vc):
            ks = (kc.astype(f32) * tailw).astype(bf16)
            return r * cdv + _dot_tn(ks, vc)

        def ctx_step(s, r):
            row0 = s * Cc if d == 0 else (nc - 1 - s) * Cc
            row0 = pl.multiple_of(row0, Cc)
            return update(r, k_ref[0, pl.ds(row0, Cc), :], v_ref[0, pl.ds(row0, Cc), :])

        def lat_step(s, r):
            t0 = s * Cc if d == 0 else (nl - 1 - s) * Cc
            t0 = pl.multiple_of(t0, Cc)
            row0 = pl.multiple_of(t0 + ctx, Cc)
            qc = q_ref[0, pl.ds(row0, Cc), :]
            kc = k_ref[0, pl.ds(row0, Cc), :]
            vc = v_ref[0, pl.ds(row0, Cc), :]
            sc = _dot_nt(qc, kc) * intra
            o = _dot(sc.astype(bf16), vc) + _dot(qc, r.astype(bf16)) * cross
            if d == 0:
                o_ref[0, pl.ds(t0, Cc), :] = o
            else:
                o_ref[0, pl.ds(t0, Cc), :] += o
            return update(r, kc, vc)

        r = lax.fori_loop(0, nc, ctx_step, jnp.zeros((Cc, Cc), f32))
        lax.fori_loop(0, nl, lat_step, r)


def _retention(lg, z, ctx):
    B, L, _ = z.shape
    T = L - ctx
    hd = RET_HEAD_DIM
    blk = lambda c0: pl.BlockSpec((1, L, hd), lambda b, h: (b, 0, c0 // hd + h))
    return pl.pallas_call(
        functools.partial(_ret_kernel, ctx=ctx),
        grid=(B, RET_HEADS),
        in_specs=[pl.BlockSpec(memory_space=pltpu.SMEM), blk(C_RETQ), blk(C_RETK), blk(C_RETV)],
        out_specs=pl.BlockSpec((1, T, hd), lambda b, h: (b, 0, h)),
        out_shape=jax.ShapeDtypeStruct((B, T, RET_WIDTH), f32),
        compiler_params=_params(("arbitrary", "arbitrary")),
        name="ret",
    )(lg, z, z, z)


def _rwkv_kernel(k_ref, v_ref, r_ref, lo_ref, w0_ref, a0_ref, w2_ref, a2_ref, kk_ref, ka_ref, rk_ref,
                 o_ref, bo_ref, s_ref, *, nc, nb):
    d = pl.program_id(1)
    s = pl.program_id(2)
    C = RWKV_CHUNK
    G = 2 * RWKV_HEAD_DIM
    rev = d == 1

    @pl.when(s == 0)
    def _():
        s_ref[...] = jnp.zeros(s_ref.shape, f32)

    ii = lax.broadcasted_iota(jnp.int32, (C, C), 0)
    jj = lax.broadcasted_iota(jnp.int32, (C, C), 1)
    dlt = jnp.where(rev, ii - jj, jj - ii)
    strict = dlt < 0
    incl = dlt <= 0
    eye = (ii == jj).astype(f32)
    gi = lax.broadcasted_iota(jnp.int32, (G, G), 0)
    gj = lax.broadcasted_iota(jnp.int32, (G, G), 1)
    blockdiag = (gi >= RWKV_HEAD_DIM) == (gj >= RWKV_HEAD_DIM)
    ones_bd = blockdiag.astype(bf16)
    lane = lax.broadcasted_iota(jnp.int32, (1, G), 1)
    hmask = [(lane < RWKV_HEAD_DIM).astype(f32), (lane >= RWKV_HEAD_DIM).astype(f32)]

    def segsum(x):
        return jnp.concatenate(
            [_split_dot(x[:, p * G:(p + 1) * G], ones_bd) for p in range(RWKV_PAIRS)], axis=1)

    tri = incl.astype(bf16)
    base = (ii >> 1) == (jj >> 1)
    offs = [((ii >> (lv + 1)) == (jj >> (lv + 1))) & ((ii >> lv) != (jj >> lv)) for lv in range(1, 6)]

    pre = []
    for bb in range(nb):
        kx = k_ref[bb].astype(f32)
        vx = v_ref[bb].astype(f32)
        rx = r_ref[bb].astype(f32)
        lo = lo_ref[bb, :, 0:DECAY_LORA + ICLR_LORA]
        u = w0_ref[0] + _dot(lo, w2_ref[0])
        softplus = jnp.maximum(-u, 0.0) + jnp.log1p(jnp.exp(-jnp.abs(u)))
        lw = -jnp.exp(-softplus - 0.5)
        a = jax.nn.sigmoid(a0_ref[0] + _dot(lo, a2_ref[0]))
        kkr = kx * kk_ref[...]
        kk = kkr * lax.rsqrt(segsum(kkr * kkr) + 1e-12)
        kd = kx * (1.0 + (a - 1.0) * ka_ref[...])
        be = kk * a
        bonus = segsum(rx * kd * rk_ref[...]) * vx
        cum = _split_dot_left(tri, lw)
        tot = jnp.where(rev, cum[0:1, :], cum[C - 1:C, :])
        gneg = jnp.exp(-cum)
        gh = jnp.exp(tot - cum)
        pre.append(dict(alb=-kk * jnp.exp(cum - lw), rb=rx * jnp.exp(cum), beb=be * gneg, kb=kd * gneg,
                        beh=be * gh, kh=kd * gh, etot=jnp.exp(tot), v=vx, bonus=bonus))

    units = [(bb, p) for bb in range(nb) for p in range(RWKV_PAIRS)]
    heads = [(un, hh) for un in units for hh in range(2)]
    sl = lambda p: slice(p * G, (p + 1) * G)
    S = {un: s_ref[un[0] * RWKV_PAIRS + un[1]] for un in units}
    Sb = {un: S[un].astype(bf16) for un in units}
    X = {un: pre[un[0]]["alb"][:, sl(un[1])] for un in units}
    Rb = {un: pre[un[0]]["rb"][:, sl(un[1])] for un in units}
    Yb = {un: pre[un[0]]["beb"][:, sl(un[1])].astype(bf16) for un in units}
    Yk = {un: pre[un[0]]["kb"][:, sl(un[1])].astype(bf16) for un in units}
    V = {un: pre[un[0]]["v"][:, sl(un[1])] for un in units}
    Vh = {(un, hh): (V[un] * hmask[hh]).astype(bf16) for un, hh in heads}

    Aab, Aak, Pab, Pak = {}, {}, {}, {}
    for un, hh in heads:
        Xh = (X[un] * hmask[hh]).astype(bf16)
        Rh = (Rb[un] * hmask[hh]).astype(bf16)
        Aab[un, hh] = jnp.where(strict, _dot_nt(Xh, Yb[un]), 0.0)
        Aak[un, hh] = jnp.where(strict, _dot_nt(Xh, Yk[un]), 0.0)
        Pab[un, hh] = jnp.where(incl, _dot_nt(Rh, Yb[un]), 0.0).astype(bf16)
        Pak[un, hh] = jnp.where(incl, _dot_nt(Rh, Yk[un]), 0.0).astype(bf16)

    Tm = {hd: eye + jnp.where(base, Aab[hd], 0.0) for hd in heads}
    for off in offs:
        Xs = {hd: _bdot(jnp.where(off, Aab[hd], 0.0), Tm[hd]) for hd in heads}
        Tm = {hd: Tm[hd] + _bdot(Tm[hd], Xs[hd]) for hd in heads}

    Bm = {un: _dot_nt(X[un].astype(bf16), Sb[un]) + _dot(Aak[un, 0].astype(bf16), Vh[un, 0])
          + _dot(Aak[un, 1].astype(bf16), Vh[un, 1]) for un in units}
    Om = {un: _dot_nt(Rb[un].astype(bf16), Sb[un]) + _dot(Pak[un, 0], Vh[un, 0]) + _dot(Pak[un, 1], Vh[un, 1])
          for un in units}
    U = {un: _bdot(Tm[un, 0], Bm[un] * hmask[0]) + _bdot(Tm[un, 1], Bm[un] * hmask[1]) for un in units}
    Om = {un: Om[un] + _dot(Pab[un, 0], (U[un] * hmask[0]).astype(bf16))
          + _dot(Pab[un, 1], (U[un] * hmask[1]).astype(bf16)) for un in units}
    for un in units:
        bb, p = un
        upd = (_dot_tn(U[un].astype(bf16), pre[bb]["beh"][:, sl(p)].astype(bf16))
               + _dot_tn(V[un].astype(bf16), pre[bb]["kh"][:, sl(p)].astype(bf16)))
        s_ref[bb * RWKV_PAIRS + p] = jnp.where(blockdiag, S[un] * pre[bb]["etot"][:, sl(p)] + upd, 0.0)

    @pl.when(s >= nc)
    def _():
        for bb in range(nb):
            o_ref[0, bb] = jnp.concatenate([Om[bb, p] for p in range(RWKV_PAIRS)], axis=1)
            bo_ref[0, bb] = pre[bb]["bonus"]


def _split_dot_left(w, x):
    hi = x.astype(bf16)
    lo = (x - hi.astype(f32)).astype(bf16)
    return _dot(w, hi) + _dot(w, lo)


def _rwkv(z, w0, a0, w2p, a2p, k_k, k_a, r_k, ctx):
    B, L, _ = z.shape
    T = L - ctx
    C = RWKV_CHUNK
    nc, nl = ctx // C, T // C
    W = RWKV_WIDTH

    def chunk(d, s):
        fwd = s
        bwd = jnp.where(s < nc, nc - 1 - s, nc + nl - 1 - (s - nc))
        return jnp.where(d == 0, fwd, bwd)

    def ochunk(d, s):
        sl = jnp.maximum(s - nc, 0)
        return jnp.where(d == 0, sl, nl - 1 - sl)

    nb = RWKV_SAMPLES_PER_STEP if B % RWKV_SAMPLES_PER_STEP == 0 else 1
    zblk = lambda c0, w: pl.BlockSpec((nb, C, w), lambda b, d, s: (b, chunk(d, s), c0 // w))
    dpar = lambda r: pl.BlockSpec((1, r, W), lambda b, d, s: (d, 0, 0))
    par = pl.BlockSpec((1, W), lambda b, d, s: (0, 0))
    oblk = pl.BlockSpec((1, nb, C, W), lambda b, d, s: (d, b, ochunk(d, s), 0))
    G = 2 * RWKV_HEAD_DIM
    return pl.pallas_call(
        functools.partial(_rwkv_kernel, nc=nc, nb=nb),
        grid=(B // nb, 2, nc + nl),
        in_specs=[zblk(C_RWK, W), zblk(C_RWV, W), zblk(C_RWR, W), zblk(C_LORA, 256),
                  dpar(1), dpar(1), dpar(G), dpar(G), par, par, par],
        out_specs=[oblk, oblk],
        out_shape=[jax.ShapeDtypeStruct((2, B, T, W), f32), jax.ShapeDtypeStruct((2, B, T, W), f32)],
        scratch_shapes=[pltpu.VMEM((nb * RWKV_PAIRS, G, G), f32)],
        compiler_params=_params(("arbitrary", "arbitrary", "arbitrary")),
        name="rwkv",
    )(z, z, z, z, w0, a0, w2p, a2p, k_k, k_a, r_k)


def _merge_kernel(x_ref, ret_ref, rw0_ref, rw1_ref, b0_ref, b1_ref, gate_ref, retg_ref, gd_ref, mod_ref,
                  ng_ref, gn_ref, lng_ref, lnb_ref, g2_ref, wbr_ref, wbw_ref, wout_ref, wr_ref,
                  x1_ref, h2_ref, lt_ref):
    hd = RET_HEAD_DIM
    ret = ret_ref[0]
    parts = []
    for hh in range(RET_HEADS):
        xh = ret[:, hh * hd:(hh + 1) * hd]
        mu = jnp.mean(xh, axis=-1, keepdims=True)
        dv = xh - mu
        var = jnp.mean(dv * dv, axis=-1, keepdims=True)
        parts.append(dv * lax.rsqrt(var + RET_EPS))
    yr = retg_ref[0].astype(f32) * (jnp.concatenate(parts, axis=1) * gn_ref[...])
    y_ret = _bdot(yr, wbr_ref[...])
    W = RWKV_WIDTH
    gi = lax.broadcasted_iota(jnp.int32, (W, W), 0)
    gj = lax.broadcasted_iota(jnp.int32, (W, W), 1)
    ones_bd = ((gi >> 6) == (gj >> 6)).astype(bf16)
    o = rw0_ref[0, 0] + rw1_ref[0, 0]
    mu = _split_dot(o, ones_bd) * (1.0 / RWKV_HEAD_DIM)
    dv = o - mu
    var = _split_dot(dv * dv, ones_bd) * (1.0 / RWKV_HEAD_DIM)
    yw = dv * lax.rsqrt(var + RWKV_EPS) * lng_ref[...] + lnb_ref[...]
    gate = _dot(gd_ref[0], g2_ref[...])
    yw = (yw + b0_ref[0, 0] + b1_ref[0, 0]) * gate
    y_rw = _bdot(yw, wbw_ref[...])
    D = y_ret.shape[1]
    g = gate_ref[0].astype(f32)
    m = g[:, :D] * y_ret + g[:, D:] * y_rw
    y = _bdot(m, wout_ref[...])

    def rms(v, gg):
        return v * lax.rsqrt(jnp.mean(v * v, axis=-1, keepdims=True) + NORM_EPS) * gg

    x1 = x_ref[0] + mod_ref[0, 0:1, :] * rms(y, ng_ref[0:1, :])
    x1_ref[0] = x1
    h2 = rms(x1, ng_ref[1:2, :]) * (1.0 + mod_ref[0, 2:3, :]) + mod_ref[0, 1:2, :]
    h2_ref[0] = h2.astype(bf16)
    lt_ref[0] = lax.dot_general(wr_ref[...], h2, (((1,), (1,)), ((), ())),
                                preferred_element_type=f32, precision=lax.Precision.HIGHEST)


def _merge(x, ret_o, rw_o, bonus, z, mod2, ng12, gn, lng, lnb, g2, wbr, wbw, wout, wrt, ctx):
    B, T, D = x.shape
    tm = 256
    co = ctx // tm
    W = RWKV_WIDTH
    row = lambda w: pl.BlockSpec((1, tm, w), lambda b, i: (b, i, 0))
    dblk = lambda dd: pl.BlockSpec((1, 1, tm, W), lambda b, i: (dd, b, i, 0))
    zblk = lambda c0, w: pl.BlockSpec((1, tm, w), lambda b, i: (b, co + i, c0 // w))
    full = lambda a: pl.BlockSpec(a.shape, lambda b, i: (0,) * a.ndim)
    return pl.pallas_call(
        _merge_kernel,
        grid=(B, T // tm),
        in_specs=[row(D), row(W), dblk(0), dblk(1), dblk(0), dblk(1),
                  zblk(C_MERGE, 2 * D), zblk(C_RETG, W), zblk(C_LORA + 128, 128),
                  pl.BlockSpec((1, 3, D), lambda b, i: (b, 0, 0)),
                  full(ng12), full(gn), full(lng), full(lnb), full(g2), full(wbr), full(wbw), full(wout),
                  full(wrt)],
        out_specs=[row(D), row(D), pl.BlockSpec((1, N_EXPERTS, tm), lambda b, i: (b, 0, i))],
        out_shape=[jax.ShapeDtypeStruct((B, T, D), f32), jax.ShapeDtypeStruct((B, T, D), bf16),
                   jax.ShapeDtypeStruct((B, N_EXPERTS, T), f32)],
        compiler_params=_params(("arbitrary", "arbitrary")),
        name="merge",
    )(x, ret_o, rw_o, rw_o, bonus, bonus, z, z, z, mod2, ng12, gn, lng, lnb, g2, wbr, wbw, wout, wrt)


def _route_kernel(lt_ref, tri_ref, slot_ref, rt_ref, *, cap):
    lg = lt_ref[0]
    E, T = lg.shape
    mx = jnp.max(lg, axis=0, keepdims=True)
    ex = jnp.exp(lg - mx)
    aff = ex / jnp.sum(ex, axis=0, keepdims=True)

    def count_ge(cand):
        return jnp.sum((aff >= cand).astype(f32), axis=1, keepdims=True)

    def exp_step(_, kk):
        k_lo, k_hi = kk
        km = jnp.floor((k_lo + k_hi) * 0.5)
        ok = count_ge(jnp.exp2(-km)) >= cap
        return jnp.where(ok, k_lo, km), jnp.where(ok, km, k_hi)

    k_lo, k_hi = lax.fori_loop(0, ROUTE_EXP_STEPS, exp_step,
                               (jnp.full((E, 1), -1.0, f32), jnp.full((E, 1), ROUTE_MAX_EXP, f32)))
    lo0 = jnp.where(k_hi >= ROUTE_MAX_EXP, 0.0, jnp.exp2(-k_hi))
    hi0 = jnp.exp2(-k_lo)

    def val_step(_, lh):
        lo, hi = lh
        mid = (lo + hi) * 0.5
        ok = count_ge(mid) >= cap
        return jnp.where(ok, mid, lo), jnp.where(ok, hi, mid)

    lo, hi = lax.fori_loop(0, ROUTE_VAL_STEPS, val_step, (lo0, hi0))
    gt = aff >= hi
    eq = (aff >= lo) & (aff < hi)
    need = cap - jnp.sum(gt.astype(f32), axis=1, keepdims=True)
    tri = tri_ref[...]
    eq_before = _dot(eq.astype(bf16), tri)
    sel = gt | (eq & (eq_before < need))
    slot = _dot(sel.astype(bf16), tri)
    slot_f = jnp.where(sel, slot, -1.0)
    slot_ref[0] = slot_f.astype(jnp.int32)
    gate = jnp.where(sel, aff, 0.0)
    packed = jnp.concatenate([slot_f, gate, jnp.zeros((128 - 2 * E, T), f32)], axis=0)
    rt_ref[0] = packed.T


def _route(lt, tri, cap):
    B, E, T = lt.shape
    return pl.pallas_call(
        functools.partial(_route_kernel, cap=cap),
        grid=(B,),
        in_specs=[pl.BlockSpec((1, E, T), lambda b: (b, 0, 0)),
                  pl.BlockSpec((T, T), lambda b: (0, 0))],
        out_specs=[pl.BlockSpec((1, E, T), lambda b: (b, 0, 0)),
                   pl.BlockSpec((1, T, 128), lambda b: (b, 0, 0))],
        out_shape=[jax.ShapeDtypeStruct((B, E, T), jnp.int32), jax.ShapeDtypeStruct((B, T, 128), f32)],
        compiler_params=_params(("arbitrary",)),
        name="route",
    )(lt, tri)


def _ffn_kernel(slot_ref, h_ref, wg_ref, wu_ref, wd_ref, o_ref, *, cap):
    T = h_ref.shape[1]
    ci = lax.broadcasted_iota(jnp.int32, (cap, T), 0)
    onehot = (slot_ref[0, 0] == ci).astype(bf16)
    xg = _dot(onehot, h_ref[0]).astype(bf16)
    hg = _dot(xg, wg_ref[0])
    hu = _dot(xg, wu_ref[0])
    hid = (hg * jax.nn.sigmoid(hg) * hu).astype(bf16)
    o_ref[0, 0] = _dot(hid, wd_ref[0]).astype(bf16)


def _ffn(slot4, h2, wg, wu, wd, cap):
    B, T, D = h2.shape
    E, _, F = wg.shape
    return pl.pallas_call(
        functools.partial(_ffn_kernel, cap=cap),
        grid=(E, B),
        in_specs=[pl.BlockSpec((1, 1, 1, T), lambda e, b: (b, e, 0, 0)),
                  pl.BlockSpec((1, T, D), lambda e, b: (b, 0, 0)),
                  pl.BlockSpec((1, D, F), lambda e, b: (e, 0, 0)),
                  pl.BlockSpec((1, D, F), lambda e, b: (e, 0, 0)),
                  pl.BlockSpec((1, F, D), lambda e, b: (e, 0, 0))],
        out_specs=pl.BlockSpec((1, 1, cap, D), lambda e, b: (b, e, 0, 0)),
        out_shape=jax.ShapeDtypeStruct((B, E, cap, D), bf16),
        compiler_params=_params(("arbitrary", "arbitrary")),
        name="ffn",
    )(slot4, h2, wg, wu, wd)


def _combine_kernel(rt_ref, eo_ref, x1_ref, mod_ref, ng_ref, o_ref, *, cap):
    rt = rt_ref[0]
    tm = rt.shape[0]
    E = eo_ref.shape[1]
    ci = lax.broadcasted_iota(jnp.int32, (tm, cap), 1).astype(f32)
    y = jnp.zeros((tm, eo_ref.shape[3]), f32)
    for e in range(E):
        p = jnp.where(rt[:, e:e + 1] == ci, rt[:, E + e:E + e + 1], 0.0).astype(bf16)
        y = y + _dot(p, eo_ref[0, e])
    yn = y * lax.rsqrt(jnp.mean(y * y, axis=-1, keepdims=True) + NORM_EPS) * ng_ref[...]
    o_ref[0] = x1_ref[0] + mod_ref[0] * yn


def _combine(rt, eo, x1, g2mod, ng3, cap):
    B, T, D = x1.shape
    E = eo.shape[1]
    tm = 512 if T % 512 == 0 else T
    return pl.pallas_call(
        functools.partial(_combine_kernel, cap=cap),
        grid=(B, T // tm),
        in_specs=[pl.BlockSpec((1, tm, 128), lambda b, i: (b, i, 0)),
                  pl.BlockSpec((1, E, cap, D), lambda b, i: (b, 0, 0, 0)),
                  pl.BlockSpec((1, tm, D), lambda b, i: (b, i, 0)),
                  pl.BlockSpec((1, 1, D), lambda b, i: (b, 0, 0)),
                  pl.BlockSpec((1, D), lambda b, i: (0, 0))],
        out_specs=pl.BlockSpec((1, tm, D), lambda b, i: (b, i, 0)),
        out_shape=jax.ShapeDtypeStruct((B, T, D), f32),
        compiler_params=_params(("arbitrary", "arbitrary")),
        name="combine",
    )(rt, eo, x1, g2mod, ng3)


def _column_perm():
    sk, sv, rk, rv, wd, ad = 0, 512, 1024, 1536, 2048, 2112
    q0 = 2176
    rq, rg, rr, gd, mg = q0, q0 + 512, q0 + 1024, q0 + 1536, q0 + 1664
    rng = lambda a, n: np.arange(a, a + n)
    return np.concatenate([rng(mg, 2048), rng(sk, 512), rng(sv, 512), rng(rq, 512), rng(rg, 512),
                           rng(rk, 512), rng(rv, 512), rng(rr, 512), rng(wd, 64), rng(ad, 64), rng(gd, 128)])


def _rope_tables(T):
    t = jnp.arange(T)
    nfreq = RET_HEAD_DIM // 4
    inv = ROPE_BASE ** (-jnp.arange(nfreq, dtype=f32) / nfreq)
    ang = jnp.concatenate([(t // GRID_W).astype(f32)[:, None] * inv,
                           (t % GRID_W).astype(f32)[:, None] * inv], axis=-1)
    cos, sin = jnp.cos(ang), jnp.sin(ang)
    return jnp.concatenate([cos, cos], axis=1), jnp.concatenate([-sin, sin], axis=1)


def kernel(x, c, ctx, c_ctx, w_mod, b_mod, norm_g, w_in, ret_log_decay, ret_gn_g, rwkv_mu, rwkv_k_k, rwkv_k_a,
           rwkv_r_k, rwkv_w0, rwkv_w2, rwkv_a0, rwkv_a2, rwkv_g2, rwkv_ln_g, rwkv_ln_b, w_br_ret, w_br_rwkv,
           w_out, w_router, w_gate, w_up, w_down):
    B, T, D = x.shape
    CT = ctx.shape[1]
    assert w_mod.shape[0] == 1 and D == D_MODEL
    assert CT % ROWS == 0 and T % ROWS == 0 and T % GRID_W == 0
    cap = CAPACITY_FACTOR * T // N_EXPERTS
    assert cap % 8 == 0

    mrows = -(-(B + 1) // 8) * 8
    cc = jnp.zeros((mrows, D), f32).at[:B].set(c).at[B].set(c_ctx)
    mod = _modulation(cc, w_mod[0], b_mod[0])
    lat = mod[:B].reshape(B, N_MOD, D)
    cm = jnp.broadcast_to(mod[B].reshape(1, N_MOD, D), (B, N_MOD, D))
    modrows = jnp.concatenate([lat[:, 0:2], cm[:, 0:2]], axis=1)

    perm = _column_perm()
    w_perm = w_in[0][:, perm].astype(bf16)
    mu = rwkv_mu[0]
    ss = 2 * RWKV_WIDTH + DECAY_LORA + ICLR_LORA
    mu_full = jnp.zeros((2, IN_COLS), f32)
    mu_full = mu_full.at[:, C_RWK:C_RWK + 1024].set(mu[:, 0:1024])
    mu_full = mu_full.at[:, C_RWR:C_RWR + 512].set(mu[:, ss:ss + 512])
    mu_full = mu_full.at[:, C_LORA:C_LORA + 128].set(mu[:, 1024:ss])
    mu_full = mu_full.at[:, C_LORA + 128:].set(mu[:, ss + 512:])
    cosf, sinf = _rope_tables(T)
    xall = jnp.concatenate([ctx, x], axis=1)
    z = _inproj(xall, modrows, norm_g[0, 0:1], w_perm, mu_full, cosf, sinf, CT)

    lg = -jnp.exp(ret_log_decay[0].astype(f32))
    ret_o = _retention(lg, z, CT)

    G = 2 * RWKV_HEAD_DIM
    w2p = jnp.zeros((2, G, RWKV_WIDTH), f32).at[:, :DECAY_LORA].set(rwkv_w2[0]).astype(bf16)
    a2p = jnp.zeros((2, G, RWKV_WIDTH), f32).at[:, DECAY_LORA:].set(rwkv_a2[0]).astype(bf16)
    rw_o, bonus = _rwkv(z, rwkv_w0[0][:, None, :], rwkv_a0[0][:, None, :], w2p, a2p,
                        rwkv_k_k[0][None], rwkv_k_a[0][None], rwkv_r_k[0][None], CT)

    mod2 = jnp.stack([lat[:, 2], lat[:, 3], lat[:, 4]], axis=1)
    x1, h2, lt = _merge(x, ret_o, rw_o, bonus, z, mod2, norm_g[0, 1:3], ret_gn_g[0][None], rwkv_ln_g[0][None],
                        rwkv_ln_b[0][None], rwkv_g2[0].astype(bf16), w_br_ret[0].astype(bf16),
                        w_br_rwkv[0].astype(bf16), w_out[0].astype(bf16), w_router[0].T, CT)

    ti = jnp.arange(T)
    tri = (ti[:, None] < ti[None, :]).astype(bf16)
    slot, rt = _route(lt, tri, cap)
    eo = _ffn(slot.reshape(B, N_EXPERTS, 1, T), h2, w_gate[0].astype(bf16), w_up[0].astype(bf16),
              w_down[0].astype(bf16), cap)
    return _combine(rt, eo, x1, lat[:, 5:6], norm_g[0, 3:4], cap)
```

```python
import functools

import numpy as np
import jax
import jax.numpy as jnp
from jax import lax
from jax.experimental import pallas as pl
from jax.experimental.pallas import tpu as pltpu

f32 = jnp.float32
bf16 = jnp.bfloat16

D_MODEL = 1024
GRID_W = 64
RET_HEAD_DIM = 128
RET_WIDTH = 512
RET_HEADS = 4
RET_CHUNK = 128
RET_EPS = 1e-5
ROPE_BASE = 10000.0
RWKV_HEAD_DIM = 64
RWKV_WIDTH = 512
RWKV_PAIRS = 4
RWKV_CHUNK = 64
RWKV_SAMPLES_PER_STEP = 4
DECAY_LORA = 64
ICLR_LORA = 64
GATE_LORA = 128
RWKV_EPS = 64e-5
N_EXPERTS = 16
EXPERT_FF = 1024
CAPACITY_FACTOR = 2
N_MOD = 6
NORM_EPS = 1e-6

C_MERGE, C_RETK, C_RETV, C_RETQ, C_RETG = 0, 2048, 2560, 3072, 3584
C_RWK, C_RWV, C_RWR, C_LORA = 4096, 4608, 5120, 5632
IN_COLS = 5888
TN = 256
ROWS = 256

ROUTE_MAX_EXP = 126.0
ROUTE_EXP_STEPS = 8
ROUTE_VAL_STEPS = 26

VMEM_LIMIT = 56 * 1024 * 1024


def _dot(a, b):
    return jnp.dot(a, b, preferred_element_type=f32)


def _dot_nt(a, b):
    return lax.dot_general(a, b, (((1,), (1,)), ((), ())), preferred_element_type=f32)


def _dot_tn(a, b):
    return lax.dot_general(a, b, (((0,), (0,)), ((), ())), preferred_element_type=f32)


def _bdot(a, b):
    return _dot(a.astype(bf16), b.astype(bf16))


def _split_dot(x, w):
    hi = x.astype(bf16)
    lo = (x - hi.astype(f32)).astype(bf16)
    return _dot(hi, w) + _dot(lo, w)


def _params(sem, limit=VMEM_LIMIT):
    return pltpu.CompilerParams(dimension_semantics=sem, vmem_limit_bytes=limit)


def _mod_kernel(c_ref, w_ref, b_ref, o_ref):
    c = c_ref[...]
    s = c * jax.nn.sigmoid(c)
    o_ref[...] = _bdot(s, w_ref[...]) + b_ref[...]


def _modulation(cc, w_mod, b_mod):
    m, d = cc.shape
    n = w_mod.shape[1]
    tn = 512
    return pl.pallas_call(
        _mod_kernel,
        grid=(n // tn,),
        in_specs=[pl.BlockSpec((m, d), lambda j: (0, 0)),
                  pl.BlockSpec((d, tn), lambda j: (0, j)),
                  pl.BlockSpec((1, tn), lambda j: (0, j))],
        out_specs=pl.BlockSpec((m, tn), lambda j: (0, j)),
        out_shape=jax.ShapeDtypeStruct((m, n), f32),
        compiler_params=_params(("arbitrary",)),
        name="mod",
    )(cc, w_mod, b_mod.reshape(1, n))


def _inproj_kernel(x_ref, mod_ref, g_ref, w_ref, mu_ref, cos_ref, sin_ref, o_ref, h_ref, z_ref, *, ctx):
    n = pl.program_id(1)
    L = x_ref.shape[1]
    nchunk = L // ROWS
    PAD = 8

    @pl.when(n == 0)
    def _():
        z_ref[0:PAD, :] = jnp.zeros((PAD, TN), f32)
        z_ref[PAD + L:PAD + L + PAD, :] = jnp.zeros((PAD, TN), f32)
        for i in range(nchunk):
            r0 = i * ROWS
            xb = x_ref[0, r0:r0 + ROWS, :]
            y = xb * lax.rsqrt(jnp.mean(xb * xb, axis=-1, keepdims=True) + NORM_EPS) * g_ref[...]
            o = 2 if r0 < ctx else 0
            sh = mod_ref[0, o:o + 1, :]
            sc = mod_ref[0, o + 1:o + 2, :]
            h_ref[r0:r0 + ROWS, :] = (y * (1.0 + sc) + sh).astype(bf16)

    for i in range(nchunk):
        r0 = i * ROWS
        z_ref[PAD + r0:PAD + r0 + ROWS, :] = _dot(h_ref[r0:r0 + ROWS, :], w_ref[...])

    def emit(fn):
        for i in range(nchunk):
            r0 = i * ROWS
            o_ref[0, r0:r0 + ROWS, :] = fn(z_ref[PAD + r0:PAD + r0 + ROWS, :], r0).astype(bf16)

    def rope(z, r0, scale):
        if scale != 1.0:
            z = z * scale
        if r0 < ctx:
            return z
        t0 = r0 - ctx
        cs = cos_ref[t0:t0 + ROWS, :]
        sn = sin_ref[t0:t0 + ROWS, :]
        parts = []
        for hh in range(TN // RET_HEAD_DIM):
            zh = z[:, hh * RET_HEAD_DIM:(hh + 1) * RET_HEAD_DIM]
            parts.append(zh * cs + pltpu.roll(zh, RET_HEAD_DIM // 2, 1) * sn)
        return jnp.concatenate(parts, axis=1)

    def shift(z, r0):
        prev = z_ref[PAD + r0 - 1:PAD + r0 - 1 + ROWS, :]
        nxt = z_ref[PAD + r0 + 1:PAD + r0 + 1 + ROWS, :]
        rid = r0 + lax.broadcasted_iota(jnp.int32, (ROWS, 1), 0)
        prev = jnp.where(rid == ctx, 0.0, prev)
        nxt = jnp.where(rid == ctx - 1, 0.0, nxt)
        return z + mu_ref[0:1, :] * (prev - z) + mu_ref[1:2, :] * (nxt - z)

    def lora_act(z, r0):
        zs = shift(z, r0)
        lane = lax.broadcasted_iota(jnp.int32, (1, TN), 1)
        return jnp.where(lane < DECAY_LORA, jnp.tanh(zs),
                         jnp.where(lane < DECAY_LORA + ICLR_LORA, zs, jax.nn.sigmoid(zs)))

    t = lambda c: c // TN

    @pl.when(n < t(C_RETK))
    def _():
        emit(lambda z, r0: jax.nn.sigmoid(z))

    @pl.when((n >= t(C_RETK)) & (n < t(C_RETV)))
    def _():
        emit(lambda z, r0: rope(z, r0, RET_HEAD_DIM ** -0.5))

    @pl.when((n >= t(C_RETV)) & (n < t(C_RETQ)))
    def _():
        emit(lambda z, r0: z)

    @pl.when((n >= t(C_RETQ)) & (n < t(C_RETG)))
    def _():
        emit(lambda z, r0: rope(z, r0, 1.0))

    @pl.when((n >= t(C_RETG)) & (n < t(C_RWK)))
    def _():
        emit(lambda z, r0: z * jax.nn.sigmoid(z))

    @pl.when((n >= t(C_RWK)) & (n < t(C_LORA)))
    def _():
        emit(shift)

    @pl.when(n >= t(C_LORA))
    def _():
        emit(lora_act)


def _inproj(xall, modrows, g0, w_perm, mu_full, cosf, sinf, ctx):
    B, L, D = xall.shape
    T = L - ctx
    nt = IN_COLS // TN
    return pl.pallas_call(
        functools.partial(_inproj_kernel, ctx=ctx),
        grid=(B, nt),
        in_specs=[pl.BlockSpec((1, L, D), lambda b, n: (b, 0, 0)),
                  pl.BlockSpec((1, 4, D), lambda b, n: (b, 0, 0)),
                  pl.BlockSpec((1, D), lambda b, n: (0, 0)),
                  pl.BlockSpec((D, TN), lambda b, n: (0, n)),
                  pl.BlockSpec((2, TN), lambda b, n: (0, n)),
                  pl.BlockSpec((T, RET_HEAD_DIM), lambda b, n: (0, 0)),
                  pl.BlockSpec((T, RET_HEAD_DIM), lambda b, n: (0, 0))],
        out_specs=pl.BlockSpec((1, L, TN), lambda b, n: (b, 0, n)),
        out_shape=jax.ShapeDtypeStruct((B, L, IN_COLS), bf16),
        scratch_shapes=[pltpu.VMEM((L, D), bf16), pltpu.VMEM((L + 16, TN), f32)],
        compiler_params=_params(("arbitrary", "arbitrary")),
        name="inproj",
    )(xall, modrows, g0, w_perm, mu_full, cosf, sinf)


def _ret_kernel(lg_ref, q_ref, k_ref, v_ref, o_ref, r_ref, tab_ref, *, ctx):
    L = q_ref.shape[1]
    Cc = RET_CHUNK
    hd = RET_HEAD_DIM
    nc = ctx // Cc
    nl = (L - ctx) // Cc
    combos = [(h, d) for h in range(RET_HEADS) for d in (0, 1)]
    INTRA, CROSS, TAIL, DECAY = 0, 1, 2, 3

    @pl.when(pl.program_id(0) == 0)
    def _():
        ii = lax.broadcasted_iota(jnp.int32, (Cc, Cc), 0).astype(f32)
        jj = lax.broadcasted_iota(jnp.int32, (Cc, Cc), 1).astype(f32)
        for idx, (h, d) in enumerate(combos):
            lg = lg_ref[d, h]
            if d == 0:
                diff = ii - jj
                cross = jnp.exp(lg * (ii + 1.0))
                tailw = jnp.exp(lg * (Cc - 1.0 - ii))
            else:
                diff = jj - ii
                cross = jnp.exp(lg * (Cc - ii))
                tailw = jnp.exp(lg * ii)
            tab_ref[idx, INTRA] = jnp.where(diff >= 0, jnp.exp(lg * jnp.maximum(diff, 0.0)), 0.0)
            tab_ref[idx, CROSS] = cross
            tab_ref[idx, TAIL] = tailw
            tab_ref[idx, DECAY] = jnp.exp(jnp.zeros((Cc, Cc), f32) + lg * Cc)

    r_ref[...] = jnp.zeros(r_ref.shape, f32)
    o_ref[...] = jnp.zeros(o_ref.shape, f32)

    def cols(h):
        return slice(h * hd, (h + 1) * hd)

    def update(idx, kc, vc):
        ks = (kc.astype(f32) * tab_ref[idx, TAIL]).astype(bf16)
        r_ref[idx] = r_ref[idx] * tab_ref[idx, DECAY] + _dot_tn(ks, vc)

    def ctx_step(s, carry):
        for idx, (h, d) in enumerate(combos):
            row0 = pl.multiple_of(s * Cc if d == 0 else (nc - 1 - s) * Cc, Cc)
            update(idx, k_ref[0, pl.ds(row0, Cc), cols(h)], v_ref[0, pl.ds(row0, Cc), cols(h)])
        return carry

    def lat_step(s, carry):
        t0s, qs, ks, vs = [], [], [], []
        for h, d in combos:
            t0 = pl.multiple_of(s * Cc if d == 0 else (nl - 1 - s) * Cc, Cc)
            row0 = pl.multiple_of(t0 + ctx, Cc)
            t0s.append(t0)
            qs.append(q_ref[0, pl.ds(row0, Cc), cols(h)])
            ks.append(k_ref[0, pl.ds(row0, Cc), cols(h)])
            vs.append(v_ref[0, pl.ds(row0, Cc), cols(h)])
        n = len(combos)
        sc = [(_dot_nt(qs[i], ks[i]) * tab_ref[i, INTRA]).astype(bf16) for i in range(n)]
        oc = [_dot(qs[i], r_ref[i].astype(bf16)) * tab_ref[i, CROSS] for i in range(n)]
        oi = [_dot(sc[i], vs[i]) for i in range(n)]
        for i, (h, d) in enumerate(combos):
            o_ref[0, pl.ds(t0s[i], Cc), cols(h)] += oi[i] + oc[i]
        for i in range(n):
            update(i, ks[i], vs[i])
        return carry

    lax.fori_loop(0, nc, ctx_step, 0)
    lax.fori_loop(0, nl, lat_step, 0)


def _retention(lg, z, ctx):
    B, L, _ = z.shape
    T = L - ctx
    W = RET_WIDTH
    Cc = RET_CHUNK
    blk = lambda c0: pl.BlockSpec((1, L, W), lambda b: (b, 0, c0 // W))
    nchain = 2 * RET_HEADS
    return pl.pallas_call(
        functools.partial(_ret_kernel, ctx=ctx),
        grid=(B,),
        in_specs=[pl.BlockSpec(memory_space=pltpu.SMEM), blk(C_RETQ), blk(C_RETK), blk(C_RETV)],
        out_specs=pl.BlockSpec((1, T, W), lambda b: (b, 0, 0)),
        out_shape=jax.ShapeDtypeStruct((B, T, W), f32),
        scratch_shapes=[pltpu.VMEM((nchain, Cc, Cc), f32), pltpu.VMEM((nchain, 4, Cc, Cc), f32)],
        compiler_params=_params(("arbitrary",)),
        name="ret",
    )(lg, z, z, z)


def _rwkv_kernel(k_ref, v_ref, r_ref, lo_ref, w0_ref, a0_ref, w2_ref, a2_ref, kk_ref, ka_ref, rk_ref,
                 o_ref, bo_ref, s_ref, *, nc, nb):
    d = pl.program_id(1)
    s = pl.program_id(2)
    C = RWKV_CHUNK
    G = 2 * RWKV_HEAD_DIM
    rev = d == 1

    @pl.when(s == 0)
    def _():
        s_ref[...] = jnp.zeros(s_ref.shape, f32)

    ii = lax.broadcasted_iota(jnp.int32, (C, C), 0)
    jj = lax.broadcasted_iota(jnp.int32, (C, C), 1)
    dlt = jnp.where(rev, ii - jj, jj - ii)
    strict = dlt < 0
    incl = dlt <= 0
    eye = (ii == jj).astype(f32)
    gi = lax.broadcasted_iota(jnp.int32, (G, G), 0)
    gj = lax.broadcasted_iota(jnp.int32, (G, G), 1)
    blockdiag = (gi >= RWKV_HEAD_DIM) == (gj >= RWKV_HEAD_DIM)
    ones_bd = blockdiag.astype(bf16)
    lane = lax.broadcasted_iota(jnp.int32, (1, G), 1)
    hmask = [(lane < RWKV_HEAD_DIM).astype(f32), (lane >= RWKV_HEAD_DIM).astype(f32)]

    def segsum(x):
        return jnp.concatenate(
            [_split_dot(x[:, p * G:(p + 1) * G], ones_bd) for p in range(RWKV_PAIRS)], axis=1)

    tri = incl.astype(bf16)
    base = (ii >> 1) == (jj >> 1)
    offs = [((ii >> (lv + 1)) == (jj >> (lv + 1))) & ((ii >> lv) != (jj >> lv)) for lv in range(1, 6)]

    pre = []
    for bb in range(nb):
        kx = k_ref[bb].astype(f32)
        vx = v_ref[bb].astype(f32)
        rx = r_ref[bb].astype(f32)
        lo = lo_ref[bb, :, 0:DECAY_LORA + ICLR_LORA]
        u = w0_ref[0] + _dot(lo, w2_ref[0])
        softplus = jnp.maximum(-u, 0.0) + jnp.log1p(jnp.exp(-jnp.abs(u)))
        lw = -jnp.exp(-softplus - 0.5)
        a = jax.nn.sigmoid(a0_ref[0] + _dot(lo, a2_ref[0]))
        kkr = kx * kk_ref[...]
        kk = kkr * lax.rsqrt(segsum(kkr * kkr) + 1e-12)
        kd = kx * (1.0 + (a - 1.0) * ka_ref[...])
        be = kk * a
        bonus = segsum(rx * kd * rk_ref[...]) * vx
        cum = _split_dot_left(tri, lw)
        tot = jnp.where(rev, cum[0:1, :], cum[C - 1:C, :])
        gneg = jnp.exp(-cum)
        gh = jnp.exp(tot - cum)
        pre.append(dict(alb=-kk * jnp.exp(cum - lw), rb=rx * jnp.exp(cum), beb=be * gneg, kb=kd * gneg,
                        beh=be * gh, kh=kd * gh, etot=jnp.exp(tot), v=vx, bonus=bonus))

    units = [(bb, p) for bb in range(nb) for p in range(RWKV_PAIRS)]
    heads = [(un, hh) for un in units for hh in range(2)]
    sl = lambda p: slice(p * G, (p + 1) * G)
    S = {un: s_ref[un[0] * RWKV_PAIRS + un[1]] for un in units}
    Sb = {un: S[un].astype(bf16) for un in units}
    X = {un: pre[un[0]]["alb"][:, sl(un[1])] for un in units}
    Rb = {un: pre[un[0]]["rb"][:, sl(un[1])] for un in units}
    Yb = {un: pre[un[0]]["beb"][:, sl(un[1])].astype(bf16) for un in units}
    Yk = {un: pre[un[0]]["kb"][:, sl(un[1])].astype(bf16) for un in units}
    V = {un: pre[un[0]]["v"][:, sl(un[1])] for un in units}
    Vh = {(un, hh): (V[un] * hmask[hh]).astype(bf16) for un, hh in heads}

    Aab, Aak, Pab, Pak = {}, {}, {}, {}
    for un, hh in heads:
        Xh = (X[un] * hmask[hh]).astype(bf16)
        Rh = (Rb[un] * hmask[hh]).astype(bf16)
        Aab[un, hh] = jnp.where(strict, _dot_nt(Xh, Yb[un]), 0.0)
        Aak[un, hh] = jnp.where(strict, _dot_nt(Xh, Yk[un]), 0.0)
        Pab[un, hh] = jnp.where(incl, _dot_nt(Rh, Yb[un]), 0.0).astype(bf16)
        Pak[un, hh] = jnp.where(incl, _dot_nt(Rh, Yk[un]), 0.0).astype(bf16)

    Tm = {hd: eye + jnp.where(base, Aab[hd], 0.0) for hd in heads}
    for off in offs:
        Xs = {hd: _bdot(jnp.where(off, Aab[hd], 0.0), Tm[hd]) for hd in heads}
        Tm = {hd: Tm[hd] + _bdot(Tm[hd], Xs[hd]) for hd in heads}

    Bm = {un: _dot_nt(X[un].astype(bf16), Sb[un]) + _dot(Aak[un, 0].astype(bf16), Vh[un, 0])
          + _dot(Aak[un, 1].astype(bf16), Vh[un, 1]) for un in units}
    Om = {un: _dot_nt(Rb[un].astype(bf16), Sb[un]) + _dot(Pak[un, 0], Vh[un, 0]) + _dot(Pak[un, 1], Vh[un, 1])
          for un in units}
    U = {un: _bdot(Tm[un, 0], Bm[un] * hmask[0]) + _bdot(Tm[un, 1], Bm[un] * hmask[1]) for un in units}
    Om = {un: Om[un] + _dot(Pab[un, 0], (U[un] * hmask[0]).astype(bf16))
          + _dot(Pab[un, 1], (U[un] * hmask[1]).astype(bf16)) for un in units}
    for un in units:
        bb, p = un
        upd = (_dot_tn(U[un].astype(bf16), pre[bb]["beh"][:, sl(p)].astype(bf16))
               + _dot_tn(V[un].astype(bf16), pre[bb]["kh"][:, sl(p)].astype(bf16)))
        s_ref[bb * RWKV_PAIRS + p] = jnp.where(blockdiag, S[un] * pre[bb]["etot"][:, sl(p)] + upd, 0.0)

    @pl.when(s >= nc)
    def _():
        for bb in range(nb):
            o_ref[0, bb] = jnp.concatenate([Om[bb, p] for p in range(RWKV_PAIRS)], axis=1)
            bo_ref[0, bb] = pre[bb]["bonus"]


def _split_dot_left(w, x):
    hi = x.astype(bf16)
    lo = (x - hi.astype(f32)).astype(bf16)
    return _dot(w, hi) + _dot(w, lo)


def _rwkv(z, w0, a0, w2p, a2p, k_k, k_a, r_k, ctx):
    B, L, _ = z.shape
    T = L - ctx
    C = RWKV_CHUNK
    nc, nl = ctx // C, T // C
    W = RWKV_WIDTH

    def chunk(d, s):
        fwd = s
        bwd = jnp.where(s < nc, nc - 1 - s, nc + nl - 1 - (s - nc))
        return jnp.where(d == 0, fwd, bwd)

    def ochunk(d, s):
        sl = jnp.maximum(s - nc, 0)
        return jnp.where(d == 0, sl, nl - 1 - sl)

    nb = RWKV_SAMPLES_PER_STEP if B % RWKV_SAMPLES_PER_STEP == 0 else 1
    zblk = lambda c0, w: pl.BlockSpec((nb, C, w), lambda b, d, s: (b, chunk(d, s), c0 // w))
    dpar = lambda r: pl.BlockSpec((1, r, W), lambda b, d, s: (d, 0, 0))
    par = pl.BlockSpec((1, W), lambda b, d, s: (0, 0))
    oblk = pl.BlockSpec((1, nb, C, W), lambda b, d, s: (d, b, ochunk(d, s), 0))
    G = 2 * RWKV_HEAD_DIM
    return pl.pallas_call(
        functools.partial(_rwkv_kernel, nc=nc, nb=nb),
        grid=(B // nb, 2, nc + nl),
        in_specs=[zblk(C_RWK, W), zblk(C_RWV, W), zblk(C_RWR, W), zblk(C_LORA, 256),
                  dpar(1), dpar(1), dpar(G), dpar(G), par, par, par],
        out_specs=[oblk, oblk],
        out_shape=[jax.ShapeDtypeStruct((2, B, T, W), f32), jax.ShapeDtypeStruct((2, B, T, W), f32)],
        scratch_shapes=[pltpu.VMEM((nb * RWKV_PAIRS, G, G), f32)],
        compiler_params=_params(("arbitrary", "arbitrary", "arbitrary")),
        name="rwkv",
    )(z, z, z, z, w0, a0, w2p, a2p, k_k, k_a, r_k)


def _merge_kernel(x_ref, ret_ref, rw0_ref, rw1_ref, b0_ref, b1_ref, gate_ref, retg_ref, gd_ref, mod_ref,
                  ng_ref, gn_ref, lng_ref, lnb_ref, g2_ref, wbr_ref, wbw_ref, wout_ref, wr_ref,
                  x1_ref, h2_ref, lt_ref):
    hd = RET_HEAD_DIM
    ret = ret_ref[0]
    parts = []
    for hh in range(RET_HEADS):
        xh = ret[:, hh * hd:(hh + 1) * hd]
        mu = jnp.mean(xh, axis=-1, keepdims=True)
        dv = xh - mu
        var = jnp.mean(dv * dv, axis=-1, keepdims=True)
        parts.append(dv * lax.rsqrt(var + RET_EPS))
    yr = retg_ref[0].astype(f32) * (jnp.concatenate(parts, axis=1) * gn_ref[...])
    y_ret = _bdot(yr, wbr_ref[...])
    W = RWKV_WIDTH
    gi = lax.broadcasted_iota(jnp.int32, (W, W), 0)
    gj = lax.broadcasted_iota(jnp.int32, (W, W), 1)
    ones_bd = ((gi >> 6) == (gj >> 6)).astype(bf16)
    o = rw0_ref[0, 0] + rw1_ref[0, 0]
    mu = _split_dot(o, ones_bd) * (1.0 / RWKV_HEAD_DIM)
    dv = o - mu
    var = _split_dot(dv * dv, ones_bd) * (1.0 / RWKV_HEAD_DIM)
    yw = dv * lax.rsqrt(var + RWKV_EPS) * lng_ref[...] + lnb_ref[...]
    gate = _dot(gd_ref[0], g2_ref[...])
    yw = (yw + b0_ref[0, 0] + b1_ref[0, 0]) * gate
    y_rw = _bdot(yw, wbw_ref[...])
    D = y_ret.shape[1]
    g = gate_ref[0].astype(f32)
    m = g[:, :D] * y_ret + g[:, D:] * y_rw
    y = _bdot(m, wout_ref[...])

    def rms(v, gg):
        return v * lax.rsqrt(jnp.mean(v * v, axis=-1, keepdims=True) + NORM_EPS) * gg

    x1 = x_ref[0] + mod_ref[0, 0:1, :] * rms(y, ng_ref[0:1, :])
    x1_ref[0] = x1
    h2 = rms(x1, ng_ref[1:2, :]) * (1.0 + mod_ref[0, 2:3, :]) + mod_ref[0, 1:2, :]
    h2_ref[0] = h2.astype(bf16)
    lt_ref[0] = lax.dot_general(wr_ref[...], h2, (((1,), (1,)), ((), ())),
                                preferred_element_type=f32, precision=lax.Precision.HIGHEST)


def _merge(x, ret_o, rw_o, bonus, z, mod2, ng12, gn, lng, lnb, g2, wbr, wbw, wout, wrt, ctx):
    B, T, D = x.shape
    tm = 256
    co = ctx // tm
    W = RWKV_WIDTH
    row = lambda w: pl.BlockSpec((1, tm, w), lambda b, i: (b, i, 0))
    dblk = lambda dd: pl.BlockSpec((1, 1, tm, W), lambda b, i: (dd, b, i, 0))
    zblk = lambda c0, w: pl.BlockSpec((1, tm, w), lambda b, i: (b, co + i, c0 // w))
    full = lambda a: pl.BlockSpec(a.shape, lambda b, i: (0,) * a.ndim)
    return pl.pallas_call(
        _merge_kernel,
        grid=(B, T // tm),
        in_specs=[row(D), row(W), dblk(0), dblk(1), dblk(0), dblk(1),
                  zblk(C_MERGE, 2 * D), zblk(C_RETG, W), zblk(C_LORA + 128, 128),
                  pl.BlockSpec((1, 3, D), lambda b, i: (b, 0, 0)),
                  full(ng12), full(gn), full(lng), full(lnb), full(g2), full(wbr), full(wbw), full(wout),
                  full(wrt)],
        out_specs=[row(D), row(D), pl.BlockSpec((1, N_EXPERTS, tm), lambda b, i: (b, 0, i))],
        out_shape=[jax.ShapeDtypeStruct((B, T, D), f32), jax.ShapeDtypeStruct((B, T, D), bf16),
                   jax.ShapeDtypeStruct((B, N_EXPERTS, T), f32)],
        compiler_params=_params(("arbitrary", "arbitrary")),
        name="merge",
    )(x, ret_o, rw_o, rw_o, bonus, bonus, z, z, z, mod2, ng12, gn, lng, lnb, g2, wbr, wbw, wout, wrt)


def _route_kernel(lt_ref, tri_ref, slot_ref, rt_ref, *, cap):
    lg = lt_ref[0]
    E, T = lg.shape
    mx = jnp.max(lg, axis=0, keepdims=True)
    ex = jnp.exp(lg - mx)
    aff = ex / jnp.sum(ex, axis=0, keepdims=True)

    def count_ge(cand):
        return jnp.sum((aff >= cand).astype(f32), axis=1, keepdims=True)

    def exp_step(_, kk):
        k_lo, k_hi = kk
        km = jnp.floor((k_lo + k_hi) * 0.5)
        ok = count_ge(jnp.exp2(-km)) >= cap
        return jnp.where(ok, k_lo, km), jnp.where(ok, km, k_hi)

    k_lo, k_hi = lax.fori_loop(0, ROUTE_EXP_STEPS, exp_step,
                               (jnp.full((E, 1), -1.0, f32), jnp.full((E, 1), ROUTE_MAX_EXP, f32)))
    lo0 = jnp.where(k_hi >= ROUTE_MAX_EXP, 0.0, jnp.exp2(-k_hi))
    hi0 = jnp.exp2(-k_lo)

    def val_step(_, lh):
        lo, hi = lh
        mid = (lo + hi) * 0.5
        ok = count_ge(mid) >= cap
        return jnp.where(ok, mid, lo), jnp.where(ok, hi, mid)

    lo, hi = lax.fori_loop(0, ROUTE_VAL_STEPS, val_step, (lo0, hi0))
    gt = aff >= hi
    eq = (aff >= lo) & (aff < hi)
    need = cap - jnp.sum(gt.astype(f32), axis=1, keepdims=True)
    tri = tri_ref[...]
    eq_before = _dot(eq.astype(bf16), tri)
    sel = gt | (eq & (eq_before < need))
    slot = _dot(sel.astype(bf16), tri)
    slot_f = jnp.where(sel, slot, -1.0)
    slot_ref[0] = slot_f.astype(jnp.int32)
    gate = jnp.where(sel, aff, 0.0)
    packed = jnp.concatenate([slot_f, gate, jnp.zeros((128 - 2 * E, T), f32)], axis=0)
    rt_ref[0] = packed.T


def _route(lt, tri, cap):
    B, E, T = lt.shape
    return pl.pallas_call(
        functools.partial(_route_kernel, cap=cap),
        grid=(B,),
        in_specs=[pl.BlockSpec((1, E, T), lambda b: (b, 0, 0)),
                  pl.BlockSpec((T, T), lambda b: (0, 0))],
        out_specs=[pl.BlockSpec((1, E, T), lambda b: (b, 0, 0)),
                   pl.BlockSpec((1, T, 128), lambda b: (b, 0, 0))],
        out_shape=[jax.ShapeDtypeStruct((B, E, T), jnp.int32), jax.ShapeDtypeStruct((B, T, 128), f32)],
        compiler_params=_params(("arbitrary",)),
        name="route",
    )(lt, tri)


def _ffn_kernel(slot_ref, h_ref, wg_ref, wu_ref, wd_ref, o_ref, *, cap):
    T = h_ref.shape[1]
    ci = lax.broadcasted_iota(jnp.int32, (cap, T), 0)
    onehot = (slot_ref[0, 0] == ci).astype(bf16)
    xg = _dot(onehot, h_ref[0]).astype(bf16)
    hg = _dot(xg, wg_ref[0])
    hu = _dot(xg, wu_ref[0])
    hid = (hg * jax.nn.sigmoid(hg) * hu).astype(bf16)
    o_ref[0, 0] = _dot(hid, wd_ref[0]).astype(bf16)


def _ffn(slot4, h2, wg, wu, wd, cap):
    B, T, D = h2.shape
    E, _, F = wg.shape
    return pl.pallas_call(
        functools.partial(_ffn_kernel, cap=cap),
        grid=(E, B),
        in_specs=[pl.BlockSpec((1, 1, 1, T), lambda e, b: (b, e, 0, 0)),
                  pl.BlockSpec((1, T, D), lambda e, b: (b, 0, 0)),
                  pl.BlockSpec((1, D, F), lambda e, b: (e, 0, 0)),
                  pl.BlockSpec((1, D, F), lambda e, b: (e, 0, 0)),
                  pl.BlockSpec((1, F, D), lambda e, b: (e, 0, 0))],
        out_specs=pl.BlockSpec((1, 1, cap, D), lambda e, b: (b, e, 0, 0)),
        out_shape=jax.ShapeDtypeStruct((B, E, cap, D), bf16),
        compiler_params=_params(("arbitrary", "arbitrary")),
        name="ffn",
    )(slot4, h2, wg, wu, wd)


def _combine_kernel(rt_ref, eo_ref, x1_ref, mod_ref, ng_ref, o_ref, *, cap):
    rt = rt_ref[0]
    tm = rt.shape[0]
    E = eo_ref.shape[1]
    ci = lax.broadcasted_iota(jnp.int32, (tm, cap), 1).astype(f32)
    y = jnp.zeros((tm, eo_ref.shape[3]), f32)
    for e in range(E):
        p = jnp.where(rt[:, e:e + 1] == ci, rt[:, E + e:E + e + 1], 0.0).astype(bf16)
        y = y + _dot(p, eo_ref[0, e])
    yn = y * lax.rsqrt(jnp.mean(y * y, axis=-1, keepdims=True) + NORM_EPS) * ng_ref[...]
    o_ref[0] = x1_ref[0] + mod_ref[0] * yn


def _combine(rt, eo, x1, g2mod, ng3, cap):
    B, T, D = x1.shape
    E = eo.shape[1]
    tm = 512 if T % 512 == 0 else T
    return pl.pallas_call(
        functools.partial(_combine_kernel, cap=cap),
        grid=(B, T // tm),
        in_specs=[pl.BlockSpec((1, tm, 128), lambda b, i: (b, i, 0)),
                  pl.BlockSpec((1, E, cap, D), lambda b, i: (b, 0, 0, 0)),
                  pl.BlockSpec((1, tm, D), lambda b, i: (b, i, 0)),
                  pl.BlockSpec((1, 1, D), lambda b, i: (b, 0, 0)),
                  pl.BlockSpec((1, D), lambda b, i: (0, 0))],
        out_specs=pl.BlockSpec((1, tm, D), lambda b, i: (b, i, 0)),
        out_shape=jax.ShapeDtypeStruct((B, T, D), f32),
        compiler_params=_params(("arbitrary", "arbitrary")),
        name="combine",
    )(rt, eo, x1, g2mod, ng3)


def _column_perm():
    sk, sv, rk, rv, wd, ad = 0, 512, 1024, 1536, 2048, 2112
    q0 = 2176
    rq, rg, rr, gd, mg = q0, q0 + 512, q0 + 1024, q0 + 1536, q0 + 1664
    rng = lambda a, n: np.arange(a, a + n)
    return np.concatenate([rng(mg, 2048), rng(sk, 512), rng(sv, 512), rng(rq, 512), rng(rg, 512),
                           rng(rk, 512), rng(rv, 512), rng(rr, 512), rng(wd, 64), rng(ad, 64), rng(gd, 128)])


def _rope_tables(T):
    t = jnp.arange(T)
    nfreq = RET_HEAD_DIM // 4
    inv = ROPE_BASE ** (-jnp.arange(nfreq, dtype=f32) / nfreq)
    ang = jnp.concatenate([(t // GRID_W).astype(f32)[:, None] * inv,
                           (t % GRID_W).astype(f32)[:, None] * inv], axis=-1)
    cos, sin = jnp.cos(ang), jnp.sin(ang)
    return jnp.concatenate([cos, cos], axis=1), jnp.concatenate([-sin, sin], axis=1)


def kernel(x, c, ctx, c_ctx, w_mod, b_mod, norm_g, w_in, ret_log_decay, ret_gn_g, rwkv_mu, rwkv_k_k, rwkv_k_a,
           rwkv_r_k, rwkv_w0, rwkv_w2, rwkv_a0, rwkv_a2, rwkv_g2, rwkv_ln_g, rwkv_ln_b, w_br_ret, w_br_rwkv,
           w_out, w_router, w_gate, w_up, w_down):
    B, T, D = x.shape
    CT = ctx.shape[1]
    assert w_mod.shape[0] == 1 and D == D_MODEL
    assert CT % ROWS == 0 and T % ROWS == 0 and T % GRID_W == 0
    cap = CAPACITY_FACTOR * T // N_EXPERTS
    assert cap % 8 == 0

    mrows = -(-(B + 1) // 8) * 8
    cc = jnp.zeros((mrows, D), f32).at[:B].set(c).at[B].set(c_ctx)
    mod = _modulation(cc, w_mod[0], b_mod[0])
    lat = mod[:B].reshape(B, N_MOD, D)
    cm = jnp.broadcast_to(mod[B].reshape(1, N_MOD, D), (B, N_MOD, D))
    modrows = jnp.concatenate([lat[:, 0:2], cm[:, 0:2]], axis=1)

    perm = _column_perm()
    w_perm = w_in[0][:, perm].astype(bf16)
    mu = rwkv_mu[0]
    ss = 2 * RWKV_WIDTH + DECAY_LORA + ICLR_LORA
    mu_full = jnp.zeros((2, IN_COLS), f32)
    mu_full = mu_full.at[:, C_RWK:C_RWK + 1024].set(mu[:, 0:1024])
    mu_full = mu_full.at[:, C_RWR:C_RWR + 512].set(mu[:, ss:ss + 512])
    mu_full = mu_full.at[:, C_LORA:C_LORA + 128].set(mu[:, 1024:ss])
    mu_full = mu_full.at[:, C_LORA + 128:].set(mu[:, ss + 512:])
    cosf, sinf = _rope_tables(T)
    xall = jnp.concatenate([ctx, x], axis=1)
    z = _inproj(xall, modrows, norm_g[0, 0:1], w_perm, mu_full, cosf, sinf, CT)

    lg = -jnp.exp(ret_log_decay[0].astype(f32))
    ret_o = _retention(lg, z, CT)

    G = 2 * RWKV_HEAD_DIM
    w2p = jnp.zeros((2, G, RWKV_WIDTH), f32).at[:, :DECAY_LORA].set(rwkv_w2[0]).astype(bf16)
    a2p = jnp.zeros((2, G, RWKV_WIDTH), f32).at[:, DECAY_LORA:].set(rwkv_a2[0]).astype(bf16)
    rw_o, bonus = _rwkv(z, rwkv_w0[0][:, None, :], rwkv_a0[0][:, None, :], w2p, a2p,
                        rwkv_k_k[0][None], rwkv_k_a[0][None], rwkv_r_k[0][None], CT)

    mod2 = jnp.stack([lat[:, 2], lat[:, 3], lat[:, 4]], axis=1)
    x1, h2, lt = _merge(x, ret_o, rw_o, bonus, z, mod2, norm_g[0, 1:3], ret_gn_g[0][None], rwkv_ln_g[0][None],
                        rwkv_ln_b[0][None], rwkv_g2[0].astype(bf16), w_br_ret[0].astype(bf16),
                        w_br_rwkv[0].astype(bf16), w_out[0].astype(bf16), w_router[0].T, CT)

    ti = jnp.arange(T)
    tri = (ti[:, None] < ti[None, :]).astype(bf16)
    slot, rt = _route(lt, tri, cap)
    eo = _ffn(slot.reshape(B, N_EXPERTS, 1, T), h2, w_gate[0].astype(bf16), w_up[0].astype(bf16),
              w_down[0].astype(bf16), cap)
    return _combine(rt, eo, x1, lat[:, 5:6], norm_g[0, 3:4], cap)
```

```python
import functools

import numpy as np
import jax
import jax.numpy as jnp
from jax import lax
from jax.experimental import pallas as pl
from jax.experimental.pallas import tpu as pltpu

f32 = jnp.float32
bf16 = jnp.bfloat16

D_MODEL = 1024
GRID_W = 64
RET_HEAD_DIM = 128
RET_WIDTH = 512
RET_HEADS = 4
RET_CHUNK = 128
RET_EPS = 1e-5
ROPE_BASE = 10000.0
RWKV_HEAD_DIM = 64
RWKV_WIDTH = 512
RWKV_PAIRS = 4
RWKV_CHUNK = 64
RWKV_SAMPLES_PER_STEP = 4
DECAY_LORA = 64
ICLR_LORA = 64
GATE_LORA = 128
RWKV_EPS = 64e-5
N_EXPERTS = 16
EXPERT_FF = 1024
CAPACITY_FACTOR = 2
N_MOD = 6
NORM_EPS = 1e-6

C_MERGE, C_RETK, C_RETV, C_RETQ, C_RETG = 0, 2048, 2560, 3072, 3584
C_RWK, C_RWV, C_RWR, C_LORA = 4096, 4608, 5120, 5632
IN_COLS = 5888
TN = 256
ROWS = 256

ROUTE_MAX_EXP = 126.0
ROUTE_EXP_STEPS = 8
ROUTE_VAL_STEPS = 26

VMEM_LIMIT = 56 * 1024 * 1024


def _dot(a, b):
    return jnp.dot(a, b, preferred_element_type=f32)


def _dot_nt(a, b):
    return lax.dot_general(a, b, (((1,), (1,)), ((), ())), preferred_element_type=f32)


def _dot_tn(a, b):
    return lax.dot_general(a, b, (((0,), (0,)), ((), ())), preferred_element_type=f32)


def _bdot(a, b):
    return _dot(a.astype(bf16), b.astype(bf16))


def _split_dot(x, w):
    hi = x.astype(bf16)
    lo = (x - hi.astype(f32)).astype(bf16)
    return _dot(hi, w) + _dot(lo, w)


def _params(sem, limit=VMEM_LIMIT):
    return pltpu.CompilerParams(dimension_semantics=sem, vmem_limit_bytes=limit)


def _mod_kernel(c_ref, w_ref, b_ref, o_ref):
    c = c_ref[...]
    s = c * jax.nn.sigmoid(c)
    o_ref[...] = _bdot(s, w_ref[...]) + b_ref[...]


def _modulation(cc, w_mod, b_mod):
    m, d = cc.shape
    n = w_mod.shape[1]
    tn = 512
    return pl.pallas_call(
        _mod_kernel,
        grid=(n // tn,),
        in_specs=[pl.BlockSpec((m, d), lambda j: (0, 0)),
                  pl.BlockSpec((d, tn), lambda j: (0, j)),
                  pl.BlockSpec((1, tn), lambda j: (0, j))],
        out_specs=pl.BlockSpec((m, tn), lambda j: (0, j)),
        out_shape=jax.ShapeDtypeStruct((m, n), f32),
        compiler_params=_params(("arbitrary",)),
        name="mod",
    )(cc, w_mod, b_mod.reshape(1, n))


def _inproj_kernel(x_ref, mod_ref, g_ref, w_ref, mu_ref, cos_ref, sin_ref, o_ref, h_ref, z_ref, *, ctx):
    n = pl.program_id(1)
    L = x_ref.shape[1]
    nchunk = L // ROWS
    PAD = 8

    @pl.when(n == 0)
    def _():
        z_ref[0:PAD, :] = jnp.zeros((PAD, TN), f32)
        z_ref[PAD + L:PAD + L + PAD, :] = jnp.zeros((PAD, TN), f32)
        for i in range(nchunk):
            r0 = i * ROWS
            xb = x_ref[0, r0:r0 + ROWS, :]
            y = xb * lax.rsqrt(jnp.mean(xb * xb, axis=-1, keepdims=True) + NORM_EPS) * g_ref[...]
            o = 2 if r0 < ctx else 0
            sh = mod_ref[0, o:o + 1, :]
            sc = mod_ref[0, o + 1:o + 2, :]
            h_ref[r0:r0 + ROWS, :] = (y * (1.0 + sc) + sh).astype(bf16)

    for i in range(nchunk):
        r0 = i * ROWS
        z_ref[PAD + r0:PAD + r0 + ROWS, :] = _dot(h_ref[r0:r0 + ROWS, :], w_ref[...])

    def emit(fn):
        for i in range(nchunk):
            r0 = i * ROWS
            o_ref[0, r0:r0 + ROWS, :] = fn(z_ref[PAD + r0:PAD + r0 + ROWS, :], r0).astype(bf16)

    def rope(z, r0, scale):
        if scale != 1.0:
            z = z * scale
        if r0 < ctx:
            return z
        t0 = r0 - ctx
        cs = cos_ref[t0:t0 + ROWS, :]
        sn = sin_ref[t0:t0 + ROWS, :]
        parts = []
        for hh in range(TN // RET_HEAD_DIM):
            zh = z[:, hh * RET_HEAD_DIM:(hh + 1) * RET_HEAD_DIM]
            parts.append(zh * cs + pltpu.roll(zh, RET_HEAD_DIM // 2, 1) * sn)
        return jnp.concatenate(parts, axis=1)

    def shift(z, r0):
        prev = z_ref[PAD + r0 - 1:PAD + r0 - 1 + ROWS, :]
        nxt = z_ref[PAD + r0 + 1:PAD + r0 + 1 + ROWS, :]
        rid = r0 + lax.broadcasted_iota(jnp.int32, (ROWS, 1), 0)
        prev = jnp.where(rid == ctx, 0.0, prev)
        nxt = jnp.where(rid == ctx - 1, 0.0, nxt)
        return z + mu_ref[0:1, :] * (prev - z) + mu_ref[1:2, :] * (nxt - z)

    def lora_act(z, r0):
        zs = shift(z, r0)
        lane = lax.broadcasted_iota(jnp.int32, (1, TN), 1)
        return jnp.where(lane < DECAY_LORA, jnp.tanh(zs),
                         jnp.where(lane < DECAY_LORA + ICLR_LORA, zs, jax.nn.sigmoid(zs)))

    t = lambda c: c // TN

    @pl.when(n < t(C_RETK))
    def _():
        emit(lambda z, r0: jax.nn.sigmoid(z))

    @pl.when((n >= t(C_RETK)) & (n < t(C_RETV)))
    def _():
        emit(lambda z, r0: rope(z, r0, RET_HEAD_DIM ** -0.5))

    @pl.when((n >= t(C_RETV)) & (n < t(C_RETQ)))
    def _():
        emit(lambda z, r0: z)

    @pl.when((n >= t(C_RETQ)) & (n < t(C_RETG)))
    def _():
        emit(lambda z, r0: rope(z, r0, 1.0))

    @pl.when((n >= t(C_RETG)) & (n < t(C_RWK)))
    def _():
        emit(lambda z, r0: z * jax.nn.sigmoid(z))

    @pl.when((n >= t(C_RWK)) & (n < t(C_LORA)))
    def _():
        emit(shift)

    @pl.when(n >= t(C_LORA))
    def _():
        emit(lora_act)


def _inproj(xall, modrows, g0, w_perm, mu_full, cosf, sinf, ctx):
    B, L, D = xall.shape
    T = L - ctx
    nt = IN_COLS // TN
    return pl.pallas_call(
        functools.partial(_inproj_kernel, ctx=ctx),
        grid=(B, nt),
        in_specs=[pl.BlockSpec((1, L, D), lambda b, n: (b, 0, 0)),
                  pl.BlockSpec((1, 4, D), lambda b, n: (b, 0, 0)),
                  pl.BlockSpec((1, D), lambda b, n: (0, 0)),
                  pl.BlockSpec((D, TN), lambda b, n: (0, n)),
                  pl.BlockSpec((2, TN), lambda b, n: (0, n)),
                  pl.BlockSpec((T, RET_HEAD_DIM), lambda b, n: (0, 0)),
                  pl.BlockSpec((T, RET_HEAD_DIM), lambda b, n: (0, 0))],
        out_specs=pl.BlockSpec((1, L, TN), lambda b, n: (b, 0, n)),
        out_shape=jax.ShapeDtypeStruct((B, L, IN_COLS), bf16),
        scratch_shapes=[pltpu.VMEM((L, D), bf16), pltpu.VMEM((L + 16, TN), f32)],
        compiler_params=_params(("arbitrary", "arbitrary")),
        name="inproj",
    )(xall, modrows, g0, w_perm, mu_full, cosf, sinf)


def _ret_kernel(lg_ref, q_ref, k_ref, v_ref, o_ref, r_ref, tab_ref, *, ctx):
    L = q_ref.shape[1]
    Cc = RET_CHUNK
    hd = RET_HEAD_DIM
    nc = ctx // Cc
    nl = (L - ctx) // Cc
    combos = [(h, d) for h in range(RET_HEADS) for d in (0, 1)]
    INTRA, CROSS, TAIL, DECAY = 0, 1, 2, 3

    @pl.when(pl.program_id(0) == 0)
    def _():
        ii = lax.broadcasted_iota(jnp.int32, (Cc, Cc), 0).astype(f32)
        jj = lax.broadcasted_iota(jnp.int32, (Cc, Cc), 1).astype(f32)
        for idx, (h, d) in enumerate(combos):
            lg = lg_ref[d, h]
            if d == 0:
                diff = ii - jj
                cross = jnp.exp(lg * (ii + 1.0))
                tailw = jnp.exp(lg * (Cc - 1.0 - ii))
            else:
                diff = jj - ii
                cross = jnp.exp(lg * (Cc - ii))
                tailw = jnp.exp(lg * ii)
            tab_ref[idx, INTRA] = jnp.where(diff >= 0, jnp.exp(lg * jnp.maximum(diff, 0.0)), 0.0)
            tab_ref[idx, CROSS] = cross
            tab_ref[idx, TAIL] = tailw
            tab_ref[idx, DECAY] = jnp.exp(jnp.zeros((Cc, Cc), f32) + lg * Cc)

    r_ref[...] = jnp.zeros(r_ref.shape, f32)
    o_ref[...] = jnp.zeros(o_ref.shape, f32)

    def cols(h):
        return slice(h * hd, (h + 1) * hd)

    def update(idx, kc, vc):
        ks = (kc.astype(f32) * tab_ref[idx, TAIL]).astype(bf16)
        r_ref[idx] = r_ref[idx] * tab_ref[idx, DECAY] + _dot_tn(ks, vc)

    def ctx_step(s, carry):
        for idx, (h, d) in enumerate(combos):
            row0 = pl.multiple_of(s * Cc if d == 0 else (nc - 1 - s) * Cc, Cc)
            update(idx, k_ref[0, pl.ds(row0, Cc), cols(h)], v_ref[0, pl.ds(row0, Cc), cols(h)])
        return carry

    def lat_step(s, carry):
        t0s, qs, ks, vs = [], [], [], []
        for h, d in combos:
            t0 = pl.multiple_of(s * Cc if d == 0 else (nl - 1 - s) * Cc, Cc)
            row0 = pl.multiple_of(t0 + ctx, Cc)
            t0s.append(t0)
            qs.append(q_ref[0, pl.ds(row0, Cc), cols(h)])
            ks.append(k_ref[0, pl.ds(row0, Cc), cols(h)])
            vs.append(v_ref[0, pl.ds(row0, Cc), cols(h)])
        n = len(combos)
        sc = [(_dot_nt(qs[i], ks[i]) * tab_ref[i, INTRA]).astype(bf16) for i in range(n)]
        oc = [_dot(qs[i], r_ref[i].astype(bf16)) * tab_ref[i, CROSS] for i in range(n)]
        oi = [_dot(sc[i], vs[i]) for i in range(n)]
        for i, (h, d) in enumerate(combos):
            o_ref[0, pl.ds(t0s[i], Cc), cols(h)] += oi[i] + oc[i]
        for i in range(n):
            update(i, ks[i], vs[i])
        return carry

    lax.fori_loop(0, nc, ctx_step, 0)
    lax.fori_loop(0, nl, lat_step, 0)


def _retention(lg, z, ctx):
    B, L, _ = z.shape
    T = L - ctx
    W = RET_WIDTH
    Cc = RET_CHUNK
    blk = lambda c0: pl.BlockSpec((1, L, W), lambda b: (b, 0, c0 // W))
    nchain = 2 * RET_HEADS
    return pl.pallas_call(
        functools.partial(_ret_kernel, ctx=ctx),
        grid=(B,),
        in_specs=[pl.BlockSpec(memory_space=pltpu.SMEM), blk(C_RETQ), blk(C_RETK), blk(C_RETV)],
        out_specs=pl.BlockSpec((1, T, W), lambda b: (b, 0, 0)),
        out_shape=jax.ShapeDtypeStruct((B, T, W), f32),
        scratch_shapes=[pltpu.VMEM((nchain, Cc, Cc), f32), pltpu.VMEM((nchain, 4, Cc, Cc), f32)],
        compiler_params=_params(("arbitrary",)),
        name="ret",
    )(lg, z, z, z)


def _rwkv_kernel(k_ref, v_ref, r_ref, lo_ref, w0_ref, a0_ref, w2_ref, a2_ref, kk_ref, ka_ref, rk_ref,
                 o_ref, bo_ref, s_ref, *, nc, nb):
    d = pl.program_id(1)
    s = pl.program_id(2)
    C = RWKV_CHUNK
    G = 2 * RWKV_HEAD_DIM
    rev = d == 1

    @pl.when(s == 0)
    def _():
        s_ref[...] = jnp.zeros(s_ref.shape, f32)

    ii = lax.broadcasted_iota(jnp.int32, (C, G), 0)
    lane = lax.broadcasted_iota(jnp.int32, (C, G), 1)
    jj = lane & (RWKV_HEAD_DIM - 1)
    head0 = lane < RWKV_HEAD_DIM
    dlt = jnp.where(rev, ii - jj, jj - ii)
    strict = dlt < 0
    incl = dlt <= 0
    eye = (ii == jj).astype(f32)
    gi = lax.broadcasted_iota(jnp.int32, (G, G), 0)
    gj = lax.broadcasted_iota(jnp.int32, (G, G), 1)
    blockdiag = (gi >= RWKV_HEAD_DIM) == (gj >= RWKV_HEAD_DIM)
    ones_bd = blockdiag.astype(bf16)
    tri = incl[:, 0:C].astype(bf16)
    base = (ii >> 1) == (jj >> 1)
    offs = [((ii >> (lv + 1)) == (jj >> (lv + 1))) & ((ii >> lv) != (jj >> lv)) for lv in range(1, 6)]

    def segsum(x):
        return jnp.concatenate(
            [_bdot(x[:, p * G:(p + 1) * G], ones_bd) for p in range(RWKV_PAIRS)], axis=1)

    def stack(x):
        xb = x.astype(bf16)
        zero = jnp.zeros_like(xb)
        return jnp.concatenate([jnp.where(head0, xb, zero), jnp.where(head0, zero, xb)], axis=0)

    pre = []
    for bb in range(nb):
        kx = k_ref[bb].astype(f32)
        vx = v_ref[bb].astype(f32)
        rx = r_ref[bb].astype(f32)
        lo = lo_ref[bb, :, 0:DECAY_LORA + ICLR_LORA]
        u = w0_ref[0] + _dot(lo, w2_ref[0])
        softplus = jnp.maximum(-u, 0.0) + jnp.log1p(jnp.exp(-jnp.abs(u)))
        lw = -jnp.exp(-softplus - 0.5)
        a = jax.nn.sigmoid(a0_ref[0] + _dot(lo, a2_ref[0]))
        kkr = kx * kk_ref[...]
        kk = kkr * lax.rsqrt(segsum(kkr * kkr) + 1e-12)
        kd = kx * (1.0 + (a - 1.0) * ka_ref[...])
        be = kk * a
        bonus = segsum(rx * kd * rk_ref[...]) * vx
        cum = _split_dot_left(tri, lw)
        tot = jnp.where(rev, cum[0:1, :], cum[C - 1:C, :])
        gneg = jnp.exp(-cum)
        gh = jnp.exp(tot - cum)
        pre.append(dict(alb=-kk * jnp.exp(cum - lw), rb=rx * jnp.exp(cum), beb=be * gneg, kb=kd * gneg,
                        beh=be * gh, kh=kd * gh, etot=jnp.exp(tot), v=vx, bonus=bonus))

    units = [(bb, p) for bb in range(nb) for p in range(RWKV_PAIRS)]
    sl = lambda p: slice(p * G, (p + 1) * G)
    part = lambda un, name: pre[un[0]][name][:, sl(un[1])]
    S = {un: s_ref[un[0] * RWKV_PAIRS + un[1]] for un in units}
    Sb = {un: S[un].astype(bf16) for un in units}
    X = {un: part(un, "alb").astype(bf16) for un in units}
    Rb = {un: part(un, "rb").astype(bf16) for un in units}
    Ybs = {un: stack(part(un, "beb")) for un in units}
    Yks = {un: stack(part(un, "kb")) for un in units}
    Vb = {un: part(un, "v").astype(bf16) for un in units}
    Vs = {un: stack(part(un, "v")) for un in units}

    XR = {un: jnp.concatenate([X[un], Rb[un]], axis=0) for un in units}
    APb = {un: _dot_nt(XR[un], Ybs[un]) for un in units}
    APk = {un: _dot_nt(XR[un], Yks[un]) for un in units}
    BO = {un: _dot_nt(XR[un], Sb[un]) for un in units}
    Aab = {un: jnp.where(strict, APb[un][0:C], 0.0) for un in units}
    Aak = {un: jnp.where(strict, APk[un][0:C], 0.0).astype(bf16) for un in units}
    Pab = {un: jnp.where(incl, APb[un][C:2 * C], 0.0).astype(bf16) for un in units}
    Pak = {un: jnp.where(incl, APk[un][C:2 * C], 0.0).astype(bf16) for un in units}

    Tm = {un: eye + jnp.where(base, Aab[un], 0.0) for un in units}
    for off in offs:
        Xs = {un: _dot(jnp.where(off, Aab[un], 0.0).astype(bf16), stack(Tm[un])) for un in units}
        Tm = {un: Tm[un] + _dot(Tm[un].astype(bf16), stack(Xs[un])) for un in units}

    Bm = {un: BO[un][0:C] + _dot(Aak[un], Vs[un]) for un in units}
    U = {un: _dot(Tm[un].astype(bf16), stack(Bm[un])) for un in units}
    Om = {un: BO[un][C:2 * C]
          + _dot(jnp.concatenate([Pab[un], Pak[un]], axis=1), jnp.concatenate([stack(U[un]), Vs[un]], axis=0))
          for un in units}
    for un in units:
        bb, p = un
        upd = _dot_tn(jnp.concatenate([U[un].astype(bf16), Vb[un]], axis=0),
                      jnp.concatenate([part(un, "beh").astype(bf16), part(un, "kh").astype(bf16)], axis=0))
        s_ref[bb * RWKV_PAIRS + p] = jnp.where(blockdiag, S[un] * part(un, "etot") + upd, 0.0)

    @pl.when(s >= nc)
    def _():
        for bb in range(nb):
            o_ref[0, bb] = jnp.concatenate([Om[bb, p] for p in range(RWKV_PAIRS)], axis=1)
            bo_ref[0, bb] = pre[bb]["bonus"]


def _split_dot_left(w, x):
    hi = x.astype(bf16)
    lo = (x - hi.astype(f32)).astype(bf16)
    return _dot(w, hi) + _dot(w, lo)


def _rwkv(z, w0, a0, w2p, a2p, k_k, k_a, r_k, ctx):
    B, L, _ = z.shape
    T = L - ctx
    C = RWKV_CHUNK
    nc, nl = ctx // C, T // C
    W = RWKV_WIDTH

    def chunk(d, s):
        fwd = s
        bwd = jnp.where(s < nc, nc - 1 - s, nc + nl - 1 - (s - nc))
        return jnp.where(d == 0, fwd, bwd)

    def ochunk(d, s):
        sl = jnp.maximum(s - nc, 0)
        return jnp.where(d == 0, sl, nl - 1 - sl)

    nb = RWKV_SAMPLES_PER_STEP if B % RWKV_SAMPLES_PER_STEP == 0 else 1
    zblk = lambda c0, w: pl.BlockSpec((nb, C, w), lambda b, d, s: (b, chunk(d, s), c0 // w))
    dpar = lambda r: pl.BlockSpec((1, r, W), lambda b, d, s: (d, 0, 0))
    par = pl.BlockSpec((1, W), lambda b, d, s: (0, 0))
    oblk = pl.BlockSpec((1, nb, C, W), lambda b, d, s: (d, b, ochunk(d, s), 0))
    G = 2 * RWKV_HEAD_DIM
    return pl.pallas_call(
        functools.partial(_rwkv_kernel, nc=nc, nb=nb),
        grid=(B // nb, 2, nc + nl),
        in_specs=[zblk(C_RWK, W), zblk(C_RWV, W), zblk(C_RWR, W), zblk(C_LORA, 256),
                  dpar(1), dpar(1), dpar(G), dpar(G), par, par, par],
        out_specs=[oblk, oblk],
        out_shape=[jax.ShapeDtypeStruct((2, B, T, W), f32), jax.ShapeDtypeStruct((2, B, T, W), f32)],
        scratch_shapes=[pltpu.VMEM((nb * RWKV_PAIRS, G, G), f32)],
        compiler_params=_params(("arbitrary", "arbitrary", "arbitrary")),
        name="rwkv",
    )(z, z, z, z, w0, a0, w2p, a2p, k_k, k_a, r_k)


def _merge_kernel(x_ref, ret_ref, rw0_ref, rw1_ref, b0_ref, b1_ref, gate_ref, retg_ref, gd_ref, mod_ref,
                  ng_ref, gn_ref, lng_ref, lnb_ref, g2_ref, wbr_ref, wbw_ref, wout_ref, wr_ref,
                  x1_ref, h2_ref, lt_ref):
    hd = RET_HEAD_DIM
    ret = ret_ref[0]
    parts = []
    for hh in range(RET_HEADS):
        xh = ret[:, hh * hd:(hh + 1) * hd]
        mu = jnp.mean(xh, axis=-1, keepdims=True)
        dv = xh - mu
        var = jnp.mean(dv * dv, axis=-1, keepdims=True)
        parts.append(dv * lax.rsqrt(var + RET_EPS))
    yr = retg_ref[0].astype(f32) * (jnp.concatenate(parts, axis=1) * gn_ref[...])
    y_ret = _bdot(yr, wbr_ref[...])
    W = RWKV_WIDTH
    gi = lax.broadcasted_iota(jnp.int32, (W, W), 0)
    gj = lax.broadcasted_iota(jnp.int32, (W, W), 1)
    ones_bd = ((gi >> 6) == (gj >> 6)).astype(bf16)
    o = rw0_ref[0, 0] + rw1_ref[0, 0]
    mu = _split_dot(o, ones_bd) * (1.0 / RWKV_HEAD_DIM)
    dv = o - mu
    var = _split_dot(dv * dv, ones_bd) * (1.0 / RWKV_HEAD_DIM)
    yw = dv * lax.rsqrt(var + RWKV_EPS) * lng_ref[...] + lnb_ref[...]
    gate = _dot(gd_ref[0], g2_ref[...])
    yw = (yw + b0_ref[0, 0] + b1_ref[0, 0]) * gate
    y_rw = _bdot(yw, wbw_ref[...])
    D = y_ret.shape[1]
    g = gate_ref[0].astype(f32)
    m = g[:, :D] * y_ret + g[:, D:] * y_rw
    y = _bdot(m, wout_ref[...])

    def rms(v, gg):
        return v * lax.rsqrt(jnp.mean(v * v, axis=-1, keepdims=True) + NORM_EPS) * gg

    x1 = x_ref[0] + mod_ref[0, 0:1, :] * rms(y, ng_ref[0:1, :])
    x1_ref[0] = x1
    h2 = rms(x1, ng_ref[1:2, :]) * (1.0 + mod_ref[0, 2:3, :]) + mod_ref[0, 1:2, :]
    h2_ref[0] = h2.astype(bf16)
    lt_ref[0] = lax.dot_general(wr_ref[...], h2, (((1,), (1,)), ((), ())),
                                preferred_element_type=f32, precision=lax.Precision.HIGHEST)


def _merge(x, ret_o, rw_o, bonus, z, mod2, ng12, gn, lng, lnb, g2, wbr, wbw, wout, wrt, ctx):
    B, T, D = x.shape
    tm = 256
    co = ctx // tm
    W = RWKV_WIDTH
    row = lambda w: pl.BlockSpec((1, tm, w), lambda b, i: (b, i, 0))
    dblk = lambda dd: pl.BlockSpec((1, 1, tm, W), lambda b, i: (dd, b, i, 0))
    zblk = lambda c0, w: pl.BlockSpec((1, tm, w), lambda b, i: (b, co + i, c0 // w))
    full = lambda a: pl.BlockSpec(a.shape, lambda b, i: (0,) * a.ndim)
    return pl.pallas_call(
        _merge_kernel,
        grid=(B, T // tm),
        in_specs=[row(D), row(W), dblk(0), dblk(1), dblk(0), dblk(1),
                  zblk(C_MERGE, 2 * D), zblk(C_RETG, W), zblk(C_LORA + 128, 128),
                  pl.BlockSpec((1, 3, D), lambda b, i: (b, 0, 0)),
                  full(ng12), full(gn), full(lng), full(lnb), full(g2), full(wbr), full(wbw), full(wout),
                  full(wrt)],
        out_specs=[row(D), row(D), pl.BlockSpec((1, N_EXPERTS, tm), lambda b, i: (b, 0, i))],
        out_shape=[jax.ShapeDtypeStruct((B, T, D), f32), jax.ShapeDtypeStruct((B, T, D), bf16),
                   jax.ShapeDtypeStruct((B, N_EXPERTS, T), f32)],
        compiler_params=_params(("arbitrary", "arbitrary")),
        name="merge",
    )(x, ret_o, rw_o, rw_o, bonus, bonus, z, z, z, mod2, ng12, gn, lng, lnb, g2, wbr, wbw, wout, wrt)


def _route_kernel(lt_ref, tri_ref, slot_ref, rt_ref, *, cap):
    lg = lt_ref[0]
    E, T = lg.shape
    mx = jnp.max(lg, axis=0, keepdims=True)
    ex = jnp.exp(lg - mx)
    aff = ex / jnp.sum(ex, axis=0, keepdims=True)

    def count_ge(cand):
        return jnp.sum((aff >= cand).astype(f32), axis=1, keepdims=True)

    def exp_step(_, kk):
        k_lo, k_hi = kk
        km = jnp.floor((k_lo + k_hi) * 0.5)
        ok = count_ge(jnp.exp2(-km)) >= cap
        return jnp.where(ok, k_lo, km), jnp.where(ok, km, k_hi)

    k_lo, k_hi = lax.fori_loop(0, ROUTE_EXP_STEPS, exp_step,
                               (jnp.full((E, 1), -1.0, f32), jnp.full((E, 1), ROUTE_MAX_EXP, f32)))
    lo0 = jnp.where(k_hi >= ROUTE_MAX_EXP, 0.0, jnp.exp2(-k_hi))
    hi0 = jnp.exp2(-k_lo)

    def val_step(_, lh):
        lo, hi = lh
        mid = (lo + hi) * 0.5
        ok = count_ge(mid) >= cap
        return jnp.where(ok, mid, lo), jnp.where(ok, hi, mid)

    lo, hi = lax.fori_loop(0, ROUTE_VAL_STEPS, val_step, (lo0, hi0))
    gt = aff >= hi
    eq = (aff >= lo) & (aff < hi)
    need = cap - jnp.sum(gt.astype(f32), axis=1, keepdims=True)
    tri = tri_ref[...]
    eq_before = _dot(eq.astype(bf16), tri)
    sel = gt | (eq & (eq_before < need))
    slot = _dot(sel.astype(bf16), tri)
    slot_f = jnp.where(sel, slot, -1.0)
    slot_ref[0] = slot_f.astype(jnp.int32)
    gate = jnp.where(sel, aff, 0.0)
    packed = jnp.concatenate([slot_f, gate, jnp.zeros((128 - 2 * E, T), f32)], axis=0)
    rt_ref[0] = packed.T


def _route(lt, tri, cap):
    B, E, T = lt.shape
    return pl.pallas_call(
        functools.partial(_route_kernel, cap=cap),
        grid=(B,),
        in_specs=[pl.BlockSpec((1, E, T), lambda b: (b, 0, 0)),
                  pl.BlockSpec((T, T), lambda b: (0, 0))],
        out_specs=[pl.BlockSpec((1, E, T), lambda b: (b, 0, 0)),
                   pl.BlockSpec((1, T, 128), lambda b: (b, 0, 0))],
        out_shape=[jax.ShapeDtypeStruct((B, E, T), jnp.int32), jax.ShapeDtypeStruct((B, T, 128), f32)],
        compiler_params=_params(("arbitrary",)),
        name="route",
    )(lt, tri)


def _ffn_kernel(slot_ref, h_ref, wg_ref, wu_ref, wd_ref, o_ref, *, cap):
    T = h_ref.shape[1]
    ci = lax.broadcasted_iota(jnp.int32, (cap, T), 0)
    onehot = (slot_ref[0, 0] == ci).astype(bf16)
    xg = _dot(onehot, h_ref[0]).astype(bf16)
    hg = _dot(xg, wg_ref[0])
    hu = _dot(xg, wu_ref[0])
    hid = (hg * jax.nn.sigmoid(hg) * hu).astype(bf16)
    o_ref[0, 0] = _dot(hid, wd_ref[0]).astype(bf16)


def _ffn(slot4, h2, wg, wu, wd, cap):
    B, T, D = h2.shape
    E, _, F = wg.shape
    return pl.pallas_call(
        functools.partial(_ffn_kernel, cap=cap),
        grid=(E, B),
        in_specs=[pl.BlockSpec((1, 1, 1, T), lambda e, b: (b, e, 0, 0)),
                  pl.BlockSpec((1, T, D), lambda e, b: (b, 0, 0)),
                  pl.BlockSpec((1, D, F), lambda e, b: (e, 0, 0)),
                  pl.BlockSpec((1, D, F), lambda e, b: (e, 0, 0)),
                  pl.BlockSpec((1, F, D), lambda e, b: (e, 0, 0))],
        out_specs=pl.BlockSpec((1, 1, cap, D), lambda e, b: (b, e, 0, 0)),
        out_shape=jax.ShapeDtypeStruct((B, E, cap, D), bf16),
        compiler_params=_params(("arbitrary", "arbitrary")),
        name="ffn",
    )(slot4, h2, wg, wu, wd)


def _combine_kernel(rt_ref, eo_ref, x1_ref, mod_ref, ng_ref, o_ref, *, cap):
    rt = rt_ref[0]
    tm = rt.shape[0]
    E = eo_ref.shape[1]
    ci = lax.broadcasted_iota(jnp.int32, (tm, cap), 1).astype(f32)
    y = jnp.zeros((tm, eo_ref.shape[3]), f32)
    for e in range(E):
        p = jnp.where(rt[:, e:e + 1] == ci, rt[:, E + e:E + e + 1], 0.0).astype(bf16)
        y = y + _dot(p, eo_ref[0, e])
    yn = y * lax.rsqrt(jnp.mean(y * y, axis=-1, keepdims=True) + NORM_EPS) * ng_ref[...]
    o_ref[0] = x1_ref[0] + mod_ref[0] * yn


def _combine(rt, eo, x1, g2mod, ng3, cap):
    B, T, D = x1.shape
    E = eo.shape[1]
    tm = 512 if T % 512 == 0 else T
    return pl.pallas_call(
        functools.partial(_combine_kernel, cap=cap),
        grid=(B, T // tm),
        in_specs=[pl.BlockSpec((1, tm, 128), lambda b, i: (b, i, 0)),
                  pl.BlockSpec((1, E, cap, D), lambda b, i: (b, 0, 0, 0)),
                  pl.BlockSpec((1, tm, D), lambda b, i: (b, i, 0)),
                  pl.BlockSpec((1, 1, D), lambda b, i: (b, 0, 0)),
                  pl.BlockSpec((1, D), lambda b, i: (0, 0))],
        out_specs=pl.BlockSpec((1, tm, D), lambda b, i: (b, i, 0)),
        out_shape=jax.ShapeDtypeStruct((B, T, D), f32),
        compiler_params=_params(("arbitrary", "arbitrary")),
        name="combine",
    )(rt, eo, x1, g2mod, ng3)


def _column_perm():
    sk, sv, rk, rv, wd, ad = 0, 512, 1024, 1536, 2048, 2112
    q0 = 2176
    rq, rg, rr, gd, mg = q0, q0 + 512, q0 + 1024, q0 + 1536, q0 + 1664
    rng = lambda a, n: np.arange(a, a + n)
    return np.concatenate([rng(mg, 2048), rng(sk, 512), rng(sv, 512), rng(rq, 512), rng(rg, 512),
                           rng(rk, 512), rng(rv, 512), rng(rr, 512), rng(wd, 64), rng(ad, 64), rng(gd, 128)])


def _rope_tables(T):
    t = jnp.arange(T)
    nfreq = RET_HEAD_DIM // 4
    inv = ROPE_BASE ** (-jnp.arange(nfreq, dtype=f32) / nfreq)
    ang = jnp.concatenate([(t // GRID_W).astype(f32)[:, None] * inv,
                           (t % GRID_W).astype(f32)[:, None] * inv], axis=-1)
    cos, sin = jnp.cos(ang), jnp.sin(ang)
    return jnp.concatenate([cos, cos], axis=1), jnp.concatenate([-sin, sin], axis=1)


def kernel(x, c, ctx, c_ctx, w_mod, b_mod, norm_g, w_in, ret_log_decay, ret_gn_g, rwkv_mu, rwkv_k_k, rwkv_k_a,
           rwkv_r_k, rwkv_w0, rwkv_w2, rwkv_a0, rwkv_a2, rwkv_g2, rwkv_ln_g, rwkv_ln_b, w_br_ret, w_br_rwkv,
           w_out, w_router, w_gate, w_up, w_down):
    B, T, D = x.shape
    CT = ctx.shape[1]
    assert w_mod.shape[0] == 1 and D == D_MODEL
    assert CT % ROWS == 0 and T % ROWS == 0 and T % GRID_W == 0
    cap = CAPACITY_FACTOR * T // N_EXPERTS
    assert cap % 8 == 0

    mrows = -(-(B + 1) // 8) * 8
    cc = jnp.zeros((mrows, D), f32).at[:B].set(c).at[B].set(c_ctx)
    mod = _modulation(cc, w_mod[0], b_mod[0])
    lat = mod[:B].reshape(B, N_MOD, D)
    cm = jnp.broadcast_to(mod[B].reshape(1, N_MOD, D), (B, N_MOD, D))
    modrows = jnp.concatenate([lat[:, 0:2], cm[:, 0:2]], axis=1)

    perm = _column_perm()
    w_perm = w_in[0][:, perm].astype(bf16)
    mu = rwkv_mu[0]
    ss = 2 * RWKV_WIDTH + DECAY_LORA + ICLR_LORA
    mu_full = jnp.zeros((2, IN_COLS), f32)
    mu_full = mu_full.at[:, C_RWK:C_RWK + 1024].set(mu[:, 0:1024])
    mu_full = mu_full.at[:, C_RWR:C_RWR + 512].set(mu[:, ss:ss + 512])
    mu_full = mu_full.at[:, C_LORA:C_LORA + 128].set(mu[:, 1024:ss])
    mu_full = mu_full.at[:, C_LORA + 128:].set(mu[:, ss + 512:])
    cosf, sinf = _rope_tables(T)
    xall = jnp.concatenate([ctx, x], axis=1)
    z = _inproj(xall, modrows, norm_g[0, 0:1], w_perm, mu_full, cosf, sinf, CT)

    lg = -jnp.exp(ret_log_decay[0].astype(f32))
    ret_o = _retention(lg, z, CT)

    G = 2 * RWKV_HEAD_DIM
    w2p = jnp.zeros((2, G, RWKV_WIDTH), f32).at[:, :DECAY_LORA].set(rwkv_w2[0]).astype(bf16)
    a2p = jnp.zeros((2, G, RWKV_WIDTH), f32).at[:, DECAY_LORA:].set(rwkv_a2[0]).astype(bf16)
    rw_o, bonus = _rwkv(z, rwkv_w0[0][:, None, :], rwkv_a0[0][:, None, :], w2p, a2p,
                        rwkv_k_k[0][None], rwkv_k_a[0][None], rwkv_r_k[0][None], CT)

    mod2 = jnp.stack([lat[:, 2], lat[:, 3], lat[:, 4]], axis=1)
    x1, h2, lt = _merge(x, ret_o, rw_o, bonus, z, mod2, norm_g[0, 1:3], ret_gn_g[0][None], rwkv_ln_g[0][None],
                        rwkv_ln_b[0][None], rwkv_g2[0].astype(bf16), w_br_ret[0].astype(bf16),
                        w_br_rwkv[0].astype(bf16), w_out[0].astype(bf16), w_router[0].T, CT)

    ti = jnp.arange(T)
    tri = (ti[:, None] < ti[None, :]).astype(bf16)
    slot, rt = _route(lt, tri, cap)
    eo = _ffn(slot.reshape(B, N_EXPERTS, 1, T), h2, w_gate[0].astype(bf16), w_up[0].astype(bf16),
              w_down[0].astype(bf16), cap)
    return _combine(rt, eo, x1, lat[:, 5:6], norm_g[0, 3:4], cap)
```

```python
import functools

import numpy as np
import jax
import jax.numpy as jnp
from jax import lax
from jax.experimental import pallas as pl
from jax.experimental.pallas import tpu as pltpu

f32 = jnp.float32
bf16 = jnp.bfloat16

D_MODEL = 1024
GRID_W = 64
RET_HEAD_DIM = 128
RET_WIDTH = 512
RET_HEADS = 4
RET_CHUNK = 128
RET_EPS = 1e-5
ROPE_BASE = 10000.0
RWKV_HEAD_DIM = 64
RWKV_WIDTH = 512
RWKV_PAIRS = 4
RWKV_CHUNK = 64
RWKV_SAMPLES_PER_STEP = 4
DECAY_LORA = 64
ICLR_LORA = 64
GATE_LORA = 128
RWKV_EPS = 64e-5
N_EXPERTS = 16
EXPERT_FF = 1024
CAPACITY_FACTOR = 2
N_MOD = 6
NORM_EPS = 1e-6

C_MERGE, C_RETK, C_RETV, C_RETQ, C_RETG = 0, 2048, 2560, 3072, 3584
C_RWK, C_RWV, C_RWR, C_LORA = 4096, 4608, 5120, 5632
IN_COLS = 5888
TN = 256
ROWS = 256
MERGE_ROWS = 512

ROUTE_MAX_EXP = 126.0
ROUTE_EXP_STEPS = 8
ROUTE_VAL_STEPS = 26

VMEM_LIMIT = 56 * 1024 * 1024


def _dot(a, b):
    return jnp.dot(a, b, preferred_element_type=f32)


def _dot_nt(a, b):
    return lax.dot_general(a, b, (((1,), (1,)), ((), ())), preferred_element_type=f32)


def _dot_tn(a, b):
    return lax.dot_general(a, b, (((0,), (0,)), ((), ())), preferred_element_type=f32)


def _bdot(a, b):
    return _dot(a.astype(bf16), b.astype(bf16))


def _split_dot(x, w):
    hi = x.astype(bf16)
    lo = (x - hi.astype(f32)).astype(bf16)
    return _dot(hi, w) + _dot(lo, w)


def _params(sem, limit=VMEM_LIMIT):
    return pltpu.CompilerParams(dimension_semantics=sem, vmem_limit_bytes=limit)


def _mod_kernel(c_ref, w_ref, b_ref, o_ref):
    c = c_ref[...]
    s = c * jax.nn.sigmoid(c)
    o_ref[...] = _bdot(s, w_ref[...]) + b_ref[...]


def _modulation(cc, w_mod, b_mod):
    m, d = cc.shape
    n = w_mod.shape[1]
    tn = 512
    return pl.pallas_call(
        _mod_kernel,
        grid=(n // tn,),
        in_specs=[pl.BlockSpec((m, d), lambda j: (0, 0)),
                  pl.BlockSpec((d, tn), lambda j: (0, j)),
                  pl.BlockSpec((1, tn), lambda j: (0, j))],
        out_specs=pl.BlockSpec((m, tn), lambda j: (0, j)),
        out_shape=jax.ShapeDtypeStruct((m, n), f32),
        compiler_params=_params(("arbitrary",)),
        name="mod",
    )(cc, w_mod, b_mod.reshape(1, n))


def _inproj_kernel(x_ref, c_ref, mod_ref, g_ref, w_ref, mu_ref, cos_ref, sin_ref, o_ref, h_ref, z0_ref, z1_ref):
    n = pl.program_id(1)
    nt = pl.num_programs(1) - 1
    T = x_ref.shape[1]
    L = T + c_ref.shape[1]
    nchunk = L // ROWS
    PAD = 8

    @pl.when(n == 0)
    def _():
        for zb in (z0_ref, z1_ref):
            zb[0:PAD, :] = jnp.zeros((PAD, TN), f32)
            zb[PAD + L:PAD + L + PAD, :] = jnp.zeros((PAD, TN), f32)
        for i in range(nchunk):
            r0 = i * ROWS
            xb = x_ref[0, r0:r0 + ROWS, :] if r0 < T else c_ref[0, r0 - T:r0 - T + ROWS, :]
            y = xb * lax.rsqrt(jnp.mean(xb * xb, axis=-1, keepdims=True) + NORM_EPS) * g_ref[...]
            o = 0 if r0 < T else 2
            sh = mod_ref[0, o:o + 1, :]
            sc = mod_ref[0, o + 1:o + 2, :]
            h_ref[r0:r0 + ROWS, :] = (y * (1.0 + sc) + sh).astype(bf16)

    def rope(z, r0, scale):
        if scale != 1.0:
            z = z * scale
        if r0 >= T:
            return z
        cs = cos_ref[r0:r0 + ROWS, :]
        sn = sin_ref[r0:r0 + ROWS, :]
        parts = []
        for hh in range(TN // RET_HEAD_DIM):
            zh = z[:, hh * RET_HEAD_DIM:(hh + 1) * RET_HEAD_DIM]
            parts.append(zh * cs + pltpu.roll(zh, RET_HEAD_DIM // 2, 1) * sn)
        return jnp.concatenate(parts, axis=1)

    def shift(z, r0, zsrc):
        prev = zsrc[PAD + r0 - 1:PAD + r0 - 1 + ROWS, :]
        nxt = zsrc[PAD + r0 + 1:PAD + r0 + 1 + ROWS, :]
        rid = r0 + lax.broadcasted_iota(jnp.int32, (ROWS, 1), 0)
        prev = jnp.where(rid == T, 0.0, prev)
        nxt = jnp.where(rid == T - 1, 0.0, nxt)
        return z + mu_ref[0:1, :] * (prev - z) + mu_ref[1:2, :] * (nxt - z)

    def lora_act(z, r0, zsrc):
        zs = shift(z, r0, zsrc)
        lane = lax.broadcasted_iota(jnp.int32, (1, TN), 1)
        return jnp.where(lane < DECAY_LORA, jnp.tanh(zs),
                         jnp.where(lane < DECAY_LORA + ICLR_LORA, zs, jax.nn.sigmoid(zs)))

    def step(post, matmul, zdst, zsrc):
        for i in range(nchunk):
            r0 = i * ROWS
            if matmul:
                zdst[PAD + r0:PAD + r0 + ROWS, :] = _dot(h_ref[r0:r0 + ROWS, :], w_ref[...])
            if post is not None:
                o_ref[0, r0:r0 + ROWS, :] = post(zsrc[PAD + r0:PAD + r0 + ROWS, :], r0, zsrc).astype(bf16)

    def run(cond, post, matmul):
        for par, (zdst, zsrc) in enumerate(((z0_ref, z1_ref), (z1_ref, z0_ref))):
            pl.when(cond & (n % 2 == par))(functools.partial(step, post, matmul, zdst, zsrc))

    t = lambda c: c // TN
    m = n - 1
    run(n == 0, None, True)
    run((m >= 0) & (m < t(C_RETK)), lambda z, r0, zs: jax.nn.sigmoid(z), True)
    run((m >= t(C_RETK)) & (m < t(C_RETV)), lambda z, r0, zs: rope(z, r0, RET_HEAD_DIM ** -0.5), True)
    run((m >= t(C_RETV)) & (m < t(C_RETQ)), lambda z, r0, zs: z, True)
    run((m >= t(C_RETQ)) & (m < t(C_RETG)), lambda z, r0, zs: rope(z, r0, 1.0), True)
    run((m >= t(C_RETG)) & (m < t(C_RWK)), lambda z, r0, zs: z * jax.nn.sigmoid(z), True)
    run((m >= t(C_RWK)) & (m < t(C_LORA)), shift, True)
    run((m >= t(C_LORA)) & (n < nt), lora_act, True)
    run((m >= t(C_LORA)) & (n == nt), lora_act, False)


def _inproj(x, ctx, modrows, g0, w_perm, mu_full, cosf, sinf):
    B, T, D = x.shape
    CT = ctx.shape[1]
    L = T + CT
    nt = IN_COLS // TN
    mm = lambda n: jnp.minimum(n, nt - 1)
    pp = lambda n: jnp.maximum(n - 1, 0)
    return pl.pallas_call(
        _inproj_kernel,
        grid=(B, nt + 1),
        in_specs=[pl.BlockSpec((1, T, D), lambda b, n: (b, 0, 0)),
                  pl.BlockSpec((1, CT, D), lambda b, n: (b, 0, 0)),
                  pl.BlockSpec((1, 4, D), lambda b, n: (b, 0, 0)),
                  pl.BlockSpec((1, D), lambda b, n: (0, 0)),
                  pl.BlockSpec((D, TN), lambda b, n: (0, mm(n))),
                  pl.BlockSpec((2, TN), lambda b, n: (0, pp(n))),
                  pl.BlockSpec((T, RET_HEAD_DIM), lambda b, n: (0, 0)),
                  pl.BlockSpec((T, RET_HEAD_DIM), lambda b, n: (0, 0))],
        out_specs=pl.BlockSpec((1, L, TN), lambda b, n: (b, 0, pp(n))),
        out_shape=jax.ShapeDtypeStruct((B, L, IN_COLS), bf16),
        scratch_shapes=[pltpu.VMEM((L, D), bf16), pltpu.VMEM((L + 16, TN), f32), pltpu.VMEM((L + 16, TN), f32)],
        compiler_params=_params(("arbitrary", "arbitrary")),
        name="inproj",
    )(x, ctx, modrows, g0, w_perm, mu_full, cosf, sinf)


def _ret_kernel(lg_ref, q_ref, k_ref, v_ref, o_ref, r_ref, tab_ref, *, ctx):
    L = q_ref.shape[1]
    Cc = RET_CHUNK
    hd = RET_HEAD_DIM
    lat = L - ctx
    nc = ctx // Cc
    nl = lat // Cc
    combos = [(h, d) for h in range(RET_HEADS) for d in (0, 1)]
    INTRA, CROSS, TAIL, DECAY = 0, 1, 2, 3

    @pl.when(pl.program_id(0) == 0)
    def _():
        ii = lax.broadcasted_iota(jnp.int32, (Cc, Cc), 0).astype(f32)
        jj = lax.broadcasted_iota(jnp.int32, (Cc, Cc), 1).astype(f32)
        for idx, (h, d) in enumerate(combos):
            lg = lg_ref[d, h]
            if d == 0:
                diff = ii - jj
                cross = jnp.exp(lg * (ii + 1.0))
                tailw = jnp.exp(lg * (Cc - 1.0 - ii))
            else:
                diff = jj - ii
                cross = jnp.exp(lg * (Cc - ii))
                tailw = jnp.exp(lg * ii)
            tab_ref[idx, INTRA] = jnp.where(diff >= 0, jnp.exp(lg * jnp.maximum(diff, 0.0)), 0.0)
            tab_ref[idx, CROSS] = cross
            tab_ref[idx, TAIL] = tailw
            tab_ref[idx, DECAY] = jnp.exp(jnp.zeros((Cc, Cc), f32) + lg * Cc)

    r_ref[...] = jnp.zeros(r_ref.shape, f32)
    o_ref[...] = jnp.zeros(o_ref.shape, f32)

    def cols(h):
        return slice(h * hd, (h + 1) * hd)

    def update(idx, kc, vc):
        ks = (kc.astype(f32) * tab_ref[idx, TAIL]).astype(bf16)
        r_ref[idx] = r_ref[idx] * tab_ref[idx, DECAY] + _dot_tn(ks, vc)

    def ctx_step(s, carry):
        for idx, (h, d) in enumerate(combos):
            row0 = pl.multiple_of(lat + (s * Cc if d == 0 else (nc - 1 - s) * Cc), Cc)
            update(idx, k_ref[0, pl.ds(row0, Cc), cols(h)], v_ref[0, pl.ds(row0, Cc), cols(h)])
        return carry

    def lat_step(s, carry):
        t0s, qs, ks, vs = [], [], [], []
        for h, d in combos:
            t0 = pl.multiple_of(s * Cc if d == 0 else (nl - 1 - s) * Cc, Cc)
            row0 = t0
            t0s.append(t0)
            qs.append(q_ref[0, pl.ds(row0, Cc), cols(h)])
            ks.append(k_ref[0, pl.ds(row0, Cc), cols(h)])
            vs.append(v_ref[0, pl.ds(row0, Cc), cols(h)])
        n = len(combos)
        sc = [(_dot_nt(qs[i], ks[i]) * tab_ref[i, INTRA]).astype(bf16) for i in range(n)]
        oc = [_dot(qs[i], r_ref[i].astype(bf16)) * tab_ref[i, CROSS] for i in range(n)]
        oi = [_dot(sc[i], vs[i]) for i in range(n)]
        for i, (h, d) in enumerate(combos):
            o_ref[0, pl.ds(t0s[i], Cc), cols(h)] += oi[i] + oc[i]
        for i in range(n):
            update(i, ks[i], vs[i])
        return carry

    lax.fori_loop(0, nc, ctx_step, 0)
    lax.fori_loop(0, nl, lat_step, 0)


def _retention(lg, z, ctx):
    B, L, _ = z.shape
    T = L - ctx
    W = RET_WIDTH
    Cc = RET_CHUNK
    blk = lambda c0: pl.BlockSpec((1, L, W), lambda b: (b, 0, c0 // W))
    nchain = 2 * RET_HEADS
    return pl.pallas_call(
        functools.partial(_ret_kernel, ctx=ctx),
        grid=(B,),
        in_specs=[pl.BlockSpec(memory_space=pltpu.SMEM), blk(C_RETQ), blk(C_RETK), blk(C_RETV)],
        out_specs=pl.BlockSpec((1, T, W), lambda b: (b, 0, 0)),
        out_shape=jax.ShapeDtypeStruct((B, T, W), f32),
        scratch_shapes=[pltpu.VMEM((nchain, Cc, Cc), f32), pltpu.VMEM((nchain, 4, Cc, Cc), f32)],
        compiler_params=_params(("arbitrary",)),
        name="ret",
    )(lg, z, z, z)


def _rwkv_kernel(k_ref, v_ref, r_ref, lo_ref, w0_ref, a0_ref, w2_ref, a2_ref, kk_ref, ka_ref, rk_ref,
                 o_ref, bo_ref, s_ref, *, nc, nb):
    d = pl.program_id(1)
    s = pl.program_id(2)
    C = RWKV_CHUNK
    G = 2 * RWKV_HEAD_DIM
    rev = d == 1

    @pl.when(s == 0)
    def _():
        s_ref[...] = jnp.zeros(s_ref.shape, f32)

    ii = lax.broadcasted_iota(jnp.int32, (C, G), 0)
    lane = lax.broadcasted_iota(jnp.int32, (C, G), 1)
    jj = lane & (RWKV_HEAD_DIM - 1)
    head0 = lane < RWKV_HEAD_DIM
    dlt = jnp.where(rev, ii - jj, jj - ii)
    strict = dlt < 0
    incl = dlt <= 0
    eye = (ii == jj).astype(f32)
    gi = lax.broadcasted_iota(jnp.int32, (G, G), 0)
    gj = lax.broadcasted_iota(jnp.int32, (G, G), 1)
    blockdiag = (gi >= RWKV_HEAD_DIM) == (gj >= RWKV_HEAD_DIM)
    ones_bd = blockdiag.astype(bf16)
    tri = incl[:, 0:C].astype(bf16)
    base = (ii >> 1) == (jj >> 1)
    offs = [((ii >> (lv + 1)) == (jj >> (lv + 1))) & ((ii >> lv) != (jj >> lv)) for lv in range(1, 6)]

    def segsum(x):
        return jnp.concatenate(
            [_bdot(x[:, p * G:(p + 1) * G], ones_bd) for p in range(RWKV_PAIRS)], axis=1)

    def stack(x):
        xb = x.astype(bf16)
        zero = jnp.zeros_like(xb)
        return jnp.concatenate([jnp.where(head0, xb, zero), jnp.where(head0, zero, xb)], axis=0)

    pre = []
    for bb in range(nb):
        kx = k_ref[bb].astype(f32)
        vx = v_ref[bb].astype(f32)
        rx = r_ref[bb].astype(f32)
        lo = lo_ref[bb, :, 0:DECAY_LORA + ICLR_LORA]
        u = w0_ref[0] + _dot(lo, w2_ref[0])
        softplus = jnp.maximum(-u, 0.0) + jnp.log1p(jnp.exp(-jnp.abs(u)))
        lw = -jnp.exp(-softplus - 0.5)
        a = jax.nn.sigmoid(a0_ref[0] + _dot(lo, a2_ref[0]))
        kkr = kx * kk_ref[...]
        kk = kkr * lax.rsqrt(segsum(kkr * kkr) + 1e-12)
        kd = kx * (1.0 + (a - 1.0) * ka_ref[...])
        be = kk * a
        bonus = segsum(rx * kd * rk_ref[...]) * vx
        cum = _split_dot_left(tri, lw)
        tot = jnp.where(rev, cum[0:1, :], cum[C - 1:C, :])
        gneg = jnp.exp(-cum)
        gh = jnp.exp(tot - cum)
        pre.append(dict(alb=-kk * jnp.exp(cum - lw), rb=rx * jnp.exp(cum), beb=be * gneg, kb=kd * gneg,
                        beh=be * gh, kh=kd * gh, etot=jnp.exp(tot), v=vx, bonus=bonus))

    units = [(bb, p) for bb in range(nb) for p in range(RWKV_PAIRS)]
    sl = lambda p: slice(p * G, (p + 1) * G)
    part = lambda un, name: pre[un[0]][name][:, sl(un[1])]
    S = {un: s_ref[un[0] * RWKV_PAIRS + un[1]] for un in units}
    Sb = {un: S[un].astype(bf16) for un in units}
    X = {un: part(un, "alb").astype(bf16) for un in units}
    Rb = {un: part(un, "rb").astype(bf16) for un in units}
    Ybs = {un: stack(part(un, "beb")) for un in units}
    Yks = {un: stack(part(un, "kb")) for un in units}
    Vb = {un: part(un, "v").astype(bf16) for un in units}
    Vs = {un: stack(part(un, "v")) for un in units}

    XR = {un: jnp.concatenate([X[un], Rb[un]], axis=0) for un in units}
    APb = {un: _dot_nt(XR[un], Ybs[un]) for un in units}
    APk = {un: _dot_nt(XR[un], Yks[un]) for un in units}
    BO = {un: _dot_nt(XR[un], Sb[un]) for un in units}
    Aab = {un: jnp.where(strict, APb[un][0:C], 0.0) for un in units}
    Aak = {un: jnp.where(strict, APk[un][0:C], 0.0).astype(bf16) for un in units}
    Pab = {un: jnp.where(incl, APb[un][C:2 * C], 0.0).astype(bf16) for un in units}
    Pak = {un: jnp.where(incl, APk[un][C:2 * C], 0.0).astype(bf16) for un in units}

    Tm = {un: eye + jnp.where(base, Aab[un], 0.0) for un in units}
    for off in offs:
        Xs = {un: _dot(jnp.where(off, Aab[un], 0.0).astype(bf16), stack(Tm[un])) for un in units}
        Tm = {un: Tm[un] + _dot(Tm[un].astype(bf16), stack(Xs[un])) for un in units}

    Bm = {un: BO[un][0:C] + _dot(Aak[un], Vs[un]) for un in units}
    U = {un: _dot(Tm[un].astype(bf16), stack(Bm[un])) for un in units}
    Om = {un: BO[un][C:2 * C]
          + _dot(jnp.concatenate([Pab[un], Pak[un]], axis=1), jnp.concatenate([stack(U[un]), Vs[un]], axis=0))
          for un in units}
    for un in units:
        bb, p = un
        upd = _dot_tn(jnp.concatenate([U[un].astype(bf16), Vb[un]], axis=0),
                      jnp.concatenate([part(un, "beh").astype(bf16), part(un, "kh").astype(bf16)], axis=0))
        s_ref[bb * RWKV_PAIRS + p] = jnp.where(blockdiag, S[un] * part(un, "etot") + upd, 0.0)

    @pl.when(s >= nc)
    def _():
        for bb in range(nb):
            o_ref[0, bb] = jnp.concatenate([Om[bb, p] for p in range(RWKV_PAIRS)], axis=1)
            bo_ref[0, bb] = pre[bb]["bonus"]


def _split_dot_left(w, x):
    hi = x.astype(bf16)
    lo = (x - hi.astype(f32)).astype(bf16)
    return _dot(w, hi) + _dot(w, lo)


def _rwkv(z, w0, a0, w2p, a2p, k_k, k_a, r_k, ctx):
    B, L, _ = z.shape
    T = L - ctx
    C = RWKV_CHUNK
    nc, nl = ctx // C, T // C
    W = RWKV_WIDTH

    def chunk(d, s):
        fwd = jnp.where(s < nc, nl + s, s - nc)
        bwd = jnp.where(s < nc, nl + nc - 1 - s, nl - 1 - (s - nc))
        return jnp.where(d == 0, fwd, bwd)

    def ochunk(d, s):
        sl = jnp.maximum(s - nc, 0)
        return jnp.where(d == 0, sl, nl - 1 - sl)

    nb = RWKV_SAMPLES_PER_STEP if B % RWKV_SAMPLES_PER_STEP == 0 else 1
    zblk = lambda c0, w: pl.BlockSpec((nb, C, w), lambda b, d, s: (b, chunk(d, s), c0 // w))
    dpar = lambda r: pl.BlockSpec((1, r, W), lambda b, d, s: (d, 0, 0))
    par = pl.BlockSpec((1, W), lambda b, d, s: (0, 0))
    oblk = pl.BlockSpec((1, nb, C, W), lambda b, d, s: (d, b, ochunk(d, s), 0))
    G = 2 * RWKV_HEAD_DIM
    return pl.pallas_call(
        functools.partial(_rwkv_kernel, nc=nc, nb=nb),
        grid=(B // nb, 2, nc + nl),
        in_specs=[zblk(C_RWK, W), zblk(C_RWV, W), zblk(C_RWR, W), zblk(C_LORA, 256),
                  dpar(1), dpar(1), dpar(G), dpar(G), par, par, par],
        out_specs=[oblk, oblk],
        out_shape=[jax.ShapeDtypeStruct((2, B, T, W), f32), jax.ShapeDtypeStruct((2, B, T, W), f32)],
        scratch_shapes=[pltpu.VMEM((nb * RWKV_PAIRS, G, G), f32)],
        compiler_params=_params(("arbitrary", "arbitrary", "arbitrary")),
        name="rwkv",
    )(z, z, z, z, w0, a0, w2p, a2p, k_k, k_a, r_k)


def _merge_kernel(x_ref, ret_ref, rw0_ref, rw1_ref, b0_ref, b1_ref, gate_ref, retg_ref, gd_ref, mod_ref,
                  ng_ref, gn_ref, lng_ref, lnb_ref, g2_ref, wbr_ref, wbw_ref, wout_ref, wrh_ref, wrl_ref,
                  x1_ref, h2_ref, lt_ref):
    hd = RET_HEAD_DIM
    ret = ret_ref[0]
    parts = []
    for hh in range(RET_HEADS):
        xh = ret[:, hh * hd:(hh + 1) * hd]
        mu = jnp.mean(xh, axis=-1, keepdims=True)
        dv = xh - mu
        var = jnp.mean(dv * dv, axis=-1, keepdims=True)
        parts.append(dv * lax.rsqrt(var + RET_EPS))
    yr = retg_ref[0].astype(f32) * (jnp.concatenate(parts, axis=1) * gn_ref[...])
    y_ret = _bdot(yr, wbr_ref[...])
    W = RWKV_WIDTH
    gi = lax.broadcasted_iota(jnp.int32, (W, W), 0)
    gj = lax.broadcasted_iota(jnp.int32, (W, W), 1)
    ones_bd = ((gi >> 6) == (gj >> 6)).astype(bf16)
    o = rw0_ref[0, 0] + rw1_ref[0, 0]
    mu = _split_dot(o, ones_bd) * (1.0 / RWKV_HEAD_DIM)
    dv = o - mu
    var = _bdot(dv * dv, ones_bd) * (1.0 / RWKV_HEAD_DIM)
    yw = dv * lax.rsqrt(var + RWKV_EPS) * lng_ref[...] + lnb_ref[...]
    gate = _dot(gd_ref[0], g2_ref[...])
    yw = (yw + b0_ref[0, 0] + b1_ref[0, 0]) * gate
    y_rw = _bdot(yw, wbw_ref[...])
    D = y_ret.shape[1]
    g = gate_ref[0].astype(f32)
    m = g[:, :D] * y_ret + g[:, D:] * y_rw
    y = _bdot(m, wout_ref[...])

    def rms(v, gg):
        return v * lax.rsqrt(jnp.mean(v * v, axis=-1, keepdims=True) + NORM_EPS) * gg

    x1 = x_ref[0] + mod_ref[0, 0:1, :] * rms(y, ng_ref[0:1, :])
    x1_ref[0] = x1
    h2 = rms(x1, ng_ref[1:2, :]) * (1.0 + mod_ref[0, 2:3, :]) + mod_ref[0, 1:2, :]
    h2b = h2.astype(bf16)
    h2_ref[0] = h2b
    h2l = (h2 - h2b.astype(f32)).astype(bf16)
    lgt = _dot(h2b, wrh_ref[...]) + _dot(h2l, wrh_ref[...]) + _dot(h2b, wrl_ref[...])
    lt_ref[0] = lgt.T[0:N_EXPERTS, :]


def _merge(x, ret_o, rw_o, bonus, z, mod2, ng12, gn, lng, lnb, g2, wbr, wbw, wout, wrh, wrl, ctx):
    B, T, D = x.shape
    tm = MERGE_ROWS if T % MERGE_ROWS == 0 else ROWS
    co = 0
    W = RWKV_WIDTH
    row = lambda w: pl.BlockSpec((1, tm, w), lambda b, i: (b, i, 0))
    dblk = lambda dd: pl.BlockSpec((1, 1, tm, W), lambda b, i: (dd, b, i, 0))
    zblk = lambda c0, w: pl.BlockSpec((1, tm, w), lambda b, i: (b, co + i, c0 // w))
    full = lambda a: pl.BlockSpec(a.shape, lambda b, i: (0,) * a.ndim)
    return pl.pallas_call(
        _merge_kernel,
        grid=(B, T // tm),
        in_specs=[row(D), row(W), dblk(0), dblk(1), dblk(0), dblk(1),
                  zblk(C_MERGE, 2 * D), zblk(C_RETG, W), zblk(C_LORA + 128, 128),
                  pl.BlockSpec((1, 3, D), lambda b, i: (b, 0, 0)),
                  full(ng12), full(gn), full(lng), full(lnb), full(g2), full(wbr), full(wbw), full(wout),
                  full(wrh), full(wrl)],
        out_specs=[row(D), row(D), pl.BlockSpec((1, N_EXPERTS, tm), lambda b, i: (b, 0, i))],
        out_shape=[jax.ShapeDtypeStruct((B, T, D), f32), jax.ShapeDtypeStruct((B, T, D), bf16),
                   jax.ShapeDtypeStruct((B, N_EXPERTS, T), f32)],
        compiler_params=_params(("arbitrary", "arbitrary")),
        name="merge",
    )(x, ret_o, rw_o, rw_o, bonus, bonus, z, z, z, mod2, ng12, gn, lng, lnb, g2, wbr, wbw, wout, wrh, wrl)


def _route_kernel(lt_ref, tri_ref, slot_ref, rt_ref, *, cap):
    lg = lt_ref[0]
    E, T = lg.shape
    mx = jnp.max(lg, axis=0, keepdims=True)
    ex = jnp.exp(lg - mx)
    aff = ex / jnp.sum(ex, axis=0, keepdims=True)

    def count_ge(cand):
        return jnp.sum((aff >= cand).astype(f32), axis=1, keepdims=True)

    def exp_step(_, kk):
        k_lo, k_hi = kk
        km = jnp.floor((k_lo + k_hi) * 0.5)
        ok = count_ge(jnp.exp2(-km)) >= cap
        return jnp.where(ok, k_lo, km), jnp.where(ok, km, k_hi)

    k_lo, k_hi = lax.fori_loop(0, ROUTE_EXP_STEPS, exp_step,
                               (jnp.full((E, 1), -1.0, f32), jnp.full((E, 1), ROUTE_MAX_EXP, f32)))
    lo0 = jnp.where(k_hi >= ROUTE_MAX_EXP, 0.0, jnp.exp2(-k_hi))
    hi0 = jnp.exp2(-k_lo)

    def val_step(_, lh):
        lo, hi = lh
        mid = (lo + hi) * 0.5
        ok = count_ge(mid) >= cap
        return jnp.where(ok, mid, lo), jnp.where(ok, hi, mid)

    lo, hi = lax.fori_loop(0, ROUTE_VAL_STEPS, val_step, (lo0, hi0))
    gt = aff >= hi
    eq = (aff >= lo) & (aff < hi)
    need = cap - jnp.sum(gt.astype(f32), axis=1, keepdims=True)
    tri = tri_ref[...]
    eq_before = _dot(eq.astype(bf16), tri)
    sel = gt | (eq & (eq_before < need))
    slot = _dot(sel.astype(bf16), tri)
    slot_f = jnp.where(sel, slot, -1.0)
    slot_ref[0] = slot_f.astype(jnp.int32)
    gate = jnp.where(sel, aff, 0.0)
    packed = jnp.concatenate([slot_f, gate, jnp.zeros((128 - 2 * E, T), f32)], axis=0)
    rt_ref[0] = packed.T


def _route(lt, tri, cap):
    B, E, T = lt.shape
    return pl.pallas_call(
        functools.partial(_route_kernel, cap=cap),
        grid=(B,),
        in_specs=[pl.BlockSpec((1, E, T), lambda b: (b, 0, 0)),
                  pl.BlockSpec((T, T), lambda b: (0, 0))],
        out_specs=[pl.BlockSpec((1, E, T), lambda b: (b, 0, 0)),
                   pl.BlockSpec((1, T, 128), lambda b: (b, 0, 0))],
        out_shape=[jax.ShapeDtypeStruct((B, E, T), jnp.int32), jax.ShapeDtypeStruct((B, T, 128), f32)],
        compiler_params=_params(("arbitrary",)),
        name="route",
    )(lt, tri)


def _ffn_kernel(slot_ref, h_ref, wg_ref, wu_ref, wd_ref, o_ref, wgb_ref, wub_ref, wdb_ref, *, cap):
    T = h_ref.shape[1]

    @pl.when(pl.program_id(1) == 0)
    def _():
        wgb_ref[...] = wg_ref[0].astype(bf16)
        wub_ref[...] = wu_ref[0].astype(bf16)
        wdb_ref[...] = wd_ref[0].astype(bf16)

    ci = lax.broadcasted_iota(jnp.int32, (cap, T), 0)
    onehot = (slot_ref[0, 0] == ci).astype(bf16)
    xg = _dot(onehot, h_ref[0]).astype(bf16)
    hg = _dot(xg, wgb_ref[...])
    hu = _dot(xg, wub_ref[...])
    hid = (hg * jax.nn.sigmoid(hg) * hu).astype(bf16)
    o_ref[0, 0] = _dot(hid, wdb_ref[...]).astype(bf16)


def _ffn(slot4, h2, wg, wu, wd, cap):
    B, T, D = h2.shape
    E, _, F = wg.shape
    return pl.pallas_call(
        functools.partial(_ffn_kernel, cap=cap),
        grid=(E, B),
        in_specs=[pl.BlockSpec((1, 1, 1, T), lambda e, b: (b, e, 0, 0)),
                  pl.BlockSpec((1, T, D), lambda e, b: (b, 0, 0)),
                  pl.BlockSpec((1, D, F), lambda e, b: (e, 0, 0)),
                  pl.BlockSpec((1, D, F), lambda e, b: (e, 0, 0)),
                  pl.BlockSpec((1, F, D), lambda e, b: (e, 0, 0))],
        out_specs=pl.BlockSpec((1, 1, cap, D), lambda e, b: (b, e, 0, 0)),
        out_shape=jax.ShapeDtypeStruct((B, E, cap, D), bf16),
        scratch_shapes=[pltpu.VMEM((D, F), bf16), pltpu.VMEM((D, F), bf16), pltpu.VMEM((F, D), bf16)],
        compiler_params=_params(("arbitrary", "arbitrary")),
        name="ffn",
    )(slot4, h2, wg, wu, wd)


def _combine_kernel(rt_ref, eo_ref, x1_ref, mod_ref, ng_ref, o_ref, *, cap):
    rt = rt_ref[0]
    tm = rt.shape[0]
    E = eo_ref.shape[1]
    ci = lax.broadcasted_iota(jnp.int32, (tm, cap), 1).astype(f32)
    y = jnp.zeros((tm, eo_ref.shape[3]), f32)
    for e in range(E):
        p = jnp.where(rt[:, e:e + 1] == ci, rt[:, E + e:E + e + 1], 0.0).astype(bf16)
        y = y + _dot(p, eo_ref[0, e])
    yn = y * lax.rsqrt(jnp.mean(y * y, axis=-1, keepdims=True) + NORM_EPS) * ng_ref[...]
    o_ref[0] = x1_ref[0] + mod_ref[0] * yn


def _combine(rt, eo, x1, g2mod, ng3, cap):
    B, T, D = x1.shape
    E = eo.shape[1]
    tm = 512 if T % 512 == 0 else T
    return pl.pallas_call(
        functools.partial(_combine_kernel, cap=cap),
        grid=(B, T // tm),
        in_specs=[pl.BlockSpec((1, tm, 128), lambda b, i: (b, i, 0)),
                  pl.BlockSpec((1, E, cap, D), lambda b, i: (b, 0, 0, 0)),
                  pl.BlockSpec((1, tm, D), lambda b, i: (b, i, 0)),
                  pl.BlockSpec((1, 1, D), lambda b, i: (b, 0, 0)),
                  pl.BlockSpec((1, D), lambda b, i: (0, 0))],
        out_specs=pl.BlockSpec((1, tm, D), lambda b, i: (b, i, 0)),
        out_shape=jax.ShapeDtypeStruct((B, T, D), f32),
        compiler_params=_params(("arbitrary", "arbitrary")),
        name="combine",
    )(rt, eo, x1, g2mod, ng3)


def _column_perm():
    sk, sv, rk, rv, wd, ad = 0, 512, 1024, 1536, 2048, 2112
    q0 = 2176
    rq, rg, rr, gd, mg = q0, q0 + 512, q0 + 1024, q0 + 1536, q0 + 1664
    rng = lambda a, n: np.arange(a, a + n)
    return np.concatenate([rng(mg, 2048), rng(sk, 512), rng(sv, 512), rng(rq, 512), rng(rg, 512),
                           rng(rk, 512), rng(rv, 512), rng(rr, 512), rng(wd, 64), rng(ad, 64), rng(gd, 128)])


def _rope_tables(T):
    t = jnp.arange(T)
    nfreq = RET_HEAD_DIM // 4
    inv = ROPE_BASE ** (-jnp.arange(nfreq, dtype=f32) / nfreq)
    ang = jnp.concatenate([(t // GRID_W).astype(f32)[:, None] * inv,
                           (t % GRID_W).astype(f32)[:, None] * inv], axis=-1)
    cos, sin = jnp.cos(ang), jnp.sin(ang)
    return jnp.concatenate([cos, cos], axis=1), jnp.concatenate([-sin, sin], axis=1)


def kernel(x, c, ctx, c_ctx, w_mod, b_mod, norm_g, w_in, ret_log_decay, ret_gn_g, rwkv_mu, rwkv_k_k, rwkv_k_a,
           rwkv_r_k, rwkv_w0, rwkv_w2, rwkv_a0, rwkv_a2, rwkv_g2, rwkv_ln_g, rwkv_ln_b, w_br_ret, w_br_rwkv,
           w_out, w_router, w_gate, w_up, w_down):
    B, T, D = x.shape
    CT = ctx.shape[1]
    assert w_mod.shape[0] == 1 and D == D_MODEL
    assert CT % ROWS == 0 and T % ROWS == 0 and T % GRID_W == 0
    cap = CAPACITY_FACTOR * T // N_EXPERTS
    assert cap % 8 == 0

    mrows = -(-(B + 1) // 8) * 8
    cc = jnp.zeros((mrows, D), f32).at[:B].set(c).at[B].set(c_ctx)
    mod = _modulation(cc, w_mod[0], b_mod[0])
    lat = mod[:B].reshape(B, N_MOD, D)
    cm = jnp.broadcast_to(mod[B].reshape(1, N_MOD, D), (B, N_MOD, D))
    modrows = jnp.concatenate([lat[:, 0:2], cm[:, 0:2]], axis=1)

    perm = _column_perm()
    w_perm = w_in[0][:, perm].astype(bf16)
    mu = rwkv_mu[0]
    ss = 2 * RWKV_WIDTH + DECAY_LORA + ICLR_LORA
    mu_full = jnp.zeros((2, IN_COLS), f32)
    mu_full = mu_full.at[:, C_RWK:C_RWK + 1024].set(mu[:, 0:1024])
    mu_full = mu_full.at[:, C_RWR:C_RWR + 512].set(mu[:, ss:ss + 512])
    mu_full = mu_full.at[:, C_LORA:C_LORA + 128].set(mu[:, 1024:ss])
    mu_full = mu_full.at[:, C_LORA + 128:].set(mu[:, ss + 512:])
    cosf, sinf = _rope_tables(T)
    z = _inproj(x, ctx, modrows, norm_g[0, 0:1], w_perm, mu_full, cosf, sinf)

    lg = -jnp.exp(ret_log_decay[0].astype(f32))
    ret_o = _retention(lg, z, CT)

    G = 2 * RWKV_HEAD_DIM
    w2p = jnp.zeros((2, G, RWKV_WIDTH), f32).at[:, :DECAY_LORA].set(rwkv_w2[0]).astype(bf16)
    a2p = jnp.zeros((2, G, RWKV_WIDTH), f32).at[:, DECAY_LORA:].set(rwkv_a2[0]).astype(bf16)
    rw_o, bonus = _rwkv(z, rwkv_w0[0][:, None, :], rwkv_a0[0][:, None, :], w2p, a2p,
                        rwkv_k_k[0][None], rwkv_k_a[0][None], rwkv_r_k[0][None], CT)

    mod2 = jnp.stack([lat[:, 2], lat[:, 3], lat[:, 4]], axis=1)
    wr_pad = jnp.zeros((D, 128), f32).at[:, :N_EXPERTS].set(w_router[0])
    wr_hi = wr_pad.astype(bf16)
    wr_lo = (wr_pad - wr_hi.astype(f32)).astype(bf16)
    x1, h2, lt = _merge(x, ret_o, rw_o, bonus, z, mod2, norm_g[0, 1:3], ret_gn_g[0][None], rwkv_ln_g[0][None],
                        rwkv_ln_b[0][None], rwkv_g2[0].astype(bf16), w_br_ret[0].astype(bf16),
                        w_br_rwkv[0].astype(bf16), w_out[0].astype(bf16), wr_hi, wr_lo, CT)

    ti = jnp.arange(T)
    tri = (ti[:, None] < ti[None, :]).astype(bf16)
    slot, rt = _route(lt, tri, cap)
    eo = _ffn(slot.reshape(B, N_EXPERTS, 1, T), h2, w_gate[0], w_up[0], w_down[0], cap)
    return _combine(rt, eo, x1, lat[:, 5:6], norm_g[0, 3:4], cap)
```

```python
import functools

import numpy as np
import jax
import jax.numpy as jnp
from jax import lax
from jax.experimental import pallas as pl
from jax.experimental.pallas import tpu as pltpu

f32 = jnp.float32
bf16 = jnp.bfloat16

D_MODEL = 1024
GRID_W = 64
RET_HEAD_DIM = 128
RET_WIDTH = 512
RET_HEADS = 4
RET_CHUNK = 128
RET_EPS = 1e-5
ROPE_BASE = 10000.0
RWKV_HEAD_DIM = 64
RWKV_WIDTH = 512
RWKV_PAIRS = 4
RWKV_CHUNK = 64
RWKV_SAMPLES_PER_STEP = 4
DECAY_LORA = 64
ICLR_LORA = 64
GATE_LORA = 128
RWKV_EPS = 64e-5
N_EXPERTS = 16
EXPERT_FF = 1024
CAPACITY_FACTOR = 2
N_MOD = 6
NORM_EPS = 1e-6

C_MERGE, C_RETK, C_RETV, C_RETQ, C_RETG = 0, 2048, 2560, 3072, 3584
C_RWK, C_RWV, C_RWR, C_LORA = 4096, 4608, 5120, 5632
USED_COLS = 5888
TN = 512
IN_COLS = -(-USED_COLS // TN) * TN
ROWS = 256
MERGE_ROWS = 512

ROUTE_MAX_EXP = 126.0
ROUTE_EXP_STEPS = 8
ROUTE_VAL_STEPS = 26

VMEM_LIMIT = 56 * 1024 * 1024


def _dot(a, b):
    return jnp.dot(a, b, preferred_element_type=f32)


def _dot_nt(a, b):
    return lax.dot_general(a, b, (((1,), (1,)), ((), ())), preferred_element_type=f32)


def _dot_tn(a, b):
    return lax.dot_general(a, b, (((0,), (0,)), ((), ())), preferred_element_type=f32)


def _bdot(a, b):
    return _dot(a.astype(bf16), b.astype(bf16))


def _split_dot(x, w):
    hi = x.astype(bf16)
    lo = (x - hi.astype(f32)).astype(bf16)
    return _dot(hi, w) + _dot(lo, w)


def _params(sem, limit=VMEM_LIMIT):
    return pltpu.CompilerParams(dimension_semantics=sem, vmem_limit_bytes=limit)


def _mod_kernel(c_ref, w_ref, b_ref, o_ref):
    c = c_ref[...]
    s = c * jax.nn.sigmoid(c)
    o_ref[...] = _bdot(s, w_ref[...]) + b_ref[...]


def _modulation(cc, w_mod, b_mod):
    m, d = cc.shape
    n = w_mod.shape[1]
    tn = 512
    return pl.pallas_call(
        _mod_kernel,
        grid=(n // tn,),
        in_specs=[pl.BlockSpec((m, d), lambda j: (0, 0)),
                  pl.BlockSpec((d, tn), lambda j: (0, j)),
                  pl.BlockSpec((1, tn), lambda j: (0, j))],
        out_specs=pl.BlockSpec((m, tn), lambda j: (0, j)),
        out_shape=jax.ShapeDtypeStruct((m, n), f32),
        compiler_params=_params(("arbitrary",)),
        name="mod",
    )(cc, w_mod, b_mod.reshape(1, n))


def _inproj_kernel(x_ref, c_ref, mod_ref, g_ref, w_ref, mu_ref, cos_ref, sin_ref, o_ref, h_ref, z0_ref, z1_ref):
    n = pl.program_id(1)
    nt = pl.num_programs(1) - 1
    T = x_ref.shape[1]
    L = T + c_ref.shape[1]
    nchunk = L // ROWS
    PAD = 8

    @pl.when(n == 0)
    def _():
        for zb in (z0_ref, z1_ref):
            zb[0:PAD, :] = jnp.zeros((PAD, TN), f32)
            zb[PAD + L:PAD + L + PAD, :] = jnp.zeros((PAD, TN), f32)
        for i in range(nchunk):
            r0 = i * ROWS
            xb = x_ref[0, r0:r0 + ROWS, :] if r0 < T else c_ref[0, r0 - T:r0 - T + ROWS, :]
            y = xb * lax.rsqrt(jnp.mean(xb * xb, axis=-1, keepdims=True) + NORM_EPS) * g_ref[...]
            o = 0 if r0 < T else 2
            sh = mod_ref[0, o:o + 1, :]
            sc = mod_ref[0, o + 1:o + 2, :]
            h_ref[r0:r0 + ROWS, :] = (y * (1.0 + sc) + sh).astype(bf16)

    def rope(z, r0, scale):
        if scale != 1.0:
            z = z * scale
        if r0 >= T:
            return z
        cs = cos_ref[r0:r0 + ROWS, :]
        sn = sin_ref[r0:r0 + ROWS, :]
        parts = []
        for hh in range(TN // RET_HEAD_DIM):
            zh = z[:, hh * RET_HEAD_DIM:(hh + 1) * RET_HEAD_DIM]
            parts.append(zh * cs + pltpu.roll(zh, RET_HEAD_DIM // 2, 1) * sn)
        return jnp.concatenate(parts, axis=1)

    def shift(z, r0, zsrc):
        prev = zsrc[PAD + r0 - 1:PAD + r0 - 1 + ROWS, :]
        nxt = zsrc[PAD + r0 + 1:PAD + r0 + 1 + ROWS, :]
        rid = r0 + lax.broadcasted_iota(jnp.int32, (ROWS, 1), 0)
        prev = jnp.where(rid == T, 0.0, prev)
        nxt = jnp.where(rid == T - 1, 0.0, nxt)
        return z + mu_ref[0:1, :] * (prev - z) + mu_ref[1:2, :] * (nxt - z)

    def lora_act(z, r0, zsrc):
        zs = shift(z, r0, zsrc)
        lane = lax.broadcasted_iota(jnp.int32, (1, TN), 1)
        return jnp.where(lane < DECAY_LORA, jnp.tanh(zs),
                         jnp.where(lane < DECAY_LORA + ICLR_LORA, zs, jax.nn.sigmoid(zs)))

    def step(post, matmul, zdst, zsrc):
        for i in range(nchunk):
            r0 = i * ROWS
            if matmul:
                zdst[PAD + r0:PAD + r0 + ROWS, :] = _dot(h_ref[r0:r0 + ROWS, :], w_ref[...])
            if post is not None:
                o_ref[0, r0:r0 + ROWS, :] = post(zsrc[PAD + r0:PAD + r0 + ROWS, :], r0, zsrc).astype(bf16)

    def run(cond, post, matmul):
        for par, (zdst, zsrc) in enumerate(((z0_ref, z1_ref), (z1_ref, z0_ref))):
            pl.when(cond & (n % 2 == par))(functools.partial(step, post, matmul, zdst, zsrc))

    t = lambda c: c // TN
    m = n - 1
    run(n == 0, None, True)
    run((m >= 0) & (m < t(C_RETK)), lambda z, r0, zs: jax.nn.sigmoid(z), True)
    run((m >= t(C_RETK)) & (m < t(C_RETV)), lambda z, r0, zs: rope(z, r0, RET_HEAD_DIM ** -0.5), True)
    run((m >= t(C_RETV)) & (m < t(C_RETQ)), lambda z, r0, zs: z, True)
    run((m >= t(C_RETQ)) & (m < t(C_RETG)), lambda z, r0, zs: rope(z, r0, 1.0), True)
    run((m >= t(C_RETG)) & (m < t(C_RWK)), lambda z, r0, zs: z * jax.nn.sigmoid(z), True)
    run((m >= t(C_RWK)) & (m < t(C_LORA)), shift, True)
    run((m >= t(C_LORA)) & (n < nt), lora_act, True)
    run((m >= t(C_LORA)) & (n == nt), lora_act, False)


def _inproj(x, ctx, modrows, g0, w_perm, mu_full, cosf, sinf):
    B, T, D = x.shape
    CT = ctx.shape[1]
    L = T + CT
    nt = IN_COLS // TN
    mm = lambda n: jnp.minimum(n, nt - 1)
    pp = lambda n: jnp.maximum(n - 1, 0)
    return pl.pallas_call(
        _inproj_kernel,
        grid=(B, nt + 1),
        in_specs=[pl.BlockSpec((1, T, D), lambda b, n: (b, 0, 0)),
                  pl.BlockSpec((1, CT, D), lambda b, n: (b, 0, 0)),
                  pl.BlockSpec((1, 4, D), lambda b, n: (b, 0, 0)),
                  pl.BlockSpec((1, D), lambda b, n: (0, 0)),
                  pl.BlockSpec((D, TN), lambda b, n: (0, mm(n))),
                  pl.BlockSpec((2, TN), lambda b, n: (0, pp(n))),
                  pl.BlockSpec((T, RET_HEAD_DIM), lambda b, n: (0, 0)),
                  pl.BlockSpec((T, RET_HEAD_DIM), lambda b, n: (0, 0))],
        out_specs=pl.BlockSpec((1, L, TN), lambda b, n: (b, 0, pp(n))),
        out_shape=jax.ShapeDtypeStruct((B, L, IN_COLS), bf16),
        scratch_shapes=[pltpu.VMEM((L, D), bf16), pltpu.VMEM((L + 16, TN), f32), pltpu.VMEM((L + 16, TN), f32)],
        compiler_params=_params(("arbitrary", "arbitrary")),
        name="inproj",
    )(x, ctx, modrows, g0, w_perm, mu_full, cosf, sinf)


def _ret_kernel(lg_ref, q_ref, k_ref, v_ref, o_ref, r_ref, tab_ref, *, ctx):
    L = q_ref.shape[1]
    Cc = RET_CHUNK
    hd = RET_HEAD_DIM
    lat = L - ctx
    nc = ctx // Cc
    nl = lat // Cc
    combos = [(h, d) for h in range(RET_HEADS) for d in (0, 1)]
    INTRA, CROSS, TAIL, DECAY = 0, 1, 2, 3

    @pl.when(pl.program_id(0) == 0)
    def _():
        ii = lax.broadcasted_iota(jnp.int32, (Cc, Cc), 0).astype(f32)
        jj = lax.broadcasted_iota(jnp.int32, (Cc, Cc), 1).astype(f32)
        for idx, (h, d) in enumerate(combos):
            lg = lg_ref[d, h]
            if d == 0:
                diff = ii - jj
                cross = jnp.exp(lg * (ii + 1.0))
                tailw = jnp.exp(lg * (Cc - 1.0 - ii))
            else:
                diff = jj - ii
                cross = jnp.exp(lg * (Cc - ii))
                tailw = jnp.exp(lg * ii)
            tab_ref[idx, INTRA] = jnp.where(diff >= 0, jnp.exp(lg * jnp.maximum(diff, 0.0)), 0.0)
            tab_ref[idx, CROSS] = cross
            tab_ref[idx, TAIL] = tailw
            tab_ref[idx, DECAY] = jnp.exp(jnp.zeros((Cc, Cc), f32) + lg * Cc)

    r_ref[...] = jnp.zeros(r_ref.shape, f32)
    o_ref[...] = jnp.zeros(o_ref.shape, f32)

    def cols(h):
        return slice(h * hd, (h + 1) * hd)

    def update(idx, kc, vc):
        ks = (kc.astype(f32) * tab_ref[idx, TAIL]).astype(bf16)
        r_ref[idx] = r_ref[idx] * tab_ref[idx, DECAY] + _dot_tn(ks, vc)

    def ctx_step(s, carry):
        for idx, (h, d) in enumerate(combos):
            row0 = pl.multiple_of(lat + (s * Cc if d == 0 else (nc - 1 - s) * Cc), Cc)
            update(idx, k_ref[0, pl.ds(row0, Cc), cols(h)], v_ref[0, pl.ds(row0, Cc), cols(h)])
        return carry

    def lat_step(s, carry):
        t0s, qs, ks, vs = [], [], [], []
        for h, d in combos:
            t0 = pl.multiple_of(s * Cc if d == 0 else (nl - 1 - s) * Cc, Cc)
            row0 = t0
            t0s.append(t0)
            qs.append(q_ref[0, pl.ds(row0, Cc), cols(h)])
            ks.append(k_ref[0, pl.ds(row0, Cc), cols(h)])
            vs.append(v_ref[0, pl.ds(row0, Cc), cols(h)])
        n = len(combos)
        sc = [(_dot_nt(qs[i], ks[i]) * tab_ref[i, INTRA]).astype(bf16) for i in range(n)]
        oc = [_dot(qs[i], r_ref[i].astype(bf16)) * tab_ref[i, CROSS] for i in range(n)]
        oi = [_dot(sc[i], vs[i]) for i in range(n)]
        for i, (h, d) in enumerate(combos):
            o_ref[0, pl.ds(t0s[i], Cc), cols(h)] += oi[i] + oc[i]
        for i in range(n):
            update(i, ks[i], vs[i])
        return carry

    lax.fori_loop(0, nc, ctx_step, 0)
    lax.fori_loop(0, nl, lat_step, 0)


def _retention(lg, z, ctx):
    B, L, _ = z.shape
    T = L - ctx
    W = RET_WIDTH
    Cc = RET_CHUNK
    blk = lambda c0: pl.BlockSpec((1, L, W), lambda b: (b, 0, c0 // W))
    nchain = 2 * RET_HEADS
    return pl.pallas_call(
        functools.partial(_ret_kernel, ctx=ctx),
        grid=(B,),
        in_specs=[pl.BlockSpec(memory_space=pltpu.SMEM), blk(C_RETQ), blk(C_RETK), blk(C_RETV)],
        out_specs=pl.BlockSpec((1, T, W), lambda b: (b, 0, 0)),
        out_shape=jax.ShapeDtypeStruct((B, T, W), f32),
        scratch_shapes=[pltpu.VMEM((nchain, Cc, Cc), f32), pltpu.VMEM((nchain, 4, Cc, Cc), f32)],
        compiler_params=_params(("arbitrary",)),
        name="ret",
    )(lg, z, z, z)


def _rwkv_kernel(k_ref, v_ref, r_ref, lo_ref, w0_ref, a0_ref, w2_ref, a2_ref, kk_ref, ka_ref, rk_ref,
                 o_ref, bo_ref, s_ref, *, nc, nb):
    d = pl.program_id(1)
    s = pl.program_id(2)
    C = RWKV_CHUNK
    G = 2 * RWKV_HEAD_DIM
    rev = d == 1

    @pl.when(s == 0)
    def _():
        s_ref[...] = jnp.zeros(s_ref.shape, f32)

    ii = lax.broadcasted_iota(jnp.int32, (C, G), 0)
    lane = lax.broadcasted_iota(jnp.int32, (C, G), 1)
    jj = lane & (RWKV_HEAD_DIM - 1)
    head0 = lane < RWKV_HEAD_DIM
    dlt = jnp.where(rev, ii - jj, jj - ii)
    strict = dlt < 0
    incl = dlt <= 0
    eye = (ii == jj).astype(f32)
    gi = lax.broadcasted_iota(jnp.int32, (G, G), 0)
    gj = lax.broadcasted_iota(jnp.int32, (G, G), 1)
    blockdiag = (gi >= RWKV_HEAD_DIM) == (gj >= RWKV_HEAD_DIM)
    ones_bd = blockdiag.astype(bf16)
    tri = incl[:, 0:C].astype(bf16)
    base = (ii >> 1) == (jj >> 1)
    offs = [((ii >> (lv + 1)) == (jj >> (lv + 1))) & ((ii >> lv) != (jj >> lv)) for lv in range(1, 6)]

    def segsum(x):
        return jnp.concatenate(
            [_bdot(x[:, p * G:(p + 1) * G], ones_bd) for p in range(RWKV_PAIRS)], axis=1)

    def stack(x):
        xb = x.astype(bf16)
        zero = jnp.zeros_like(xb)
        return jnp.concatenate([jnp.where(head0, xb, zero), jnp.where(head0, zero, xb)], axis=0)

    pre = []
    for bb in range(nb):
        kx = k_ref[bb].astype(f32)
        vx = v_ref[bb].astype(f32)
        rx = r_ref[bb].astype(f32)
        lo = lo_ref[bb, :, 0:DECAY_LORA + ICLR_LORA]
        u = w0_ref[0] + _dot(lo, w2_ref[0])
        softplus = jnp.maximum(-u, 0.0) + jnp.log1p(jnp.exp(-jnp.abs(u)))
        lw = -jnp.exp(-softplus - 0.5)
        a = jax.nn.sigmoid(a0_ref[0] + _dot(lo, a2_ref[0]))
        kkr = kx * kk_ref[...]
        kk = kkr * lax.rsqrt(segsum(kkr * kkr) + 1e-12)
        kd = kx * (1.0 + (a - 1.0) * ka_ref[...])
        be = kk * a
        bonus = segsum(rx * kd * rk_ref[...]) * vx
        cum = _split_dot_left(tri, lw)
        tot = jnp.where(rev, cum[0:1, :], cum[C - 1:C, :])
        gneg = jnp.exp(-cum)
        gh = jnp.exp(tot - cum)
        pre.append(dict(alb=-kk * jnp.exp(cum - lw), rb=rx * jnp.exp(cum), beb=be * gneg, kb=kd * gneg,
                        beh=be * gh, kh=kd * gh, etot=jnp.exp(tot), v=vx, bonus=bonus))

    units = [(bb, p) for bb in range(nb) for p in range(RWKV_PAIRS)]
    sl = lambda p: slice(p * G, (p + 1) * G)
    part = lambda un, name: pre[un[0]][name][:, sl(un[1])]
    S = {un: s_ref[un[0] * RWKV_PAIRS + un[1]] for un in units}
    Sb = {un: S[un].astype(bf16) for un in units}
    X = {un: part(un, "alb").astype(bf16) for un in units}
    Rb = {un: part(un, "rb").astype(bf16) for un in units}
    Ybs = {un: stack(part(un, "beb")) for un in units}
    Yks = {un: stack(part(un, "kb")) for un in units}
    Vb = {un: part(un, "v").astype(bf16) for un in units}
    Vs = {un: stack(part(un, "v")) for un in units}

    XR = {un: jnp.concatenate([X[un], Rb[un]], axis=0) for un in units}
    APb = {un: _dot_nt(XR[un], Ybs[un]) for un in units}
    APk = {un: _dot_nt(XR[un], Yks[un]) for un in units}
    BO = {un: _dot_nt(XR[un], Sb[un]) for un in units}
    Aab = {un: jnp.where(strict, APb[un][0:C], 0.0) for un in units}
    Aak = {un: jnp.where(strict, APk[un][0:C], 0.0).astype(bf16) for un in units}
    Pab = {un: jnp.where(incl, APb[un][C:2 * C], 0.0).astype(bf16) for un in units}
    Pak = {un: jnp.where(incl, APk[un][C:2 * C], 0.0).astype(bf16) for un in units}

    Tm = {un: eye + jnp.where(base, Aab[un], 0.0) for un in units}
    for off in offs:
        Xs = {un: _dot(jnp.where(off, Aab[un], 0.0).astype(bf16), stack(Tm[un])) for un in units}
        Tm = {un: Tm[un] + _dot(Tm[un].astype(bf16), stack(Xs[un])) for un in units}

    Bm = {un: BO[un][0:C] + _dot(Aak[un], Vs[un]) for un in units}
    U = {un: _dot(Tm[un].astype(bf16), stack(Bm[un])) for un in units}
    Om = {un: BO[un][C:2 * C]
          + _dot(jnp.concatenate([Pab[un], Pak[un]], axis=1), jnp.concatenate([stack(U[un]), Vs[un]], axis=0))
          for un in units}
    for un in units:
        bb, p = un
        upd = _dot_tn(jnp.concatenate([U[un].astype(bf16), Vb[un]], axis=0),
                      jnp.concatenate([part(un, "beh").astype(bf16), part(un, "kh").astype(bf16)], axis=0))
        s_ref[bb * RWKV_PAIRS + p] = jnp.where(blockdiag, S[un] * part(un, "etot") + upd, 0.0)

    @pl.when(s >= nc)
    def _():
        for bb in range(nb):
            o_ref[0, bb] = jnp.concatenate([Om[bb, p] for p in range(RWKV_PAIRS)], axis=1)
            bo_ref[0, bb] = pre[bb]["bonus"]


def _split_dot_left(w, x):
    hi = x.astype(bf16)
    lo = (x - hi.astype(f32)).astype(bf16)
    return _dot(w, hi) + _dot(w, lo)


def _rwkv(z, w0, a0, w2p, a2p, k_k, k_a, r_k, ctx):
    B, L, _ = z.shape
    T = L - ctx
    C = RWKV_CHUNK
    nc, nl = ctx // C, T // C
    W = RWKV_WIDTH

    def chunk(d, s):
        fwd = jnp.where(s < nc, nl + s, s - nc)
        bwd = jnp.where(s < nc, nl + nc - 1 - s, nl - 1 - (s - nc))
        return jnp.where(d == 0, fwd, bwd)

    def ochunk(d, s):
        sl = jnp.maximum(s - nc, 0)
        return jnp.where(d == 0, sl, nl - 1 - sl)

    nb = RWKV_SAMPLES_PER_STEP if B % RWKV_SAMPLES_PER_STEP == 0 else 1
    zblk = lambda c0, w: pl.BlockSpec((nb, C, w), lambda b, d, s: (b, chunk(d, s), c0 // w))
    dpar = lambda r: pl.BlockSpec((1, r, W), lambda b, d, s: (d, 0, 0))
    par = pl.BlockSpec((1, W), lambda b, d, s: (0, 0))
    oblk = pl.BlockSpec((1, nb, C, W), lambda b, d, s: (d, b, ochunk(d, s), 0))
    G = 2 * RWKV_HEAD_DIM
    return pl.pallas_call(
        functools.partial(_rwkv_kernel, nc=nc, nb=nb),
        grid=(B // nb, 2, nc + nl),
        in_specs=[zblk(C_RWK, W), zblk(C_RWV, W), zblk(C_RWR, W), zblk(C_LORA, 256),
                  dpar(1), dpar(1), dpar(G), dpar(G), par, par, par],
        out_specs=[oblk, oblk],
        out_shape=[jax.ShapeDtypeStruct((2, B, T, W), f32), jax.ShapeDtypeStruct((2, B, T, W), f32)],
        scratch_shapes=[pltpu.VMEM((nb * RWKV_PAIRS, G, G), f32)],
        compiler_params=_params(("arbitrary", "arbitrary", "arbitrary")),
        name="rwkv",
    )(z, z, z, z, w0, a0, w2p, a2p, k_k, k_a, r_k)


def _merge_kernel(x_ref, ret_ref, rw0_ref, rw1_ref, b0_ref, b1_ref, gate_ref, retg_ref, gd_ref, mod_ref,
                  ng_ref, gn_ref, lng_ref, lnb_ref, g2_ref, wbr_ref, wbw_ref, wout_ref, wrh_ref, wrl_ref,
                  x1_ref, h2_ref, lt_ref):
    hd = RET_HEAD_DIM
    ret = ret_ref[0]
    parts = []
    for hh in range(RET_HEADS):
        xh = ret[:, hh * hd:(hh + 1) * hd]
        mu = jnp.mean(xh, axis=-1, keepdims=True)
        dv = xh - mu
        var = jnp.mean(dv * dv, axis=-1, keepdims=True)
        parts.append(dv * lax.rsqrt(var + RET_EPS))
    yr = retg_ref[0].astype(f32) * (jnp.concatenate(parts, axis=1) * gn_ref[...])
    y_ret = _bdot(yr, wbr_ref[...])
    W = RWKV_WIDTH
    gi = lax.broadcasted_iota(jnp.int32, (W, W), 0)
    gj = lax.broadcasted_iota(jnp.int32, (W, W), 1)
    ones_bd = ((gi >> 6) == (gj >> 6)).astype(bf16)
    o = rw0_ref[0, 0] + rw1_ref[0, 0]
    mu = _split_dot(o, ones_bd) * (1.0 / RWKV_HEAD_DIM)
    dv = o - mu
    var = _bdot(dv * dv, ones_bd) * (1.0 / RWKV_HEAD_DIM)
    yw = dv * lax.rsqrt(var + RWKV_EPS) * lng_ref[...] + lnb_ref[...]
    gate = _dot(gd_ref[0], g2_ref[...])
    yw = (yw + b0_ref[0, 0] + b1_ref[0, 0]) * gate
    y_rw = _bdot(yw, wbw_ref[...])
    D = y_ret.shape[1]
    g = gate_ref[0].astype(f32)
    m = g[:, :D] * y_ret + g[:, D:] * y_rw
    y = _bdot(m, wout_ref[...])

    def rms(v, gg):
        return v * lax.rsqrt(jnp.mean(v * v, axis=-1, keepdims=True) + NORM_EPS) * gg

    x1 = x_ref[0] + mod_ref[0, 0:1, :] * rms(y, ng_ref[0:1, :])
    x1_ref[0] = x1
    h2 = rms(x1, ng_ref[1:2, :]) * (1.0 + mod_ref[0, 2:3, :]) + mod_ref[0, 1:2, :]
    h2b = h2.astype(bf16)
    h2_ref[0] = h2b
    h2l = (h2 - h2b.astype(f32)).astype(bf16)
    lgt = _dot(h2b, wrh_ref[...]) + _dot(h2l, wrh_ref[...]) + _dot(h2b, wrl_ref[...])
    lt_ref[0] = lgt.T[0:N_EXPERTS, :]


def _merge(x, ret_o, rw_o, bonus, z, mod2, ng12, gn, lng, lnb, g2, wbr, wbw, wout, wrh, wrl, ctx):
    B, T, D = x.shape
    tm = MERGE_ROWS if T % MERGE_ROWS == 0 else ROWS
    co = 0
    W = RWKV_WIDTH
    row = lambda w: pl.BlockSpec((1, tm, w), lambda b, i: (b, i, 0))
    dblk = lambda dd: pl.BlockSpec((1, 1, tm, W), lambda b, i: (dd, b, i, 0))
    zblk = lambda c0, w: pl.BlockSpec((1, tm, w), lambda b, i: (b, co + i, c0 // w))
    full = lambda a: pl.BlockSpec(a.shape, lambda b, i: (0,) * a.ndim)
    return pl.pallas_call(
        _merge_kernel,
        grid=(B, T // tm),
        in_specs=[row(D), row(W), dblk(0), dblk(1), dblk(0), dblk(1),
                  zblk(C_MERGE, 2 * D), zblk(C_RETG, W), zblk(C_LORA + 128, 128),
                  pl.BlockSpec((1, 3, D), lambda b, i: (b, 0, 0)),
                  full(ng12), full(gn), full(lng), full(lnb), full(g2), full(wbr), full(wbw), full(wout),
                  full(wrh), full(wrl)],
        out_specs=[row(D), row(D), pl.BlockSpec((1, N_EXPERTS, tm), lambda b, i: (b, 0, i))],
        out_shape=[jax.ShapeDtypeStruct((B, T, D), f32), jax.ShapeDtypeStruct((B, T, D), bf16),
                   jax.ShapeDtypeStruct((B, N_EXPERTS, T), f32)],
        compiler_params=_params(("arbitrary", "arbitrary")),
        name="merge",
    )(x, ret_o, rw_o, rw_o, bonus, bonus, z, z, z, mod2, ng12, gn, lng, lnb, g2, wbr, wbw, wout, wrh, wrl)


def _route_kernel(lt_ref, tri_ref, slot_ref, rt_ref, *, cap):
    lg = lt_ref[0]
    E, T = lg.shape
    mx = jnp.max(lg, axis=0, keepdims=True)
    ex = jnp.exp(lg - mx)
    aff = ex / jnp.sum(ex, axis=0, keepdims=True)

    def count_ge(cand):
        return jnp.sum((aff >= cand).astype(f32), axis=1, keepdims=True)

    def exp_step(_, kk):
        k_lo, k_hi = kk
        km = jnp.floor((k_lo + k_hi) * 0.5)
        ok = count_ge(jnp.exp2(-km)) >= cap
        return jnp.where(ok, k_lo, km), jnp.where(ok, km, k_hi)

    k_lo, k_hi = lax.fori_loop(0, ROUTE_EXP_STEPS, exp_step,
                               (jnp.full((E, 1), -1.0, f32), jnp.full((E, 1), ROUTE_MAX_EXP, f32)))
    lo0 = jnp.where(k_hi >= ROUTE_MAX_EXP, 0.0, jnp.exp2(-k_hi))
    hi0 = jnp.exp2(-k_lo)

    def val_step(_, lh):
        lo, hi = lh
        mid = (lo + hi) * 0.5
        ok = count_ge(mid) >= cap
        return jnp.where(ok, mid, lo), jnp.where(ok, hi, mid)

    lo, hi = lax.fori_loop(0, ROUTE_VAL_STEPS, val_step, (lo0, hi0))
    gt = aff >= hi
    eq = (aff >= lo) & (aff < hi)
    need = cap - jnp.sum(gt.astype(f32), axis=1, keepdims=True)
    tri = tri_ref[...]
    eq_before = _dot(eq.astype(bf16), tri)
    sel = gt | (eq & (eq_before < need))
    slot = _dot(sel.astype(bf16), tri)
    slot_f = jnp.where(sel, slot, -1.0)
    slot_ref[0] = slot_f.astype(jnp.int32)
    gate = jnp.where(sel, aff, 0.0)
    packed = jnp.concatenate([slot_f, gate, jnp.zeros((128 - 2 * E, T), f32)], axis=0)
    rt_ref[0] = packed.T


def _route(lt, tri, cap):
    B, E, T = lt.shape
    return pl.pallas_call(
        functools.partial(_route_kernel, cap=cap),
        grid=(B,),
        in_specs=[pl.BlockSpec((1, E, T), lambda b: (b, 0, 0)),
                  pl.BlockSpec((T, T), lambda b: (0, 0))],
        out_specs=[pl.BlockSpec((1, E, T), lambda b: (b, 0, 0)),
                   pl.BlockSpec((1, T, 128), lambda b: (b, 0, 0))],
        out_shape=[jax.ShapeDtypeStruct((B, E, T), jnp.int32), jax.ShapeDtypeStruct((B, T, 128), f32)],
        compiler_params=_params(("arbitrary",)),
        name="route",
    )(lt, tri)


def _ffn_kernel(slot_ref, h_ref, wg_ref, wu_ref, wd_ref, o_ref, wgb_ref, wub_ref, wdb_ref, *, cap):
    T = h_ref.shape[1]

    @pl.when(pl.program_id(1) == 0)
    def _():
        wgb_ref[...] = wg_ref[0].astype(bf16)
        wub_ref[...] = wu_ref[0].astype(bf16)
        wdb_ref[...] = wd_ref[0].astype(bf16)

    ci = lax.broadcasted_iota(jnp.int32, (cap, T), 0)
    onehot = (slot_ref[0, 0] == ci).astype(bf16)
    xg = _dot(onehot, h_ref[0]).astype(bf16)
    hg = _dot(xg, wgb_ref[...])
    hu = _dot(xg, wub_ref[...])
    hid = (hg * jax.nn.sigmoid(hg) * hu).astype(bf16)
    o_ref[0, 0] = _dot(hid, wdb_ref[...]).astype(bf16)


def _ffn(slot4, h2, wg, wu, wd, cap):
    B, T, D = h2.shape
    E, _, F = wg.shape
    return pl.pallas_call(
        functools.partial(_ffn_kernel, cap=cap),
        grid=(E, B),
        in_specs=[pl.BlockSpec((1, 1, 1, T), lambda e, b: (b, e, 0, 0)),
                  pl.BlockSpec((1, T, D), lambda e, b: (b, 0, 0)),
                  pl.BlockSpec((1, D, F), lambda e, b: (e, 0, 0)),
                  pl.BlockSpec((1, D, F), lambda e, b: (e, 0, 0)),
                  pl.BlockSpec((1, F, D), lambda e, b: (e, 0, 0))],
        out_specs=pl.BlockSpec((1, 1, cap, D), lambda e, b: (b, e, 0, 0)),
        out_shape=jax.ShapeDtypeStruct((B, E, cap, D), bf16),
        scratch_shapes=[pltpu.VMEM((D, F), bf16), pltpu.VMEM((D, F), bf16), pltpu.VMEM((F, D), bf16)],
        compiler_params=_params(("arbitrary", "arbitrary")),
        name="ffn",
    )(slot4, h2, wg, wu, wd)


def _combine_kernel(rt_ref, eo_ref, x1_ref, mod_ref, ng_ref, o_ref, *, cap):
    rt = rt_ref[0]
    tm = rt.shape[0]
    E = eo_ref.shape[1]
    ci = lax.broadcasted_iota(jnp.int32, (tm, cap), 1).astype(f32)
    y = jnp.zeros((tm, eo_ref.shape[3]), f32)
    for e in range(E):
        p = jnp.where(rt[:, e:e + 1] == ci, rt[:, E + e:E + e + 1], 0.0).astype(bf16)
        y = y + _dot(p, eo_ref[0, e])
    yn = y * lax.rsqrt(jnp.mean(y * y, axis=-1, keepdims=True) + NORM_EPS) * ng_ref[...]
    o_ref[0] = x1_ref[0] + mod_ref[0] * yn


def _combine(rt, eo, x1, g2mod, ng3, cap):
    B, T, D = x1.shape
    E = eo.shape[1]
    tm = 512 if T % 512 == 0 else T
    return pl.pallas_call(
        functools.partial(_combine_kernel, cap=cap),
        grid=(B, T // tm),
        in_specs=[pl.BlockSpec((1, tm, 128), lambda b, i: (b, i, 0)),
                  pl.BlockSpec((1, E, cap, D), lambda b, i: (b, 0, 0, 0)),
                  pl.BlockSpec((1, tm, D), lambda b, i: (b, i, 0)),
                  pl.BlockSpec((1, 1, D), lambda b, i: (b, 0, 0)),
                  pl.BlockSpec((1, D), lambda b, i: (0, 0))],
        out_specs=pl.BlockSpec((1, tm, D), lambda b, i: (b, i, 0)),
        out_shape=jax.ShapeDtypeStruct((B, T, D), f32),
        compiler_params=_params(("arbitrary", "arbitrary")),
        name="combine",
    )(rt, eo, x1, g2mod, ng3)


def _column_perm():
    sk, sv, rk, rv, wd, ad = 0, 512, 1024, 1536, 2048, 2112
    q0 = 2176
    rq, rg, rr, gd, mg = q0, q0 + 512, q0 + 1024, q0 + 1536, q0 + 1664
    rng = lambda a, n: np.arange(a, a + n)
    return np.concatenate([rng(mg, 2048), rng(sk, 512), rng(sv, 512), rng(rq, 512), rng(rg, 512),
                           rng(rk, 512), rng(rv, 512), rng(rr, 512), rng(wd, 64), rng(ad, 64), rng(gd, 128)])


def _rope_tables(T):
    t = jnp.arange(T)
    nfreq = RET_HEAD_DIM // 4
    inv = ROPE_BASE ** (-jnp.arange(nfreq, dtype=f32) / nfreq)
    ang = jnp.concatenate([(t // GRID_W).astype(f32)[:, None] * inv,
                           (t % GRID_W).astype(f32)[:, None] * inv], axis=-1)
    cos, sin = jnp.cos(ang), jnp.sin(ang)
    return jnp.concatenate([cos, cos], axis=1), jnp.concatenate([-sin, sin], axis=1)


def kernel(x, c, ctx, c_ctx, w_mod, b_mod, norm_g, w_in, ret_log_decay, ret_gn_g, rwkv_mu, rwkv_k_k, rwkv_k_a,
           rwkv_r_k, rwkv_w0, rwkv_w2, rwkv_a0, rwkv_a2, rwkv_g2, rwkv_ln_g, rwkv_ln_b, w_br_ret, w_br_rwkv,
           w_out, w_router, w_gate, w_up, w_down):
    B, T, D = x.shape
    CT = ctx.shape[1]
    assert w_mod.shape[0] == 1 and D == D_MODEL
    assert CT % ROWS == 0 and T % ROWS == 0 and T % GRID_W == 0
    cap = CAPACITY_FACTOR * T // N_EXPERTS
    assert cap % 8 == 0

    mrows = -(-(B + 1) // 8) * 8
    cc = jnp.zeros((mrows, D), f32).at[:B].set(c).at[B].set(c_ctx)
    mod = _modulation(cc, w_mod[0], b_mod[0])
    lat = mod[:B].reshape(B, N_MOD, D)
    cm = jnp.broadcast_to(mod[B].reshape(1, N_MOD, D), (B, N_MOD, D))
    modrows = jnp.concatenate([lat[:, 0:2], cm[:, 0:2]], axis=1)

    perm = _column_perm()
    w_perm = jnp.pad(w_in[0][:, perm].astype(bf16), ((0, 0), (0, IN_COLS - USED_COLS)))
    mu = rwkv_mu[0]
    ss = 2 * RWKV_WIDTH + DECAY_LORA + ICLR_LORA
    mu_full = jnp.zeros((2, IN_COLS), f32)
    mu_full = mu_full.at[:, C_RWK:C_RWK + 1024].set(mu[:, 0:1024])
    mu_full = mu_full.at[:, C_RWR:C_RWR + 512].set(mu[:, ss:ss + 512])
    mu_full = mu_full.at[:, C_LORA:C_LORA + 128].set(mu[:, 1024:ss])
    mu_full = mu_full.at[:, C_LORA + 128:USED_COLS].set(mu[:, ss + 512:])
    cosf, sinf = _rope_tables(T)
    z = _inproj(x, ctx, modrows, norm_g[0, 0:1], w_perm, mu_full, cosf, sinf)

    lg = -jnp.exp(ret_log_decay[0].astype(f32))
    ret_o = _retention(lg, z, CT)

    G = 2 * RWKV_HEAD_DIM
    w2p = jnp.zeros((2, G, RWKV_WIDTH), f32).at[:, :DECAY_LORA].set(rwkv_w2[0]).astype(bf16)
    a2p = jnp.zeros((2, G, RWKV_WIDTH), f32).at[:, DECAY_LORA:].set(rwkv_a2[0]).astype(bf16)
    rw_o, bonus = _rwkv(z, rwkv_w0[0][:, None, :], rwkv_a0[0][:, None, :], w2p, a2p,
                        rwkv_k_k[0][None], rwkv_k_a[0][None], rwkv_r_k[0][None], CT)

    mod2 = jnp.stack([lat[:, 2], lat[:, 3], lat[:, 4]], axis=1)
    wr_pad = jnp.zeros((D, 128), f32).at[:, :N_EXPERTS].set(w_router[0])
    wr_hi = wr_pad.astype(bf16)
    wr_lo = (wr_pad - wr_hi.astype(f32)).astype(bf16)
    x1, h2, lt = _merge(x, ret_o, rw_o, bonus, z, mod2, norm_g[0, 1:3], ret_gn_g[0][None], rwkv_ln_g[0][None],
                        rwkv_ln_b[0][None], rwkv_g2[0].astype(bf16), w_br_ret[0].astype(bf16),
                        w_br_rwkv[0].astype(bf16), w_out[0].astype(bf16), wr_hi, wr_lo, CT)

    ti = jnp.arange(T)
    tri = (ti[:, None] < ti[None, :]).astype(bf16)
    slot, rt = _route(lt, tri, cap)
    eo = _ffn(slot.reshape(B, N_EXPERTS, 1, T), h2, w_gate[0], w_up[0], w_down[0], cap)
    return _combine(rt, eo, x1, lat[:, 5:6], norm_g[0, 3:4], cap)
```

```python
import functools

import numpy as np
import jax
import jax.numpy as jnp
from jax import lax
from jax.experimental import pallas as pl
from jax.experimental.pallas import tpu as pltpu

f32 = jnp.float32
bf16 = jnp.bfloat16

D_MODEL = 1024
GRID_W = 64
RET_HEAD_DIM = 128
RET_WIDTH = 512
RET_HEADS = 4
RET_CHUNK = 128
RET_EPS = 1e-5
ROPE_BASE = 10000.0
RWKV_HEAD_DIM = 64
RWKV_WIDTH = 512
RWKV_PAIRS = 4
RWKV_CHUNK = 64
RWKV_SAMPLES_PER_STEP = 4
DECAY_LORA = 64
ICLR_LORA = 64
GATE_LORA = 128
RWKV_EPS = 64e-5
N_EXPERTS = 16
EXPERT_FF = 1024
CAPACITY_FACTOR = 2
N_MOD = 6
NORM_EPS = 1e-6

C_MERGE, C_RETK, C_RETV, C_RETQ, C_RETG = 0, 2048, 2560, 3072, 3584
C_RWK, C_RWV, C_RWR, C_LORA = 4096, 4608, 5120, 5632
USED_COLS = 5888
TN = 512
IN_COLS = -(-USED_COLS // TN) * TN
ROWS = 256
STEP_ROWS = 768
POST_ROWS = 128
MERGE_ROWS = 512

ROUTE_MAX_EXP = 126.0
ROUTE_EXP_STEPS = 8
ROUTE_VAL_STEPS = 26

VMEM_LIMIT = 56 * 1024 * 1024


def _dot(a, b):
    return jnp.dot(a, b, preferred_element_type=f32)


def _dot_nt(a, b):
    return lax.dot_general(a, b, (((1,), (1,)), ((), ())), preferred_element_type=f32)


def _dot_tn(a, b):
    return lax.dot_general(a, b, (((0,), (0,)), ((), ())), preferred_element_type=f32)


def _bdot(a, b):
    return _dot(a.astype(bf16), b.astype(bf16))


def _split_dot(x, w):
    hi = x.astype(bf16)
    lo = (x - hi.astype(f32)).astype(bf16)
    return _dot(hi, w) + _dot(lo, w)


def _params(sem, limit=VMEM_LIMIT):
    return pltpu.CompilerParams(dimension_semantics=sem, vmem_limit_bytes=limit)


def _mod_kernel(c_ref, w_ref, b_ref, o_ref):
    c = c_ref[...]
    s = c * jax.nn.sigmoid(c)
    o_ref[...] = _bdot(s, w_ref[...]) + b_ref[...]


def _modulation(cc, w_mod, b_mod):
    m, d = cc.shape
    n = w_mod.shape[1]
    tn = 512
    return pl.pallas_call(
        _mod_kernel,
        grid=(n // tn,),
        in_specs=[pl.BlockSpec((m, d), lambda j: (0, 0)),
                  pl.BlockSpec((d, tn), lambda j: (0, j)),
                  pl.BlockSpec((1, tn), lambda j: (0, j))],
        out_specs=pl.BlockSpec((m, tn), lambda j: (0, j)),
        out_shape=jax.ShapeDtypeStruct((m, n), f32),
        compiler_params=_params(("arbitrary",)),
        name="mod",
    )(cc, w_mod, b_mod.reshape(1, n))


def _inproj_kernel(x_ref, c_ref, mod_ref, g_ref, w_ref, mu_ref, cos_ref, sin_ref, o_ref, h_ref, z0_ref, z1_ref):
    n = pl.program_id(1)
    nt = pl.num_programs(1) - 1
    T = x_ref.shape[1]
    L = T + c_ref.shape[1]
    nlat = T // ROWS
    nchunk = L // ROWS
    PAD = 8

    SR = STEP_ROWS if L % STEP_ROWS == 0 else ROWS

    @pl.when(n == 0)
    def _():
        for zb in (z0_ref, z1_ref):
            zb[0:PAD, :] = jnp.zeros((PAD, TN), f32)
            zb[PAD + L:PAD + L + PAD, :] = jnp.zeros((PAD, TN), f32)

        def norm_chunk(src_ref, s0, r0, o):
            xb = src_ref[0, pl.ds(s0, ROWS), :]
            y = xb * lax.rsqrt(jnp.mean(xb * xb, axis=-1, keepdims=True) + NORM_EPS) * g_ref[...]
            sh = mod_ref[0, o:o + 1, :]
            sc = mod_ref[0, o + 1:o + 2, :]
            h_ref[pl.ds(r0, ROWS), :] = (y * (1.0 + sc) + sh).astype(bf16)

        def lat_body(i, carry):
            r0 = pl.multiple_of(i * ROWS, ROWS)
            norm_chunk(x_ref, r0, r0, 0)
            return carry

        lax.fori_loop(0, nlat, lat_body, 0)
        for j in range(nlat, nchunk):
            norm_chunk(c_ref, j * ROWS - T, j * ROWS, 2)

    PR = POST_ROWS

    def cur(zsrc, r0):
        return zsrc[pl.ds(PAD + r0, PR), :]

    def plain(fn):
        return lambda zsrc, r0: fn(cur(zsrc, r0))

    def rope(scale):
        def post(zsrc, r0):
            z = cur(zsrc, r0)
            if scale != 1.0:
                z = z * scale
            cs = cos_ref[pl.ds(r0, PR), :]
            sn = sin_ref[pl.ds(r0, PR), :]
            parts = []
            for hh in range(TN // RET_HEAD_DIM):
                zh = z[:, hh * RET_HEAD_DIM:(hh + 1) * RET_HEAD_DIM]
                parts.append(zh * cs + pltpu.roll(zh, RET_HEAD_DIM // 2, 1) * sn)
            return jnp.concatenate(parts, axis=1)
        return post

    def shift(zsrc, r0):
        win = zsrc[pl.ds(r0, PR + 2 * PAD), :]
        prev = win[PAD - 1:PAD - 1 + PR]
        z = win[PAD:PAD + PR]
        nxt = win[PAD + 1:PAD + 1 + PR]
        rid = r0 + lax.broadcasted_iota(jnp.int32, (PR, 1), 0)
        prev = jnp.where(rid == T, 0.0, prev)
        nxt = jnp.where(rid == T - 1, 0.0, nxt)
        return z + mu_ref[0:1, :] * (prev - z) + mu_ref[1:2, :] * (nxt - z)

    def lora_act(zsrc, r0):
        zs = shift(zsrc, r0)
        lane = lax.broadcasted_iota(jnp.int32, (1, TN), 1)
        return jnp.where(lane < DECAY_LORA, jnp.tanh(zs),
                         jnp.where(lane < DECAY_LORA + ICLR_LORA, zs, jax.nn.sigmoid(zs)))

    def step(post, matmul, zdst, zsrc):
        def body(i, carry):
            r0 = pl.multiple_of(i * SR, SR)
            if matmul:
                zdst[pl.ds(PAD + r0, SR), :] = _dot(h_ref[pl.ds(r0, SR), :], w_ref[...])
            if post is not None:
                for sub in range(SR // PR):
                    r = pl.multiple_of(r0 + sub * PR, PR)
                    o_ref[0, pl.ds(r, PR), :] = post(zsrc, r).astype(bf16)
            return carry
        lax.fori_loop(0, L // SR, body, 0)

    def run(cond, post, matmul):
        for par, (zdst, zsrc) in enumerate(((z0_ref, z1_ref), (z1_ref, z0_ref))):
            pl.when(cond & (n % 2 == par))(functools.partial(step, post, matmul, zdst, zsrc))

    t = lambda c: c // TN
    m = n - 1
    run(n == 0, None, True)
    run((m >= 0) & (m < t(C_RETK)), plain(jax.nn.sigmoid), True)
    run((m >= t(C_RETK)) & (m < t(C_RETV)), rope(RET_HEAD_DIM ** -0.5), True)
    run((m >= t(C_RETV)) & (m < t(C_RETQ)), plain(lambda z: z), True)
    run((m >= t(C_RETQ)) & (m < t(C_RETG)), rope(1.0), True)
    run((m >= t(C_RETG)) & (m < t(C_RWK)), plain(lambda z: z * jax.nn.sigmoid(z)), True)
    run((m >= t(C_RWK)) & (m < t(C_LORA)), shift, True)
    run((m >= t(C_LORA)) & (n < nt), lora_act, True)
    run((m >= t(C_LORA)) & (n == nt), lora_act, False)


def _inproj(x, ctx, modrows, g0, w_perm, mu_full, cosf, sinf):
    B, T, D = x.shape
    CT = ctx.shape[1]
    L = T + CT
    nt = IN_COLS // TN
    mm = lambda n: jnp.minimum(n, nt - 1)
    pp = lambda n: jnp.maximum(n - 1, 0)
    return pl.pallas_call(
        _inproj_kernel,
        grid=(B, nt + 1),
        in_specs=[pl.BlockSpec((1, T, D), lambda b, n: (b, 0, 0)),
                  pl.BlockSpec((1, CT, D), lambda b, n: (b, 0, 0)),
                  pl.BlockSpec((1, 4, D), lambda b, n: (b, 0, 0)),
                  pl.BlockSpec((1, D), lambda b, n: (0, 0)),
                  pl.BlockSpec((D, TN), lambda b, n: (0, mm(n))),
                  pl.BlockSpec((2, TN), lambda b, n: (0, pp(n))),
                  pl.BlockSpec((L, RET_HEAD_DIM), lambda b, n: (0, 0)),
                  pl.BlockSpec((L, RET_HEAD_DIM), lambda b, n: (0, 0))],
        out_specs=pl.BlockSpec((1, L, TN), lambda b, n: (b, 0, pp(n))),
        out_shape=jax.ShapeDtypeStruct((B, L, IN_COLS), bf16),
        scratch_shapes=[pltpu.VMEM((L, D), bf16), pltpu.VMEM((L + 16, TN), f32), pltpu.VMEM((L + 16, TN), f32)],
        compiler_params=_params(("arbitrary", "arbitrary")),
        name="inproj",
    )(x, ctx, modrows, g0, w_perm, mu_full, cosf, sinf)


def _ret_kernel(lg_ref, q_ref, k_ref, v_ref, o_ref, r_ref, tab_ref, *, ctx):
    L = q_ref.shape[1]
    Cc = RET_CHUNK
    hd = RET_HEAD_DIM
    lat = L - ctx
    nc = ctx // Cc
    nl = lat // Cc
    combos = [(h, d) for h in range(RET_HEADS) for d in (0, 1)]
    INTRA, CROSS, TAIL, DECAY = 0, 1, 2, 3

    @pl.when(pl.program_id(0) == 0)
    def _():
        ii = lax.broadcasted_iota(jnp.int32, (Cc, Cc), 0).astype(f32)
        jj = lax.broadcasted_iota(jnp.int32, (Cc, Cc), 1).astype(f32)
        for idx, (h, d) in enumerate(combos):
            lg = lg_ref[d, h]
            if d == 0:
                diff = ii - jj
                cross = jnp.exp(lg * (ii + 1.0))
                tailw = jnp.exp(lg * (Cc - 1.0 - ii))
            else:
                diff = jj - ii
                cross = jnp.exp(lg * (Cc - ii))
                tailw = jnp.exp(lg * ii)
            tab_ref[idx, INTRA] = jnp.where(diff >= 0, jnp.exp(lg * jnp.maximum(diff, 0.0)), 0.0)
            tab_ref[idx, CROSS] = cross
            tab_ref[idx, TAIL] = tailw
            tab_ref[idx, DECAY] = jnp.exp(jnp.zeros((Cc, Cc), f32) + lg * Cc)

    r_ref[...] = jnp.zeros(r_ref.shape, f32)
    o_ref[...] = jnp.zeros(o_ref.shape, f32)

    def cols(h):
        return slice(h * hd, (h + 1) * hd)

    def update(idx, kc, vc):
        ks = (kc.astype(f32) * tab_ref[idx, TAIL]).astype(bf16)
        r_ref[idx] = r_ref[idx] * tab_ref[idx, DECAY] + _dot_tn(ks, vc)

    def ctx_step(s, carry):
        for idx, (h, d) in enumerate(combos):
            row0 = pl.multiple_of(lat + (s * Cc if d == 0 else (nc - 1 - s) * Cc), Cc)
            update(idx, k_ref[0, pl.ds(row0, Cc), cols(h)], v_ref[0, pl.ds(row0, Cc), cols(h)])
        return carry

    def lat_step(s, carry):
        t0s, qs, ks, vs = [], [], [], []
        for h, d in combos:
            t0 = pl.multiple_of(s * Cc if d == 0 else (nl - 1 - s) * Cc, Cc)
            row0 = t0
            t0s.append(t0)
            qs.append(q_ref[0, pl.ds(row0, Cc), cols(h)])
            ks.append(k_ref[0, pl.ds(row0, Cc), cols(h)])
            vs.append(v_ref[0, pl.ds(row0, Cc), cols(h)])
        n = len(combos)
        sc = [(_dot_nt(qs[i], ks[i]) * tab_ref[i, INTRA]).astype(bf16) for i in range(n)]
        oc = [_dot(qs[i], r_ref[i].astype(bf16)) * tab_ref[i, CROSS] for i in range(n)]
        oi = [_dot(sc[i], vs[i]) for i in range(n)]
        for i, (h, d) in enumerate(combos):
            o_ref[0, pl.ds(t0s[i], Cc), cols(h)] += oi[i] + oc[i]
        for i in range(n):
            update(i, ks[i], vs[i])
        return carry

    lax.fori_loop(0, nc, ctx_step, 0)
    lax.fori_loop(0, nl, lat_step, 0)


def _retention(lg, z, ctx):
    B, L, _ = z.shape
    T = L - ctx
    W = RET_WIDTH
    Cc = RET_CHUNK
    blk = lambda c0: pl.BlockSpec((1, L, W), lambda b: (b, 0, c0 // W))
    nchain = 2 * RET_HEADS
    return pl.pallas_call(
        functools.partial(_ret_kernel, ctx=ctx),
        grid=(B,),
        in_specs=[pl.BlockSpec(memory_space=pltpu.SMEM), blk(C_RETQ), blk(C_RETK), blk(C_RETV)],
        out_specs=pl.BlockSpec((1, T, W), lambda b: (b, 0, 0)),
        out_shape=jax.ShapeDtypeStruct((B, T, W), f32),
        scratch_shapes=[pltpu.VMEM((nchain, Cc, Cc), f32), pltpu.VMEM((nchain, 4, Cc, Cc), f32)],
        compiler_params=_params(("arbitrary",)),
        name="ret",
    )(lg, z, z, z)


def _rwkv_kernel(k_ref, v_ref, r_ref, lo_ref, w0_ref, a0_ref, w2_ref, a2_ref, kk_ref, ka_ref, rk_ref,
                 o_ref, bo_ref, s_ref, *, nc, nb):
    d = pl.program_id(1)
    s = pl.program_id(2)
    C = RWKV_CHUNK
    G = 2 * RWKV_HEAD_DIM
    rev = d == 1

    @pl.when(s == 0)
    def _():
        s_ref[...] = jnp.zeros(s_ref.shape, f32)

    ii = lax.broadcasted_iota(jnp.int32, (C, G), 0)
    lane = lax.broadcasted_iota(jnp.int32, (C, G), 1)
    jj = lane & (RWKV_HEAD_DIM - 1)
    head0 = lane < RWKV_HEAD_DIM
    dlt = jnp.where(rev, ii - jj, jj - ii)
    strict = dlt < 0
    incl = dlt <= 0
    eye = (ii == jj).astype(f32)
    gi = lax.broadcasted_iota(jnp.int32, (G, G), 0)
    gj = lax.broadcasted_iota(jnp.int32, (G, G), 1)
    blockdiag = (gi >= RWKV_HEAD_DIM) == (gj >= RWKV_HEAD_DIM)
    ones_bd = blockdiag.astype(bf16)
    tri = incl[:, 0:C].astype(bf16)
    base = (ii >> 1) == (jj >> 1)
    offs = [((ii >> (lv + 1)) == (jj >> (lv + 1))) & ((ii >> lv) != (jj >> lv)) for lv in range(1, 6)]

    def segsum(x):
        return jnp.concatenate(
            [_bdot(x[:, p * G:(p + 1) * G], ones_bd) for p in range(RWKV_PAIRS)], axis=1)

    def stack(x):
        xb = x.astype(bf16)
        zero = jnp.zeros_like(xb)
        return jnp.concatenate([jnp.where(head0, xb, zero), jnp.where(head0, zero, xb)], axis=0)

    pre = []
    for bb in range(nb):
        kx = k_ref[bb].astype(f32)
        vx = v_ref[bb].astype(f32)
        rx = r_ref[bb].astype(f32)
        lo = lo_ref[bb, :, 0:DECAY_LORA + ICLR_LORA]
        u = w0_ref[0] + _dot(lo, w2_ref[0])
        softplus = jnp.maximum(-u, 0.0) + jnp.log1p(jnp.exp(-jnp.abs(u)))
        lw = -jnp.exp(-softplus - 0.5)
        a = jax.nn.sigmoid(a0_ref[0] + _dot(lo, a2_ref[0]))
        kkr = kx * kk_ref[...]
        kk = kkr * lax.rsqrt(segsum(kkr * kkr) + 1e-12)
        kd = kx * (1.0 + (a - 1.0) * ka_ref[...])
        be = kk * a
        bonus = segsum(rx * kd * rk_ref[...]) * vx
        cum = _split_dot_left(tri, lw)
        tot = jnp.where(rev, cum[0:1, :], cum[C - 1:C, :])
        gneg = jnp.exp(-cum)
        gh = jnp.exp(tot - cum)
        pre.append(dict(alb=-kk * jnp.exp(cum - lw), rb=rx * jnp.exp(cum), beb=be * gneg, kb=kd * gneg,
                        beh=be * gh, kh=kd * gh, etot=jnp.exp(tot), v=vx, bonus=bonus))

    units = [(bb, p) for bb in range(nb) for p in range(RWKV_PAIRS)]
    sl = lambda p: slice(p * G, (p + 1) * G)
    part = lambda un, name: pre[un[0]][name][:, sl(un[1])]
    S = {un: s_ref[un[0] * RWKV_PAIRS + un[1]] for un in units}
    Sb = {un: S[un].astype(bf16) for un in units}
    X = {un: part(un, "alb").astype(bf16) for un in units}
    Rb = {un: part(un, "rb").astype(bf16) for un in units}
    Ybs = {un: stack(part(un, "beb")) for un in units}
    Yks = {un: stack(part(un, "kb")) for un in units}
    Vb = {un: part(un, "v").astype(bf16) for un in units}
    Vs = {un: stack(part(un, "v")) for un in units}

    XR = {un: jnp.concatenate([X[un], Rb[un]], axis=0) for un in units}
    APb = {un: _dot_nt(XR[un], Ybs[un]) for un in units}
    APk = {un: _dot_nt(XR[un], Yks[un]) for un in units}
    BO = {un: _dot_nt(XR[un], Sb[un]) for un in units}
    Aab = {un: jnp.where(strict, APb[un][0:C], 0.0) for un in units}
    Aak = {un: jnp.where(strict, APk[un][0:C], 0.0).astype(bf16) for un in units}
    Pab = {un: jnp.where(incl, APb[un][C:2 * C], 0.0).astype(bf16) for un in units}
    Pak = {un: jnp.where(incl, APk[un][C:2 * C], 0.0).astype(bf16) for un in units}

    Tm = {un: eye + jnp.where(base, Aab[un], 0.0) for un in units}
    for off in offs:
        Xs = {un: _dot(jnp.where(off, Aab[un], 0.0).astype(bf16), stack(Tm[un])) for un in units}
        Tm = {un: Tm[un] + _dot(Tm[un].astype(bf16), stack(Xs[un])) for un in units}

    Bm = {un: BO[un][0:C] + _dot(Aak[un], Vs[un]) for un in units}
    U = {un: _dot(Tm[un].astype(bf16), stack(Bm[un])) for un in units}
    Om = {un: BO[un][C:2 * C]
          + _dot(jnp.concatenate([Pab[un], Pak[un]], axis=1), jnp.concatenate([stack(U[un]), Vs[un]], axis=0))
          for un in units}
    for un in units:
        bb, p = un
        upd = _dot_tn(jnp.concatenate([U[un].astype(bf16), Vb[un]], axis=0),
                      jnp.concatenate([part(un, "beh").astype(bf16), part(un, "kh").astype(bf16)], axis=0))
        s_ref[bb * RWKV_PAIRS + p] = jnp.where(blockdiag, S[un] * part(un, "etot") + upd, 0.0)

    @pl.when(s >= nc)
    def _():
        for bb in range(nb):
            o_ref[0, bb] = jnp.concatenate([Om[bb, p] for p in range(RWKV_PAIRS)], axis=1)
            bo_ref[0, bb] = pre[bb]["bonus"]


def _split_dot_left(w, x):
    hi = x.astype(bf16)
    lo = (x - hi.astype(f32)).astype(bf16)
    return _dot(w, hi) + _dot(w, lo)


def _rwkv(z, w0, a0, w2p, a2p, k_k, k_a, r_k, ctx):
    B, L, _ = z.shape
    T = L - ctx
    C = RWKV_CHUNK
    nc, nl = ctx // C, T // C
    W = RWKV_WIDTH

    def chunk(d, s):
        fwd = jnp.where(s < nc, nl + s, s - nc)
        bwd = jnp.where(s < nc, nl + nc - 1 - s, nl - 1 - (s - nc))
        return jnp.where(d == 0, fwd, bwd)

    def ochunk(d, s):
        sl = jnp.maximum(s - nc, 0)
        return jnp.where(d == 0, sl, nl - 1 - sl)

    nb = RWKV_SAMPLES_PER_STEP if B % RWKV_SAMPLES_PER_STEP == 0 else 1
    zblk = lambda c0, w: pl.BlockSpec((nb, C, w), lambda b, d, s: (b, chunk(d, s), c0 // w))
    dpar = lambda r: pl.BlockSpec((1, r, W), lambda b, d, s: (d, 0, 0))
    par = pl.BlockSpec((1, W), lambda b, d, s: (0, 0))
    oblk = pl.BlockSpec((1, nb, C, W), lambda b, d, s: (d, b, ochunk(d, s), 0))
    G = 2 * RWKV_HEAD_DIM
    return pl.pallas_call(
        functools.partial(_rwkv_kernel, nc=nc, nb=nb),
        grid=(B // nb, 2, nc + nl),
        in_specs=[zblk(C_RWK, W), zblk(C_RWV, W), zblk(C_RWR, W), zblk(C_LORA, 256),
                  dpar(1), dpar(1), dpar(G), dpar(G), par, par, par],
        out_specs=[oblk, oblk],
        out_shape=[jax.ShapeDtypeStruct((2, B, T, W), f32), jax.ShapeDtypeStruct((2, B, T, W), f32)],
        scratch_shapes=[pltpu.VMEM((nb * RWKV_PAIRS, G, G), f32)],
        compiler_params=_params(("arbitrary", "arbitrary", "arbitrary")),
        name="rwkv",
    )(z, z, z, z, w0, a0, w2p, a2p, k_k, k_a, r_k)


def _merge_kernel(x_ref, ret_ref, rw0_ref, rw1_ref, b0_ref, b1_ref, gate_ref, retg_ref, gd_ref, mod_ref,
                  ng_ref, gn_ref, lng_ref, lnb_ref, g2_ref, wbr_ref, wbw_ref, wout_ref, wrh_ref, wrl_ref,
                  x1_ref, h2_ref, lt_ref):
    hd = RET_HEAD_DIM
    ret = ret_ref[0]
    parts = []
    for hh in range(RET_HEADS):
        xh = ret[:, hh * hd:(hh + 1) * hd]
        mu = jnp.mean(xh, axis=-1, keepdims=True)
        dv = xh - mu
        var = jnp.mean(dv * dv, axis=-1, keepdims=True)
        parts.append(dv * lax.rsqrt(var + RET_EPS))
    yr = retg_ref[0].astype(f32) * (jnp.concatenate(parts, axis=1) * gn_ref[...])
    y_ret = _bdot(yr, wbr_ref[...])
    W = RWKV_WIDTH
    gi = lax.broadcasted_iota(jnp.int32, (W, W), 0)
    gj = lax.broadcasted_iota(jnp.int32, (W, W), 1)
    ones_bd = ((gi >> 6) == (gj >> 6)).astype(bf16)
    o = rw0_ref[0, 0] + rw1_ref[0, 0]
    mu = _split_dot(o, ones_bd) * (1.0 / RWKV_HEAD_DIM)
    dv = o - mu
    var = _bdot(dv * dv, ones_bd) * (1.0 / RWKV_HEAD_DIM)
    yw = dv * lax.rsqrt(var + RWKV_EPS) * lng_ref[...] + lnb_ref[...]
    gate = _dot(gd_ref[0], g2_ref[...])
    yw = (yw + b0_ref[0, 0] + b1_ref[0, 0]) * gate
    y_rw = _bdot(yw, wbw_ref[...])
    D = y_ret.shape[1]
    g = gate_ref[0].astype(f32)
    m = g[:, :D] * y_ret + g[:, D:] * y_rw
    y = _bdot(m, wout_ref[...])

    def rms(v, gg):
        return v * lax.rsqrt(jnp.mean(v * v, axis=-1, keepdims=True) + NORM_EPS) * gg

    x1 = x_ref[0] + mod_ref[0, 0:1, :] * rms(y, ng_ref[0:1, :])
    x1_ref[0] = x1
    h2 = rms(x1, ng_ref[1:2, :]) * (1.0 + mod_ref[0, 2:3, :]) + mod_ref[0, 1:2, :]
    h2b = h2.astype(bf16)
    h2_ref[0] = h2b
    h2l = (h2 - h2b.astype(f32)).astype(bf16)
    lgt = _dot(h2b, wrh_ref[...]) + _dot(h2l, wrh_ref[...]) + _dot(h2b, wrl_ref[...])
    lt_ref[0] = lgt.T[0:N_EXPERTS, :]


def _merge(x, ret_o, rw_o, bonus, z, mod2, ng12, gn, lng, lnb, g2, wbr, wbw, wout, wrh, wrl, ctx):
    B, T, D = x.shape
    tm = MERGE_ROWS if T % MERGE_ROWS == 0 else ROWS
    co = 0
    W = RWKV_WIDTH
    row = lambda w: pl.BlockSpec((1, tm, w), lambda b, i: (b, i, 0))
    dblk = lambda dd: pl.BlockSpec((1, 1, tm, W), lambda b, i: (dd, b, i, 0))
    zblk = lambda c0, w: pl.BlockSpec((1, tm, w), lambda b, i: (b, co + i, c0 // w))
    full = lambda a: pl.BlockSpec(a.shape, lambda b, i: (0,) * a.ndim)
    return pl.pallas_call(
        _merge_kernel,
        grid=(B, T // tm),
        in_specs=[row(D), row(W), dblk(0), dblk(1), dblk(0), dblk(1),
                  zblk(C_MERGE, 2 * D), zblk(C_RETG, W), zblk(C_LORA + 128, 128),
                  pl.BlockSpec((1, 3, D), lambda b, i: (b, 0, 0)),
                  full(ng12), full(gn), full(lng), full(lnb), full(g2), full(wbr), full(wbw), full(wout),
                  full(wrh), full(wrl)],
        out_specs=[row(D), row(D), pl.BlockSpec((1, N_EXPERTS, tm), lambda b, i: (b, 0, i))],
        out_shape=[jax.ShapeDtypeStruct((B, T, D), f32), jax.ShapeDtypeStruct((B, T, D), bf16),
                   jax.ShapeDtypeStruct((B, N_EXPERTS, T), f32)],
        compiler_params=_params(("arbitrary", "arbitrary")),
        name="merge",
    )(x, ret_o, rw_o, rw_o, bonus, bonus, z, z, z, mod2, ng12, gn, lng, lnb, g2, wbr, wbw, wout, wrh, wrl)


def _route_kernel(lt_ref, tri_ref, slot_ref, rt_ref, *, cap):
    lg = lt_ref[0]
    E, T = lg.shape
    mx = jnp.max(lg, axis=0, keepdims=True)
    ex = jnp.exp(lg - mx)
    aff = ex / jnp.sum(ex, axis=0, keepdims=True)

    def count_ge(cand):
        return jnp.sum((aff >= cand).astype(f32), axis=1, keepdims=True)

    def exp_step(_, kk):
        k_lo, k_hi = kk
        km = jnp.floor((k_lo + k_hi) * 0.5)
        ok = count_ge(jnp.exp2(-km)) >= cap
        return jnp.where(ok, k_lo, km), jnp.where(ok, km, k_hi)

    k_lo, k_hi = lax.fori_loop(0, ROUTE_EXP_STEPS, exp_step,
                               (jnp.full((E, 1), -1.0, f32), jnp.full((E, 1), ROUTE_MAX_EXP, f32)))
    lo0 = jnp.where(k_hi >= ROUTE_MAX_EXP, 0.0, jnp.exp2(-k_hi))
    hi0 = jnp.exp2(-k_lo)

    def val_step(_, lh):
        lo, hi = lh
        mid = (lo + hi) * 0.5
        ok = count_ge(mid) >= cap
        return jnp.where(ok, mid, lo), jnp.where(ok, hi, mid)

    lo, hi = lax.fori_loop(0, ROUTE_VAL_STEPS, val_step, (lo0, hi0))
    gt = aff >= hi
    eq = (aff >= lo) & (aff < hi)
    need = cap - jnp.sum(gt.astype(f32), axis=1, keepdims=True)
    tri = tri_ref[...]
    eq_before = _dot(eq.astype(bf16), tri)
    sel = gt | (eq & (eq_before < need))
    slot = _dot(sel.astype(bf16), tri)
    slot_f = jnp.where(sel, slot, -1.0)
    slot_ref[0] = slot_f.astype(jnp.int32)
    gate = jnp.where(sel, aff, 0.0)
    packed = jnp.concatenate([slot_f, gate, jnp.zeros((128 - 2 * E, T), f32)], axis=0)
    rt_ref[0] = packed.T


def _route(lt, tri, cap):
    B, E, T = lt.shape
    return pl.pallas_call(
        functools.partial(_route_kernel, cap=cap),
        grid=(B,),
        in_specs=[pl.BlockSpec((1, E, T), lambda b: (b, 0, 0)),
                  pl.BlockSpec((T, T), lambda b: (0, 0))],
        out_specs=[pl.BlockSpec((1, E, T), lambda b: (b, 0, 0)),
                   pl.BlockSpec((1, T, 128), lambda b: (b, 0, 0))],
        out_shape=[jax.ShapeDtypeStruct((B, E, T), jnp.int32), jax.ShapeDtypeStruct((B, T, 128), f32)],
        compiler_params=_params(("arbitrary",)),
        name="route",
    )(lt, tri)


def _ffn_kernel(slot_ref, h_ref, wg_ref, wu_ref, wd_ref, o_ref, wgb_ref, wub_ref, wdb_ref, *, cap):
    T = h_ref.shape[1]

    @pl.when(pl.program_id(1) == 0)
    def _():
        wgb_ref[...] = wg_ref[0].astype(bf16)
        wub_ref[...] = wu_ref[0].astype(bf16)
        wdb_ref[...] = wd_ref[0].astype(bf16)

    ci = lax.broadcasted_iota(jnp.int32, (cap, T), 0)
    onehot = (slot_ref[0, 0] == ci).astype(bf16)
    xg = _dot(onehot, h_ref[0]).astype(bf16)
    hg = _dot(xg, wgb_ref[...])
    hu = _dot(xg, wub_ref[...])
    hid = (hg * jax.nn.sigmoid(hg) * hu).astype(bf16)
    o_ref[0, 0] = _dot(hid, wdb_ref[...]).astype(bf16)


def _ffn(slot4, h2, wg, wu, wd, cap):
    B, T, D = h2.shape
    E, _, F = wg.shape
    return pl.pallas_call(
        functools.partial(_ffn_kernel, cap=cap),
        grid=(E, B),
        in_specs=[pl.BlockSpec((1, 1, 1, T), lambda e, b: (b, e, 0, 0)),
                  pl.BlockSpec((1, T, D), lambda e, b: (b, 0, 0)),
                  pl.BlockSpec((1, D, F), lambda e, b: (e, 0, 0)),
                  pl.BlockSpec((1, D, F), lambda e, b: (e, 0, 0)),
                  pl.BlockSpec((1, F, D), lambda e, b: (e, 0, 0))],
        out_specs=pl.BlockSpec((1, 1, cap, D), lambda e, b: (b, e, 0, 0)),
        out_shape=jax.ShapeDtypeStruct((B, E, cap, D), bf16),
        scratch_shapes=[pltpu.VMEM((D, F), bf16), pltpu.VMEM((D, F), bf16), pltpu.VMEM((F, D), bf16)],
        compiler_params=_params(("arbitrary", "arbitrary")),
        name="ffn",
    )(slot4, h2, wg, wu, wd)


def _combine_kernel(rt_ref, eo_ref, x1_ref, mod_ref, ng_ref, o_ref, *, cap):
    rt = rt_ref[0]
    tm = rt.shape[0]
    E = eo_ref.shape[1]
    ci = lax.broadcasted_iota(jnp.int32, (tm, cap), 1).astype(f32)
    y = jnp.zeros((tm, eo_ref.shape[3]), f32)
    for e in range(E):
        p = jnp.where(rt[:, e:e + 1] == ci, rt[:, E + e:E + e + 1], 0.0).astype(bf16)
        y = y + _dot(p, eo_ref[0, e])
    yn = y * lax.rsqrt(jnp.mean(y * y, axis=-1, keepdims=True) + NORM_EPS) * ng_ref[...]
    o_ref[0] = x1_ref[0] + mod_ref[0] * yn


def _combine(rt, eo, x1, g2mod, ng3, cap):
    B, T, D = x1.shape
    E = eo.shape[1]
    tm = 512 if T % 512 == 0 else T
    return pl.pallas_call(
        functools.partial(_combine_kernel, cap=cap),
        grid=(B, T // tm),
        in_specs=[pl.BlockSpec((1, tm, 128), lambda b, i: (b, i, 0)),
                  pl.BlockSpec((1, E, cap, D), lambda b, i: (b, 0, 0, 0)),
                  pl.BlockSpec((1, tm, D), lambda b, i: (b, i, 0)),
                  pl.BlockSpec((1, 1, D), lambda b, i: (b, 0, 0)),
                  pl.BlockSpec((1, D), lambda b, i: (0, 0))],
        out_specs=pl.BlockSpec((1, tm, D), lambda b, i: (b, i, 0)),
        out_shape=jax.ShapeDtypeStruct((B, T, D), f32),
        compiler_params=_params(("arbitrary", "arbitrary")),
        name="combine",
    )(rt, eo, x1, g2mod, ng3)


def _column_perm():
    sk, sv, rk, rv, wd, ad = 0, 512, 1024, 1536, 2048, 2112
    q0 = 2176
    rq, rg, rr, gd, mg = q0, q0 + 512, q0 + 1024, q0 + 1536, q0 + 1664
    rng = lambda a, n: np.arange(a, a + n)
    return np.concatenate([rng(mg, 2048), rng(sk, 512), rng(sv, 512), rng(rq, 512), rng(rg, 512),
                           rng(rk, 512), rng(rv, 512), rng(rr, 512), rng(wd, 64), rng(ad, 64), rng(gd, 128)])


def _rope_tables(T, CT):
    t = jnp.arange(T)
    nfreq = RET_HEAD_DIM // 4
    inv = ROPE_BASE ** (-jnp.arange(nfreq, dtype=f32) / nfreq)
    ang = jnp.concatenate([(t // GRID_W).astype(f32)[:, None] * inv,
                           (t % GRID_W).astype(f32)[:, None] * inv], axis=-1)
    cos, sin = jnp.cos(ang), jnp.sin(ang)
    cosf = jnp.concatenate([cos, cos], axis=1)
    sinf = jnp.concatenate([-sin, sin], axis=1)
    return (jnp.concatenate([cosf, jnp.ones((CT, RET_HEAD_DIM), f32)], axis=0),
            jnp.concatenate([sinf, jnp.zeros((CT, RET_HEAD_DIM), f32)], axis=0))


def kernel(x, c, ctx, c_ctx, w_mod, b_mod, norm_g, w_in, ret_log_decay, ret_gn_g, rwkv_mu, rwkv_k_k, rwkv_k_a,
           rwkv_r_k, rwkv_w0, rwkv_w2, rwkv_a0, rwkv_a2, rwkv_g2, rwkv_ln_g, rwkv_ln_b, w_br_ret, w_br_rwkv,
           w_out, w_router, w_gate, w_up, w_down):
    B, T, D = x.shape
    CT = ctx.shape[1]
    assert w_mod.shape[0] == 1 and D == D_MODEL
    assert CT % ROWS == 0 and T % ROWS == 0 and T % GRID_W == 0
    cap = CAPACITY_FACTOR * T // N_EXPERTS
    assert cap % 8 == 0

    mrows = -(-(B + 1) // 8) * 8
    cc = jnp.zeros((mrows, D), f32).at[:B].set(c).at[B].set(c_ctx)
    mod = _modulation(cc, w_mod[0], b_mod[0])
    lat = mod[:B].reshape(B, N_MOD, D)
    cm = jnp.broadcast_to(mod[B].reshape(1, N_MOD, D), (B, N_MOD, D))
    modrows = jnp.concatenate([lat[:, 0:2], cm[:, 0:2]], axis=1)

    perm = _column_perm()
    w_perm = jnp.pad(w_in[0][:, perm].astype(bf16), ((0, 0), (0, IN_COLS - USED_COLS)))
    mu = rwkv_mu[0]
    ss = 2 * RWKV_WIDTH + DECAY_LORA + ICLR_LORA
    mu_full = jnp.zeros((2, IN_COLS), f32)
    mu_full = mu_full.at[:, C_RWK:C_RWK + 1024].set(mu[:, 0:1024])
    mu_full = mu_full.at[:, C_RWR:C_RWR + 512].set(mu[:, ss:ss + 512])
    mu_full = mu_full.at[:, C_LORA:C_LORA + 128].set(mu[:, 1024:ss])
    mu_full = mu_full.at[:, C_LORA + 128:USED_COLS].set(mu[:, ss + 512:])
    cosf, sinf = _rope_tables(T, CT)
    z = _inproj(x, ctx, modrows, norm_g[0, 0:1], w_perm, mu_full, cosf, sinf)

    lg = -jnp.exp(ret_log_decay[0].astype(f32))
    ret_o = _retention(lg, z, CT)

    G = 2 * RWKV_HEAD_DIM
    w2p = jnp.zeros((2, G, RWKV_WIDTH), f32).at[:, :DECAY_LORA].set(rwkv_w2[0]).astype(bf16)
    a2p = jnp.zeros((2, G, RWKV_WIDTH), f32).at[:, DECAY_LORA:].set(rwkv_a2[0]).astype(bf16)
    rw_o, bonus = _rwkv(z, rwkv_w0[0][:, None, :], rwkv_a0[0][:, None, :], w2p, a2p,
                        rwkv_k_k[0][None], rwkv_k_a[0][None], rwkv_r_k[0][None], CT)

    mod2 = jnp.stack([lat[:, 2], lat[:, 3], lat[:, 4]], axis=1)
    wr_pad = jnp.zeros((D, 128), f32).at[:, :N_EXPERTS].set(w_router[0])
    wr_hi = wr_pad.astype(bf16)
    wr_lo = (wr_pad - wr_hi.astype(f32)).astype(bf16)
    x1, h2, lt = _merge(x, ret_o, rw_o, bonus, z, mod2, norm_g[0, 1:3], ret_gn_g[0][None], rwkv_ln_g[0][None],
                        rwkv_ln_b[0][None], rwkv_g2[0].astype(bf16), w_br_ret[0].astype(bf16),
                        w_br_rwkv[0].astype(bf16), w_out[0].astype(bf16), wr_hi, wr_lo, CT)

    ti = jnp.arange(T)
    tri = (ti[:, None] < ti[None, :]).astype(bf16)
    slot, rt = _route(lt, tri, cap)
    eo = _ffn(slot.reshape(B, N_EXPERTS, 1, T), h2, w_gate[0], w_up[0], w_down[0], cap)
    return _combine(rt, eo, x1, lat[:, 5:6], norm_g[0, 3:4], cap)
```

```python
import functools

import numpy as np
import jax
import jax.numpy as jnp
from jax import lax
from jax.experimental import pallas as pl
from jax.experimental.pallas import tpu as pltpu

f32 = jnp.float32
bf16 = jnp.bfloat16

D_MODEL = 1024
GRID_W = 64
RET_HEAD_DIM = 128
RET_WIDTH = 512
RET_HEADS = 4
RET_CHUNK = 128
RET_EPS = 1e-5
ROPE_BASE = 10000.0
RWKV_HEAD_DIM = 64
RWKV_WIDTH = 512
RWKV_PAIRS = 4
RWKV_CHUNK = 64
RWKV_SAMPLES_PER_STEP = 4
DECAY_LORA = 64
ICLR_LORA = 64
GATE_LORA = 128
RWKV_EPS = 64e-5
N_EXPERTS = 16
EXPERT_FF = 1024
CAPACITY_FACTOR = 2
N_MOD = 6
NORM_EPS = 1e-6

C_MERGE, C_RETK, C_RETV, C_RETQ, C_RETG = 0, 2048, 2560, 3072, 3584
C_RWK, C_RWV, C_RWR, C_LORA = 4096, 4608, 5120, 5632
USED_COLS = 5888
TN = 512
IN_COLS = -(-USED_COLS // TN) * TN
ROWS = 256
STEP_ROWS = 768
POST_ROWS = 128
MERGE_ROWS = 512

ROUTE_MAX_EXP = 126.0
ROUTE_EXP_STEPS = 8
ROUTE_VAL_STEPS = 26

VMEM_LIMIT = 56 * 1024 * 1024


def _dot(a, b):
    return jnp.dot(a, b, preferred_element_type=f32)


def _dot_nt(a, b):
    return lax.dot_general(a, b, (((1,), (1,)), ((), ())), preferred_element_type=f32)


def _dot_tn(a, b):
    return lax.dot_general(a, b, (((0,), (0,)), ((), ())), preferred_element_type=f32)


def _bdot(a, b):
    return _dot(a.astype(bf16), b.astype(bf16))


def _split_dot(x, w):
    hi = x.astype(bf16)
    lo = (x - hi.astype(f32)).astype(bf16)
    return _dot(hi, w) + _dot(lo, w)


def _params(sem, limit=VMEM_LIMIT):
    return pltpu.CompilerParams(dimension_semantics=sem, vmem_limit_bytes=limit)


def _mod_kernel(c_ref, w_ref, b_ref, o_ref):
    c = c_ref[...]
    s = c * jax.nn.sigmoid(c)
    o_ref[...] = _bdot(s, w_ref[...]) + b_ref[...]


def _modulation(cc, w_mod, b_mod):
    m, d = cc.shape
    n = w_mod.shape[1]
    tn = 512
    return pl.pallas_call(
        _mod_kernel,
        grid=(n // tn,),
        in_specs=[pl.BlockSpec((m, d), lambda j: (0, 0)),
                  pl.BlockSpec((d, tn), lambda j: (0, j)),
                  pl.BlockSpec((1, tn), lambda j: (0, j))],
        out_specs=pl.BlockSpec((m, tn), lambda j: (0, j)),
        out_shape=jax.ShapeDtypeStruct((m, n), f32),
        compiler_params=_params(("arbitrary",)),
        name="mod",
    )(cc, w_mod, b_mod.reshape(1, n))


def _inproj_kernel(x_ref, c_ref, mod_ref, g_ref, w_ref, mu_ref, cos_ref, sin_ref, o_ref, h_ref, z_ref):
    n = pl.program_id(1)
    T = x_ref.shape[1]
    L = T + c_ref.shape[1]
    nlat = T // ROWS
    nchunk = L // ROWS
    PAD = 8

    SR = STEP_ROWS if L % STEP_ROWS == 0 else ROWS

    @pl.when(n == 0)
    def _():
        z_ref[0:PAD, :] = jnp.zeros((PAD, TN), f32)
        z_ref[PAD + L:PAD + L + PAD, :] = jnp.zeros((PAD, TN), f32)

        def norm_chunk(src_ref, s0, r0, o):
            xb = src_ref[0, pl.ds(s0, ROWS), :]
            y = xb * lax.rsqrt(jnp.mean(xb * xb, axis=-1, keepdims=True) + NORM_EPS) * g_ref[...]
            sh = mod_ref[0, o:o + 1, :]
            sc = mod_ref[0, o + 1:o + 2, :]
            h_ref[pl.ds(r0, ROWS), :] = (y * (1.0 + sc) + sh).astype(bf16)

        def lat_body(i, carry):
            r0 = pl.multiple_of(i * ROWS, ROWS)
            norm_chunk(x_ref, r0, r0, 0)
            return carry

        lax.fori_loop(0, nlat, lat_body, 0)
        for j in range(nlat, nchunk):
            norm_chunk(c_ref, j * ROWS - T, j * ROWS, 2)

    PR = POST_ROWS

    def rope(scale):
        def post(z, r0):
            if scale != 1.0:
                z = z * scale
            cs = cos_ref[pl.ds(r0, SR), :]
            sn = sin_ref[pl.ds(r0, SR), :]
            parts = []
            for hh in range(TN // RET_HEAD_DIM):
                zh = z[:, hh * RET_HEAD_DIM:(hh + 1) * RET_HEAD_DIM]
                parts.append(zh * cs + pltpu.roll(zh, RET_HEAD_DIM // 2, 1) * sn)
            return jnp.concatenate(parts, axis=1)
        return post

    def shifted(r0):
        win = z_ref[pl.ds(r0, PR + 2 * PAD), :]
        prev = win[PAD - 1:PAD - 1 + PR]
        z = win[PAD:PAD + PR]
        nxt = win[PAD + 1:PAD + 1 + PR]
        rid = r0 + lax.broadcasted_iota(jnp.int32, (PR, 1), 0)
        prev = jnp.where(rid == T, 0.0, prev)
        nxt = jnp.where(rid == T - 1, 0.0, nxt)
        return z + mu_ref[0:1, :] * (prev - z) + mu_ref[1:2, :] * (nxt - z)

    def lora_act(zs):
        lane = lax.broadcasted_iota(jnp.int32, (1, TN), 1)
        return jnp.where(lane < DECAY_LORA, jnp.tanh(zs),
                         jnp.where(lane < DECAY_LORA + ICLR_LORA, zs, jax.nn.sigmoid(zs)))

    def product(r0):
        return _dot(h_ref[pl.ds(r0, SR), :], w_ref[...])

    def direct(post):
        def body(i, carry):
            r0 = pl.multiple_of(i * SR, SR)
            o_ref[0, pl.ds(r0, SR), :] = post(product(r0), r0).astype(bf16)
            return carry
        lax.fori_loop(0, L // SR, body, 0)

    def via_buffer(act):
        def mm_body(i, carry):
            r0 = pl.multiple_of(i * SR, SR)
            z_ref[pl.ds(PAD + r0, SR), :] = product(r0)
            return carry
        lax.fori_loop(0, L // SR, mm_body, 0)

        def post_body(i, carry):
            r0 = pl.multiple_of(i * PR, PR)
            o_ref[0, pl.ds(r0, PR), :] = act(shifted(r0)).astype(bf16)
            return carry
        lax.fori_loop(0, L // PR, post_body, 0)

    t = lambda c: c // TN
    pl.when(n < t(C_RETK))(lambda: direct(lambda z, r0: jax.nn.sigmoid(z)))
    pl.when((n >= t(C_RETK)) & (n < t(C_RETV)))(lambda: direct(rope(RET_HEAD_DIM ** -0.5)))
    pl.when((n >= t(C_RETV)) & (n < t(C_RETQ)))(lambda: direct(lambda z, r0: z))
    pl.when((n >= t(C_RETQ)) & (n < t(C_RETG)))(lambda: direct(rope(1.0)))
    pl.when((n >= t(C_RETG)) & (n < t(C_RWK)))(lambda: direct(lambda z, r0: z * jax.nn.sigmoid(z)))
    pl.when((n >= t(C_RWK)) & (n < t(C_LORA)))(lambda: via_buffer(lambda zs: zs))
    pl.when(n >= t(C_LORA))(lambda: via_buffer(lora_act))


def _inproj(x, ctx, modrows, g0, w_perm, mu_full, cosf, sinf):
    B, T, D = x.shape
    CT = ctx.shape[1]
    L = T + CT
    nt = IN_COLS // TN
    return pl.pallas_call(
        _inproj_kernel,
        grid=(B, nt),
        in_specs=[pl.BlockSpec((1, T, D), lambda b, n: (b, 0, 0)),
                  pl.BlockSpec((1, CT, D), lambda b, n: (b, 0, 0)),
                  pl.BlockSpec((1, 4, D), lambda b, n: (b, 0, 0)),
                  pl.BlockSpec((1, D), lambda b, n: (0, 0)),
                  pl.BlockSpec((D, TN), lambda b, n: (0, n)),
                  pl.BlockSpec((2, TN), lambda b, n: (0, n)),
                  pl.BlockSpec((L, RET_HEAD_DIM), lambda b, n: (0, 0)),
                  pl.BlockSpec((L, RET_HEAD_DIM), lambda b, n: (0, 0))],
        out_specs=pl.BlockSpec((1, L, TN), lambda b, n: (b, 0, n)),
        out_shape=jax.ShapeDtypeStruct((B, L, IN_COLS), bf16),
        scratch_shapes=[pltpu.VMEM((L, D), bf16), pltpu.VMEM((L + 16, TN), f32)],
        compiler_params=_params(("arbitrary", "arbitrary")),
        name="inproj",
    )(x, ctx, modrows, g0, w_perm, mu_full, cosf, sinf)


def _ret_kernel(lg_ref, q_ref, k_ref, v_ref, o_ref, r_ref, tab_ref, *, ctx):
    L = q_ref.shape[1]
    Cc = RET_CHUNK
    hd = RET_HEAD_DIM
    lat = L - ctx
    nc = ctx // Cc
    nl = lat // Cc
    combos = [(h, d) for h in range(RET_HEADS) for d in (0, 1)]
    INTRA, CROSS, TAIL, DECAY = 0, 1, 2, 3

    @pl.when(pl.program_id(0) == 0)
    def _():
        ii = lax.broadcasted_iota(jnp.int32, (Cc, Cc), 0).astype(f32)
        jj = lax.broadcasted_iota(jnp.int32, (Cc, Cc), 1).astype(f32)
        for idx, (h, d) in enumerate(combos):
            lg = lg_ref[d, h]
            if d == 0:
                diff = ii - jj
                cross = jnp.exp(lg * (ii + 1.0))
                tailw = jnp.exp(lg * (Cc - 1.0 - ii))
            else:
                diff = jj - ii
                cross = jnp.exp(lg * (Cc - ii))
                tailw = jnp.exp(lg * ii)
            tab_ref[idx, INTRA] = jnp.where(diff >= 0, jnp.exp(lg * jnp.maximum(diff, 0.0)), 0.0)
            tab_ref[idx, CROSS] = cross
            tab_ref[idx, TAIL] = tailw
            tab_ref[idx, DECAY] = jnp.exp(jnp.zeros((Cc, Cc), f32) + lg * Cc)

    r_ref[...] = jnp.zeros(r_ref.shape, f32)
    o_ref[...] = jnp.zeros(o_ref.shape, f32)

    def cols(h):
        return slice(h * hd, (h + 1) * hd)

    def update(idx, kc, vc):
        ks = (kc.astype(f32) * tab_ref[idx, TAIL]).astype(bf16)
        r_ref[idx] = r_ref[idx] * tab_ref[idx, DECAY] + _dot_tn(ks, vc)

    def ctx_step(s, carry):
        for idx, (h, d) in enumerate(combos):
            row0 = pl.multiple_of(lat + (s * Cc if d == 0 else (nc - 1 - s) * Cc), Cc)
            update(idx, k_ref[0, pl.ds(row0, Cc), cols(h)], v_ref[0, pl.ds(row0, Cc), cols(h)])
        return carry

    def lat_step(s, carry):
        t0s, qs, ks, vs = [], [], [], []
        for h, d in combos:
            t0 = pl.multiple_of(s * Cc if d == 0 else (nl - 1 - s) * Cc, Cc)
            row0 = t0
            t0s.append(t0)
            qs.append(q_ref[0, pl.ds(row0, Cc), cols(h)])
            ks.append(k_ref[0, pl.ds(row0, Cc), cols(h)])
            vs.append(v_ref[0, pl.ds(row0, Cc), cols(h)])
        n = len(combos)
        sc = [(_dot_nt(qs[i], ks[i]) * tab_ref[i, INTRA]).astype(bf16) for i in range(n)]
        oc = [_dot(qs[i], r_ref[i].astype(bf16)) * tab_ref[i, CROSS] for i in range(n)]
        oi = [_dot(sc[i], vs[i]) for i in range(n)]
        for i, (h, d) in enumerate(combos):
            o_ref[0, pl.ds(t0s[i], Cc), cols(h)] += oi[i] + oc[i]
        for i in range(n):
            update(i, ks[i], vs[i])
        return carry

    lax.fori_loop(0, nc, ctx_step, 0)
    lax.fori_loop(0, nl, lat_step, 0)


def _retention(lg, z, ctx):
    B, L, _ = z.shape
    T = L - ctx
    W = RET_WIDTH
    Cc = RET_CHUNK
    blk = lambda c0: pl.BlockSpec((1, L, W), lambda b: (b, 0, c0 // W))
    nchain = 2 * RET_HEADS
    return pl.pallas_call(
        functools.partial(_ret_kernel, ctx=ctx),
        grid=(B,),
        in_specs=[pl.BlockSpec(memory_space=pltpu.SMEM), blk(C_RETQ), blk(C_RETK), blk(C_RETV)],
        out_specs=pl.BlockSpec((1, T, W), lambda b: (b, 0, 0)),
        out_shape=jax.ShapeDtypeStruct((B, T, W), f32),
        scratch_shapes=[pltpu.VMEM((nchain, Cc, Cc), f32), pltpu.VMEM((nchain, 4, Cc, Cc), f32)],
        compiler_params=_params(("arbitrary",)),
        name="ret",
    )(lg, z, z, z)


def _rwkv_kernel(k_ref, v_ref, r_ref, lo_ref, w0_ref, a0_ref, w2_ref, a2_ref, kk_ref, ka_ref, rk_ref,
                 o_ref, bo_ref, s_ref, *, nc, nb):
    d = pl.program_id(1)
    s = pl.program_id(2)
    C = RWKV_CHUNK
    G = 2 * RWKV_HEAD_DIM
    rev = d == 1

    @pl.when(s == 0)
    def _():
        s_ref[...] = jnp.zeros(s_ref.shape, f32)

    ii = lax.broadcasted_iota(jnp.int32, (C, G), 0)
    lane = lax.broadcasted_iota(jnp.int32, (C, G), 1)
    jj = lane & (RWKV_HEAD_DIM - 1)
    head0 = lane < RWKV_HEAD_DIM
    dlt = jnp.where(rev, ii - jj, jj - ii)
    strict = dlt < 0
    incl = dlt <= 0
    eye = (ii == jj).astype(f32)
    gi = lax.broadcasted_iota(jnp.int32, (G, G), 0)
    gj = lax.broadcasted_iota(jnp.int32, (G, G), 1)
    blockdiag = (gi >= RWKV_HEAD_DIM) == (gj >= RWKV_HEAD_DIM)
    ones_bd = blockdiag.astype(bf16)
    tri = incl[:, 0:C].astype(bf16)
    base = (ii >> 1) == (jj >> 1)
    offs = [((ii >> (lv + 1)) == (jj >> (lv + 1))) & ((ii >> lv) != (jj >> lv)) for lv in range(1, 6)]

    def segsum(x):
        return jnp.concatenate(
            [_bdot(x[:, p * G:(p + 1) * G], ones_bd) for p in range(RWKV_PAIRS)], axis=1)

    def stack(x):
        xb = x.astype(bf16)
        zero = jnp.zeros_like(xb)
        return jnp.concatenate([jnp.where(head0, xb, zero), jnp.where(head0, zero, xb)], axis=0)

    pre = []
    for bb in range(nb):
        kx = k_ref[bb].astype(f32)
        vx = v_ref[bb].astype(f32)
        rx = r_ref[bb].astype(f32)
        lo = lo_ref[bb, :, 0:DECAY_LORA + ICLR_LORA]
        u = w0_ref[0] + _dot(lo, w2_ref[0])
        softplus = jnp.maximum(-u, 0.0) + jnp.log1p(jnp.exp(-jnp.abs(u)))
        lw = -jnp.exp(-softplus - 0.5)
        a = jax.nn.sigmoid(a0_ref[0] + _dot(lo, a2_ref[0]))
        kkr = kx * kk_ref[...]
        kk = kkr * lax.rsqrt(segsum(kkr * kkr) + 1e-12)
        kd = kx * (1.0 + (a - 1.0) * ka_ref[...])
        be = kk * a
        bonus = segsum(rx * kd * rk_ref[...]) * vx
        cum = _split_dot_left(tri, lw)
        tot = jnp.where(rev, cum[0:1, :], cum[C - 1:C, :])
        gneg = jnp.exp(-cum)
        gh = jnp.exp(tot - cum)
        pre.append(dict(alb=-kk * jnp.exp(cum - lw), rb=rx * jnp.exp(cum), beb=be * gneg, kb=kd * gneg,
                        beh=be * gh, kh=kd * gh, etot=jnp.exp(tot), v=vx, bonus=bonus))

    units = [(bb, p) for bb in range(nb) for p in range(RWKV_PAIRS)]
    sl = lambda p: slice(p * G, (p + 1) * G)
    part = lambda un, name: pre[un[0]][name][:, sl(un[1])]
    S = {un: s_ref[un[0] * RWKV_PAIRS + un[1]] for un in units}
    Sb = {un: S[un].astype(bf16) for un in units}
    X = {un: part(un, "alb").astype(bf16) for un in units}
    Rb = {un: part(un, "rb").astype(bf16) for un in units}
    Ybs = {un: stack(part(un, "beb")) for un in units}
    Yks = {un: stack(part(un, "kb")) for un in units}
    Vb = {un: part(un, "v").astype(bf16) for un in units}
    Vs = {un: stack(part(un, "v")) for un in units}

    XR = {un: jnp.concatenate([X[un], Rb[un]], axis=0) for un in units}
    APb = {un: _dot_nt(XR[un], Ybs[un]) for un in units}
    APk = {un: _dot_nt(XR[un], Yks[un]) for un in units}
    BO = {un: _dot_nt(XR[un], Sb[un]) for un in units}
    Aab = {un: jnp.where(strict, APb[un][0:C], 0.0) for un in units}
    Aak = {un: jnp.where(strict, APk[un][0:C], 0.0).astype(bf16) for un in units}
    Pab = {un: jnp.where(incl, APb[un][C:2 * C], 0.0).astype(bf16) for un in units}
    Pak = {un: jnp.where(incl, APk[un][C:2 * C], 0.0).astype(bf16) for un in units}

    Tm = {un: eye + jnp.where(base, Aab[un], 0.0) for un in units}
    for off in offs:
        Xs = {un: _dot(jnp.where(off, Aab[un], 0.0).astype(bf16), stack(Tm[un])) for un in units}
        Tm = {un: Tm[un] + _dot(Tm[un].astype(bf16), stack(Xs[un])) for un in units}

    Bm = {un: BO[un][0:C] + _dot(Aak[un], Vs[un]) for un in units}
    U = {un: _dot(Tm[un].astype(bf16), stack(Bm[un])) for un in units}
    Om = {un: BO[un][C:2 * C]
          + _dot(jnp.concatenate([Pab[un], Pak[un]], axis=1), jnp.concatenate([stack(U[un]), Vs[un]], axis=0))
          for un in units}
    for un in units:
        bb, p = un
        upd = _dot_tn(jnp.concatenate([U[un].astype(bf16), Vb[un]], axis=0),
                      jnp.concatenate([part(un, "beh").astype(bf16), part(un, "kh").astype(bf16)], axis=0))
        s_ref[bb * RWKV_PAIRS + p] = jnp.where(blockdiag, S[un] * part(un, "etot") + upd, 0.0)

    @pl.when(s >= nc)
    def _():
        for bb in range(nb):
            o_ref[0, bb] = jnp.concatenate([Om[bb, p] for p in range(RWKV_PAIRS)], axis=1)
            bo_ref[0, bb] = pre[bb]["bonus"]


def _split_dot_left(w, x):
    hi = x.astype(bf16)
    lo = (x - hi.astype(f32)).astype(bf16)
    return _dot(w, hi) + _dot(w, lo)


def _rwkv(z, w0, a0, w2p, a2p, k_k, k_a, r_k, ctx):
    B, L, _ = z.shape
    T = L - ctx
    C = RWKV_CHUNK
    nc, nl = ctx // C, T // C
    W = RWKV_WIDTH

    def chunk(d, s):
        fwd = jnp.where(s < nc, nl + s, s - nc)
        bwd = jnp.where(s < nc, nl + nc - 1 - s, nl - 1 - (s - nc))
        return jnp.where(d == 0, fwd, bwd)

    def ochunk(d, s):
        sl = jnp.maximum(s - nc, 0)
        return jnp.where(d == 0, sl, nl - 1 - sl)

    nb = RWKV_SAMPLES_PER_STEP if B % RWKV_SAMPLES_PER_STEP == 0 else 1
    zblk = lambda c0, w: pl.BlockSpec((nb, C, w), lambda b, d, s: (b, chunk(d, s), c0 // w))
    dpar = lambda r: pl.BlockSpec((1, r, W), lambda b, d, s: (d, 0, 0))
    par = pl.BlockSpec((1, W), lambda b, d, s: (0, 0))
    oblk = pl.BlockSpec((1, nb, C, W), lambda b, d, s: (d, b, ochunk(d, s), 0))
    G = 2 * RWKV_HEAD_DIM
    return pl.pallas_call(
        functools.partial(_rwkv_kernel, nc=nc, nb=nb),
        grid=(B // nb, 2, nc + nl),
        in_specs=[zblk(C_RWK, W), zblk(C_RWV, W), zblk(C_RWR, W), zblk(C_LORA, 256),
                  dpar(1), dpar(1), dpar(G), dpar(G), par, par, par],
        out_specs=[oblk, oblk],
        out_shape=[jax.ShapeDtypeStruct((2, B, T, W), f32), jax.ShapeDtypeStruct((2, B, T, W), f32)],
        scratch_shapes=[pltpu.VMEM((nb * RWKV_PAIRS, G, G), f32)],
        compiler_params=_params(("arbitrary", "arbitrary", "arbitrary")),
        name="rwkv",
    )(z, z, z, z, w0, a0, w2p, a2p, k_k, k_a, r_k)


def _merge_kernel(x_ref, ret_ref, rw0_ref, rw1_ref, b0_ref, b1_ref, gate_ref, retg_ref, gd_ref, mod_ref,
                  ng_ref, gn_ref, lng_ref, lnb_ref, g2_ref, wbr_ref, wbw_ref, wout_ref, wrh_ref, wrl_ref,
                  x1_ref, h2_ref, lt_ref):
    hd = RET_HEAD_DIM
    ret = ret_ref[0]
    parts = []
    for hh in range(RET_HEADS):
        xh = ret[:, hh * hd:(hh + 1) * hd]
        mu = jnp.mean(xh, axis=-1, keepdims=True)
        dv = xh - mu
        var = jnp.mean(dv * dv, axis=-1, keepdims=True)
        parts.append(dv * lax.rsqrt(var + RET_EPS))
    yr = retg_ref[0].astype(f32) * (jnp.concatenate(parts, axis=1) * gn_ref[...])
    y_ret = _bdot(yr, wbr_ref[...])
    W = RWKV_WIDTH
    gi = lax.broadcasted_iota(jnp.int32, (W, W), 0)
    gj = lax.broadcasted_iota(jnp.int32, (W, W), 1)
    ones_bd = ((gi >> 6) == (gj >> 6)).astype(bf16)
    o = rw0_ref[0, 0] + rw1_ref[0, 0]
    mu = _split_dot(o, ones_bd) * (1.0 / RWKV_HEAD_DIM)
    dv = o - mu
    var = _bdot(dv * dv, ones_bd) * (1.0 / RWKV_HEAD_DIM)
    yw = dv * lax.rsqrt(var + RWKV_EPS) * lng_ref[...] + lnb_ref[...]
    gate = _dot(gd_ref[0], g2_ref[...])
    yw = (yw + b0_ref[0, 0] + b1_ref[0, 0]) * gate
    y_rw = _bdot(yw, wbw_ref[...])
    D = y_ret.shape[1]
    g = gate_ref[0].astype(f32)
    m = g[:, :D] * y_ret + g[:, D:] * y_rw
    y = _bdot(m, wout_ref[...])

    def rms(v, gg):
        return v * lax.rsqrt(jnp.mean(v * v, axis=-1, keepdims=True) + NORM_EPS) * gg

    x1 = x_ref[0] + mod_ref[0, 0:1, :] * rms(y, ng_ref[0:1, :])
    x1_ref[0] = x1
    h2 = rms(x1, ng_ref[1:2, :]) * (1.0 + mod_ref[0, 2:3, :]) + mod_ref[0, 1:2, :]
    h2b = h2.astype(bf16)
    h2_ref[0] = h2b
    h2l = (h2 - h2b.astype(f32)).astype(bf16)
    lgt = _dot(h2b, wrh_ref[...]) + _dot(h2l, wrh_ref[...]) + _dot(h2b, wrl_ref[...])
    lt_ref[0] = lgt.T[0:N_EXPERTS, :]


def _merge(x, ret_o, rw_o, bonus, z, mod2, ng12, gn, lng, lnb, g2, wbr, wbw, wout, wrh, wrl, ctx):
    B, T, D = x.shape
    tm = MERGE_ROWS if T % MERGE_ROWS == 0 else ROWS
    co = 0
    W = RWKV_WIDTH
    row = lambda w: pl.BlockSpec((1, tm, w), lambda b, i: (b, i, 0))
    dblk = lambda dd: pl.BlockSpec((1, 1, tm, W), lambda b, i: (dd, b, i, 0))
    zblk = lambda c0, w: pl.BlockSpec((1, tm, w), lambda b, i: (b, co + i, c0 // w))
    full = lambda a: pl.BlockSpec(a.shape, lambda b, i: (0,) * a.ndim)
    return pl.pallas_call(
        _merge_kernel,
        grid=(B, T // tm),
        in_specs=[row(D), row(W), dblk(0), dblk(1), dblk(0), dblk(1),
                  zblk(C_MERGE, 2 * D), zblk(C_RETG, W), zblk(C_LORA + 128, 128),
                  pl.BlockSpec((1, 3, D), lambda b, i: (b, 0, 0)),
                  full(ng12), full(gn), full(lng), full(lnb), full(g2), full(wbr), full(wbw), full(wout),
                  full(wrh), full(wrl)],
        out_specs=[row(D), row(D), pl.BlockSpec((1, N_EXPERTS, tm), lambda b, i: (b, 0, i))],
        out_shape=[jax.ShapeDtypeStruct((B, T, D), f32), jax.ShapeDtypeStruct((B, T, D), bf16),
                   jax.ShapeDtypeStruct((B, N_EXPERTS, T), f32)],
        compiler_params=_params(("arbitrary", "arbitrary")),
        name="merge",
    )(x, ret_o, rw_o, rw_o, bonus, bonus, z, z, z, mod2, ng12, gn, lng, lnb, g2, wbr, wbw, wout, wrh, wrl)


def _route_kernel(lt_ref, tri_ref, slot_ref, rt_ref, *, cap):
    lg = lt_ref[0]
    E, T = lg.shape
    mx = jnp.max(lg, axis=0, keepdims=True)
    ex = jnp.exp(lg - mx)
    aff = ex / jnp.sum(ex, axis=0, keepdims=True)

    def count_ge(cand):
        return jnp.sum((aff >= cand).astype(f32), axis=1, keepdims=True)

    def exp_step(_, kk):
        k_lo, k_hi = kk
        km = jnp.floor((k_lo + k_hi) * 0.5)
        ok = count_ge(jnp.exp2(-km)) >= cap
        return jnp.where(ok, k_lo, km), jnp.where(ok, km, k_hi)

    k_lo, k_hi = lax.fori_loop(0, ROUTE_EXP_STEPS, exp_step,
                               (jnp.full((E, 1), -1.0, f32), jnp.full((E, 1), ROUTE_MAX_EXP, f32)))
    lo0 = jnp.where(k_hi >= ROUTE_MAX_EXP, 0.0, jnp.exp2(-k_hi))
    hi0 = jnp.exp2(-k_lo)

    def val_step(_, lh):
        lo, hi = lh
        mid = (lo + hi) * 0.5
        ok = count_ge(mid) >= cap
        return jnp.where(ok, mid, lo), jnp.where(ok, hi, mid)

    lo, hi = lax.fori_loop(0, ROUTE_VAL_STEPS, val_step, (lo0, hi0))
    gt = aff >= hi
    eq = (aff >= lo) & (aff < hi)
    need = cap - jnp.sum(gt.astype(f32), axis=1, keepdims=True)
    tri = tri_ref[...]
    eq_before = _dot(eq.astype(bf16), tri)
    sel = gt | (eq & (eq_before < need))
    slot = _dot(sel.astype(bf16), tri)
    slot_f = jnp.where(sel, slot, -1.0)
    slot_ref[0] = slot_f.astype(jnp.int32)
    gate = jnp.where(sel, aff, 0.0)
    packed = jnp.concatenate([slot_f, gate, jnp.zeros((128 - 2 * E, T), f32)], axis=0)
    rt_ref[0] = packed.T


def _route(lt, tri, cap):
    B, E, T = lt.shape
    return pl.pallas_call(
        functools.partial(_route_kernel, cap=cap),
        grid=(B,),
        in_specs=[pl.BlockSpec((1, E, T), lambda b: (b, 0, 0)),
                  pl.BlockSpec((T, T), lambda b: (0, 0))],
        out_specs=[pl.BlockSpec((1, E, T), lambda b: (b, 0, 0)),
                   pl.BlockSpec((1, T, 128), lambda b: (b, 0, 0))],
        out_shape=[jax.ShapeDtypeStruct((B, E, T), jnp.int32), jax.ShapeDtypeStruct((B, T, 128), f32)],
        compiler_params=_params(("arbitrary",)),
        name="route",
    )(lt, tri)


def _ffn_kernel(slot_ref, h_ref, wg_ref, wu_ref, wd_ref, o_ref, wgb_ref, wub_ref, wdb_ref, *, cap):
    T = h_ref.shape[1]

    @pl.when(pl.program_id(1) == 0)
    def _():
        wgb_ref[...] = wg_ref[0].astype(bf16)
        wub_ref[...] = wu_ref[0].astype(bf16)
        wdb_ref[...] = wd_ref[0].astype(bf16)

    ci = lax.broadcasted_iota(jnp.int32, (cap, T), 0)
    onehot = (slot_ref[0, 0] == ci).astype(bf16)
    xg = _dot(onehot, h_ref[0]).astype(bf16)
    hg = _dot(xg, wgb_ref[...])
    hu = _dot(xg, wub_ref[...])
    hid = (hg * jax.nn.sigmoid(hg) * hu).astype(bf16)
    o_ref[0, 0] = _dot(hid, wdb_ref[...]).astype(bf16)


def _ffn(slot4, h2, wg, wu, wd, cap):
    B, T, D = h2.shape
    E, _, F = wg.shape
    return pl.pallas_call(
        functools.partial(_ffn_kernel, cap=cap),
        grid=(E, B),
        in_specs=[pl.BlockSpec((1, 1, 1, T), lambda e, b: (b, e, 0, 0)),
                  pl.BlockSpec((1, T, D), lambda e, b: (b, 0, 0)),
                  pl.BlockSpec((1, D, F), lambda e, b: (e, 0, 0)),
                  pl.BlockSpec((1, D, F), lambda e, b: (e, 0, 0)),
                  pl.BlockSpec((1, F, D), lambda e, b: (e, 0, 0))],
        out_specs=pl.BlockSpec((1, 1, cap, D), lambda e, b: (b, e, 0, 0)),
        out_shape=jax.ShapeDtypeStruct((B, E, cap, D), bf16),
        scratch_shapes=[pltpu.VMEM((D, F), bf16), pltpu.VMEM((D, F), bf16), pltpu.VMEM((F, D), bf16)],
        compiler_params=_params(("arbitrary", "arbitrary")),
        name="ffn",
    )(slot4, h2, wg, wu, wd)


def _combine_kernel(rt_ref, eo_ref, x1_ref, mod_ref, ng_ref, o_ref, *, cap):
    rt = rt_ref[0]
    tm = rt.shape[0]
    E = eo_ref.shape[1]
    ci = lax.broadcasted_iota(jnp.int32, (tm, cap), 1).astype(f32)
    y = jnp.zeros((tm, eo_ref.shape[3]), f32)
    for e in range(E):
        p = jnp.where(rt[:, e:e + 1] == ci, rt[:, E + e:E + e + 1], 0.0).astype(bf16)
        y = y + _dot(p, eo_ref[0, e])
    yn = y * lax.rsqrt(jnp.mean(y * y, axis=-1, keepdims=True) + NORM_EPS) * ng_ref[...]
    o_ref[0] = x1_ref[0] + mod_ref[0] * yn


def _combine(rt, eo, x1, g2mod, ng3, cap):
    B, T, D = x1.shape
    E = eo.shape[1]
    tm = 512 if T % 512 == 0 else T
    return pl.pallas_call(
        functools.partial(_combine_kernel, cap=cap),
        grid=(B, T // tm),
        in_specs=[pl.BlockSpec((1, tm, 128), lambda b, i: (b, i, 0)),
                  pl.BlockSpec((1, E, cap, D), lambda b, i: (b, 0, 0, 0)),
                  pl.BlockSpec((1, tm, D), lambda b, i: (b, i, 0)),
                  pl.BlockSpec((1, 1, D), lambda b, i: (b, 0, 0)),
                  pl.BlockSpec((1, D), lambda b, i: (0, 0))],
        out_specs=pl.BlockSpec((1, tm, D), lambda b, i: (b, i, 0)),
        out_shape=jax.ShapeDtypeStruct((B, T, D), f32),
        compiler_params=_params(("arbitrary", "arbitrary")),
        name="combine",
    )(rt, eo, x1, g2mod, ng3)


def _column_perm():
    sk, sv, rk, rv, wd, ad = 0, 512, 1024, 1536, 2048, 2112
    q0 = 2176
    rq, rg, rr, gd, mg = q0, q0 + 512, q0 + 1024, q0 + 1536, q0 + 1664
    rng = lambda a, n: np.arange(a, a + n)
    return np.concatenate([rng(mg, 2048), rng(sk, 512), rng(sv, 512), rng(rq, 512), rng(rg, 512),
                           rng(rk, 512), rng(rv, 512), rng(rr, 512), rng(wd, 64), rng(ad, 64), rng(gd, 128)])


def _rope_tables(T, CT):
    t = jnp.arange(T)
    nfreq = RET_HEAD_DIM // 4
    inv = ROPE_BASE ** (-jnp.arange(nfreq, dtype=f32) / nfreq)
    ang = jnp.concatenate([(t // GRID_W).astype(f32)[:, None] * inv,
                           (t % GRID_W).astype(f32)[:, None] * inv], axis=-1)
    cos, sin = jnp.cos(ang), jnp.sin(ang)
    cosf = jnp.concatenate([cos, cos], axis=1)
    sinf = jnp.concatenate([-sin, sin], axis=1)
    return (jnp.concatenate([cosf, jnp.ones((CT, RET_HEAD_DIM), f32)], axis=0),
            jnp.concatenate([sinf, jnp.zeros((CT, RET_HEAD_DIM), f32)], axis=0))


def kernel(x, c, ctx, c_ctx, w_mod, b_mod, norm_g, w_in, ret_log_decay, ret_gn_g, rwkv_mu, rwkv_k_k, rwkv_k_a,
           rwkv_r_k, rwkv_w0, rwkv_w2, rwkv_a0, rwkv_a2, rwkv_g2, rwkv_ln_g, rwkv_ln_b, w_br_ret, w_br_rwkv,
           w_out, w_router, w_gate, w_up, w_down):
    B, T, D = x.shape
    CT = ctx.shape[1]
    assert w_mod.shape[0] == 1 and D == D_MODEL
    assert CT % ROWS == 0 and T % ROWS == 0 and T % GRID_W == 0
    cap = CAPACITY_FACTOR * T // N_EXPERTS
    assert cap % 8 == 0

    mrows = -(-(B + 1) // 8) * 8
    cc = jnp.zeros((mrows, D), f32).at[:B].set(c).at[B].set(c_ctx)
    mod = _modulation(cc, w_mod[0], b_mod[0])
    lat = mod[:B].reshape(B, N_MOD, D)
    cm = jnp.broadcast_to(mod[B].reshape(1, N_MOD, D), (B, N_MOD, D))
    modrows = jnp.concatenate([lat[:, 0:2], cm[:, 0:2]], axis=1)

    perm = _column_perm()
    w_perm = jnp.pad(w_in[0][:, perm].astype(bf16), ((0, 0), (0, IN_COLS - USED_COLS)))
    mu = rwkv_mu[0]
    ss = 2 * RWKV_WIDTH + DECAY_LORA + ICLR_LORA
    mu_full = jnp.zeros((2, IN_COLS), f32)
    mu_full = mu_full.at[:, C_RWK:C_RWK + 1024].set(mu[:, 0:1024])
    mu_full = mu_full.at[:, C_RWR:C_RWR + 512].set(mu[:, ss:ss + 512])
    mu_full = mu_full.at[:, C_LORA:C_LORA + 128].set(mu[:, 1024:ss])
    mu_full = mu_full.at[:, C_LORA + 128:USED_COLS].set(mu[:, ss + 512:])
    cosf, sinf = _rope_tables(T, CT)
    z = _inproj(x, ctx, modrows, norm_g[0, 0:1], w_perm, mu_full, cosf, sinf)

    lg = -jnp.exp(ret_log_decay[0].astype(f32))
    ret_o = _retention(lg, z, CT)

    G = 2 * RWKV_HEAD_DIM
    w2p = jnp.zeros((2, G, RWKV_WIDTH), f32).at[:, :DECAY_LORA].set(rwkv_w2[0]).astype(bf16)
    a2p = jnp.zeros((2, G, RWKV_WIDTH), f32).at[:, DECAY_LORA:].set(rwkv_a2[0]).astype(bf16)
    rw_o, bonus = _rwkv(z, rwkv_w0[0][:, None, :], rwkv_a0[0][:, None, :], w2p, a2p,
                        rwkv_k_k[0][None], rwkv_k_a[0][None], rwkv_r_k[0][None], CT)

    mod2 = jnp.stack([lat[:, 2], lat[:, 3], lat[:, 4]], axis=1)
    wr_pad = jnp.zeros((D, 128), f32).at[:, :N_EXPERTS].set(w_router[0])
    wr_hi = wr_pad.astype(bf16)
    wr_lo = (wr_pad - wr_hi.astype(f32)).astype(bf16)
    x1, h2, lt = _merge(x, ret_o, rw_o, bonus, z, mod2, norm_g[0, 1:3], ret_gn_g[0][None], rwkv_ln_g[0][None],
                        rwkv_ln_b[0][None], rwkv_g2[0].astype(bf16), w_br_ret[0].astype(bf16),
                        w_br_rwkv[0].astype(bf16), w_out[0].astype(bf16), wr_hi, wr_lo, CT)

    ti = jnp.arange(T)
    tri = (ti[:, None] < ti[None, :]).astype(bf16)
    slot, rt = _route(lt, tri, cap)
    eo = _ffn(slot.reshape(B, N_EXPERTS, 1, T), h2, w_gate[0], w_up[0], w_down[0], cap)
    return _combine(rt, eo, x1, lat[:, 5:6], norm_g[0, 3:4], cap)
```

```python
import functools

import numpy as np
import jax
import jax.numpy as jnp
from jax import lax
from jax.experimental import pallas as pl
from jax.experimental.pallas import tpu as pltpu

f32 = jnp.float32
bf16 = jnp.bfloat16

D_MODEL = 1024
GRID_W = 64
RET_HEAD_DIM = 128
RET_WIDTH = 512
RET_HEADS = 4
RET_CHUNK = 128
RET_EPS = 1e-5
ROPE_BASE = 10000.0
RWKV_HEAD_DIM = 64
RWKV_WIDTH = 512
RWKV_PAIRS = 4
RWKV_CHUNK = 64
RWKV_SAMPLES_PER_STEP = 4
DECAY_LORA = 64
ICLR_LORA = 64
GATE_LORA = 128
RWKV_EPS = 64e-5
N_EXPERTS = 16
EXPERT_FF = 1024
CAPACITY_FACTOR = 2
N_MOD = 6
NORM_EPS = 1e-6

C_MERGE, C_RETK, C_RETV, C_RETQ, C_RETG = 0, 2048, 2560, 3072, 3584
C_RWK, C_RWV, C_RWR, C_LORA = 4096, 4608, 5120, 5632
USED_COLS = 5888
TN = 512
IN_COLS = -(-USED_COLS // TN) * TN
ROWS = 256
STEP_ROWS = 768
POST_ROWS = 128
MERGE_ROWS = 512

ROUTE_MAX_EXP = 126.0
ROUTE_EXP_STEPS = 8
ROUTE_VAL_STEPS = 26

VMEM_LIMIT = 56 * 1024 * 1024


def _dot(a, b):
    return jnp.dot(a, b, preferred_element_type=f32)


def _dot_nt(a, b):
    return lax.dot_general(a, b, (((1,), (1,)), ((), ())), preferred_element_type=f32)


def _dot_tn(a, b):
    return lax.dot_general(a, b, (((0,), (0,)), ((), ())), preferred_element_type=f32)


def _bdot(a, b):
    return _dot(a.astype(bf16), b.astype(bf16))


def _split_dot(x, w):
    hi = x.astype(bf16)
    lo = (x - hi.astype(f32)).astype(bf16)
    return _dot(hi, w) + _dot(lo, w)


def _sigmoid(x):
    return 0.5 * jnp.tanh(0.5 * x) + 0.5


def _params(sem, limit=VMEM_LIMIT):
    return pltpu.CompilerParams(dimension_semantics=sem, vmem_limit_bytes=limit)


def _mod_kernel(c_ref, w_ref, b_ref, o_ref):
    c = c_ref[...]
    s = c * jax.nn.sigmoid(c)
    o_ref[...] = _bdot(s, w_ref[...]) + b_ref[...]


def _modulation(cc, w_mod, b_mod):
    m, d = cc.shape
    n = w_mod.shape[1]
    tn = 512
    return pl.pallas_call(
        _mod_kernel,
        grid=(n // tn,),
        in_specs=[pl.BlockSpec((m, d), lambda j: (0, 0)),
                  pl.BlockSpec((d, tn), lambda j: (0, j)),
                  pl.BlockSpec((1, tn), lambda j: (0, j))],
        out_specs=pl.BlockSpec((m, tn), lambda j: (0, j)),
        out_shape=jax.ShapeDtypeStruct((m, n), f32),
        compiler_params=_params(("arbitrary",)),
        name="mod",
    )(cc, w_mod, b_mod.reshape(1, n))


def _inproj_kernel(x_ref, c_ref, mod_ref, g_ref, w_ref, mu_ref, cos_ref, sin_ref, o_ref, h_ref, z_ref):
    n = pl.program_id(1)
    T = x_ref.shape[1]
    L = T + c_ref.shape[1]
    nlat = T // ROWS
    nchunk = L // ROWS
    PAD = 8

    SR = STEP_ROWS if L % STEP_ROWS == 0 else ROWS

    @pl.when(n == 0)
    def _():
        z_ref[0:PAD, :] = jnp.zeros((PAD, TN), f32)
        z_ref[PAD + L:PAD + L + PAD, :] = jnp.zeros((PAD, TN), f32)

        def norm_chunk(src_ref, s0, r0, o):
            xb = src_ref[0, pl.ds(s0, ROWS), :]
            y = xb * lax.rsqrt(jnp.mean(xb * xb, axis=-1, keepdims=True) + NORM_EPS) * g_ref[...]
            sh = mod_ref[0, o:o + 1, :]
            sc = mod_ref[0, o + 1:o + 2, :]
            h_ref[pl.ds(r0, ROWS), :] = (y * (1.0 + sc) + sh).astype(bf16)

        def lat_body(i, carry):
            r0 = pl.multiple_of(i * ROWS, ROWS)
            norm_chunk(x_ref, r0, r0, 0)
            return carry

        lax.fori_loop(0, nlat, lat_body, 0)
        for j in range(nlat, nchunk):
            norm_chunk(c_ref, j * ROWS - T, j * ROWS, 2)

    PR = POST_ROWS

    def rope(scale):
        def post(z, r0):
            if scale != 1.0:
                z = z * scale
            cs = cos_ref[pl.ds(r0, SR), :]
            sn = sin_ref[pl.ds(r0, SR), :]
            parts = []
            for hh in range(TN // RET_HEAD_DIM):
                zh = z[:, hh * RET_HEAD_DIM:(hh + 1) * RET_HEAD_DIM]
                parts.append(zh * cs + pltpu.roll(zh, RET_HEAD_DIM // 2, 1) * sn)
            return jnp.concatenate(parts, axis=1)
        return post

    def shifted(r0):
        win = z_ref[pl.ds(r0, PR + 2 * PAD), :]
        prev = win[PAD - 1:PAD - 1 + PR]
        z = win[PAD:PAD + PR]
        nxt = win[PAD + 1:PAD + 1 + PR]
        rid = r0 + lax.broadcasted_iota(jnp.int32, (PR, 1), 0)
        prev = jnp.where(rid == T, 0.0, prev)
        nxt = jnp.where(rid == T - 1, 0.0, nxt)
        return z + mu_ref[0:1, :] * (prev - z) + mu_ref[1:2, :] * (nxt - z)

    def lora_act(zs):
        lane = lax.broadcasted_iota(jnp.int32, (1, TN), 1)
        return jnp.where(lane < DECAY_LORA, jnp.tanh(zs),
                         jnp.where(lane < DECAY_LORA + ICLR_LORA, zs, jax.nn.sigmoid(zs)))

    def product(r0):
        return _dot(h_ref[pl.ds(r0, SR), :], w_ref[...])

    def direct(post):
        def body(i, carry):
            r0 = pl.multiple_of(i * SR, SR)
            o_ref[0, pl.ds(r0, SR), :] = post(product(r0), r0).astype(bf16)
            return carry
        lax.fori_loop(0, L // SR, body, 0)

    def via_buffer(act):
        def mm_body(i, carry):
            r0 = pl.multiple_of(i * SR, SR)
            z_ref[pl.ds(PAD + r0, SR), :] = product(r0)
            return carry
        lax.fori_loop(0, L // SR, mm_body, 0)

        def post_body(i, carry):
            r0 = pl.multiple_of(i * PR, PR)
            o_ref[0, pl.ds(r0, PR), :] = act(shifted(r0)).astype(bf16)
            return carry
        lax.fori_loop(0, L // PR, post_body, 0)

    t = lambda c: c // TN
    pl.when(n < t(C_RETK))(lambda: direct(lambda z, r0: _sigmoid(z)))
    pl.when((n >= t(C_RETK)) & (n < t(C_RETV)))(lambda: direct(rope(RET_HEAD_DIM ** -0.5)))
    pl.when((n >= t(C_RETV)) & (n < t(C_RETQ)))(lambda: direct(lambda z, r0: z))
    pl.when((n >= t(C_RETQ)) & (n < t(C_RETG)))(lambda: direct(rope(1.0)))
    pl.when((n >= t(C_RETG)) & (n < t(C_RWK)))(lambda: direct(lambda z, r0: z * _sigmoid(z)))
    pl.when((n >= t(C_RWK)) & (n < t(C_LORA)))(lambda: via_buffer(lambda zs: zs))
    pl.when(n >= t(C_LORA))(lambda: via_buffer(lora_act))


def _inproj(x, ctx, modrows, g0, w_perm, mu_full, cosf, sinf):
    B, T, D = x.shape
    CT = ctx.shape[1]
    L = T + CT
    nt = IN_COLS // TN
    return pl.pallas_call(
        _inproj_kernel,
        grid=(B, nt),
        in_specs=[pl.BlockSpec((1, T, D), lambda b, n: (b, 0, 0)),
                  pl.BlockSpec((1, CT, D), lambda b, n: (b, 0, 0)),
                  pl.BlockSpec((1, 4, D), lambda b, n: (b, 0, 0)),
                  pl.BlockSpec((1, D), lambda b, n: (0, 0)),
                  pl.BlockSpec((D, TN), lambda b, n: (0, n)),
                  pl.BlockSpec((2, TN), lambda b, n: (0, n)),
                  pl.BlockSpec((L, RET_HEAD_DIM), lambda b, n: (0, 0)),
                  pl.BlockSpec((L, RET_HEAD_DIM), lambda b, n: (0, 0))],
        out_specs=pl.BlockSpec((1, L, TN), lambda b, n: (b, 0, n)),
        out_shape=jax.ShapeDtypeStruct((B, L, IN_COLS), bf16),
        scratch_shapes=[pltpu.VMEM((L, D), bf16), pltpu.VMEM((L + 16, TN), f32)],
        compiler_params=_params(("arbitrary", "arbitrary")),
        name="inproj",
    )(x, ctx, modrows, g0, w_perm, mu_full, cosf, sinf)


def _ret_kernel(lg_ref, q_ref, k_ref, v_ref, o_ref, r_ref, tab_ref, *, ctx):
    L = q_ref.shape[1]
    Cc = RET_CHUNK
    hd = RET_HEAD_DIM
    lat = L - ctx
    nc = ctx // Cc
    nl = lat // Cc
    combos = [(h, d) for h in range(RET_HEADS) for d in (0, 1)]
    INTRA, CROSS, TAIL, DECAY = 0, 1, 2, 3

    @pl.when(pl.program_id(0) == 0)
    def _():
        ii = lax.broadcasted_iota(jnp.int32, (Cc, Cc), 0).astype(f32)
        jj = lax.broadcasted_iota(jnp.int32, (Cc, Cc), 1).astype(f32)
        for idx, (h, d) in enumerate(combos):
            lg = lg_ref[d, h]
            if d == 0:
                diff = ii - jj
                cross = jnp.exp(lg * (ii + 1.0))
                tailw = jnp.exp(lg * (Cc - 1.0 - ii))
            else:
                diff = jj - ii
                cross = jnp.exp(lg * (Cc - ii))
                tailw = jnp.exp(lg * ii)
            tab_ref[idx, INTRA] = jnp.where(diff >= 0, jnp.exp(lg * jnp.maximum(diff, 0.0)), 0.0)
            tab_ref[idx, CROSS] = cross
            tab_ref[idx, TAIL] = tailw
            tab_ref[idx, DECAY] = jnp.exp(jnp.zeros((Cc, Cc), f32) + lg * Cc)

    r_ref[...] = jnp.zeros(r_ref.shape, f32)
    o_ref[...] = jnp.zeros(o_ref.shape, f32)

    def cols(h):
        return slice(h * hd, (h + 1) * hd)

    def update(idx, kc, vc):
        ks = (kc.astype(f32) * tab_ref[idx, TAIL]).astype(bf16)
        r_ref[idx] = r_ref[idx] * tab_ref[idx, DECAY] + _dot_tn(ks, vc)

    def ctx_step(s, carry):
        for idx, (h, d) in enumerate(combos):
            row0 = pl.multiple_of(lat + (s * Cc if d == 0 else (nc - 1 - s) * Cc), Cc)
            update(idx, k_ref[0, pl.ds(row0, Cc), cols(h)], v_ref[0, pl.ds(row0, Cc), cols(h)])
        return carry

    def lat_step(s, carry):
        t0s, qs, ks, vs = [], [], [], []
        for h, d in combos:
            t0 = pl.multiple_of(s * Cc if d == 0 else (nl - 1 - s) * Cc, Cc)
            row0 = t0
            t0s.append(t0)
            qs.append(q_ref[0, pl.ds(row0, Cc), cols(h)])
            ks.append(k_ref[0, pl.ds(row0, Cc), cols(h)])
            vs.append(v_ref[0, pl.ds(row0, Cc), cols(h)])
        n = len(combos)
        sc = [(_dot_nt(qs[i], ks[i]) * tab_ref[i, INTRA]).astype(bf16) for i in range(n)]
        oc = [_dot(qs[i], r_ref[i].astype(bf16)) * tab_ref[i, CROSS] for i in range(n)]
        oi = [_dot(sc[i], vs[i]) for i in range(n)]
        for i, (h, d) in enumerate(combos):
            o_ref[0, pl.ds(t0s[i], Cc), cols(h)] += oi[i] + oc[i]
        for i in range(n):
            update(i, ks[i], vs[i])
        return carry

    lax.fori_loop(0, nc, ctx_step, 0)
    lax.fori_loop(0, nl, lat_step, 0)


def _retention(lg, z, ctx):
    B, L, _ = z.shape
    T = L - ctx
    W = RET_WIDTH
    Cc = RET_CHUNK
    blk = lambda c0: pl.BlockSpec((1, L, W), lambda b: (b, 0, c0 // W))
    nchain = 2 * RET_HEADS
    return pl.pallas_call(
        functools.partial(_ret_kernel, ctx=ctx),
        grid=(B,),
        in_specs=[pl.BlockSpec(memory_space=pltpu.SMEM), blk(C_RETQ), blk(C_RETK), blk(C_RETV)],
        out_specs=pl.BlockSpec((1, T, W), lambda b: (b, 0, 0)),
        out_shape=jax.ShapeDtypeStruct((B, T, W), f32),
        scratch_shapes=[pltpu.VMEM((nchain, Cc, Cc), f32), pltpu.VMEM((nchain, 4, Cc, Cc), f32)],
        compiler_params=_params(("arbitrary",)),
        name="ret",
    )(lg, z, z, z)


def _rwkv_kernel(k_ref, v_ref, r_ref, lo_ref, w0_ref, a0_ref, w2_ref, a2_ref, kk_ref, ka_ref, rk_ref,
                 o_ref, bo_ref, s_ref, *, nc, nb):
    d = pl.program_id(1)
    s = pl.program_id(2)
    C = RWKV_CHUNK
    G = 2 * RWKV_HEAD_DIM
    rev = d == 1

    @pl.when(s == 0)
    def _():
        s_ref[...] = jnp.zeros(s_ref.shape, f32)

    ii = lax.broadcasted_iota(jnp.int32, (C, G), 0)
    lane = lax.broadcasted_iota(jnp.int32, (C, G), 1)
    jj = lane & (RWKV_HEAD_DIM - 1)
    head0 = lane < RWKV_HEAD_DIM
    dlt = jnp.where(rev, ii - jj, jj - ii)
    strict = dlt < 0
    incl = dlt <= 0
    eye = (ii == jj).astype(f32)
    gi = lax.broadcasted_iota(jnp.int32, (G, G), 0)
    gj = lax.broadcasted_iota(jnp.int32, (G, G), 1)
    blockdiag = (gi >= RWKV_HEAD_DIM) == (gj >= RWKV_HEAD_DIM)
    ones_bd = blockdiag.astype(bf16)
    tri = incl[:, 0:C].astype(bf16)
    base = (ii >> 1) == (jj >> 1)
    offs = [((ii >> (lv + 1)) == (jj >> (lv + 1))) & ((ii >> lv) != (jj >> lv)) for lv in range(1, 6)]

    def segsum(x):
        return jnp.concatenate(
            [_bdot(x[:, p * G:(p + 1) * G], ones_bd) for p in range(RWKV_PAIRS)], axis=1)

    def stack(x):
        xb = x.astype(bf16)
        zero = jnp.zeros_like(xb)
        return jnp.concatenate([jnp.where(head0, xb, zero), jnp.where(head0, zero, xb)], axis=0)

    pre = []
    for bb in range(nb):
        kx = k_ref[bb].astype(f32)
        vx = v_ref[bb].astype(f32)
        rx = r_ref[bb].astype(f32)
        lo = lo_ref[bb, :, 0:DECAY_LORA + ICLR_LORA]
        u = w0_ref[0] + _dot(lo, w2_ref[0])
        softplus = jnp.maximum(-u, 0.0) + jnp.log1p(jnp.exp(-jnp.abs(u)))
        lw = -jnp.exp(-softplus - 0.5)
        a = jax.nn.sigmoid(a0_ref[0] + _dot(lo, a2_ref[0]))
        kkr = kx * kk_ref[...]
        kk = kkr * lax.rsqrt(segsum(kkr * kkr) + 1e-12)
        kd = kx * (1.0 + (a - 1.0) * ka_ref[...])
        be = kk * a
        bonus = segsum(rx * kd * rk_ref[...]) * vx
        cum = _split_dot_left(tri, lw)
        tot = jnp.where(rev, cum[0:1, :], cum[C - 1:C, :])
        gneg = jnp.exp(-cum)
        gh = jnp.exp(tot - cum)
        pre.append(dict(alb=-kk * jnp.exp(cum - lw), rb=rx * jnp.exp(cum), beb=be * gneg, kb=kd * gneg,
                        beh=be * gh, kh=kd * gh, etot=jnp.exp(tot), v=vx, bonus=bonus))

    units = [(bb, p) for bb in range(nb) for p in range(RWKV_PAIRS)]
    sl = lambda p: slice(p * G, (p + 1) * G)
    part = lambda un, name: pre[un[0]][name][:, sl(un[1])]
    S = {un: s_ref[un[0] * RWKV_PAIRS + un[1]] for un in units}
    Sb = {un: S[un].astype(bf16) for un in units}
    X = {un: part(un, "alb").astype(bf16) for un in units}
    Rb = {un: part(un, "rb").astype(bf16) for un in units}
    Ybs = {un: stack(part(un, "beb")) for un in units}
    Yks = {un: stack(part(un, "kb")) for un in units}
    Vb = {un: part(un, "v").astype(bf16) for un in units}
    Vs = {un: stack(part(un, "v")) for un in units}

    XR = {un: jnp.concatenate([X[un], Rb[un]], axis=0) for un in units}
    APb = {un: _dot_nt(XR[un], Ybs[un]) for un in units}
    APk = {un: _dot_nt(XR[un], Yks[un]) for un in units}
    BO = {un: _dot_nt(XR[un], Sb[un]) for un in units}
    Aab = {un: jnp.where(strict, APb[un][0:C], 0.0) for un in units}
    Aak = {un: jnp.where(strict, APk[un][0:C], 0.0).astype(bf16) for un in units}
    Pab = {un: jnp.where(incl, APb[un][C:2 * C], 0.0).astype(bf16) for un in units}
    Pak = {un: jnp.where(incl, APk[un][C:2 * C], 0.0).astype(bf16) for un in units}

    Tm = {un: eye + jnp.where(base, Aab[un], 0.0) for un in units}
    for off in offs:
        Xs = {un: _dot(jnp.where(off, Aab[un], 0.0).astype(bf16), stack(Tm[un])) for un in units}
        Tm = {un: Tm[un] + _dot(Tm[un].astype(bf16), stack(Xs[un])) for un in units}

    Bm = {un: BO[un][0:C] + _dot(Aak[un], Vs[un]) for un in units}
    U = {un: _dot(Tm[un].astype(bf16), stack(Bm[un])) for un in units}
    Om = {un: BO[un][C:2 * C]
          + _dot(jnp.concatenate([Pab[un], Pak[un]], axis=1), jnp.concatenate([stack(U[un]), Vs[un]], axis=0))
          for un in units}
    for un in units:
        bb, p = un
        upd = _dot_tn(jnp.concatenate([U[un].astype(bf16), Vb[un]], axis=0),
                      jnp.concatenate([part(un, "beh").astype(bf16), part(un, "kh").astype(bf16)], axis=0))
        s_ref[bb * RWKV_PAIRS + p] = jnp.where(blockdiag, S[un] * part(un, "etot") + upd, 0.0)

    @pl.when(s >= nc)
    def _():
        for bb in range(nb):
            o_ref[0, bb] = jnp.concatenate([Om[bb, p] for p in range(RWKV_PAIRS)], axis=1)
            bo_ref[0, bb] = pre[bb]["bonus"]


def _split_dot_left(w, x):
    hi = x.astype(bf16)
    lo = (x - hi.astype(f32)).astype(bf16)
    return _dot(w, hi) + _dot(w, lo)


def _rwkv(z, w0, a0, w2p, a2p, k_k, k_a, r_k, ctx):
    B, L, _ = z.shape
    T = L - ctx
    C = RWKV_CHUNK
    nc, nl = ctx // C, T // C
    W = RWKV_WIDTH

    def chunk(d, s):
        fwd = jnp.where(s < nc, nl + s, s - nc)
        bwd = jnp.where(s < nc, nl + nc - 1 - s, nl - 1 - (s - nc))
        return jnp.where(d == 0, fwd, bwd)

    def ochunk(d, s):
        sl = jnp.maximum(s - nc, 0)
        return jnp.where(d == 0, sl, nl - 1 - sl)

    nb = RWKV_SAMPLES_PER_STEP if B % RWKV_SAMPLES_PER_STEP == 0 else 1
    zblk = lambda c0, w: pl.BlockSpec((nb, C, w), lambda b, d, s: (b, chunk(d, s), c0 // w))
    dpar = lambda r: pl.BlockSpec((1, r, W), lambda b, d, s: (d, 0, 0))
    par = pl.BlockSpec((1, W), lambda b, d, s: (0, 0))
    oblk = pl.BlockSpec((1, nb, C, W), lambda b, d, s: (d, b, ochunk(d, s), 0))
    G = 2 * RWKV_HEAD_DIM
    return pl.pallas_call(
        functools.partial(_rwkv_kernel, nc=nc, nb=nb),
        grid=(B // nb, 2, nc + nl),
        in_specs=[zblk(C_RWK, W), zblk(C_RWV, W), zblk(C_RWR, W), zblk(C_LORA, 256),
                  dpar(1), dpar(1), dpar(G), dpar(G), par, par, par],
        out_specs=[oblk, oblk],
        out_shape=[jax.ShapeDtypeStruct((2, B, T, W), f32), jax.ShapeDtypeStruct((2, B, T, W), f32)],
        scratch_shapes=[pltpu.VMEM((nb * RWKV_PAIRS, G, G), f32)],
        compiler_params=_params(("arbitrary", "arbitrary", "arbitrary")),
        name="rwkv",
    )(z, z, z, z, w0, a0, w2p, a2p, k_k, k_a, r_k)


def _merge_kernel(x_ref, ret_ref, rw0_ref, rw1_ref, b0_ref, b1_ref, gate_ref, retg_ref, gd_ref, mod_ref,
                  ng_ref, gn_ref, lng_ref, lnb_ref, g2_ref, wbr_ref, wbw_ref, wout_ref, wrh_ref, wrl_ref,
                  x1_ref, h2_ref, lt_ref):
    hd = RET_HEAD_DIM
    ret = ret_ref[0]
    parts = []
    for hh in range(RET_HEADS):
        xh = ret[:, hh * hd:(hh + 1) * hd]
        mu = jnp.mean(xh, axis=-1, keepdims=True)
        dv = xh - mu
        var = jnp.mean(dv * dv, axis=-1, keepdims=True)
        parts.append(dv * lax.rsqrt(var + RET_EPS))
    yr = retg_ref[0].astype(f32) * (jnp.concatenate(parts, axis=1) * gn_ref[...])
    y_ret = _bdot(yr, wbr_ref[...])
    W = RWKV_WIDTH
    gi = lax.broadcasted_iota(jnp.int32, (W, W), 0)
    gj = lax.broadcasted_iota(jnp.int32, (W, W), 1)
    ones_bd = ((gi >> 6) == (gj >> 6)).astype(bf16)
    o = rw0_ref[0, 0] + rw1_ref[0, 0]
    mu = _split_dot(o, ones_bd) * (1.0 / RWKV_HEAD_DIM)
    dv = o - mu
    var = _bdot(dv * dv, ones_bd) * (1.0 / RWKV_HEAD_DIM)
    yw = dv * lax.rsqrt(var + RWKV_EPS) * lng_ref[...] + lnb_ref[...]
    gate = _dot(gd_ref[0], g2_ref[...])
    yw = (yw + b0_ref[0, 0] + b1_ref[0, 0]) * gate
    y_rw = _bdot(yw, wbw_ref[...])
    D = y_ret.shape[1]
    g = gate_ref[0].astype(f32)
    m = g[:, :D] * y_ret + g[:, D:] * y_rw
    y = _bdot(m, wout_ref[...])

    def rms(v, gg):
        return v * lax.rsqrt(jnp.mean(v * v, axis=-1, keepdims=True) + NORM_EPS) * gg

    x1 = x_ref[0] + mod_ref[0, 0:1, :] * rms(y, ng_ref[0:1, :])
    x1_ref[0] = x1
    h2 = rms(x1, ng_ref[1:2, :]) * (1.0 + mod_ref[0, 2:3, :]) + mod_ref[0, 1:2, :]
    h2b = h2.astype(bf16)
    h2_ref[0] = h2b
    h2l = (h2 - h2b.astype(f32)).astype(bf16)
    lgt = _dot(h2b, wrh_ref[...]) + _dot(h2l, wrh_ref[...]) + _dot(h2b, wrl_ref[...])
    lt_ref[0] = lgt.T[0:N_EXPERTS, :]


def _merge(x, ret_o, rw_o, bonus, z, mod2, ng12, gn, lng, lnb, g2, wbr, wbw, wout, wrh, wrl, ctx):
    B, T, D = x.shape
    tm = MERGE_ROWS if T % MERGE_ROWS == 0 else ROWS
    co = 0
    W = RWKV_WIDTH
    row = lambda w: pl.BlockSpec((1, tm, w), lambda b, i: (b, i, 0))
    dblk = lambda dd: pl.BlockSpec((1, 1, tm, W), lambda b, i: (dd, b, i, 0))
    zblk = lambda c0, w: pl.BlockSpec((1, tm, w), lambda b, i: (b, co + i, c0 // w))
    full = lambda a: pl.BlockSpec(a.shape, lambda b, i: (0,) * a.ndim)
    return pl.pallas_call(
        _merge_kernel,
        grid=(B, T // tm),
        in_specs=[row(D), row(W), dblk(0), dblk(1), dblk(0), dblk(1),
                  zblk(C_MERGE, 2 * D), zblk(C_RETG, W), zblk(C_LORA + 128, 128),
                  pl.BlockSpec((1, 3, D), lambda b, i: (b, 0, 0)),
                  full(ng12), full(gn), full(lng), full(lnb), full(g2), full(wbr), full(wbw), full(wout),
                  full(wrh), full(wrl)],
        out_specs=[row(D), row(D), pl.BlockSpec((1, N_EXPERTS, tm), lambda b, i: (b, 0, i))],
        out_shape=[jax.ShapeDtypeStruct((B, T, D), f32), jax.ShapeDtypeStruct((B, T, D), bf16),
                   jax.ShapeDtypeStruct((B, N_EXPERTS, T), f32)],
        compiler_params=_params(("arbitrary", "arbitrary")),
        name="merge",
    )(x, ret_o, rw_o, rw_o, bonus, bonus, z, z, z, mod2, ng12, gn, lng, lnb, g2, wbr, wbw, wout, wrh, wrl)


def _route_kernel(lt_ref, slot_ref, rt_ref, slotf_ref, gate_ref, *, cap):
    B, NE, T = lt_ref.shape
    lg = lt_ref[...]
    mx = jnp.max(lg, axis=1, keepdims=True)
    ex = jnp.exp(lg - mx)
    aff = (ex / jnp.sum(ex, axis=1, keepdims=True)).reshape(B * NE, T)
    E = B * NE

    def count_ge(cand):
        return jnp.sum((aff >= cand).astype(f32), axis=1, keepdims=True)

    def exp_step(_, kk):
        k_lo, k_hi = kk
        km = jnp.floor((k_lo + k_hi) * 0.5)
        ok = count_ge(jnp.exp2(-km)) >= cap
        return jnp.where(ok, k_lo, km), jnp.where(ok, km, k_hi)

    k_lo, k_hi = lax.fori_loop(0, ROUTE_EXP_STEPS, exp_step,
                               (jnp.full((E, 1), -1.0, f32), jnp.full((E, 1), ROUTE_MAX_EXP, f32)))
    lo0 = jnp.where(k_hi >= ROUTE_MAX_EXP, 0.0, jnp.exp2(-k_hi))
    hi0 = jnp.exp2(-k_lo)

    def val_step(_, lh):
        lo, hi = lh
        mid = (lo + hi) * 0.5
        ok = count_ge(mid) >= cap
        return jnp.where(ok, mid, lo), jnp.where(ok, hi, mid)

    lo, hi = lax.fori_loop(0, ROUTE_VAL_STEPS, val_step, (lo0, hi0))
    gt = aff >= hi
    eq = (aff >= lo) & (aff < hi)
    need = cap - jnp.sum(gt.astype(f32), axis=1, keepdims=True)
    tri = (lax.broadcasted_iota(jnp.int32, (T, T), 0) < lax.broadcasted_iota(jnp.int32, (T, T), 1)).astype(bf16)
    eq_before = _dot(eq.astype(bf16), tri)
    sel = gt | (eq & (eq_before < need))
    slot = _dot(sel.astype(bf16), tri)
    slot_f = jnp.where(sel, slot, -1.0)
    slot_ref[...] = slot_f.astype(jnp.int32).reshape(B, NE, T)
    slotf_ref[...] = slot_f.reshape(B, NE, T)
    gate_ref[...] = jnp.where(sel, aff, 0.0).reshape(B, NE, T)

    def transpose_sample(b, carry):
        packed = jnp.concatenate([slotf_ref[b], gate_ref[b], jnp.zeros((128 - 2 * NE, T), f32)], axis=0)
        rt_ref[b] = packed.T.astype(bf16)
        return carry

    lax.fori_loop(0, B, transpose_sample, 0)


def _route(lt, cap):
    B, E, T = lt.shape
    assert cap <= 256
    return pl.pallas_call(
        functools.partial(_route_kernel, cap=cap),
        grid=(1,),
        in_specs=[pl.BlockSpec((B, E, T), lambda i: (0, 0, 0))],
        out_specs=[pl.BlockSpec((B, E, T), lambda i: (0, 0, 0)), pl.BlockSpec((B, T, 128), lambda i: (0, 0, 0))],
        out_shape=[jax.ShapeDtypeStruct((B, E, T), jnp.int32), jax.ShapeDtypeStruct((B, T, 128), bf16)],
        scratch_shapes=[pltpu.VMEM((B, E, T), f32), pltpu.VMEM((B, E, T), f32)],
        compiler_params=_params(("arbitrary",)),
        name="route",
    )(lt)


def _ffn_kernel(slot_ref, h_ref, wg_ref, wu_ref, wd_ref, o_ref, wgb_ref, wub_ref, wdb_ref, *, cap):
    T = h_ref.shape[1]

    @pl.when(pl.program_id(1) == 0)
    def _():
        wgb_ref[...] = wg_ref[0].astype(bf16)
        wub_ref[...] = wu_ref[0].astype(bf16)
        wdb_ref[...] = wd_ref[0].astype(bf16)

    ci = lax.broadcasted_iota(jnp.int32, (cap, T), 0)
    onehot = (slot_ref[0, 0] == ci).astype(bf16)
    xg = _dot(onehot, h_ref[0]).astype(bf16)
    hg = _dot(xg, wgb_ref[...])
    hu = _dot(xg, wub_ref[...])
    hid = (hg * jax.nn.sigmoid(hg) * hu).astype(bf16)
    o_ref[0, 0] = _dot(hid, wdb_ref[...]).astype(bf16)


def _ffn(slot4, h2, wg, wu, wd, cap):
    B, T, D = h2.shape
    E, _, F = wg.shape
    return pl.pallas_call(
        functools.partial(_ffn_kernel, cap=cap),
        grid=(E, B),
        in_specs=[pl.BlockSpec((1, 1, 1, T), lambda e, b: (b, e, 0, 0)),
                  pl.BlockSpec((1, T, D), lambda e, b: (b, 0, 0)),
                  pl.BlockSpec((1, D, F), lambda e, b: (e, 0, 0)),
                  pl.BlockSpec((1, D, F), lambda e, b: (e, 0, 0)),
                  pl.BlockSpec((1, F, D), lambda e, b: (e, 0, 0))],
        out_specs=pl.BlockSpec((1, 1, cap, D), lambda e, b: (b, e, 0, 0)),
        out_shape=jax.ShapeDtypeStruct((B, E, cap, D), bf16),
        scratch_shapes=[pltpu.VMEM((D, F), bf16), pltpu.VMEM((D, F), bf16), pltpu.VMEM((F, D), bf16)],
        compiler_params=_params(("arbitrary", "arbitrary")),
        name="ffn",
    )(slot4, h2, wg, wu, wd)


def _combine_kernel(rt_ref, eo_ref, x1_ref, mod_ref, ng_ref, o_ref, *, cap):
    rt = rt_ref[0].astype(f32)
    tm = rt.shape[0]
    E = eo_ref.shape[1]
    ci = lax.broadcasted_iota(jnp.int32, (tm, cap), 1).astype(f32)
    y = jnp.zeros((tm, eo_ref.shape[3]), f32)
    for e in range(E):
        p = jnp.where(rt[:, e:e + 1] == ci, rt[:, E + e:E + e + 1], 0.0).astype(bf16)
        y = y + _dot(p, eo_ref[0, e])
    yn = y * lax.rsqrt(jnp.mean(y * y, axis=-1, keepdims=True) + NORM_EPS) * ng_ref[...]
    o_ref[0] = x1_ref[0] + mod_ref[0] * yn


def _combine(rt, eo, x1, g2mod, ng3, cap):
    B, T, D = x1.shape
    E = eo.shape[1]
    tm = 512 if T % 512 == 0 else T
    return pl.pallas_call(
        functools.partial(_combine_kernel, cap=cap),
        grid=(B, T // tm),
        in_specs=[pl.BlockSpec((1, tm, 128), lambda b, i: (b, i, 0)),
                  pl.BlockSpec((1, E, cap, D), lambda b, i: (b, 0, 0, 0)),
                  pl.BlockSpec((1, tm, D), lambda b, i: (b, i, 0)),
                  pl.BlockSpec((1, 1, D), lambda b, i: (b, 0, 0)),
                  pl.BlockSpec((1, D), lambda b, i: (0, 0))],
        out_specs=pl.BlockSpec((1, tm, D), lambda b, i: (b, i, 0)),
        out_shape=jax.ShapeDtypeStruct((B, T, D), f32),
        compiler_params=_params(("arbitrary", "arbitrary")),
        name="combine",
    )(rt, eo, x1, g2mod, ng3)


def _column_perm():
    sk, sv, rk, rv, wd, ad = 0, 512, 1024, 1536, 2048, 2112
    q0 = 2176
    rq, rg, rr, gd, mg = q0, q0 + 512, q0 + 1024, q0 + 1536, q0 + 1664
    rng = lambda a, n: np.arange(a, a + n)
    return np.concatenate([rng(mg, 2048), rng(sk, 512), rng(sv, 512), rng(rq, 512), rng(rg, 512),
                           rng(rk, 512), rng(rv, 512), rng(rr, 512), rng(wd, 64), rng(ad, 64), rng(gd, 128)])


def _rope_tables(T, CT):
    t = jnp.arange(T)
    nfreq = RET_HEAD_DIM // 4
    inv = ROPE_BASE ** (-jnp.arange(nfreq, dtype=f32) / nfreq)
    ang = jnp.concatenate([(t // GRID_W).astype(f32)[:, None] * inv,
                           (t % GRID_W).astype(f32)[:, None] * inv], axis=-1)
    cos, sin = jnp.cos(ang), jnp.sin(ang)
    cosf = jnp.concatenate([cos, cos], axis=1)
    sinf = jnp.concatenate([-sin, sin], axis=1)
    return (jnp.concatenate([cosf, jnp.ones((CT, RET_HEAD_DIM), f32)], axis=0),
            jnp.concatenate([sinf, jnp.zeros((CT, RET_HEAD_DIM), f32)], axis=0))


def kernel(x, c, ctx, c_ctx, w_mod, b_mod, norm_g, w_in, ret_log_decay, ret_gn_g, rwkv_mu, rwkv_k_k, rwkv_k_a,
           rwkv_r_k, rwkv_w0, rwkv_w2, rwkv_a0, rwkv_a2, rwkv_g2, rwkv_ln_g, rwkv_ln_b, w_br_ret, w_br_rwkv,
           w_out, w_router, w_gate, w_up, w_down):
    B, T, D = x.shape
    CT = ctx.shape[1]
    assert w_mod.shape[0] == 1 and D == D_MODEL
    assert CT % ROWS == 0 and T % ROWS == 0 and T % GRID_W == 0
    cap = CAPACITY_FACTOR * T // N_EXPERTS
    assert cap % 8 == 0

    mrows = -(-(B + 1) // 8) * 8
    cc = jnp.zeros((mrows, D), f32).at[:B].set(c).at[B].set(c_ctx)
    mod = _modulation(cc, w_mod[0], b_mod[0])
    lat = mod[:B].reshape(B, N_MOD, D)
    cm = jnp.broadcast_to(mod[B].reshape(1, N_MOD, D), (B, N_MOD, D))
    modrows = jnp.concatenate([lat[:, 0:2], cm[:, 0:2]], axis=1)

    perm = _column_perm()
    w_perm = jnp.pad(w_in[0][:, perm].astype(bf16), ((0, 0), (0, IN_COLS - USED_COLS)))
    mu = rwkv_mu[0]
    ss = 2 * RWKV_WIDTH + DECAY_LORA + ICLR_LORA
    mu_full = jnp.zeros((2, IN_COLS), f32)
    mu_full = mu_full.at[:, C_RWK:C_RWK + 1024].set(mu[:, 0:1024])
    mu_full = mu_full.at[:, C_RWR:C_RWR + 512].set(mu[:, ss:ss + 512])
    mu_full = mu_full.at[:, C_LORA:C_LORA + 128].set(mu[:, 1024:ss])
    mu_full = mu_full.at[:, C_LORA + 128:USED_COLS].set(mu[:, ss + 512:])
    cosf, sinf = _rope_tables(T, CT)
    z = _inproj(x, ctx, modrows, norm_g[0, 0:1], w_perm, mu_full, cosf, sinf)

    lg = -jnp.exp(ret_log_decay[0].astype(f32))
    ret_o = _retention(lg, z, CT)

    G = 2 * RWKV_HEAD_DIM
    w2p = jnp.zeros((2, G, RWKV_WIDTH), f32).at[:, :DECAY_LORA].set(rwkv_w2[0]).astype(bf16)
    a2p = jnp.zeros((2, G, RWKV_WIDTH), f32).at[:, DECAY_LORA:].set(rwkv_a2[0]).astype(bf16)
    rw_o, bonus = _rwkv(z, rwkv_w0[0][:, None, :], rwkv_a0[0][:, None, :], w2p, a2p,
                        rwkv_k_k[0][None], rwkv_k_a[0][None], rwkv_r_k[0][None], CT)

    mod2 = jnp.stack([lat[:, 2], lat[:, 3], lat[:, 4]], axis=1)
    wr_pad = jnp.zeros((D, 128), f32).at[:, :N_EXPERTS].set(w_router[0])
    wr_hi = wr_pad.astype(bf16)
    wr_lo = (wr_pad - wr_hi.astype(f32)).astype(bf16)
    x1, h2, lt = _merge(x, ret_o, rw_o, bonus, z, mod2, norm_g[0, 1:3], ret_gn_g[0][None], rwkv_ln_g[0][None],
                        rwkv_ln_b[0][None], rwkv_g2[0].astype(bf16), w_br_ret[0].astype(bf16),
                        w_br_rwkv[0].astype(bf16), w_out[0].astype(bf16), wr_hi, wr_lo, CT)

    slot, rt = _route(lt, cap)
    eo = _ffn(slot.reshape(B, N_EXPERTS, 1, T), h2, w_gate[0], w_up[0], w_down[0], cap)
    return _combine(rt, eo, x1, lat[:, 5:6], norm_g[0, 3:4], cap)
```

```python
import functools

import numpy as np
import jax
import jax.numpy as jnp
from jax import lax
from jax.experimental import pallas as pl
from jax.experimental.pallas import tpu as pltpu

f32 = jnp.float32
bf16 = jnp.bfloat16

D_MODEL = 1024
GRID_W = 64
RET_HEAD_DIM = 128
RET_WIDTH = 512
RET_HEADS = 4
RET_CHUNK = 128
RET_EPS = 1e-5
ROPE_BASE = 10000.0
RWKV_HEAD_DIM = 64
RWKV_WIDTH = 512
RWKV_PAIRS = 4
RWKV_CHUNK = 64
RWKV_SAMPLES_PER_STEP = 4
DECAY_LORA = 64
ICLR_LORA = 64
GATE_LORA = 128
RWKV_EPS = 64e-5
N_EXPERTS = 16
EXPERT_FF = 1024
CAPACITY_FACTOR = 2
N_MOD = 6
NORM_EPS = 1e-6

C_MERGE, C_RETK, C_RETV, C_RETQ, C_RETG = 0, 2048, 2560, 3072, 3584
C_RWK, C_RWV, C_RWR, C_LORA = 4096, 4608, 5120, 5632
USED_COLS = 5888
TN = 512
IN_COLS = -(-USED_COLS // TN) * TN
ROWS = 256
STEP_ROWS = 768
POST_ROWS = 128
MERGE_ROWS = 512

ROUTE_MAX_EXP = 126.0
ROUTE_EXP_STEPS = 8
ROUTE_VAL_STEPS = 26

VMEM_LIMIT = 56 * 1024 * 1024
FFN_SAMPLES_PER_STEP = 2
FFN_VMEM_LIMIT = 62 * 1024 * 1024


def _dot(a, b):
    return jnp.dot(a, b, preferred_element_type=f32)


def _dot_nt(a, b):
    return lax.dot_general(a, b, (((1,), (1,)), ((), ())), preferred_element_type=f32)


def _dot_tn(a, b):
    return lax.dot_general(a, b, (((0,), (0,)), ((), ())), preferred_element_type=f32)


def _bdot(a, b):
    return _dot(a.astype(bf16), b.astype(bf16))


def _split_dot(x, w):
    hi = x.astype(bf16)
    lo = (x - hi.astype(f32)).astype(bf16)
    return _dot(hi, w) + _dot(lo, w)


def _sigmoid(x):
    return 0.5 * jnp.tanh(0.5 * x) + 0.5


def _params(sem, limit=VMEM_LIMIT):
    return pltpu.CompilerParams(dimension_semantics=sem, vmem_limit_bytes=limit)


def _mod_kernel(c_ref, w_ref, b_ref, o_ref):
    c = c_ref[...]
    s = c * jax.nn.sigmoid(c)
    o_ref[...] = _bdot(s, w_ref[...]) + b_ref[...]


def _modulation(cc, w_mod, b_mod):
    m, d = cc.shape
    n = w_mod.shape[1]
    tn = 512
    return pl.pallas_call(
        _mod_kernel,
        grid=(n // tn,),
        in_specs=[pl.BlockSpec((m, d), lambda j: (0, 0)),
                  pl.BlockSpec((d, tn), lambda j: (0, j)),
                  pl.BlockSpec((1, tn), lambda j: (0, j))],
        out_specs=pl.BlockSpec((m, tn), lambda j: (0, j)),
        out_shape=jax.ShapeDtypeStruct((m, n), f32),
        compiler_params=_params(("arbitrary",)),
        name="mod",
    )(cc, w_mod, b_mod.reshape(1, n))


def _inproj_kernel(x_ref, c_ref, mod_ref, g_ref, w_ref, mu_ref, cos_ref, sin_ref, o_ref, h_ref, z_ref):
    n = pl.program_id(1)
    T = x_ref.shape[1]
    L = T + c_ref.shape[1]
    nlat = T // ROWS
    nchunk = L // ROWS
    PAD = 8

    SR = STEP_ROWS if L % STEP_ROWS == 0 else ROWS

    @pl.when(n == 0)
    def _():
        z_ref[0:PAD, :] = jnp.zeros((PAD, TN), f32)
        z_ref[PAD + L:PAD + L + PAD, :] = jnp.zeros((PAD, TN), f32)

        def norm_chunk(src_ref, s0, r0, o):
            xb = src_ref[0, pl.ds(s0, ROWS), :]
            y = xb * lax.rsqrt(jnp.mean(xb * xb, axis=-1, keepdims=True) + NORM_EPS) * g_ref[...]
            sh = mod_ref[0, o:o + 1, :]
            sc = mod_ref[0, o + 1:o + 2, :]
            h_ref[pl.ds(r0, ROWS), :] = (y * (1.0 + sc) + sh).astype(bf16)

        def lat_body(i, carry):
            r0 = pl.multiple_of(i * ROWS, ROWS)
            norm_chunk(x_ref, r0, r0, 0)
            return carry

        lax.fori_loop(0, nlat, lat_body, 0)
        for j in range(nlat, nchunk):
            norm_chunk(c_ref, j * ROWS - T, j * ROWS, 2)

    PR = POST_ROWS

    def rope(scale):
        def post(z, r0):
            if scale != 1.0:
                z = z * scale
            cs = cos_ref[pl.ds(r0, SR), :]
            sn = sin_ref[pl.ds(r0, SR), :]
            parts = []
            for hh in range(TN // RET_HEAD_DIM):
                zh = z[:, hh * RET_HEAD_DIM:(hh + 1) * RET_HEAD_DIM]
                parts.append(zh * cs + pltpu.roll(zh, RET_HEAD_DIM // 2, 1) * sn)
            return jnp.concatenate(parts, axis=1)
        return post

    def shifted(r0):
        win = z_ref[pl.ds(r0, PR + 2 * PAD), :]
        prev = win[PAD - 1:PAD - 1 + PR]
        z = win[PAD:PAD + PR]
        nxt = win[PAD + 1:PAD + 1 + PR]
        rid = r0 + lax.broadcasted_iota(jnp.int32, (PR, 1), 0)
        prev = jnp.where(rid == T, 0.0, prev)
        nxt = jnp.where(rid == T - 1, 0.0, nxt)
        return z + mu_ref[0:1, :] * (prev - z) + mu_ref[1:2, :] * (nxt - z)

    def lora_act(zs):
        lane = lax.broadcasted_iota(jnp.int32, (1, TN), 1)
        return jnp.where(lane < DECAY_LORA, jnp.tanh(zs),
                         jnp.where(lane < DECAY_LORA + ICLR_LORA, zs, jax.nn.sigmoid(zs)))

    def product(r0):
        return _dot(h_ref[pl.ds(r0, SR), :], w_ref[...])

    def direct(post):
        def body(i, carry):
            r0 = pl.multiple_of(i * SR, SR)
            o_ref[0, pl.ds(r0, SR), :] = post(product(r0), r0).astype(bf16)
            return carry
        lax.fori_loop(0, L // SR, body, 0)

    def via_buffer(act):
        def mm_body(i, carry):
            r0 = pl.multiple_of(i * SR, SR)
            z_ref[pl.ds(PAD + r0, SR), :] = product(r0)
            return carry
        lax.fori_loop(0, L // SR, mm_body, 0)

        def post_body(i, carry):
            r0 = pl.multiple_of(i * PR, PR)
            o_ref[0, pl.ds(r0, PR), :] = act(shifted(r0)).astype(bf16)
            return carry
        lax.fori_loop(0, L // PR, post_body, 0)

    t = lambda c: c // TN
    pl.when(n < t(C_RETK))(lambda: direct(lambda z, r0: _sigmoid(z)))
    pl.when((n >= t(C_RETK)) & (n < t(C_RETV)))(lambda: direct(rope(RET_HEAD_DIM ** -0.5)))
    pl.when((n >= t(C_RETV)) & (n < t(C_RETQ)))(lambda: direct(lambda z, r0: z))
    pl.when((n >= t(C_RETQ)) & (n < t(C_RETG)))(lambda: direct(rope(1.0)))
    pl.when((n >= t(C_RETG)) & (n < t(C_RWK)))(lambda: direct(lambda z, r0: z * _sigmoid(z)))
    pl.when((n >= t(C_RWK)) & (n < t(C_LORA)))(lambda: via_buffer(lambda zs: zs))
    pl.when(n >= t(C_LORA))(lambda: via_buffer(lora_act))


def _inproj(x, ctx, modrows, g0, w_perm, mu_full, cosf, sinf):
    B, T, D = x.shape
    CT = ctx.shape[1]
    L = T + CT
    nt = IN_COLS // TN
    return pl.pallas_call(
        _inproj_kernel,
        grid=(B, nt),
        in_specs=[pl.BlockSpec((1, T, D), lambda b, n: (b, 0, 0)),
                  pl.BlockSpec((1, CT, D), lambda b, n: (b, 0, 0)),
                  pl.BlockSpec((1, 4, D), lambda b, n: (b, 0, 0)),
                  pl.BlockSpec((1, D), lambda b, n: (0, 0)),
                  pl.BlockSpec((D, TN), lambda b, n: (0, n)),
                  pl.BlockSpec((2, TN), lambda b, n: (0, n)),
                  pl.BlockSpec((L, RET_HEAD_DIM), lambda b, n: (0, 0)),
                  pl.BlockSpec((L, RET_HEAD_DIM), lambda b, n: (0, 0))],
        out_specs=pl.BlockSpec((1, L, TN), lambda b, n: (b, 0, n)),
        out_shape=jax.ShapeDtypeStruct((B, L, IN_COLS), bf16),
        scratch_shapes=[pltpu.VMEM((L, D), bf16), pltpu.VMEM((L + 16, TN), f32)],
        compiler_params=_params(("arbitrary", "arbitrary")),
        name="inproj",
    )(x, ctx, modrows, g0, w_perm, mu_full, cosf, sinf)


def _ret_kernel(lg_ref, q_ref, k_ref, v_ref, o_ref, r_ref, tab_ref, *, ctx):
    L = q_ref.shape[1]
    Cc = RET_CHUNK
    hd = RET_HEAD_DIM
    lat = L - ctx
    nc = ctx // Cc
    nl = lat // Cc
    combos = [(h, d) for h in range(RET_HEADS) for d in (0, 1)]
    INTRA, CROSS, TAIL, DECAY = 0, 1, 2, 3

    @pl.when(pl.program_id(0) == 0)
    def _():
        ii = lax.broadcasted_iota(jnp.int32, (Cc, Cc), 0).astype(f32)
        jj = lax.broadcasted_iota(jnp.int32, (Cc, Cc), 1).astype(f32)
        for idx, (h, d) in enumerate(combos):
            lg = lg_ref[d, h]
            if d == 0:
                diff = ii - jj
                cross = jnp.exp(lg * (ii + 1.0))
                tailw = jnp.exp(lg * (Cc - 1.0 - ii))
            else:
                diff = jj - ii
                cross = jnp.exp(lg * (Cc - ii))
                tailw = jnp.exp(lg * ii)
            tab_ref[idx, INTRA] = jnp.where(diff >= 0, jnp.exp(lg * jnp.maximum(diff, 0.0)), 0.0)
            tab_ref[idx, CROSS] = cross
            tab_ref[idx, TAIL] = tailw
            tab_ref[idx, DECAY] = jnp.exp(jnp.zeros((Cc, Cc), f32) + lg * Cc)

    r_ref[...] = jnp.zeros(r_ref.shape, f32)
    o_ref[...] = jnp.zeros(o_ref.shape, f32)

    def cols(h):
        return slice(h * hd, (h + 1) * hd)

    def update(idx, kc, vc):
        ks = (kc.astype(f32) * tab_ref[idx, TAIL]).astype(bf16)
        r_ref[idx] = r_ref[idx] * tab_ref[idx, DECAY] + _dot_tn(ks, vc)

    def ctx_step(s, carry):
        for idx, (h, d) in enumerate(combos):
            row0 = pl.multiple_of(lat + (s * Cc if d == 0 else (nc - 1 - s) * Cc), Cc)
            update(idx, k_ref[0, pl.ds(row0, Cc), cols(h)], v_ref[0, pl.ds(row0, Cc), cols(h)])
        return carry

    def lat_step(s, carry):
        t0s, qs, ks, vs = [], [], [], []
        for h, d in combos:
            t0 = pl.multiple_of(s * Cc if d == 0 else (nl - 1 - s) * Cc, Cc)
            row0 = t0
            t0s.append(t0)
            qs.append(q_ref[0, pl.ds(row0, Cc), cols(h)])
            ks.append(k_ref[0, pl.ds(row0, Cc), cols(h)])
            vs.append(v_ref[0, pl.ds(row0, Cc), cols(h)])
        n = len(combos)
        sc = [(_dot_nt(qs[i], ks[i]) * tab_ref[i, INTRA]).astype(bf16) for i in range(n)]
        oc = [_dot(qs[i], r_ref[i].astype(bf16)) * tab_ref[i, CROSS] for i in range(n)]
        oi = [_dot(sc[i], vs[i]) for i in range(n)]
        for i, (h, d) in enumerate(combos):
            o_ref[0, pl.ds(t0s[i], Cc), cols(h)] += oi[i] + oc[i]
        for i in range(n):
            update(i, ks[i], vs[i])
        return carry

    lax.fori_loop(0, nc, ctx_step, 0)
    lax.fori_loop(0, nl, lat_step, 0)


def _retention(lg, z, ctx):
    B, L, _ = z.shape
    T = L - ctx
    W = RET_WIDTH
    Cc = RET_CHUNK
    blk = lambda c0: pl.BlockSpec((1, L, W), lambda b: (b, 0, c0 // W))
    nchain = 2 * RET_HEADS
    return pl.pallas_call(
        functools.partial(_ret_kernel, ctx=ctx),
        grid=(B,),
        in_specs=[pl.BlockSpec(memory_space=pltpu.SMEM), blk(C_RETQ), blk(C_RETK), blk(C_RETV)],
        out_specs=pl.BlockSpec((1, T, W), lambda b: (b, 0, 0)),
        out_shape=jax.ShapeDtypeStruct((B, T, W), f32),
        scratch_shapes=[pltpu.VMEM((nchain, Cc, Cc), f32), pltpu.VMEM((nchain, 4, Cc, Cc), f32)],
        compiler_params=_params(("arbitrary",)),
        name="ret",
    )(lg, z, z, z)


def _rwkv_kernel(k_ref, v_ref, r_ref, lo_ref, w0_ref, a0_ref, w2_ref, a2_ref, kk_ref, ka_ref, rk_ref,
                 o_ref, bo_ref, s_ref, *, nc, nb):
    d = pl.program_id(1)
    s = pl.program_id(2)
    C = RWKV_CHUNK
    G = 2 * RWKV_HEAD_DIM
    rev = d == 1

    @pl.when(s == 0)
    def _():
        s_ref[...] = jnp.zeros(s_ref.shape, f32)

    ii = lax.broadcasted_iota(jnp.int32, (C, G), 0)
    lane = lax.broadcasted_iota(jnp.int32, (C, G), 1)
    jj = lane & (RWKV_HEAD_DIM - 1)
    head0 = lane < RWKV_HEAD_DIM
    dlt = jnp.where(rev, ii - jj, jj - ii)
    strict = dlt < 0
    incl = dlt <= 0
    eye = (ii == jj).astype(f32)
    gi = lax.broadcasted_iota(jnp.int32, (G, G), 0)
    gj = lax.broadcasted_iota(jnp.int32, (G, G), 1)
    blockdiag = (gi >= RWKV_HEAD_DIM) == (gj >= RWKV_HEAD_DIM)
    ones_bd = blockdiag.astype(bf16)
    tri = incl[:, 0:C].astype(bf16)
    base = (ii >> 1) == (jj >> 1)
    offs = [((ii >> (lv + 1)) == (jj >> (lv + 1))) & ((ii >> lv) != (jj >> lv)) for lv in range(1, 6)]

    def segsum(x):
        return jnp.concatenate(
            [_bdot(x[:, p * G:(p + 1) * G], ones_bd) for p in range(RWKV_PAIRS)], axis=1)

    def stack(x):
        xb = x.astype(bf16)
        zero = jnp.zeros_like(xb)
        return jnp.concatenate([jnp.where(head0, xb, zero), jnp.where(head0, zero, xb)], axis=0)

    pre = []
    for bb in range(nb):
        kx = k_ref[bb].astype(f32)
        vx = v_ref[bb].astype(f32)
        rx = r_ref[bb].astype(f32)
        lo = lo_ref[bb, :, 0:DECAY_LORA + ICLR_LORA]
        u = w0_ref[0] + _dot(lo, w2_ref[0])
        softplus = jnp.maximum(-u, 0.0) + jnp.log1p(jnp.exp(-jnp.abs(u)))
        lw = -jnp.exp(-softplus - 0.5)
        a = jax.nn.sigmoid(a0_ref[0] + _dot(lo, a2_ref[0]))
        kkr = kx * kk_ref[...]
        kk = kkr * lax.rsqrt(segsum(kkr * kkr) + 1e-12)
        kd = kx * (1.0 + (a - 1.0) * ka_ref[...])
        be = kk * a
        bonus = segsum(rx * kd * rk_ref[...]) * vx
        cum = _split_dot_left(tri, lw)
        tot = jnp.where(rev, cum[0:1, :], cum[C - 1:C, :])
        gneg = jnp.exp(-cum)
        gh = jnp.exp(tot - cum)
        pre.append(dict(alb=-kk * jnp.exp(cum - lw), rb=rx * jnp.exp(cum), beb=be * gneg, kb=kd * gneg,
                        beh=be * gh, kh=kd * gh, etot=jnp.exp(tot), v=vx, bonus=bonus))

    units = [(bb, p) for bb in range(nb) for p in range(RWKV_PAIRS)]
    sl = lambda p: slice(p * G, (p + 1) * G)
    part = lambda un, name: pre[un[0]][name][:, sl(un[1])]
    S = {un: s_ref[un[0] * RWKV_PAIRS + un[1]] for un in units}
    Sb = {un: S[un].astype(bf16) for un in units}
    X = {un: part(un, "alb").astype(bf16) for un in units}
    Rb = {un: part(un, "rb").astype(bf16) for un in units}
    Ybs = {un: stack(part(un, "beb")) for un in units}
    Yks = {un: stack(part(un, "kb")) for un in units}
    Vb = {un: part(un, "v").astype(bf16) for un in units}
    Vs = {un: stack(part(un, "v")) for un in units}

    XR = {un: jnp.concatenate([X[un], Rb[un]], axis=0) for un in units}
    APb = {un: _dot_nt(XR[un], Ybs[un]) for un in units}
    APk = {un: _dot_nt(XR[un], Yks[un]) for un in units}
    BO = {un: _dot_nt(XR[un], Sb[un]) for un in units}
    Aab = {un: jnp.where(strict, APb[un][0:C], 0.0) for un in units}
    Aak = {un: jnp.where(strict, APk[un][0:C], 0.0).astype(bf16) for un in units}
    Pab = {un: jnp.where(incl, APb[un][C:2 * C], 0.0).astype(bf16) for un in units}
    Pak = {un: jnp.where(incl, APk[un][C:2 * C], 0.0).astype(bf16) for un in units}

    Tm = {un: eye + jnp.where(base, Aab[un], 0.0) for un in units}
    for off in offs:
        Xs = {un: _dot(jnp.where(off, Aab[un], 0.0).astype(bf16), stack(Tm[un])) for un in units}
        Tm = {un: Tm[un] + _dot(Tm[un].astype(bf16), stack(Xs[un])) for un in units}

    AV = {un: _dot(jnp.concatenate([Aak[un], Pak[un]], axis=0), Vs[un]) for un in units}
    Bm = {un: BO[un][0:C] + AV[un][0:C] for un in units}
    U = {un: _dot(Tm[un].astype(bf16), stack(Bm[un])) for un in units}
    Om = {un: BO[un][C:2 * C] + AV[un][C:2 * C] + _dot(Pab[un], stack(U[un])) for un in units}
    for un in units:
        bb, p = un
        upd = _dot_tn(jnp.concatenate([U[un].astype(bf16), Vb[un]], axis=0),
                      jnp.concatenate([part(un, "beh").astype(bf16), part(un, "kh").astype(bf16)], axis=0))
        s_ref[bb * RWKV_PAIRS + p] = jnp.where(blockdiag, S[un] * part(un, "etot") + upd, 0.0)

    @pl.when(s >= nc)
    def _():
        for bb in range(nb):
            o_ref[0, bb] = jnp.concatenate([Om[bb, p] for p in range(RWKV_PAIRS)], axis=1)
            bo_ref[0, bb] = pre[bb]["bonus"]


def _split_dot_left(w, x):
    hi = x.astype(bf16)
    lo = (x - hi.astype(f32)).astype(bf16)
    return _dot(w, hi) + _dot(w, lo)


def _rwkv(z, w0, a0, w2p, a2p, k_k, k_a, r_k, ctx):
    B, L, _ = z.shape
    T = L - ctx
    C = RWKV_CHUNK
    nc, nl = ctx // C, T // C
    W = RWKV_WIDTH

    def chunk(d, s):
        fwd = jnp.where(s < nc, nl + s, s - nc)
        bwd = jnp.where(s < nc, nl + nc - 1 - s, nl - 1 - (s - nc))
        return jnp.where(d == 0, fwd, bwd)

    def ochunk(d, s):
        sl = jnp.maximum(s - nc, 0)
        return jnp.where(d == 0, sl, nl - 1 - sl)

    nb = RWKV_SAMPLES_PER_STEP if B % RWKV_SAMPLES_PER_STEP == 0 else 1
    zblk = lambda c0, w: pl.BlockSpec((nb, C, w), lambda b, d, s: (b, chunk(d, s), c0 // w))
    dpar = lambda r: pl.BlockSpec((1, r, W), lambda b, d, s: (d, 0, 0))
    par = pl.BlockSpec((1, W), lambda b, d, s: (0, 0))
    oblk = pl.BlockSpec((1, nb, C, W), lambda b, d, s: (d, b, ochunk(d, s), 0))
    G = 2 * RWKV_HEAD_DIM
    return pl.pallas_call(
        functools.partial(_rwkv_kernel, nc=nc, nb=nb),
        grid=(B // nb, 2, nc + nl),
        in_specs=[zblk(C_RWK, W), zblk(C_RWV, W), zblk(C_RWR, W), zblk(C_LORA, 256),
                  dpar(1), dpar(1), dpar(G), dpar(G), par, par, par],
        out_specs=[oblk, oblk],
        out_shape=[jax.ShapeDtypeStruct((2, B, T, W), f32), jax.ShapeDtypeStruct((2, B, T, W), f32)],
        scratch_shapes=[pltpu.VMEM((nb * RWKV_PAIRS, G, G), f32)],
        compiler_params=_params(("arbitrary", "arbitrary", "arbitrary")),
        name="rwkv",
    )(z, z, z, z, w0, a0, w2p, a2p, k_k, k_a, r_k)


def _merge_kernel(x_ref, ret_ref, rw0_ref, rw1_ref, b0_ref, b1_ref, gate_ref, retg_ref, gd_ref, mod_ref,
                  ng_ref, gn_ref, lng_ref, lnb_ref, g2_ref, wbr_ref, wbw_ref, wout_ref, wrh_ref, wrl_ref,
                  x1_ref, h2_ref, lt_ref):
    hd = RET_HEAD_DIM
    ret = ret_ref[0]
    parts = []
    for hh in range(RET_HEADS):
        xh = ret[:, hh * hd:(hh + 1) * hd]
        mu = jnp.mean(xh, axis=-1, keepdims=True)
        dv = xh - mu
        var = jnp.mean(dv * dv, axis=-1, keepdims=True)
        parts.append(dv * lax.rsqrt(var + RET_EPS))
    yr = retg_ref[0].astype(f32) * (jnp.concatenate(parts, axis=1) * gn_ref[...])
    y_ret = _bdot(yr, wbr_ref[...])
    W = RWKV_WIDTH
    gi = lax.broadcasted_iota(jnp.int32, (W, W), 0)
    gj = lax.broadcasted_iota(jnp.int32, (W, W), 1)
    ones_bd = ((gi >> 6) == (gj >> 6)).astype(bf16)
    o = rw0_ref[0, 0] + rw1_ref[0, 0]
    mu = _split_dot(o, ones_bd) * (1.0 / RWKV_HEAD_DIM)
    dv = o - mu
    var = _bdot(dv * dv, ones_bd) * (1.0 / RWKV_HEAD_DIM)
    yw = dv * lax.rsqrt(var + RWKV_EPS) * lng_ref[...] + lnb_ref[...]
    gate = _dot(gd_ref[0], g2_ref[...])
    yw = (yw + b0_ref[0, 0] + b1_ref[0, 0]) * gate
    y_rw = _bdot(yw, wbw_ref[...])
    D = y_ret.shape[1]
    g = gate_ref[0].astype(f32)
    m = g[:, :D] * y_ret + g[:, D:] * y_rw
    y = _bdot(m, wout_ref[...])

    def rms(v, gg):
        return v * lax.rsqrt(jnp.mean(v * v, axis=-1, keepdims=True) + NORM_EPS) * gg

    x1 = x_ref[0] + mod_ref[0, 0:1, :] * rms(y, ng_ref[0:1, :])
    x1_ref[0] = x1
    h2 = rms(x1, ng_ref[1:2, :]) * (1.0 + mod_ref[0, 2:3, :]) + mod_ref[0, 1:2, :]
    h2b = h2.astype(bf16)
    h2_ref[0] = h2b
    h2l = (h2 - h2b.astype(f32)).astype(bf16)
    lgt = _dot(h2b, wrh_ref[...]) + _dot(h2l, wrh_ref[...]) + _dot(h2b, wrl_ref[...])
    lt_ref[0] = lgt.T[0:N_EXPERTS, :]


def _merge(x, ret_o, rw_o, bonus, z, mod2, ng12, gn, lng, lnb, g2, wbr, wbw, wout, wrh, wrl, ctx):
    B, T, D = x.shape
    tm = MERGE_ROWS if T % MERGE_ROWS == 0 else ROWS
    co = 0
    W = RWKV_WIDTH
    row = lambda w: pl.BlockSpec((1, tm, w), lambda b, i: (b, i, 0))
    dblk = lambda dd: pl.BlockSpec((1, 1, tm, W), lambda b, i: (dd, b, i, 0))
    zblk = lambda c0, w: pl.BlockSpec((1, tm, w), lambda b, i: (b, co + i, c0 // w))
    full = lambda a: pl.BlockSpec(a.shape, lambda b, i: (0,) * a.ndim)
    return pl.pallas_call(
        _merge_kernel,
        grid=(B, T // tm),
        in_specs=[row(D), row(W), dblk(0), dblk(1), dblk(0), dblk(1),
                  zblk(C_MERGE, 2 * D), zblk(C_RETG, W), zblk(C_LORA + 128, 128),
                  pl.BlockSpec((1, 3, D), lambda b, i: (b, 0, 0)),
                  full(ng12), full(gn), full(lng), full(lnb), full(g2), full(wbr), full(wbw), full(wout),
                  full(wrh), full(wrl)],
        out_specs=[row(D), row(D), pl.BlockSpec((1, N_EXPERTS, tm), lambda b, i: (b, 0, i))],
        out_shape=[jax.ShapeDtypeStruct((B, T, D), f32), jax.ShapeDtypeStruct((B, T, D), bf16),
                   jax.ShapeDtypeStruct((B, N_EXPERTS, T), f32)],
        compiler_params=_params(("arbitrary", "arbitrary")),
        name="merge",
    )(x, ret_o, rw_o, rw_o, bonus, bonus, z, z, z, mod2, ng12, gn, lng, lnb, g2, wbr, wbw, wout, wrh, wrl)


def _route_kernel(lt_ref, slot_ref, rt_ref, slotf_ref, gate_ref, *, cap):
    B, NE, T = lt_ref.shape
    lg = lt_ref[...]
    mx = jnp.max(lg, axis=1, keepdims=True)
    ex = jnp.exp(lg - mx)
    aff = (ex / jnp.sum(ex, axis=1, keepdims=True)).reshape(B * NE, T)
    E = B * NE

    def count_ge(cand):
        return jnp.sum((aff >= cand).astype(f32), axis=1, keepdims=True)

    def exp_step(_, kk):
        k_lo, k_hi = kk
        km = jnp.floor((k_lo + k_hi) * 0.5)
        ok = count_ge(jnp.exp2(-km)) >= cap
        return jnp.where(ok, k_lo, km), jnp.where(ok, km, k_hi)

    k_lo, k_hi = lax.fori_loop(0, ROUTE_EXP_STEPS, exp_step,
                               (jnp.full((E, 1), -1.0, f32), jnp.full((E, 1), ROUTE_MAX_EXP, f32)))
    lo0 = jnp.where(k_hi >= ROUTE_MAX_EXP, 0.0, jnp.exp2(-k_hi))
    hi0 = jnp.exp2(-k_lo)

    def val_step(_, lh):
        lo, hi = lh
        mid = (lo + hi) * 0.5
        ok = count_ge(mid) >= cap
        return jnp.where(ok, mid, lo), jnp.where(ok, hi, mid)

    lo, hi = lax.fori_loop(0, ROUTE_VAL_STEPS, val_step, (lo0, hi0))
    gt = aff >= hi
    eq = (aff >= lo) & (aff < hi)
    need = cap - jnp.sum(gt.astype(f32), axis=1, keepdims=True)
    tri = (lax.broadcasted_iota(jnp.int32, (T, T), 0) < lax.broadcasted_iota(jnp.int32, (T, T), 1)).astype(bf16)
    eq_before = _dot(eq.astype(bf16), tri)
    sel = gt | (eq & (eq_before < need))
    slot = _dot(sel.astype(bf16), tri)
    slot_f = jnp.where(sel, slot, -1.0)
    slot_ref[...] = slot_f.astype(jnp.int32).reshape(B, NE, T)
    slotf_ref[...] = slot_f.reshape(B, NE, T)
    gate_ref[...] = jnp.where(sel, aff, 0.0).reshape(B, NE, T)

    def transpose_sample(b, carry):
        packed = jnp.concatenate([slotf_ref[b], gate_ref[b], jnp.zeros((128 - 2 * NE, T), f32)], axis=0)
        rt_ref[b] = packed.T.astype(bf16)
        return carry

    lax.fori_loop(0, B, transpose_sample, 0)


def _route(lt, cap):
    B, E, T = lt.shape
    assert cap <= 256
    return pl.pallas_call(
        functools.partial(_route_kernel, cap=cap),
        grid=(1,),
        in_specs=[pl.BlockSpec((B, E, T), lambda i: (0, 0, 0))],
        out_specs=[pl.BlockSpec((B, E, T), lambda i: (0, 0, 0)), pl.BlockSpec((B, T, 128), lambda i: (0, 0, 0))],
        out_shape=[jax.ShapeDtypeStruct((B, E, T), jnp.int32), jax.ShapeDtypeStruct((B, T, 128), bf16)],
        scratch_shapes=[pltpu.VMEM((B, E, T), f32), pltpu.VMEM((B, E, T), f32)],
        compiler_params=_params(("arbitrary",)),
        name="route",
    )(lt)


def _ffn_kernel(slot_ref, h_ref, wg_ref, wu_ref, wd_ref, o_ref, wgb_ref, wub_ref, wdb_ref, *, cap):
    T = h_ref.shape[1]

    @pl.when(pl.program_id(1) == 0)
    def _():
        wgb_ref[...] = wg_ref[0].astype(bf16)
        wub_ref[...] = wu_ref[0].astype(bf16)
        wdb_ref[...] = wd_ref[0].astype(bf16)

    nb = h_ref.shape[0]
    ci = lax.broadcasted_iota(jnp.int32, (cap, T), 0)
    xg = jnp.concatenate(
        [_dot((slot_ref[bb, 0] == ci).astype(bf16), h_ref[bb]).astype(bf16) for bb in range(nb)], axis=0)
    hg = _dot(xg, wgb_ref[...])
    hu = _dot(xg, wub_ref[...])
    hid = (hg * _sigmoid(hg) * hu).astype(bf16)
    out = _dot(hid, wdb_ref[...]).astype(bf16)
    for bb in range(nb):
        o_ref[bb, 0] = out[bb * cap:(bb + 1) * cap]


def _ffn(slot4, h2, wg, wu, wd, cap):
    B, T, D = h2.shape
    E, _, F = wg.shape
    nb = FFN_SAMPLES_PER_STEP if B % FFN_SAMPLES_PER_STEP == 0 else 1
    return pl.pallas_call(
        functools.partial(_ffn_kernel, cap=cap),
        grid=(E, B // nb),
        in_specs=[pl.BlockSpec((nb, 1, 1, T), lambda e, b: (b, e, 0, 0)),
                  pl.BlockSpec((nb, T, D), lambda e, b: (b, 0, 0)),
                  pl.BlockSpec((1, D, F), lambda e, b: (e, 0, 0)),
                  pl.BlockSpec((1, D, F), lambda e, b: (e, 0, 0)),
                  pl.BlockSpec((1, F, D), lambda e, b: (e, 0, 0))],
        out_specs=pl.BlockSpec((nb, 1, cap, D), lambda e, b: (b, e, 0, 0)),
        out_shape=jax.ShapeDtypeStruct((B, E, cap, D), bf16),
        scratch_shapes=[pltpu.VMEM((D, F), bf16), pltpu.VMEM((D, F), bf16), pltpu.VMEM((F, D), bf16)],
        compiler_params=_params(("arbitrary", "arbitrary"), FFN_VMEM_LIMIT),
        name="ffn",
    )(slot4, h2, wg, wu, wd)


def _combine_kernel(rt_ref, eo_ref, x1_ref, mod_ref, ng_ref, o_ref, *, cap):
    rt = rt_ref[0].astype(f32)
    tm = rt.shape[0]
    E = eo_ref.shape[1]
    ci = lax.broadcasted_iota(jnp.int32, (tm, cap), 1).astype(f32)
    y = jnp.zeros((tm, eo_ref.shape[3]), f32)
    for e in range(E):
        p = jnp.where(rt[:, e:e + 1] == ci, rt[:, E + e:E + e + 1], 0.0).astype(bf16)
        y = y + _dot(p, eo_ref[0, e])
    yn = y * lax.rsqrt(jnp.mean(y * y, axis=-1, keepdims=True) + NORM_EPS) * ng_ref[...]
    o_ref[0] = x1_ref[0] + mod_ref[0] * yn


def _combine(rt, eo, x1, g2mod, ng3, cap):
    B, T, D = x1.shape
    E = eo.shape[1]
    tm = 512 if T % 512 == 0 else T
    return pl.pallas_call(
        functools.partial(_combine_kernel, cap=cap),
        grid=(B, T // tm),
        in_specs=[pl.BlockSpec((1, tm, 128), lambda b, i: (b, i, 0)),
                  pl.BlockSpec((1, E, cap, D), lambda b, i: (b, 0, 0, 0)),
                  pl.BlockSpec((1, tm, D), lambda b, i: (b, i, 0)),
                  pl.BlockSpec((1, 1, D), lambda b, i: (b, 0, 0)),
                  pl.BlockSpec((1, D), lambda b, i: (0, 0))],
        out_specs=pl.BlockSpec((1, tm, D), lambda b, i: (b, i, 0)),
        out_shape=jax.ShapeDtypeStruct((B, T, D), f32),
        compiler_params=_params(("arbitrary", "arbitrary")),
        name="combine",
    )(rt, eo, x1, g2mod, ng3)


def _column_perm():
    sk, sv, rk, rv, wd, ad = 0, 512, 1024, 1536, 2048, 2112
    q0 = 2176
    rq, rg, rr, gd, mg = q0, q0 + 512, q0 + 1024, q0 + 1536, q0 + 1664
    rng = lambda a, n: np.arange(a, a + n)
    return np.concatenate([rng(mg, 2048), rng(sk, 512), rng(sv, 512), rng(rq, 512), rng(rg, 512),
                           rng(rk, 512), rng(rv, 512), rng(rr, 512), rng(wd, 64), rng(ad, 64), rng(gd, 128)])


def _rope_tables(T, CT):
    t = jnp.arange(T)
    nfreq = RET_HEAD_DIM // 4
    inv = ROPE_BASE ** (-jnp.arange(nfreq, dtype=f32) / nfreq)
    ang = jnp.concatenate([(t // GRID_W).astype(f32)[:, None] * inv,
                           (t % GRID_W).astype(f32)[:, None] * inv], axis=-1)
    cos, sin = jnp.cos(ang), jnp.sin(ang)
    cosf = jnp.concatenate([cos, cos], axis=1)
    sinf = jnp.concatenate([-sin, sin], axis=1)
    return (jnp.concatenate([cosf, jnp.ones((CT, RET_HEAD_DIM), f32)], axis=0),
            jnp.concatenate([sinf, jnp.zeros((CT, RET_HEAD_DIM), f32)], axis=0))


def kernel(x, c, ctx, c_ctx, w_mod, b_mod, norm_g, w_in, ret_log_decay, ret_gn_g, rwkv_mu, rwkv_k_k, rwkv_k_a,
           rwkv_r_k, rwkv_w0, rwkv_w2, rwkv_a0, rwkv_a2, rwkv_g2, rwkv_ln_g, rwkv_ln_b, w_br_ret, w_br_rwkv,
           w_out, w_router, w_gate, w_up, w_down):
    B, T, D = x.shape
    CT = ctx.shape[1]
    assert w_mod.shape[0] == 1 and D == D_MODEL
    assert CT % ROWS == 0 and T % ROWS == 0 and T % GRID_W == 0
    cap = CAPACITY_FACTOR * T // N_EXPERTS
    assert cap % 8 == 0

    mrows = -(-(B + 1) // 8) * 8
    cc = jnp.zeros((mrows, D), f32).at[:B].set(c).at[B].set(c_ctx)
    mod = _modulation(cc, w_mod[0], b_mod[0])
    lat = mod[:B].reshape(B, N_MOD, D)
    cm = jnp.broadcast_to(mod[B].reshape(1, N_MOD, D), (B, N_MOD, D))
    modrows = jnp.concatenate([lat[:, 0:2], cm[:, 0:2]], axis=1)

    perm = _column_perm()
    w_perm = jnp.pad(w_in[0][:, perm].astype(bf16), ((0, 0), (0, IN_COLS - USED_COLS)))
    mu = rwkv_mu[0]
    ss = 2 * RWKV_WIDTH + DECAY_LORA + ICLR_LORA
    mu_full = jnp.zeros((2, IN_COLS), f32)
    mu_full = mu_full.at[:, C_RWK:C_RWK + 1024].set(mu[:, 0:1024])
    mu_full = mu_full.at[:, C_RWR:C_RWR + 512].set(mu[:, ss:ss + 512])
    mu_full = mu_full.at[:, C_LORA:C_LORA + 128].set(mu[:, 1024:ss])
    mu_full = mu_full.at[:, C_LORA + 128:USED_COLS].set(mu[:, ss + 512:])
    cosf, sinf = _rope_tables(T, CT)
    z = _inproj(x, ctx, modrows, norm_g[0, 0:1], w_perm, mu_full, cosf, sinf)

    lg = -jnp.exp(ret_log_decay[0].astype(f32))
    ret_o = _retention(lg, z, CT)

    G = 2 * RWKV_HEAD_DIM
    w2p = jnp.zeros((2, G, RWKV_WIDTH), f32).at[:, :DECAY_LORA].set(rwkv_w2[0]).astype(bf16)
    a2p = jnp.zeros((2, G, RWKV_WIDTH), f32).at[:, DECAY_LORA:].set(rwkv_a2[0]).astype(bf16)
    rw_o, bonus = _rwkv(z, rwkv_w0[0][:, None, :], rwkv_a0[0][:, None, :], w2p, a2p,
                        rwkv_k_k[0][None], rwkv_k_a[0][None], rwkv_r_k[0][None], CT)

    mod2 = jnp.stack([lat[:, 2], lat[:, 3], lat[:, 4]], axis=1)
    wr_pad = jnp.zeros((D, 128), f32).at[:, :N_EXPERTS].set(w_router[0])
    wr_hi = wr_pad.astype(bf16)
    wr_lo = (wr_pad - wr_hi.astype(f32)).astype(bf16)
    x1, h2, lt = _merge(x, ret_o, rw_o, bonus, z, mod2, norm_g[0, 1:3], ret_gn_g[0][None], rwkv_ln_g[0][None],
                        rwkv_ln_b[0][None], rwkv_g2[0].astype(bf16), w_br_ret[0].astype(bf16),
                        w_br_rwkv[0].astype(bf16), w_out[0].astype(bf16), wr_hi, wr_lo, CT)

    slot, rt = _route(lt, cap)
    eo = _ffn(slot.reshape(B, N_EXPERTS, 1, T), h2, w_gate[0], w_up[0], w_down[0], cap)
    return _combine(rt, eo, x1, lat[:, 5:6], norm_g[0, 3:4], cap)
```

```python
import functools

import numpy as np
import jax
import jax.numpy as jnp
from jax import lax
from jax.experimental import pallas as pl
from jax.experimental.pallas import tpu as pltpu

f32 = jnp.float32
bf16 = jnp.bfloat16

D_MODEL = 1024
GRID_W = 64
RET_HEAD_DIM = 128
RET_WIDTH = 512
RET_HEADS = 4
RET_CHUNK = 128
RET_EPS = 1e-5
ROPE_BASE = 10000.0
RWKV_HEAD_DIM = 64
RWKV_WIDTH = 512
RWKV_PAIRS = 4
RWKV_CHUNK = 64
RWKV_CHUNKS_PER_STEP = 4
RWKV_SAMPLES_PER_STEP = 4
DECAY_LORA = 64
ICLR_LORA = 64
GATE_LORA = 128
RWKV_EPS = 64e-5
N_EXPERTS = 16
EXPERT_FF = 1024
CAPACITY_FACTOR = 2
N_MOD = 6
NORM_EPS = 1e-6

C_MERGE, C_RETK, C_RETV, C_RETQ, C_RETG = 0, 2048, 2560, 3072, 3584
C_RWK, C_RWV, C_RWR, C_LORA = 4096, 4608, 5120, 5632
USED_COLS = 5888
TN = 512
IN_COLS = -(-USED_COLS // TN) * TN
ROWS = 256
STEP_ROWS = 768
POST_ROWS = 128
MERGE_ROWS = 512

ROUTE_MAX_EXP = 126.0
ROUTE_EXP_STEPS = 8
ROUTE_VAL_STEPS = 26

VMEM_LIMIT = 56 * 1024 * 1024
FFN_SAMPLES_PER_STEP = 2
FFN_VMEM_LIMIT = 62 * 1024 * 1024


def _dot(a, b):
    return jnp.dot(a, b, preferred_element_type=f32)


def _dot_nt(a, b):
    return lax.dot_general(a, b, (((1,), (1,)), ((), ())), preferred_element_type=f32)


def _dot_tn(a, b):
    return lax.dot_general(a, b, (((0,), (0,)), ((), ())), preferred_element_type=f32)


def _bdot(a, b):
    return _dot(a.astype(bf16), b.astype(bf16))


def _split_dot(x, w):
    hi = x.astype(bf16)
    lo = (x - hi.astype(f32)).astype(bf16)
    return _dot(hi, w) + _dot(lo, w)


def _sigmoid(x):
    return 0.5 * jnp.tanh(0.5 * x) + 0.5


def _params(sem, limit=VMEM_LIMIT):
    return pltpu.CompilerParams(dimension_semantics=sem, vmem_limit_bytes=limit)


def _mod_kernel(c_ref, w_ref, b_ref, o_ref):
    c = c_ref[...]
    s = c * jax.nn.sigmoid(c)
    o_ref[...] = _bdot(s, w_ref[...]) + b_ref[...]


def _modulation(cc, w_mod, b_mod):
    m, d = cc.shape
    n = w_mod.shape[1]
    tn = 512
    return pl.pallas_call(
        _mod_kernel,
        grid=(n // tn,),
        in_specs=[pl.BlockSpec((m, d), lambda j: (0, 0)),
                  pl.BlockSpec((d, tn), lambda j: (0, j)),
                  pl.BlockSpec((1, tn), lambda j: (0, j))],
        out_specs=pl.BlockSpec((m, tn), lambda j: (0, j)),
        out_shape=jax.ShapeDtypeStruct((m, n), f32),
        compiler_params=_params(("arbitrary",)),
        name="mod",
    )(cc, w_mod, b_mod.reshape(1, n))


def _inproj_kernel(x_ref, c_ref, mod_ref, g_ref, w_ref, mu_ref, cos_ref, sin_ref, o_ref, h_ref, z_ref):
    n = pl.program_id(1)
    T = x_ref.shape[1]
    L = T + c_ref.shape[1]
    nlat = T // ROWS
    nchunk = L // ROWS
    PAD = 8

    SR = STEP_ROWS if L % STEP_ROWS == 0 else ROWS

    @pl.when(n == 0)
    def _():
        z_ref[0:PAD, :] = jnp.zeros((PAD, TN), f32)
        z_ref[PAD + L:PAD + L + PAD, :] = jnp.zeros((PAD, TN), f32)

        def norm_chunk(src_ref, s0, r0, o):
            xb = src_ref[0, pl.ds(s0, ROWS), :]
            y = xb * lax.rsqrt(jnp.mean(xb * xb, axis=-1, keepdims=True) + NORM_EPS) * g_ref[...]
            sh = mod_ref[0, o:o + 1, :]
            sc = mod_ref[0, o + 1:o + 2, :]
            h_ref[pl.ds(r0, ROWS), :] = (y * (1.0 + sc) + sh).astype(bf16)

        def lat_body(i, carry):
            r0 = pl.multiple_of(i * ROWS, ROWS)
            norm_chunk(x_ref, r0, r0, 0)
            return carry

        lax.fori_loop(0, nlat, lat_body, 0)
        for j in range(nlat, nchunk):
            norm_chunk(c_ref, j * ROWS - T, j * ROWS, 2)

    PR = POST_ROWS

    def rope(scale):
        def post(z, r0):
            if scale != 1.0:
                z = z * scale
            cs = cos_ref[pl.ds(r0, SR), :]
            sn = sin_ref[pl.ds(r0, SR), :]
            parts = []
            for hh in range(TN // RET_HEAD_DIM):
                zh = z[:, hh * RET_HEAD_DIM:(hh + 1) * RET_HEAD_DIM]
                parts.append(zh * cs + pltpu.roll(zh, RET_HEAD_DIM // 2, 1) * sn)
            return jnp.concatenate(parts, axis=1)
        return post

    def shifted(r0):
        win = z_ref[pl.ds(r0, PR + 2 * PAD), :]
        prev = win[PAD - 1:PAD - 1 + PR]
        z = win[PAD:PAD + PR]
        nxt = win[PAD + 1:PAD + 1 + PR]
        rid = r0 + lax.broadcasted_iota(jnp.int32, (PR, 1), 0)
        prev = jnp.where(rid == T, 0.0, prev)
        nxt = jnp.where(rid == T - 1, 0.0, nxt)
        return z + mu_ref[0:1, :] * (prev - z) + mu_ref[1:2, :] * (nxt - z)

    def lora_act(zs):
        lane = lax.broadcasted_iota(jnp.int32, (1, TN), 1)
        return jnp.where(lane < DECAY_LORA, jnp.tanh(zs),
                         jnp.where(lane < DECAY_LORA + ICLR_LORA, zs, jax.nn.sigmoid(zs)))

    def product(r0):
        return _dot(h_ref[pl.ds(r0, SR), :], w_ref[...])

    def direct(post):
        def body(i, carry):
            r0 = pl.multiple_of(i * SR, SR)
            o_ref[0, pl.ds(r0, SR), :] = post(product(r0), r0).astype(bf16)
            return carry
        lax.fori_loop(0, L // SR, body, 0)

    def via_buffer(act):
        def mm_body(i, carry):
            r0 = pl.multiple_of(i * SR, SR)
            z_ref[pl.ds(PAD + r0, SR), :] = product(r0)
            return carry
        lax.fori_loop(0, L // SR, mm_body, 0)

        def post_body(i, carry):
            r0 = pl.multiple_of(i * PR, PR)
            o_ref[0, pl.ds(r0, PR), :] = act(shifted(r0)).astype(bf16)
            return carry
        lax.fori_loop(0, L // PR, post_body, 0)

    t = lambda c: c // TN
    pl.when(n < t(C_RETK))(lambda: direct(lambda z, r0: _sigmoid(z)))
    pl.when((n >= t(C_RETK)) & (n < t(C_RETV)))(lambda: direct(rope(RET_HEAD_DIM ** -0.5)))
    pl.when((n >= t(C_RETV)) & (n < t(C_RETQ)))(lambda: direct(lambda z, r0: z))
    pl.when((n >= t(C_RETQ)) & (n < t(C_RETG)))(lambda: direct(rope(1.0)))
    pl.when((n >= t(C_RETG)) & (n < t(C_RWK)))(lambda: direct(lambda z, r0: z * _sigmoid(z)))
    pl.when((n >= t(C_RWK)) & (n < t(C_LORA)))(lambda: via_buffer(lambda zs: zs))
    pl.when(n >= t(C_LORA))(lambda: via_buffer(lora_act))


def _inproj(x, ctx, modrows, g0, w_perm, mu_full, cosf, sinf):
    B, T, D = x.shape
    CT = ctx.shape[1]
    L = T + CT
    nt = IN_COLS // TN
    return pl.pallas_call(
        _inproj_kernel,
        grid=(B, nt),
        in_specs=[pl.BlockSpec((1, T, D), lambda b, n: (b, 0, 0)),
                  pl.BlockSpec((1, CT, D), lambda b, n: (b, 0, 0)),
                  pl.BlockSpec((1, 4, D), lambda b, n: (b, 0, 0)),
                  pl.BlockSpec((1, D), lambda b, n: (0, 0)),
                  pl.BlockSpec((D, TN), lambda b, n: (0, n)),
                  pl.BlockSpec((2, TN), lambda b, n: (0, n)),
                  pl.BlockSpec((L, RET_HEAD_DIM), lambda b, n: (0, 0)),
                  pl.BlockSpec((L, RET_HEAD_DIM), lambda b, n: (0, 0))],
        out_specs=pl.BlockSpec((1, L, TN), lambda b, n: (b, 0, n)),
        out_shape=jax.ShapeDtypeStruct((B, L, IN_COLS), bf16),
        scratch_shapes=[pltpu.VMEM((L, D), bf16), pltpu.VMEM((L + 16, TN), f32)],
        compiler_params=_params(("arbitrary", "arbitrary")),
        name="inproj",
    )(x, ctx, modrows, g0, w_perm, mu_full, cosf, sinf)


def _ret_kernel(lg_ref, q_ref, k_ref, v_ref, o_ref, r_ref, tab_ref, *, ctx):
    L = q_ref.shape[1]
    Cc = RET_CHUNK
    hd = RET_HEAD_DIM
    lat = L - ctx
    nc = ctx // Cc
    nl = lat // Cc
    combos = [(h, d) for h in range(RET_HEADS) for d in (0, 1)]
    INTRA, CROSS, TAIL, DECAY = 0, 1, 2, 3

    @pl.when(pl.program_id(0) == 0)
    def _():
        ii = lax.broadcasted_iota(jnp.int32, (Cc, Cc), 0).astype(f32)
        jj = lax.broadcasted_iota(jnp.int32, (Cc, Cc), 1).astype(f32)
        for idx, (h, d) in enumerate(combos):
            lg = lg_ref[d, h]
            if d == 0:
                diff = ii - jj
                cross = jnp.exp(lg * (ii + 1.0))
                tailw = jnp.exp(lg * (Cc - 1.0 - ii))
            else:
                diff = jj - ii
                cross = jnp.exp(lg * (Cc - ii))
                tailw = jnp.exp(lg * ii)
            tab_ref[idx, INTRA] = jnp.where(diff >= 0, jnp.exp(lg * jnp.maximum(diff, 0.0)), 0.0)
            tab_ref[idx, CROSS] = cross
            tab_ref[idx, TAIL] = tailw
            tab_ref[idx, DECAY] = jnp.exp(jnp.zeros((Cc, Cc), f32) + lg * Cc)

    r_ref[...] = jnp.zeros(r_ref.shape, f32)
    o_ref[...] = jnp.zeros(o_ref.shape, f32)

    def cols(h):
        return slice(h * hd, (h + 1) * hd)

    def update(idx, kc, vc):
        ks = (kc.astype(f32) * tab_ref[idx, TAIL]).astype(bf16)
        r_ref[idx] = r_ref[idx] * tab_ref[idx, DECAY] + _dot_tn(ks, vc)

    def ctx_step(s, carry):
        for idx, (h, d) in enumerate(combos):
            row0 = pl.multiple_of(lat + (s * Cc if d == 0 else (nc - 1 - s) * Cc), Cc)
            update(idx, k_ref[0, pl.ds(row0, Cc), cols(h)], v_ref[0, pl.ds(row0, Cc), cols(h)])
        return carry

    def lat_step(s, carry):
        t0s, qs, ks, vs = [], [], [], []
        for h, d in combos:
            t0 = pl.multiple_of(s * Cc if d == 0 else (nl - 1 - s) * Cc, Cc)
            row0 = t0
            t0s.append(t0)
            qs.append(q_ref[0, pl.ds(row0, Cc), cols(h)])
            ks.append(k_ref[0, pl.ds(row0, Cc), cols(h)])
            vs.append(v_ref[0, pl.ds(row0, Cc), cols(h)])
        n = len(combos)
        sc = [(_dot_nt(qs[i], ks[i]) * tab_ref[i, INTRA]).astype(bf16) for i in range(n)]
        oc = [_dot(qs[i], r_ref[i].astype(bf16)) * tab_ref[i, CROSS] for i in range(n)]
        oi = [_dot(sc[i], vs[i]) for i in range(n)]
        for i, (h, d) in enumerate(combos):
            o_ref[0, pl.ds(t0s[i], Cc), cols(h)] += oi[i] + oc[i]
        for i in range(n):
            update(i, ks[i], vs[i])
        return carry

    lax.fori_loop(0, nc, ctx_step, 0)
    lax.fori_loop(0, nl, lat_step, 0)


def _retention(lg, z, ctx):
    B, L, _ = z.shape
    T = L - ctx
    W = RET_WIDTH
    Cc = RET_CHUNK
    blk = lambda c0: pl.BlockSpec((1, L, W), lambda b: (b, 0, c0 // W))
    nchain = 2 * RET_HEADS
    return pl.pallas_call(
        functools.partial(_ret_kernel, ctx=ctx),
        grid=(B,),
        in_specs=[pl.BlockSpec(memory_space=pltpu.SMEM), blk(C_RETQ), blk(C_RETK), blk(C_RETV)],
        out_specs=pl.BlockSpec((1, T, W), lambda b: (b, 0, 0)),
        out_shape=jax.ShapeDtypeStruct((B, T, W), f32),
        scratch_shapes=[pltpu.VMEM((nchain, Cc, Cc), f32), pltpu.VMEM((nchain, 4, Cc, Cc), f32)],
        compiler_params=_params(("arbitrary",)),
        name="ret",
    )(lg, z, z, z)


def _rwkv_kernel(k_ref, v_ref, r_ref, lo_ref, w0_ref, a0_ref, w2_ref, a2_ref, kk_ref, ka_ref, rk_ref,
                 o_ref, bo_ref, s_ref, *, nc, nb, nsub):
    d = pl.program_id(1)
    s = pl.program_id(2)
    C = RWKV_CHUNK

    @pl.when(s == 0)
    def _():
        s_ref[...] = jnp.zeros(s_ref.shape, f32)

    def sub_chunk(j, carry):
        r0 = pl.multiple_of(jnp.where(d == 1, nsub - 1 - j, j) * C, C)
        rows = lambda ref: ref.at[:, pl.ds(r0, C), :]
        _rwkv_chunk(rows(k_ref), rows(v_ref), rows(r_ref), rows(lo_ref), w0_ref, a0_ref, w2_ref, a2_ref,
                    kk_ref, ka_ref, rk_ref, o_ref.at[:, :, pl.ds(r0, C), :], bo_ref.at[:, :, pl.ds(r0, C), :],
                    s_ref, rev=d == 1, emit=s >= nc, nb=nb)
        return carry

    lax.fori_loop(0, nsub, sub_chunk, 0)


def _rwkv_chunk(k_ref, v_ref, r_ref, lo_ref, w0_ref, a0_ref, w2_ref, a2_ref, kk_ref, ka_ref, rk_ref,
                o_ref, bo_ref, s_ref, *, rev, emit, nb):
    C = RWKV_CHUNK
    G = 2 * RWKV_HEAD_DIM

    ii = lax.broadcasted_iota(jnp.int32, (C, G), 0)
    lane = lax.broadcasted_iota(jnp.int32, (C, G), 1)
    jj = lane & (RWKV_HEAD_DIM - 1)
    head0 = lane < RWKV_HEAD_DIM
    dlt = jnp.where(rev, ii - jj, jj - ii)
    strict = dlt < 0
    incl = dlt <= 0
    eye = (ii == jj).astype(f32)
    gi = lax.broadcasted_iota(jnp.int32, (G, G), 0)
    gj = lax.broadcasted_iota(jnp.int32, (G, G), 1)
    blockdiag = (gi >= RWKV_HEAD_DIM) == (gj >= RWKV_HEAD_DIM)
    ones_bd = blockdiag.astype(bf16)
    tri = incl[:, 0:C].astype(bf16)
    base = (ii >> 1) == (jj >> 1)
    offs = [((ii >> (lv + 1)) == (jj >> (lv + 1))) & ((ii >> lv) != (jj >> lv)) for lv in range(1, 6)]

    def segsum(x):
        return jnp.concatenate(
            [_bdot(x[:, p * G:(p + 1) * G], ones_bd) for p in range(RWKV_PAIRS)], axis=1)

    def stack(x):
        xb = x.astype(bf16)
        zero = jnp.zeros_like(xb)
        return jnp.concatenate([jnp.where(head0, xb, zero), jnp.where(head0, zero, xb)], axis=0)

    pre = []
    for bb in range(nb):
        kx = k_ref[bb].astype(f32)
        vx = v_ref[bb].astype(f32)
        rx = r_ref[bb].astype(f32)
        lo = lo_ref[bb, :, 0:DECAY_LORA + ICLR_LORA]
        u = w0_ref[0] + _dot(lo, w2_ref[0])
        softplus = jnp.maximum(-u, 0.0) + jnp.log1p(jnp.exp(-jnp.abs(u)))
        lw = -jnp.exp(-softplus - 0.5)
        a = jax.nn.sigmoid(a0_ref[0] + _dot(lo, a2_ref[0]))
        kkr = kx * kk_ref[...]
        kk = kkr * lax.rsqrt(segsum(kkr * kkr) + 1e-12)
        kd = kx * (1.0 + (a - 1.0) * ka_ref[...])
        be = kk * a
        bonus = segsum(rx * kd * rk_ref[...]) * vx
        cum = _split_dot_left(tri, lw)
        tot = jnp.where(rev, cum[0:1, :], cum[C - 1:C, :])
        gneg = jnp.exp(-cum)
        gh = jnp.exp(tot - cum)
        pre.append(dict(alb=-kk * jnp.exp(cum - lw), rb=rx * jnp.exp(cum), beb=be * gneg, kb=kd * gneg,
                        beh=be * gh, kh=kd * gh, etot=jnp.exp(tot), v=vx, bonus=bonus))

    units = [(bb, p) for bb in range(nb) for p in range(RWKV_PAIRS)]
    sl = lambda p: slice(p * G, (p + 1) * G)
    part = lambda un, name: pre[un[0]][name][:, sl(un[1])]
    S = {un: s_ref[un[0] * RWKV_PAIRS + un[1]] for un in units}
    Sb = {un: S[un].astype(bf16) for un in units}
    X = {un: part(un, "alb").astype(bf16) for un in units}
    Rb = {un: part(un, "rb").astype(bf16) for un in units}
    Ybs = {un: stack(part(un, "beb")) for un in units}
    Yks = {un: stack(part(un, "kb")) for un in units}
    Vb = {un: part(un, "v").astype(bf16) for un in units}
    Vs = {un: stack(part(un, "v")) for un in units}

    XR = {un: jnp.concatenate([X[un], Rb[un]], axis=0) for un in units}
    APb = {un: _dot_nt(XR[un], Ybs[un]) for un in units}
    APk = {un: _dot_nt(XR[un], Yks[un]) for un in units}
    BO = {un: _dot_nt(XR[un], Sb[un]) for un in units}
    Aab = {un: jnp.where(strict, APb[un][0:C], 0.0) for un in units}
    Aak = {un: jnp.where(strict, APk[un][0:C], 0.0).astype(bf16) for un in units}
    Pab = {un: jnp.where(incl, APb[un][C:2 * C], 0.0).astype(bf16) for un in units}
    Pak = {un: jnp.where(incl, APk[un][C:2 * C], 0.0).astype(bf16) for un in units}

    Tm = {un: eye + jnp.where(base, Aab[un], 0.0) for un in units}
    for off in offs:
        Xs = {un: _dot(jnp.where(off, Aab[un], 0.0).astype(bf16), stack(Tm[un])) for un in units}
        Tm = {un: Tm[un] + _dot(Tm[un].astype(bf16), stack(Xs[un])) for un in units}

    AV = {un: _dot(jnp.concatenate([Aak[un], Pak[un]], axis=0), Vs[un]) for un in units}
    Bm = {un: BO[un][0:C] + AV[un][0:C] for un in units}
    U = {un: _dot(Tm[un].astype(bf16), stack(Bm[un])) for un in units}
    Om = {un: BO[un][C:2 * C] + AV[un][C:2 * C] + _dot(Pab[un], stack(U[un])) for un in units}
    for un in units:
        bb, p = un
        upd = _dot_tn(jnp.concatenate([U[un].astype(bf16), Vb[un]], axis=0),
                      jnp.concatenate([part(un, "beh").astype(bf16), part(un, "kh").astype(bf16)], axis=0))
        s_ref[bb * RWKV_PAIRS + p] = jnp.where(blockdiag, S[un] * part(un, "etot") + upd, 0.0)

    @pl.when(emit)
    def _():
        for bb in range(nb):
            o_ref[0, bb] = jnp.concatenate([Om[bb, p] for p in range(RWKV_PAIRS)], axis=1)
            bo_ref[0, bb] = pre[bb]["bonus"]


def _split_dot_left(w, x):
    hi = x.astype(bf16)
    lo = (x - hi.astype(f32)).astype(bf16)
    return _dot(w, hi) + _dot(w, lo)


def _rwkv(z, w0, a0, w2p, a2p, k_k, k_a, r_k, ctx):
    B, L, _ = z.shape
    T = L - ctx
    nsub = RWKV_CHUNKS_PER_STEP
    while ctx % (nsub * RWKV_CHUNK) or T % (nsub * RWKV_CHUNK):
        nsub //= 2
    C = nsub * RWKV_CHUNK
    nc, nl = ctx // C, T // C
    W = RWKV_WIDTH

    def chunk(d, s):
        fwd = jnp.where(s < nc, nl + s, s - nc)
        bwd = jnp.where(s < nc, nl + nc - 1 - s, nl - 1 - (s - nc))
        return jnp.where(d == 0, fwd, bwd)

    def ochunk(d, s):
        sl = jnp.maximum(s - nc, 0)
        return jnp.where(d == 0, sl, nl - 1 - sl)

    nb = RWKV_SAMPLES_PER_STEP if B % RWKV_SAMPLES_PER_STEP == 0 else 1
    zblk = lambda c0, w: pl.BlockSpec((nb, C, w), lambda b, d, s: (b, chunk(d, s), c0 // w))
    dpar = lambda r: pl.BlockSpec((1, r, W), lambda b, d, s: (d, 0, 0))
    par = pl.BlockSpec((1, W), lambda b, d, s: (0, 0))
    oblk = pl.BlockSpec((1, nb, C, W), lambda b, d, s: (d, b, ochunk(d, s), 0))
    G = 2 * RWKV_HEAD_DIM
    return pl.pallas_call(
        functools.partial(_rwkv_kernel, nc=nc, nb=nb, nsub=nsub),
        grid=(B // nb, 2, nc + nl),
        in_specs=[zblk(C_RWK, W), zblk(C_RWV, W), zblk(C_RWR, W), zblk(C_LORA, 256),
                  dpar(1), dpar(1), dpar(G), dpar(G), par, par, par],
        out_specs=[oblk, oblk],
        out_shape=[jax.ShapeDtypeStruct((2, B, T, W), f32), jax.ShapeDtypeStruct((2, B, T, W), f32)],
        scratch_shapes=[pltpu.VMEM((nb * RWKV_PAIRS, G, G), f32)],
        compiler_params=_params(("arbitrary", "arbitrary", "arbitrary")),
        name="rwkv",
    )(z, z, z, z, w0, a0, w2p, a2p, k_k, k_a, r_k)


def _merge_kernel(x_ref, ret_ref, rw0_ref, rw1_ref, b0_ref, b1_ref, gate_ref, retg_ref, gd_ref, mod_ref,
                  ng_ref, gn_ref, lng_ref, lnb_ref, g2_ref, wbr_ref, wbw_ref, wout_ref, wrh_ref, wrl_ref,
                  x1_ref, h2_ref, lt_ref):
    hd = RET_HEAD_DIM
    ret = ret_ref[0]
    parts = []
    for hh in range(RET_HEADS):
        xh = ret[:, hh * hd:(hh + 1) * hd]
        mu = jnp.mean(xh, axis=-1, keepdims=True)
        dv = xh - mu
        var = jnp.mean(dv * dv, axis=-1, keepdims=True)
        parts.append(dv * lax.rsqrt(var + RET_EPS))
    yr = retg_ref[0].astype(f32) * (jnp.concatenate(parts, axis=1) * gn_ref[...])
    y_ret = _bdot(yr, wbr_ref[...])
    W = RWKV_WIDTH
    gi = lax.broadcasted_iota(jnp.int32, (W, W), 0)
    gj = lax.broadcasted_iota(jnp.int32, (W, W), 1)
    ones_bd = ((gi >> 6) == (gj >> 6)).astype(bf16)
    o = rw0_ref[0, 0] + rw1_ref[0, 0]
    mu = _split_dot(o, ones_bd) * (1.0 / RWKV_HEAD_DIM)
    dv = o - mu
    var = _bdot(dv * dv, ones_bd) * (1.0 / RWKV_HEAD_DIM)
    yw = dv * lax.rsqrt(var + RWKV_EPS) * lng_ref[...] + lnb_ref[...]
    gate = _dot(gd_ref[0], g2_ref[...])
    yw = (yw + b0_ref[0, 0] + b1_ref[0, 0]) * gate
    y_rw = _bdot(yw, wbw_ref[...])
    D = y_ret.shape[1]
    g = gate_ref[0].astype(f32)
    m = g[:, :D] * y_ret + g[:, D:] * y_rw
    y = _bdot(m, wout_ref[...])

    def rms(v, gg):
        return v * lax.rsqrt(jnp.mean(v * v, axis=-1, keepdims=True) + NORM_EPS) * gg

    x1 = x_ref[0] + mod_ref[0, 0:1, :] * rms(y, ng_ref[0:1, :])
    x1_ref[0] = x1
    h2 = rms(x1, ng_ref[1:2, :]) * (1.0 + mod_ref[0, 2:3, :]) + mod_ref[0, 1:2, :]
    h2b = h2.astype(bf16)
    h2_ref[0] = h2b
    h2l = (h2 - h2b.astype(f32)).astype(bf16)
    lgt = _dot(h2b, wrh_ref[...]) + _dot(h2l, wrh_ref[...]) + _dot(h2b, wrl_ref[...])
    lt_ref[0] = lgt.T[0:N_EXPERTS, :]


def _merge(x, ret_o, rw_o, bonus, z, mod2, ng12, gn, lng, lnb, g2, wbr, wbw, wout, wrh, wrl, ctx):
    B, T, D = x.shape
    tm = MERGE_ROWS if T % MERGE_ROWS == 0 else ROWS
    co = 0
    W = RWKV_WIDTH
    row = lambda w: pl.BlockSpec((1, tm, w), lambda b, i: (b, i, 0))
    dblk = lambda dd: pl.BlockSpec((1, 1, tm, W), lambda b, i: (dd, b, i, 0))
    zblk = lambda c0, w: pl.BlockSpec((1, tm, w), lambda b, i: (b, co + i, c0 // w))
    full = lambda a: pl.BlockSpec(a.shape, lambda b, i: (0,) * a.ndim)
    return pl.pallas_call(
        _merge_kernel,
        grid=(B, T // tm),
        in_specs=[row(D), row(W), dblk(0), dblk(1), dblk(0), dblk(1),
                  zblk(C_MERGE, 2 * D), zblk(C_RETG, W), zblk(C_LORA + 128, 128),
                  pl.BlockSpec((1, 3, D), lambda b, i: (b, 0, 0)),
                  full(ng12), full(gn), full(lng), full(lnb), full(g2), full(wbr), full(wbw), full(wout),
                  full(wrh), full(wrl)],
        out_specs=[row(D), row(D), pl.BlockSpec((1, N_EXPERTS, tm), lambda b, i: (b, 0, i))],
        out_shape=[jax.ShapeDtypeStruct((B, T, D), f32), jax.ShapeDtypeStruct((B, T, D), bf16),
                   jax.ShapeDtypeStruct((B, N_EXPERTS, T), f32)],
        compiler_params=_params(("arbitrary", "arbitrary")),
        name="merge",
    )(x, ret_o, rw_o, rw_o, bonus, bonus, z, z, z, mod2, ng12, gn, lng, lnb, g2, wbr, wbw, wout, wrh, wrl)


def _route_kernel(lt_ref, slot_ref, rt_ref, slotf_ref, gate_ref, *, cap):
    B, NE, T = lt_ref.shape
    lg = lt_ref[...]
    mx = jnp.max(lg, axis=1, keepdims=True)
    ex = jnp.exp(lg - mx)
    aff = (ex / jnp.sum(ex, axis=1, keepdims=True)).reshape(B * NE, T)
    E = B * NE

    def count_ge(cand):
        return jnp.sum((aff >= cand).astype(f32), axis=1, keepdims=True)

    def exp_step(_, kk):
        k_lo, k_hi = kk
        km = jnp.floor((k_lo + k_hi) * 0.5)
        ok = count_ge(jnp.exp2(-km)) >= cap
        return jnp.where(ok, k_lo, km), jnp.where(ok, km, k_hi)

    k_lo, k_hi = lax.fori_loop(0, ROUTE_EXP_STEPS, exp_step,
                               (jnp.full((E, 1), -1.0, f32), jnp.full((E, 1), ROUTE_MAX_EXP, f32)))
    lo0 = jnp.where(k_hi >= ROUTE_MAX_EXP, 0.0, jnp.exp2(-k_hi))
    hi0 = jnp.exp2(-k_lo)

    def val_step(_, lh):
        lo, hi = lh
        mid = (lo + hi) * 0.5
        ok = count_ge(mid) >= cap
        return jnp.where(ok, mid, lo), jnp.where(ok, hi, mid)

    lo, hi = lax.fori_loop(0, ROUTE_VAL_STEPS, val_step, (lo0, hi0))
    gt = aff >= hi
    eq = (aff >= lo) & (aff < hi)
    need = cap - jnp.sum(gt.astype(f32), axis=1, keepdims=True)
    tri = (lax.broadcasted_iota(jnp.int32, (T, T), 0) < lax.broadcasted_iota(jnp.int32, (T, T), 1)).astype(bf16)
    eq_before = _dot(eq.astype(bf16), tri)
    sel = gt | (eq & (eq_before < need))
    slot = _dot(sel.astype(bf16), tri)
    slot_f = jnp.where(sel, slot, -1.0)
    slot_ref[...] = slot_f.astype(jnp.int32).reshape(B, NE, T)
    slotf_ref[...] = slot_f.reshape(B, NE, T)
    gate_ref[...] = jnp.where(sel, aff, 0.0).reshape(B, NE, T)

    def transpose_sample(b, carry):
        packed = jnp.concatenate([slotf_ref[b], gate_ref[b], jnp.zeros((128 - 2 * NE, T), f32)], axis=0)
        rt_ref[b] = packed.T.astype(bf16)
        return carry

    lax.fori_loop(0, B, transpose_sample, 0)


def _route(lt, cap):
    B, E, T = lt.shape
    assert cap <= 256
    return pl.pallas_call(
        functools.partial(_route_kernel, cap=cap),
        grid=(1,),
        in_specs=[pl.BlockSpec((B, E, T), lambda i: (0, 0, 0))],
        out_specs=[pl.BlockSpec((B, E, T), lambda i: (0, 0, 0)), pl.BlockSpec((B, T, 128), lambda i: (0, 0, 0))],
        out_shape=[jax.ShapeDtypeStruct((B, E, T), jnp.int32), jax.ShapeDtypeStruct((B, T, 128), bf16)],
        scratch_shapes=[pltpu.VMEM((B, E, T), f32), pltpu.VMEM((B, E, T), f32)],
        compiler_params=_params(("arbitrary",)),
        name="route",
    )(lt)


def _ffn_kernel(slot_ref, h_ref, wg_ref, wu_ref, wd_ref, o_ref, wgb_ref, wub_ref, wdb_ref, *, cap):
    T = h_ref.shape[1]

    @pl.when(pl.program_id(1) == 0)
    def _():
        wgb_ref[...] = wg_ref[0].astype(bf16)
        wub_ref[...] = wu_ref[0].astype(bf16)
        wdb_ref[...] = wd_ref[0].astype(bf16)

    nb = h_ref.shape[0]
    ci = lax.broadcasted_iota(jnp.int32, (cap, T), 0)
    xg = jnp.concatenate(
        [_dot((slot_ref[bb, 0] == ci).astype(bf16), h_ref[bb]).astype(bf16) for bb in range(nb)], axis=0)
    hg = _dot(xg, wgb_ref[...])
    hu = _dot(xg, wub_ref[...])
    hid = (hg * _sigmoid(hg) * hu).astype(bf16)
    out = _dot(hid, wdb_ref[...]).astype(bf16)
    for bb in range(nb):
        o_ref[bb, 0] = out[bb * cap:(bb + 1) * cap]


def _ffn(slot4, h2, wg, wu, wd, cap):
    B, T, D = h2.shape
    E, _, F = wg.shape
    nb = FFN_SAMPLES_PER_STEP if B % FFN_SAMPLES_PER_STEP == 0 else 1
    return pl.pallas_call(
        functools.partial(_ffn_kernel, cap=cap),
        grid=(E, B // nb),
        in_specs=[pl.BlockSpec((nb, 1, 1, T), lambda e, b: (b, e, 0, 0)),
                  pl.BlockSpec((nb, T, D), lambda e, b: (b, 0, 0)),
                  pl.BlockSpec((1, D, F), lambda e, b: (e, 0, 0)),
                  pl.BlockSpec((1, D, F), lambda e, b: (e, 0, 0)),
                  pl.BlockSpec((1, F, D), lambda e, b: (e, 0, 0))],
        out_specs=pl.BlockSpec((nb, 1, cap, D), lambda e, b: (b, e, 0, 0)),
        out_shape=jax.ShapeDtypeStruct((B, E, cap, D), bf16),
        scratch_shapes=[pltpu.VMEM((D, F), bf16), pltpu.VMEM((D, F), bf16), pltpu.VMEM((F, D), bf16)],
        compiler_params=_params(("arbitrary", "arbitrary"), FFN_VMEM_LIMIT),
        name="ffn",
    )(slot4, h2, wg, wu, wd)


def _combine_kernel(rt_ref, eo_ref, x1_ref, mod_ref, ng_ref, o_ref, *, cap):
    rt = rt_ref[0].astype(f32)
    tm = rt.shape[0]
    E = eo_ref.shape[1]
    ci = lax.broadcasted_iota(jnp.int32, (tm, cap), 1).astype(f32)
    y = jnp.zeros((tm, eo_ref.shape[3]), f32)
    for e in range(E):
        p = jnp.where(rt[:, e:e + 1] == ci, rt[:, E + e:E + e + 1], 0.0).astype(bf16)
        y = y + _dot(p, eo_ref[0, e])
    yn = y * lax.rsqrt(jnp.mean(y * y, axis=-1, keepdims=True) + NORM_EPS) * ng_ref[...]
    o_ref[0] = x1_ref[0] + mod_ref[0] * yn


def _combine(rt, eo, x1, g2mod, ng3, cap):
    B, T, D = x1.shape
    E = eo.shape[1]
    tm = 512 if T % 512 == 0 else T
    return pl.pallas_call(
        functools.partial(_combine_kernel, cap=cap),
        grid=(B, T // tm),
        in_specs=[pl.BlockSpec((1, tm, 128), lambda b, i: (b, i, 0)),
                  pl.BlockSpec((1, E, cap, D), lambda b, i: (b, 0, 0, 0)),
                  pl.BlockSpec((1, tm, D), lambda b, i: (b, i, 0)),
                  pl.BlockSpec((1, 1, D), lambda b, i: (b, 0, 0)),
                  pl.BlockSpec((1, D), lambda b, i: (0, 0))],
        out_specs=pl.BlockSpec((1, tm, D), lambda b, i: (b, i, 0)),
        out_shape=jax.ShapeDtypeStruct((B, T, D), f32),
        compiler_params=_params(("arbitrary", "arbitrary")),
        name="combine",
    )(rt, eo, x1, g2mod, ng3)


def _column_perm():
    sk, sv, rk, rv, wd, ad = 0, 512, 1024, 1536, 2048, 2112
    q0 = 2176
    rq, rg, rr, gd, mg = q0, q0 + 512, q0 + 1024, q0 + 1536, q0 + 1664
    rng = lambda a, n: np.arange(a, a + n)
    return np.concatenate([rng(mg, 2048), rng(sk, 512), rng(sv, 512), rng(rq, 512), rng(rg, 512),
                           rng(rk, 512), rng(rv, 512), rng(rr, 512), rng(wd, 64), rng(ad, 64), rng(gd, 128)])


def _rope_tables(T, CT):
    t = jnp.arange(T)
    nfreq = RET_HEAD_DIM // 4
    inv = ROPE_BASE ** (-jnp.arange(nfreq, dtype=f32) / nfreq)
    ang = jnp.concatenate([(t // GRID_W).astype(f32)[:, None] * inv,
                           (t % GRID_W).astype(f32)[:, None] * inv], axis=-1)
    cos, sin = jnp.cos(ang), jnp.sin(ang)
    cosf = jnp.concatenate([cos, cos], axis=1)
    sinf = jnp.concatenate([-sin, sin], axis=1)
    return (jnp.concatenate([cosf, jnp.ones((CT, RET_HEAD_DIM), f32)], axis=0),
            jnp.concatenate([sinf, jnp.zeros((CT, RET_HEAD_DIM), f32)], axis=0))


def kernel(x, c, ctx, c_ctx, w_mod, b_mod, norm_g, w_in, ret_log_decay, ret_gn_g, rwkv_mu, rwkv_k_k, rwkv_k_a,
           rwkv_r_k, rwkv_w0, rwkv_w2, rwkv_a0, rwkv_a2, rwkv_g2, rwkv_ln_g, rwkv_ln_b, w_br_ret, w_br_rwkv,
           w_out, w_router, w_gate, w_up, w_down):
    B, T, D = x.shape
    CT = ctx.shape[1]
    assert w_mod.shape[0] == 1 and D == D_MODEL
    assert CT % ROWS == 0 and T % ROWS == 0 and T % GRID_W == 0
    cap = CAPACITY_FACTOR * T // N_EXPERTS
    assert cap % 8 == 0

    mrows = -(-(B + 1) // 8) * 8
    cc = jnp.zeros((mrows, D), f32).at[:B].set(c).at[B].set(c_ctx)
    mod = _modulation(cc, w_mod[0], b_mod[0])
    lat = mod[:B].reshape(B, N_MOD, D)
    cm = jnp.broadcast_to(mod[B].reshape(1, N_MOD, D), (B, N_MOD, D))
    modrows = jnp.concatenate([lat[:, 0:2], cm[:, 0:2]], axis=1)

    perm = _column_perm()
    w_perm = jnp.pad(w_in[0][:, perm].astype(bf16), ((0, 0), (0, IN_COLS - USED_COLS)))
    mu = rwkv_mu[0]
    ss = 2 * RWKV_WIDTH + DECAY_LORA + ICLR_LORA
    mu_full = jnp.zeros((2, IN_COLS), f32)
    mu_full = mu_full.at[:, C_RWK:C_RWK + 1024].set(mu[:, 0:1024])
    mu_full = mu_full.at[:, C_RWR:C_RWR + 512].set(mu[:, ss:ss + 512])
    mu_full = mu_full.at[:, C_LORA:C_LORA + 128].set(mu[:, 1024:ss])
    mu_full = mu_full.at[:, C_LORA + 128:USED_COLS].set(mu[:, ss + 512:])
    cosf, sinf = _rope_tables(T, CT)
    z = _inproj(x, ctx, modrows, norm_g[0, 0:1], w_perm, mu_full, cosf, sinf)

    lg = -jnp.exp(ret_log_decay[0].astype(f32))
    ret_o = _retention(lg, z, CT)

    G = 2 * RWKV_HEAD_DIM
    w2p = jnp.zeros((2, G, RWKV_WIDTH), f32).at[:, :DECAY_LORA].set(rwkv_w2[0]).astype(bf16)
    a2p = jnp.zeros((2, G, RWKV_WIDTH), f32).at[:, DECAY_LORA:].set(rwkv_a2[0]).astype(bf16)
    rw_o, bonus = _rwkv(z, rwkv_w0[0][:, None, :], rwkv_a0[0][:, None, :], w2p, a2p,
                        rwkv_k_k[0][None], rwkv_k_a[0][None], rwkv_r_k[0][None], CT)

    mod2 = jnp.stack([lat[:, 2], lat[:, 3], lat[:, 4]], axis=1)
    wr_pad = jnp.zeros((D, 128), f32).at[:, :N_EXPERTS].set(w_router[0])
    wr_hi = wr_pad.astype(bf16)
    wr_lo = (wr_pad - wr_hi.astype(f32)).astype(bf16)
    x1, h2, lt = _merge(x, ret_o, rw_o, bonus, z, mod2, norm_g[0, 1:3], ret_gn_g[0][None], rwkv_ln_g[0][None],
                        rwkv_ln_b[0][None], rwkv_g2[0].astype(bf16), w_br_ret[0].astype(bf16),
                        w_br_rwkv[0].astype(bf16), w_out[0].astype(bf16), wr_hi, wr_lo, CT)

    slot, rt = _route(lt, cap)
    eo = _ffn(slot.reshape(B, N_EXPERTS, 1, T), h2, w_gate[0], w_up[0], w_down[0], cap)
    return _combine(rt, eo, x1, lat[:, 5:6], norm_g[0, 3:4], cap)
```

```python
import functools

import numpy as np
import jax
import jax.numpy as jnp
from jax import lax
from jax.experimental import pallas as pl
from jax.experimental.pallas import tpu as pltpu

f32 = jnp.float32
bf16 = jnp.bfloat16

D_MODEL = 1024
GRID_W = 64
RET_HEAD_DIM = 128
RET_WIDTH = 512
RET_HEADS = 4
RET_CHUNK = 128
RET_EPS = 1e-5
ROPE_BASE = 10000.0
RWKV_HEAD_DIM = 64
RWKV_WIDTH = 512
RWKV_PAIRS = 4
RWKV_CHUNK = 64
RWKV_CHUNKS_PER_STEP = 4
RWKV_SAMPLES_PER_STEP = 4
DECAY_LORA = 64
ICLR_LORA = 64
GATE_LORA = 128
RWKV_EPS = 64e-5
N_EXPERTS = 16
EXPERT_FF = 1024
CAPACITY_FACTOR = 2
N_MOD = 6
NORM_EPS = 1e-6

C_MERGE, C_RETK, C_RETV, C_RETQ, C_RETG = 0, 2048, 2560, 3072, 3584
C_RWK, C_RWV, C_RWR, C_LORA = 4096, 4608, 5120, 5632
USED_COLS = 5888
TN = 512
IN_COLS = -(-USED_COLS // TN) * TN
ROWS = 256
STEP_ROWS = 768
POST_ROWS = 128
MERGE_ROWS = 512

ROUTE_MAX_EXP = 126.0
ROUTE_EXP_STEPS = 8
ROUTE_VAL_STEPS = 26

VMEM_LIMIT = 56 * 1024 * 1024
FFN_SAMPLES_PER_STEP = 2
GATHER_TILE = 256
GATHER_ALIGN_LOG2 = 4
GATHER_ALIGN = 1 << GATHER_ALIGN_LOG2
GATHER_WINDOW = 80
FFN_VMEM_LIMIT = 62 * 1024 * 1024


def _dot(a, b):
    return jnp.dot(a, b, preferred_element_type=f32)


def _dot_nt(a, b):
    return lax.dot_general(a, b, (((1,), (1,)), ((), ())), preferred_element_type=f32)


def _dot_tn(a, b):
    return lax.dot_general(a, b, (((0,), (0,)), ((), ())), preferred_element_type=f32)


def _bdot(a, b):
    return _dot(a.astype(bf16), b.astype(bf16))


def _split_dot(x, w):
    hi = x.astype(bf16)
    lo = (x - hi.astype(f32)).astype(bf16)
    return _dot(hi, w) + _dot(lo, w)


def _sigmoid(x):
    return 0.5 * jnp.tanh(0.5 * x) + 0.5


def _params(sem, limit=VMEM_LIMIT):
    return pltpu.CompilerParams(dimension_semantics=sem, vmem_limit_bytes=limit)


def _mod_kernel(c_ref, w_ref, b_ref, o_ref):
    c = c_ref[...]
    s = c * jax.nn.sigmoid(c)
    o_ref[...] = _bdot(s, w_ref[...]) + b_ref[...]


def _modulation(cc, w_mod, b_mod):
    m, d = cc.shape
    n = w_mod.shape[1]
    tn = 512
    return pl.pallas_call(
        _mod_kernel,
        grid=(n // tn,),
        in_specs=[pl.BlockSpec((m, d), lambda j: (0, 0)),
                  pl.BlockSpec((d, tn), lambda j: (0, j)),
                  pl.BlockSpec((1, tn), lambda j: (0, j))],
        out_specs=pl.BlockSpec((m, tn), lambda j: (0, j)),
        out_shape=jax.ShapeDtypeStruct((m, n), f32),
        compiler_params=_params(("arbitrary",)),
        name="mod",
    )(cc, w_mod, b_mod.reshape(1, n))


def _inproj_kernel(x_ref, c_ref, mod_ref, g_ref, w_ref, mu_ref, cos_ref, sin_ref, o_ref, h_ref, z_ref):
    n = pl.program_id(1)
    T = x_ref.shape[1]
    L = T + c_ref.shape[1]
    nlat = T // ROWS
    nchunk = L // ROWS
    PAD = 8

    SR = STEP_ROWS if L % STEP_ROWS == 0 else ROWS

    @pl.when(n == 0)
    def _():
        z_ref[0:PAD, :] = jnp.zeros((PAD, TN), f32)
        z_ref[PAD + L:PAD + L + PAD, :] = jnp.zeros((PAD, TN), f32)

        def norm_chunk(src_ref, s0, r0, o):
            xb = src_ref[0, pl.ds(s0, ROWS), :]
            y = xb * lax.rsqrt(jnp.mean(xb * xb, axis=-1, keepdims=True) + NORM_EPS) * g_ref[...]
            sh = mod_ref[0, o:o + 1, :]
            sc = mod_ref[0, o + 1:o + 2, :]
            h_ref[pl.ds(r0, ROWS), :] = (y * (1.0 + sc) + sh).astype(bf16)

        def lat_body(i, carry):
            r0 = pl.multiple_of(i * ROWS, ROWS)
            norm_chunk(x_ref, r0, r0, 0)
            return carry

        lax.fori_loop(0, nlat, lat_body, 0)
        for j in range(nlat, nchunk):
            norm_chunk(c_ref, j * ROWS - T, j * ROWS, 2)

    PR = POST_ROWS

    def rope(scale):
        def post(z, r0):
            if scale != 1.0:
                z = z * scale
            cs = cos_ref[pl.ds(r0, SR), :]
            sn = sin_ref[pl.ds(r0, SR), :]
            parts = []
            for hh in range(TN // RET_HEAD_DIM):
                zh = z[:, hh * RET_HEAD_DIM:(hh + 1) * RET_HEAD_DIM]
                parts.append(zh * cs + pltpu.roll(zh, RET_HEAD_DIM // 2, 1) * sn)
            return jnp.concatenate(parts, axis=1)
        return post

    def shifted(r0):
        win = z_ref[pl.ds(r0, PR + 2 * PAD), :]
        prev = win[PAD - 1:PAD - 1 + PR]
        z = win[PAD:PAD + PR]
        nxt = win[PAD + 1:PAD + 1 + PR]
        rid = r0 + lax.broadcasted_iota(jnp.int32, (PR, 1), 0)
        prev = jnp.where(rid == T, 0.0, prev)
        nxt = jnp.where(rid == T - 1, 0.0, nxt)
        return z + mu_ref[0:1, :] * (prev - z) + mu_ref[1:2, :] * (nxt - z)

    def lora_act(zs):
        lane = lax.broadcasted_iota(jnp.int32, (1, TN), 1)
        return jnp.where(lane < DECAY_LORA, jnp.tanh(zs),
                         jnp.where(lane < DECAY_LORA + ICLR_LORA, zs, jax.nn.sigmoid(zs)))

    def product(r0):
        return _dot(h_ref[pl.ds(r0, SR), :], w_ref[...])

    def direct(post):
        def body(i, carry):
            r0 = pl.multiple_of(i * SR, SR)
            o_ref[0, pl.ds(r0, SR), :] = post(product(r0), r0).astype(bf16)
            return carry
        lax.fori_loop(0, L // SR, body, 0)

    def via_buffer(act):
        def mm_body(i, carry):
            r0 = pl.multiple_of(i * SR, SR)
            z_ref[pl.ds(PAD + r0, SR), :] = product(r0)
            return carry
        lax.fori_loop(0, L // SR, mm_body, 0)

        def post_body(i, carry):
            r0 = pl.multiple_of(i * PR, PR)
            o_ref[0, pl.ds(r0, PR), :] = act(shifted(r0)).astype(bf16)
            return carry
        lax.fori_loop(0, L // PR, post_body, 0)

    t = lambda c: c // TN
    pl.when(n < t(C_RETK))(lambda: direct(lambda z, r0: _sigmoid(z)))
    pl.when((n >= t(C_RETK)) & (n < t(C_RETV)))(lambda: direct(rope(RET_HEAD_DIM ** -0.5)))
    pl.when((n >= t(C_RETV)) & (n < t(C_RETQ)))(lambda: direct(lambda z, r0: z))
    pl.when((n >= t(C_RETQ)) & (n < t(C_RETG)))(lambda: direct(rope(1.0)))
    pl.when((n >= t(C_RETG)) & (n < t(C_RWK)))(lambda: direct(lambda z, r0: z * _sigmoid(z)))
    pl.when((n >= t(C_RWK)) & (n < t(C_LORA)))(lambda: via_buffer(lambda zs: zs))
    pl.when(n >= t(C_LORA))(lambda: via_buffer(lora_act))


def _inproj(x, ctx, modrows, g0, w_perm, mu_full, cosf, sinf):
    B, T, D = x.shape
    CT = ctx.shape[1]
    L = T + CT
    nt = IN_COLS // TN
    return pl.pallas_call(
        _inproj_kernel,
        grid=(B, nt),
        in_specs=[pl.BlockSpec((1, T, D), lambda b, n: (b, 0, 0)),
                  pl.BlockSpec((1, CT, D), lambda b, n: (b, 0, 0)),
                  pl.BlockSpec((1, 4, D), lambda b, n: (b, 0, 0)),
                  pl.BlockSpec((1, D), lambda b, n: (0, 0)),
                  pl.BlockSpec((D, TN), lambda b, n: (0, n)),
                  pl.BlockSpec((2, TN), lambda b, n: (0, n)),
                  pl.BlockSpec((L, RET_HEAD_DIM), lambda b, n: (0, 0)),
                  pl.BlockSpec((L, RET_HEAD_DIM), lambda b, n: (0, 0))],
        out_specs=pl.BlockSpec((1, L, TN), lambda b, n: (b, 0, n)),
        out_shape=jax.ShapeDtypeStruct((B, L, IN_COLS), bf16),
        scratch_shapes=[pltpu.VMEM((L, D), bf16), pltpu.VMEM((L + 16, TN), f32)],
        compiler_params=_params(("arbitrary", "arbitrary")),
        name="inproj",
    )(x, ctx, modrows, g0, w_perm, mu_full, cosf, sinf)


def _ret_kernel(lg_ref, q_ref, k_ref, v_ref, o_ref, r_ref, tab_ref, *, ctx):
    L = q_ref.shape[1]
    Cc = RET_CHUNK
    hd = RET_HEAD_DIM
    lat = L - ctx
    nc = ctx // Cc
    nl = lat // Cc
    combos = [(h, d) for h in range(RET_HEADS) for d in (0, 1)]
    INTRA, CROSS, TAIL, DECAY = 0, 1, 2, 3

    @pl.when(pl.program_id(0) == 0)
    def _():
        ii = lax.broadcasted_iota(jnp.int32, (Cc, Cc), 0).astype(f32)
        jj = lax.broadcasted_iota(jnp.int32, (Cc, Cc), 1).astype(f32)
        for idx, (h, d) in enumerate(combos):
            lg = lg_ref[d, h]
            if d == 0:
                diff = ii - jj
                cross = jnp.exp(lg * (ii + 1.0))
                tailw = jnp.exp(lg * (Cc - 1.0 - ii))
            else:
                diff = jj - ii
                cross = jnp.exp(lg * (Cc - ii))
                tailw = jnp.exp(lg * ii)
            tab_ref[idx, INTRA] = jnp.where(diff >= 0, jnp.exp(lg * jnp.maximum(diff, 0.0)), 0.0)
            tab_ref[idx, CROSS] = cross
            tab_ref[idx, TAIL] = tailw
            tab_ref[idx, DECAY] = jnp.exp(jnp.zeros((Cc, Cc), f32) + lg * Cc)

    r_ref[...] = jnp.zeros(r_ref.shape, f32)
    o_ref[...] = jnp.zeros(o_ref.shape, f32)

    def cols(h):
        return slice(h * hd, (h + 1) * hd)

    def update(idx, kc, vc):
        ks = (kc.astype(f32) * tab_ref[idx, TAIL]).astype(bf16)
        r_ref[idx] = r_ref[idx] * tab_ref[idx, DECAY] + _dot_tn(ks, vc)

    def ctx_step(s, carry):
        for idx, (h, d) in enumerate(combos):
            row0 = pl.multiple_of(lat + (s * Cc if d == 0 else (nc - 1 - s) * Cc), Cc)
            update(idx, k_ref[0, pl.ds(row0, Cc), cols(h)], v_ref[0, pl.ds(row0, Cc), cols(h)])
        return carry

    def lat_step(s, carry):
        t0s, qs, ks, vs = [], [], [], []
        for h, d in combos:
            t0 = pl.multiple_of(s * Cc if d == 0 else (nl - 1 - s) * Cc, Cc)
            row0 = t0
            t0s.append(t0)
            qs.append(q_ref[0, pl.ds(row0, Cc), cols(h)])
            ks.append(k_ref[0, pl.ds(row0, Cc), cols(h)])
            vs.append(v_ref[0, pl.ds(row0, Cc), cols(h)])
        n = len(combos)
        sc = [(_dot_nt(qs[i], ks[i]) * tab_ref[i, INTRA]).astype(bf16) for i in range(n)]
        oc = [_dot(qs[i], r_ref[i].astype(bf16)) * tab_ref[i, CROSS] for i in range(n)]
        oi = [_dot(sc[i], vs[i]) for i in range(n)]
        for i, (h, d) in enumerate(combos):
            o_ref[0, pl.ds(t0s[i], Cc), cols(h)] += oi[i] + oc[i]
        for i in range(n):
            update(i, ks[i], vs[i])
        return carry

    lax.fori_loop(0, nc, ctx_step, 0)
    lax.fori_loop(0, nl, lat_step, 0)


def _retention(lg, z, ctx):
    B, L, _ = z.shape
    T = L - ctx
    W = RET_WIDTH
    Cc = RET_CHUNK
    blk = lambda c0: pl.BlockSpec((1, L, W), lambda b: (b, 0, c0 // W))
    nchain = 2 * RET_HEADS
    return pl.pallas_call(
        functools.partial(_ret_kernel, ctx=ctx),
        grid=(B,),
        in_specs=[pl.BlockSpec(memory_space=pltpu.SMEM), blk(C_RETQ), blk(C_RETK), blk(C_RETV)],
        out_specs=pl.BlockSpec((1, T, W), lambda b: (b, 0, 0)),
        out_shape=jax.ShapeDtypeStruct((B, T, W), f32),
        scratch_shapes=[pltpu.VMEM((nchain, Cc, Cc), f32), pltpu.VMEM((nchain, 4, Cc, Cc), f32)],
        compiler_params=_params(("arbitrary",)),
        name="ret",
    )(lg, z, z, z)


def _rwkv_kernel(k_ref, v_ref, r_ref, lo_ref, w0_ref, a0_ref, w2_ref, a2_ref, kk_ref, ka_ref, rk_ref,
                 o_ref, bo_ref, s_ref, *, nc, nb, nsub):
    d = pl.program_id(1)
    s = pl.program_id(2)
    C = RWKV_CHUNK

    @pl.when(s == 0)
    def _():
        s_ref[...] = jnp.zeros(s_ref.shape, f32)

    def sub_chunk(j, carry):
        r0 = pl.multiple_of(jnp.where(d == 1, nsub - 1 - j, j) * C, C)
        rows = lambda ref: ref.at[:, pl.ds(r0, C), :]
        _rwkv_chunk(rows(k_ref), rows(v_ref), rows(r_ref), rows(lo_ref), w0_ref, a0_ref, w2_ref, a2_ref,
                    kk_ref, ka_ref, rk_ref, o_ref.at[:, :, pl.ds(r0, C), :], bo_ref.at[:, :, pl.ds(r0, C), :],
                    s_ref, rev=d == 1, emit=s >= nc, nb=nb)
        return carry

    lax.fori_loop(0, nsub, sub_chunk, 0)


def _rwkv_chunk(k_ref, v_ref, r_ref, lo_ref, w0_ref, a0_ref, w2_ref, a2_ref, kk_ref, ka_ref, rk_ref,
                o_ref, bo_ref, s_ref, *, rev, emit, nb):
    C = RWKV_CHUNK
    G = 2 * RWKV_HEAD_DIM

    ii = lax.broadcasted_iota(jnp.int32, (C, G), 0)
    lane = lax.broadcasted_iota(jnp.int32, (C, G), 1)
    jj = lane & (RWKV_HEAD_DIM - 1)
    head0 = lane < RWKV_HEAD_DIM
    dlt = jnp.where(rev, ii - jj, jj - ii)
    strict = dlt < 0
    incl = dlt <= 0
    eye = (ii == jj).astype(f32)
    gi = lax.broadcasted_iota(jnp.int32, (G, G), 0)
    gj = lax.broadcasted_iota(jnp.int32, (G, G), 1)
    blockdiag = (gi >= RWKV_HEAD_DIM) == (gj >= RWKV_HEAD_DIM)
    ones_bd = blockdiag.astype(bf16)
    tri = incl[:, 0:C].astype(bf16)
    base = (ii >> 1) == (jj >> 1)
    offs = [((ii >> (lv + 1)) == (jj >> (lv + 1))) & ((ii >> lv) != (jj >> lv)) for lv in range(1, 6)]

    def segsum(x):
        return jnp.concatenate(
            [_bdot(x[:, p * G:(p + 1) * G], ones_bd) for p in range(RWKV_PAIRS)], axis=1)

    def stack(x):
        xb = x.astype(bf16)
        zero = jnp.zeros_like(xb)
        return jnp.concatenate([jnp.where(head0, xb, zero), jnp.where(head0, zero, xb)], axis=0)

    pre = []
    for bb in range(nb):
        kx = k_ref[bb].astype(f32)
        vx = v_ref[bb].astype(f32)
        rx = r_ref[bb].astype(f32)
        lo = lo_ref[bb, :, 0:DECAY_LORA + ICLR_LORA]
        u = w0_ref[0] + _dot(lo, w2_ref[0])
        softplus = jnp.maximum(-u, 0.0) + jnp.log1p(jnp.exp(-jnp.abs(u)))
        lw = -jnp.exp(-softplus - 0.5)
        a = jax.nn.sigmoid(a0_ref[0] + _dot(lo, a2_ref[0]))
        kkr = kx * kk_ref[...]
        kk = kkr * lax.rsqrt(segsum(kkr * kkr) + 1e-12)
        kd = kx * (1.0 + (a - 1.0) * ka_ref[...])
        be = kk * a
        bonus = segsum(rx * kd * rk_ref[...]) * vx
        cum = _split_dot_left(tri, lw)
        tot = jnp.where(rev, cum[0:1, :], cum[C - 1:C, :])
        gneg = jnp.exp(-cum)
        gh = jnp.exp(tot - cum)
        pre.append(dict(alb=-kk * jnp.exp(cum - lw), rb=rx * jnp.exp(cum), beb=be * gneg, kb=kd * gneg,
                        beh=be * gh, kh=kd * gh, etot=jnp.exp(tot), v=vx, bonus=bonus))

    units = [(bb, p) for bb in range(nb) for p in range(RWKV_PAIRS)]
    sl = lambda p: slice(p * G, (p + 1) * G)
    part = lambda un, name: pre[un[0]][name][:, sl(un[1])]
    S = {un: s_ref[un[0] * RWKV_PAIRS + un[1]] for un in units}
    Sb = {un: S[un].astype(bf16) for un in units}
    X = {un: part(un, "alb").astype(bf16) for un in units}
    Rb = {un: part(un, "rb").astype(bf16) for un in units}
    Ybs = {un: stack(part(un, "beb")) for un in units}
    Yks = {un: stack(part(un, "kb")) for un in units}
    Vb = {un: part(un, "v").astype(bf16) for un in units}
    Vs = {un: stack(part(un, "v")) for un in units}

    XR = {un: jnp.concatenate([X[un], Rb[un]], axis=0) for un in units}
    APb = {un: _dot_nt(XR[un], Ybs[un]) for un in units}
    APk = {un: _dot_nt(XR[un], Yks[un]) for un in units}
    BO = {un: _dot_nt(XR[un], Sb[un]) for un in units}
    Aab = {un: jnp.where(strict, APb[un][0:C], 0.0) for un in units}
    Aak = {un: jnp.where(strict, APk[un][0:C], 0.0).astype(bf16) for un in units}
    Pab = {un: jnp.where(incl, APb[un][C:2 * C], 0.0).astype(bf16) for un in units}
    Pak = {un: jnp.where(incl, APk[un][C:2 * C], 0.0).astype(bf16) for un in units}

    Tm = {un: eye + jnp.where(base, Aab[un], 0.0) for un in units}
    for off in offs:
        Xs = {un: _dot(jnp.where(off, Aab[un], 0.0).astype(bf16), stack(Tm[un])) for un in units}
        Tm = {un: Tm[un] + _dot(Tm[un].astype(bf16), stack(Xs[un])) for un in units}

    AV = {un: _dot(jnp.concatenate([Aak[un], Pak[un]], axis=0), Vs[un]) for un in units}
    Bm = {un: BO[un][0:C] + AV[un][0:C] for un in units}
    U = {un: _dot(Tm[un].astype(bf16), stack(Bm[un])) for un in units}
    Om = {un: BO[un][C:2 * C] + AV[un][C:2 * C] + _dot(Pab[un], stack(U[un])) for un in units}
    for un in units:
        bb, p = un
        upd = _dot_tn(jnp.concatenate([U[un].astype(bf16), Vb[un]], axis=0),
                      jnp.concatenate([part(un, "beh").astype(bf16), part(un, "kh").astype(bf16)], axis=0))
        s_ref[bb * RWKV_PAIRS + p] = jnp.where(blockdiag, S[un] * part(un, "etot") + upd, 0.0)

    @pl.when(emit)
    def _():
        for bb in range(nb):
            o_ref[0, bb] = jnp.concatenate([Om[bb, p] for p in range(RWKV_PAIRS)], axis=1)
            bo_ref[0, bb] = pre[bb]["bonus"]


def _split_dot_left(w, x):
    hi = x.astype(bf16)
    lo = (x - hi.astype(f32)).astype(bf16)
    return _dot(w, hi) + _dot(w, lo)


def _rwkv(z, w0, a0, w2p, a2p, k_k, k_a, r_k, ctx):
    B, L, _ = z.shape
    T = L - ctx
    nsub = RWKV_CHUNKS_PER_STEP
    while ctx % (nsub * RWKV_CHUNK) or T % (nsub * RWKV_CHUNK):
        nsub //= 2
    C = nsub * RWKV_CHUNK
    nc, nl = ctx // C, T // C
    W = RWKV_WIDTH

    def chunk(d, s):
        fwd = jnp.where(s < nc, nl + s, s - nc)
        bwd = jnp.where(s < nc, nl + nc - 1 - s, nl - 1 - (s - nc))
        return jnp.where(d == 0, fwd, bwd)

    def ochunk(d, s):
        sl = jnp.maximum(s - nc, 0)
        return jnp.where(d == 0, sl, nl - 1 - sl)

    nb = RWKV_SAMPLES_PER_STEP if B % RWKV_SAMPLES_PER_STEP == 0 else 1
    zblk = lambda c0, w: pl.BlockSpec((nb, C, w), lambda b, d, s: (b, chunk(d, s), c0 // w))
    dpar = lambda r: pl.BlockSpec((1, r, W), lambda b, d, s: (d, 0, 0))
    par = pl.BlockSpec((1, W), lambda b, d, s: (0, 0))
    oblk = pl.BlockSpec((1, nb, C, W), lambda b, d, s: (d, b, ochunk(d, s), 0))
    G = 2 * RWKV_HEAD_DIM
    return pl.pallas_call(
        functools.partial(_rwkv_kernel, nc=nc, nb=nb, nsub=nsub),
        grid=(B // nb, 2, nc + nl),
        in_specs=[zblk(C_RWK, W), zblk(C_RWV, W), zblk(C_RWR, W), zblk(C_LORA, 256),
                  dpar(1), dpar(1), dpar(G), dpar(G), par, par, par],
        out_specs=[oblk, oblk],
        out_shape=[jax.ShapeDtypeStruct((2, B, T, W), f32), jax.ShapeDtypeStruct((2, B, T, W), f32)],
        scratch_shapes=[pltpu.VMEM((nb * RWKV_PAIRS, G, G), f32)],
        compiler_params=_params(("arbitrary", "arbitrary", "arbitrary")),
        name="rwkv",
    )(z, z, z, z, w0, a0, w2p, a2p, k_k, k_a, r_k)


def _merge_kernel(x_ref, ret_ref, rw0_ref, rw1_ref, b0_ref, b1_ref, gate_ref, retg_ref, gd_ref, mod_ref,
                  ng_ref, gn_ref, lng_ref, lnb_ref, g2_ref, wbr_ref, wbw_ref, wout_ref, wrh_ref, wrl_ref,
                  x1_ref, h2_ref, lt_ref):
    hd = RET_HEAD_DIM
    ret = ret_ref[0]
    parts = []
    for hh in range(RET_HEADS):
        xh = ret[:, hh * hd:(hh + 1) * hd]
        mu = jnp.mean(xh, axis=-1, keepdims=True)
        dv = xh - mu
        var = jnp.mean(dv * dv, axis=-1, keepdims=True)
        parts.append(dv * lax.rsqrt(var + RET_EPS))
    yr = retg_ref[0].astype(f32) * (jnp.concatenate(parts, axis=1) * gn_ref[...])
    y_ret = _bdot(yr, wbr_ref[...])
    W = RWKV_WIDTH
    gi = lax.broadcasted_iota(jnp.int32, (W, W), 0)
    gj = lax.broadcasted_iota(jnp.int32, (W, W), 1)
    ones_bd = ((gi >> 6) == (gj >> 6)).astype(bf16)
    o = rw0_ref[0, 0] + rw1_ref[0, 0]
    mu = _split_dot(o, ones_bd) * (1.0 / RWKV_HEAD_DIM)
    dv = o - mu
    var = _bdot(dv * dv, ones_bd) * (1.0 / RWKV_HEAD_DIM)
    yw = dv * lax.rsqrt(var + RWKV_EPS) * lng_ref[...] + lnb_ref[...]
    gate = _dot(gd_ref[0], g2_ref[...])
    yw = (yw + b0_ref[0, 0] + b1_ref[0, 0]) * gate
    y_rw = _bdot(yw, wbw_ref[...])
    D = y_ret.shape[1]
    g = gate_ref[0].astype(f32)
    m = g[:, :D] * y_ret + g[:, D:] * y_rw
    y = _bdot(m, wout_ref[...])

    def rms(v, gg):
        return v * lax.rsqrt(jnp.mean(v * v, axis=-1, keepdims=True) + NORM_EPS) * gg

    x1 = x_ref[0] + mod_ref[0, 0:1, :] * rms(y, ng_ref[0:1, :])
    x1_ref[0] = x1
    h2 = rms(x1, ng_ref[1:2, :]) * (1.0 + mod_ref[0, 2:3, :]) + mod_ref[0, 1:2, :]
    h2b = h2.astype(bf16)
    h2_ref[0] = h2b
    h2l = (h2 - h2b.astype(f32)).astype(bf16)
    lgt = _dot(h2b, wrh_ref[...]) + _dot(h2l, wrh_ref[...]) + _dot(h2b, wrl_ref[...])
    lt_ref[0] = lgt.T[0:N_EXPERTS, :]


def _merge(x, ret_o, rw_o, bonus, z, mod2, ng12, gn, lng, lnb, g2, wbr, wbw, wout, wrh, wrl, ctx):
    B, T, D = x.shape
    tm = MERGE_ROWS if T % MERGE_ROWS == 0 else ROWS
    co = 0
    W = RWKV_WIDTH
    row = lambda w: pl.BlockSpec((1, tm, w), lambda b, i: (b, i, 0))
    dblk = lambda dd: pl.BlockSpec((1, 1, tm, W), lambda b, i: (dd, b, i, 0))
    zblk = lambda c0, w: pl.BlockSpec((1, tm, w), lambda b, i: (b, co + i, c0 // w))
    full = lambda a: pl.BlockSpec(a.shape, lambda b, i: (0,) * a.ndim)
    return pl.pallas_call(
        _merge_kernel,
        grid=(B, T // tm),
        in_specs=[row(D), row(W), dblk(0), dblk(1), dblk(0), dblk(1),
                  zblk(C_MERGE, 2 * D), zblk(C_RETG, W), zblk(C_LORA + 128, 128),
                  pl.BlockSpec((1, 3, D), lambda b, i: (b, 0, 0)),
                  full(ng12), full(gn), full(lng), full(lnb), full(g2), full(wbr), full(wbw), full(wout),
                  full(wrh), full(wrl)],
        out_specs=[row(D), row(D), pl.BlockSpec((1, N_EXPERTS, tm), lambda b, i: (b, 0, i))],
        out_shape=[jax.ShapeDtypeStruct((B, T, D), f32), jax.ShapeDtypeStruct((B, T, D), bf16),
                   jax.ShapeDtypeStruct((B, N_EXPERTS, T), f32)],
        compiler_params=_params(("arbitrary", "arbitrary")),
        name="merge",
    )(x, ret_o, rw_o, rw_o, bonus, bonus, z, z, z, mod2, ng12, gn, lng, lnb, g2, wbr, wbw, wout, wrh, wrl)


def _route_kernel(lt_ref, slot_ref, rt_ref, cnt_ref, slotf_ref, gate_ref, *, cap):
    B, NE, T = lt_ref.shape
    lg = lt_ref[...]
    mx = jnp.max(lg, axis=1, keepdims=True)
    ex = jnp.exp(lg - mx)
    aff = (ex / jnp.sum(ex, axis=1, keepdims=True)).reshape(B * NE, T)
    E = B * NE

    def count_ge(cand):
        return jnp.sum((aff >= cand).astype(f32), axis=1, keepdims=True)

    def exp_step(_, kk):
        k_lo, k_hi = kk
        km = jnp.floor((k_lo + k_hi) * 0.5)
        ok = count_ge(jnp.exp2(-km)) >= cap
        return jnp.where(ok, k_lo, km), jnp.where(ok, km, k_hi)

    k_lo, k_hi = lax.fori_loop(0, ROUTE_EXP_STEPS, exp_step,
                               (jnp.full((E, 1), -1.0, f32), jnp.full((E, 1), ROUTE_MAX_EXP, f32)))
    lo0 = jnp.where(k_hi >= ROUTE_MAX_EXP, 0.0, jnp.exp2(-k_hi))
    hi0 = jnp.exp2(-k_lo)

    def val_step(_, lh):
        lo, hi = lh
        mid = (lo + hi) * 0.5
        ok = count_ge(mid) >= cap
        return jnp.where(ok, mid, lo), jnp.where(ok, hi, mid)

    lo, hi = lax.fori_loop(0, ROUTE_VAL_STEPS, val_step, (lo0, hi0))
    gt = aff >= hi
    eq = (aff >= lo) & (aff < hi)
    need = cap - jnp.sum(gt.astype(f32), axis=1, keepdims=True)
    tri = (lax.broadcasted_iota(jnp.int32, (T, T), 0) < lax.broadcasted_iota(jnp.int32, (T, T), 1)).astype(bf16)
    eq_before = _dot(eq.astype(bf16), tri)
    sel = gt | (eq & (eq_before < need))
    slot = _dot(sel.astype(bf16), tri)
    slot_f = jnp.where(sel, slot, -1.0)
    slot_ref[...] = slot_f.astype(jnp.int32).reshape(B, NE, T)
    before = (lax.broadcasted_iota(jnp.int32, (T, 128), 0)
              < lax.broadcasted_iota(jnp.int32, (T, 128), 1) * GATHER_TILE).astype(bf16)
    cnt_ref[...] = _dot(sel.astype(bf16), before).astype(jnp.int32).reshape(B, NE, 128)
    slotf_ref[...] = slot_f.reshape(B, NE, T)
    gate_ref[...] = jnp.where(sel, aff, 0.0).reshape(B, NE, T)

    def transpose_sample(b, carry):
        packed = jnp.concatenate([slotf_ref[b], gate_ref[b], jnp.zeros((128 - 2 * NE, T), f32)], axis=0)
        rt_ref[b] = packed.T.astype(bf16)
        return carry

    lax.fori_loop(0, B, transpose_sample, 0)


def _route(lt, cap):
    B, E, T = lt.shape
    assert cap <= 256 and T % GATHER_TILE == 0 and T // GATHER_TILE < 128
    return pl.pallas_call(
        functools.partial(_route_kernel, cap=cap),
        grid=(1,),
        in_specs=[pl.BlockSpec((B, E, T), lambda i: (0, 0, 0))],
        out_specs=[pl.BlockSpec((B, E, T), lambda i: (0, 0, 0)), pl.BlockSpec((B, T, 128), lambda i: (0, 0, 0)),
                   pl.BlockSpec((B, E, 128), lambda i: (0, 0, 0))],
        out_shape=[jax.ShapeDtypeStruct((B, E, T), jnp.int32), jax.ShapeDtypeStruct((B, T, 128), bf16),
                   jax.ShapeDtypeStruct((B, E, 128), jnp.int32)],
        scratch_shapes=[pltpu.VMEM((B, E, T), f32), pltpu.VMEM((B, E, T), f32)],
        compiler_params=_params(("arbitrary",)),
        name="route",
    )(lt)


def _ffn_kernel(cnt_ref, slot_ref, h_ref, wg_ref, wu_ref, wd_ref, o_ref, wgb_ref, wub_ref, wdb_ref, xg_ref, *, cap):
    T = h_ref.shape[1]
    e = pl.program_id(0)

    @pl.when(pl.program_id(1) == 0)
    def _():
        wgb_ref[...] = wg_ref[0].astype(bf16)
        wub_ref[...] = wu_ref[0].astype(bf16)
        wdb_ref[...] = wd_ref[0].astype(bf16)

    nb = h_ref.shape[0]
    GT, GW, GA = GATHER_TILE, GATHER_WINDOW, GATHER_ALIGN
    nt = T // GT
    for bb in range(nb):
        b = pl.program_id(1) * nb + bb
        base = [pl.multiple_of(lax.shift_left(lax.shift_right_logical(cnt_ref[b, e, j], GATHER_ALIGN_LOG2),
                                              GATHER_ALIGN_LOG2), GA) for j in range(nt)]
        fits = cnt_ref[b, e, 1] - base[0] <= GW
        for j in range(1, nt):
            fits = fits & (cnt_ref[b, e, j + 1] - base[j] <= GW)

        @pl.when(fits)
        def _():
            xg_ref[bb] = jnp.zeros(xg_ref.shape[1:], f32)
            for j in range(nt):
                rows = base[j] + lax.broadcasted_iota(jnp.int32, (GW, GT), 0)
                onehot = (slot_ref[bb, 0, j:j + 1, :] == rows).astype(bf16)
                xg_ref[bb, pl.ds(base[j], GW), :] += _dot(onehot, h_ref[bb, j * GT:(j + 1) * GT, :])

        @pl.when(jnp.logical_not(fits))
        def _():
            rows = lax.broadcasted_iota(jnp.int32, (cap, GT), 0)
            acc = jnp.zeros((cap, h_ref.shape[2]), f32)
            for j in range(nt):
                onehot = (slot_ref[bb, 0, j:j + 1, :] == rows).astype(bf16)
                acc = acc + _dot(onehot, h_ref[bb, j * GT:(j + 1) * GT, :])
            xg_ref[bb, 0:cap, :] = acc

    xg = jnp.concatenate([xg_ref[bb, 0:cap, :].astype(bf16) for bb in range(nb)], axis=0)
    hg = _dot(xg, wgb_ref[...])
    hu = _dot(xg, wub_ref[...])
    hid = (hg * _sigmoid(hg) * hu).astype(bf16)
    out = _dot(hid, wdb_ref[...]).astype(bf16)
    for bb in range(nb):
        o_ref[bb, 0] = out[bb * cap:(bb + 1) * cap]


def _ffn(cnt, slot4, h2, wg, wu, wd, cap):
    B, T, D = h2.shape
    E, _, F = wg.shape
    nb = FFN_SAMPLES_PER_STEP if B % FFN_SAMPLES_PER_STEP == 0 else 1
    grid_spec = pltpu.PrefetchScalarGridSpec(
        num_scalar_prefetch=1,
        grid=(E, B // nb),
        in_specs=[pl.BlockSpec((nb, 1, T // GATHER_TILE, GATHER_TILE), lambda e, b, c: (b, e, 0, 0)),
                  pl.BlockSpec((nb, T, D), lambda e, b, c: (b, 0, 0)),
                  pl.BlockSpec((1, D, F), lambda e, b, c: (e, 0, 0)),
                  pl.BlockSpec((1, D, F), lambda e, b, c: (e, 0, 0)),
                  pl.BlockSpec((1, F, D), lambda e, b, c: (e, 0, 0))],
        out_specs=pl.BlockSpec((nb, 1, cap, D), lambda e, b, c: (b, e, 0, 0)),
        scratch_shapes=[pltpu.VMEM((D, F), bf16), pltpu.VMEM((D, F), bf16), pltpu.VMEM((F, D), bf16),
                        pltpu.VMEM((nb, cap + GATHER_WINDOW, D), f32)])
    return pl.pallas_call(
        functools.partial(_ffn_kernel, cap=cap),
        grid_spec=grid_spec,
        out_shape=jax.ShapeDtypeStruct((B, E, cap, D), bf16),
        compiler_params=_params(("arbitrary", "arbitrary"), FFN_VMEM_LIMIT),
        name="ffn",
    )(cnt, slot4, h2, wg, wu, wd)


def _combine_kernel(rt_ref, eo_ref, x1_ref, mod_ref, ng_ref, o_ref, *, cap):
    rt = rt_ref[0].astype(f32)
    tm = rt.shape[0]
    E = eo_ref.shape[1]
    ci = lax.broadcasted_iota(jnp.int32, (tm, cap), 1).astype(f32)
    y = jnp.zeros((tm, eo_ref.shape[3]), f32)
    for e in range(E):
        p = jnp.where(rt[:, e:e + 1] == ci, rt[:, E + e:E + e + 1], 0.0).astype(bf16)
        y = y + _dot(p, eo_ref[0, e])
    yn = y * lax.rsqrt(jnp.mean(y * y, axis=-1, keepdims=True) + NORM_EPS) * ng_ref[...]
    o_ref[0] = x1_ref[0] + mod_ref[0] * yn


def _combine(rt, eo, x1, g2mod, ng3, cap):
    B, T, D = x1.shape
    E = eo.shape[1]
    tm = 512 if T % 512 == 0 else T
    return pl.pallas_call(
        functools.partial(_combine_kernel, cap=cap),
        grid=(B, T // tm),
        in_specs=[pl.BlockSpec((1, tm, 128), lambda b, i: (b, i, 0)),
                  pl.BlockSpec((1, E, cap, D), lambda b, i: (b, 0, 0, 0)),
                  pl.BlockSpec((1, tm, D), lambda b, i: (b, i, 0)),
                  pl.BlockSpec((1, 1, D), lambda b, i: (b, 0, 0)),
                  pl.BlockSpec((1, D), lambda b, i: (0, 0))],
        out_specs=pl.BlockSpec((1, tm, D), lambda b, i: (b, i, 0)),
        out_shape=jax.ShapeDtypeStruct((B, T, D), f32),
        compiler_params=_params(("arbitrary", "arbitrary")),
        name="combine",
    )(rt, eo, x1, g2mod, ng3)


def _permute_columns(w):
    sk, sv, rk, rv, wd, ad = 0, 512, 1024, 1536, 2048, 2112
    q0 = 2176
    rq, rg, rr, gd, mg = q0, q0 + 512, q0 + 1024, q0 + 1536, q0 + 1664
    order = [(mg, 2048), (sk, 512), (sv, 512), (rq, 512), (rg, 512), (rk, 512), (rv, 512), (rr, 512),
             (wd, 64), (ad, 64), (gd, 128)]
    parts = [w[:, a:a + n] for a, n in order]
    parts.append(jnp.zeros((w.shape[0], IN_COLS - USED_COLS), w.dtype))
    return jnp.concatenate(parts, axis=1)


def _rope_tables(T, CT):
    t = jnp.arange(T)
    nfreq = RET_HEAD_DIM // 4
    inv = ROPE_BASE ** (-jnp.arange(nfreq, dtype=f32) / nfreq)
    ang = jnp.concatenate([(t // GRID_W).astype(f32)[:, None] * inv,
                           (t % GRID_W).astype(f32)[:, None] * inv], axis=-1)
    cos, sin = jnp.cos(ang), jnp.sin(ang)
    cosf = jnp.concatenate([cos, cos], axis=1)
    sinf = jnp.concatenate([-sin, sin], axis=1)
    return (jnp.concatenate([cosf, jnp.ones((CT, RET_HEAD_DIM), f32)], axis=0),
            jnp.concatenate([sinf, jnp.zeros((CT, RET_HEAD_DIM), f32)], axis=0))


def kernel(x, c, ctx, c_ctx, w_mod, b_mod, norm_g, w_in, ret_log_decay, ret_gn_g, rwkv_mu, rwkv_k_k, rwkv_k_a,
           rwkv_r_k, rwkv_w0, rwkv_w2, rwkv_a0, rwkv_a2, rwkv_g2, rwkv_ln_g, rwkv_ln_b, w_br_ret, w_br_rwkv,
           w_out, w_router, w_gate, w_up, w_down):
    B, T, D = x.shape
    CT = ctx.shape[1]
    assert w_mod.shape[0] == 1 and D == D_MODEL
    assert CT % ROWS == 0 and T % ROWS == 0 and T % GRID_W == 0
    cap = CAPACITY_FACTOR * T // N_EXPERTS
    assert cap % 8 == 0

    mrows = -(-(B + 1) // 8) * 8
    cc = jnp.zeros((mrows, D), f32).at[:B].set(c).at[B].set(c_ctx)
    mod = _modulation(cc, w_mod[0], b_mod[0])
    lat = mod[:B].reshape(B, N_MOD, D)
    cm = jnp.broadcast_to(mod[B].reshape(1, N_MOD, D), (B, N_MOD, D))
    modrows = jnp.concatenate([lat[:, 0:2], cm[:, 0:2]], axis=1)

    w_perm = _permute_columns(w_in[0]).astype(bf16)
    mu = rwkv_mu[0]
    ss = 2 * RWKV_WIDTH + DECAY_LORA + ICLR_LORA
    mu_full = jnp.zeros((2, IN_COLS), f32)
    mu_full = mu_full.at[:, C_RWK:C_RWK + 1024].set(mu[:, 0:1024])
    mu_full = mu_full.at[:, C_RWR:C_RWR + 512].set(mu[:, ss:ss + 512])
    mu_full = mu_full.at[:, C_LORA:C_LORA + 128].set(mu[:, 1024:ss])
    mu_full = mu_full.at[:, C_LORA + 128:USED_COLS].set(mu[:, ss + 512:])
    cosf, sinf = _rope_tables(T, CT)
    z = _inproj(x, ctx, modrows, norm_g[0, 0:1], w_perm, mu_full, cosf, sinf)

    lg = -jnp.exp(ret_log_decay[0].astype(f32))
    ret_o = _retention(lg, z, CT)

    G = 2 * RWKV_HEAD_DIM
    w2p = jnp.zeros((2, G, RWKV_WIDTH), f32).at[:, :DECAY_LORA].set(rwkv_w2[0]).astype(bf16)
    a2p = jnp.zeros((2, G, RWKV_WIDTH), f32).at[:, DECAY_LORA:].set(rwkv_a2[0]).astype(bf16)
    rw_o, bonus = _rwkv(z, rwkv_w0[0][:, None, :], rwkv_a0[0][:, None, :], w2p, a2p,
                        rwkv_k_k[0][None], rwkv_k_a[0][None], rwkv_r_k[0][None], CT)

    mod2 = jnp.stack([lat[:, 2], lat[:, 3], lat[:, 4]], axis=1)
    wr_pad = jnp.zeros((D, 128), f32).at[:, :N_EXPERTS].set(w_router[0])
    wr_hi = wr_pad.astype(bf16)
    wr_lo = (wr_pad - wr_hi.astype(f32)).astype(bf16)
    x1, h2, lt = _merge(x, ret_o, rw_o, bonus, z, mod2, norm_g[0, 1:3], ret_gn_g[0][None], rwkv_ln_g[0][None],
                        rwkv_ln_b[0][None], rwkv_g2[0].astype(bf16), w_br_ret[0].astype(bf16),
                        w_br_rwkv[0].astype(bf16), w_out[0].astype(bf16), wr_hi, wr_lo, CT)

    slot, rt, cnt = _route(lt, cap)
    eo = _ffn(cnt[:, :, :T // GATHER_TILE + 1], slot.reshape(B, N_EXPERTS, T // GATHER_TILE, GATHER_TILE), h2,
              w_gate[0], w_up[0], w_down[0], cap)
    return _combine(rt, eo, x1, lat[:, 5:6], norm_g[0, 3:4], cap)
```

```python
import functools

import numpy as np
import jax
import jax.numpy as jnp
from jax import lax
from jax.experimental import pallas as pl
from jax.experimental.pallas import tpu as pltpu

f32 = jnp.float32
bf16 = jnp.bfloat16

D_MODEL = 1024
GRID_W = 64
RET_HEAD_DIM = 128
RET_WIDTH = 512
RET_HEADS = 4
RET_CHUNK = 128
RET_EPS = 1e-5
ROPE_BASE = 10000.0
RWKV_HEAD_DIM = 64
RWKV_WIDTH = 512
RWKV_PAIRS = 4
RWKV_CHUNK = 64
RWKV_CHUNKS_PER_STEP = 4
RWKV_SAMPLES_PER_STEP = 4
DECAY_LORA = 64
ICLR_LORA = 64
GATE_LORA = 128
RWKV_EPS = 64e-5
N_EXPERTS = 16
EXPERT_FF = 1024
CAPACITY_FACTOR = 2
N_MOD = 6
NORM_EPS = 1e-6

C_MERGE, C_RETK, C_RETV, C_RETQ, C_RETG = 0, 2048, 2560, 3072, 3584
C_RWK, C_RWV, C_RWR, C_LORA = 4096, 4608, 5120, 5632
USED_COLS = 5888
TN = 512
IN_COLS = -(-USED_COLS // TN) * TN
ROWS = 256
STEP_ROWS = 768
POST_ROWS = 128
MERGE_ROWS = 512

ROUTE_MAX_EXP = 126.0
ROUTE_EXP_STEPS = 8
ROUTE_VAL_STEPS = 26

VMEM_LIMIT = 56 * 1024 * 1024
FFN_SAMPLES_PER_STEP = 2
GATHER_TILE = 256
GATHER_ALIGN_LOG2 = 4
GATHER_ALIGN = 1 << GATHER_ALIGN_LOG2
GATHER_WINDOW = 80
SCATTER_WINDOW = 128
FFN_VMEM_LIMIT = 62 * 1024 * 1024


def _dot(a, b):
    return jnp.dot(a, b, preferred_element_type=f32)


def _dot_nt(a, b):
    return lax.dot_general(a, b, (((1,), (1,)), ((), ())), preferred_element_type=f32)


def _dot_tn(a, b):
    return lax.dot_general(a, b, (((0,), (0,)), ((), ())), preferred_element_type=f32)


def _bdot(a, b):
    return _dot(a.astype(bf16), b.astype(bf16))


def _split_dot(x, w):
    hi = x.astype(bf16)
    lo = (x - hi.astype(f32)).astype(bf16)
    return _dot(hi, w) + _dot(lo, w)


def _sigmoid(x):
    return 0.5 * jnp.tanh(0.5 * x) + 0.5


def _params(sem, limit=VMEM_LIMIT):
    return pltpu.CompilerParams(dimension_semantics=sem, vmem_limit_bytes=limit)


def _mod_kernel(c_ref, w_ref, b_ref, o_ref):
    c = c_ref[...]
    s = c * jax.nn.sigmoid(c)
    o_ref[...] = _bdot(s, w_ref[...]) + b_ref[...]


def _modulation(cc, w_mod, b_mod):
    m, d = cc.shape
    n = w_mod.shape[1]
    tn = 512
    return pl.pallas_call(
        _mod_kernel,
        grid=(n // tn,),
        in_specs=[pl.BlockSpec((m, d), lambda j: (0, 0)),
                  pl.BlockSpec((d, tn), lambda j: (0, j)),
                  pl.BlockSpec((1, tn), lambda j: (0, j))],
        out_specs=pl.BlockSpec((m, tn), lambda j: (0, j)),
        out_shape=jax.ShapeDtypeStruct((m, n), f32),
        compiler_params=_params(("arbitrary",)),
        name="mod",
    )(cc, w_mod, b_mod.reshape(1, n))


def _inproj_kernel(x_ref, c_ref, mod_ref, g_ref, w_ref, mu_ref, cos_ref, sin_ref, o_ref, h_ref, z_ref):
    n = pl.program_id(1)
    T = x_ref.shape[1]
    L = T + c_ref.shape[1]
    nlat = T // ROWS
    nchunk = L // ROWS
    PAD = 8

    SR = STEP_ROWS if L % STEP_ROWS == 0 else ROWS

    @pl.when(n == 0)
    def _():
        z_ref[0:PAD, :] = jnp.zeros((PAD, TN), f32)
        z_ref[PAD + L:PAD + L + PAD, :] = jnp.zeros((PAD, TN), f32)

        def norm_chunk(src_ref, s0, r0, o):
            xb = src_ref[0, pl.ds(s0, ROWS), :]
            y = xb * lax.rsqrt(jnp.mean(xb * xb, axis=-1, keepdims=True) + NORM_EPS) * g_ref[...]
            sh = mod_ref[0, o:o + 1, :]
            sc = mod_ref[0, o + 1:o + 2, :]
            h_ref[pl.ds(r0, ROWS), :] = (y * (1.0 + sc) + sh).astype(bf16)

        def lat_body(i, carry):
            r0 = pl.multiple_of(i * ROWS, ROWS)
            norm_chunk(x_ref, r0, r0, 0)
            return carry

        lax.fori_loop(0, nlat, lat_body, 0)
        for j in range(nlat, nchunk):
            norm_chunk(c_ref, j * ROWS - T, j * ROWS, 2)

    PR = POST_ROWS

    def rope(scale):
        def post(z, r0):
            if scale != 1.0:
                z = z * scale
            cs = cos_ref[pl.ds(r0, SR), :]
            sn = sin_ref[pl.ds(r0, SR), :]
            parts = []
            for hh in range(TN // RET_HEAD_DIM):
                zh = z[:, hh * RET_HEAD_DIM:(hh + 1) * RET_HEAD_DIM]
                parts.append(zh * cs + pltpu.roll(zh, RET_HEAD_DIM // 2, 1) * sn)
            return jnp.concatenate(parts, axis=1)
        return post

    def shifted(r0):
        win = z_ref[pl.ds(r0, PR + 2 * PAD), :]
        prev = win[PAD - 1:PAD - 1 + PR]
        z = win[PAD:PAD + PR]
        nxt = win[PAD + 1:PAD + 1 + PR]
        rid = r0 + lax.broadcasted_iota(jnp.int32, (PR, 1), 0)
        prev = jnp.where(rid == T, 0.0, prev)
        nxt = jnp.where(rid == T - 1, 0.0, nxt)
        return z + mu_ref[0:1, :] * (prev - z) + mu_ref[1:2, :] * (nxt - z)

    def lora_act(zs):
        lane = lax.broadcasted_iota(jnp.int32, (1, TN), 1)
        return jnp.where(lane < DECAY_LORA, jnp.tanh(zs),
                         jnp.where(lane < DECAY_LORA + ICLR_LORA, zs, jax.nn.sigmoid(zs)))

    def product(r0):
        return _dot(h_ref[pl.ds(r0, SR), :], w_ref[...])

    def direct(post):
        def body(i, carry):
            r0 = pl.multiple_of(i * SR, SR)
            o_ref[0, pl.ds(r0, SR), :] = post(product(r0), r0).astype(bf16)
            return carry
        lax.fori_loop(0, L // SR, body, 0)

    def via_buffer(act):
        def mm_body(i, carry):
            r0 = pl.multiple_of(i * SR, SR)
            z_ref[pl.ds(PAD + r0, SR), :] = product(r0)
            return carry
        lax.fori_loop(0, L // SR, mm_body, 0)

        def post_body(i, carry):
            r0 = pl.multiple_of(i * PR, PR)
            o_ref[0, pl.ds(r0, PR), :] = act(shifted(r0)).astype(bf16)
            return carry
        lax.fori_loop(0, L // PR, post_body, 0)

    t = lambda c: c // TN
    pl.when(n < t(C_RETK))(lambda: direct(lambda z, r0: _sigmoid(z)))
    pl.when((n >= t(C_RETK)) & (n < t(C_RETV)))(lambda: direct(rope(RET_HEAD_DIM ** -0.5)))
    pl.when((n >= t(C_RETV)) & (n < t(C_RETQ)))(lambda: direct(lambda z, r0: z))
    pl.when((n >= t(C_RETQ)) & (n < t(C_RETG)))(lambda: direct(rope(1.0)))
    pl.when((n >= t(C_RETG)) & (n < t(C_RWK)))(lambda: direct(lambda z, r0: z * _sigmoid(z)))
    pl.when((n >= t(C_RWK)) & (n < t(C_LORA)))(lambda: via_buffer(lambda zs: zs))
    pl.when(n >= t(C_LORA))(lambda: via_buffer(lora_act))


def _inproj(x, ctx, modrows, g0, w_perm, mu_full, cosf, sinf):
    B, T, D = x.shape
    CT = ctx.shape[1]
    L = T + CT
    nt = IN_COLS // TN
    return pl.pallas_call(
        _inproj_kernel,
        grid=(B, nt),
        in_specs=[pl.BlockSpec((1, T, D), lambda b, n: (b, 0, 0)),
                  pl.BlockSpec((1, CT, D), lambda b, n: (b, 0, 0)),
                  pl.BlockSpec((1, 4, D), lambda b, n: (b, 0, 0)),
                  pl.BlockSpec((1, D), lambda b, n: (0, 0)),
                  pl.BlockSpec((D, TN), lambda b, n: (0, n)),
                  pl.BlockSpec((2, TN), lambda b, n: (0, n)),
                  pl.BlockSpec((L, RET_HEAD_DIM), lambda b, n: (0, 0)),
                  pl.BlockSpec((L, RET_HEAD_DIM), lambda b, n: (0, 0))],
        out_specs=pl.BlockSpec((1, L, TN), lambda b, n: (b, 0, n)),
        out_shape=jax.ShapeDtypeStruct((B, L, IN_COLS), bf16),
        scratch_shapes=[pltpu.VMEM((L, D), bf16), pltpu.VMEM((L + 16, TN), f32)],
        compiler_params=_params(("arbitrary", "arbitrary")),
        name="inproj",
    )(x, ctx, modrows, g0, w_perm, mu_full, cosf, sinf)


def _ret_kernel(lg_ref, q_ref, k_ref, v_ref, o_ref, r_ref, tab_ref, *, ctx):
    L = q_ref.shape[1]
    Cc = RET_CHUNK
    hd = RET_HEAD_DIM
    lat = L - ctx
    nc = ctx // Cc
    nl = lat // Cc
    combos = [(h, d) for h in range(RET_HEADS) for d in (0, 1)]
    INTRA, CROSS, TAIL, DECAY = 0, 1, 2, 3

    @pl.when(pl.program_id(0) == 0)
    def _():
        ii = lax.broadcasted_iota(jnp.int32, (Cc, Cc), 0).astype(f32)
        jj = lax.broadcasted_iota(jnp.int32, (Cc, Cc), 1).astype(f32)
        for idx, (h, d) in enumerate(combos):
            lg = lg_ref[d, h]
            if d == 0:
                diff = ii - jj
                cross = jnp.exp(lg * (ii + 1.0))
                tailw = jnp.exp(lg * (Cc - 1.0 - ii))
            else:
                diff = jj - ii
                cross = jnp.exp(lg * (Cc - ii))
                tailw = jnp.exp(lg * ii)
            tab_ref[idx, INTRA] = jnp.where(diff >= 0, jnp.exp(lg * jnp.maximum(diff, 0.0)), 0.0)
            tab_ref[idx, CROSS] = cross
            tab_ref[idx, TAIL] = tailw
            tab_ref[idx, DECAY] = jnp.exp(jnp.zeros((Cc, Cc), f32) + lg * Cc)

    r_ref[...] = jnp.zeros(r_ref.shape, f32)
    o_ref[...] = jnp.zeros(o_ref.shape, f32)

    def cols(h):
        return slice(h * hd, (h + 1) * hd)

    def update(idx, kc, vc):
        ks = (kc.astype(f32) * tab_ref[idx, TAIL]).astype(bf16)
        r_ref[idx] = r_ref[idx] * tab_ref[idx, DECAY] + _dot_tn(ks, vc)

    def ctx_step(s, carry):
        for idx, (h, d) in enumerate(combos):
            row0 = pl.multiple_of(lat + (s * Cc if d == 0 else (nc - 1 - s) * Cc), Cc)
            update(idx, k_ref[0, pl.ds(row0, Cc), cols(h)], v_ref[0, pl.ds(row0, Cc), cols(h)])
        return carry

    def lat_step(s, carry):
        t0s, qs, ks, vs = [], [], [], []
        for h, d in combos:
            t0 = pl.multiple_of(s * Cc if d == 0 else (nl - 1 - s) * Cc, Cc)
            row0 = t0
            t0s.append(t0)
            qs.append(q_ref[0, pl.ds(row0, Cc), cols(h)])
            ks.append(k_ref[0, pl.ds(row0, Cc), cols(h)])
            vs.append(v_ref[0, pl.ds(row0, Cc), cols(h)])
        n = len(combos)
        sc = [(_dot_nt(qs[i], ks[i]) * tab_ref[i, INTRA]).astype(bf16) for i in range(n)]
        oc = [_dot(qs[i], r_ref[i].astype(bf16)) * tab_ref[i, CROSS] for i in range(n)]
        oi = [_dot(sc[i], vs[i]) for i in range(n)]
        for i, (h, d) in enumerate(combos):
            o_ref[0, pl.ds(t0s[i], Cc), cols(h)] += oi[i] + oc[i]
        for i in range(n):
            update(i, ks[i], vs[i])
        return carry

    lax.fori_loop(0, nc, ctx_step, 0)
    lax.fori_loop(0, nl, lat_step, 0)


def _retention(lg, z, ctx):
    B, L, _ = z.shape
    T = L - ctx
    W = RET_WIDTH
    Cc = RET_CHUNK
    blk = lambda c0: pl.BlockSpec((1, L, W), lambda b: (b, 0, c0 // W))
    nchain = 2 * RET_HEADS
    return pl.pallas_call(
        functools.partial(_ret_kernel, ctx=ctx),
        grid=(B,),
        in_specs=[pl.BlockSpec(memory_space=pltpu.SMEM), blk(C_RETQ), blk(C_RETK), blk(C_RETV)],
        out_specs=pl.BlockSpec((1, T, W), lambda b: (b, 0, 0)),
        out_shape=jax.ShapeDtypeStruct((B, T, W), f32),
        scratch_shapes=[pltpu.VMEM((nchain, Cc, Cc), f32), pltpu.VMEM((nchain, 4, Cc, Cc), f32)],
        compiler_params=_params(("arbitrary",)),
        name="ret",
    )(lg, z, z, z)


def _rwkv_kernel(k_ref, v_ref, r_ref, lo_ref, w0_ref, a0_ref, w2_ref, a2_ref, kk_ref, ka_ref, rk_ref,
                 o_ref, bo_ref, s_ref, *, nc, nb, nsub):
    d = pl.program_id(1)
    s = pl.program_id(2)
    C = RWKV_CHUNK

    @pl.when(s == 0)
    def _():
        s_ref[...] = jnp.zeros(s_ref.shape, f32)

    def sub_chunk(j, carry):
        r0 = pl.multiple_of(jnp.where(d == 1, nsub - 1 - j, j) * C, C)
        rows = lambda ref: ref.at[:, pl.ds(r0, C), :]
        _rwkv_chunk(rows(k_ref), rows(v_ref), rows(r_ref), rows(lo_ref), w0_ref, a0_ref, w2_ref, a2_ref,
                    kk_ref, ka_ref, rk_ref, o_ref.at[:, :, pl.ds(r0, C), :], bo_ref.at[:, :, pl.ds(r0, C), :],
                    s_ref, rev=d == 1, emit=s >= nc, nb=nb)
        return carry

    lax.fori_loop(0, nsub, sub_chunk, 0)


def _rwkv_chunk(k_ref, v_ref, r_ref, lo_ref, w0_ref, a0_ref, w2_ref, a2_ref, kk_ref, ka_ref, rk_ref,
                o_ref, bo_ref, s_ref, *, rev, emit, nb):
    C = RWKV_CHUNK
    G = 2 * RWKV_HEAD_DIM

    ii = lax.broadcasted_iota(jnp.int32, (C, G), 0)
    lane = lax.broadcasted_iota(jnp.int32, (C, G), 1)
    jj = lane & (RWKV_HEAD_DIM - 1)
    head0 = lane < RWKV_HEAD_DIM
    dlt = jnp.where(rev, ii - jj, jj - ii)
    strict = dlt < 0
    incl = dlt <= 0
    eye = (ii == jj).astype(f32)
    gi = lax.broadcasted_iota(jnp.int32, (G, G), 0)
    gj = lax.broadcasted_iota(jnp.int32, (G, G), 1)
    blockdiag = (gi >= RWKV_HEAD_DIM) == (gj >= RWKV_HEAD_DIM)
    ones_bd = blockdiag.astype(bf16)
    tri = incl[:, 0:C].astype(bf16)
    base = (ii >> 1) == (jj >> 1)
    offs = [((ii >> (lv + 1)) == (jj >> (lv + 1))) & ((ii >> lv) != (jj >> lv)) for lv in range(1, 6)]

    def segsum(x):
        return jnp.concatenate(
            [_bdot(x[:, p * G:(p + 1) * G], ones_bd) for p in range(RWKV_PAIRS)], axis=1)

    def stack(x):
        xb = x.astype(bf16)
        zero = jnp.zeros_like(xb)
        return jnp.concatenate([jnp.where(head0, xb, zero), jnp.where(head0, zero, xb)], axis=0)

    pre = []
    for bb in range(nb):
        kx = k_ref[bb].astype(f32)
        vx = v_ref[bb].astype(f32)
        rx = r_ref[bb].astype(f32)
        lo = lo_ref[bb, :, 0:DECAY_LORA + ICLR_LORA]
        u = w0_ref[0] + _dot(lo, w2_ref[0])
        softplus = jnp.maximum(-u, 0.0) + jnp.log1p(jnp.exp(-jnp.abs(u)))
        lw = -jnp.exp(-softplus - 0.5)
        a = jax.nn.sigmoid(a0_ref[0] + _dot(lo, a2_ref[0]))
        kkr = kx * kk_ref[...]
        kk = kkr * lax.rsqrt(segsum(kkr * kkr) + 1e-12)
        kd = kx * (1.0 + (a - 1.0) * ka_ref[...])
        be = kk * a
        bonus = segsum(rx * kd * rk_ref[...]) * vx
        cum = _split_dot_left(tri, lw)
        tot = jnp.where(rev, cum[0:1, :], cum[C - 1:C, :])
        gneg = jnp.exp(-cum)
        gh = jnp.exp(tot - cum)
        pre.append(dict(alb=-kk * jnp.exp(cum - lw), rb=rx * jnp.exp(cum), beb=be * gneg, kb=kd * gneg,
                        beh=be * gh, kh=kd * gh, etot=jnp.exp(tot), v=vx, bonus=bonus))

    units = [(bb, p) for bb in range(nb) for p in range(RWKV_PAIRS)]
    sl = lambda p: slice(p * G, (p + 1) * G)
    part = lambda un, name: pre[un[0]][name][:, sl(un[1])]
    S = {un: s_ref[un[0] * RWKV_PAIRS + un[1]] for un in units}
    Sb = {un: S[un].astype(bf16) for un in units}
    X = {un: part(un, "alb").astype(bf16) for un in units}
    Rb = {un: part(un, "rb").astype(bf16) for un in units}
    Ybs = {un: stack(part(un, "beb")) for un in units}
    Yks = {un: stack(part(un, "kb")) for un in units}
    Vb = {un: part(un, "v").astype(bf16) for un in units}
    Vs = {un: stack(part(un, "v")) for un in units}

    XR = {un: jnp.concatenate([X[un], Rb[un]], axis=0) for un in units}
    APb = {un: _dot_nt(XR[un], Ybs[un]) for un in units}
    APk = {un: _dot_nt(XR[un], Yks[un]) for un in units}
    BO = {un: _dot_nt(XR[un], Sb[un]) for un in units}
    Aab = {un: jnp.where(strict, APb[un][0:C], 0.0) for un in units}
    Aak = {un: jnp.where(strict, APk[un][0:C], 0.0).astype(bf16) for un in units}
    Pab = {un: jnp.where(incl, APb[un][C:2 * C], 0.0).astype(bf16) for un in units}
    Pak = {un: jnp.where(incl, APk[un][C:2 * C], 0.0).astype(bf16) for un in units}

    Tm = {un: eye + jnp.where(base, Aab[un], 0.0) for un in units}
    for off in offs:
        Xs = {un: _dot(jnp.where(off, Aab[un], 0.0).astype(bf16), stack(Tm[un])) for un in units}
        Tm = {un: Tm[un] + _dot(Tm[un].astype(bf16), stack(Xs[un])) for un in units}

    AV = {un: _dot(jnp.concatenate([Aak[un], Pak[un]], axis=0), Vs[un]) for un in units}
    Bm = {un: BO[un][0:C] + AV[un][0:C] for un in units}
    U = {un: _dot(Tm[un].astype(bf16), stack(Bm[un])) for un in units}
    Om = {un: BO[un][C:2 * C] + AV[un][C:2 * C] + _dot(Pab[un], stack(U[un])) for un in units}
    for un in units:
        bb, p = un
        upd = _dot_tn(jnp.concatenate([U[un].astype(bf16), Vb[un]], axis=0),
                      jnp.concatenate([part(un, "beh").astype(bf16), part(un, "kh").astype(bf16)], axis=0))
        s_ref[bb * RWKV_PAIRS + p] = jnp.where(blockdiag, S[un] * part(un, "etot") + upd, 0.0)

    @pl.when(emit)
    def _():
        for bb in range(nb):
            o_ref[0, bb] = jnp.concatenate([Om[bb, p] for p in range(RWKV_PAIRS)], axis=1)
            bo_ref[0, bb] = pre[bb]["bonus"]


def _split_dot_left(w, x):
    hi = x.astype(bf16)
    lo = (x - hi.astype(f32)).astype(bf16)
    return _dot(w, hi) + _dot(w, lo)


def _rwkv(z, w0, a0, w2p, a2p, k_k, k_a, r_k, ctx):
    B, L, _ = z.shape
    T = L - ctx
    nsub = RWKV_CHUNKS_PER_STEP
    while ctx % (nsub * RWKV_CHUNK) or T % (nsub * RWKV_CHUNK):
        nsub //= 2
    C = nsub * RWKV_CHUNK
    nc, nl = ctx // C, T // C
    W = RWKV_WIDTH

    def chunk(d, s):
        fwd = jnp.where(s < nc, nl + s, s - nc)
        bwd = jnp.where(s < nc, nl + nc - 1 - s, nl - 1 - (s - nc))
        return jnp.where(d == 0, fwd, bwd)

    def ochunk(d, s):
        sl = jnp.maximum(s - nc, 0)
        return jnp.where(d == 0, sl, nl - 1 - sl)

    nb = RWKV_SAMPLES_PER_STEP if B % RWKV_SAMPLES_PER_STEP == 0 else 1
    zblk = lambda c0, w: pl.BlockSpec((nb, C, w), lambda b, d, s: (b, chunk(d, s), c0 // w))
    dpar = lambda r: pl.BlockSpec((1, r, W), lambda b, d, s: (d, 0, 0))
    par = pl.BlockSpec((1, W), lambda b, d, s: (0, 0))
    oblk = pl.BlockSpec((1, nb, C, W), lambda b, d, s: (d, b, ochunk(d, s), 0))
    G = 2 * RWKV_HEAD_DIM
    return pl.pallas_call(
        functools.partial(_rwkv_kernel, nc=nc, nb=nb, nsub=nsub),
        grid=(B // nb, 2, nc + nl),
        in_specs=[zblk(C_RWK, W), zblk(C_RWV, W), zblk(C_RWR, W), zblk(C_LORA, 256),
                  dpar(1), dpar(1), dpar(G), dpar(G), par, par, par],
        out_specs=[oblk, oblk],
        out_shape=[jax.ShapeDtypeStruct((2, B, T, W), f32), jax.ShapeDtypeStruct((2, B, T, W), f32)],
        scratch_shapes=[pltpu.VMEM((nb * RWKV_PAIRS, G, G), f32)],
        compiler_params=_params(("arbitrary", "arbitrary", "arbitrary")),
        name="rwkv",
    )(z, z, z, z, w0, a0, w2p, a2p, k_k, k_a, r_k)


def _merge_kernel(x_ref, ret_ref, rw0_ref, rw1_ref, b0_ref, b1_ref, gate_ref, retg_ref, gd_ref, mod_ref,
                  ng_ref, gn_ref, lng_ref, lnb_ref, g2_ref, wbr_ref, wbw_ref, wout_ref, wrh_ref, wrl_ref,
                  x1_ref, h2_ref, lt_ref):
    hd = RET_HEAD_DIM
    ret = ret_ref[0]
    parts = []
    for hh in range(RET_HEADS):
        xh = ret[:, hh * hd:(hh + 1) * hd]
        mu = jnp.mean(xh, axis=-1, keepdims=True)
        dv = xh - mu
        var = jnp.mean(dv * dv, axis=-1, keepdims=True)
        parts.append(dv * lax.rsqrt(var + RET_EPS))
    yr = retg_ref[0].astype(f32) * (jnp.concatenate(parts, axis=1) * gn_ref[...])
    y_ret = _bdot(yr, wbr_ref[...])
    W = RWKV_WIDTH
    gi = lax.broadcasted_iota(jnp.int32, (W, W), 0)
    gj = lax.broadcasted_iota(jnp.int32, (W, W), 1)
    ones_bd = ((gi >> 6) == (gj >> 6)).astype(bf16)
    o = rw0_ref[0, 0] + rw1_ref[0, 0]
    mu = _split_dot(o, ones_bd) * (1.0 / RWKV_HEAD_DIM)
    dv = o - mu
    var = _bdot(dv * dv, ones_bd) * (1.0 / RWKV_HEAD_DIM)
    yw = dv * lax.rsqrt(var + RWKV_EPS) * lng_ref[...] + lnb_ref[...]
    gate = _dot(gd_ref[0], g2_ref[...])
    yw = (yw + b0_ref[0, 0] + b1_ref[0, 0]) * gate
    y_rw = _bdot(yw, wbw_ref[...])
    D = y_ret.shape[1]
    g = gate_ref[0].astype(f32)
    m = g[:, :D] * y_ret + g[:, D:] * y_rw
    y = _bdot(m, wout_ref[...])

    def rms(v, gg):
        return v * lax.rsqrt(jnp.mean(v * v, axis=-1, keepdims=True) + NORM_EPS) * gg

    x1 = x_ref[0] + mod_ref[0, 0:1, :] * rms(y, ng_ref[0:1, :])
    x1_ref[0] = x1
    h2 = rms(x1, ng_ref[1:2, :]) * (1.0 + mod_ref[0, 2:3, :]) + mod_ref[0, 1:2, :]
    h2b = h2.astype(bf16)
    h2_ref[0] = h2b
    h2l = (h2 - h2b.astype(f32)).astype(bf16)
    lgt = _dot(h2b, wrh_ref[...]) + _dot(h2l, wrh_ref[...]) + _dot(h2b, wrl_ref[...])
    lt_ref[0] = lgt.T[0:N_EXPERTS, :]


def _merge(x, ret_o, rw_o, bonus, z, mod2, ng12, gn, lng, lnb, g2, wbr, wbw, wout, wrh, wrl, ctx):
    B, T, D = x.shape
    tm = MERGE_ROWS if T % MERGE_ROWS == 0 else ROWS
    co = 0
    W = RWKV_WIDTH
    row = lambda w: pl.BlockSpec((1, tm, w), lambda b, i: (b, i, 0))
    dblk = lambda dd: pl.BlockSpec((1, 1, tm, W), lambda b, i: (dd, b, i, 0))
    zblk = lambda c0, w: pl.BlockSpec((1, tm, w), lambda b, i: (b, co + i, c0 // w))
    full = lambda a: pl.BlockSpec(a.shape, lambda b, i: (0,) * a.ndim)
    return pl.pallas_call(
        _merge_kernel,
        grid=(B, T // tm),
        in_specs=[row(D), row(W), dblk(0), dblk(1), dblk(0), dblk(1),
                  zblk(C_MERGE, 2 * D), zblk(C_RETG, W), zblk(C_LORA + 128, 128),
                  pl.BlockSpec((1, 3, D), lambda b, i: (b, 0, 0)),
                  full(ng12), full(gn), full(lng), full(lnb), full(g2), full(wbr), full(wbw), full(wout),
                  full(wrh), full(wrl)],
        out_specs=[row(D), row(D), pl.BlockSpec((1, N_EXPERTS, tm), lambda b, i: (b, 0, i))],
        out_shape=[jax.ShapeDtypeStruct((B, T, D), f32), jax.ShapeDtypeStruct((B, T, D), bf16),
                   jax.ShapeDtypeStruct((B, N_EXPERTS, T), f32)],
        compiler_params=_params(("arbitrary", "arbitrary")),
        name="merge",
    )(x, ret_o, rw_o, rw_o, bonus, bonus, z, z, z, mod2, ng12, gn, lng, lnb, g2, wbr, wbw, wout, wrh, wrl)


def _route_kernel(lt_ref, slot_ref, rt_ref, cnt_ref, slotf_ref, gate_ref, *, cap):
    B, NE, T = lt_ref.shape
    lg = lt_ref[...]
    mx = jnp.max(lg, axis=1, keepdims=True)
    ex = jnp.exp(lg - mx)
    aff = (ex / jnp.sum(ex, axis=1, keepdims=True)).reshape(B * NE, T)
    E = B * NE

    def count_ge(cand):
        return jnp.sum((aff >= cand).astype(f32), axis=1, keepdims=True)

    def exp_step(_, kk):
        k_lo, k_hi = kk
        km = jnp.floor((k_lo + k_hi) * 0.5)
        ok = count_ge(jnp.exp2(-km)) >= cap
        return jnp.where(ok, k_lo, km), jnp.where(ok, km, k_hi)

    k_lo, k_hi = lax.fori_loop(0, ROUTE_EXP_STEPS, exp_step,
                               (jnp.full((E, 1), -1.0, f32), jnp.full((E, 1), ROUTE_MAX_EXP, f32)))
    lo0 = jnp.where(k_hi >= ROUTE_MAX_EXP, 0.0, jnp.exp2(-k_hi))
    hi0 = jnp.exp2(-k_lo)

    def val_step(_, lh):
        lo, hi = lh
        mid = (lo + hi) * 0.5
        ok = count_ge(mid) >= cap
        return jnp.where(ok, mid, lo), jnp.where(ok, hi, mid)

    lo, hi = lax.fori_loop(0, ROUTE_VAL_STEPS, val_step, (lo0, hi0))
    gt = aff >= hi
    eq = (aff >= lo) & (aff < hi)
    need = cap - jnp.sum(gt.astype(f32), axis=1, keepdims=True)
    tri = (lax.broadcasted_iota(jnp.int32, (T, T), 0) < lax.broadcasted_iota(jnp.int32, (T, T), 1)).astype(bf16)
    eq_before = _dot(eq.astype(bf16), tri)
    sel = gt | (eq & (eq_before < need))
    slot = _dot(sel.astype(bf16), tri)
    slot_f = jnp.where(sel, slot, -1.0)
    slot_ref[...] = slot_f.astype(jnp.int32).reshape(B, NE, T)
    before = (lax.broadcasted_iota(jnp.int32, (T, 128), 0)
              < lax.broadcasted_iota(jnp.int32, (T, 128), 1) * GATHER_TILE).astype(bf16)
    cnt_ref[...] = _dot(sel.astype(bf16), before).astype(jnp.int32).reshape(B, NE, 128)
    slotf_ref[...] = slot_f.reshape(B, NE, T)
    gate_ref[...] = jnp.where(sel, aff, 0.0).reshape(B, NE, T)

    def transpose_sample(b, carry):
        packed = jnp.concatenate([slotf_ref[b], gate_ref[b], jnp.zeros((128 - 2 * NE, T), f32)], axis=0)
        rt_ref[b] = packed.T.astype(bf16)
        return carry

    lax.fori_loop(0, B, transpose_sample, 0)


def _route(lt, cap):
    B, E, T = lt.shape
    assert cap <= 256 and T % GATHER_TILE == 0 and T // GATHER_TILE < 128
    return pl.pallas_call(
        functools.partial(_route_kernel, cap=cap),
        grid=(1,),
        in_specs=[pl.BlockSpec((B, E, T), lambda i: (0, 0, 0))],
        out_specs=[pl.BlockSpec((B, E, T), lambda i: (0, 0, 0)), pl.BlockSpec((B, T, 128), lambda i: (0, 0, 0)),
                   pl.BlockSpec((B, E, 128), lambda i: (0, 0, 0))],
        out_shape=[jax.ShapeDtypeStruct((B, E, T), jnp.int32), jax.ShapeDtypeStruct((B, T, 128), bf16),
                   jax.ShapeDtypeStruct((B, E, 128), jnp.int32)],
        scratch_shapes=[pltpu.VMEM((B, E, T), f32), pltpu.VMEM((B, E, T), f32)],
        compiler_params=_params(("arbitrary",)),
        name="route",
    )(lt)


def _ffn_kernel(cnt_ref, slot_ref, h_ref, wg_ref, wu_ref, wd_ref, o_ref, wgb_ref, wub_ref, wdb_ref, xg_ref, *, cap):
    T = h_ref.shape[1]
    e = pl.program_id(0)

    @pl.when(pl.program_id(1) == 0)
    def _():
        wgb_ref[...] = wg_ref[0].astype(bf16)
        wub_ref[...] = wu_ref[0].astype(bf16)
        wdb_ref[...] = wd_ref[0].astype(bf16)

    nb = h_ref.shape[0]
    GT, GW, GA = GATHER_TILE, GATHER_WINDOW, GATHER_ALIGN
    nt = T // GT
    for bb in range(nb):
        b = pl.program_id(1) * nb + bb
        base = [pl.multiple_of(lax.shift_left(lax.shift_right_logical(cnt_ref[b, e, j], GATHER_ALIGN_LOG2),
                                              GATHER_ALIGN_LOG2), GA) for j in range(nt)]
        fits = cnt_ref[b, e, 1] - base[0] <= GW
        for j in range(1, nt):
            fits = fits & (cnt_ref[b, e, j + 1] - base[j] <= GW)

        @pl.when(fits)
        def _():
            xg_ref[bb] = jnp.zeros(xg_ref.shape[1:], f32)
            for j in range(nt):
                rows = base[j] + lax.broadcasted_iota(jnp.int32, (GW, GT), 0)
                onehot = (slot_ref[bb, 0, j:j + 1, :] == rows).astype(bf16)
                xg_ref[bb, pl.ds(base[j], GW), :] += _dot(onehot, h_ref[bb, j * GT:(j + 1) * GT, :])

        @pl.when(jnp.logical_not(fits))
        def _():
            rows = lax.broadcasted_iota(jnp.int32, (cap, GT), 0)
            acc = jnp.zeros((cap, h_ref.shape[2]), f32)
            for j in range(nt):
                onehot = (slot_ref[bb, 0, j:j + 1, :] == rows).astype(bf16)
                acc = acc + _dot(onehot, h_ref[bb, j * GT:(j + 1) * GT, :])
            xg_ref[bb, 0:cap, :] = acc

    xg = jnp.concatenate([xg_ref[bb, 0:cap, :].astype(bf16) for bb in range(nb)], axis=0)
    hg = _dot(xg, wgb_ref[...])
    hu = _dot(xg, wub_ref[...])
    hid = (hg * _sigmoid(hg) * hu).astype(bf16)
    out = _dot(hid, wdb_ref[...]).astype(bf16)
    for bb in range(nb):
        o_ref[bb, 0] = out[bb * cap:(bb + 1) * cap]


def _ffn(cnt, slot4, h2, wg, wu, wd, cap):
    B, T, D = h2.shape
    E, _, F = wg.shape
    nb = FFN_SAMPLES_PER_STEP if B % FFN_SAMPLES_PER_STEP == 0 else 1
    grid_spec = pltpu.PrefetchScalarGridSpec(
        num_scalar_prefetch=1,
        grid=(E, B // nb),
        in_specs=[pl.BlockSpec((nb, 1, T // GATHER_TILE, GATHER_TILE), lambda e, b, c: (b, e, 0, 0)),
                  pl.BlockSpec((nb, T, D), lambda e, b, c: (b, 0, 0)),
                  pl.BlockSpec((1, D, F), lambda e, b, c: (e, 0, 0)),
                  pl.BlockSpec((1, D, F), lambda e, b, c: (e, 0, 0)),
                  pl.BlockSpec((1, F, D), lambda e, b, c: (e, 0, 0))],
        out_specs=pl.BlockSpec((nb, 1, cap, D), lambda e, b, c: (b, e, 0, 0)),
        scratch_shapes=[pltpu.VMEM((D, F), bf16), pltpu.VMEM((D, F), bf16), pltpu.VMEM((F, D), bf16),
                        pltpu.VMEM((nb, cap + GATHER_WINDOW, D), f32)])
    return pl.pallas_call(
        functools.partial(_ffn_kernel, cap=cap),
        grid_spec=grid_spec,
        out_shape=jax.ShapeDtypeStruct((B, E, cap, D), bf16),
        compiler_params=_params(("arbitrary", "arbitrary"), FFN_VMEM_LIMIT),
        name="ffn",
    )(cnt, slot4, h2, wg, wu, wd)


def _combine_kernel(cnt_ref, rt_ref, eo_ref, x1_ref, mod_ref, ng_ref, o_ref, y_ref, *, cap):
    b = pl.program_id(0)
    i = pl.program_id(1)
    rt = rt_ref[0].astype(f32)
    tm = rt.shape[0]
    E = eo_ref.shape[1]
    W = min(SCATTER_WINDOW, cap)
    per = tm // GATHER_TILE
    base, fits = [], None
    for e in range(E):
        lo = cnt_ref[b, e, i * per]
        be = jnp.minimum(lax.shift_left(lax.shift_right_logical(lo, GATHER_ALIGN_LOG2), GATHER_ALIGN_LOG2), cap - W)
        base.append(pl.multiple_of(be, GATHER_ALIGN))
        ok = cnt_ref[b, e, (i + 1) * per] - be <= W
        fits = ok if fits is None else fits & ok

    def scatter_matrix(e, first, width):
        cols = (first + lax.broadcasted_iota(jnp.int32, (tm, width), 1)).astype(f32)
        return jnp.where(rt[:, e:e + 1] == cols, rt[:, E + e:E + e + 1], 0.0).astype(bf16)

    @pl.when(fits)
    def _():
        y = jnp.zeros(y_ref.shape, f32)
        for e in range(0, E, 2):
            p = jnp.concatenate([scatter_matrix(e, base[e], W), scatter_matrix(e + 1, base[e + 1], W)], axis=1)
            rows = jnp.concatenate([eo_ref[0, e, pl.ds(base[e], W), :], eo_ref[0, e + 1, pl.ds(base[e + 1], W), :]],
                                   axis=0)
            y = y + _dot(p, rows)
        y_ref[...] = y

    @pl.when(jnp.logical_not(fits))
    def _():
        y = jnp.zeros(y_ref.shape, f32)
        for e in range(E):
            y = y + _dot(scatter_matrix(e, 0, cap), eo_ref[0, e])
        y_ref[...] = y

    y = y_ref[...]
    yn = y * lax.rsqrt(jnp.mean(y * y, axis=-1, keepdims=True) + NORM_EPS) * ng_ref[...]
    o_ref[0] = x1_ref[0] + mod_ref[0] * yn


def _combine(cnt, rt, eo, x1, g2mod, ng3, cap):
    B, T, D = x1.shape
    E = eo.shape[1]
    tm = 512 if T % 512 == 0 else T
    assert E % 2 == 0 and tm % GATHER_TILE == 0
    grid_spec = pltpu.PrefetchScalarGridSpec(
        num_scalar_prefetch=1,
        grid=(B, T // tm),
        in_specs=[pl.BlockSpec((1, tm, 128), lambda b, i, c: (b, i, 0)),
                  pl.BlockSpec((1, E, cap, D), lambda b, i, c: (b, 0, 0, 0)),
                  pl.BlockSpec((1, tm, D), lambda b, i, c: (b, i, 0)),
                  pl.BlockSpec((1, 1, D), lambda b, i, c: (b, 0, 0)),
                  pl.BlockSpec((1, D), lambda b, i, c: (0, 0))],
        out_specs=pl.BlockSpec((1, tm, D), lambda b, i, c: (b, i, 0)),
        scratch_shapes=[pltpu.VMEM((tm, D), f32)])
    return pl.pallas_call(
        functools.partial(_combine_kernel, cap=cap),
        grid_spec=grid_spec,
        out_shape=jax.ShapeDtypeStruct((B, T, D), f32),
        compiler_params=_params(("arbitrary", "arbitrary")),
        name="combine",
    )(cnt, rt, eo, x1, g2mod, ng3)


def _permute_columns(w):
    sk, sv, rk, rv, wd, ad = 0, 512, 1024, 1536, 2048, 2112
    q0 = 2176
    rq, rg, rr, gd, mg = q0, q0 + 512, q0 + 1024, q0 + 1536, q0 + 1664
    order = [(mg, 2048), (sk, 512), (sv, 512), (rq, 512), (rg, 512), (rk, 512), (rv, 512), (rr, 512),
             (wd, 64), (ad, 64), (gd, 128)]
    parts = [w[:, a:a + n] for a, n in order]
    parts.append(jnp.zeros((w.shape[0], IN_COLS - USED_COLS), w.dtype))
    return jnp.concatenate(parts, axis=1)


def _rope_tables(T, CT):
    t = jnp.arange(T)
    nfreq = RET_HEAD_DIM // 4
    inv = ROPE_BASE ** (-jnp.arange(nfreq, dtype=f32) / nfreq)
    ang = jnp.concatenate([(t // GRID_W).astype(f32)[:, None] * inv,
                           (t % GRID_W).astype(f32)[:, None] * inv], axis=-1)
    cos, sin = jnp.cos(ang), jnp.sin(ang)
    cosf = jnp.concatenate([cos, cos], axis=1)
    sinf = jnp.concatenate([-sin, sin], axis=1)
    return (jnp.concatenate([cosf, jnp.ones((CT, RET_HEAD_DIM), f32)], axis=0),
            jnp.concatenate([sinf, jnp.zeros((CT, RET_HEAD_DIM), f32)], axis=0))


def kernel(x, c, ctx, c_ctx, w_mod, b_mod, norm_g, w_in, ret_log_decay, ret_gn_g, rwkv_mu, rwkv_k_k, rwkv_k_a,
           rwkv_r_k, rwkv_w0, rwkv_w2, rwkv_a0, rwkv_a2, rwkv_g2, rwkv_ln_g, rwkv_ln_b, w_br_ret, w_br_rwkv,
           w_out, w_router, w_gate, w_up, w_down):
    B, T, D = x.shape
    CT = ctx.shape[1]
    assert w_mod.shape[0] == 1 and D == D_MODEL
    assert CT % ROWS == 0 and T % ROWS == 0 and T % GRID_W == 0
    cap = CAPACITY_FACTOR * T // N_EXPERTS
    assert cap % 8 == 0

    mrows = -(-(B + 1) // 8) * 8
    cc = jnp.zeros((mrows, D), f32).at[:B].set(c).at[B].set(c_ctx)
    mod = _modulation(cc, w_mod[0], b_mod[0])
    lat = mod[:B].reshape(B, N_MOD, D)
    cm = jnp.broadcast_to(mod[B].reshape(1, N_MOD, D), (B, N_MOD, D))
    modrows = jnp.concatenate([lat[:, 0:2], cm[:, 0:2]], axis=1)

    w_perm = _permute_columns(w_in[0]).astype(bf16)
    mu = rwkv_mu[0]
    ss = 2 * RWKV_WIDTH + DECAY_LORA + ICLR_LORA
    mu_full = jnp.zeros((2, IN_COLS), f32)
    mu_full = mu_full.at[:, C_RWK:C_RWK + 1024].set(mu[:, 0:1024])
    mu_full = mu_full.at[:, C_RWR:C_RWR + 512].set(mu[:, ss:ss + 512])
    mu_full = mu_full.at[:, C_LORA:C_LORA + 128].set(mu[:, 1024:ss])
    mu_full = mu_full.at[:, C_LORA + 128:USED_COLS].set(mu[:, ss + 512:])
    cosf, sinf = _rope_tables(T, CT)
    z = _inproj(x, ctx, modrows, norm_g[0, 0:1], w_perm, mu_full, cosf, sinf)

    lg = -jnp.exp(ret_log_decay[0].astype(f32))
    ret_o = _retention(lg, z, CT)

    G = 2 * RWKV_HEAD_DIM
    w2p = jnp.zeros((2, G, RWKV_WIDTH), f32).at[:, :DECAY_LORA].set(rwkv_w2[0]).astype(bf16)
    a2p = jnp.zeros((2, G, RWKV_WIDTH), f32).at[:, DECAY_LORA:].set(rwkv_a2[0]).astype(bf16)
    rw_o, bonus = _rwkv(z, rwkv_w0[0][:, None, :], rwkv_a0[0][:, None, :], w2p, a2p,
                        rwkv_k_k[0][None], rwkv_k_a[0][None], rwkv_r_k[0][None], CT)

    mod2 = jnp.stack([lat[:, 2], lat[:, 3], lat[:, 4]], axis=1)
    wr_pad = jnp.zeros((D, 128), f32).at[:, :N_EXPERTS].set(w_router[0])
    wr_hi = wr_pad.astype(bf16)
    wr_lo = (wr_pad - wr_hi.astype(f32)).astype(bf16)
    x1, h2, lt = _merge(x, ret_o, rw_o, bonus, z, mod2, norm_g[0, 1:3], ret_gn_g[0][None], rwkv_ln_g[0][None],
                        rwkv_ln_b[0][None], rwkv_g2[0].astype(bf16), w_br_ret[0].astype(bf16),
                        w_br_rwkv[0].astype(bf16), w_out[0].astype(bf16), wr_hi, wr_lo, CT)

    slot, rt, cnt = _route(lt, cap)
    cnt = cnt[:, :, :T // GATHER_TILE + 1]
    eo = _ffn(cnt, slot.reshape(B, N_EXPERTS, T // GATHER_TILE, GATHER_TILE), h2, w_gate[0], w_up[0], w_down[0], cap)
    return _combine(cnt, rt, eo, x1, lat[:, 5:6], norm_g[0, 3:4], cap)
```

```python
import functools

import numpy as np
import jax
import jax.numpy as jnp
from jax import lax
from jax.experimental import pallas as pl
from jax.experimental.pallas import tpu as pltpu

f32 = jnp.float32
bf16 = jnp.bfloat16

D_MODEL = 1024
GRID_W = 64
RET_HEAD_DIM = 128
RET_WIDTH = 512
RET_HEADS = 4
RET_CHUNK = 128
RET_EPS = 1e-5
ROPE_BASE = 10000.0
RWKV_HEAD_DIM = 64
RWKV_WIDTH = 512
RWKV_PAIRS = 4
RWKV_CHUNK = 64
RWKV_CHUNKS_PER_STEP = 4
RWKV_SAMPLES_PER_STEP = 8
DECAY_LORA = 64
ICLR_LORA = 64
GATE_LORA = 128
RWKV_EPS = 64e-5
N_EXPERTS = 16
EXPERT_FF = 1024
CAPACITY_FACTOR = 2
N_MOD = 6
NORM_EPS = 1e-6

C_MERGE, C_RETK, C_RETV, C_RETQ, C_RETG = 0, 2048, 2560, 3072, 3584
C_RWK, C_RWV, C_RWR, C_LORA = 4096, 4608, 5120, 5632
USED_COLS = 5888
TN = 512
IN_COLS = -(-USED_COLS // TN) * TN
ROWS = 256
STEP_ROWS = 768
POST_ROWS = 128
MERGE_ROWS = 512

ROUTE_MAX_EXP = 126.0
ROUTE_EXP_STEPS = 8
ROUTE_VAL_STEPS = 26

VMEM_LIMIT = 56 * 1024 * 1024
FFN_SAMPLES_PER_STEP = 2
GATHER_TILE = 256
GATHER_ALIGN_LOG2 = 4
GATHER_ALIGN = 1 << GATHER_ALIGN_LOG2
GATHER_WINDOW = 80
SCATTER_WINDOW = 64
SCATTER_GROUP = 4
FFN_VMEM_LIMIT = 62 * 1024 * 1024


def _dot(a, b):
    return jnp.dot(a, b, preferred_element_type=f32)


def _dot_nt(a, b):
    return lax.dot_general(a, b, (((1,), (1,)), ((), ())), preferred_element_type=f32)


def _dot_tn(a, b):
    return lax.dot_general(a, b, (((0,), (0,)), ((), ())), preferred_element_type=f32)


def _bdot(a, b):
    return _dot(a.astype(bf16), b.astype(bf16))


def _split_dot(x, w):
    hi = x.astype(bf16)
    lo = (x - hi.astype(f32)).astype(bf16)
    return _dot(hi, w) + _dot(lo, w)


def _sigmoid(x):
    return 0.5 * jnp.tanh(0.5 * x) + 0.5


def _params(sem, limit=VMEM_LIMIT):
    return pltpu.CompilerParams(dimension_semantics=sem, vmem_limit_bytes=limit)


def _mod_kernel(c_ref, w_ref, b_ref, o_ref):
    c = c_ref[...]
    s = c * jax.nn.sigmoid(c)
    o_ref[...] = _bdot(s, w_ref[...]) + b_ref[...]


def _modulation(cc, w_mod, b_mod):
    m, d = cc.shape
    n = w_mod.shape[1]
    tn = 512
    return pl.pallas_call(
        _mod_kernel,
        grid=(n // tn,),
        in_specs=[pl.BlockSpec((m, d), lambda j: (0, 0)),
                  pl.BlockSpec((d, tn), lambda j: (0, j)),
                  pl.BlockSpec((1, tn), lambda j: (0, j))],
        out_specs=pl.BlockSpec((m, tn), lambda j: (0, j)),
        out_shape=jax.ShapeDtypeStruct((m, n), f32),
        compiler_params=_params(("arbitrary",)),
        name="mod",
    )(cc, w_mod, b_mod.reshape(1, n))


def _inproj_kernel(x_ref, c_ref, mod_ref, g_ref, w_ref, mu_ref, cos_ref, sin_ref, o_ref, h_ref, z_ref):
    n = pl.program_id(1)
    T = x_ref.shape[1]
    L = T + c_ref.shape[1]
    nlat = T // ROWS
    nchunk = L // ROWS
    PAD = 8

    SR = STEP_ROWS if L % STEP_ROWS == 0 else ROWS

    @pl.when(n == 0)
    def _():
        z_ref[0:PAD, :] = jnp.zeros((PAD, TN), f32)
        z_ref[PAD + L:PAD + L + PAD, :] = jnp.zeros((PAD, TN), f32)

        def norm_chunk(src_ref, s0, r0, o):
            xb = src_ref[0, pl.ds(s0, ROWS), :]
            y = xb * lax.rsqrt(jnp.mean(xb * xb, axis=-1, keepdims=True) + NORM_EPS) * g_ref[...]
            sh = mod_ref[0, o:o + 1, :]
            sc = mod_ref[0, o + 1:o + 2, :]
            h_ref[pl.ds(r0, ROWS), :] = (y * (1.0 + sc) + sh).astype(bf16)

        def lat_body(i, carry):
            r0 = pl.multiple_of(i * ROWS, ROWS)
            norm_chunk(x_ref, r0, r0, 0)
            return carry

        lax.fori_loop(0, nlat, lat_body, 0)
        for j in range(nlat, nchunk):
            norm_chunk(c_ref, j * ROWS - T, j * ROWS, 2)

    PR = POST_ROWS

    def rope(scale):
        def post(z, r0):
            if scale != 1.0:
                z = z * scale
            cs = cos_ref[pl.ds(r0, SR), :]
            sn = sin_ref[pl.ds(r0, SR), :]
            parts = []
            for hh in range(TN // RET_HEAD_DIM):
                zh = z[:, hh * RET_HEAD_DIM:(hh + 1) * RET_HEAD_DIM]
                parts.append(zh * cs + pltpu.roll(zh, RET_HEAD_DIM // 2, 1) * sn)
            return jnp.concatenate(parts, axis=1)
        return post

    def shifted(r0):
        win = z_ref[pl.ds(r0, PR + 2 * PAD), :]
        prev = win[PAD - 1:PAD - 1 + PR]
        z = win[PAD:PAD + PR]
        nxt = win[PAD + 1:PAD + 1 + PR]
        rid = r0 + lax.broadcasted_iota(jnp.int32, (PR, 1), 0)
        prev = jnp.where(rid == T, 0.0, prev)
        nxt = jnp.where(rid == T - 1, 0.0, nxt)
        return z + mu_ref[0:1, :] * (prev - z) + mu_ref[1:2, :] * (nxt - z)

    def lora_act(zs):
        lane = lax.broadcasted_iota(jnp.int32, (1, TN), 1)
        return jnp.where(lane < DECAY_LORA, jnp.tanh(zs),
                         jnp.where(lane < DECAY_LORA + ICLR_LORA, zs, jax.nn.sigmoid(zs)))

    def product(r0):
        return _dot(h_ref[pl.ds(r0, SR), :], w_ref[...])

    def direct(post):
        def body(i, carry):
            r0 = pl.multiple_of(i * SR, SR)
            o_ref[0, pl.ds(r0, SR), :] = post(product(r0), r0).astype(bf16)
            return carry
        lax.fori_loop(0, L // SR, body, 0)

    def via_buffer(act):
        def mm_body(i, carry):
            r0 = pl.multiple_of(i * SR, SR)
            z_ref[pl.ds(PAD + r0, SR), :] = product(r0)
            return carry
        lax.fori_loop(0, L // SR, mm_body, 0)

        def post_body(i, carry):
            r0 = pl.multiple_of(i * PR, PR)
            o_ref[0, pl.ds(r0, PR), :] = act(shifted(r0)).astype(bf16)
            return carry
        lax.fori_loop(0, L // PR, post_body, 0)

    t = lambda c: c // TN
    pl.when(n < t(C_RETK))(lambda: direct(lambda z, r0: _sigmoid(z)))
    pl.when((n >= t(C_RETK)) & (n < t(C_RETV)))(lambda: direct(rope(RET_HEAD_DIM ** -0.5)))
    pl.when((n >= t(C_RETV)) & (n < t(C_RETQ)))(lambda: direct(lambda z, r0: z))
    pl.when((n >= t(C_RETQ)) & (n < t(C_RETG)))(lambda: direct(rope(1.0)))
    pl.when((n >= t(C_RETG)) & (n < t(C_RWK)))(lambda: direct(lambda z, r0: z * _sigmoid(z)))
    pl.when((n >= t(C_RWK)) & (n < t(C_LORA)))(lambda: via_buffer(lambda zs: zs))
    pl.when(n >= t(C_LORA))(lambda: via_buffer(lora_act))


def _inproj(x, ctx, modrows, g0, w_perm, mu_full, cosf, sinf):
    B, T, D = x.shape
    CT = ctx.shape[1]
    L = T + CT
    nt = IN_COLS // TN
    return pl.pallas_call(
        _inproj_kernel,
        grid=(B, nt),
        in_specs=[pl.BlockSpec((1, T, D), lambda b, n: (b, 0, 0)),
                  pl.BlockSpec((1, CT, D), lambda b, n: (b, 0, 0)),
                  pl.BlockSpec((1, 4, D), lambda b, n: (b, 0, 0)),
                  pl.BlockSpec((1, D), lambda b, n: (0, 0)),
                  pl.BlockSpec((D, TN), lambda b, n: (0, n)),
                  pl.BlockSpec((2, TN), lambda b, n: (0, n)),
                  pl.BlockSpec((L, RET_HEAD_DIM), lambda b, n: (0, 0)),
                  pl.BlockSpec((L, RET_HEAD_DIM), lambda b, n: (0, 0))],
        out_specs=pl.BlockSpec((1, L, TN), lambda b, n: (b, 0, n)),
        out_shape=jax.ShapeDtypeStruct((B, L, IN_COLS), bf16),
        scratch_shapes=[pltpu.VMEM((L, D), bf16), pltpu.VMEM((L + 16, TN), f32)],
        compiler_params=_params(("arbitrary", "arbitrary")),
        name="inproj",
    )(x, ctx, modrows, g0, w_perm, mu_full, cosf, sinf)


def _ret_kernel(lg_ref, q_ref, k_ref, v_ref, o_ref, r_ref, tab_ref, *, ctx):
    L = q_ref.shape[1]
    Cc = RET_CHUNK
    hd = RET_HEAD_DIM
    lat = L - ctx
    nc = ctx // Cc
    nl = lat // Cc
    combos = [(h, d) for h in range(RET_HEADS) for d in (0, 1)]
    INTRA, CROSS, TAIL, DECAY = 0, 1, 2, 3

    @pl.when(pl.program_id(0) == 0)
    def _():
        ii = lax.broadcasted_iota(jnp.int32, (Cc, Cc), 0).astype(f32)
        jj = lax.broadcasted_iota(jnp.int32, (Cc, Cc), 1).astype(f32)
        for idx, (h, d) in enumerate(combos):
            lg = lg_ref[d, h]
            if d == 0:
                diff = ii - jj
                cross = jnp.exp(lg * (ii + 1.0))
                tailw = jnp.exp(lg * (Cc - 1.0 - ii))
            else:
                diff = jj - ii
                cross = jnp.exp(lg * (Cc - ii))
                tailw = jnp.exp(lg * ii)
            tab_ref[idx, INTRA] = jnp.where(diff >= 0, jnp.exp(lg * jnp.maximum(diff, 0.0)), 0.0)
            tab_ref[idx, CROSS] = cross
            tab_ref[idx, TAIL] = tailw
            tab_ref[idx, DECAY] = jnp.exp(jnp.zeros((Cc, Cc), f32) + lg * Cc)

    r_ref[...] = jnp.zeros(r_ref.shape, f32)
    o_ref[...] = jnp.zeros(o_ref.shape, f32)

    def cols(h):
        return slice(h * hd, (h + 1) * hd)

    def update(idx, kc, vc):
        ks = (kc.astype(f32) * tab_ref[idx, TAIL]).astype(bf16)
        r_ref[idx] = r_ref[idx] * tab_ref[idx, DECAY] + _dot_tn(ks, vc)

    def ctx_step(s, carry):
        for idx, (h, d) in enumerate(combos):
            row0 = pl.multiple_of(lat + (s * Cc if d == 0 else (nc - 1 - s) * Cc), Cc)
            update(idx, k_ref[0, pl.ds(row0, Cc), cols(h)], v_ref[0, pl.ds(row0, Cc), cols(h)])
        return carry

    def lat_step(s, carry):
        t0s, qs, ks, vs = [], [], [], []
        for h, d in combos:
            t0 = pl.multiple_of(s * Cc if d == 0 else (nl - 1 - s) * Cc, Cc)
            row0 = t0
            t0s.append(t0)
            qs.append(q_ref[0, pl.ds(row0, Cc), cols(h)])
            ks.append(k_ref[0, pl.ds(row0, Cc), cols(h)])
            vs.append(v_ref[0, pl.ds(row0, Cc), cols(h)])
        n = len(combos)
        sc = [(_dot_nt(qs[i], ks[i]) * tab_ref[i, INTRA]).astype(bf16) for i in range(n)]
        oc = [_dot(qs[i], r_ref[i].astype(bf16)) * tab_ref[i, CROSS] for i in range(n)]
        oi = [_dot(sc[i], vs[i]) for i in range(n)]
        for i, (h, d) in enumerate(combos):
            o_ref[0, pl.ds(t0s[i], Cc), cols(h)] += oi[i] + oc[i]
        for i in range(n):
            update(i, ks[i], vs[i])
        return carry

    lax.fori_loop(0, nc, ctx_step, 0)
    lax.fori_loop(0, nl, lat_step, 0)


def _retention(lg, z, ctx):
    B, L, _ = z.shape
    T = L - ctx
    W = RET_WIDTH
    Cc = RET_CHUNK
    blk = lambda c0: pl.BlockSpec((1, L, W), lambda b: (b, 0, c0 // W))
    nchain = 2 * RET_HEADS
    return pl.pallas_call(
        functools.partial(_ret_kernel, ctx=ctx),
        grid=(B,),
        in_specs=[pl.BlockSpec(memory_space=pltpu.SMEM), blk(C_RETQ), blk(C_RETK), blk(C_RETV)],
        out_specs=pl.BlockSpec((1, T, W), lambda b: (b, 0, 0)),
        out_shape=jax.ShapeDtypeStruct((B, T, W), f32),
        scratch_shapes=[pltpu.VMEM((nchain, Cc, Cc), f32), pltpu.VMEM((nchain, 4, Cc, Cc), f32)],
        compiler_params=_params(("arbitrary",)),
        name="ret",
    )(lg, z, z, z)


def _rwkv_kernel(k_ref, v_ref, r_ref, lo_ref, w0_ref, a0_ref, w2_ref, a2_ref, kk_ref, ka_ref, rk_ref,
                 o_ref, bo_ref, s_ref, *, nc, nb, nsub):
    d = pl.program_id(1)
    s = pl.program_id(2)
    C = RWKV_CHUNK

    @pl.when(s == 0)
    def _():
        s_ref[...] = jnp.zeros(s_ref.shape, f32)

    def sub_chunk(j, carry):
        r0 = pl.multiple_of(jnp.where(d == 1, nsub - 1 - j, j) * C, C)
        rows = lambda ref: ref.at[:, pl.ds(r0, C), :]
        _rwkv_chunk(rows(k_ref), rows(v_ref), rows(r_ref), rows(lo_ref), w0_ref, a0_ref, w2_ref, a2_ref,
                    kk_ref, ka_ref, rk_ref, o_ref.at[:, :, pl.ds(r0, C), :], bo_ref.at[:, :, pl.ds(r0, C), :],
                    s_ref, rev=d == 1, emit=s >= nc, nb=nb)
        return carry

    lax.fori_loop(0, nsub, sub_chunk, 0)


def _rwkv_chunk(k_ref, v_ref, r_ref, lo_ref, w0_ref, a0_ref, w2_ref, a2_ref, kk_ref, ka_ref, rk_ref,
                o_ref, bo_ref, s_ref, *, rev, emit, nb):
    C = RWKV_CHUNK
    G = 2 * RWKV_HEAD_DIM

    ii = lax.broadcasted_iota(jnp.int32, (C, G), 0)
    lane = lax.broadcasted_iota(jnp.int32, (C, G), 1)
    jj = lane & (RWKV_HEAD_DIM - 1)
    head0 = lane < RWKV_HEAD_DIM
    dlt = jnp.where(rev, ii - jj, jj - ii)
    strict = dlt < 0
    incl = dlt <= 0
    eye = (ii == jj).astype(f32)
    gi = lax.broadcasted_iota(jnp.int32, (G, G), 0)
    gj = lax.broadcasted_iota(jnp.int32, (G, G), 1)
    blockdiag = (gi >= RWKV_HEAD_DIM) == (gj >= RWKV_HEAD_DIM)
    ones_bd = blockdiag.astype(bf16)
    tri = incl[:, 0:C].astype(bf16)
    base = (ii >> 1) == (jj >> 1)
    offs = [((ii >> (lv + 1)) == (jj >> (lv + 1))) & ((ii >> lv) != (jj >> lv)) for lv in range(1, 6)]

    def segsum(x):
        return jnp.concatenate(
            [_bdot(x[:, p * G:(p + 1) * G], ones_bd) for p in range(RWKV_PAIRS)], axis=1)

    def stack(x):
        xb = x.astype(bf16)
        zero = jnp.zeros_like(xb)
        return jnp.concatenate([jnp.where(head0, xb, zero), jnp.where(head0, zero, xb)], axis=0)

    pre = []
    for bb in range(nb):
        kx = k_ref[bb].astype(f32)
        vx = v_ref[bb].astype(f32)
        rx = r_ref[bb].astype(f32)
        lo = lo_ref[bb, :, 0:DECAY_LORA + ICLR_LORA]
        u = w0_ref[0] + _dot(lo, w2_ref[0])
        softplus = jnp.maximum(-u, 0.0) + jnp.log1p(jnp.exp(-jnp.abs(u)))
        lw = -jnp.exp(-softplus - 0.5)
        a = jax.nn.sigmoid(a0_ref[0] + _dot(lo, a2_ref[0]))
        kkr = kx * kk_ref[...]
        kk = kkr * lax.rsqrt(segsum(kkr * kkr) + 1e-12)
        kd = kx * (1.0 + (a - 1.0) * ka_ref[...])
        be = kk * a
        bonus = segsum(rx * kd * rk_ref[...]) * vx
        cum = _split_dot_left(tri, lw)
        tot = jnp.where(rev, cum[0:1, :], cum[C - 1:C, :])
        gneg = jnp.exp(-cum)
        gh = jnp.exp(tot - cum)
        pre.append(dict(alb=-kk * jnp.exp(cum - lw), rb=rx * jnp.exp(cum), beb=be * gneg, kb=kd * gneg,
                        beh=be * gh, kh=kd * gh, etot=jnp.exp(tot), v=vx, bonus=bonus))

    units = [(bb, p) for bb in range(nb) for p in range(RWKV_PAIRS)]
    sl = lambda p: slice(p * G, (p + 1) * G)
    part = lambda un, name: pre[un[0]][name][:, sl(un[1])]
    S = {un: s_ref[un[0] * RWKV_PAIRS + un[1]] for un in units}
    Sb = {un: S[un].astype(bf16) for un in units}
    X = {un: part(un, "alb").astype(bf16) for un in units}
    Rb = {un: part(un, "rb").astype(bf16) for un in units}
    Ybs = {un: stack(part(un, "beb")) for un in units}
    Yks = {un: stack(part(un, "kb")) for un in units}
    Vb = {un: part(un, "v").astype(bf16) for un in units}
    Vs = {un: stack(part(un, "v")) for un in units}

    XR = {un: jnp.concatenate([X[un], Rb[un]], axis=0) for un in units}
    APb = {un: _dot_nt(XR[un], Ybs[un]) for un in units}
    APk = {un: _dot_nt(XR[un], Yks[un]) for un in units}
    BO = {un: _dot_nt(XR[un], Sb[un]) for un in units}
    Aab = {un: jnp.where(strict, APb[un][0:C], 0.0) for un in units}
    Aak = {un: jnp.where(strict, APk[un][0:C], 0.0).astype(bf16) for un in units}
    Pab = {un: jnp.where(incl, APb[un][C:2 * C], 0.0).astype(bf16) for un in units}
    Pak = {un: jnp.where(incl, APk[un][C:2 * C], 0.0).astype(bf16) for un in units}

    Tm = {un: eye + jnp.where(base, Aab[un], 0.0) for un in units}
    for off in offs:
        Xs = {un: _dot(jnp.where(off, Aab[un], 0.0).astype(bf16), stack(Tm[un])) for un in units}
        Tm = {un: Tm[un] + _dot(Tm[un].astype(bf16), stack(Xs[un])) for un in units}

    AV = {un: _dot(jnp.concatenate([Aak[un], Pak[un]], axis=0), Vs[un]) for un in units}
    Bm = {un: BO[un][0:C] + AV[un][0:C] for un in units}
    U = {un: _dot(Tm[un].astype(bf16), stack(Bm[un])) for un in units}
    Om = {un: BO[un][C:2 * C] + AV[un][C:2 * C] + _dot(Pab[un], stack(U[un])) for un in units}
    for un in units:
        bb, p = un
        upd = _dot_tn(jnp.concatenate([U[un].astype(bf16), Vb[un]], axis=0),
                      jnp.concatenate([part(un, "beh").astype(bf16), part(un, "kh").astype(bf16)], axis=0))
        s_ref[bb * RWKV_PAIRS + p] = jnp.where(blockdiag, S[un] * part(un, "etot") + upd, 0.0)

    @pl.when(emit)
    def _():
        for bb in range(nb):
            o_ref[0, bb] = jnp.concatenate([Om[bb, p] for p in range(RWKV_PAIRS)], axis=1)
            bo_ref[0, bb] = pre[bb]["bonus"]


def _split_dot_left(w, x):
    hi = x.astype(bf16)
    lo = (x - hi.astype(f32)).astype(bf16)
    return _dot(w, hi) + _dot(w, lo)


def _rwkv(z, w0, a0, w2p, a2p, k_k, k_a, r_k, ctx):
    B, L, _ = z.shape
    T = L - ctx
    nsub = RWKV_CHUNKS_PER_STEP
    while ctx % (nsub * RWKV_CHUNK) or T % (nsub * RWKV_CHUNK):
        nsub //= 2
    C = nsub * RWKV_CHUNK
    nc, nl = ctx // C, T // C
    W = RWKV_WIDTH

    def chunk(d, s):
        fwd = jnp.where(s < nc, nl + s, s - nc)
        bwd = jnp.where(s < nc, nl + nc - 1 - s, nl - 1 - (s - nc))
        return jnp.where(d == 0, fwd, bwd)

    def ochunk(d, s):
        sl = jnp.maximum(s - nc, 0)
        return jnp.where(d == 0, sl, nl - 1 - sl)

    nb = RWKV_SAMPLES_PER_STEP if B % RWKV_SAMPLES_PER_STEP == 0 else 1
    zblk = lambda c0, w: pl.BlockSpec((nb, C, w), lambda b, d, s: (b, chunk(d, s), c0 // w))
    dpar = lambda r: pl.BlockSpec((1, r, W), lambda b, d, s: (d, 0, 0))
    par = pl.BlockSpec((1, W), lambda b, d, s: (0, 0))
    oblk = pl.BlockSpec((1, nb, C, W), lambda b, d, s: (d, b, ochunk(d, s), 0))
    G = 2 * RWKV_HEAD_DIM
    return pl.pallas_call(
        functools.partial(_rwkv_kernel, nc=nc, nb=nb, nsub=nsub),
        grid=(B // nb, 2, nc + nl),
        in_specs=[zblk(C_RWK, W), zblk(C_RWV, W), zblk(C_RWR, W), zblk(C_LORA, 256),
                  dpar(1), dpar(1), dpar(G), dpar(G), par, par, par],
        out_specs=[oblk, oblk],
        out_shape=[jax.ShapeDtypeStruct((2, B, T, W), f32), jax.ShapeDtypeStruct((2, B, T, W), f32)],
        scratch_shapes=[pltpu.VMEM((nb * RWKV_PAIRS, G, G), f32)],
        compiler_params=_params(("arbitrary", "arbitrary", "arbitrary")),
        name="rwkv",
    )(z, z, z, z, w0, a0, w2p, a2p, k_k, k_a, r_k)


def _merge_kernel(x_ref, ret_ref, rw0_ref, rw1_ref, b0_ref, b1_ref, gate_ref, retg_ref, gd_ref, mod_ref,
                  ng_ref, gn_ref, lng_ref, lnb_ref, g2_ref, wbr_ref, wbw_ref, wout_ref, wrh_ref, wrl_ref,
                  x1_ref, h2_ref, lt_ref):
    hd = RET_HEAD_DIM
    ret = ret_ref[0]
    parts = []
    for hh in range(RET_HEADS):
        xh = ret[:, hh * hd:(hh + 1) * hd]
        mu = jnp.mean(xh, axis=-1, keepdims=True)
        dv = xh - mu
        var = jnp.mean(dv * dv, axis=-1, keepdims=True)
        parts.append(dv * lax.rsqrt(var + RET_EPS))
    yr = retg_ref[0].astype(f32) * (jnp.concatenate(parts, axis=1) * gn_ref[...])
    y_ret = _bdot(yr, wbr_ref[...])
    W = RWKV_WIDTH
    gi = lax.broadcasted_iota(jnp.int32, (W, W), 0)
    gj = lax.broadcasted_iota(jnp.int32, (W, W), 1)
    ones_bd = ((gi >> 6) == (gj >> 6)).astype(bf16)
    o = rw0_ref[0, 0] + rw1_ref[0, 0]
    mu = _split_dot(o, ones_bd) * (1.0 / RWKV_HEAD_DIM)
    dv = o - mu
    var = _bdot(dv * dv, ones_bd) * (1.0 / RWKV_HEAD_DIM)
    yw = dv * lax.rsqrt(var + RWKV_EPS) * lng_ref[...] + lnb_ref[...]
    gate = _dot(gd_ref[0], g2_ref[...])
    yw = (yw + b0_ref[0, 0] + b1_ref[0, 0]) * gate
    y_rw = _bdot(yw, wbw_ref[...])
    D = y_ret.shape[1]
    g = gate_ref[0].astype(f32)
    m = g[:, :D] * y_ret + g[:, D:] * y_rw
    y = _bdot(m, wout_ref[...])

    def rms(v, gg):
        return v * lax.rsqrt(jnp.mean(v * v, axis=-1, keepdims=True) + NORM_EPS) * gg

    x1 = x_ref[0] + mod_ref[0, 0:1, :] * rms(y, ng_ref[0:1, :])
    x1_ref[0] = x1
    h2 = rms(x1, ng_ref[1:2, :]) * (1.0 + mod_ref[0, 2:3, :]) + mod_ref[0, 1:2, :]
    h2b = h2.astype(bf16)
    h2_ref[0] = h2b
    h2l = (h2 - h2b.astype(f32)).astype(bf16)
    lgt = _dot(h2b, wrh_ref[...]) + _dot(h2l, wrh_ref[...]) + _dot(h2b, wrl_ref[...])
    lt_ref[0] = lgt.T[0:N_EXPERTS, :]


def _merge(x, ret_o, rw_o, bonus, z, mod2, ng12, gn, lng, lnb, g2, wbr, wbw, wout, wrh, wrl, ctx):
    B, T, D = x.shape
    tm = MERGE_ROWS if T % MERGE_ROWS == 0 else ROWS
    co = 0
    W = RWKV_WIDTH
    row = lambda w: pl.BlockSpec((1, tm, w), lambda b, i: (b, i, 0))
    dblk = lambda dd: pl.BlockSpec((1, 1, tm, W), lambda b, i: (dd, b, i, 0))
    zblk = lambda c0, w: pl.BlockSpec((1, tm, w), lambda b, i: (b, co + i, c0 // w))
    full = lambda a: pl.BlockSpec(a.shape, lambda b, i: (0,) * a.ndim)
    return pl.pallas_call(
        _merge_kernel,
        grid=(B, T // tm),
        in_specs=[row(D), row(W), dblk(0), dblk(1), dblk(0), dblk(1),
                  zblk(C_MERGE, 2 * D), zblk(C_RETG, W), zblk(C_LORA + 128, 128),
                  pl.BlockSpec((1, 3, D), lambda b, i: (b, 0, 0)),
                  full(ng12), full(gn), full(lng), full(lnb), full(g2), full(wbr), full(wbw), full(wout),
                  full(wrh), full(wrl)],
        out_specs=[row(D), row(D), pl.BlockSpec((1, N_EXPERTS, tm), lambda b, i: (b, 0, i))],
        out_shape=[jax.ShapeDtypeStruct((B, T, D), f32), jax.ShapeDtypeStruct((B, T, D), bf16),
                   jax.ShapeDtypeStruct((B, N_EXPERTS, T), f32)],
        compiler_params=_params(("arbitrary", "arbitrary")),
        name="merge",
    )(x, ret_o, rw_o, rw_o, bonus, bonus, z, z, z, mod2, ng12, gn, lng, lnb, g2, wbr, wbw, wout, wrh, wrl)


def _route_kernel(lt_ref, slot_ref, rt_ref, cnt_ref, slotf_ref, gate_ref, *, cap):
    B, NE, T = lt_ref.shape
    lg = lt_ref[...]
    mx = jnp.max(lg, axis=1, keepdims=True)
    ex = jnp.exp(lg - mx)
    aff = (ex / jnp.sum(ex, axis=1, keepdims=True)).reshape(B * NE, T)
    E = B * NE

    def count_ge(cand):
        return jnp.sum((aff >= cand).astype(f32), axis=1, keepdims=True)

    def exp_step(_, kk):
        k_lo, k_hi = kk
        km = jnp.floor((k_lo + k_hi) * 0.5)
        ok = count_ge(jnp.exp2(-km)) >= cap
        return jnp.where(ok, k_lo, km), jnp.where(ok, km, k_hi)

    k_lo, k_hi = lax.fori_loop(0, ROUTE_EXP_STEPS, exp_step,
                               (jnp.full((E, 1), -1.0, f32), jnp.full((E, 1), ROUTE_MAX_EXP, f32)))
    lo0 = jnp.where(k_hi >= ROUTE_MAX_EXP, 0.0, jnp.exp2(-k_hi))
    hi0 = jnp.exp2(-k_lo)

    def val_step(_, lh):
        lo, hi = lh
        mid = (lo + hi) * 0.5
        ok = count_ge(mid) >= cap
        return jnp.where(ok, mid, lo), jnp.where(ok, hi, mid)

    lo, hi = lax.fori_loop(0, ROUTE_VAL_STEPS, val_step, (lo0, hi0))
    gt = aff >= hi
    eq = (aff >= lo) & (aff < hi)
    need = cap - jnp.sum(gt.astype(f32), axis=1, keepdims=True)
    tri = (lax.broadcasted_iota(jnp.int32, (T, T), 0) < lax.broadcasted_iota(jnp.int32, (T, T), 1)).astype(bf16)
    eq_before = _dot(eq.astype(bf16), tri)
    sel = gt | (eq & (eq_before < need))
    slot = _dot(sel.astype(bf16), tri)
    slot_f = jnp.where(sel, slot, -1.0)
    slot_ref[...] = slot_f.astype(jnp.int32).reshape(B, NE, T)
    before = (lax.broadcasted_iota(jnp.int32, (T, 128), 0)
              < lax.broadcasted_iota(jnp.int32, (T, 128), 1) * GATHER_TILE).astype(bf16)
    cnt_ref[...] = _dot(sel.astype(bf16), before).astype(jnp.int32).reshape(B, NE, 128)
    slotf_ref[...] = slot_f.reshape(B, NE, T)
    gate_ref[...] = jnp.where(sel, aff, 0.0).reshape(B, NE, T)

    def transpose_sample(b, carry):
        packed = jnp.concatenate([slotf_ref[b], gate_ref[b], jnp.zeros((128 - 2 * NE, T), f32)], axis=0)
        rt_ref[b] = packed.T.astype(bf16)
        return carry

    lax.fori_loop(0, B, transpose_sample, 0)


def _route(lt, cap):
    B, E, T = lt.shape
    assert cap <= 256 and T % GATHER_TILE == 0 and T // GATHER_TILE < 128
    return pl.pallas_call(
        functools.partial(_route_kernel, cap=cap),
        grid=(1,),
        in_specs=[pl.BlockSpec((B, E, T), lambda i: (0, 0, 0))],
        out_specs=[pl.BlockSpec((B, E, T), lambda i: (0, 0, 0)), pl.BlockSpec((B, T, 128), lambda i: (0, 0, 0)),
                   pl.BlockSpec((B, E, 128), lambda i: (0, 0, 0))],
        out_shape=[jax.ShapeDtypeStruct((B, E, T), jnp.int32), jax.ShapeDtypeStruct((B, T, 128), bf16),
                   jax.ShapeDtypeStruct((B, E, 128), jnp.int32)],
        scratch_shapes=[pltpu.VMEM((B, E, T), f32), pltpu.VMEM((B, E, T), f32)],
        compiler_params=_params(("arbitrary",)),
        name="route",
    )(lt)


def _ffn_kernel(cnt_ref, slot_ref, h_ref, wg_ref, wu_ref, wd_ref, o_ref, wgb_ref, wub_ref, wdb_ref, xg_ref, *, cap):
    T = h_ref.shape[1]
    e = pl.program_id(0)

    @pl.when(pl.program_id(1) == 0)
    def _():
        wgb_ref[...] = wg_ref[0].astype(bf16)
        wub_ref[...] = wu_ref[0].astype(bf16)
        wdb_ref[...] = wd_ref[0].astype(bf16)

    nb = h_ref.shape[0]
    GT, GW, GA = GATHER_TILE, GATHER_WINDOW, GATHER_ALIGN
    nt = T // GT
    for bb in range(nb):
        b = pl.program_id(1) * nb + bb
        base = [pl.multiple_of(lax.shift_left(lax.shift_right_logical(cnt_ref[b, e, j], GATHER_ALIGN_LOG2),
                                              GATHER_ALIGN_LOG2), GA) for j in range(nt)]
        fits = cnt_ref[b, e, 1] - base[0] <= GW
        for j in range(1, nt):
            fits = fits & (cnt_ref[b, e, j + 1] - base[j] <= GW)

        @pl.when(fits)
        def _():
            xg_ref[bb] = jnp.zeros(xg_ref.shape[1:], f32)
            for j in range(nt):
                rows = base[j] + lax.broadcasted_iota(jnp.int32, (GW, GT), 0)
                onehot = (slot_ref[bb, 0, j:j + 1, :] == rows).astype(bf16)
                xg_ref[bb, pl.ds(base[j], GW), :] += _dot(onehot, h_ref[bb, j * GT:(j + 1) * GT, :])

        @pl.when(jnp.logical_not(fits))
        def _():
            rows = lax.broadcasted_iota(jnp.int32, (cap, GT), 0)
            acc = jnp.zeros((cap, h_ref.shape[2]), f32)
            for j in range(nt):
                onehot = (slot_ref[bb, 0, j:j + 1, :] == rows).astype(bf16)
                acc = acc + _dot(onehot, h_ref[bb, j * GT:(j + 1) * GT, :])
            xg_ref[bb, 0:cap, :] = acc

    xg = jnp.concatenate([xg_ref[bb, 0:cap, :].astype(bf16) for bb in range(nb)], axis=0)
    hg = _dot(xg, wgb_ref[...])
    hu = _dot(xg, wub_ref[...])
    hid = (hg * _sigmoid(hg) * hu).astype(bf16)
    out = _dot(hid, wdb_ref[...]).astype(bf16)
    for bb in range(nb):
        o_ref[bb, 0] = out[bb * cap:(bb + 1) * cap]


def _ffn(cnt, slot4, h2, wg, wu, wd, cap):
    B, T, D = h2.shape
    E, _, F = wg.shape
    nb = FFN_SAMPLES_PER_STEP if B % FFN_SAMPLES_PER_STEP == 0 else 1
    grid_spec = pltpu.PrefetchScalarGridSpec(
        num_scalar_prefetch=1,
        grid=(E, B // nb),
        in_specs=[pl.BlockSpec((nb, 1, T // GATHER_TILE, GATHER_TILE), lambda e, b, c: (b, e, 0, 0)),
                  pl.BlockSpec((nb, T, D), lambda e, b, c: (b, 0, 0)),
                  pl.BlockSpec((1, D, F), lambda e, b, c: (e, 0, 0)),
                  pl.BlockSpec((1, D, F), lambda e, b, c: (e, 0, 0)),
                  pl.BlockSpec((1, F, D), lambda e, b, c: (e, 0, 0))],
        out_specs=pl.BlockSpec((nb, 1, cap, D), lambda e, b, c: (b, e, 0, 0)),
        scratch_shapes=[pltpu.VMEM((D, F), bf16), pltpu.VMEM((D, F), bf16), pltpu.VMEM((F, D), bf16),
                        pltpu.VMEM((nb, cap + GATHER_WINDOW, D), f32)])
    return pl.pallas_call(
        functools.partial(_ffn_kernel, cap=cap),
        grid_spec=grid_spec,
        out_shape=jax.ShapeDtypeStruct((B, E, cap, D), bf16),
        compiler_params=_params(("arbitrary", "arbitrary"), FFN_VMEM_LIMIT),
        name="ffn",
    )(cnt, slot4, h2, wg, wu, wd)


def _combine_kernel(cnt_ref, rt_ref, eo_ref, x1_ref, mod_ref, ng_ref, o_ref, y_ref, *, cap):
    b = pl.program_id(0)
    i = pl.program_id(1)
    rt = rt_ref[0].astype(f32)
    tm = rt.shape[0]
    E = eo_ref.shape[1]
    W = min(SCATTER_WINDOW, cap)
    GS = SCATTER_GROUP
    ST = GATHER_TILE
    for st in range(tm // ST):
        t = i * (tm // ST) + st
        r = rt[st * ST:(st + 1) * ST]
        base, fits = [], None
        for e in range(E):
            lo = cnt_ref[b, e, t]
            be = jnp.minimum(lax.shift_left(lax.shift_right_logical(lo, GATHER_ALIGN_LOG2), GATHER_ALIGN_LOG2),
                             cap - W)
            base.append(pl.multiple_of(be, GATHER_ALIGN))
            ok = cnt_ref[b, e, t + 1] - be <= W
            fits = ok if fits is None else fits & ok

        def scatter_matrix(e, first, width, r=r):
            cols = (first + lax.broadcasted_iota(jnp.int32, (ST, width), 1)).astype(f32)
            return jnp.where(r[:, e:e + 1] == cols, r[:, E + e:E + e + 1], 0.0).astype(bf16)

        @pl.when(fits)
        def _():
            y = jnp.zeros((ST, y_ref.shape[1]), f32)
            for g in range(0, E, GS):
                p = jnp.concatenate([scatter_matrix(e, base[e], W) for e in range(g, g + GS)], axis=1)
                rows = jnp.concatenate([eo_ref[0, e, pl.ds(base[e], W), :] for e in range(g, g + GS)], axis=0)
                y = y + _dot(p, rows)
            y_ref[st * ST:(st + 1) * ST, :] = y

        @pl.when(jnp.logical_not(fits))
        def _():
            y = jnp.zeros((ST, y_ref.shape[1]), f32)
            for e in range(E):
                y = y + _dot(scatter_matrix(e, 0, cap), eo_ref[0, e])
            y_ref[st * ST:(st + 1) * ST, :] = y

    y = y_ref[...]
    yn = y * lax.rsqrt(jnp.mean(y * y, axis=-1, keepdims=True) + NORM_EPS) * ng_ref[...]
    o_ref[0] = x1_ref[0] + mod_ref[0] * yn


def _combine(cnt, rt, eo, x1, g2mod, ng3, cap):
    B, T, D = x1.shape
    E = eo.shape[1]
    tm = 512 if T % 512 == 0 else T
    assert E % SCATTER_GROUP == 0 and tm % GATHER_TILE == 0
    grid_spec = pltpu.PrefetchScalarGridSpec(
        num_scalar_prefetch=1,
        grid=(B, T // tm),
        in_specs=[pl.BlockSpec((1, tm, 128), lambda b, i, c: (b, i, 0)),
                  pl.BlockSpec((1, E, cap, D), lambda b, i, c: (b, 0, 0, 0)),
                  pl.BlockSpec((1, tm, D), lambda b, i, c: (b, i, 0)),
                  pl.BlockSpec((1, 1, D), lambda b, i, c: (b, 0, 0)),
                  pl.BlockSpec((1, D), lambda b, i, c: (0, 0))],
        out_specs=pl.BlockSpec((1, tm, D), lambda b, i, c: (b, i, 0)),
        scratch_shapes=[pltpu.VMEM((tm, D), f32)])
    return pl.pallas_call(
        functools.partial(_combine_kernel, cap=cap),
        grid_spec=grid_spec,
        out_shape=jax.ShapeDtypeStruct((B, T, D), f32),
        compiler_params=_params(("arbitrary", "arbitrary")),
        name="combine",
    )(cnt, rt, eo, x1, g2mod, ng3)


def _permute_columns(w):
    sk, sv, rk, rv, wd, ad = 0, 512, 1024, 1536, 2048, 2112
    q0 = 2176
    rq, rg, rr, gd, mg = q0, q0 + 512, q0 + 1024, q0 + 1536, q0 + 1664
    order = [(mg, 2048), (sk, 512), (sv, 512), (rq, 512), (rg, 512), (rk, 512), (rv, 512), (rr, 512),
             (wd, 64), (ad, 64), (gd, 128)]
    parts = [w[:, a:a + n] for a, n in order]
    parts.append(jnp.zeros((w.shape[0], IN_COLS - USED_COLS), w.dtype))
    return jnp.concatenate(parts, axis=1)


def _rope_tables(T, CT):
    t = jnp.arange(T)
    nfreq = RET_HEAD_DIM // 4
    inv = ROPE_BASE ** (-jnp.arange(nfreq, dtype=f32) / nfreq)
    ang = jnp.concatenate([(t // GRID_W).astype(f32)[:, None] * inv,
                           (t % GRID_W).astype(f32)[:, None] * inv], axis=-1)
    cos, sin = jnp.cos(ang), jnp.sin(ang)
    cosf = jnp.concatenate([cos, cos], axis=1)
    sinf = jnp.concatenate([-sin, sin], axis=1)
    return (jnp.concatenate([cosf, jnp.ones((CT, RET_HEAD_DIM), f32)], axis=0),
            jnp.concatenate([sinf, jnp.zeros((CT, RET_HEAD_DIM), f32)], axis=0))


def kernel(x, c, ctx, c_ctx, w_mod, b_mod, norm_g, w_in, ret_log_decay, ret_gn_g, rwkv_mu, rwkv_k_k, rwkv_k_a,
           rwkv_r_k, rwkv_w0, rwkv_w2, rwkv_a0, rwkv_a2, rwkv_g2, rwkv_ln_g, rwkv_ln_b, w_br_ret, w_br_rwkv,
           w_out, w_router, w_gate, w_up, w_down):
    B, T, D = x.shape
    CT = ctx.shape[1]
    assert w_mod.shape[0] == 1 and D == D_MODEL
    assert CT % ROWS == 0 and T % ROWS == 0 and T % GRID_W == 0
    cap = CAPACITY_FACTOR * T // N_EXPERTS
    assert cap % 8 == 0

    mrows = -(-(B + 1) // 8) * 8
    cc = jnp.zeros((mrows, D), f32).at[:B].set(c).at[B].set(c_ctx)
    mod = _modulation(cc, w_mod[0], b_mod[0])
    lat = mod[:B].reshape(B, N_MOD, D)
    cm = jnp.broadcast_to(mod[B].reshape(1, N_MOD, D), (B, N_MOD, D))
    modrows = jnp.concatenate([lat[:, 0:2], cm[:, 0:2]], axis=1)

    w_perm = _permute_columns(w_in[0]).astype(bf16)
    mu = rwkv_mu[0]
    ss = 2 * RWKV_WIDTH + DECAY_LORA + ICLR_LORA
    mu_full = jnp.zeros((2, IN_COLS), f32)
    mu_full = mu_full.at[:, C_RWK:C_RWK + 1024].set(mu[:, 0:1024])
    mu_full = mu_full.at[:, C_RWR:C_RWR + 512].set(mu[:, ss:ss + 512])
    mu_full = mu_full.at[:, C_LORA:C_LORA + 128].set(mu[:, 1024:ss])
    mu_full = mu_full.at[:, C_LORA + 128:USED_COLS].set(mu[:, ss + 512:])
    cosf, sinf = _rope_tables(T, CT)
    z = _inproj(x, ctx, modrows, norm_g[0, 0:1], w_perm, mu_full, cosf, sinf)

    lg = -jnp.exp(ret_log_decay[0].astype(f32))
    ret_o = _retention(lg, z, CT)

    G = 2 * RWKV_HEAD_DIM
    w2p = jnp.zeros((2, G, RWKV_WIDTH), f32).at[:, :DECAY_LORA].set(rwkv_w2[0]).astype(bf16)
    a2p = jnp.zeros((2, G, RWKV_WIDTH), f32).at[:, DECAY_LORA:].set(rwkv_a2[0]).astype(bf16)
    rw_o, bonus = _rwkv(z, rwkv_w0[0][:, None, :], rwkv_a0[0][:, None, :], w2p, a2p,
                        rwkv_k_k[0][None], rwkv_k_a[0][None], rwkv_r_k[0][None], CT)

    mod2 = jnp.stack([lat[:, 2], lat[:, 3], lat[:, 4]], axis=1)
    wr_pad = jnp.zeros((D, 128), f32).at[:, :N_EXPERTS].set(w_router[0])
    wr_hi = wr_pad.astype(bf16)
    wr_lo = (wr_pad - wr_hi.astype(f32)).astype(bf16)
    x1, h2, lt = _merge(x, ret_o, rw_o, bonus, z, mod2, norm_g[0, 1:3], ret_gn_g[0][None], rwkv_ln_g[0][None],
                        rwkv_ln_b[0][None], rwkv_g2[0].astype(bf16), w_br_ret[0].astype(bf16),
                        w_br_rwkv[0].astype(bf16), w_out[0].astype(bf16), wr_hi, wr_lo, CT)

    slot, rt, cnt = _route(lt, cap)
    cnt = cnt[:, :, :T // GATHER_TILE + 1]
    eo = _ffn(cnt, slot.reshape(B, N_EXPERTS, T // GATHER_TILE, GATHER_TILE), h2, w_gate[0], w_up[0], w_down[0], cap)
    return _combine(cnt, rt, eo, x1, lat[:, 5:6], norm_g[0, 3:4], cap)
```

```python
import functools

import numpy as np
import jax
import jax.numpy as jnp
from jax import lax
from jax.experimental import pallas as pl
from jax.experimental.pallas import tpu as pltpu

f32 = jnp.float32
bf16 = jnp.bfloat16

D_MODEL = 1024
GRID_W = 64
RET_HEAD_DIM = 128
RET_WIDTH = 512
RET_HEADS = 4
RET_CHUNK = 128
RET_EPS = 1e-5
ROPE_BASE = 10000.0
RWKV_HEAD_DIM = 64
RWKV_WIDTH = 512
RWKV_PAIRS = 4
RWKV_CHUNK = 64
RWKV_CHUNKS_PER_STEP = 4
RWKV_SAMPLES_PER_STEP = 8
DECAY_LORA = 64
ICLR_LORA = 64
GATE_LORA = 128
RWKV_EPS = 64e-5
N_EXPERTS = 16
EXPERT_FF = 1024
CAPACITY_FACTOR = 2
N_MOD = 6
NORM_EPS = 1e-6

C_MERGE, C_RETK, C_RETV, C_RETQ, C_RETG = 0, 2048, 2560, 3072, 3584
C_RWK, C_RWV, C_RWR, C_LORA = 4096, 4608, 5120, 5632
USED_COLS = 5888
TN = 512
IN_COLS = -(-USED_COLS // TN) * TN
ROWS = 256
STEP_ROWS = 768
POST_ROWS = 128
MERGE_ROWS = 512

ROUTE_MAX_EXP = 126.0
ROUTE_EXP_STEPS = 8
ROUTE_VAL_STEPS = 26

VMEM_LIMIT = 56 * 1024 * 1024
FFN_SAMPLES_PER_STEP = 2
GATHER_TILE = 256
GATHER_ALIGN_LOG2 = 4
GATHER_ALIGN = 1 << GATHER_ALIGN_LOG2
GATHER_WINDOW = 80
SCATTER_WINDOW = 64
SCATTER_GROUP = 4
FFN_VMEM_LIMIT = 62 * 1024 * 1024


def _dot(a, b):
    return jnp.dot(a, b, preferred_element_type=f32)


def _dot_nt(a, b):
    return lax.dot_general(a, b, (((1,), (1,)), ((), ())), preferred_element_type=f32)


def _dot_tn(a, b):
    return lax.dot_general(a, b, (((0,), (0,)), ((), ())), preferred_element_type=f32)


def _bdot(a, b):
    return _dot(a.astype(bf16), b.astype(bf16))


def _split_dot(x, w):
    hi = x.astype(bf16)
    lo = (x - hi.astype(f32)).astype(bf16)
    return _dot(hi, w) + _dot(lo, w)


def _sigmoid(x):
    return 0.5 * jnp.tanh(0.5 * x) + 0.5


def _params(sem, limit=VMEM_LIMIT):
    return pltpu.CompilerParams(dimension_semantics=sem, vmem_limit_bytes=limit)


def _mod_kernel(c_ref, w_ref, b_ref, o_ref):
    c = c_ref[...]
    s = c * jax.nn.sigmoid(c)
    o_ref[...] = _bdot(s, w_ref[...]) + b_ref[...]


def _modulation(cc, w_mod, b_mod):
    m, d = cc.shape
    n = w_mod.shape[1]
    tn = 512
    return pl.pallas_call(
        _mod_kernel,
        grid=(n // tn,),
        in_specs=[pl.BlockSpec((m, d), lambda j: (0, 0)),
                  pl.BlockSpec((d, tn), lambda j: (0, j)),
                  pl.BlockSpec((1, tn), lambda j: (0, j))],
        out_specs=pl.BlockSpec((m, tn), lambda j: (0, j)),
        out_shape=jax.ShapeDtypeStruct((m, n), f32),
        compiler_params=_params(("arbitrary",)),
        name="mod",
    )(cc, w_mod, b_mod.reshape(1, n))


def _inproj_kernel(x_ref, c_ref, mod_ref, g_ref, w_ref, mu_ref, cos_ref, sin_ref, o_ref, h_ref, z_ref):
    n = pl.program_id(1)
    T = x_ref.shape[1]
    L = T + c_ref.shape[1]
    nlat = T // ROWS
    nchunk = L // ROWS
    PAD = 8

    SR = STEP_ROWS if L % STEP_ROWS == 0 else ROWS

    @pl.when(n == 0)
    def _():
        z_ref[0:PAD, :] = jnp.zeros((PAD, TN), f32)
        z_ref[PAD + L:PAD + L + PAD, :] = jnp.zeros((PAD, TN), f32)

        def norm_chunk(src_ref, s0, r0, o):
            xb = src_ref[0, pl.ds(s0, ROWS), :]
            y = xb * lax.rsqrt(jnp.mean(xb * xb, axis=-1, keepdims=True) + NORM_EPS) * g_ref[...]
            sh = mod_ref[0, o:o + 1, :]
            sc = mod_ref[0, o + 1:o + 2, :]
            h_ref[pl.ds(r0, ROWS), :] = (y * (1.0 + sc) + sh).astype(bf16)

        def lat_body(i, carry):
            r0 = pl.multiple_of(i * ROWS, ROWS)
            norm_chunk(x_ref, r0, r0, 0)
            return carry

        lax.fori_loop(0, nlat, lat_body, 0)
        for j in range(nlat, nchunk):
            norm_chunk(c_ref, j * ROWS - T, j * ROWS, 2)

    PR = POST_ROWS

    def rope(scale):
        def post(z, r0):
            if scale != 1.0:
                z = z * scale
            cs = cos_ref[pl.ds(r0, SR), :]
            sn = sin_ref[pl.ds(r0, SR), :]
            parts = []
            for hh in range(TN // RET_HEAD_DIM):
                zh = z[:, hh * RET_HEAD_DIM:(hh + 1) * RET_HEAD_DIM]
                parts.append(zh * cs + pltpu.roll(zh, RET_HEAD_DIM // 2, 1) * sn)
            return jnp.concatenate(parts, axis=1)
        return post

    def shifted(r0, at_boundary):
        win = z_ref[pl.ds(r0, PR + 2 * PAD), :]
        prev = win[PAD - 1:PAD - 1 + PR]
        z = win[PAD:PAD + PR]
        nxt = win[PAD + 1:PAD + 1 + PR]
        if at_boundary:
            rid = r0 + lax.broadcasted_iota(jnp.int32, (PR, 1), 0)
            prev = jnp.where(rid == T, 0.0, prev)
            nxt = jnp.where(rid == T - 1, 0.0, nxt)
        mu0 = mu_ref[0:1, :]
        mu1 = mu_ref[1:2, :]
        return (1.0 - mu0 - mu1) * z + mu0 * prev + mu1 * nxt

    def lora_act(zs):
        lane = lax.broadcasted_iota(jnp.int32, (1, TN), 1)
        return jnp.where(lane < DECAY_LORA, jnp.tanh(zs),
                         jnp.where(lane < DECAY_LORA + ICLR_LORA, zs, jax.nn.sigmoid(zs)))

    def product(r0):
        return _dot(h_ref[pl.ds(r0, SR), :], w_ref[...])

    def direct(post):
        def body(i, carry):
            r0 = pl.multiple_of(i * SR, SR)
            o_ref[0, pl.ds(r0, SR), :] = post(product(r0), r0).astype(bf16)
            return carry
        lax.fori_loop(0, L // SR, body, 0)

    def via_buffer(act):
        def mm_body(i, carry):
            r0 = pl.multiple_of(i * SR, SR)
            z_ref[pl.ds(PAD + r0, SR), :] = product(r0)
            return carry
        lax.fori_loop(0, L // SR, mm_body, 0)

        def post(r0, at_boundary):
            o_ref[0, pl.ds(r0, PR), :] = act(shifted(r0, at_boundary)).astype(bf16)

        def post_body(i, carry):
            post(pl.multiple_of(i * PR, PR), False)
            return carry

        seam = T // PR
        lax.fori_loop(0, seam - 1, post_body, 0)
        post((seam - 1) * PR, True)
        post(seam * PR, True)
        lax.fori_loop(seam + 1, L // PR, post_body, 0)

    t = lambda c: c // TN
    pl.when(n < t(C_RETK))(lambda: direct(lambda z, r0: _sigmoid(z)))
    pl.when((n >= t(C_RETK)) & (n < t(C_RETV)))(lambda: direct(rope(RET_HEAD_DIM ** -0.5)))
    pl.when((n >= t(C_RETV)) & (n < t(C_RETQ)))(lambda: direct(lambda z, r0: z))
    pl.when((n >= t(C_RETQ)) & (n < t(C_RETG)))(lambda: direct(rope(1.0)))
    pl.when((n >= t(C_RETG)) & (n < t(C_RWK)))(lambda: direct(lambda z, r0: z * _sigmoid(z)))
    pl.when((n >= t(C_RWK)) & (n < t(C_LORA)))(lambda: via_buffer(lambda zs: zs))
    pl.when(n >= t(C_LORA))(lambda: via_buffer(lora_act))


def _inproj(x, ctx, modrows, g0, w_perm, mu_full, cosf, sinf):
    B, T, D = x.shape
    CT = ctx.shape[1]
    L = T + CT
    nt = IN_COLS // TN
    return pl.pallas_call(
        _inproj_kernel,
        grid=(B, nt),
        in_specs=[pl.BlockSpec((1, T, D), lambda b, n: (b, 0, 0)),
                  pl.BlockSpec((1, CT, D), lambda b, n: (b, 0, 0)),
                  pl.BlockSpec((1, 4, D), lambda b, n: (b, 0, 0)),
                  pl.BlockSpec((1, D), lambda b, n: (0, 0)),
                  pl.BlockSpec((D, TN), lambda b, n: (0, n)),
                  pl.BlockSpec((2, TN), lambda b, n: (0, n)),
                  pl.BlockSpec((L, RET_HEAD_DIM), lambda b, n: (0, 0)),
                  pl.BlockSpec((L, RET_HEAD_DIM), lambda b, n: (0, 0))],
        out_specs=pl.BlockSpec((1, L, TN), lambda b, n: (b, 0, n)),
        out_shape=jax.ShapeDtypeStruct((B, L, IN_COLS), bf16),
        scratch_shapes=[pltpu.VMEM((L, D), bf16), pltpu.VMEM((L + 16, TN), f32)],
        compiler_params=_params(("arbitrary", "arbitrary")),
        name="inproj",
    )(x, ctx, modrows, g0, w_perm, mu_full, cosf, sinf)


def _ret_kernel(lg_ref, q_ref, k_ref, v_ref, o_ref, r_ref, tab_ref, *, ctx):
    L = q_ref.shape[1]
    Cc = RET_CHUNK
    hd = RET_HEAD_DIM
    lat = L - ctx
    nc = ctx // Cc
    nl = lat // Cc
    combos = [(h, d) for h in range(RET_HEADS) for d in (0, 1)]
    INTRA, CROSS, TAIL, DECAY = 0, 1, 2, 3

    @pl.when(pl.program_id(0) == 0)
    def _():
        ii = lax.broadcasted_iota(jnp.int32, (Cc, Cc), 0).astype(f32)
        jj = lax.broadcasted_iota(jnp.int32, (Cc, Cc), 1).astype(f32)
        for idx, (h, d) in enumerate(combos):
            lg = lg_ref[d, h]
            if d == 0:
                diff = ii - jj
                cross = jnp.exp(lg * (ii + 1.0))
                tailw = jnp.exp(lg * (Cc - 1.0 - ii))
            else:
                diff = jj - ii
                cross = jnp.exp(lg * (Cc - ii))
                tailw = jnp.exp(lg * ii)
            tab_ref[idx, INTRA] = jnp.where(diff >= 0, jnp.exp(lg * jnp.maximum(diff, 0.0)), 0.0)
            tab_ref[idx, CROSS] = cross
            tab_ref[idx, TAIL] = tailw
            tab_ref[idx, DECAY] = jnp.exp(jnp.zeros((Cc, Cc), f32) + lg * Cc)

    r_ref[...] = jnp.zeros(r_ref.shape, f32)
    o_ref[...] = jnp.zeros(o_ref.shape, f32)

    def cols(h):
        return slice(h * hd, (h + 1) * hd)

    def update(idx, kc, vc):
        ks = (kc.astype(f32) * tab_ref[idx, TAIL]).astype(bf16)
        r_ref[idx] = r_ref[idx] * tab_ref[idx, DECAY] + _dot_tn(ks, vc)

    def ctx_step(s, carry):
        for idx, (h, d) in enumerate(combos):
            row0 = pl.multiple_of(lat + (s * Cc if d == 0 else (nc - 1 - s) * Cc), Cc)
            update(idx, k_ref[0, pl.ds(row0, Cc), cols(h)], v_ref[0, pl.ds(row0, Cc), cols(h)])
        return carry

    def lat_step(s, carry):
        t0s, qs, ks, vs = [], [], [], []
        for h, d in combos:
            t0 = pl.multiple_of(s * Cc if d == 0 else (nl - 1 - s) * Cc, Cc)
            row0 = t0
            t0s.append(t0)
            qs.append(q_ref[0, pl.ds(row0, Cc), cols(h)])
            ks.append(k_ref[0, pl.ds(row0, Cc), cols(h)])
            vs.append(v_ref[0, pl.ds(row0, Cc), cols(h)])
        n = len(combos)
        sc = [(_dot_nt(qs[i], ks[i]) * tab_ref[i, INTRA]).astype(bf16) for i in range(n)]
        oc = [_dot(qs[i], r_ref[i].astype(bf16)) * tab_ref[i, CROSS] for i in range(n)]
        oi = [_dot(sc[i], vs[i]) for i in range(n)]
        for i, (h, d) in enumerate(combos):
            o_ref[0, pl.ds(t0s[i], Cc), cols(h)] += oi[i] + oc[i]
        for i in range(n):
            update(i, ks[i], vs[i])
        return carry

    lax.fori_loop(0, nc, ctx_step, 0)
    lax.fori_loop(0, nl, lat_step, 0)


def _retention(lg, z, ctx):
    B, L, _ = z.shape
    T = L - ctx
    W = RET_WIDTH
    Cc = RET_CHUNK
    blk = lambda c0: pl.BlockSpec((1, L, W), lambda b: (b, 0, c0 // W))
    nchain = 2 * RET_HEADS
    return pl.pallas_call(
        functools.partial(_ret_kernel, ctx=ctx),
        grid=(B,),
        in_specs=[pl.BlockSpec(memory_space=pltpu.SMEM), blk(C_RETQ), blk(C_RETK), blk(C_RETV)],
        out_specs=pl.BlockSpec((1, T, W), lambda b: (b, 0, 0)),
        out_shape=jax.ShapeDtypeStruct((B, T, W), f32),
        scratch_shapes=[pltpu.VMEM((nchain, Cc, Cc), f32), pltpu.VMEM((nchain, 4, Cc, Cc), f32)],
        compiler_params=_params(("arbitrary",)),
        name="ret",
    )(lg, z, z, z)


def _rwkv_kernel(k_ref, v_ref, r_ref, lo_ref, w0_ref, a0_ref, w2_ref, a2_ref, kk_ref, ka_ref, rk_ref,
                 o_ref, bo_ref, s_ref, *, nc, nb, nsub):
    d = pl.program_id(1)
    s = pl.program_id(2)
    C = RWKV_CHUNK

    @pl.when(s == 0)
    def _():
        s_ref[...] = jnp.zeros(s_ref.shape, f32)

    def sub_chunk(j, carry):
        r0 = pl.multiple_of(jnp.where(d == 1, nsub - 1 - j, j) * C, C)
        rows = lambda ref: ref.at[:, pl.ds(r0, C), :]
        _rwkv_chunk(rows(k_ref), rows(v_ref), rows(r_ref), rows(lo_ref), w0_ref, a0_ref, w2_ref, a2_ref,
                    kk_ref, ka_ref, rk_ref, o_ref.at[:, :, pl.ds(r0, C), :], bo_ref.at[:, :, pl.ds(r0, C), :],
                    s_ref, rev=d == 1, emit=s >= nc, nb=nb)
        return carry

    lax.fori_loop(0, nsub, sub_chunk, 0)


def _rwkv_chunk(k_ref, v_ref, r_ref, lo_ref, w0_ref, a0_ref, w2_ref, a2_ref, kk_ref, ka_ref, rk_ref,
                o_ref, bo_ref, s_ref, *, rev, emit, nb):
    C = RWKV_CHUNK
    G = 2 * RWKV_HEAD_DIM

    ii = lax.broadcasted_iota(jnp.int32, (C, G), 0)
    lane = lax.broadcasted_iota(jnp.int32, (C, G), 1)
    jj = lane & (RWKV_HEAD_DIM - 1)
    head0 = lane < RWKV_HEAD_DIM
    dlt = jnp.where(rev, ii - jj, jj - ii)
    strict = dlt < 0
    incl = dlt <= 0
    eye = (ii == jj).astype(f32)
    gi = lax.broadcasted_iota(jnp.int32, (G, G), 0)
    gj = lax.broadcasted_iota(jnp.int32, (G, G), 1)
    blockdiag = (gi >= RWKV_HEAD_DIM) == (gj >= RWKV_HEAD_DIM)
    ones_bd = blockdiag.astype(bf16)
    tri = incl[:, 0:C].astype(bf16)
    base = (ii >> 1) == (jj >> 1)
    offs = [((ii >> (lv + 1)) == (jj >> (lv + 1))) & ((ii >> lv) != (jj >> lv)) for lv in range(1, 6)]

    def segsum(x):
        return jnp.concatenate(
            [_bdot(x[:, p * G:(p + 1) * G], ones_bd) for p in range(RWKV_PAIRS)], axis=1)

    def stack(x):
        xb = x.astype(bf16)
        zero = jnp.zeros_like(xb)
        return jnp.concatenate([jnp.where(head0, xb, zero), jnp.where(head0, zero, xb)], axis=0)

    pre = []
    for bb in range(nb):
        kx = k_ref[bb].astype(f32)
        vx = v_ref[bb].astype(f32)
        rx = r_ref[bb].astype(f32)
        lo = lo_ref[bb, :, 0:DECAY_LORA + ICLR_LORA]
        u = w0_ref[0] + _dot(lo, w2_ref[0])
        softplus = jnp.maximum(-u, 0.0) + jnp.log1p(jnp.exp(-jnp.abs(u)))
        lw = -jnp.exp(-softplus - 0.5)
        a = jax.nn.sigmoid(a0_ref[0] + _dot(lo, a2_ref[0]))
        kkr = kx * kk_ref[...]
        kk = kkr * lax.rsqrt(segsum(kkr * kkr) + 1e-12)
        kd = kx * (1.0 + (a - 1.0) * ka_ref[...])
        be = kk * a
        bonus = segsum(rx * kd * rk_ref[...]) * vx
        cum = _split_dot_left(tri, lw)
        tot = jnp.where(rev, cum[0:1, :], cum[C - 1:C, :])
        gneg = jnp.exp(-cum)
        gh = jnp.exp(tot - cum)
        pre.append(dict(alb=-kk * jnp.exp(cum - lw), rb=rx * jnp.exp(cum), beb=be * gneg, kb=kd * gneg,
                        beh=be * gh, kh=kd * gh, etot=jnp.exp(tot), v=vx, bonus=bonus))

    units = [(bb, p) for bb in range(nb) for p in range(RWKV_PAIRS)]
    sl = lambda p: slice(p * G, (p + 1) * G)
    part = lambda un, name: pre[un[0]][name][:, sl(un[1])]
    S = {un: s_ref[un[0] * RWKV_PAIRS + un[1]] for un in units}
    Sb = {un: S[un].astype(bf16) for un in units}
    X = {un: part(un, "alb").astype(bf16) for un in units}
    Rb = {un: part(un, "rb").astype(bf16) for un in units}
    Ybs = {un: stack(part(un, "beb")) for un in units}
    Yks = {un: stack(part(un, "kb")) for un in units}
    Vb = {un: part(un, "v").astype(bf16) for un in units}
    Vs = {un: stack(part(un, "v")) for un in units}

    XR = {un: jnp.concatenate([X[un], Rb[un]], axis=0) for un in units}
    APb = {un: _dot_nt(XR[un], Ybs[un]) for un in units}
    APk = {un: _dot_nt(XR[un], Yks[un]) for un in units}
    BO = {un: _dot_nt(XR[un], Sb[un]) for un in units}
    Aab = {un: jnp.where(strict, APb[un][0:C], 0.0) for un in units}
    Aak = {un: jnp.where(strict, APk[un][0:C], 0.0).astype(bf16) for un in units}
    Pab = {un: jnp.where(incl, APb[un][C:2 * C], 0.0).astype(bf16) for un in units}
    Pak = {un: jnp.where(incl, APk[un][C:2 * C], 0.0).astype(bf16) for un in units}

    Tm = {un: eye + jnp.where(base, Aab[un], 0.0) for un in units}
    for off in offs:
        Xs = {un: _dot(jnp.where(off, Aab[un], 0.0).astype(bf16), stack(Tm[un])) for un in units}
        Tm = {un: Tm[un] + _dot(Tm[un].astype(bf16), stack(Xs[un])) for un in units}

    AV = {un: _dot(jnp.concatenate([Aak[un], Pak[un]], axis=0), Vs[un]) for un in units}
    Bm = {un: BO[un][0:C] + AV[un][0:C] for un in units}
    U = {un: _dot(Tm[un].astype(bf16), stack(Bm[un])) for un in units}
    Om = {un: BO[un][C:2 * C] + AV[un][C:2 * C] + _dot(Pab[un], stack(U[un])) for un in units}
    for un in units:
        bb, p = un
        upd = _dot_tn(jnp.concatenate([U[un].astype(bf16), Vb[un]], axis=0),
                      jnp.concatenate([part(un, "beh").astype(bf16), part(un, "kh").astype(bf16)], axis=0))
        s_ref[bb * RWKV_PAIRS + p] = jnp.where(blockdiag, S[un] * part(un, "etot") + upd, 0.0)

    @pl.when(emit)
    def _():
        for bb in range(nb):
            o_ref[0, bb] = jnp.concatenate([Om[bb, p] for p in range(RWKV_PAIRS)], axis=1)
            bo_ref[0, bb] = pre[bb]["bonus"]


def _split_dot_left(w, x):
    hi = x.astype(bf16)
    lo = (x - hi.astype(f32)).astype(bf16)
    return _dot(w, hi) + _dot(w, lo)


def _rwkv(z, w0, a0, w2p, a2p, k_k, k_a, r_k, ctx):
    B, L, _ = z.shape
    T = L - ctx
    nsub = RWKV_CHUNKS_PER_STEP
    while ctx % (nsub * RWKV_CHUNK) or T % (nsub * RWKV_CHUNK):
        nsub //= 2
    C = nsub * RWKV_CHUNK
    nc, nl = ctx // C, T // C
    W = RWKV_WIDTH

    def chunk(d, s):
        fwd = jnp.where(s < nc, nl + s, s - nc)
        bwd = jnp.where(s < nc, nl + nc - 1 - s, nl - 1 - (s - nc))
        return jnp.where(d == 0, fwd, bwd)

    def ochunk(d, s):
        sl = jnp.maximum(s - nc, 0)
        return jnp.where(d == 0, sl, nl - 1 - sl)

    nb = RWKV_SAMPLES_PER_STEP if B % RWKV_SAMPLES_PER_STEP == 0 else 1
    zblk = lambda c0, w: pl.BlockSpec((nb, C, w), lambda b, d, s: (b, chunk(d, s), c0 // w))
    dpar = lambda r: pl.BlockSpec((1, r, W), lambda b, d, s: (d, 0, 0))
    par = pl.BlockSpec((1, W), lambda b, d, s: (0, 0))
    oblk = pl.BlockSpec((1, nb, C, W), lambda b, d, s: (d, b, ochunk(d, s), 0))
    G = 2 * RWKV_HEAD_DIM
    return pl.pallas_call(
        functools.partial(_rwkv_kernel, nc=nc, nb=nb, nsub=nsub),
        grid=(B // nb, 2, nc + nl),
        in_specs=[zblk(C_RWK, W), zblk(C_RWV, W), zblk(C_RWR, W), zblk(C_LORA, 256),
                  dpar(1), dpar(1), dpar(G), dpar(G), par, par, par],
        out_specs=[oblk, oblk],
        out_shape=[jax.ShapeDtypeStruct((2, B, T, W), f32), jax.ShapeDtypeStruct((2, B, T, W), f32)],
        scratch_shapes=[pltpu.VMEM((nb * RWKV_PAIRS, G, G), f32)],
        compiler_params=_params(("arbitrary", "arbitrary", "arbitrary")),
        name="rwkv",
    )(z, z, z, z, w0, a0, w2p, a2p, k_k, k_a, r_k)


def _merge_kernel(x_ref, ret_ref, rw0_ref, rw1_ref, b0_ref, b1_ref, gate_ref, retg_ref, gd_ref, mod_ref,
                  ng_ref, gn_ref, lng_ref, lnb_ref, g2_ref, wbr_ref, wbw_ref, wout_ref, wrh_ref, wrl_ref,
                  x1_ref, h2_ref, lt_ref):
    hd = RET_HEAD_DIM
    ret = ret_ref[0]
    parts = []
    for hh in range(RET_HEADS):
        xh = ret[:, hh * hd:(hh + 1) * hd]
        mu = jnp.mean(xh, axis=-1, keepdims=True)
        dv = xh - mu
        var = jnp.mean(dv * dv, axis=-1, keepdims=True)
        parts.append(dv * lax.rsqrt(var + RET_EPS))
    yr = retg_ref[0].astype(f32) * (jnp.concatenate(parts, axis=1) * gn_ref[...])
    y_ret = _bdot(yr, wbr_ref[...])
    W = RWKV_WIDTH
    gi = lax.broadcasted_iota(jnp.int32, (W, W), 0)
    gj = lax.broadcasted_iota(jnp.int32, (W, W), 1)
    ones_bd = ((gi >> 6) == (gj >> 6)).astype(bf16)
    o = rw0_ref[0, 0] + rw1_ref[0, 0]
    mu = _split_dot(o, ones_bd) * (1.0 / RWKV_HEAD_DIM)
    dv = o - mu
    var = _bdot(dv * dv, ones_bd) * (1.0 / RWKV_HEAD_DIM)
    yw = dv * lax.rsqrt(var + RWKV_EPS) * lng_ref[...] + lnb_ref[...]
    gate = _dot(gd_ref[0], g2_ref[...])
    yw = (yw + b0_ref[0, 0] + b1_ref[0, 0]) * gate
    y_rw = _bdot(yw, wbw_ref[...])
    D = y_ret.shape[1]
    g = gate_ref[0].astype(f32)
    m = g[:, :D] * y_ret + g[:, D:] * y_rw
    y = _bdot(m, wout_ref[...])

    def rms(v, gg):
        return v * lax.rsqrt(jnp.mean(v * v, axis=-1, keepdims=True) + NORM_EPS) * gg

    x1 = x_ref[0] + mod_ref[0, 0:1, :] * rms(y, ng_ref[0:1, :])
    x1_ref[0] = x1
    h2 = rms(x1, ng_ref[1:2, :]) * (1.0 + mod_ref[0, 2:3, :]) + mod_ref[0, 1:2, :]
    h2b = h2.astype(bf16)
    h2_ref[0] = h2b
    h2l = (h2 - h2b.astype(f32)).astype(bf16)
    lgt = _dot(h2b, wrh_ref[...]) + _dot(h2l, wrh_ref[...]) + _dot(h2b, wrl_ref[...])
    lt_ref[0] = lgt.T[0:N_EXPERTS, :]


def _merge(x, ret_o, rw_o, bonus, z, mod2, ng12, gn, lng, lnb, g2, wbr, wbw, wout, wrh, wrl, ctx):
    B, T, D = x.shape
    tm = MERGE_ROWS if T % MERGE_ROWS == 0 else ROWS
    co = 0
    W = RWKV_WIDTH
    row = lambda w: pl.BlockSpec((1, tm, w), lambda b, i: (b, i, 0))
    dblk = lambda dd: pl.BlockSpec((1, 1, tm, W), lambda b, i: (dd, b, i, 0))
    zblk = lambda c0, w: pl.BlockSpec((1, tm, w), lambda b, i: (b, co + i, c0 // w))
    full = lambda a: pl.BlockSpec(a.shape, lambda b, i: (0,) * a.ndim)
    return pl.pallas_call(
        _merge_kernel,
        grid=(B, T // tm),
        in_specs=[row(D), row(W), dblk(0), dblk(1), dblk(0), dblk(1),
                  zblk(C_MERGE, 2 * D), zblk(C_RETG, W), zblk(C_LORA + 128, 128),
                  pl.BlockSpec((1, 3, D), lambda b, i: (b, 0, 0)),
                  full(ng12), full(gn), full(lng), full(lnb), full(g2), full(wbr), full(wbw), full(wout),
                  full(wrh), full(wrl)],
        out_specs=[row(D), row(D), pl.BlockSpec((1, N_EXPERTS, tm), lambda b, i: (b, 0, i))],
        out_shape=[jax.ShapeDtypeStruct((B, T, D), f32), jax.ShapeDtypeStruct((B, T, D), bf16),
                   jax.ShapeDtypeStruct((B, N_EXPERTS, T), f32)],
        compiler_params=_params(("arbitrary", "arbitrary")),
        name="merge",
    )(x, ret_o, rw_o, rw_o, bonus, bonus, z, z, z, mod2, ng12, gn, lng, lnb, g2, wbr, wbw, wout, wrh, wrl)


def _route_kernel(lt_ref, slot_ref, rt_ref, cnt_ref, slotf_ref, gate_ref, *, cap):
    B, NE, T = lt_ref.shape
    lg = lt_ref[...]
    mx = jnp.max(lg, axis=1, keepdims=True)
    ex = jnp.exp(lg - mx)
    aff = (ex / jnp.sum(ex, axis=1, keepdims=True)).reshape(B * NE, T)
    E = B * NE

    def count_ge(cand):
        return jnp.sum((aff >= cand).astype(f32), axis=1, keepdims=True)

    def exp_step(_, kk):
        k_lo, k_hi = kk
        km = jnp.floor((k_lo + k_hi) * 0.5)
        ok = count_ge(jnp.exp2(-km)) >= cap
        return jnp.where(ok, k_lo, km), jnp.where(ok, km, k_hi)

    k_lo, k_hi = lax.fori_loop(0, ROUTE_EXP_STEPS, exp_step,
                               (jnp.full((E, 1), -1.0, f32), jnp.full((E, 1), ROUTE_MAX_EXP, f32)))
    lo0 = jnp.where(k_hi >= ROUTE_MAX_EXP, 0.0, jnp.exp2(-k_hi))
    hi0 = jnp.exp2(-k_lo)

    def val_step(_, lh):
        lo, hi = lh
        mid = (lo + hi) * 0.5
        ok = count_ge(mid) >= cap
        return jnp.where(ok, mid, lo), jnp.where(ok, hi, mid)

    lo, hi = lax.fori_loop(0, ROUTE_VAL_STEPS, val_step, (lo0, hi0))
    gt = aff >= hi
    eq = (aff >= lo) & (aff < hi)
    need = cap - jnp.sum(gt.astype(f32), axis=1, keepdims=True)
    tri = (lax.broadcasted_iota(jnp.int32, (T, T), 0) < lax.broadcasted_iota(jnp.int32, (T, T), 1)).astype(bf16)
    eq_before = _dot(eq.astype(bf16), tri)
    sel = gt | (eq & (eq_before < need))
    slot = _dot(sel.astype(bf16), tri)
    slot_f = jnp.where(sel, slot, -1.0)
    slot_ref[...] = slot_f.astype(jnp.int32).reshape(B, NE, T)
    before = (lax.broadcasted_iota(jnp.int32, (T, 128), 0)
              < lax.broadcasted_iota(jnp.int32, (T, 128), 1) * GATHER_TILE).astype(bf16)
    cnt_ref[...] = _dot(sel.astype(bf16), before).astype(jnp.int32).reshape(B, NE, 128)
    slotf_ref[...] = slot_f.reshape(B, NE, T)
    gate_ref[...] = jnp.where(sel, aff, 0.0).reshape(B, NE, T)

    def transpose_sample(b, carry):
        packed = jnp.concatenate([slotf_ref[b], gate_ref[b], jnp.zeros((128 - 2 * NE, T), f32)], axis=0)
        rt_ref[b] = packed.T.astype(bf16)
        return carry

    lax.fori_loop(0, B, transpose_sample, 0)


def _route(lt, cap):
    B, E, T = lt.shape
    assert cap <= 256 and T % GATHER_TILE == 0 and T // GATHER_TILE < 128
    return pl.pallas_call(
        functools.partial(_route_kernel, cap=cap),
        grid=(1,),
        in_specs=[pl.BlockSpec((B, E, T), lambda i: (0, 0, 0))],
        out_specs=[pl.BlockSpec((B, E, T), lambda i: (0, 0, 0)), pl.BlockSpec((B, T, 128), lambda i: (0, 0, 0)),
                   pl.BlockSpec((B, E, 128), lambda i: (0, 0, 0))],
        out_shape=[jax.ShapeDtypeStruct((B, E, T), jnp.int32), jax.ShapeDtypeStruct((B, T, 128), bf16),
                   jax.ShapeDtypeStruct((B, E, 128), jnp.int32)],
        scratch_shapes=[pltpu.VMEM((B, E, T), f32), pltpu.VMEM((B, E, T), f32)],
        compiler_params=_params(("arbitrary",)),
        name="route",
    )(lt)


def _ffn_kernel(cnt_ref, slot_ref, h_ref, wg_ref, wu_ref, wd_ref, o_ref, wgb_ref, wub_ref, wdb_ref, xg_ref, *, cap):
    T = h_ref.shape[1]
    e = pl.program_id(0)

    @pl.when(pl.program_id(1) == 0)
    def _():
        wgb_ref[...] = wg_ref[0].astype(bf16)
        wub_ref[...] = wu_ref[0].astype(bf16)
        wdb_ref[...] = wd_ref[0].astype(bf16)

    nb = h_ref.shape[0]
    GT, GW, GA = GATHER_TILE, GATHER_WINDOW, GATHER_ALIGN
    nt = T // GT
    for bb in range(nb):
        b = pl.program_id(1) * nb + bb
        base = [pl.multiple_of(lax.shift_left(lax.shift_right_logical(cnt_ref[b, e, j], GATHER_ALIGN_LOG2),
                                              GATHER_ALIGN_LOG2), GA) for j in range(nt)]
        fits = cnt_ref[b, e, 1] - base[0] <= GW
        for j in range(1, nt):
            fits = fits & (cnt_ref[b, e, j + 1] - base[j] <= GW)

        @pl.when(fits)
        def _():
            xg_ref[bb] = jnp.zeros(xg_ref.shape[1:], f32)
            for j in range(nt):
                rows = base[j] + lax.broadcasted_iota(jnp.int32, (GW, GT), 0)
                onehot = (slot_ref[bb, 0, j:j + 1, :] == rows).astype(bf16)
                xg_ref[bb, pl.ds(base[j], GW), :] += _dot(onehot, h_ref[bb, j * GT:(j + 1) * GT, :])

        @pl.when(jnp.logical_not(fits))
        def _():
            rows = lax.broadcasted_iota(jnp.int32, (cap, GT), 0)
            acc = jnp.zeros((cap, h_ref.shape[2]), f32)
            for j in range(nt):
                onehot = (slot_ref[bb, 0, j:j + 1, :] == rows).astype(bf16)
                acc = acc + _dot(onehot, h_ref[bb, j * GT:(j + 1) * GT, :])
            xg_ref[bb, 0:cap, :] = acc

    xg = jnp.concatenate([xg_ref[bb, 0:cap, :].astype(bf16) for bb in range(nb)], axis=0)
    hg = _dot(xg, wgb_ref[...])
    hu = _dot(xg, wub_ref[...])
    hid = (hg * _sigmoid(hg) * hu).astype(bf16)
    out = _dot(hid, wdb_ref[...]).astype(bf16)
    for bb in range(nb):
        o_ref[bb, 0] = out[bb * cap:(bb + 1) * cap]


def _ffn(cnt, slot4, h2, wg, wu, wd, cap):
    B, T, D = h2.shape
    E, _, F = wg.shape
    nb = FFN_SAMPLES_PER_STEP if B % FFN_SAMPLES_PER_STEP == 0 else 1
    grid_spec = pltpu.PrefetchScalarGridSpec(
        num_scalar_prefetch=1,
        grid=(E, B // nb),
        in_specs=[pl.BlockSpec((nb, 1, T // GATHER_TILE, GATHER_TILE), lambda e, b, c: (b, e, 0, 0)),
                  pl.BlockSpec((nb, T, D), lambda e, b, c: (b, 0, 0)),
                  pl.BlockSpec((1, D, F), lambda e, b, c: (e, 0, 0)),
                  pl.BlockSpec((1, D, F), lambda e, b, c: (e, 0, 0)),
                  pl.BlockSpec((1, F, D), lambda e, b, c: (e, 0, 0))],
        out_specs=pl.BlockSpec((nb, 1, cap, D), lambda e, b, c: (b, e, 0, 0)),
        scratch_shapes=[pltpu.VMEM((D, F), bf16), pltpu.VMEM((D, F), bf16), pltpu.VMEM((F, D), bf16),
                        pltpu.VMEM((nb, cap + GATHER_WINDOW, D), f32)])
    return pl.pallas_call(
        functools.partial(_ffn_kernel, cap=cap),
        grid_spec=grid_spec,
        out_shape=jax.ShapeDtypeStruct((B, E, cap, D), bf16),
        compiler_params=_params(("arbitrary", "arbitrary"), FFN_VMEM_LIMIT),
        name="ffn",
    )(cnt, slot4, h2, wg, wu, wd)


def _combine_kernel(cnt_ref, rt_ref, eo_ref, x1_ref, mod_ref, ng_ref, o_ref, y_ref, *, cap):
    b = pl.program_id(0)
    i = pl.program_id(1)
    rt = rt_ref[0].astype(f32)
    tm = rt.shape[0]
    E = eo_ref.shape[1]
    W = min(SCATTER_WINDOW, cap)
    GS = SCATTER_GROUP
    ST = GATHER_TILE
    for st in range(tm // ST):
        t = i * (tm // ST) + st
        r = rt[st * ST:(st + 1) * ST]
        base, fits = [], None
        for e in range(E):
            lo = cnt_ref[b, e, t]
            be = jnp.minimum(lax.shift_left(lax.shift_right_logical(lo, GATHER_ALIGN_LOG2), GATHER_ALIGN_LOG2),
                             cap - W)
            base.append(pl.multiple_of(be, GATHER_ALIGN))
            ok = cnt_ref[b, e, t + 1] - be <= W
            fits = ok if fits is None else fits & ok

        def scatter_matrix(e, first, width, r=r):
            cols = (first + lax.broadcasted_iota(jnp.int32, (ST, width), 1)).astype(f32)
            return jnp.where(r[:, e:e + 1] == cols, r[:, E + e:E + e + 1], 0.0).astype(bf16)

        @pl.when(fits)
        def _():
            y = jnp.zeros((ST, y_ref.shape[1]), f32)
            for g in range(0, E, GS):
                p = jnp.concatenate([scatter_matrix(e, base[e], W) for e in range(g, g + GS)], axis=1)
                rows = jnp.concatenate([eo_ref[0, e, pl.ds(base[e], W), :] for e in range(g, g + GS)], axis=0)
                y = y + _dot(p, rows)
            y_ref[st * ST:(st + 1) * ST, :] = y

        @pl.when(jnp.logical_not(fits))
        def _():
            y = jnp.zeros((ST, y_ref.shape[1]), f32)
            for e in range(E):
                y = y + _dot(scatter_matrix(e, 0, cap), eo_ref[0, e])
            y_ref[st * ST:(st + 1) * ST, :] = y

    y = y_ref[...]
    yn = y * lax.rsqrt(jnp.mean(y * y, axis=-1, keepdims=True) + NORM_EPS) * ng_ref[...]
    o_ref[0] = x1_ref[0] + mod_ref[0] * yn


def _combine(cnt, rt, eo, x1, g2mod, ng3, cap):
    B, T, D = x1.shape
    E = eo.shape[1]
    tm = 512 if T % 512 == 0 else T
    assert E % SCATTER_GROUP == 0 and tm % GATHER_TILE == 0
    grid_spec = pltpu.PrefetchScalarGridSpec(
        num_scalar_prefetch=1,
        grid=(B, T // tm),
        in_specs=[pl.BlockSpec((1, tm, 128), lambda b, i, c: (b, i, 0)),
                  pl.BlockSpec((1, E, cap, D), lambda b, i, c: (b, 0, 0, 0)),
                  pl.BlockSpec((1, tm, D), lambda b, i, c: (b, i, 0)),
                  pl.BlockSpec((1, 1, D), lambda b, i, c: (b, 0, 0)),
                  pl.BlockSpec((1, D), lambda b, i, c: (0, 0))],
        out_specs=pl.BlockSpec((1, tm, D), lambda b, i, c: (b, i, 0)),
        scratch_shapes=[pltpu.VMEM((tm, D), f32)])
    return pl.pallas_call(
        functools.partial(_combine_kernel, cap=cap),
        grid_spec=grid_spec,
        out_shape=jax.ShapeDtypeStruct((B, T, D), f32),
        compiler_params=_params(("arbitrary", "arbitrary")),
        name="combine",
    )(cnt, rt, eo, x1, g2mod, ng3)


def _permute_columns(w):
    sk, sv, rk, rv, wd, ad = 0, 512, 1024, 1536, 2048, 2112
    q0 = 2176
    rq, rg, rr, gd, mg = q0, q0 + 512, q0 + 1024, q0 + 1536, q0 + 1664
    order = [(mg, 2048), (sk, 512), (sv, 512), (rq, 512), (rg, 512), (rk, 512), (rv, 512), (rr, 512),
             (wd, 64), (ad, 64), (gd, 128)]
    parts = [w[:, a:a + n] for a, n in order]
    parts.append(jnp.zeros((w.shape[0], IN_COLS - USED_COLS), w.dtype))
    return jnp.concatenate(parts, axis=1)


def _rope_tables(T, CT):
    t = jnp.arange(T)
    nfreq = RET_HEAD_DIM // 4
    inv = ROPE_BASE ** (-jnp.arange(nfreq, dtype=f32) / nfreq)
    ang = jnp.concatenate([(t // GRID_W).astype(f32)[:, None] * inv,
                           (t % GRID_W).astype(f32)[:, None] * inv], axis=-1)
    cos, sin = jnp.cos(ang), jnp.sin(ang)
    cosf = jnp.concatenate([cos, cos], axis=1)
    sinf = jnp.concatenate([-sin, sin], axis=1)
    return (jnp.concatenate([cosf, jnp.ones((CT, RET_HEAD_DIM), f32)], axis=0),
            jnp.concatenate([sinf, jnp.zeros((CT, RET_HEAD_DIM), f32)], axis=0))


def kernel(x, c, ctx, c_ctx, w_mod, b_mod, norm_g, w_in, ret_log_decay, ret_gn_g, rwkv_mu, rwkv_k_k, rwkv_k_a,
           rwkv_r_k, rwkv_w0, rwkv_w2, rwkv_a0, rwkv_a2, rwkv_g2, rwkv_ln_g, rwkv_ln_b, w_br_ret, w_br_rwkv,
           w_out, w_router, w_gate, w_up, w_down):
    B, T, D = x.shape
    CT = ctx.shape[1]
    assert w_mod.shape[0] == 1 and D == D_MODEL
    assert CT % ROWS == 0 and T % ROWS == 0 and T % GRID_W == 0
    cap = CAPACITY_FACTOR * T // N_EXPERTS
    assert cap % 8 == 0

    mrows = -(-(B + 1) // 8) * 8
    cc = jnp.zeros((mrows, D), f32).at[:B].set(c).at[B].set(c_ctx)
    mod = _modulation(cc, w_mod[0], b_mod[0])
    lat = mod[:B].reshape(B, N_MOD, D)
    cm = jnp.broadcast_to(mod[B].reshape(1, N_MOD, D), (B, N_MOD, D))
    modrows = jnp.concatenate([lat[:, 0:2], cm[:, 0:2]], axis=1)

    w_perm = _permute_columns(w_in[0]).astype(bf16)
    mu = rwkv_mu[0]
    ss = 2 * RWKV_WIDTH + DECAY_LORA + ICLR_LORA
    mu_full = jnp.zeros((2, IN_COLS), f32)
    mu_full = mu_full.at[:, C_RWK:C_RWK + 1024].set(mu[:, 0:1024])
    mu_full = mu_full.at[:, C_RWR:C_RWR + 512].set(mu[:, ss:ss + 512])
    mu_full = mu_full.at[:, C_LORA:C_LORA + 128].set(mu[:, 1024:ss])
    mu_full = mu_full.at[:, C_LORA + 128:USED_COLS].set(mu[:, ss + 512:])
    cosf, sinf = _rope_tables(T, CT)
    z = _inproj(x, ctx, modrows, norm_g[0, 0:1], w_perm, mu_full, cosf, sinf)

    lg = -jnp.exp(ret_log_decay[0].astype(f32))
    ret_o = _retention(lg, z, CT)

    G = 2 * RWKV_HEAD_DIM
    w2p = jnp.zeros((2, G, RWKV_WIDTH), f32).at[:, :DECAY_LORA].set(rwkv_w2[0]).astype(bf16)
    a2p = jnp.zeros((2, G, RWKV_WIDTH), f32).at[:, DECAY_LORA:].set(rwkv_a2[0]).astype(bf16)
    rw_o, bonus = _rwkv(z, rwkv_w0[0][:, None, :], rwkv_a0[0][:, None, :], w2p, a2p,
                        rwkv_k_k[0][None], rwkv_k_a[0][None], rwkv_r_k[0][None], CT)

    mod2 = jnp.stack([lat[:, 2], lat[:, 3], lat[:, 4]], axis=1)
    wr_pad = jnp.zeros((D, 128), f32).at[:, :N_EXPERTS].set(w_router[0])
    wr_hi = wr_pad.astype(bf16)
    wr_lo = (wr_pad - wr_hi.astype(f32)).astype(bf16)
    x1, h2, lt = _merge(x, ret_o, rw_o, bonus, z, mod2, norm_g[0, 1:3], ret_gn_g[0][None], rwkv_ln_g[0][None],
                        rwkv_ln_b[0][None], rwkv_g2[0].astype(bf16), w_br_ret[0].astype(bf16),
                        w_br_rwkv[0].astype(bf16), w_out[0].astype(bf16), wr_hi, wr_lo, CT)

    slot, rt, cnt = _route(lt, cap)
    cnt = cnt[:, :, :T // GATHER_TILE + 1]
    eo = _ffn(cnt, slot.reshape(B, N_EXPERTS, T // GATHER_TILE, GATHER_TILE), h2, w_gate[0], w_up[0], w_down[0], cap)
    return _combine(cnt, rt, eo, x1, lat[:, 5:6], norm_g[0, 3:4], cap)
```

```python
import functools

import jax
import jax.numpy as jnp
from jax import lax
from jax.experimental import pallas as pl
from jax.experimental.pallas import tpu as pltpu

f32 = jnp.float32
bf16 = jnp.bfloat16

D_MODEL = 1024
GRID_W = 64
RET_HEAD_DIM = 128
RET_WIDTH = 512
RET_HEADS = 4
RET_CHUNK = 128
RET_EPS = 1e-5
ROPE_BASE = 10000.0
RWKV_HEAD_DIM = 64
RWKV_WIDTH = 512
RWKV_PAIRS = 4
RWKV_CHUNK = 64
RWKV_CHUNKS_PER_STEP = 4
RWKV_SAMPLES_PER_STEP = 8
DECAY_LORA = 64
ICLR_LORA = 64
GATE_LORA = 128
RWKV_EPS = 64e-5
N_EXPERTS = 16
EXPERT_FF = 1024
CAPACITY_FACTOR = 2
N_MOD = 6
NORM_EPS = 1e-6

C_MERGE, C_RETK, C_RETV, C_RETQ, C_RETG = 0, 2048, 2560, 3072, 3584
C_RWK, C_RWV, C_RWR, C_LORA = 4096, 4608, 5120, 5632
USED_COLS = 5888
TN = 512
IN_COLS = -(-USED_COLS // TN) * TN
ROWS = 256
STEP_ROWS = 768
POST_ROWS = 128
MERGE_ROWS = 512

ROUTE_MAX_EXP = 126.0
ROUTE_EXP_STEPS = 8
ROUTE_VAL_STEPS = 26

VMEM_LIMIT = 56 * 1024 * 1024
FFN_SAMPLES_PER_STEP = 2
GATHER_TILE = 256
GATHER_ALIGN_LOG2 = 4
GATHER_ALIGN = 1 << GATHER_ALIGN_LOG2
GATHER_WINDOW = 80
SCATTER_WINDOW = 64
SCATTER_GROUP = 4
FFN_VMEM_LIMIT = 62 * 1024 * 1024


def _dot(a, b):
    return jnp.dot(a, b, preferred_element_type=f32)


def _dot_nt(a, b):
    return lax.dot_general(a, b, (((1,), (1,)), ((), ())), preferred_element_type=f32)


def _dot_tn(a, b):
    return lax.dot_general(a, b, (((0,), (0,)), ((), ())), preferred_element_type=f32)


def _bdot(a, b):
    return _dot(a.astype(bf16), b.astype(bf16))


def _split_dot(x, w):
    hi = x.astype(bf16)
    lo = (x - hi.astype(f32)).astype(bf16)
    return _dot(hi, w) + _dot(lo, w)


def _sigmoid(x):
    return 0.5 * jnp.tanh(0.5 * x) + 0.5


def _params(sem, limit=VMEM_LIMIT):
    return pltpu.CompilerParams(dimension_semantics=sem, vmem_limit_bytes=limit)


def _mod_kernel(c_ref, w_ref, b_ref, o_ref):
    c = c_ref[...]
    s = c * jax.nn.sigmoid(c)
    o_ref[...] = _bdot(s, w_ref[...]) + b_ref[...]


def _modulation(cc, w_mod, b_mod):
    m, d = cc.shape
    n = w_mod.shape[1]
    tn = 512
    return pl.pallas_call(
        _mod_kernel,
        grid=(n // tn,),
        in_specs=[pl.BlockSpec((m, d), lambda j: (0, 0)),
                  pl.BlockSpec((d, tn), lambda j: (0, j)),
                  pl.BlockSpec((1, tn), lambda j: (0, j))],
        out_specs=pl.BlockSpec((m, tn), lambda j: (0, j)),
        out_shape=jax.ShapeDtypeStruct((m, n), f32),
        compiler_params=_params(("arbitrary",)),
        name="mod",
    )(cc, w_mod, b_mod.reshape(1, n))


def _inproj_kernel(x_ref, c_ref, mod_ref, g_ref, w_ref, mu_ref, cos_ref, sin_ref, o_ref, h_ref, z_ref):
    n = pl.program_id(1)
    T = x_ref.shape[1]
    L = T + c_ref.shape[1]
    nlat = T // ROWS
    nchunk = L // ROWS
    PAD = 8

    SR = STEP_ROWS if L % STEP_ROWS == 0 else ROWS

    @pl.when(n == 0)
    def _():
        z_ref[0:PAD, :] = jnp.zeros((PAD, TN), f32)
        z_ref[PAD + L:PAD + L + PAD, :] = jnp.zeros((PAD, TN), f32)

        def norm_chunk(src_ref, s0, r0, o):
            xb = src_ref[0, pl.ds(s0, ROWS), :]
            y = xb * lax.rsqrt(jnp.mean(xb * xb, axis=-1, keepdims=True) + NORM_EPS) * g_ref[...]
            sh = mod_ref[0, o:o + 1, :]
            sc = mod_ref[0, o + 1:o + 2, :]
            h_ref[pl.ds(r0, ROWS), :] = (y * (1.0 + sc) + sh).astype(bf16)

        def lat_body(i, carry):
            r0 = pl.multiple_of(i * ROWS, ROWS)
            norm_chunk(x_ref, r0, r0, 0)
            return carry

        lax.fori_loop(0, nlat, lat_body, 0)
        for j in range(nlat, nchunk):
            norm_chunk(c_ref, j * ROWS - T, j * ROWS, 2)

    PR = POST_ROWS

    def rope(scale):
        def post(z, r0):
            if scale != 1.0:
                z = z * scale
            cs = cos_ref[pl.ds(r0, SR), :]
            sn = sin_ref[pl.ds(r0, SR), :]
            parts = []
            for hh in range(TN // RET_HEAD_DIM):
                zh = z[:, hh * RET_HEAD_DIM:(hh + 1) * RET_HEAD_DIM]
                parts.append(zh * cs + pltpu.roll(zh, RET_HEAD_DIM // 2, 1) * sn)
            return jnp.concatenate(parts, axis=1)
        return post

    def shifted(r0, at_boundary):
        win = z_ref[pl.ds(r0, PR + 2 * PAD), :]
        prev = win[PAD - 1:PAD - 1 + PR]
        z = win[PAD:PAD + PR]
        nxt = win[PAD + 1:PAD + 1 + PR]
        if at_boundary:
            rid = r0 + lax.broadcasted_iota(jnp.int32, (PR, 1), 0)
            prev = jnp.where(rid == T, 0.0, prev)
            nxt = jnp.where(rid == T - 1, 0.0, nxt)
        mu0 = mu_ref[0:1, :]
        mu1 = mu_ref[1:2, :]
        return (1.0 - mu0 - mu1) * z + mu0 * prev + mu1 * nxt

    def lora_act(zs):
        lane = lax.broadcasted_iota(jnp.int32, (1, TN), 1)
        return jnp.where(lane < DECAY_LORA, jnp.tanh(zs),
                         jnp.where(lane < DECAY_LORA + ICLR_LORA, zs, jax.nn.sigmoid(zs)))

    def product(r0):
        return _dot(h_ref[pl.ds(r0, SR), :], w_ref[...])

    def direct(post):
        def body(i, carry):
            r0 = pl.multiple_of(i * SR, SR)
            o_ref[0, pl.ds(r0, SR), :] = post(product(r0), r0).astype(bf16)
            return carry
        lax.fori_loop(0, L // SR, body, 0)

    def via_buffer(act):
        def mm_body(i, carry):
            r0 = pl.multiple_of(i * SR, SR)
            z_ref[pl.ds(PAD + r0, SR), :] = product(r0)
            return carry
        lax.fori_loop(0, L // SR, mm_body, 0)

        def post(r0, at_boundary):
            o_ref[0, pl.ds(r0, PR), :] = act(shifted(r0, at_boundary)).astype(bf16)

        def post_body(i, carry):
            post(pl.multiple_of(i * PR, PR), False)
            return carry

        seam = T // PR
        lax.fori_loop(0, seam - 1, post_body, 0)
        post((seam - 1) * PR, True)
        post(seam * PR, True)
        lax.fori_loop(seam + 1, L // PR, post_body, 0)

    t = lambda c: c // TN
    pl.when(n < t(C_RETK))(lambda: direct(lambda z, r0: _sigmoid(z)))
    pl.when((n >= t(C_RETK)) & (n < t(C_RETV)))(lambda: direct(rope(RET_HEAD_DIM ** -0.5)))
    pl.when((n >= t(C_RETV)) & (n < t(C_RETQ)))(lambda: direct(lambda z, r0: z))
    pl.when((n >= t(C_RETQ)) & (n < t(C_RETG)))(lambda: direct(rope(1.0)))
    pl.when((n >= t(C_RETG)) & (n < t(C_RWK)))(lambda: direct(lambda z, r0: z * _sigmoid(z)))
    pl.when((n >= t(C_RWK)) & (n < t(C_LORA)))(lambda: via_buffer(lambda zs: zs))
    pl.when(n >= t(C_LORA))(lambda: via_buffer(lora_act))


def _inproj(x, ctx, modrows, g0, w_perm, mu_full, cosf, sinf):
    B, T, D = x.shape
    CT = ctx.shape[1]
    L = T + CT
    nt = IN_COLS // TN
    return pl.pallas_call(
        _inproj_kernel,
        grid=(B, nt),
        in_specs=[pl.BlockSpec((1, T, D), lambda b, n: (b, 0, 0)),
                  pl.BlockSpec((1, CT, D), lambda b, n: (b, 0, 0)),
                  pl.BlockSpec((1, 4, D), lambda b, n: (b, 0, 0)),
                  pl.BlockSpec((1, D), lambda b, n: (0, 0)),
                  pl.BlockSpec((D, TN), lambda b, n: (0, n)),
                  pl.BlockSpec((2, TN), lambda b, n: (0, n)),
                  pl.BlockSpec((L, RET_HEAD_DIM), lambda b, n: (0, 0)),
                  pl.BlockSpec((L, RET_HEAD_DIM), lambda b, n: (0, 0))],
        out_specs=pl.BlockSpec((1, L, TN), lambda b, n: (b, 0, n)),
        out_shape=jax.ShapeDtypeStruct((B, L, IN_COLS), bf16),
        scratch_shapes=[pltpu.VMEM((L, D), bf16), pltpu.VMEM((L + 16, TN), f32)],
        compiler_params=_params(("arbitrary", "arbitrary")),
        name="inproj",
    )(x, ctx, modrows, g0, w_perm, mu_full, cosf, sinf)


def _ret_kernel(lg_ref, q_ref, k_ref, v_ref, o_ref, r_ref, tab_ref, *, ctx):
    L = q_ref.shape[1]
    Cc = RET_CHUNK
    hd = RET_HEAD_DIM
    lat = L - ctx
    nc = ctx // Cc
    nl = lat // Cc
    combos = [(h, d) for h in range(RET_HEADS) for d in (0, 1)]
    INTRA, CROSS, TAIL, DECAY = 0, 1, 2, 3

    @pl.when(pl.program_id(0) == 0)
    def _():
        ii = lax.broadcasted_iota(jnp.int32, (Cc, Cc), 0).astype(f32)
        jj = lax.broadcasted_iota(jnp.int32, (Cc, Cc), 1).astype(f32)
        for idx, (h, d) in enumerate(combos):
            lg = lg_ref[d, h]
            if d == 0:
                diff = ii - jj
                cross = jnp.exp(lg * (ii + 1.0))
                tailw = jnp.exp(lg * (Cc - 1.0 - ii))
            else:
                diff = jj - ii
                cross = jnp.exp(lg * (Cc - ii))
                tailw = jnp.exp(lg * ii)
            tab_ref[idx, INTRA] = jnp.where(diff >= 0, jnp.exp(lg * jnp.maximum(diff, 0.0)), 0.0)
            tab_ref[idx, CROSS] = cross
            tab_ref[idx, TAIL] = tailw
            tab_ref[idx, DECAY] = jnp.exp(jnp.zeros((Cc, Cc), f32) + lg * Cc)

    r_ref[...] = jnp.zeros(r_ref.shape, f32)
    o_ref[...] = jnp.zeros(o_ref.shape, f32)

    def cols(h):
        return slice(h * hd, (h + 1) * hd)

    def update(idx, kc, vc):
        ks = (kc.astype(f32) * tab_ref[idx, TAIL]).astype(bf16)
        r_ref[idx] = r_ref[idx] * tab_ref[idx, DECAY] + _dot_tn(ks, vc)

    def ctx_step(s, carry):
        for idx, (h, d) in enumerate(combos):
            row0 = pl.multiple_of(lat + (s * Cc if d == 0 else (nc - 1 - s) * Cc), Cc)
            update(idx, k_ref[0, pl.ds(row0, Cc), cols(h)], v_ref[0, pl.ds(row0, Cc), cols(h)])
        return carry

    def lat_step(s, carry):
        t0s, qs, ks, vs = [], [], [], []
        for h, d in combos:
            t0 = pl.multiple_of(s * Cc if d == 0 else (nl - 1 - s) * Cc, Cc)
            row0 = t0
            t0s.append(t0)
            qs.append(q_ref[0, pl.ds(row0, Cc), cols(h)])
            ks.append(k_ref[0, pl.ds(row0, Cc), cols(h)])
            vs.append(v_ref[0, pl.ds(row0, Cc), cols(h)])
        n = len(combos)
        sc = [(_dot_nt(qs[i], ks[i]) * tab_ref[i, INTRA]).astype(bf16) for i in range(n)]
        oc = [_dot(qs[i], r_ref[i].astype(bf16)) * tab_ref[i, CROSS] for i in range(n)]
        oi = [_dot(sc[i], vs[i]) for i in range(n)]
        for i, (h, d) in enumerate(combos):
            o_ref[0, pl.ds(t0s[i], Cc), cols(h)] += oi[i] + oc[i]
        for i in range(n):
            update(i, ks[i], vs[i])
        return carry

    lax.fori_loop(0, nc, ctx_step, 0)
    lax.fori_loop(0, nl, lat_step, 0)


def _retention(lg, z, ctx):
    B, L, _ = z.shape
    T = L - ctx
    W = RET_WIDTH
    Cc = RET_CHUNK
    blk = lambda c0: pl.BlockSpec((1, L, W), lambda b: (b, 0, c0 // W))
    nchain = 2 * RET_HEADS
    return pl.pallas_call(
        functools.partial(_ret_kernel, ctx=ctx),
        grid=(B,),
        in_specs=[pl.BlockSpec(memory_space=pltpu.SMEM), blk(C_RETQ), blk(C_RETK), blk(C_RETV)],
        out_specs=pl.BlockSpec((1, T, W), lambda b: (b, 0, 0)),
        out_shape=jax.ShapeDtypeStruct((B, T, W), f32),
        scratch_shapes=[pltpu.VMEM((nchain, Cc, Cc), f32), pltpu.VMEM((nchain, 4, Cc, Cc), f32)],
        compiler_params=_params(("arbitrary",)),
        name="ret",
    )(lg, z, z, z)


def _rwkv_kernel(k_ref, v_ref, r_ref, lo_ref, w0_ref, a0_ref, w2_ref, a2_ref, kk_ref, ka_ref, rk_ref,
                 o_ref, bo_ref, s_ref, *, nc, nb, nsub):
    d = pl.program_id(1)
    s = pl.program_id(2)
    C = RWKV_CHUNK

    @pl.when(s == 0)
    def _():
        s_ref[...] = jnp.zeros(s_ref.shape, f32)

    def sub_chunk(j, carry):
        r0 = pl.multiple_of(jnp.where(d == 1, nsub - 1 - j, j) * C, C)
        rows = lambda ref: ref.at[:, pl.ds(r0, C), :]
        _rwkv_chunk(rows(k_ref), rows(v_ref), rows(r_ref), rows(lo_ref), w0_ref, a0_ref, w2_ref, a2_ref,
                    kk_ref, ka_ref, rk_ref, o_ref.at[:, :, pl.ds(r0, C), :], bo_ref.at[:, :, pl.ds(r0, C), :],
                    s_ref, rev=d == 1, emit=s >= nc, nb=nb)
        return carry

    lax.fori_loop(0, nsub, sub_chunk, 0)


def _rwkv_chunk(k_ref, v_ref, r_ref, lo_ref, w0_ref, a0_ref, w2_ref, a2_ref, kk_ref, ka_ref, rk_ref,
                o_ref, bo_ref, s_ref, *, rev, emit, nb):
    C = RWKV_CHUNK
    G = 2 * RWKV_HEAD_DIM

    ii = lax.broadcasted_iota(jnp.int32, (C, G), 0)
    lane = lax.broadcasted_iota(jnp.int32, (C, G), 1)
    jj = lane & (RWKV_HEAD_DIM - 1)
    head0 = lane < RWKV_HEAD_DIM
    dlt = jnp.where(rev, ii - jj, jj - ii)
    strict = dlt < 0
    incl = dlt <= 0
    eye = (ii == jj).astype(f32)
    gi = lax.broadcasted_iota(jnp.int32, (G, G), 0)
    gj = lax.broadcasted_iota(jnp.int32, (G, G), 1)
    blockdiag = (gi >= RWKV_HEAD_DIM) == (gj >= RWKV_HEAD_DIM)
    ones_bd = blockdiag.astype(bf16)
    tri = incl[:, 0:C].astype(bf16)
    base = (ii >> 1) == (jj >> 1)
    offs = [((ii >> (lv + 1)) == (jj >> (lv + 1))) & ((ii >> lv) != (jj >> lv)) for lv in range(1, 6)]

    def segsum(x):
        return jnp.concatenate(
            [_bdot(x[:, p * G:(p + 1) * G], ones_bd) for p in range(RWKV_PAIRS)], axis=1)

    def stack(x):
        xb = x.astype(bf16)
        zero = jnp.zeros_like(xb)
        return jnp.concatenate([jnp.where(head0, xb, zero), jnp.where(head0, zero, xb)], axis=0)

    pre = []
    for bb in range(nb):
        kx = k_ref[bb].astype(f32)
        vx = v_ref[bb].astype(f32)
        rx = r_ref[bb].astype(f32)
        lo = lo_ref[bb, :, 0:DECAY_LORA + ICLR_LORA]
        u = w0_ref[0] + _dot(lo, w2_ref[0])
        softplus = jnp.maximum(-u, 0.0) + jnp.log1p(jnp.exp(-jnp.abs(u)))
        lw = -jnp.exp(-softplus - 0.5)
        a = jax.nn.sigmoid(a0_ref[0] + _dot(lo, a2_ref[0]))
        kkr = kx * kk_ref[...]
        kk = kkr * lax.rsqrt(segsum(kkr * kkr) + 1e-12)
        kd = kx * (1.0 + (a - 1.0) * ka_ref[...])
        be = kk * a
        bonus = segsum(rx * kd * rk_ref[...]) * vx
        cum = _split_dot_left(tri, lw)
        tot = jnp.where(rev, cum[0:1, :], cum[C - 1:C, :])
        gneg = jnp.exp(-cum)
        gh = jnp.exp(tot - cum)
        pre.append(dict(alb=-kk * jnp.exp(cum - lw), rb=rx * jnp.exp(cum), beb=be * gneg, kb=kd * gneg,
                        beh=be * gh, kh=kd * gh, etot=jnp.exp(tot), v=vx, bonus=bonus))

    units = [(bb, p) for bb in range(nb) for p in range(RWKV_PAIRS)]
    sl = lambda p: slice(p * G, (p + 1) * G)
    part = lambda un, name: pre[un[0]][name][:, sl(un[1])]
    S = {un: s_ref[un[0] * RWKV_PAIRS + un[1]] for un in units}
    Sb = {un: S[un].astype(bf16) for un in units}
    X = {un: part(un, "alb").astype(bf16) for un in units}
    Rb = {un: part(un, "rb").astype(bf16) for un in units}
    Ybs = {un: stack(part(un, "beb")) for un in units}
    Yks = {un: stack(part(un, "kb")) for un in units}
    Vb = {un: part(un, "v").astype(bf16) for un in units}
    Vs = {un: stack(part(un, "v")) for un in units}

    XR = {un: jnp.concatenate([X[un], Rb[un]], axis=0) for un in units}
    APb = {un: _dot_nt(XR[un], Ybs[un]) for un in units}
    APk = {un: _dot_nt(XR[un], Yks[un]) for un in units}
    BO = {un: _dot_nt(XR[un], Sb[un]) for un in units}
    Aab = {un: jnp.where(strict, APb[un][0:C], 0.0) for un in units}
    Aak = {un: jnp.where(strict, APk[un][0:C], 0.0).astype(bf16) for un in units}
    Pab = {un: jnp.where(incl, APb[un][C:2 * C], 0.0).astype(bf16) for un in units}
    Pak = {un: jnp.where(incl, APk[un][C:2 * C], 0.0).astype(bf16) for un in units}

    Tm = {un: eye + jnp.where(base, Aab[un], 0.0) for un in units}
    for off in offs:
        Xs = {un: _dot(jnp.where(off, Aab[un], 0.0).astype(bf16), stack(Tm[un])) for un in units}
        Tm = {un: Tm[un] + _dot(Tm[un].astype(bf16), stack(Xs[un])) for un in units}

    AV = {un: _dot(jnp.concatenate([Aak[un], Pak[un]], axis=0), Vs[un]) for un in units}
    Bm = {un: BO[un][0:C] + AV[un][0:C] for un in units}
    U = {un: _dot(Tm[un].astype(bf16), stack(Bm[un])) for un in units}
    Om = {un: BO[un][C:2 * C] + AV[un][C:2 * C] + _dot(Pab[un], stack(U[un])) for un in units}
    for un in units:
        bb, p = un
        upd = _dot_tn(jnp.concatenate([U[un].astype(bf16), Vb[un]], axis=0),
                      jnp.concatenate([part(un, "beh").astype(bf16), part(un, "kh").astype(bf16)], axis=0))
        s_ref[bb * RWKV_PAIRS + p] = jnp.where(blockdiag, S[un] * part(un, "etot") + upd, 0.0)

    @pl.when(emit)
    def _():
        for bb in range(nb):
            o_ref[0, bb] = jnp.concatenate([Om[bb, p] for p in range(RWKV_PAIRS)], axis=1)
            bo_ref[0, bb] = pre[bb]["bonus"]


def _split_dot_left(w, x):
    hi = x.astype(bf16)
    lo = (x - hi.astype(f32)).astype(bf16)
    return _dot(w, hi) + _dot(w, lo)


def _rwkv(z, w0, a0, w2p, a2p, k_k, k_a, r_k, ctx):
    B, L, _ = z.shape
    T = L - ctx
    nsub = RWKV_CHUNKS_PER_STEP
    while ctx % (nsub * RWKV_CHUNK) or T % (nsub * RWKV_CHUNK):
        nsub //= 2
    C = nsub * RWKV_CHUNK
    nc, nl = ctx // C, T // C
    W = RWKV_WIDTH

    def chunk(d, s):
        fwd = jnp.where(s < nc, nl + s, s - nc)
        bwd = jnp.where(s < nc, nl + nc - 1 - s, nl - 1 - (s - nc))
        return jnp.where(d == 0, fwd, bwd)

    def ochunk(d, s):
        sl = jnp.maximum(s - nc, 0)
        return jnp.where(d == 0, sl, nl - 1 - sl)

    nb = RWKV_SAMPLES_PER_STEP if B % RWKV_SAMPLES_PER_STEP == 0 else 1
    zblk = lambda c0, w: pl.BlockSpec((nb, C, w), lambda b, d, s: (b, chunk(d, s), c0 // w))
    dpar = lambda r: pl.BlockSpec((1, r, W), lambda b, d, s: (d, 0, 0))
    par = pl.BlockSpec((1, W), lambda b, d, s: (0, 0))
    oblk = pl.BlockSpec((1, nb, C, W), lambda b, d, s: (d, b, ochunk(d, s), 0))
    G = 2 * RWKV_HEAD_DIM
    return pl.pallas_call(
        functools.partial(_rwkv_kernel, nc=nc, nb=nb, nsub=nsub),
        grid=(B // nb, 2, nc + nl),
        in_specs=[zblk(C_RWK, W), zblk(C_RWV, W), zblk(C_RWR, W), zblk(C_LORA, 256),
                  dpar(1), dpar(1), dpar(G), dpar(G), par, par, par],
        out_specs=[oblk, oblk],
        out_shape=[jax.ShapeDtypeStruct((2, B, T, W), f32), jax.ShapeDtypeStruct((2, B, T, W), f32)],
        scratch_shapes=[pltpu.VMEM((nb * RWKV_PAIRS, G, G), f32)],
        compiler_params=_params(("arbitrary", "arbitrary", "arbitrary")),
        name="rwkv",
    )(z, z, z, z, w0, a0, w2p, a2p, k_k, k_a, r_k)


def _merge_kernel(x_ref, ret_ref, rw0_ref, rw1_ref, b0_ref, b1_ref, gate_ref, retg_ref, gd_ref, mod_ref,
                  ng_ref, gn_ref, lng_ref, lnb_ref, g2_ref, wbr_ref, wbw_ref, wout_ref, wrh_ref, wrl_ref,
                  x1_ref, h2_ref, lt_ref):
    hd = RET_HEAD_DIM
    ret = ret_ref[0]
    parts = []
    for hh in range(RET_HEADS):
        xh = ret[:, hh * hd:(hh + 1) * hd]
        mu = jnp.mean(xh, axis=-1, keepdims=True)
        dv = xh - mu
        var = jnp.mean(dv * dv, axis=-1, keepdims=True)
        parts.append(dv * lax.rsqrt(var + RET_EPS))
    yr = retg_ref[0].astype(f32) * (jnp.concatenate(parts, axis=1) * gn_ref[...])
    y_ret = _bdot(yr, wbr_ref[...])
    W = RWKV_WIDTH
    gi = lax.broadcasted_iota(jnp.int32, (W, W), 0)
    gj = lax.broadcasted_iota(jnp.int32, (W, W), 1)
    ones_bd = ((gi >> 6) == (gj >> 6)).astype(bf16)
    o = rw0_ref[0, 0] + rw1_ref[0, 0]
    mu = _split_dot(o, ones_bd) * (1.0 / RWKV_HEAD_DIM)
    dv = o - mu
    var = _bdot(dv * dv, ones_bd) * (1.0 / RWKV_HEAD_DIM)
    yw = dv * lax.rsqrt(var + RWKV_EPS) * lng_ref[...] + lnb_ref[...]
    gate = _dot(gd_ref[0], g2_ref[...])
    yw = (yw + b0_ref[0, 0] + b1_ref[0, 0]) * gate
    y_rw = _bdot(yw, wbw_ref[...])
    D = y_ret.shape[1]
    g = gate_ref[0].astype(f32)
    m = g[:, :D] * y_ret + g[:, D:] * y_rw
    y = _bdot(m, wout_ref[...])

    def rms(v, gg):
        return v * lax.rsqrt(jnp.mean(v * v, axis=-1, keepdims=True) + NORM_EPS) * gg

    x1 = x_ref[0] + mod_ref[0, 0:1, :] * rms(y, ng_ref[0:1, :])
    x1_ref[0] = x1
    h2 = rms(x1, ng_ref[1:2, :]) * (1.0 + mod_ref[0, 2:3, :]) + mod_ref[0, 1:2, :]
    h2b = h2.astype(bf16)
    h2_ref[0] = h2b
    h2l = (h2 - h2b.astype(f32)).astype(bf16)
    lgt = _dot(h2b, wrh_ref[...]) + _dot(h2l, wrh_ref[...]) + _dot(h2b, wrl_ref[...])
    lt_ref[0] = lgt.T[0:N_EXPERTS, :]


def _merge(x, ret_o, rw_o, bonus, z, mod2, ng12, gn, lng, lnb, g2, wbr, wbw, wout, wrh, wrl):
    B, T, D = x.shape
    tm = MERGE_ROWS if T % MERGE_ROWS == 0 else ROWS
    W = RWKV_WIDTH
    row = lambda w: pl.BlockSpec((1, tm, w), lambda b, i: (b, i, 0))
    dblk = lambda dd: pl.BlockSpec((1, 1, tm, W), lambda b, i: (dd, b, i, 0))
    zblk = lambda c0, w: pl.BlockSpec((1, tm, w), lambda b, i: (b, i, c0 // w))
    full = lambda a: pl.BlockSpec(a.shape, lambda b, i: (0,) * a.ndim)
    return pl.pallas_call(
        _merge_kernel,
        grid=(B, T // tm),
        in_specs=[row(D), row(W), dblk(0), dblk(1), dblk(0), dblk(1),
                  zblk(C_MERGE, 2 * D), zblk(C_RETG, W), zblk(C_LORA + 128, 128),
                  pl.BlockSpec((1, 3, D), lambda b, i: (b, 0, 0)),
                  full(ng12), full(gn), full(lng), full(lnb), full(g2), full(wbr), full(wbw), full(wout),
                  full(wrh), full(wrl)],
        out_specs=[row(D), row(D), pl.BlockSpec((1, N_EXPERTS, tm), lambda b, i: (b, 0, i))],
        out_shape=[jax.ShapeDtypeStruct((B, T, D), f32), jax.ShapeDtypeStruct((B, T, D), bf16),
                   jax.ShapeDtypeStruct((B, N_EXPERTS, T), f32)],
        compiler_params=_params(("arbitrary", "arbitrary")),
        name="merge",
    )(x, ret_o, rw_o, rw_o, bonus, bonus, z, z, z, mod2, ng12, gn, lng, lnb, g2, wbr, wbw, wout, wrh, wrl)


def _route_kernel(lt_ref, slot_ref, rt_ref, cnt_ref, slotf_ref, gate_ref, *, cap):
    B, NE, T = lt_ref.shape
    lg = lt_ref[...]
    mx = jnp.max(lg, axis=1, keepdims=True)
    ex = jnp.exp(lg - mx)
    aff = (ex / jnp.sum(ex, axis=1, keepdims=True)).reshape(B * NE, T)
    E = B * NE

    def count_ge(cand):
        return jnp.sum((aff >= cand).astype(f32), axis=1, keepdims=True)

    def exp_step(_, kk):
        k_lo, k_hi = kk
        km = jnp.floor((k_lo + k_hi) * 0.5)
        ok = count_ge(jnp.exp2(-km)) >= cap
        return jnp.where(ok, k_lo, km), jnp.where(ok, km, k_hi)

    k_lo, k_hi = lax.fori_loop(0, ROUTE_EXP_STEPS, exp_step,
                               (jnp.full((E, 1), -1.0, f32), jnp.full((E, 1), ROUTE_MAX_EXP, f32)))
    lo0 = jnp.where(k_hi >= ROUTE_MAX_EXP, 0.0, jnp.exp2(-k_hi))
    hi0 = jnp.exp2(-k_lo)

    def val_step(_, lh):
        lo, hi = lh
        mid = (lo + hi) * 0.5
        ok = count_ge(mid) >= cap
        return jnp.where(ok, mid, lo), jnp.where(ok, hi, mid)

    lo, hi = lax.fori_loop(0, ROUTE_VAL_STEPS, val_step, (lo0, hi0))
    gt = aff >= hi
    eq = (aff >= lo) & (aff < hi)
    need = cap - jnp.sum(gt.astype(f32), axis=1, keepdims=True)
    tri = (lax.broadcasted_iota(jnp.int32, (T, T), 0) < lax.broadcasted_iota(jnp.int32, (T, T), 1)).astype(bf16)
    eq_before = _dot(eq.astype(bf16), tri)
    sel = gt | (eq & (eq_before < need))
    slot = _dot(sel.astype(bf16), tri)
    slot_f = jnp.where(sel, slot, -1.0)
    slot_ref[...] = slot_f.astype(jnp.int32).reshape(B, NE, T)
    before = (lax.broadcasted_iota(jnp.int32, (T, 128), 0)
              < lax.broadcasted_iota(jnp.int32, (T, 128), 1) * GATHER_TILE).astype(bf16)
    cnt_ref[...] = _dot(sel.astype(bf16), before).astype(jnp.int32).reshape(B, NE, 128)
    slotf_ref[...] = slot_f.reshape(B, NE, T)
    gate_ref[...] = jnp.where(sel, aff, 0.0).reshape(B, NE, T)

    def transpose_sample(b, carry):
        packed = jnp.concatenate([slotf_ref[b], gate_ref[b], jnp.zeros((128 - 2 * NE, T), f32)], axis=0)
        rt_ref[b] = packed.T.astype(bf16)
        return carry

    lax.fori_loop(0, B, transpose_sample, 0)


def _route(lt, cap):
    B, E, T = lt.shape
    assert cap <= 256 and T % GATHER_TILE == 0 and T // GATHER_TILE < 128
    return pl.pallas_call(
        functools.partial(_route_kernel, cap=cap),
        grid=(1,),
        in_specs=[pl.BlockSpec((B, E, T), lambda i: (0, 0, 0))],
        out_specs=[pl.BlockSpec((B, E, T), lambda i: (0, 0, 0)), pl.BlockSpec((B, T, 128), lambda i: (0, 0, 0)),
                   pl.BlockSpec((B, E, 128), lambda i: (0, 0, 0))],
        out_shape=[jax.ShapeDtypeStruct((B, E, T), jnp.int32), jax.ShapeDtypeStruct((B, T, 128), bf16),
                   jax.ShapeDtypeStruct((B, E, 128), jnp.int32)],
        scratch_shapes=[pltpu.VMEM((B, E, T), f32), pltpu.VMEM((B, E, T), f32)],
        compiler_params=_params(("arbitrary",)),
        name="route",
    )(lt)


def _ffn_kernel(cnt_ref, slot_ref, h_ref, wg_ref, wu_ref, wd_ref, o_ref, wgb_ref, wub_ref, wdb_ref, xg_ref, *, cap):
    T = h_ref.shape[1]
    e = pl.program_id(0)

    @pl.when(pl.program_id(1) == 0)
    def _():
        wgb_ref[...] = wg_ref[0].astype(bf16)
        wub_ref[...] = wu_ref[0].astype(bf16)
        wdb_ref[...] = wd_ref[0].astype(bf16)

    nb = h_ref.shape[0]
    GT, GW, GA = GATHER_TILE, GATHER_WINDOW, GATHER_ALIGN
    nt = T // GT
    for bb in range(nb):
        b = pl.program_id(1) * nb + bb
        base = [pl.multiple_of(lax.shift_left(lax.shift_right_logical(cnt_ref[b, e, j], GATHER_ALIGN_LOG2),
                                              GATHER_ALIGN_LOG2), GA) for j in range(nt)]
        fits = cnt_ref[b, e, 1] - base[0] <= GW
        for j in range(1, nt):
            fits = fits & (cnt_ref[b, e, j + 1] - base[j] <= GW)

        @pl.when(fits)
        def _():
            xg_ref[bb] = jnp.zeros(xg_ref.shape[1:], f32)
            for j in range(nt):
                rows = base[j] + lax.broadcasted_iota(jnp.int32, (GW, GT), 0)
                onehot = (slot_ref[bb, 0, j:j + 1, :] == rows).astype(bf16)
                xg_ref[bb, pl.ds(base[j], GW), :] += _dot(onehot, h_ref[bb, j * GT:(j + 1) * GT, :])

        @pl.when(jnp.logical_not(fits))
        def _():
            rows = lax.broadcasted_iota(jnp.int32, (cap, GT), 0)
            acc = jnp.zeros((cap, h_ref.shape[2]), f32)
            for j in range(nt):
                onehot = (slot_ref[bb, 0, j:j + 1, :] == rows).astype(bf16)
                acc = acc + _dot(onehot, h_ref[bb, j * GT:(j + 1) * GT, :])
            xg_ref[bb, 0:cap, :] = acc

    xg = jnp.concatenate([xg_ref[bb, 0:cap, :].astype(bf16) for bb in range(nb)], axis=0)
    hg = _dot(xg, wgb_ref[...])
    hu = _dot(xg, wub_ref[...])
    hid = (hg * _sigmoid(hg) * hu).astype(bf16)
    out = _dot(hid, wdb_ref[...]).astype(bf16)
    for bb in range(nb):
        o_ref[bb, 0] = out[bb * cap:(bb + 1) * cap]


def _ffn(cnt, slot4, h2, wg, wu, wd, cap):
    B, T, D = h2.shape
    E, _, F = wg.shape
    nb = FFN_SAMPLES_PER_STEP if B % FFN_SAMPLES_PER_STEP == 0 else 1
    grid_spec = pltpu.PrefetchScalarGridSpec(
        num_scalar_prefetch=1,
        grid=(E, B // nb),
        in_specs=[pl.BlockSpec((nb, 1, T // GATHER_TILE, GATHER_TILE), lambda e, b, c: (b, e, 0, 0)),
                  pl.BlockSpec((nb, T, D), lambda e, b, c: (b, 0, 0)),
                  pl.BlockSpec((1, D, F), lambda e, b, c: (e, 0, 0)),
                  pl.BlockSpec((1, D, F), lambda e, b, c: (e, 0, 0)),
                  pl.BlockSpec((1, F, D), lambda e, b, c: (e, 0, 0))],
        out_specs=pl.BlockSpec((nb, 1, cap, D), lambda e, b, c: (b, e, 0, 0)),
        scratch_shapes=[pltpu.VMEM((D, F), bf16), pltpu.VMEM((D, F), bf16), pltpu.VMEM((F, D), bf16),
                        pltpu.VMEM((nb, cap + GATHER_WINDOW, D), f32)])
    return pl.pallas_call(
        functools.partial(_ffn_kernel, cap=cap),
        grid_spec=grid_spec,
        out_shape=jax.ShapeDtypeStruct((B, E, cap, D), bf16),
        compiler_params=_params(("arbitrary", "arbitrary"), FFN_VMEM_LIMIT),
        name="ffn",
    )(cnt, slot4, h2, wg, wu, wd)


def _combine_kernel(cnt_ref, rt_ref, eo_ref, x1_ref, mod_ref, ng_ref, o_ref, y_ref, *, cap):
    b = pl.program_id(0)
    i = pl.program_id(1)
    rt = rt_ref[0].astype(f32)
    tm = rt.shape[0]
    E = eo_ref.shape[1]
    W = min(SCATTER_WINDOW, cap)
    GS = SCATTER_GROUP
    ST = GATHER_TILE
    for st in range(tm // ST):
        t = i * (tm // ST) + st
        r = rt[st * ST:(st + 1) * ST]
        base, fits = [], None
        for e in range(E):
            lo = cnt_ref[b, e, t]
            be = jnp.minimum(lax.shift_left(lax.shift_right_logical(lo, GATHER_ALIGN_LOG2), GATHER_ALIGN_LOG2),
                             cap - W)
            base.append(pl.multiple_of(be, GATHER_ALIGN))
            ok = cnt_ref[b, e, t + 1] - be <= W
            fits = ok if fits is None else fits & ok

        def scatter_matrix(e, first, width, r=r):
            cols = (first + lax.broadcasted_iota(jnp.int32, (ST, width), 1)).astype(f32)
            return jnp.where(r[:, e:e + 1] == cols, r[:, E + e:E + e + 1], 0.0).astype(bf16)

        @pl.when(fits)
        def _():
            y = jnp.zeros((ST, y_ref.shape[1]), f32)
            for g in range(0, E, GS):
                p = jnp.concatenate([scatter_matrix(e, base[e], W) for e in range(g, g + GS)], axis=1)
                rows = jnp.concatenate([eo_ref[0, e, pl.ds(base[e], W), :] for e in range(g, g + GS)], axis=0)
                y = y + _dot(p, rows)
            y_ref[st * ST:(st + 1) * ST, :] = y

        @pl.when(jnp.logical_not(fits))
        def _():
            y = jnp.zeros((ST, y_ref.shape[1]), f32)
            for e in range(E):
                y = y + _dot(scatter_matrix(e, 0, cap), eo_ref[0, e])
            y_ref[st * ST:(st + 1) * ST, :] = y

    y = y_ref[...]
    yn = y * lax.rsqrt(jnp.mean(y * y, axis=-1, keepdims=True) + NORM_EPS) * ng_ref[...]
    o_ref[0] = x1_ref[0] + mod_ref[0] * yn


def _combine(cnt, rt, eo, x1, g2mod, ng3, cap):
    B, T, D = x1.shape
    E = eo.shape[1]
    tm = 512 if T % 512 == 0 else T
    assert E % SCATTER_GROUP == 0 and tm % GATHER_TILE == 0
    grid_spec = pltpu.PrefetchScalarGridSpec(
        num_scalar_prefetch=1,
        grid=(B, T // tm),
        in_specs=[pl.BlockSpec((1, tm, 128), lambda b, i, c: (b, i, 0)),
                  pl.BlockSpec((1, E, cap, D), lambda b, i, c: (b, 0, 0, 0)),
                  pl.BlockSpec((1, tm, D), lambda b, i, c: (b, i, 0)),
                  pl.BlockSpec((1, 1, D), lambda b, i, c: (b, 0, 0)),
                  pl.BlockSpec((1, D), lambda b, i, c: (0, 0))],
        out_specs=pl.BlockSpec((1, tm, D), lambda b, i, c: (b, i, 0)),
        scratch_shapes=[pltpu.VMEM((tm, D), f32)])
    return pl.pallas_call(
        functools.partial(_combine_kernel, cap=cap),
        grid_spec=grid_spec,
        out_shape=jax.ShapeDtypeStruct((B, T, D), f32),
        compiler_params=_params(("arbitrary", "arbitrary")),
        name="combine",
    )(cnt, rt, eo, x1, g2mod, ng3)


def _permute_columns(w):
    sk, sv, rk, rv, wd, ad = 0, 512, 1024, 1536, 2048, 2112
    q0 = 2176
    rq, rg, rr, gd, mg = q0, q0 + 512, q0 + 1024, q0 + 1536, q0 + 1664
    order = [(mg, 2048), (sk, 512), (sv, 512), (rq, 512), (rg, 512), (rk, 512), (rv, 512), (rr, 512),
             (wd, 64), (ad, 64), (gd, 128)]
    parts = [w[:, a:a + n] for a, n in order]
    parts.append(jnp.zeros((w.shape[0], IN_COLS - USED_COLS), w.dtype))
    return jnp.concatenate(parts, axis=1)


def _rope_tables(T, CT):
    t = jnp.arange(T)
    nfreq = RET_HEAD_DIM // 4
    inv = ROPE_BASE ** (-jnp.arange(nfreq, dtype=f32) / nfreq)
    ang = jnp.concatenate([(t // GRID_W).astype(f32)[:, None] * inv,
                           (t % GRID_W).astype(f32)[:, None] * inv], axis=-1)
    cos, sin = jnp.cos(ang), jnp.sin(ang)
    cosf = jnp.concatenate([cos, cos], axis=1)
    sinf = jnp.concatenate([-sin, sin], axis=1)
    return (jnp.concatenate([cosf, jnp.ones((CT, RET_HEAD_DIM), f32)], axis=0),
            jnp.concatenate([sinf, jnp.zeros((CT, RET_HEAD_DIM), f32)], axis=0))


def kernel(x, c, ctx, c_ctx, w_mod, b_mod, norm_g, w_in, ret_log_decay, ret_gn_g, rwkv_mu, rwkv_k_k, rwkv_k_a,
           rwkv_r_k, rwkv_w0, rwkv_w2, rwkv_a0, rwkv_a2, rwkv_g2, rwkv_ln_g, rwkv_ln_b, w_br_ret, w_br_rwkv,
           w_out, w_router, w_gate, w_up, w_down):
    B, T, D = x.shape
    CT = ctx.shape[1]
    assert w_mod.shape[0] == 1 and D == D_MODEL
    assert CT % ROWS == 0 and T % ROWS == 0 and T % GRID_W == 0
    cap = CAPACITY_FACTOR * T // N_EXPERTS
    assert cap % 8 == 0

    mrows = -(-(B + 1) // 8) * 8
    cc = jnp.zeros((mrows, D), f32).at[:B].set(c).at[B].set(c_ctx)
    mod = _modulation(cc, w_mod[0], b_mod[0])
    lat = mod[:B].reshape(B, N_MOD, D)
    cm = jnp.broadcast_to(mod[B].reshape(1, N_MOD, D), (B, N_MOD, D))
    modrows = jnp.concatenate([lat[:, 0:2], cm[:, 0:2]], axis=1)

    w_perm = _permute_columns(w_in[0]).astype(bf16)
    mu = rwkv_mu[0]
    ss = 2 * RWKV_WIDTH + DECAY_LORA + ICLR_LORA
    mu_full = jnp.zeros((2, IN_COLS), f32)
    mu_full = mu_full.at[:, C_RWK:C_RWK + 1024].set(mu[:, 0:1024])
    mu_full = mu_full.at[:, C_RWR:C_RWR + 512].set(mu[:, ss:ss + 512])
    mu_full = mu_full.at[:, C_LORA:C_LORA + 128].set(mu[:, 1024:ss])
    mu_full = mu_full.at[:, C_LORA + 128:USED_COLS].set(mu[:, ss + 512:])
    cosf, sinf = _rope_tables(T, CT)
    z = _inproj(x, ctx, modrows, norm_g[0, 0:1], w_perm, mu_full, cosf, sinf)

    lg = -jnp.exp(ret_log_decay[0].astype(f32))
    ret_o = _retention(lg, z, CT)

    G = 2 * RWKV_HEAD_DIM
    w2p = jnp.zeros((2, G, RWKV_WIDTH), f32).at[:, :DECAY_LORA].set(rwkv_w2[0]).astype(bf16)
    a2p = jnp.zeros((2, G, RWKV_WIDTH), f32).at[:, DECAY_LORA:].set(rwkv_a2[0]).astype(bf16)
    rw_o, bonus = _rwkv(z, rwkv_w0[0][:, None, :], rwkv_a0[0][:, None, :], w2p, a2p,
                        rwkv_k_k[0][None], rwkv_k_a[0][None], rwkv_r_k[0][None], CT)

    mod2 = jnp.stack([lat[:, 2], lat[:, 3], lat[:, 4]], axis=1)
    wr_pad = jnp.zeros((D, 128), f32).at[:, :N_EXPERTS].set(w_router[0])
    wr_hi = wr_pad.astype(bf16)
    wr_lo = (wr_pad - wr_hi.astype(f32)).astype(bf16)
    x1, h2, lt = _merge(x, ret_o, rw_o, bonus, z, mod2, norm_g[0, 1:3], ret_gn_g[0][None], rwkv_ln_g[0][None],
                        rwkv_ln_b[0][None], rwkv_g2[0].astype(bf16), w_br_ret[0].astype(bf16),
                        w_br_rwkv[0].astype(bf16), w_out[0].astype(bf16), wr_hi, wr_lo)

    slot, rt, cnt = _route(lt, cap)
    cnt = cnt[:, :, :T // GATHER_TILE + 1]
    eo = _ffn(cnt, slot.reshape(B, N_EXPERTS, T // GATHER_TILE, GATHER_TILE), h2, w_gate[0], w_up[0], w_down[0], cap)
    return _combine(cnt, rt, eo, x1, lat[:, 5:6], norm_g[0, 3:4], cap)
```

```python
import functools

import jax
import jax.numpy as jnp
from jax import lax
from jax.experimental import pallas as pl
from jax.experimental.pallas import tpu as pltpu

f32 = jnp.float32
bf16 = jnp.bfloat16

D_MODEL = 1024
GRID_W = 64
RET_HEAD_DIM = 128
RET_WIDTH = 512
RET_HEADS = 4
RET_CHUNK = 128
RET_EPS = 1e-5
ROPE_BASE = 10000.0
RWKV_HEAD_DIM = 64
RWKV_WIDTH = 512
RWKV_PAIRS = 4
RWKV_CHUNK = 64
RWKV_CHUNKS_PER_STEP = 4
RWKV_SAMPLES_PER_STEP = 8
DECAY_LORA = 64
ICLR_LORA = 64
GATE_LORA = 128
RWKV_EPS = 64e-5
N_EXPERTS = 16
EXPERT_FF = 1024
CAPACITY_FACTOR = 2
N_MOD = 6
NORM_EPS = 1e-6

C_MERGE, C_RETK, C_RETV, C_RETQ, C_RETG = 0, 2048, 2560, 3072, 3584
C_RWK, C_RWV, C_RWR, C_LORA = 4096, 4608, 5120, 5632
USED_COLS = 5888
TN = 512
IN_COLS = -(-USED_COLS // TN) * TN
ROWS = 256
STEP_ROWS = 768
POST_ROWS = 128
QUERY_ROWS = 1024
MERGE_ROWS = 512

ROUTE_MAX_EXP = 126.0
ROUTE_EXP_STEPS = 8
ROUTE_VAL_STEPS = 26

VMEM_LIMIT = 56 * 1024 * 1024
FFN_SAMPLES_PER_STEP = 2
GATHER_TILE = 256
GATHER_ALIGN_LOG2 = 4
GATHER_ALIGN = 1 << GATHER_ALIGN_LOG2
GATHER_WINDOW = 80
SCATTER_WINDOW = 64
SCATTER_GROUP = 4
FFN_VMEM_LIMIT = 62 * 1024 * 1024


def _dot(a, b):
    return jnp.dot(a, b, preferred_element_type=f32)


def _dot_nt(a, b):
    return lax.dot_general(a, b, (((1,), (1,)), ((), ())), preferred_element_type=f32)


def _dot_tn(a, b):
    return lax.dot_general(a, b, (((0,), (0,)), ((), ())), preferred_element_type=f32)


def _bdot(a, b):
    return _dot(a.astype(bf16), b.astype(bf16))


def _split_dot(x, w):
    hi = x.astype(bf16)
    lo = (x - hi.astype(f32)).astype(bf16)
    return _dot(hi, w) + _dot(lo, w)


def _sigmoid(x):
    return 0.5 * jnp.tanh(0.5 * x) + 0.5


def _params(sem, limit=VMEM_LIMIT):
    return pltpu.CompilerParams(dimension_semantics=sem, vmem_limit_bytes=limit)


def _mod_kernel(c_ref, w_ref, b_ref, o_ref):
    c = c_ref[...]
    s = c * jax.nn.sigmoid(c)
    o_ref[...] = _bdot(s, w_ref[...]) + b_ref[...]


def _modulation(cc, w_mod, b_mod):
    m, d = cc.shape
    n = w_mod.shape[1]
    tn = 512
    return pl.pallas_call(
        _mod_kernel,
        grid=(n // tn,),
        in_specs=[pl.BlockSpec((m, d), lambda j: (0, 0)),
                  pl.BlockSpec((d, tn), lambda j: (0, j)),
                  pl.BlockSpec((1, tn), lambda j: (0, j))],
        out_specs=pl.BlockSpec((m, tn), lambda j: (0, j)),
        out_shape=jax.ShapeDtypeStruct((m, n), f32),
        compiler_params=_params(("arbitrary",)),
        name="mod",
    )(cc, w_mod, b_mod.reshape(1, n))


def _inproj_kernel(x_ref, c_ref, mod_ref, g_ref, w_ref, mu_ref, cos_ref, sin_ref, o_ref, h_ref, z_ref):
    n = pl.program_id(1)
    T = x_ref.shape[1]
    L = T + c_ref.shape[1]
    nlat = T // ROWS
    nchunk = L // ROWS
    PAD = 8

    SR = STEP_ROWS if L % STEP_ROWS == 0 else ROWS

    @pl.when(n == 0)
    def _():
        z_ref[0:PAD, :] = jnp.zeros((PAD, TN), f32)
        z_ref[PAD + L:PAD + L + PAD, :] = jnp.zeros((PAD, TN), f32)

        def norm_chunk(src_ref, s0, r0, o):
            xb = src_ref[0, pl.ds(s0, ROWS), :]
            y = xb * lax.rsqrt(jnp.mean(xb * xb, axis=-1, keepdims=True) + NORM_EPS) * g_ref[...]
            sh = mod_ref[0, o:o + 1, :]
            sc = mod_ref[0, o + 1:o + 2, :]
            h_ref[pl.ds(r0, ROWS), :] = (y * (1.0 + sc) + sh).astype(bf16)

        def lat_body(i, carry):
            r0 = pl.multiple_of(i * ROWS, ROWS)
            norm_chunk(x_ref, r0, r0, 0)
            return carry

        lax.fori_loop(0, nlat, lat_body, 0)
        for j in range(nlat, nchunk):
            norm_chunk(c_ref, j * ROWS - T, j * ROWS, 2)

    PR = POST_ROWS
    QR = QUERY_ROWS

    def rope(scale):
        def post(z, r0):
            if scale != 1.0:
                z = z * scale
            cs = cos_ref[pl.ds(r0, z.shape[0]), :]
            sn = sin_ref[pl.ds(r0, z.shape[0]), :]
            parts = []
            for hh in range(TN // RET_HEAD_DIM):
                zh = z[:, hh * RET_HEAD_DIM:(hh + 1) * RET_HEAD_DIM]
                parts.append(zh * cs + pltpu.roll(zh, RET_HEAD_DIM // 2, 1) * sn)
            return jnp.concatenate(parts, axis=1)
        return post

    def shifted(r0, at_boundary):
        win = z_ref[pl.ds(r0, PR + 2 * PAD), :]
        prev = win[PAD - 1:PAD - 1 + PR]
        z = win[PAD:PAD + PR]
        nxt = win[PAD + 1:PAD + 1 + PR]
        if at_boundary:
            rid = r0 + lax.broadcasted_iota(jnp.int32, (PR, 1), 0)
            prev = jnp.where(rid == T, 0.0, prev)
            nxt = jnp.where(rid == T - 1, 0.0, nxt)
        mu0 = mu_ref[0:1, :]
        mu1 = mu_ref[1:2, :]
        return (1.0 - mu0 - mu1) * z + mu0 * prev + mu1 * nxt

    def lora_act(zs):
        lane = lax.broadcasted_iota(jnp.int32, (1, TN), 1)
        return jnp.where(lane < DECAY_LORA, jnp.tanh(zs),
                         jnp.where(lane < DECAY_LORA + ICLR_LORA, zs, jax.nn.sigmoid(zs)))

    def product(r0, rows=SR):
        return _dot(h_ref[pl.ds(r0, rows), :], w_ref[...])

    def direct(post, latent_only=False):
        rows, total = (QR, T) if latent_only and T % QR == 0 else (SR, L)

        def body(i, carry):
            r0 = pl.multiple_of(i * rows, rows)
            o_ref[0, pl.ds(r0, rows), :] = post(product(r0, rows), r0).astype(bf16)
            return carry
        lax.fori_loop(0, total // rows, body, 0)
        if total < L:
            o_ref[0, total:L, :] = jnp.zeros((L - total, TN), bf16)

    def via_buffer(act):
        def mm_body(i, carry):
            r0 = pl.multiple_of(i * SR, SR)
            z_ref[pl.ds(PAD + r0, SR), :] = product(r0)
            return carry
        lax.fori_loop(0, L // SR, mm_body, 0)

        def post(r0, at_boundary):
            o_ref[0, pl.ds(r0, PR), :] = act(shifted(r0, at_boundary)).astype(bf16)

        def post_body(i, carry):
            post(pl.multiple_of(i * PR, PR), False)
            return carry

        seam = T // PR
        lax.fori_loop(0, seam - 1, post_body, 0)
        post((seam - 1) * PR, True)
        post(seam * PR, True)
        lax.fori_loop(seam + 1, L // PR, post_body, 0)

    t = lambda c: c // TN
    pl.when(n < t(C_RETK))(lambda: direct(lambda z, r0: _sigmoid(z), latent_only=True))
    pl.when((n >= t(C_RETK)) & (n < t(C_RETV)))(lambda: direct(rope(RET_HEAD_DIM ** -0.5)))
    pl.when((n >= t(C_RETV)) & (n < t(C_RETQ)))(lambda: direct(lambda z, r0: z))
    pl.when((n >= t(C_RETQ)) & (n < t(C_RETG)))(lambda: direct(rope(1.0), latent_only=True))
    pl.when((n >= t(C_RETG)) & (n < t(C_RWK)))(lambda: direct(lambda z, r0: z * _sigmoid(z), latent_only=True))
    pl.when((n >= t(C_RWK)) & (n < t(C_LORA)))(lambda: via_buffer(lambda zs: zs))
    pl.when(n >= t(C_LORA))(lambda: via_buffer(lora_act))


def _inproj(x, ctx, modrows, g0, w_perm, mu_full, cosf, sinf):
    B, T, D = x.shape
    CT = ctx.shape[1]
    L = T + CT
    nt = IN_COLS // TN
    return pl.pallas_call(
        _inproj_kernel,
        grid=(B, nt),
        in_specs=[pl.BlockSpec((1, T, D), lambda b, n: (b, 0, 0)),
                  pl.BlockSpec((1, CT, D), lambda b, n: (b, 0, 0)),
                  pl.BlockSpec((1, 4, D), lambda b, n: (b, 0, 0)),
                  pl.BlockSpec((1, D), lambda b, n: (0, 0)),
                  pl.BlockSpec((D, TN), lambda b, n: (0, n)),
                  pl.BlockSpec((2, TN), lambda b, n: (0, n)),
                  pl.BlockSpec((L, RET_HEAD_DIM), lambda b, n: (0, 0)),
                  pl.BlockSpec((L, RET_HEAD_DIM), lambda b, n: (0, 0))],
        out_specs=pl.BlockSpec((1, L, TN), lambda b, n: (b, 0, n)),
        out_shape=jax.ShapeDtypeStruct((B, L, IN_COLS), bf16),
        scratch_shapes=[pltpu.VMEM((L, D), bf16), pltpu.VMEM((L + 16, TN), f32)],
        compiler_params=_params(("arbitrary", "arbitrary")),
        name="inproj",
    )(x, ctx, modrows, g0, w_perm, mu_full, cosf, sinf)


def _ret_kernel(lg_ref, q_ref, k_ref, v_ref, o_ref, r_ref, tab_ref, *, ctx):
    L = q_ref.shape[1]
    Cc = RET_CHUNK
    hd = RET_HEAD_DIM
    lat = L - ctx
    nc = ctx // Cc
    nl = lat // Cc
    combos = [(h, d) for h in range(RET_HEADS) for d in (0, 1)]
    INTRA, CROSS, TAIL, DECAY = 0, 1, 2, 3

    @pl.when(pl.program_id(0) == 0)
    def _():
        ii = lax.broadcasted_iota(jnp.int32, (Cc, Cc), 0).astype(f32)
        jj = lax.broadcasted_iota(jnp.int32, (Cc, Cc), 1).astype(f32)
        for idx, (h, d) in enumerate(combos):
            lg = lg_ref[d, h]
            if d == 0:
                diff = ii - jj
                cross = jnp.exp(lg * (ii + 1.0))
                tailw = jnp.exp(lg * (Cc - 1.0 - ii))
            else:
                diff = jj - ii
                cross = jnp.exp(lg * (Cc - ii))
                tailw = jnp.exp(lg * ii)
            tab_ref[idx, INTRA] = jnp.where(diff >= 0, jnp.exp(lg * jnp.maximum(diff, 0.0)), 0.0)
            tab_ref[idx, CROSS] = cross
            tab_ref[idx, TAIL] = tailw
            tab_ref[idx, DECAY] = jnp.exp(jnp.zeros((Cc, Cc), f32) + lg * Cc)

    r_ref[...] = jnp.zeros(r_ref.shape, f32)
    o_ref[...] = jnp.zeros(o_ref.shape, f32)

    def cols(h):
        return slice(h * hd, (h + 1) * hd)

    def update(idx, kc, vc):
        ks = (kc.astype(f32) * tab_ref[idx, TAIL]).astype(bf16)
        r_ref[idx] = r_ref[idx] * tab_ref[idx, DECAY] + _dot_tn(ks, vc)

    def ctx_step(s, carry):
        for idx, (h, d) in enumerate(combos):
            row0 = pl.multiple_of(lat + (s * Cc if d == 0 else (nc - 1 - s) * Cc), Cc)
            update(idx, k_ref[0, pl.ds(row0, Cc), cols(h)], v_ref[0, pl.ds(row0, Cc), cols(h)])
        return carry

    def lat_step(s, carry):
        t0s, qs, ks, vs = [], [], [], []
        for h, d in combos:
            t0 = pl.multiple_of(s * Cc if d == 0 else (nl - 1 - s) * Cc, Cc)
            row0 = t0
            t0s.append(t0)
            qs.append(q_ref[0, pl.ds(row0, Cc), cols(h)])
            ks.append(k_ref[0, pl.ds(row0, Cc), cols(h)])
            vs.append(v_ref[0, pl.ds(row0, Cc), cols(h)])
        n = len(combos)
        sc = [(_dot_nt(qs[i], ks[i]) * tab_ref[i, INTRA]).astype(bf16) for i in range(n)]
        oc = [_dot(qs[i], r_ref[i].astype(bf16)) * tab_ref[i, CROSS] for i in range(n)]
        oi = [_dot(sc[i], vs[i]) for i in range(n)]
        for i, (h, d) in enumerate(combos):
            o_ref[0, pl.ds(t0s[i], Cc), cols(h)] += oi[i] + oc[i]
        for i in range(n):
            update(i, ks[i], vs[i])
        return carry

    lax.fori_loop(0, nc, ctx_step, 0)
    lax.fori_loop(0, nl, lat_step, 0)


def _retention(lg, z, ctx):
    B, L, _ = z.shape
    T = L - ctx
    W = RET_WIDTH
    Cc = RET_CHUNK
    blk = lambda c0: pl.BlockSpec((1, L, W), lambda b: (b, 0, c0 // W))
    nchain = 2 * RET_HEADS
    return pl.pallas_call(
        functools.partial(_ret_kernel, ctx=ctx),
        grid=(B,),
        in_specs=[pl.BlockSpec(memory_space=pltpu.SMEM), blk(C_RETQ), blk(C_RETK), blk(C_RETV)],
        out_specs=pl.BlockSpec((1, T, W), lambda b: (b, 0, 0)),
        out_shape=jax.ShapeDtypeStruct((B, T, W), f32),
        scratch_shapes=[pltpu.VMEM((nchain, Cc, Cc), f32), pltpu.VMEM((nchain, 4, Cc, Cc), f32)],
        compiler_params=_params(("arbitrary",)),
        name="ret",
    )(lg, z, z, z)


def _rwkv_kernel(k_ref, v_ref, r_ref, lo_ref, w0_ref, a0_ref, w2_ref, a2_ref, kk_ref, ka_ref, rk_ref,
                 o_ref, bo_ref, s_ref, *, nc, nb, nsub):
    d = pl.program_id(1)
    s = pl.program_id(2)
    C = RWKV_CHUNK

    @pl.when(s == 0)
    def _():
        s_ref[...] = jnp.zeros(s_ref.shape, f32)

    def sub_chunk(j, carry):
        r0 = pl.multiple_of(jnp.where(d == 1, nsub - 1 - j, j) * C, C)
        rows = lambda ref: ref.at[:, pl.ds(r0, C), :]
        _rwkv_chunk(rows(k_ref), rows(v_ref), rows(r_ref), rows(lo_ref), w0_ref, a0_ref, w2_ref, a2_ref,
                    kk_ref, ka_ref, rk_ref, o_ref.at[:, :, pl.ds(r0, C), :], bo_ref.at[:, :, pl.ds(r0, C), :],
                    s_ref, rev=d == 1, emit=s >= nc, nb=nb)
        return carry

    lax.fori_loop(0, nsub, sub_chunk, 0)


def _rwkv_chunk(k_ref, v_ref, r_ref, lo_ref, w0_ref, a0_ref, w2_ref, a2_ref, kk_ref, ka_ref, rk_ref,
                o_ref, bo_ref, s_ref, *, rev, emit, nb):
    C = RWKV_CHUNK
    G = 2 * RWKV_HEAD_DIM

    ii = lax.broadcasted_iota(jnp.int32, (C, G), 0)
    lane = lax.broadcasted_iota(jnp.int32, (C, G), 1)
    jj = lane & (RWKV_HEAD_DIM - 1)
    head0 = lane < RWKV_HEAD_DIM
    dlt = jnp.where(rev, ii - jj, jj - ii)
    strict = dlt < 0
    incl = dlt <= 0
    eye = (ii == jj).astype(f32)
    gi = lax.broadcasted_iota(jnp.int32, (G, G), 0)
    gj = lax.broadcasted_iota(jnp.int32, (G, G), 1)
    blockdiag = (gi >= RWKV_HEAD_DIM) == (gj >= RWKV_HEAD_DIM)
    ones_bd = blockdiag.astype(bf16)
    tri = incl[:, 0:C].astype(bf16)
    base = (ii >> 1) == (jj >> 1)
    offs = [((ii >> (lv + 1)) == (jj >> (lv + 1))) & ((ii >> lv) != (jj >> lv)) for lv in range(1, 6)]

    def segsum(x):
        return jnp.concatenate(
            [_bdot(x[:, p * G:(p + 1) * G], ones_bd) for p in range(RWKV_PAIRS)], axis=1)

    def stack(x):
        xb = x.astype(bf16)
        zero = jnp.zeros_like(xb)
        return jnp.concatenate([jnp.where(head0, xb, zero), jnp.where(head0, zero, xb)], axis=0)

    pre = []
    for bb in range(nb):
        kx = k_ref[bb].astype(f32)
        vx = v_ref[bb].astype(f32)
        rx = r_ref[bb].astype(f32)
        lo = lo_ref[bb, :, 0:DECAY_LORA + ICLR_LORA]
        u = w0_ref[0] + _dot(lo, w2_ref[0])
        softplus = jnp.maximum(-u, 0.0) + jnp.log1p(jnp.exp(-jnp.abs(u)))
        lw = -jnp.exp(-softplus - 0.5)
        a = jax.nn.sigmoid(a0_ref[0] + _dot(lo, a2_ref[0]))
        kkr = kx * kk_ref[...]
        kk = kkr * lax.rsqrt(segsum(kkr * kkr) + 1e-12)
        kd = kx * (1.0 + (a - 1.0) * ka_ref[...])
        be = kk * a
        bonus = segsum(rx * kd * rk_ref[...]) * vx
        cum = _split_dot_left(tri, lw)
        tot = jnp.where(rev, cum[0:1, :], cum[C - 1:C, :])
        gneg = jnp.exp(-cum)
        gh = jnp.exp(tot - cum)
        pre.append(dict(alb=-kk * jnp.exp(cum - lw), rb=rx * jnp.exp(cum), beb=be * gneg, kb=kd * gneg,
                        beh=be * gh, kh=kd * gh, etot=jnp.exp(tot), v=vx, bonus=bonus))

    units = [(bb, p) for bb in range(nb) for p in range(RWKV_PAIRS)]
    sl = lambda p: slice(p * G, (p + 1) * G)
    part = lambda un, name: pre[un[0]][name][:, sl(un[1])]
    S = {un: s_ref[un[0] * RWKV_PAIRS + un[1]] for un in units}
    Sb = {un: S[un].astype(bf16) for un in units}
    X = {un: part(un, "alb").astype(bf16) for un in units}
    Rb = {un: part(un, "rb").astype(bf16) for un in units}
    Ybs = {un: stack(part(un, "beb")) for un in units}
    Yks = {un: stack(part(un, "kb")) for un in units}
    Vb = {un: part(un, "v").astype(bf16) for un in units}
    Vs = {un: stack(part(un, "v")) for un in units}

    XR = {un: jnp.concatenate([X[un], Rb[un]], axis=0) for un in units}
    APb = {un: _dot_nt(XR[un], Ybs[un]) for un in units}
    APk = {un: _dot_nt(XR[un], Yks[un]) for un in units}
    BO = {un: _dot_nt(XR[un], Sb[un]) for un in units}
    Aab = {un: jnp.where(strict, APb[un][0:C], 0.0) for un in units}
    Aak = {un: jnp.where(strict, APk[un][0:C], 0.0).astype(bf16) for un in units}
    Pab = {un: jnp.where(incl, APb[un][C:2 * C], 0.0).astype(bf16) for un in units}
    Pak = {un: jnp.where(incl, APk[un][C:2 * C], 0.0).astype(bf16) for un in units}

    Tm = {un: eye + jnp.where(base, Aab[un], 0.0) for un in units}
    for off in offs:
        Xs = {un: _dot(jnp.where(off, Aab[un], 0.0).astype(bf16), stack(Tm[un])) for un in units}
        Tm = {un: Tm[un] + _dot(Tm[un].astype(bf16), stack(Xs[un])) for un in units}

    AV = {un: _dot(jnp.concatenate([Aak[un], Pak[un]], axis=0), Vs[un]) for un in units}
    Bm = {un: BO[un][0:C] + AV[un][0:C] for un in units}
    U = {un: _dot(Tm[un].astype(bf16), stack(Bm[un])) for un in units}
    Om = {un: BO[un][C:2 * C] + AV[un][C:2 * C] + _dot(Pab[un], stack(U[un])) for un in units}
    for un in units:
        bb, p = un
        upd = _dot_tn(jnp.concatenate([U[un].astype(bf16), Vb[un]], axis=0),
                      jnp.concatenate([part(un, "beh").astype(bf16), part(un, "kh").astype(bf16)], axis=0))
        s_ref[bb * RWKV_PAIRS + p] = jnp.where(blockdiag, S[un] * part(un, "etot") + upd, 0.0)

    @pl.when(emit)
    def _():
        for bb in range(nb):
            o_ref[0, bb] = jnp.concatenate([Om[bb, p] for p in range(RWKV_PAIRS)], axis=1)
            bo_ref[0, bb] = pre[bb]["bonus"]


def _split_dot_left(w, x):
    hi = x.astype(bf16)
    lo = (x - hi.astype(f32)).astype(bf16)
    return _dot(w, hi) + _dot(w, lo)


def _rwkv(z, w0, a0, w2p, a2p, k_k, k_a, r_k, ctx):
    B, L, _ = z.shape
    T = L - ctx
    nsub = RWKV_CHUNKS_PER_STEP
    while ctx % (nsub * RWKV_CHUNK) or T % (nsub * RWKV_CHUNK):
        nsub //= 2
    C = nsub * RWKV_CHUNK
    nc, nl = ctx // C, T // C
    W = RWKV_WIDTH

    def chunk(d, s):
        fwd = jnp.where(s < nc, nl + s, s - nc)
        bwd = jnp.where(s < nc, nl + nc - 1 - s, nl - 1 - (s - nc))
        return jnp.where(d == 0, fwd, bwd)

    def ochunk(d, s):
        sl = jnp.maximum(s - nc, 0)
        return jnp.where(d == 0, sl, nl - 1 - sl)

    nb = RWKV_SAMPLES_PER_STEP if B % RWKV_SAMPLES_PER_STEP == 0 else 1
    zblk = lambda c0, w: pl.BlockSpec((nb, C, w), lambda b, d, s: (b, chunk(d, s), c0 // w))
    dpar = lambda r: pl.BlockSpec((1, r, W), lambda b, d, s: (d, 0, 0))
    par = pl.BlockSpec((1, W), lambda b, d, s: (0, 0))
    oblk = pl.BlockSpec((1, nb, C, W), lambda b, d, s: (d, b, ochunk(d, s), 0))
    G = 2 * RWKV_HEAD_DIM
    return pl.pallas_call(
        functools.partial(_rwkv_kernel, nc=nc, nb=nb, nsub=nsub),
        grid=(B // nb, 2, nc + nl),
        in_specs=[zblk(C_RWK, W), zblk(C_RWV, W), zblk(C_RWR, W), zblk(C_LORA, 256),
                  dpar(1), dpar(1), dpar(G), dpar(G), par, par, par],
        out_specs=[oblk, oblk],
        out_shape=[jax.ShapeDtypeStruct((2, B, T, W), f32), jax.ShapeDtypeStruct((2, B, T, W), f32)],
        scratch_shapes=[pltpu.VMEM((nb * RWKV_PAIRS, G, G), f32)],
        compiler_params=_params(("arbitrary", "arbitrary", "arbitrary")),
        name="rwkv",
    )(z, z, z, z, w0, a0, w2p, a2p, k_k, k_a, r_k)


def _merge_kernel(x_ref, ret_ref, rw0_ref, rw1_ref, b0_ref, b1_ref, gate_ref, retg_ref, gd_ref, mod_ref,
                  ng_ref, gn_ref, lng_ref, lnb_ref, g2_ref, wbr_ref, wbw_ref, wout_ref, wrh_ref, wrl_ref,
                  x1_ref, h2_ref, lt_ref):
    hd = RET_HEAD_DIM
    ret = ret_ref[0]
    parts = []
    for hh in range(RET_HEADS):
        xh = ret[:, hh * hd:(hh + 1) * hd]
        mu = jnp.mean(xh, axis=-1, keepdims=True)
        dv = xh - mu
        var = jnp.mean(dv * dv, axis=-1, keepdims=True)
        parts.append(dv * lax.rsqrt(var + RET_EPS))
    yr = retg_ref[0].astype(f32) * (jnp.concatenate(parts, axis=1) * gn_ref[...])
    y_ret = _bdot(yr, wbr_ref[...])
    W = RWKV_WIDTH
    gi = lax.broadcasted_iota(jnp.int32, (W, W), 0)
    gj = lax.broadcasted_iota(jnp.int32, (W, W), 1)
    ones_bd = ((gi >> 6) == (gj >> 6)).astype(bf16)
    o = rw0_ref[0, 0] + rw1_ref[0, 0]
    mu = _split_dot(o, ones_bd) * (1.0 / RWKV_HEAD_DIM)
    dv = o - mu
    var = _bdot(dv * dv, ones_bd) * (1.0 / RWKV_HEAD_DIM)
    yw = dv * lax.rsqrt(var + RWKV_EPS) * lng_ref[...] + lnb_ref[...]
    gate = _dot(gd_ref[0], g2_ref[...])
    yw = (yw + b0_ref[0, 0] + b1_ref[0, 0]) * gate
    y_rw = _bdot(yw, wbw_ref[...])
    D = y_ret.shape[1]
    g = gate_ref[0].astype(f32)
    m = g[:, :D] * y_ret + g[:, D:] * y_rw
    y = _bdot(m, wout_ref[...])

    def rms(v, gg):
        return v * lax.rsqrt(jnp.mean(v * v, axis=-1, keepdims=True) + NORM_EPS) * gg

    x1 = x_ref[0] + mod_ref[0, 0:1, :] * rms(y, ng_ref[0:1, :])
    x1_ref[0] = x1
    h2 = rms(x1, ng_ref[1:2, :]) * (1.0 + mod_ref[0, 2:3, :]) + mod_ref[0, 1:2, :]
    h2b = h2.astype(bf16)
    h2_ref[0] = h2b
    h2l = (h2 - h2b.astype(f32)).astype(bf16)
    lgt = _dot(h2b, wrh_ref[...]) + _dot(h2l, wrh_ref[...]) + _dot(h2b, wrl_ref[...])
    lt_ref[0] = lgt.T[0:N_EXPERTS, :]


def _merge(x, ret_o, rw_o, bonus, z, mod2, ng12, gn, lng, lnb, g2, wbr, wbw, wout, wrh, wrl):
    B, T, D = x.shape
    tm = MERGE_ROWS if T % MERGE_ROWS == 0 else ROWS
    W = RWKV_WIDTH
    row = lambda w: pl.BlockSpec((1, tm, w), lambda b, i: (b, i, 0))
    dblk = lambda dd: pl.BlockSpec((1, 1, tm, W), lambda b, i: (dd, b, i, 0))
    zblk = lambda c0, w: pl.BlockSpec((1, tm, w), lambda b, i: (b, i, c0 // w))
    full = lambda a: pl.BlockSpec(a.shape, lambda b, i: (0,) * a.ndim)
    return pl.pallas_call(
        _merge_kernel,
        grid=(B, T // tm),
        in_specs=[row(D), row(W), dblk(0), dblk(1), dblk(0), dblk(1),
                  zblk(C_MERGE, 2 * D), zblk(C_RETG, W), zblk(C_LORA + 128, 128),
                  pl.BlockSpec((1, 3, D), lambda b, i: (b, 0, 0)),
                  full(ng12), full(gn), full(lng), full(lnb), full(g2), full(wbr), full(wbw), full(wout),
                  full(wrh), full(wrl)],
        out_specs=[row(D), row(D), pl.BlockSpec((1, N_EXPERTS, tm), lambda b, i: (b, 0, i))],
        out_shape=[jax.ShapeDtypeStruct((B, T, D), f32), jax.ShapeDtypeStruct((B, T, D), bf16),
                   jax.ShapeDtypeStruct((B, N_EXPERTS, T), f32)],
        compiler_params=_params(("arbitrary", "arbitrary")),
        name="merge",
    )(x, ret_o, rw_o, rw_o, bonus, bonus, z, z, z, mod2, ng12, gn, lng, lnb, g2, wbr, wbw, wout, wrh, wrl)


def _route_kernel(lt_ref, slot_ref, rt_ref, cnt_ref, slotf_ref, gate_ref, *, cap):
    B, NE, T = lt_ref.shape
    lg = lt_ref[...]
    mx = jnp.max(lg, axis=1, keepdims=True)
    ex = jnp.exp(lg - mx)
    aff = (ex / jnp.sum(ex, axis=1, keepdims=True)).reshape(B * NE, T)
    E = B * NE

    def count_ge(cand):
        return jnp.sum((aff >= cand).astype(f32), axis=1, keepdims=True)

    def exp_step(_, kk):
        k_lo, k_hi = kk
        km = jnp.floor((k_lo + k_hi) * 0.5)
        ok = count_ge(jnp.exp2(-km)) >= cap
        return jnp.where(ok, k_lo, km), jnp.where(ok, km, k_hi)

    k_lo, k_hi = lax.fori_loop(0, ROUTE_EXP_STEPS, exp_step,
                               (jnp.full((E, 1), -1.0, f32), jnp.full((E, 1), ROUTE_MAX_EXP, f32)))
    lo0 = jnp.where(k_hi >= ROUTE_MAX_EXP, 0.0, jnp.exp2(-k_hi))
    hi0 = jnp.exp2(-k_lo)

    def val_step(_, lh):
        lo, hi = lh
        mid = (lo + hi) * 0.5
        ok = count_ge(mid) >= cap
        return jnp.where(ok, mid, lo), jnp.where(ok, hi, mid)

    lo, hi = lax.fori_loop(0, ROUTE_VAL_STEPS, val_step, (lo0, hi0))
    gt = aff >= hi
    eq = (aff >= lo) & (aff < hi)
    need = cap - jnp.sum(gt.astype(f32), axis=1, keepdims=True)
    tri = (lax.broadcasted_iota(jnp.int32, (T, T), 0) < lax.broadcasted_iota(jnp.int32, (T, T), 1)).astype(bf16)
    eq_before = _dot(eq.astype(bf16), tri)
    sel = gt | (eq & (eq_before < need))
    slot = _dot(sel.astype(bf16), tri)
    slot_f = jnp.where(sel, slot, -1.0)
    slot_ref[...] = slot_f.astype(jnp.int32).reshape(B, NE, T)
    before = (lax.broadcasted_iota(jnp.int32, (T, 128), 0)
              < lax.broadcasted_iota(jnp.int32, (T, 128), 1) * GATHER_TILE).astype(bf16)
    cnt_ref[...] = _dot(sel.astype(bf16), before).astype(jnp.int32).reshape(B, NE, 128)
    slotf_ref[...] = slot_f.reshape(B, NE, T)
    gate_ref[...] = jnp.where(sel, aff, 0.0).reshape(B, NE, T)

    def transpose_sample(b, carry):
        packed = jnp.concatenate([slotf_ref[b], gate_ref[b], jnp.zeros((128 - 2 * NE, T), f32)], axis=0)
        rt_ref[b] = packed.T.astype(bf16)
        return carry

    lax.fori_loop(0, B, transpose_sample, 0)


def _route(lt, cap):
    B, E, T = lt.shape
    assert cap <= 256 and T % GATHER_TILE == 0 and T // GATHER_TILE < 128
    return pl.pallas_call(
        functools.partial(_route_kernel, cap=cap),
        grid=(1,),
        in_specs=[pl.BlockSpec((B, E, T), lambda i: (0, 0, 0))],
        out_specs=[pl.BlockSpec((B, E, T), lambda i: (0, 0, 0)), pl.BlockSpec((B, T, 128), lambda i: (0, 0, 0)),
                   pl.BlockSpec((B, E, 128), lambda i: (0, 0, 0))],
        out_shape=[jax.ShapeDtypeStruct((B, E, T), jnp.int32), jax.ShapeDtypeStruct((B, T, 128), bf16),
                   jax.ShapeDtypeStruct((B, E, 128), jnp.int32)],
        scratch_shapes=[pltpu.VMEM((B, E, T), f32), pltpu.VMEM((B, E, T), f32)],
        compiler_params=_params(("arbitrary",)),
        name="route",
    )(lt)


def _ffn_kernel(cnt_ref, slot_ref, h_ref, wg_ref, wu_ref, wd_ref, o_ref, wgb_ref, wub_ref, wdb_ref, xg_ref, *, cap):
    T = h_ref.shape[1]
    e = pl.program_id(0)

    @pl.when(pl.program_id(1) == 0)
    def _():
        wgb_ref[...] = wg_ref[0].astype(bf16)
        wub_ref[...] = wu_ref[0].astype(bf16)
        wdb_ref[...] = wd_ref[0].astype(bf16)

    nb = h_ref.shape[0]
    GT, GW, GA = GATHER_TILE, GATHER_WINDOW, GATHER_ALIGN
    nt = T // GT
    for bb in range(nb):
        b = pl.program_id(1) * nb + bb
        base = [pl.multiple_of(lax.shift_left(lax.shift_right_logical(cnt_ref[b, e, j], GATHER_ALIGN_LOG2),
                                              GATHER_ALIGN_LOG2), GA) for j in range(nt)]
        fits = cnt_ref[b, e, 1] - base[0] <= GW
        for j in range(1, nt):
            fits = fits & (cnt_ref[b, e, j + 1] - base[j] <= GW)

        @pl.when(fits)
        def _():
            xg_ref[bb] = jnp.zeros(xg_ref.shape[1:], f32)
            for j in range(nt):
                rows = base[j] + lax.broadcasted_iota(jnp.int32, (GW, GT), 0)
                onehot = (slot_ref[bb, 0, j:j + 1, :] == rows).astype(bf16)
                xg_ref[bb, pl.ds(base[j], GW), :] += _dot(onehot, h_ref[bb, j * GT:(j + 1) * GT, :])

        @pl.when(jnp.logical_not(fits))
        def _():
            rows = lax.broadcasted_iota(jnp.int32, (cap, GT), 0)
            acc = jnp.zeros((cap, h_ref.shape[2]), f32)
            for j in range(nt):
                onehot = (slot_ref[bb, 0, j:j + 1, :] == rows).astype(bf16)
                acc = acc + _dot(onehot, h_ref[bb, j * GT:(j + 1) * GT, :])
            xg_ref[bb, 0:cap, :] = acc

    xg = jnp.concatenate([xg_ref[bb, 0:cap, :].astype(bf16) for bb in range(nb)], axis=0)
    hg = _dot(xg, wgb_ref[...])
    hu = _dot(xg, wub_ref[...])
    hid = (hg * _sigmoid(hg) * hu).astype(bf16)
    out = _dot(hid, wdb_ref[...]).astype(bf16)
    for bb in range(nb):
        o_ref[bb, 0] = out[bb * cap:(bb + 1) * cap]


def _ffn(cnt, slot4, h2, wg, wu, wd, cap):
    B, T, D = h2.shape
    E, _, F = wg.shape
    nb = FFN_SAMPLES_PER_STEP if B % FFN_SAMPLES_PER_STEP == 0 else 1
    grid_spec = pltpu.PrefetchScalarGridSpec(
        num_scalar_prefetch=1,
        grid=(E, B // nb),
        in_specs=[pl.BlockSpec((nb, 1, T // GATHER_TILE, GATHER_TILE), lambda e, b, c: (b, e, 0, 0)),
                  pl.BlockSpec((nb, T, D), lambda e, b, c: (b, 0, 0)),
                  pl.BlockSpec((1, D, F), lambda e, b, c: (e, 0, 0)),
                  pl.BlockSpec((1, D, F), lambda e, b, c: (e, 0, 0)),
                  pl.BlockSpec((1, F, D), lambda e, b, c: (e, 0, 0))],
        out_specs=pl.BlockSpec((nb, 1, cap, D), lambda e, b, c: (b, e, 0, 0)),
        scratch_shapes=[pltpu.VMEM((D, F), bf16), pltpu.VMEM((D, F), bf16), pltpu.VMEM((F, D), bf16),
                        pltpu.VMEM((nb, cap + GATHER_WINDOW, D), f32)])
    return pl.pallas_call(
        functools.partial(_ffn_kernel, cap=cap),
        grid_spec=grid_spec,
        out_shape=jax.ShapeDtypeStruct((B, E, cap, D), bf16),
        compiler_params=_params(("arbitrary", "arbitrary"), FFN_VMEM_LIMIT),
        name="ffn",
    )(cnt, slot4, h2, wg, wu, wd)


def _combine_kernel(cnt_ref, rt_ref, eo_ref, x1_ref, mod_ref, ng_ref, o_ref, y_ref, *, cap):
    b = pl.program_id(0)
    i = pl.program_id(1)
    rt = rt_ref[0].astype(f32)
    tm = rt.shape[0]
    E = eo_ref.shape[1]
    W = min(SCATTER_WINDOW, cap)
    GS = SCATTER_GROUP
    ST = GATHER_TILE
    for st in range(tm // ST):
        t = i * (tm // ST) + st
        r = rt[st * ST:(st + 1) * ST]
        base, fits = [], None
        for e in range(E):
            lo = cnt_ref[b, e, t]
            be = jnp.minimum(lax.shift_left(lax.shift_right_logical(lo, GATHER_ALIGN_LOG2), GATHER_ALIGN_LOG2),
                             cap - W)
            base.append(pl.multiple_of(be, GATHER_ALIGN))
            ok = cnt_ref[b, e, t + 1] - be <= W
            fits = ok if fits is None else fits & ok

        def scatter_matrix(e, first, width, r=r):
            cols = (first + lax.broadcasted_iota(jnp.int32, (ST, width), 1)).astype(f32)
            return jnp.where(r[:, e:e + 1] == cols, r[:, E + e:E + e + 1], 0.0).astype(bf16)

        @pl.when(fits)
        def _():
            y = jnp.zeros((ST, y_ref.shape[1]), f32)
            for g in range(0, E, GS):
                p = jnp.concatenate([scatter_matrix(e, base[e], W) for e in range(g, g + GS)], axis=1)
                rows = jnp.concatenate([eo_ref[0, e, pl.ds(base[e], W), :] for e in range(g, g + GS)], axis=0)
                y = y + _dot(p, rows)
            y_ref[st * ST:(st + 1) * ST, :] = y

        @pl.when(jnp.logical_not(fits))
        def _():
            y = jnp.zeros((ST, y_ref.shape[1]), f32)
            for e in range(E):
                y = y + _dot(scatter_matrix(e, 0, cap), eo_ref[0, e])
            y_ref[st * ST:(st + 1) * ST, :] = y

    y = y_ref[...]
    yn = y * lax.rsqrt(jnp.mean(y * y, axis=-1, keepdims=True) + NORM_EPS) * ng_ref[...]
    o_ref[0] = x1_ref[0] + mod_ref[0] * yn


def _combine(cnt, rt, eo, x1, g2mod, ng3, cap):
    B, T, D = x1.shape
    E = eo.shape[1]
    tm = 512 if T % 512 == 0 else T
    assert E % SCATTER_GROUP == 0 and tm % GATHER_TILE == 0
    grid_spec = pltpu.PrefetchScalarGridSpec(
        num_scalar_prefetch=1,
        grid=(B, T // tm),
        in_specs=[pl.BlockSpec((1, tm, 128), lambda b, i, c: (b, i, 0)),
                  pl.BlockSpec((1, E, cap, D), lambda b, i, c: (b, 0, 0, 0)),
                  pl.BlockSpec((1, tm, D), lambda b, i, c: (b, i, 0)),
                  pl.BlockSpec((1, 1, D), lambda b, i, c: (b, 0, 0)),
                  pl.BlockSpec((1, D), lambda b, i, c: (0, 0))],
        out_specs=pl.BlockSpec((1, tm, D), lambda b, i, c: (b, i, 0)),
        scratch_shapes=[pltpu.VMEM((tm, D), f32)])
    return pl.pallas_call(
        functools.partial(_combine_kernel, cap=cap),
        grid_spec=grid_spec,
        out_shape=jax.ShapeDtypeStruct((B, T, D), f32),
        compiler_params=_params(("arbitrary", "arbitrary")),
        name="combine",
    )(cnt, rt, eo, x1, g2mod, ng3)


def _permute_columns(w):
    sk, sv, rk, rv, wd, ad = 0, 512, 1024, 1536, 2048, 2112
    q0 = 2176
    rq, rg, rr, gd, mg = q0, q0 + 512, q0 + 1024, q0 + 1536, q0 + 1664
    order = [(mg, 2048), (sk, 512), (sv, 512), (rq, 512), (rg, 512), (rk, 512), (rv, 512), (rr, 512),
             (wd, 64), (ad, 64), (gd, 128)]
    parts = [w[:, a:a + n] for a, n in order]
    parts.append(jnp.zeros((w.shape[0], IN_COLS - USED_COLS), w.dtype))
    return jnp.concatenate(parts, axis=1)


def _rope_tables(T, CT):
    t = jnp.arange(T)
    nfreq = RET_HEAD_DIM // 4
    inv = ROPE_BASE ** (-jnp.arange(nfreq, dtype=f32) / nfreq)
    ang = jnp.concatenate([(t // GRID_W).astype(f32)[:, None] * inv,
                           (t % GRID_W).astype(f32)[:, None] * inv], axis=-1)
    cos, sin = jnp.cos(ang), jnp.sin(ang)
    cosf = jnp.concatenate([cos, cos], axis=1)
    sinf = jnp.concatenate([-sin, sin], axis=1)
    return (jnp.concatenate([cosf, jnp.ones((CT, RET_HEAD_DIM), f32)], axis=0),
            jnp.concatenate([sinf, jnp.zeros((CT, RET_HEAD_DIM), f32)], axis=0))


def kernel(x, c, ctx, c_ctx, w_mod, b_mod, norm_g, w_in, ret_log_decay, ret_gn_g, rwkv_mu, rwkv_k_k, rwkv_k_a,
           rwkv_r_k, rwkv_w0, rwkv_w2, rwkv_a0, rwkv_a2, rwkv_g2, rwkv_ln_g, rwkv_ln_b, w_br_ret, w_br_rwkv,
           w_out, w_router, w_gate, w_up, w_down):
    B, T, D = x.shape
    CT = ctx.shape[1]
    assert w_mod.shape[0] == 1 and D == D_MODEL
    assert CT % ROWS == 0 and T % ROWS == 0 and T % GRID_W == 0
    cap = CAPACITY_FACTOR * T // N_EXPERTS
    assert cap % 8 == 0

    mrows = -(-(B + 1) // 8) * 8
    cc = jnp.zeros((mrows, D), f32).at[:B].set(c).at[B].set(c_ctx)
    mod = _modulation(cc, w_mod[0], b_mod[0])
    lat = mod[:B].reshape(B, N_MOD, D)
    cm = jnp.broadcast_to(mod[B].reshape(1, N_MOD, D), (B, N_MOD, D))
    modrows = jnp.concatenate([lat[:, 0:2], cm[:, 0:2]], axis=1)

    w_perm = _permute_columns(w_in[0]).astype(bf16)
    mu = rwkv_mu[0]
    ss = 2 * RWKV_WIDTH + DECAY_LORA + ICLR_LORA
    mu_full = jnp.zeros((2, IN_COLS), f32)
    mu_full = mu_full.at[:, C_RWK:C_RWK + 1024].set(mu[:, 0:1024])
    mu_full = mu_full.at[:, C_RWR:C_RWR + 512].set(mu[:, ss:ss + 512])
    mu_full = mu_full.at[:, C_LORA:C_LORA + 128].set(mu[:, 1024:ss])
    mu_full = mu_full.at[:, C_LORA + 128:USED_COLS].set(mu[:, ss + 512:])
    cosf, sinf = _rope_tables(T, CT)
    z = _inproj(x, ctx, modrows, norm_g[0, 0:1], w_perm, mu_full, cosf, sinf)

    lg = -jnp.exp(ret_log_decay[0].astype(f32))
    ret_o = _retention(lg, z, CT)

    G = 2 * RWKV_HEAD_DIM
    w2p = jnp.zeros((2, G, RWKV_WIDTH), f32).at[:, :DECAY_LORA].set(rwkv_w2[0]).astype(bf16)
    a2p = jnp.zeros((2, G, RWKV_WIDTH), f32).at[:, DECAY_LORA:].set(rwkv_a2[0]).astype(bf16)
    rw_o, bonus = _rwkv(z, rwkv_w0[0][:, None, :], rwkv_a0[0][:, None, :], w2p, a2p,
                        rwkv_k_k[0][None], rwkv_k_a[0][None], rwkv_r_k[0][None], CT)

    mod2 = jnp.stack([lat[:, 2], lat[:, 3], lat[:, 4]], axis=1)
    wr_pad = jnp.zeros((D, 128), f32).at[:, :N_EXPERTS].set(w_router[0])
    wr_hi = wr_pad.astype(bf16)
    wr_lo = (wr_pad - wr_hi.astype(f32)).astype(bf16)
    x1, h2, lt = _merge(x, ret_o, rw_o, bonus, z, mod2, norm_g[0, 1:3], ret_gn_g[0][None], rwkv_ln_g[0][None],
                        rwkv_ln_b[0][None], rwkv_g2[0].astype(bf16), w_br_ret[0].astype(bf16),
                        w_br_rwkv[0].astype(bf16), w_out[0].astype(bf16), wr_hi, wr_lo)

    slot, rt, cnt = _route(lt, cap)
    cnt = cnt[:, :, :T // GATHER_TILE + 1]
    eo = _ffn(cnt, slot.reshape(B, N_EXPERTS, T // GATHER_TILE, GATHER_TILE), h2, w_gate[0], w_up[0], w_down[0], cap)
    return _combine(cnt, rt, eo, x1, lat[:, 5:6], norm_g[0, 3:4], cap)
```

```python
import functools

import jax
import jax.numpy as jnp
from jax import lax
from jax.experimental import pallas as pl
from jax.experimental.pallas import tpu as pltpu

f32 = jnp.float32
bf16 = jnp.bfloat16

D_MODEL = 1024
GRID_W = 64
RET_HEAD_DIM = 128
RET_WIDTH = 512
RET_HEADS = 4
RET_CHUNK = 128
RET_EPS = 1e-5
ROPE_BASE = 10000.0
RWKV_HEAD_DIM = 64
RWKV_WIDTH = 512
RWKV_PAIRS = 4
RWKV_CHUNK = 64
RWKV_CHUNKS_PER_STEP = 4
RWKV_SAMPLES_PER_STEP = 8
DECAY_LORA = 64
ICLR_LORA = 64
GATE_LORA = 128
RWKV_EPS = 64e-5
N_EXPERTS = 16
EXPERT_FF = 1024
CAPACITY_FACTOR = 2
N_MOD = 6
NORM_EPS = 1e-6

C_MERGE, C_RETK, C_RETV, C_RETQ, C_RETG = 0, 2048, 2560, 3072, 3584
C_RWK, C_RWV, C_RWR, C_LORA = 4096, 4608, 5120, 5632
USED_COLS = 5888
TN = 512
IN_COLS = -(-USED_COLS // TN) * TN
ROWS = 256
STEP_ROWS = 768
POST_ROWS = 128
QUERY_ROWS = 1024
MERGE_ROWS = 512

ROUTE_MAX_EXP = 126.0
ROUTE_EXP_STEPS = 8
ROUTE_VAL_STEPS = 26

VMEM_LIMIT = 56 * 1024 * 1024
FFN_SAMPLES_PER_STEP = 2
GATHER_TILE = 256
GATHER_ALIGN_LOG2 = 4
GATHER_ALIGN = 1 << GATHER_ALIGN_LOG2
GATHER_WINDOW = 80
SCATTER_WINDOW = 64
SCATTER_GROUP = 4
FFN_VMEM_LIMIT = 62 * 1024 * 1024


def _dot(a, b):
    return jnp.dot(a, b, preferred_element_type=f32)


def _dot_nt(a, b):
    return lax.dot_general(a, b, (((1,), (1,)), ((), ())), preferred_element_type=f32)


def _dot_tn(a, b):
    return lax.dot_general(a, b, (((0,), (0,)), ((), ())), preferred_element_type=f32)


def _bdot(a, b):
    return _dot(a.astype(bf16), b.astype(bf16))


def _split_dot(x, w):
    hi = x.astype(bf16)
    lo = (x - hi.astype(f32)).astype(bf16)
    return _dot(hi, w) + _dot(lo, w)


def _sigmoid(x):
    return 0.5 * jnp.tanh(0.5 * x) + 0.5


def _params(sem, limit=VMEM_LIMIT):
    return pltpu.CompilerParams(dimension_semantics=sem, vmem_limit_bytes=limit)


def _mod_kernel(c_ref, w_ref, b_ref, o_ref):
    c = c_ref[...]
    s = c * jax.nn.sigmoid(c)
    o_ref[...] = _bdot(s, w_ref[...]) + b_ref[...]


def _modulation(cc, w_mod, b_mod):
    m, d = cc.shape
    n = w_mod.shape[1]
    tn = 512
    return pl.pallas_call(
        _mod_kernel,
        grid=(n // tn,),
        in_specs=[pl.BlockSpec((m, d), lambda j: (0, 0)),
                  pl.BlockSpec((d, tn), lambda j: (0, j)),
                  pl.BlockSpec((1, tn), lambda j: (0, j))],
        out_specs=pl.BlockSpec((m, tn), lambda j: (0, j)),
        out_shape=jax.ShapeDtypeStruct((m, n), f32),
        compiler_params=_params(("arbitrary",)),
        name="mod",
    )(cc, w_mod, b_mod.reshape(1, n))


def _inproj_kernel(x_ref, c_ref, mod_ref, g_ref, w_ref, mu_ref, cos_ref, sin_ref, o_ref, h_ref, z_ref):
    n = pl.program_id(1)
    T = x_ref.shape[1]
    L = T + c_ref.shape[1]
    nlat = T // ROWS
    nchunk = L // ROWS
    PAD = 8

    SR = STEP_ROWS if L % STEP_ROWS == 0 else ROWS

    @pl.when(n == 0)
    def _():
        z_ref[0:PAD, :] = jnp.zeros((PAD, TN), f32)
        z_ref[PAD + L:PAD + L + PAD, :] = jnp.zeros((PAD, TN), f32)

        def norm_chunk(src_ref, s0, r0, o):
            xb = src_ref[0, pl.ds(s0, ROWS), :]
            y = xb * lax.rsqrt(jnp.mean(xb * xb, axis=-1, keepdims=True) + NORM_EPS) * g_ref[...]
            sh = mod_ref[0, o:o + 1, :]
            sc = mod_ref[0, o + 1:o + 2, :]
            h_ref[pl.ds(r0, ROWS), :] = (y * (1.0 + sc) + sh).astype(bf16)

        def lat_body(i, carry):
            r0 = pl.multiple_of(i * ROWS, ROWS)
            norm_chunk(x_ref, r0, r0, 0)
            return carry

        lax.fori_loop(0, nlat, lat_body, 0)
        for j in range(nlat, nchunk):
            norm_chunk(c_ref, j * ROWS - T, j * ROWS, 2)

    PR = POST_ROWS
    QR = QUERY_ROWS

    def rope(scale):
        def post(z, r0):
            if scale != 1.0:
                z = z * scale
            cs = cos_ref[pl.ds(r0, z.shape[0]), :]
            sn = sin_ref[pl.ds(r0, z.shape[0]), :]
            parts = []
            for hh in range(TN // RET_HEAD_DIM):
                zh = z[:, hh * RET_HEAD_DIM:(hh + 1) * RET_HEAD_DIM]
                parts.append(zh * cs + pltpu.roll(zh, RET_HEAD_DIM // 2, 1) * sn)
            return jnp.concatenate(parts, axis=1)
        return post

    def shifted(r0, at_boundary):
        win = z_ref[pl.ds(r0, PR + 2 * PAD), :]
        prev = win[PAD - 1:PAD - 1 + PR]
        z = win[PAD:PAD + PR]
        nxt = win[PAD + 1:PAD + 1 + PR]
        if at_boundary:
            rid = r0 + lax.broadcasted_iota(jnp.int32, (PR, 1), 0)
            prev = jnp.where(rid == T, 0.0, prev)
            nxt = jnp.where(rid == T - 1, 0.0, nxt)
        mu0 = mu_ref[0:1, :]
        mu1 = mu_ref[1:2, :]
        return (1.0 - mu0 - mu1) * z + mu0 * prev + mu1 * nxt

    def lora_act(zs):
        lane = lax.broadcasted_iota(jnp.int32, (1, TN), 1)
        return jnp.where(lane < DECAY_LORA, jnp.tanh(zs),
                         jnp.where(lane < DECAY_LORA + ICLR_LORA, zs, jax.nn.sigmoid(zs)))

    def product(r0, rows=SR):
        return _dot(h_ref[pl.ds(r0, rows), :], w_ref[...])

    def direct(post, latent_only=False):
        rows, total = (QR, T) if latent_only and T % QR == 0 else (SR, L)

        def body(i, carry):
            r0 = pl.multiple_of(i * rows, rows)
            o_ref[0, pl.ds(r0, rows), :] = post(product(r0, rows), r0).astype(bf16)
            return carry
        lax.fori_loop(0, total // rows, body, 0)
        if total < L:
            o_ref[0, total:L, :] = jnp.zeros((L - total, TN), bf16)

    def via_buffer(act):
        def mm_body(i, carry):
            r0 = pl.multiple_of(i * SR, SR)
            z_ref[pl.ds(PAD + r0, SR), :] = product(r0)
            return carry
        lax.fori_loop(0, L // SR, mm_body, 0)

        def post(r0, at_boundary):
            o_ref[0, pl.ds(r0, PR), :] = act(shifted(r0, at_boundary)).astype(bf16)

        def post_body(i, carry):
            post(pl.multiple_of(i * PR, PR), False)
            return carry

        seam = T // PR
        lax.fori_loop(0, seam - 1, post_body, 0)
        post((seam - 1) * PR, True)
        post(seam * PR, True)
        lax.fori_loop(seam + 1, L // PR, post_body, 0)

    t = lambda c: c // TN
    pl.when(n < t(C_RETK))(lambda: direct(lambda z, r0: _sigmoid(z), latent_only=True))
    pl.when((n >= t(C_RETK)) & (n < t(C_RETV)))(lambda: direct(rope(RET_HEAD_DIM ** -0.5)))
    pl.when((n >= t(C_RETV)) & (n < t(C_RETQ)))(lambda: direct(lambda z, r0: z))
    pl.when((n >= t(C_RETQ)) & (n < t(C_RETG)))(lambda: direct(rope(1.0), latent_only=True))
    pl.when((n >= t(C_RETG)) & (n < t(C_RWK)))(lambda: direct(lambda z, r0: z * _sigmoid(z), latent_only=True))
    pl.when((n >= t(C_RWK)) & (n < t(C_LORA)))(lambda: via_buffer(lambda zs: zs))
    pl.when(n >= t(C_LORA))(lambda: via_buffer(lora_act))


def _inproj(x, ctx, modrows, g0, w_perm, mu_full, cosf, sinf):
    B, T, D = x.shape
    CT = ctx.shape[1]
    L = T + CT
    nt = IN_COLS // TN
    return pl.pallas_call(
        _inproj_kernel,
        grid=(B, nt),
        in_specs=[pl.BlockSpec((1, T, D), lambda b, n: (b, 0, 0)),
                  pl.BlockSpec((1, CT, D), lambda b, n: (b, 0, 0)),
                  pl.BlockSpec((1, 4, D), lambda b, n: (b, 0, 0)),
                  pl.BlockSpec((1, D), lambda b, n: (0, 0)),
                  pl.BlockSpec((D, TN), lambda b, n: (0, n)),
                  pl.BlockSpec((2, TN), lambda b, n: (0, n)),
                  pl.BlockSpec((L, RET_HEAD_DIM), lambda b, n: (0, 0)),
                  pl.BlockSpec((L, RET_HEAD_DIM), lambda b, n: (0, 0))],
        out_specs=pl.BlockSpec((1, L, TN), lambda b, n: (b, 0, n)),
        out_shape=jax.ShapeDtypeStruct((B, L, IN_COLS), bf16),
        scratch_shapes=[pltpu.VMEM((L, D), bf16), pltpu.VMEM((L + 16, TN), f32)],
        compiler_params=_params(("arbitrary", "arbitrary")),
        name="inproj",
    )(x, ctx, modrows, g0, w_perm, mu_full, cosf, sinf)


def _ret_kernel(lg_ref, q_ref, k_ref, v_ref, o_ref, r_ref, tab_ref, *, ctx):
    L = q_ref.shape[1]
    Cc = RET_CHUNK
    hd = RET_HEAD_DIM
    lat = L - ctx
    nc = ctx // Cc
    nl = lat // Cc
    combos = [(h, d) for h in range(RET_HEADS) for d in (0, 1)]
    INTRA, CROSS, TAIL, DECAY = 0, 1, 2, 3

    @pl.when(pl.program_id(0) == 0)
    def _():
        ii = lax.broadcasted_iota(jnp.int32, (Cc, Cc), 0).astype(f32)
        jj = lax.broadcasted_iota(jnp.int32, (Cc, Cc), 1).astype(f32)
        for idx, (h, d) in enumerate(combos):
            lg = lg_ref[d, h]
            if d == 0:
                diff = ii - jj
                cross = jnp.exp(lg * (ii + 1.0))
                tailw = jnp.exp(lg * (Cc - 1.0 - ii))
            else:
                diff = jj - ii
                cross = jnp.exp(lg * (Cc - ii))
                tailw = jnp.exp(lg * ii)
            tab_ref[idx, INTRA] = jnp.where(diff >= 0, jnp.exp(lg * jnp.maximum(diff, 0.0)), 0.0)
            tab_ref[idx, CROSS] = cross
            tab_ref[idx, TAIL] = tailw
            tab_ref[idx, DECAY] = jnp.exp(jnp.zeros((Cc, Cc), f32) + lg * Cc)

    r_ref[...] = jnp.zeros(r_ref.shape, f32)
    o_ref[...] = jnp.zeros(o_ref.shape, f32)

    def cols(h):
        return slice(h * hd, (h + 1) * hd)

    def update(idx, kc, vc):
        ks = (kc.astype(f32) * tab_ref[idx, TAIL]).astype(bf16)
        r_ref[idx] = r_ref[idx] * tab_ref[idx, DECAY] + _dot_tn(ks, vc)

    def ctx_step(s, carry):
        for idx, (h, d) in enumerate(combos):
            row0 = pl.multiple_of(lat + (s * Cc if d == 0 else (nc - 1 - s) * Cc), Cc)
            update(idx, k_ref[0, pl.ds(row0, Cc), cols(h)], v_ref[0, pl.ds(row0, Cc), cols(h)])
        return carry

    def lat_step(s, carry):
        t0s, qs, ks, vs = [], [], [], []
        for h, d in combos:
            t0 = pl.multiple_of(s * Cc if d == 0 else (nl - 1 - s) * Cc, Cc)
            row0 = t0
            t0s.append(t0)
            qs.append(q_ref[0, pl.ds(row0, Cc), cols(h)])
            ks.append(k_ref[0, pl.ds(row0, Cc), cols(h)])
            vs.append(v_ref[0, pl.ds(row0, Cc), cols(h)])
        n = len(combos)
        sc = [(_dot_nt(qs[i], ks[i]) * tab_ref[i, INTRA]).astype(bf16) for i in range(n)]
        oc = [_dot(qs[i], r_ref[i].astype(bf16)) * tab_ref[i, CROSS] for i in range(n)]
        oi = [_dot(sc[i], vs[i]) for i in range(n)]
        for i, (h, d) in enumerate(combos):
            o_ref[0, pl.ds(t0s[i], Cc), cols(h)] += oi[i] + oc[i]
        for i in range(n):
            update(i, ks[i], vs[i])
        return carry

    lax.fori_loop(0, nc, ctx_step, 0)
    lax.fori_loop(0, nl, lat_step, 0)


def _retention(lg, z, ctx):
    B, L, _ = z.shape
    T = L - ctx
    W = RET_WIDTH
    Cc = RET_CHUNK
    blk = lambda c0: pl.BlockSpec((1, L, W), lambda b: (b, 0, c0 // W))
    nchain = 2 * RET_HEADS
    return pl.pallas_call(
        functools.partial(_ret_kernel, ctx=ctx),
        grid=(B,),
        in_specs=[pl.BlockSpec(memory_space=pltpu.SMEM), blk(C_RETQ), blk(C_RETK), blk(C_RETV)],
        out_specs=pl.BlockSpec((1, T, W), lambda b: (b, 0, 0)),
        out_shape=jax.ShapeDtypeStruct((B, T, W), f32),
        scratch_shapes=[pltpu.VMEM((nchain, Cc, Cc), f32), pltpu.VMEM((nchain, 4, Cc, Cc), f32)],
        compiler_params=_params(("arbitrary",)),
        name="ret",
    )(lg, z, z, z)


def _rwkv_kernel(k_ref, v_ref, r_ref, lo_ref, w0_ref, a0_ref, w2_ref, a2_ref, kk_ref, ka_ref, rk_ref,
                 o_ref, bo_ref, s_ref, *, nc, nb, nsub):
    d = pl.program_id(1)
    s = pl.program_id(2)
    C = RWKV_CHUNK

    @pl.when(s == 0)
    def _():
        s_ref[...] = jnp.zeros(s_ref.shape, f32)

    def sub_chunk(j, carry):
        r0 = pl.multiple_of(jnp.where(d == 1, nsub - 1 - j, j) * C, C)
        rows = lambda ref: ref.at[:, pl.ds(r0, C), :]
        _rwkv_chunk(rows(k_ref), rows(v_ref), rows(r_ref), rows(lo_ref), w0_ref, a0_ref, w2_ref, a2_ref,
                    kk_ref, ka_ref, rk_ref, o_ref.at[:, :, pl.ds(r0, C), :], bo_ref.at[:, :, pl.ds(r0, C), :],
                    s_ref, rev=d == 1, emit=s >= nc, nb=nb)
        return carry

    lax.fori_loop(0, nsub, sub_chunk, 0)


def _rwkv_chunk(k_ref, v_ref, r_ref, lo_ref, w0_ref, a0_ref, w2_ref, a2_ref, kk_ref, ka_ref, rk_ref,
                o_ref, bo_ref, s_ref, *, rev, emit, nb):
    C = RWKV_CHUNK
    G = 2 * RWKV_HEAD_DIM

    ii = lax.broadcasted_iota(jnp.int32, (C, G), 0)
    lane = lax.broadcasted_iota(jnp.int32, (C, G), 1)
    jj = lane & (RWKV_HEAD_DIM - 1)
    head0 = lane < RWKV_HEAD_DIM
    dlt = jnp.where(rev, ii - jj, jj - ii)
    strict = dlt < 0
    incl = dlt <= 0
    eye = (ii == jj).astype(f32)
    gi = lax.broadcasted_iota(jnp.int32, (G, G), 0)
    gj = lax.broadcasted_iota(jnp.int32, (G, G), 1)
    blockdiag = (gi >= RWKV_HEAD_DIM) == (gj >= RWKV_HEAD_DIM)
    ones_bd = blockdiag.astype(bf16)
    tri = incl[:, 0:C].astype(bf16)
    base = (ii >> 1) == (jj >> 1)
    offs = [((ii >> (lv + 1)) == (jj >> (lv + 1))) & ((ii >> lv) != (jj >> lv)) for lv in range(1, 6)]

    def segsum(x):
        return jnp.concatenate(
            [_bdot(x[:, p * G:(p + 1) * G], ones_bd) for p in range(RWKV_PAIRS)], axis=1)

    def stack(x):
        xb = x.astype(bf16)
        zero = jnp.zeros_like(xb)
        return jnp.concatenate([jnp.where(head0, xb, zero), jnp.where(head0, zero, xb)], axis=0)

    W = k_ref.shape[2]
    k_all = k_ref[...].astype(f32).reshape(nb * C, W)
    v_all = v_ref[...].astype(f32).reshape(nb * C, W)
    r_all = r_ref[...].astype(f32).reshape(nb * C, W)
    lo_all = lo_ref[:, :, 0:DECAY_LORA + ICLR_LORA].reshape(nb * C, DECAY_LORA + ICLR_LORA)
    u_all = w0_ref[0] + _dot(lo_all, w2_ref[0])
    softplus = jnp.maximum(-u_all, 0.0) + jnp.log1p(jnp.exp(-jnp.abs(u_all)))
    lw_all = -jnp.exp(-softplus - 0.5)
    a_all = jax.nn.sigmoid(a0_ref[0] + _dot(lo_all, a2_ref[0]))
    kkr = k_all * kk_ref[...]
    kk_all = kkr * lax.rsqrt(segsum(kkr * kkr) + 1e-12)
    kd_all = k_all * (1.0 + (a_all - 1.0) * ka_ref[...])
    be_all = kk_all * a_all
    bonus_all = segsum(r_all * kd_all * rk_ref[...]) * v_all
    pre = []
    for bb in range(nb):
        rows = slice(bb * C, (bb + 1) * C)
        vx, rx, lw, kk, kd, be, bonus = (arr[rows] for arr in (v_all, r_all, lw_all, kk_all, kd_all, be_all,
                                                                bonus_all))
        cum = _split_dot_left(tri, lw)
        tot = jnp.where(rev, cum[0:1, :], cum[C - 1:C, :])
        gneg = jnp.exp(-cum)
        gh = jnp.exp(tot - cum)
        pre.append(dict(alb=-kk * jnp.exp(cum - lw), rb=rx * jnp.exp(cum), beb=be * gneg, kb=kd * gneg,
                        beh=be * gh, kh=kd * gh, etot=jnp.exp(tot), v=vx, bonus=bonus))

    units = [(bb, p) for bb in range(nb) for p in range(RWKV_PAIRS)]
    sl = lambda p: slice(p * G, (p + 1) * G)
    part = lambda un, name: pre[un[0]][name][:, sl(un[1])]
    S = {un: s_ref[un[0] * RWKV_PAIRS + un[1]] for un in units}
    Sb = {un: S[un].astype(bf16) for un in units}
    X = {un: part(un, "alb").astype(bf16) for un in units}
    Rb = {un: part(un, "rb").astype(bf16) for un in units}
    Ybs = {un: stack(part(un, "beb")) for un in units}
    Yks = {un: stack(part(un, "kb")) for un in units}
    Vb = {un: part(un, "v").astype(bf16) for un in units}
    Vs = {un: stack(part(un, "v")) for un in units}

    XR = {un: jnp.concatenate([X[un], Rb[un]], axis=0) for un in units}
    APb = {un: _dot_nt(XR[un], Ybs[un]) for un in units}
    APk = {un: _dot_nt(XR[un], Yks[un]) for un in units}
    BO = {un: _dot_nt(XR[un], Sb[un]) for un in units}
    Aab = {un: jnp.where(strict, APb[un][0:C], 0.0) for un in units}
    Aak = {un: jnp.where(strict, APk[un][0:C], 0.0).astype(bf16) for un in units}
    Pab = {un: jnp.where(incl, APb[un][C:2 * C], 0.0).astype(bf16) for un in units}
    Pak = {un: jnp.where(incl, APk[un][C:2 * C], 0.0).astype(bf16) for un in units}

    Tm = {un: eye + jnp.where(base, Aab[un], 0.0) for un in units}
    for off in offs:
        Xs = {un: _dot(jnp.where(off, Aab[un], 0.0).astype(bf16), stack(Tm[un])) for un in units}
        Tm = {un: Tm[un] + _dot(Tm[un].astype(bf16), stack(Xs[un])) for un in units}

    AV = {un: _dot(jnp.concatenate([Aak[un], Pak[un]], axis=0), Vs[un]) for un in units}
    Bm = {un: BO[un][0:C] + AV[un][0:C] for un in units}
    U = {un: _dot(Tm[un].astype(bf16), stack(Bm[un])) for un in units}
    Om = {un: BO[un][C:2 * C] + AV[un][C:2 * C] + _dot(Pab[un], stack(U[un])) for un in units}
    for un in units:
        bb, p = un
        upd = _dot_tn(jnp.concatenate([U[un].astype(bf16), Vb[un]], axis=0),
                      jnp.concatenate([part(un, "beh").astype(bf16), part(un, "kh").astype(bf16)], axis=0))
        s_ref[bb * RWKV_PAIRS + p] = jnp.where(blockdiag, S[un] * part(un, "etot") + upd, 0.0)

    @pl.when(emit)
    def _():
        for bb in range(nb):
            o_ref[0, bb] = jnp.concatenate([Om[bb, p] for p in range(RWKV_PAIRS)], axis=1)
            bo_ref[0, bb] = pre[bb]["bonus"]


def _split_dot_left(w, x):
    hi = x.astype(bf16)
    lo = (x - hi.astype(f32)).astype(bf16)
    return _dot(jnp.concatenate([w, w], axis=1), jnp.concatenate([hi, lo], axis=0))


def _rwkv(z, w0, a0, w2p, a2p, k_k, k_a, r_k, ctx):
    B, L, _ = z.shape
    T = L - ctx
    nsub = RWKV_CHUNKS_PER_STEP
    while ctx % (nsub * RWKV_CHUNK) or T % (nsub * RWKV_CHUNK):
        nsub //= 2
    C = nsub * RWKV_CHUNK
    nc, nl = ctx // C, T // C
    W = RWKV_WIDTH

    def chunk(d, s):
        fwd = jnp.where(s < nc, nl + s, s - nc)
        bwd = jnp.where(s < nc, nl + nc - 1 - s, nl - 1 - (s - nc))
        return jnp.where(d == 0, fwd, bwd)

    def ochunk(d, s):
        sl = jnp.maximum(s - nc, 0)
        return jnp.where(d == 0, sl, nl - 1 - sl)

    nb = RWKV_SAMPLES_PER_STEP if B % RWKV_SAMPLES_PER_STEP == 0 else 1
    zblk = lambda c0, w: pl.BlockSpec((nb, C, w), lambda b, d, s: (b, chunk(d, s), c0 // w))
    dpar = lambda r: pl.BlockSpec((1, r, W), lambda b, d, s: (d, 0, 0))
    par = pl.BlockSpec((1, W), lambda b, d, s: (0, 0))
    oblk = pl.BlockSpec((1, nb, C, W), lambda b, d, s: (d, b, ochunk(d, s), 0))
    G = 2 * RWKV_HEAD_DIM
    return pl.pallas_call(
        functools.partial(_rwkv_kernel, nc=nc, nb=nb, nsub=nsub),
        grid=(B // nb, 2, nc + nl),
        in_specs=[zblk(C_RWK, W), zblk(C_RWV, W), zblk(C_RWR, W), zblk(C_LORA, 256),
                  dpar(1), dpar(1), dpar(G), dpar(G), par, par, par],
        out_specs=[oblk, oblk],
        out_shape=[jax.ShapeDtypeStruct((2, B, T, W), f32), jax.ShapeDtypeStruct((2, B, T, W), f32)],
        scratch_shapes=[pltpu.VMEM((nb * RWKV_PAIRS, G, G), f32)],
        compiler_params=_params(("arbitrary", "arbitrary", "arbitrary")),
        name="rwkv",
    )(z, z, z, z, w0, a0, w2p, a2p, k_k, k_a, r_k)


def _merge_kernel(x_ref, ret_ref, rw0_ref, rw1_ref, b0_ref, b1_ref, gate_ref, retg_ref, gd_ref, mod_ref,
                  ng_ref, gn_ref, lng_ref, lnb_ref, g2_ref, wbr_ref, wbw_ref, wout_ref, wrh_ref, wrl_ref,
                  x1_ref, h2_ref, lt_ref):
    hd = RET_HEAD_DIM
    ret = ret_ref[0]
    parts = []
    for hh in range(RET_HEADS):
        xh = ret[:, hh * hd:(hh + 1) * hd]
        mu = jnp.mean(xh, axis=-1, keepdims=True)
        dv = xh - mu
        var = jnp.mean(dv * dv, axis=-1, keepdims=True)
        parts.append(dv * lax.rsqrt(var + RET_EPS))
    yr = retg_ref[0].astype(f32) * (jnp.concatenate(parts, axis=1) * gn_ref[...])
    y_ret = _bdot(yr, wbr_ref[...])
    W = RWKV_WIDTH
    gi = lax.broadcasted_iota(jnp.int32, (W, W), 0)
    gj = lax.broadcasted_iota(jnp.int32, (W, W), 1)
    ones_bd = ((gi >> 6) == (gj >> 6)).astype(bf16)
    o = rw0_ref[0, 0] + rw1_ref[0, 0]
    mu = _split_dot(o, ones_bd) * (1.0 / RWKV_HEAD_DIM)
    dv = o - mu
    var = _bdot(dv * dv, ones_bd) * (1.0 / RWKV_HEAD_DIM)
    yw = dv * lax.rsqrt(var + RWKV_EPS) * lng_ref[...] + lnb_ref[...]
    gate = _dot(gd_ref[0], g2_ref[...])
    yw = (yw + b0_ref[0, 0] + b1_ref[0, 0]) * gate
    y_rw = _bdot(yw, wbw_ref[...])
    D = y_ret.shape[1]
    g = gate_ref[0].astype(f32)
    m = g[:, :D] * y_ret + g[:, D:] * y_rw
    y = _bdot(m, wout_ref[...])

    def rms(v, gg):
        return v * lax.rsqrt(jnp.mean(v * v, axis=-1, keepdims=True) + NORM_EPS) * gg

    x1 = x_ref[0] + mod_ref[0, 0:1, :] * rms(y, ng_ref[0:1, :])
    x1_ref[0] = x1
    h2 = rms(x1, ng_ref[1:2, :]) * (1.0 + mod_ref[0, 2:3, :]) + mod_ref[0, 1:2, :]
    h2b = h2.astype(bf16)
    h2_ref[0] = h2b
    h2l = (h2 - h2b.astype(f32)).astype(bf16)
    lgt = _dot(h2b, wrh_ref[...]) + _dot(h2l, wrh_ref[...]) + _dot(h2b, wrl_ref[...])
    lt_ref[0] = lgt.T[0:N_EXPERTS, :]


def _merge(x, ret_o, rw_o, bonus, z, mod2, ng12, gn, lng, lnb, g2, wbr, wbw, wout, wrh, wrl):
    B, T, D = x.shape
    tm = MERGE_ROWS if T % MERGE_ROWS == 0 else ROWS
    W = RWKV_WIDTH
    row = lambda w: pl.BlockSpec((1, tm, w), lambda b, i: (b, i, 0))
    dblk = lambda dd: pl.BlockSpec((1, 1, tm, W), lambda b, i: (dd, b, i, 0))
    zblk = lambda c0, w: pl.BlockSpec((1, tm, w), lambda b, i: (b, i, c0 // w))
    full = lambda a: pl.BlockSpec(a.shape, lambda b, i: (0,) * a.ndim)
    return pl.pallas_call(
        _merge_kernel,
        grid=(B, T // tm),
        in_specs=[row(D), row(W), dblk(0), dblk(1), dblk(0), dblk(1),
                  zblk(C_MERGE, 2 * D), zblk(C_RETG, W), zblk(C_LORA + 128, 128),
                  pl.BlockSpec((1, 3, D), lambda b, i: (b, 0, 0)),
                  full(ng12), full(gn), full(lng), full(lnb), full(g2), full(wbr), full(wbw), full(wout),
                  full(wrh), full(wrl)],
        out_specs=[row(D), row(D), pl.BlockSpec((1, N_EXPERTS, tm), lambda b, i: (b, 0, i))],
        out_shape=[jax.ShapeDtypeStruct((B, T, D), f32), jax.ShapeDtypeStruct((B, T, D), bf16),
                   jax.ShapeDtypeStruct((B, N_EXPERTS, T), f32)],
        compiler_params=_params(("arbitrary", "arbitrary")),
        name="merge",
    )(x, ret_o, rw_o, rw_o, bonus, bonus, z, z, z, mod2, ng12, gn, lng, lnb, g2, wbr, wbw, wout, wrh, wrl)


def _route_kernel(lt_ref, slot_ref, rt_ref, cnt_ref, slotf_ref, gate_ref, *, cap):
    B, NE, T = lt_ref.shape
    lg = lt_ref[...]
    mx = jnp.max(lg, axis=1, keepdims=True)
    ex = jnp.exp(lg - mx)
    aff = (ex / jnp.sum(ex, axis=1, keepdims=True)).reshape(B * NE, T)
    E = B * NE

    def count_ge(cand):
        return jnp.sum((aff >= cand).astype(f32), axis=1, keepdims=True)

    def exp_step(_, kk):
        k_lo, k_hi = kk
        km = jnp.floor((k_lo + k_hi) * 0.5)
        ok = count_ge(jnp.exp2(-km)) >= cap
        return jnp.where(ok, k_lo, km), jnp.where(ok, km, k_hi)

    k_lo, k_hi = lax.fori_loop(0, ROUTE_EXP_STEPS, exp_step,
                               (jnp.full((E, 1), -1.0, f32), jnp.full((E, 1), ROUTE_MAX_EXP, f32)))
    lo0 = jnp.where(k_hi >= ROUTE_MAX_EXP, 0.0, jnp.exp2(-k_hi))
    hi0 = jnp.exp2(-k_lo)

    def val_step(_, lh):
        lo, hi = lh
        mid = (lo + hi) * 0.5
        ok = count_ge(mid) >= cap
        return jnp.where(ok, mid, lo), jnp.where(ok, hi, mid)

    lo, hi = lax.fori_loop(0, ROUTE_VAL_STEPS, val_step, (lo0, hi0))
    gt = aff >= hi
    eq = (aff >= lo) & (aff < hi)
    need = cap - jnp.sum(gt.astype(f32), axis=1, keepdims=True)
    tri = (lax.broadcasted_iota(jnp.int32, (T, T), 0) < lax.broadcasted_iota(jnp.int32, (T, T), 1)).astype(bf16)
    eq_before = _dot(eq.astype(bf16), tri)
    sel = gt | (eq & (eq_before < need))
    slot = _dot(sel.astype(bf16), tri)
    slot_f = jnp.where(sel, slot, -1.0)
    slot_ref[...] = slot_f.astype(jnp.int32).reshape(B, NE, T)
    before = (lax.broadcasted_iota(jnp.int32, (T, 128), 0)
              < lax.broadcasted_iota(jnp.int32, (T, 128), 1) * GATHER_TILE).astype(bf16)
    cnt_ref[...] = _dot(sel.astype(bf16), before).astype(jnp.int32).reshape(B, NE, 128)
    slotf_ref[...] = slot_f.reshape(B, NE, T)
    gate_ref[...] = jnp.where(sel, aff, 0.0).reshape(B, NE, T)

    def transpose_sample(b, carry):
        packed = jnp.concatenate([slotf_ref[b], gate_ref[b], jnp.zeros((128 - 2 * NE, T), f32)], axis=0)
        rt_ref[b] = packed.T.astype(bf16)
        return carry

    lax.fori_loop(0, B, transpose_sample, 0)


def _route(lt, cap):
    B, E, T = lt.shape
    assert cap <= 256 and T % GATHER_TILE == 0 and T // GATHER_TILE < 128
    return pl.pallas_call(
        functools.partial(_route_kernel, cap=cap),
        grid=(1,),
        in_specs=[pl.BlockSpec((B, E, T), lambda i: (0, 0, 0))],
        out_specs=[pl.BlockSpec((B, E, T), lambda i: (0, 0, 0)), pl.BlockSpec((B, T, 128), lambda i: (0, 0, 0)),
                   pl.BlockSpec((B, E, 128), lambda i: (0, 0, 0))],
        out_shape=[jax.ShapeDtypeStruct((B, E, T), jnp.int32), jax.ShapeDtypeStruct((B, T, 128), bf16),
                   jax.ShapeDtypeStruct((B, E, 128), jnp.int32)],
        scratch_shapes=[pltpu.VMEM((B, E, T), f32), pltpu.VMEM((B, E, T), f32)],
        compiler_params=_params(("arbitrary",)),
        name="route",
    )(lt)


def _ffn_kernel(cnt_ref, slot_ref, h_ref, wg_ref, wu_ref, wd_ref, o_ref, wgb_ref, wub_ref, wdb_ref, xg_ref, *, cap):
    T = h_ref.shape[1]
    e = pl.program_id(0)

    @pl.when(pl.program_id(1) == 0)
    def _():
        wgb_ref[...] = wg_ref[0].astype(bf16)
        wub_ref[...] = wu_ref[0].astype(bf16)
        wdb_ref[...] = wd_ref[0].astype(bf16)

    nb = h_ref.shape[0]
    GT, GW, GA = GATHER_TILE, GATHER_WINDOW, GATHER_ALIGN
    nt = T // GT
    for bb in range(nb):
        b = pl.program_id(1) * nb + bb
        base = [pl.multiple_of(lax.shift_left(lax.shift_right_logical(cnt_ref[b, e, j], GATHER_ALIGN_LOG2),
                                              GATHER_ALIGN_LOG2), GA) for j in range(nt)]
        fits = cnt_ref[b, e, 1] - base[0] <= GW
        for j in range(1, nt):
            fits = fits & (cnt_ref[b, e, j + 1] - base[j] <= GW)

        @pl.when(fits)
        def _():
            xg_ref[bb] = jnp.zeros(xg_ref.shape[1:], f32)
            for j in range(nt):
                rows = base[j] + lax.broadcasted_iota(jnp.int32, (GW, GT), 0)
                onehot = (slot_ref[bb, 0, j:j + 1, :] == rows).astype(bf16)
                xg_ref[bb, pl.ds(base[j], GW), :] += _dot(onehot, h_ref[bb, j * GT:(j + 1) * GT, :])

        @pl.when(jnp.logical_not(fits))
        def _():
            rows = lax.broadcasted_iota(jnp.int32, (cap, GT), 0)
            acc = jnp.zeros((cap, h_ref.shape[2]), f32)
            for j in range(nt):
                onehot = (slot_ref[bb, 0, j:j + 1, :] == rows).astype(bf16)
                acc = acc + _dot(onehot, h_ref[bb, j * GT:(j + 1) * GT, :])
            xg_ref[bb, 0:cap, :] = acc

    xg = jnp.concatenate([xg_ref[bb, 0:cap, :].astype(bf16) for bb in range(nb)], axis=0)
    hg = _dot(xg, wgb_ref[...])
    hu = _dot(xg, wub_ref[...])
    hid = (hg * _sigmoid(hg) * hu).astype(bf16)
    out = _dot(hid, wdb_ref[...]).astype(bf16)
    for bb in range(nb):
        o_ref[bb, 0] = out[bb * cap:(bb + 1) * cap]


def _ffn(cnt, slot4, h2, wg, wu, wd, cap):
    B, T, D = h2.shape
    E, _, F = wg.shape
    nb = FFN_SAMPLES_PER_STEP if B % FFN_SAMPLES_PER_STEP == 0 else 1
    grid_spec = pltpu.PrefetchScalarGridSpec(
        num_scalar_prefetch=1,
        grid=(E, B // nb),
        in_specs=[pl.BlockSpec((nb, 1, T // GATHER_TILE, GATHER_TILE), lambda e, b, c: (b, e, 0, 0)),
                  pl.BlockSpec((nb, T, D), lambda e, b, c: (b, 0, 0)),
                  pl.BlockSpec((1, D, F), lambda e, b, c: (e, 0, 0)),
                  pl.BlockSpec((1, D, F), lambda e, b, c: (e, 0, 0)),
                  pl.BlockSpec((1, F, D), lambda e, b, c: (e, 0, 0))],
        out_specs=pl.BlockSpec((nb, 1, cap, D), lambda e, b, c: (b, e, 0, 0)),
        scratch_shapes=[pltpu.VMEM((D, F), bf16), pltpu.VMEM((D, F), bf16), pltpu.VMEM((F, D), bf16),
                        pltpu.VMEM((nb, cap + GATHER_WINDOW, D), f32)])
    return pl.pallas_call(
        functools.partial(_ffn_kernel, cap=cap),
        grid_spec=grid_spec,
        out_shape=jax.ShapeDtypeStruct((B, E, cap, D), bf16),
        compiler_params=_params(("arbitrary", "arbitrary"), FFN_VMEM_LIMIT),
        name="ffn",
    )(cnt, slot4, h2, wg, wu, wd)


def _combine_kernel(cnt_ref, rt_ref, eo_ref, x1_ref, mod_ref, ng_ref, o_ref, y_ref, *, cap):
    b = pl.program_id(0)
    i = pl.program_id(1)
    rt = rt_ref[0].astype(f32)
    tm = rt.shape[0]
    E = eo_ref.shape[1]
    W = min(SCATTER_WINDOW, cap)
    GS = SCATTER_GROUP
    ST = GATHER_TILE
    for st in range(tm // ST):
        t = i * (tm // ST) + st
        r = rt[st * ST:(st + 1) * ST]
        base, fits = [], None
        for e in range(E):
            lo = cnt_ref[b, e, t]
            be = jnp.minimum(lax.shift_left(lax.shift_right_logical(lo, GATHER_ALIGN_LOG2), GATHER_ALIGN_LOG2),
                             cap - W)
            base.append(pl.multiple_of(be, GATHER_ALIGN))
            ok = cnt_ref[b, e, t + 1] - be <= W
            fits = ok if fits is None else fits & ok

        def scatter_matrix(e, first, width, r=r):
            cols = (first + lax.broadcasted_iota(jnp.int32, (ST, width), 1)).astype(f32)
            return jnp.where(r[:, e:e + 1] == cols, r[:, E + e:E + e + 1], 0.0).astype(bf16)

        @pl.when(fits)
        def _():
            y = jnp.zeros((ST, y_ref.shape[1]), f32)
            for g in range(0, E, GS):
                p = jnp.concatenate([scatter_matrix(e, base[e], W) for e in range(g, g + GS)], axis=1)
                rows = jnp.concatenate([eo_ref[0, e, pl.ds(base[e], W), :] for e in range(g, g + GS)], axis=0)
                y = y + _dot(p, rows)
            y_ref[st * ST:(st + 1) * ST, :] = y

        @pl.when(jnp.logical_not(fits))
        def _():
            y = jnp.zeros((ST, y_ref.shape[1]), f32)
            for e in range(E):
                y = y + _dot(scatter_matrix(e, 0, cap), eo_ref[0, e])
            y_ref[st * ST:(st + 1) * ST, :] = y

    y = y_ref[...]
    yn = y * lax.rsqrt(jnp.mean(y * y, axis=-1, keepdims=True) + NORM_EPS) * ng_ref[...]
    o_ref[0] = x1_ref[0] + mod_ref[0] * yn


def _combine(cnt, rt, eo, x1, g2mod, ng3, cap):
    B, T, D = x1.shape
    E = eo.shape[1]
    tm = 512 if T % 512 == 0 else T
    assert E % SCATTER_GROUP == 0 and tm % GATHER_TILE == 0
    grid_spec = pltpu.PrefetchScalarGridSpec(
        num_scalar_prefetch=1,
        grid=(B, T // tm),
        in_specs=[pl.BlockSpec((1, tm, 128), lambda b, i, c: (b, i, 0)),
                  pl.BlockSpec((1, E, cap, D), lambda b, i, c: (b, 0, 0, 0)),
                  pl.BlockSpec((1, tm, D), lambda b, i, c: (b, i, 0)),
                  pl.BlockSpec((1, 1, D), lambda b, i, c: (b, 0, 0)),
                  pl.BlockSpec((1, D), lambda b, i, c: (0, 0))],
        out_specs=pl.BlockSpec((1, tm, D), lambda b, i, c: (b, i, 0)),
        scratch_shapes=[pltpu.VMEM((tm, D), f32)])
    return pl.pallas_call(
        functools.partial(_combine_kernel, cap=cap),
        grid_spec=grid_spec,
        out_shape=jax.ShapeDtypeStruct((B, T, D), f32),
        compiler_params=_params(("arbitrary", "arbitrary")),
        name="combine",
    )(cnt, rt, eo, x1, g2mod, ng3)


def _permute_columns(w):
    sk, sv, rk, rv, wd, ad = 0, 512, 1024, 1536, 2048, 2112
    q0 = 2176
    rq, rg, rr, gd, mg = q0, q0 + 512, q0 + 1024, q0 + 1536, q0 + 1664
    order = [(mg, 2048), (sk, 512), (sv, 512), (rq, 512), (rg, 512), (rk, 512), (rv, 512), (rr, 512),
             (wd, 64), (ad, 64), (gd, 128)]
    parts = [w[:, a:a + n] for a, n in order]
    parts.append(jnp.zeros((w.shape[0], IN_COLS - USED_COLS), w.dtype))
    return jnp.concatenate(parts, axis=1)


def _rope_tables(T, CT):
    t = jnp.arange(T)
    nfreq = RET_HEAD_DIM // 4
    inv = ROPE_BASE ** (-jnp.arange(nfreq, dtype=f32) / nfreq)
    ang = jnp.concatenate([(t // GRID_W).astype(f32)[:, None] * inv,
                           (t % GRID_W).astype(f32)[:, None] * inv], axis=-1)
    cos, sin = jnp.cos(ang), jnp.sin(ang)
    cosf = jnp.concatenate([cos, cos], axis=1)
    sinf = jnp.concatenate([-sin, sin], axis=1)
    return (jnp.concatenate([cosf, jnp.ones((CT, RET_HEAD_DIM), f32)], axis=0),
            jnp.concatenate([sinf, jnp.zeros((CT, RET_HEAD_DIM), f32)], axis=0))


def kernel(x, c, ctx, c_ctx, w_mod, b_mod, norm_g, w_in, ret_log_decay, ret_gn_g, rwkv_mu, rwkv_k_k, rwkv_k_a,
           rwkv_r_k, rwkv_w0, rwkv_w2, rwkv_a0, rwkv_a2, rwkv_g2, rwkv_ln_g, rwkv_ln_b, w_br_ret, w_br_rwkv,
           w_out, w_router, w_gate, w_up, w_down):
    B, T, D = x.shape
    CT = ctx.shape[1]
    assert w_mod.shape[0] == 1 and D == D_MODEL
    assert CT % ROWS == 0 and T % ROWS == 0 and T % GRID_W == 0
    cap = CAPACITY_FACTOR * T // N_EXPERTS
    assert cap % 8 == 0

    mrows = -(-(B + 1) // 8) * 8
    cc = jnp.zeros((mrows, D), f32).at[:B].set(c).at[B].set(c_ctx)
    mod = _modulation(cc, w_mod[0], b_mod[0])
    lat = mod[:B].reshape(B, N_MOD, D)
    cm = jnp.broadcast_to(mod[B].reshape(1, N_MOD, D), (B, N_MOD, D))
    modrows = jnp.concatenate([lat[:, 0:2], cm[:, 0:2]], axis=1)

    w_perm = _permute_columns(w_in[0]).astype(bf16)
    mu = rwkv_mu[0]
    ss = 2 * RWKV_WIDTH + DECAY_LORA + ICLR_LORA
    mu_full = jnp.zeros((2, IN_COLS), f32)
    mu_full = mu_full.at[:, C_RWK:C_RWK + 1024].set(mu[:, 0:1024])
    mu_full = mu_full.at[:, C_RWR:C_RWR + 512].set(mu[:, ss:ss + 512])
    mu_full = mu_full.at[:, C_LORA:C_LORA + 128].set(mu[:, 1024:ss])
    mu_full = mu_full.at[:, C_LORA + 128:USED_COLS].set(mu[:, ss + 512:])
    cosf, sinf = _rope_tables(T, CT)
    z = _inproj(x, ctx, modrows, norm_g[0, 0:1], w_perm, mu_full, cosf, sinf)

    lg = -jnp.exp(ret_log_decay[0].astype(f32))
    ret_o = _retention(lg, z, CT)

    G = 2 * RWKV_HEAD_DIM
    w2p = jnp.zeros((2, G, RWKV_WIDTH), f32).at[:, :DECAY_LORA].set(rwkv_w2[0]).astype(bf16)
    a2p = jnp.zeros((2, G, RWKV_WIDTH), f32).at[:, DECAY_LORA:].set(rwkv_a2[0]).astype(bf16)
    rw_o, bonus = _rwkv(z, rwkv_w0[0][:, None, :], rwkv_a0[0][:, None, :], w2p, a2p,
                        rwkv_k_k[0][None], rwkv_k_a[0][None], rwkv_r_k[0][None], CT)

    mod2 = jnp.stack([lat[:, 2], lat[:, 3], lat[:, 4]], axis=1)
    wr_pad = jnp.zeros((D, 128), f32).at[:, :N_EXPERTS].set(w_router[0])
    wr_hi = wr_pad.astype(bf16)
    wr_lo = (wr_pad - wr_hi.astype(f32)).astype(bf16)
    x1, h2, lt = _merge(x, ret_o, rw_o, bonus, z, mod2, norm_g[0, 1:3], ret_gn_g[0][None], rwkv_ln_g[0][None],
                        rwkv_ln_b[0][None], rwkv_g2[0].astype(bf16), w_br_ret[0].astype(bf16),
                        w_br_rwkv[0].astype(bf16), w_out[0].astype(bf16), wr_hi, wr_lo)

    slot, rt, cnt = _route(lt, cap)
    cnt = cnt[:, :, :T // GATHER_TILE + 1]
    eo = _ffn(cnt, slot.reshape(B, N_EXPERTS, T // GATHER_TILE, GATHER_TILE), h2, w_gate[0], w_up[0], w_down[0], cap)
    return _combine(cnt, rt, eo, x1, lat[:, 5:6], norm_g[0, 3:4], cap)
```

```python
import functools

import jax
import jax.numpy as jnp
from jax import lax
from jax.experimental import pallas as pl
from jax.experimental.pallas import tpu as pltpu

f32 = jnp.float32
bf16 = jnp.bfloat16

D_MODEL = 1024
GRID_W = 64
RET_HEAD_DIM = 128
RET_WIDTH = 512
RET_HEADS = 4
RET_CHUNK = 128
RET_EPS = 1e-5
ROPE_BASE = 10000.0
RWKV_HEAD_DIM = 64
RWKV_WIDTH = 512
RWKV_PAIRS = 4
RWKV_CHUNK = 64
RWKV_CHUNKS_PER_STEP = 4
RWKV_SAMPLES_PER_STEP = 8
DECAY_LORA = 64
ICLR_LORA = 64
GATE_LORA = 128
RWKV_EPS = 64e-5
N_EXPERTS = 16
EXPERT_FF = 1024
CAPACITY_FACTOR = 2
N_MOD = 6
NORM_EPS = 1e-6

C_MERGE, C_RETK, C_RETV, C_RETQ, C_RETG = 0, 2048, 2560, 3072, 3584
C_RWK, C_RWV, C_RWR, C_LORA = 4096, 4608, 5120, 5632
USED_COLS = 5888
TN = 512
IN_COLS = -(-USED_COLS // TN) * TN
ROWS = 256
STEP_ROWS = 768
POST_ROWS = 128
QUERY_ROWS = 1024
MERGE_ROWS = 512

ROUTE_MAX_EXP = 126.0
ROUTE_EXP_STEPS = 8
ROUTE_VAL_STEPS = 26

VMEM_LIMIT = 56 * 1024 * 1024
FFN_SAMPLES_PER_STEP = 2
GATHER_TILE = 256
GATHER_ALIGN_LOG2 = 4
GATHER_ALIGN = 1 << GATHER_ALIGN_LOG2
GATHER_WINDOW = 80
SCATTER_WINDOW = 64
SCATTER_GROUP = 4
FFN_VMEM_LIMIT = 62 * 1024 * 1024


def _dot(a, b):
    return jnp.dot(a, b, preferred_element_type=f32)


def _dot_nt(a, b):
    return lax.dot_general(a, b, (((1,), (1,)), ((), ())), preferred_element_type=f32)


def _dot_tn(a, b):
    return lax.dot_general(a, b, (((0,), (0,)), ((), ())), preferred_element_type=f32)


def _bdot(a, b):
    return _dot(a.astype(bf16), b.astype(bf16))


def _split_dot(x, w):
    hi = x.astype(bf16)
    lo = (x - hi.astype(f32)).astype(bf16)
    return _dot(hi, w) + _dot(lo, w)


def _sigmoid(x):
    return 0.5 * jnp.tanh(0.5 * x) + 0.5


def _params(sem, limit=VMEM_LIMIT):
    return pltpu.CompilerParams(dimension_semantics=sem, vmem_limit_bytes=limit)


def _mod_kernel(c_ref, w_ref, b_ref, o_ref):
    c = c_ref[...]
    s = c * jax.nn.sigmoid(c)
    o_ref[...] = _bdot(s, w_ref[...]) + b_ref[...]


def _modulation(cc, w_mod, b_mod):
    m, d = cc.shape
    n = w_mod.shape[1]
    tn = 512
    return pl.pallas_call(
        _mod_kernel,
        grid=(n // tn,),
        in_specs=[pl.BlockSpec((m, d), lambda j: (0, 0)),
                  pl.BlockSpec((d, tn), lambda j: (0, j)),
                  pl.BlockSpec((1, tn), lambda j: (0, j))],
        out_specs=pl.BlockSpec((m, tn), lambda j: (0, j)),
        out_shape=jax.ShapeDtypeStruct((m, n), f32),
        compiler_params=_params(("arbitrary",)),
        name="mod",
    )(cc, w_mod, b_mod.reshape(1, n))


def _inproj_kernel(x_ref, c_ref, mod_ref, g_ref, w_ref, mu_ref, cos_ref, sin_ref, o_ref, h_ref, z_ref):
    n = pl.program_id(1)
    T = x_ref.shape[1]
    L = T + c_ref.shape[1]
    nlat = T // ROWS
    nchunk = L // ROWS
    PAD = 8

    SR = STEP_ROWS if L % STEP_ROWS == 0 else ROWS

    @pl.when(n == 0)
    def _():
        z_ref[0:PAD, :] = jnp.zeros((PAD, TN), f32)
        z_ref[PAD + L:PAD + L + PAD, :] = jnp.zeros((PAD, TN), f32)

        def norm_chunk(src_ref, s0, r0, o):
            xb = src_ref[0, pl.ds(s0, ROWS), :]
            y = xb * lax.rsqrt(jnp.mean(xb * xb, axis=-1, keepdims=True) + NORM_EPS) * g_ref[...]
            sh = mod_ref[0, o:o + 1, :]
            sc = mod_ref[0, o + 1:o + 2, :]
            h_ref[pl.ds(r0, ROWS), :] = (y * (1.0 + sc) + sh).astype(bf16)

        def lat_body(i, carry):
            r0 = pl.multiple_of(i * ROWS, ROWS)
            norm_chunk(x_ref, r0, r0, 0)
            return carry

        lax.fori_loop(0, nlat, lat_body, 0)
        for j in range(nlat, nchunk):
            norm_chunk(c_ref, j * ROWS - T, j * ROWS, 2)

    PR = POST_ROWS
    QR = QUERY_ROWS

    def rope(scale):
        def post(z, r0):
            if scale != 1.0:
                z = z * scale
            cs = cos_ref[pl.ds(r0, z.shape[0]), :]
            sn = sin_ref[pl.ds(r0, z.shape[0]), :]
            parts = []
            for hh in range(TN // RET_HEAD_DIM):
                zh = z[:, hh * RET_HEAD_DIM:(hh + 1) * RET_HEAD_DIM]
                parts.append(zh * cs + pltpu.roll(zh, RET_HEAD_DIM // 2, 1) * sn)
            return jnp.concatenate(parts, axis=1)
        return post

    def shifted(r0, at_boundary):
        win = z_ref[pl.ds(r0, PR + 2 * PAD), :]
        prev = win[PAD - 1:PAD - 1 + PR]
        z = win[PAD:PAD + PR]
        nxt = win[PAD + 1:PAD + 1 + PR]
        if at_boundary:
            rid = r0 + lax.broadcasted_iota(jnp.int32, (PR, 1), 0)
            prev = jnp.where(rid == T, 0.0, prev)
            nxt = jnp.where(rid == T - 1, 0.0, nxt)
        mu0 = mu_ref[0:1, :]
        mu1 = mu_ref[1:2, :]
        return (1.0 - mu0 - mu1) * z + mu0 * prev + mu1 * nxt

    def lora_act(zs):
        lane = lax.broadcasted_iota(jnp.int32, (1, TN), 1)
        return jnp.where(lane < DECAY_LORA, jnp.tanh(zs),
                         jnp.where(lane < DECAY_LORA + ICLR_LORA, zs, jax.nn.sigmoid(zs)))

    def product(r0, rows=SR):
        return _dot(h_ref[pl.ds(r0, rows), :], w_ref[...])

    def direct(post, latent_only=False):
        rows, total = (QR, T) if latent_only and T % QR == 0 else (SR, L)

        def body(i, carry):
            r0 = pl.multiple_of(i * rows, rows)
            o_ref[0, pl.ds(r0, rows), :] = post(product(r0, rows), r0).astype(bf16)
            return carry
        lax.fori_loop(0, total // rows, body, 0)
        if total < L:
            o_ref[0, total:L, :] = jnp.zeros((L - total, TN), bf16)

    def via_buffer(act):
        def mm_body(i, carry):
            r0 = pl.multiple_of(i * SR, SR)
            z_ref[pl.ds(PAD + r0, SR), :] = product(r0)
            return carry
        lax.fori_loop(0, L // SR, mm_body, 0)

        def post(r0, at_boundary):
            o_ref[0, pl.ds(r0, PR), :] = act(shifted(r0, at_boundary)).astype(bf16)

        def post_body(i, carry):
            post(pl.multiple_of(i * PR, PR), False)
            return carry

        seam = T // PR
        lax.fori_loop(0, seam - 1, post_body, 0)
        post((seam - 1) * PR, True)
        post(seam * PR, True)
        lax.fori_loop(seam + 1, L // PR, post_body, 0)

    t = lambda c: c // TN
    pl.when(n < t(C_RETK))(lambda: direct(lambda z, r0: _sigmoid(z), latent_only=True))
    pl.when((n >= t(C_RETK)) & (n < t(C_RETV)))(lambda: direct(rope(RET_HEAD_DIM ** -0.5)))
    pl.when((n >= t(C_RETV)) & (n < t(C_RETQ)))(lambda: direct(lambda z, r0: z))
    pl.when((n >= t(C_RETQ)) & (n < t(C_RETG)))(lambda: direct(rope(1.0), latent_only=True))
    pl.when((n >= t(C_RETG)) & (n < t(C_RWK)))(lambda: direct(lambda z, r0: z * _sigmoid(z), latent_only=True))
    pl.when((n >= t(C_RWK)) & (n < t(C_LORA)))(lambda: via_buffer(lambda zs: zs))
    pl.when(n >= t(C_LORA))(lambda: via_buffer(lora_act))


def _inproj(x, ctx, modrows, g0, w_perm, mu_full, cosf, sinf):
    B, T, D = x.shape
    CT = ctx.shape[1]
    L = T + CT
    nt = IN_COLS // TN
    return pl.pallas_call(
        _inproj_kernel,
        grid=(B, nt),
        in_specs=[pl.BlockSpec((1, T, D), lambda b, n: (b, 0, 0)),
                  pl.BlockSpec((1, CT, D), lambda b, n: (b, 0, 0)),
                  pl.BlockSpec((1, 4, D), lambda b, n: (b, 0, 0)),
                  pl.BlockSpec((1, D), lambda b, n: (0, 0)),
                  pl.BlockSpec((D, TN), lambda b, n: (0, n)),
                  pl.BlockSpec((2, TN), lambda b, n: (0, n)),
                  pl.BlockSpec((L, RET_HEAD_DIM), lambda b, n: (0, 0)),
                  pl.BlockSpec((L, RET_HEAD_DIM), lambda b, n: (0, 0))],
        out_specs=pl.BlockSpec((1, L, TN), lambda b, n: (b, 0, n)),
        out_shape=jax.ShapeDtypeStruct((B, L, IN_COLS), bf16),
        scratch_shapes=[pltpu.VMEM((L, D), bf16), pltpu.VMEM((L + 16, TN), f32)],
        compiler_params=_params(("arbitrary", "arbitrary")),
        name="inproj",
    )(x, ctx, modrows, g0, w_perm, mu_full, cosf, sinf)


def _ret_kernel(lg_ref, q_ref, k_ref, v_ref, o_ref, r_ref, tab_ref, *, ctx):
    L = q_ref.shape[1]
    Cc = RET_CHUNK
    hd = RET_HEAD_DIM
    lat = L - ctx
    nc = ctx // Cc
    nl = lat // Cc
    combos = [(h, d) for h in range(RET_HEADS) for d in (0, 1)]
    INTRA, CROSS, TAIL, DECAY = 0, 1, 2, 3

    @pl.when(pl.program_id(0) == 0)
    def _():
        ii = lax.broadcasted_iota(jnp.int32, (Cc, Cc), 0).astype(f32)
        jj = lax.broadcasted_iota(jnp.int32, (Cc, Cc), 1).astype(f32)
        for idx, (h, d) in enumerate(combos):
            lg = lg_ref[d, h]
            if d == 0:
                diff = ii - jj
                cross = jnp.exp(lg * (ii + 1.0))
                tailw = jnp.exp(lg * (Cc - 1.0 - ii))
            else:
                diff = jj - ii
                cross = jnp.exp(lg * (Cc - ii))
                tailw = jnp.exp(lg * ii)
            tab_ref[idx, INTRA] = jnp.where(diff >= 0, jnp.exp(lg * jnp.maximum(diff, 0.0)), 0.0)
            tab_ref[idx, CROSS] = cross
            tab_ref[idx, TAIL] = tailw
            tab_ref[idx, DECAY] = jnp.exp(jnp.zeros((Cc, Cc), f32) + lg * Cc)

    r_ref[...] = jnp.zeros(r_ref.shape, f32)
    o_ref[...] = jnp.zeros(o_ref.shape, f32)

    def cols(h):
        return slice(h * hd, (h + 1) * hd)

    def update(idx, kc, vc):
        ks = (kc.astype(f32) * tab_ref[idx, TAIL]).astype(bf16)
        r_ref[idx] = r_ref[idx] * tab_ref[idx, DECAY] + _dot_tn(ks, vc)

    def ctx_step(s, carry):
        for idx, (h, d) in enumerate(combos):
            row0 = pl.multiple_of(lat + (s * Cc if d == 0 else (nc - 1 - s) * Cc), Cc)
            update(idx, k_ref[0, pl.ds(row0, Cc), cols(h)], v_ref[0, pl.ds(row0, Cc), cols(h)])
        return carry

    def lat_step(s, carry):
        t0s, qs, ks, vs = [], [], [], []
        for h, d in combos:
            t0 = pl.multiple_of(s * Cc if d == 0 else (nl - 1 - s) * Cc, Cc)
            row0 = t0
            t0s.append(t0)
            qs.append(q_ref[0, pl.ds(row0, Cc), cols(h)])
            ks.append(k_ref[0, pl.ds(row0, Cc), cols(h)])
            vs.append(v_ref[0, pl.ds(row0, Cc), cols(h)])
        n = len(combos)
        sc = [(_dot_nt(qs[i], ks[i]) * tab_ref[i, INTRA]).astype(bf16) for i in range(n)]
        oc = [_dot(qs[i], r_ref[i].astype(bf16)) * tab_ref[i, CROSS] for i in range(n)]
        oi = [_dot(sc[i], vs[i]) for i in range(n)]
        for i, (h, d) in enumerate(combos):
            o_ref[0, pl.ds(t0s[i], Cc), cols(h)] += oi[i] + oc[i]
        for i in range(n):
            update(i, ks[i], vs[i])
        return carry

    lax.fori_loop(0, nc, ctx_step, 0)
    lax.fori_loop(0, nl, lat_step, 0)


def _retention(lg, z, ctx):
    B, L, _ = z.shape
    T = L - ctx
    W = RET_WIDTH
    Cc = RET_CHUNK
    blk = lambda c0: pl.BlockSpec((1, L, W), lambda b: (b, 0, c0 // W))
    nchain = 2 * RET_HEADS
    return pl.pallas_call(
        functools.partial(_ret_kernel, ctx=ctx),
        grid=(B,),
        in_specs=[pl.BlockSpec(memory_space=pltpu.SMEM), blk(C_RETQ), blk(C_RETK), blk(C_RETV)],
        out_specs=pl.BlockSpec((1, T, W), lambda b: (b, 0, 0)),
        out_shape=jax.ShapeDtypeStruct((B, T, W), f32),
        scratch_shapes=[pltpu.VMEM((nchain, Cc, Cc), f32), pltpu.VMEM((nchain, 4, Cc, Cc), f32)],
        compiler_params=_params(("arbitrary",)),
        name="ret",
    )(lg, z, z, z)


def _rwkv_kernel(k_ref, v_ref, r_ref, lo_ref, w0_ref, a0_ref, w2_ref, a2_ref, kk_ref, ka_ref, rk_ref,
                 o_ref, bo_ref, s_ref, *, nc, nb, nsub):
    d = pl.program_id(1)
    s = pl.program_id(2)
    C = RWKV_CHUNK

    @pl.when(s == 0)
    def _():
        s_ref[...] = jnp.zeros(s_ref.shape, f32)

    def sub_chunk(j, carry):
        r0 = pl.multiple_of(jnp.where(d == 1, nsub - 1 - j, j) * C, C)
        rows = lambda ref: ref.at[:, pl.ds(r0, C), :]
        _rwkv_chunk(rows(k_ref), rows(v_ref), rows(r_ref), rows(lo_ref), w0_ref, a0_ref, w2_ref, a2_ref,
                    kk_ref, ka_ref, rk_ref, o_ref.at[:, :, pl.ds(r0, C), :], bo_ref.at[:, :, pl.ds(r0, C), :],
                    s_ref, rev=d == 1, emit=s >= nc, nb=nb)
        return carry

    lax.fori_loop(0, nsub, sub_chunk, 0)


def _rwkv_chunk(k_ref, v_ref, r_ref, lo_ref, w0_ref, a0_ref, w2_ref, a2_ref, kk_ref, ka_ref, rk_ref,
                o_ref, bo_ref, s_ref, *, rev, emit, nb):
    C = RWKV_CHUNK
    G = 2 * RWKV_HEAD_DIM

    ii = lax.broadcasted_iota(jnp.int32, (C, G), 0)
    lane = lax.broadcasted_iota(jnp.int32, (C, G), 1)
    jj = lane & (RWKV_HEAD_DIM - 1)
    head0 = lane < RWKV_HEAD_DIM
    dlt = jnp.where(rev, ii - jj, jj - ii)
    strict = dlt < 0
    incl = dlt <= 0
    eye = (ii == jj).astype(f32)
    gi = lax.broadcasted_iota(jnp.int32, (G, G), 0)
    gj = lax.broadcasted_iota(jnp.int32, (G, G), 1)
    blockdiag = (gi >= RWKV_HEAD_DIM) == (gj >= RWKV_HEAD_DIM)
    ones_bd = blockdiag.astype(bf16)
    tri = incl[:, 0:C].astype(bf16)
    base = (ii >> 1) == (jj >> 1)
    offs = [((ii >> (lv + 1)) == (jj >> (lv + 1))) & ((ii >> lv) != (jj >> lv)) for lv in range(1, 6)]

    def segsum(x):
        return jnp.concatenate(
            [_bdot(x[:, p * G:(p + 1) * G], ones_bd) for p in range(RWKV_PAIRS)], axis=1)

    def stack(x):
        xb = x.astype(bf16)
        zero = jnp.zeros_like(xb)
        return jnp.concatenate([jnp.where(head0, xb, zero), jnp.where(head0, zero, xb)], axis=0)

    W = k_ref.shape[2]
    k_all = k_ref[...].astype(f32).reshape(nb * C, W)
    v_all = v_ref[...].astype(f32).reshape(nb * C, W)
    r_all = r_ref[...].astype(f32).reshape(nb * C, W)
    lo_all = lo_ref[:, :, 0:DECAY_LORA + ICLR_LORA].reshape(nb * C, DECAY_LORA + ICLR_LORA)
    u_all = w0_ref[0] + _dot(lo_all, w2_ref[0])
    softplus = jnp.maximum(-u_all, 0.0) + jnp.log1p(jnp.exp(-jnp.abs(u_all)))
    lw_all = -jnp.exp(-softplus - 0.5)
    a_all = jax.nn.sigmoid(a0_ref[0] + _dot(lo_all, a2_ref[0]))
    kkr = k_all * kk_ref[...]
    kk_all = kkr * lax.rsqrt(segsum(kkr * kkr) + 1e-12)
    kd_all = k_all * (1.0 + (a_all - 1.0) * ka_ref[...])
    be_all = kk_all * a_all
    bonus_all = segsum(r_all * kd_all * rk_ref[...]) * v_all
    pre = []
    for bb in range(nb):
        rows = slice(bb * C, (bb + 1) * C)
        vx, rx, lw, kk, kd, be, bonus = (arr[rows] for arr in (v_all, r_all, lw_all, kk_all, kd_all, be_all,
                                                                bonus_all))
        cum = _split_dot_left(tri, lw)
        tot = jnp.where(rev, cum[0:1, :], cum[C - 1:C, :])
        gneg = jnp.exp(-cum)
        gh = jnp.exp(tot - cum)
        pre.append(dict(alb=-kk * jnp.exp(cum - lw), rb=rx * jnp.exp(cum), beb=be * gneg, kb=kd * gneg,
                        beh=be * gh, kh=kd * gh, etot=jnp.exp(tot), v=vx, bonus=bonus))

    units = [(bb, p) for bb in range(nb) for p in range(RWKV_PAIRS)]
    sl = lambda p: slice(p * G, (p + 1) * G)
    part = lambda un, name: pre[un[0]][name][:, sl(un[1])]
    S = {un: s_ref[un[0] * RWKV_PAIRS + un[1]] for un in units}
    Sb = {un: S[un].astype(bf16) for un in units}
    X = {un: part(un, "alb").astype(bf16) for un in units}
    Rb = {un: part(un, "rb").astype(bf16) for un in units}
    Ybs = {un: stack(part(un, "beb")) for un in units}
    Yks = {un: stack(part(un, "kb")) for un in units}
    Vb = {un: part(un, "v").astype(bf16) for un in units}
    Vs = {un: stack(part(un, "v")) for un in units}

    XR = {un: jnp.concatenate([X[un], Rb[un]], axis=0) for un in units}
    ABO = {un: _dot_nt(XR[un], jnp.concatenate([Ybs[un], Yks[un], Sb[un]], axis=0)) for un in units}
    APb = {un: ABO[un][:, 0:G] for un in units}
    APk = {un: ABO[un][:, G:2 * G] for un in units}
    BO = {un: ABO[un][:, 2 * G:3 * G] for un in units}
    Aab = {un: jnp.where(strict, APb[un][0:C], 0.0) for un in units}
    Aak = {un: jnp.where(strict, APk[un][0:C], 0.0).astype(bf16) for un in units}
    Pab = {un: jnp.where(incl, APb[un][C:2 * C], 0.0).astype(bf16) for un in units}
    Pak = {un: jnp.where(incl, APk[un][C:2 * C], 0.0).astype(bf16) for un in units}

    Tm = {un: eye + jnp.where(base, Aab[un], 0.0) for un in units}
    for off in offs:
        Xs = {un: _dot(jnp.where(off, Aab[un], 0.0).astype(bf16), stack(Tm[un])) for un in units}
        Tm = {un: Tm[un] + _dot(Tm[un].astype(bf16), stack(Xs[un])) for un in units}

    AV = {un: _dot(jnp.concatenate([Aak[un], Pak[un]], axis=0), Vs[un]) for un in units}
    Bm = {un: BO[un][0:C] + AV[un][0:C] for un in units}
    U = {un: _dot(Tm[un].astype(bf16), stack(Bm[un])) for un in units}
    Om = {un: BO[un][C:2 * C] + AV[un][C:2 * C] + _dot(Pab[un], stack(U[un])) for un in units}
    for un in units:
        bb, p = un
        upd = _dot_tn(jnp.concatenate([U[un].astype(bf16), Vb[un]], axis=0),
                      jnp.concatenate([part(un, "beh").astype(bf16), part(un, "kh").astype(bf16)], axis=0))
        s_ref[bb * RWKV_PAIRS + p] = jnp.where(blockdiag, S[un] * part(un, "etot") + upd, 0.0)

    @pl.when(emit)
    def _():
        for bb in range(nb):
            o_ref[0, bb] = jnp.concatenate([Om[bb, p] for p in range(RWKV_PAIRS)], axis=1)
            bo_ref[0, bb] = pre[bb]["bonus"]


def _split_dot_left(w, x):
    hi = x.astype(bf16)
    lo = (x - hi.astype(f32)).astype(bf16)
    return _dot(jnp.concatenate([w, w], axis=1), jnp.concatenate([hi, lo], axis=0))


def _rwkv(z, w0, a0, w2p, a2p, k_k, k_a, r_k, ctx):
    B, L, _ = z.shape
    T = L - ctx
    nsub = RWKV_CHUNKS_PER_STEP
    while ctx % (nsub * RWKV_CHUNK) or T % (nsub * RWKV_CHUNK):
        nsub //= 2
    C = nsub * RWKV_CHUNK
    nc, nl = ctx // C, T // C
    W = RWKV_WIDTH

    def chunk(d, s):
        fwd = jnp.where(s < nc, nl + s, s - nc)
        bwd = jnp.where(s < nc, nl + nc - 1 - s, nl - 1 - (s - nc))
        return jnp.where(d == 0, fwd, bwd)

    def ochunk(d, s):
        sl = jnp.maximum(s - nc, 0)
        return jnp.where(d == 0, sl, nl - 1 - sl)

    nb = RWKV_SAMPLES_PER_STEP if B % RWKV_SAMPLES_PER_STEP == 0 else 1
    zblk = lambda c0, w: pl.BlockSpec((nb, C, w), lambda b, d, s: (b, chunk(d, s), c0 // w))
    dpar = lambda r: pl.BlockSpec((1, r, W), lambda b, d, s: (d, 0, 0))
    par = pl.BlockSpec((1, W), lambda b, d, s: (0, 0))
    oblk = pl.BlockSpec((1, nb, C, W), lambda b, d, s: (d, b, ochunk(d, s), 0))
    G = 2 * RWKV_HEAD_DIM
    return pl.pallas_call(
        functools.partial(_rwkv_kernel, nc=nc, nb=nb, nsub=nsub),
        grid=(B // nb, 2, nc + nl),
        in_specs=[zblk(C_RWK, W), zblk(C_RWV, W), zblk(C_RWR, W), zblk(C_LORA, 256),
                  dpar(1), dpar(1), dpar(G), dpar(G), par, par, par],
        out_specs=[oblk, oblk],
        out_shape=[jax.ShapeDtypeStruct((2, B, T, W), f32), jax.ShapeDtypeStruct((2, B, T, W), f32)],
        scratch_shapes=[pltpu.VMEM((nb * RWKV_PAIRS, G, G), f32)],
        compiler_params=_params(("arbitrary", "arbitrary", "arbitrary")),
        name="rwkv",
    )(z, z, z, z, w0, a0, w2p, a2p, k_k, k_a, r_k)


def _merge_kernel(x_ref, ret_ref, rw0_ref, rw1_ref, b0_ref, b1_ref, gate_ref, retg_ref, gd_ref, mod_ref,
                  ng_ref, gn_ref, lng_ref, lnb_ref, g2_ref, wbr_ref, wbw_ref, wout_ref, wrh_ref, wrl_ref,
                  x1_ref, h2_ref, lt_ref):
    hd = RET_HEAD_DIM
    ret = ret_ref[0]
    parts = []
    for hh in range(RET_HEADS):
        xh = ret[:, hh * hd:(hh + 1) * hd]
        mu = jnp.mean(xh, axis=-1, keepdims=True)
        dv = xh - mu
        var = jnp.mean(dv * dv, axis=-1, keepdims=True)
        parts.append(dv * lax.rsqrt(var + RET_EPS))
    yr = retg_ref[0].astype(f32) * (jnp.concatenate(parts, axis=1) * gn_ref[...])
    y_ret = _bdot(yr, wbr_ref[...])
    W = RWKV_WIDTH
    gi = lax.broadcasted_iota(jnp.int32, (W, W), 0)
    gj = lax.broadcasted_iota(jnp.int32, (W, W), 1)
    ones_bd = ((gi >> 6) == (gj >> 6)).astype(bf16)
    o = rw0_ref[0, 0] + rw1_ref[0, 0]
    mu = _split_dot(o, ones_bd) * (1.0 / RWKV_HEAD_DIM)
    dv = o - mu
    var = _bdot(dv * dv, ones_bd) * (1.0 / RWKV_HEAD_DIM)
    yw = dv * lax.rsqrt(var + RWKV_EPS) * lng_ref[...] + lnb_ref[...]
    gate = _dot(gd_ref[0], g2_ref[...])
    yw = (yw + b0_ref[0, 0] + b1_ref[0, 0]) * gate
    y_rw = _bdot(yw, wbw_ref[...])
    D = y_ret.shape[1]
    g = gate_ref[0].astype(f32)
    m = g[:, :D] * y_ret + g[:, D:] * y_rw
    y = _bdot(m, wout_ref[...])

    def rms(v, gg):
        return v * lax.rsqrt(jnp.mean(v * v, axis=-1, keepdims=True) + NORM_EPS) * gg

    x1 = x_ref[0] + mod_ref[0, 0:1, :] * rms(y, ng_ref[0:1, :])
    x1_ref[0] = x1
    h2 = rms(x1, ng_ref[1:2, :]) * (1.0 + mod_ref[0, 2:3, :]) + mod_ref[0, 1:2, :]
    h2b = h2.astype(bf16)
    h2_ref[0] = h2b
    h2l = (h2 - h2b.astype(f32)).astype(bf16)
    lgt = _dot(h2b, wrh_ref[...]) + _dot(h2l, wrh_ref[...]) + _dot(h2b, wrl_ref[...])
    lt_ref[0] = lgt.T[0:N_EXPERTS, :]


def _merge(x, ret_o, rw_o, bonus, z, mod2, ng12, gn, lng, lnb, g2, wbr, wbw, wout, wrh, wrl):
    B, T, D = x.shape
    tm = MERGE_ROWS if T % MERGE_ROWS == 0 else ROWS
    W = RWKV_WIDTH
    row = lambda w: pl.BlockSpec((1, tm, w), lambda b, i: (b, i, 0))
    dblk = lambda dd: pl.BlockSpec((1, 1, tm, W), lambda b, i: (dd, b, i, 0))
    zblk = lambda c0, w: pl.BlockSpec((1, tm, w), lambda b, i: (b, i, c0 // w))
    full = lambda a: pl.BlockSpec(a.shape, lambda b, i: (0,) * a.ndim)
    return pl.pallas_call(
        _merge_kernel,
        grid=(B, T // tm),
        in_specs=[row(D), row(W), dblk(0), dblk(1), dblk(0), dblk(1),
                  zblk(C_MERGE, 2 * D), zblk(C_RETG, W), zblk(C_LORA + 128, 128),
                  pl.BlockSpec((1, 3, D), lambda b, i: (b, 0, 0)),
                  full(ng12), full(gn), full(lng), full(lnb), full(g2), full(wbr), full(wbw), full(wout),
                  full(wrh), full(wrl)],
        out_specs=[row(D), row(D), pl.BlockSpec((1, N_EXPERTS, tm), lambda b, i: (b, 0, i))],
        out_shape=[jax.ShapeDtypeStruct((B, T, D), f32), jax.ShapeDtypeStruct((B, T, D), bf16),
                   jax.ShapeDtypeStruct((B, N_EXPERTS, T), f32)],
        compiler_params=_params(("arbitrary", "arbitrary")),
        name="merge",
    )(x, ret_o, rw_o, rw_o, bonus, bonus, z, z, z, mod2, ng12, gn, lng, lnb, g2, wbr, wbw, wout, wrh, wrl)


def _route_kernel(lt_ref, slot_ref, rt_ref, cnt_ref, slotf_ref, gate_ref, *, cap):
    B, NE, T = lt_ref.shape
    lg = lt_ref[...]
    mx = jnp.max(lg, axis=1, keepdims=True)
    ex = jnp.exp(lg - mx)
    aff = (ex / jnp.sum(ex, axis=1, keepdims=True)).reshape(B * NE, T)
    E = B * NE

    def count_ge(cand):
        return jnp.sum((aff >= cand).astype(f32), axis=1, keepdims=True)

    def exp_step(_, kk):
        k_lo, k_hi = kk
        km = jnp.floor((k_lo + k_hi) * 0.5)
        ok = count_ge(jnp.exp2(-km)) >= cap
        return jnp.where(ok, k_lo, km), jnp.where(ok, km, k_hi)

    k_lo, k_hi = lax.fori_loop(0, ROUTE_EXP_STEPS, exp_step,
                               (jnp.full((E, 1), -1.0, f32), jnp.full((E, 1), ROUTE_MAX_EXP, f32)))
    lo0 = jnp.where(k_hi >= ROUTE_MAX_EXP, 0.0, jnp.exp2(-k_hi))
    hi0 = jnp.exp2(-k_lo)

    def val_step(_, lh):
        lo, hi = lh
        mid = (lo + hi) * 0.5
        ok = count_ge(mid) >= cap
        return jnp.where(ok, mid, lo), jnp.where(ok, hi, mid)

    lo, hi = lax.fori_loop(0, ROUTE_VAL_STEPS, val_step, (lo0, hi0))
    gt = aff >= hi
    eq = (aff >= lo) & (aff < hi)
    need = cap - jnp.sum(gt.astype(f32), axis=1, keepdims=True)
    tri = (lax.broadcasted_iota(jnp.int32, (T, T), 0) < lax.broadcasted_iota(jnp.int32, (T, T), 1)).astype(bf16)
    eq_before = _dot(eq.astype(bf16), tri)
    sel = gt | (eq & (eq_before < need))
    slot = _dot(sel.astype(bf16), tri)
    slot_f = jnp.where(sel, slot, -1.0)
    slot_ref[...] = slot_f.astype(jnp.int32).reshape(B, NE, T)
    before = (lax.broadcasted_iota(jnp.int32, (T, 128), 0)
              < lax.broadcasted_iota(jnp.int32, (T, 128), 1) * GATHER_TILE).astype(bf16)
    cnt_ref[...] = _dot(sel.astype(bf16), before).astype(jnp.int32).reshape(B, NE, 128)
    slotf_ref[...] = slot_f.reshape(B, NE, T)
    gate_ref[...] = jnp.where(sel, aff, 0.0).reshape(B, NE, T)

    def transpose_sample(b, carry):
        packed = jnp.concatenate([slotf_ref[b], gate_ref[b], jnp.zeros((128 - 2 * NE, T), f32)], axis=0)
        rt_ref[b] = packed.T.astype(bf16)
        return carry

    lax.fori_loop(0, B, transpose_sample, 0)


def _route(lt, cap):
    B, E, T = lt.shape
    assert cap <= 256 and T % GATHER_TILE == 0 and T // GATHER_TILE < 128
    return pl.pallas_call(
        functools.partial(_route_kernel, cap=cap),
        grid=(1,),
        in_specs=[pl.BlockSpec((B, E, T), lambda i: (0, 0, 0))],
        out_specs=[pl.BlockSpec((B, E, T), lambda i: (0, 0, 0)), pl.BlockSpec((B, T, 128), lambda i: (0, 0, 0)),
                   pl.BlockSpec((B, E, 128), lambda i: (0, 0, 0))],
        out_shape=[jax.ShapeDtypeStruct((B, E, T), jnp.int32), jax.ShapeDtypeStruct((B, T, 128), bf16),
                   jax.ShapeDtypeStruct((B, E, 128), jnp.int32)],
        scratch_shapes=[pltpu.VMEM((B, E, T), f32), pltpu.VMEM((B, E, T), f32)],
        compiler_params=_params(("arbitrary",)),
        name="route",
    )(lt)


def _ffn_kernel(cnt_ref, slot_ref, h_ref, wg_ref, wu_ref, wd_ref, o_ref, wgb_ref, wub_ref, wdb_ref, xg_ref, *, cap):
    T = h_ref.shape[1]
    e = pl.program_id(0)

    @pl.when(pl.program_id(1) == 0)
    def _():
        wgb_ref[...] = wg_ref[0].astype(bf16)
        wub_ref[...] = wu_ref[0].astype(bf16)
        wdb_ref[...] = wd_ref[0].astype(bf16)

    nb = h_ref.shape[0]
    GT, GW, GA = GATHER_TILE, GATHER_WINDOW, GATHER_ALIGN
    nt = T // GT
    for bb in range(nb):
        b = pl.program_id(1) * nb + bb
        base = [pl.multiple_of(lax.shift_left(lax.shift_right_logical(cnt_ref[b, e, j], GATHER_ALIGN_LOG2),
                                              GATHER_ALIGN_LOG2), GA) for j in range(nt)]
        fits = cnt_ref[b, e, 1] - base[0] <= GW
        for j in range(1, nt):
            fits = fits & (cnt_ref[b, e, j + 1] - base[j] <= GW)

        @pl.when(fits)
        def _():
            xg_ref[bb] = jnp.zeros(xg_ref.shape[1:], f32)
            for j in range(nt):
                rows = base[j] + lax.broadcasted_iota(jnp.int32, (GW, GT), 0)
                onehot = (slot_ref[bb, 0, j:j + 1, :] == rows).astype(bf16)
                xg_ref[bb, pl.ds(base[j], GW), :] += _dot(onehot, h_ref[bb, j * GT:(j + 1) * GT, :])

        @pl.when(jnp.logical_not(fits))
        def _():
            rows = lax.broadcasted_iota(jnp.int32, (cap, GT), 0)
            acc = jnp.zeros((cap, h_ref.shape[2]), f32)
            for j in range(nt):
                onehot = (slot_ref[bb, 0, j:j + 1, :] == rows).astype(bf16)
                acc = acc + _dot(onehot, h_ref[bb, j * GT:(j + 1) * GT, :])
            xg_ref[bb, 0:cap, :] = acc

    xg = jnp.concatenate([xg_ref[bb, 0:cap, :].astype(bf16) for bb in range(nb)], axis=0)
    hg = _dot(xg, wgb_ref[...])
    hu = _dot(xg, wub_ref[...])
    hid = (hg * _sigmoid(hg) * hu).astype(bf16)
    out = _dot(hid, wdb_ref[...]).astype(bf16)
    for bb in range(nb):
        o_ref[bb, 0] = out[bb * cap:(bb + 1) * cap]


def _ffn(cnt, slot4, h2, wg, wu, wd, cap):
    B, T, D = h2.shape
    E, _, F = wg.shape
    nb = FFN_SAMPLES_PER_STEP if B % FFN_SAMPLES_PER_STEP == 0 else 1
    grid_spec = pltpu.PrefetchScalarGridSpec(
        num_scalar_prefetch=1,
        grid=(E, B // nb),
        in_specs=[pl.BlockSpec((nb, 1, T // GATHER_TILE, GATHER_TILE), lambda e, b, c: (b, e, 0, 0)),
                  pl.BlockSpec((nb, T, D), lambda e, b, c: (b, 0, 0)),
                  pl.BlockSpec((1, D, F), lambda e, b, c: (e, 0, 0)),
                  pl.BlockSpec((1, D, F), lambda e, b, c: (e, 0, 0)),
                  pl.BlockSpec((1, F, D), lambda e, b, c: (e, 0, 0))],
        out_specs=pl.BlockSpec((nb, 1, cap, D), lambda e, b, c: (b, e, 0, 0)),
        scratch_shapes=[pltpu.VMEM((D, F), bf16), pltpu.VMEM((D, F), bf16), pltpu.VMEM((F, D), bf16),
                        pltpu.VMEM((nb, cap + GATHER_WINDOW, D), f32)])
    return pl.pallas_call(
        functools.partial(_ffn_kernel, cap=cap),
        grid_spec=grid_spec,
        out_shape=jax.ShapeDtypeStruct((B, E, cap, D), bf16),
        compiler_params=_params(("arbitrary", "arbitrary"), FFN_VMEM_LIMIT),
        name="ffn",
    )(cnt, slot4, h2, wg, wu, wd)


def _combine_kernel(cnt_ref, rt_ref, eo_ref, x1_ref, mod_ref, ng_ref, o_ref, y_ref, *, cap):
    b = pl.program_id(0)
    i = pl.program_id(1)
    rt = rt_ref[0].astype(f32)
    tm = rt.shape[0]
    E = eo_ref.shape[1]
    W = min(SCATTER_WINDOW, cap)
    GS = SCATTER_GROUP
    ST = GATHER_TILE
    for st in range(tm // ST):
        t = i * (tm // ST) + st
        r = rt[st * ST:(st + 1) * ST]
        base, fits = [], None
        for e in range(E):
            lo = cnt_ref[b, e, t]
            be = jnp.minimum(lax.shift_left(lax.shift_right_logical(lo, GATHER_ALIGN_LOG2), GATHER_ALIGN_LOG2),
                             cap - W)
            base.append(pl.multiple_of(be, GATHER_ALIGN))
            ok = cnt_ref[b, e, t + 1] - be <= W
            fits = ok if fits is None else fits & ok

        def scatter_matrix(e, first, width, r=r):
            cols = (first + lax.broadcasted_iota(jnp.int32, (ST, width), 1)).astype(f32)
            return jnp.where(r[:, e:e + 1] == cols, r[:, E + e:E + e + 1], 0.0).astype(bf16)

        @pl.when(fits)
        def _():
            y = jnp.zeros((ST, y_ref.shape[1]), f32)
            for g in range(0, E, GS):
                p = jnp.concatenate([scatter_matrix(e, base[e], W) for e in range(g, g + GS)], axis=1)
                rows = jnp.concatenate([eo_ref[0, e, pl.ds(base[e], W), :] for e in range(g, g + GS)], axis=0)
                y = y + _dot(p, rows)
            y_ref[st * ST:(st + 1) * ST, :] = y

        @pl.when(jnp.logical_not(fits))
        def _():
            y = jnp.zeros((ST, y_ref.shape[1]), f32)
            for e in range(E):
                y = y + _dot(scatter_matrix(e, 0, cap), eo_ref[0, e])
            y_ref[st * ST:(st + 1) * ST, :] = y

    y = y_ref[...]
    yn = y * lax.rsqrt(jnp.mean(y * y, axis=-1, keepdims=True) + NORM_EPS) * ng_ref[...]
    o_ref[0] = x1_ref[0] + mod_ref[0] * yn


def _combine(cnt, rt, eo, x1, g2mod, ng3, cap):
    B, T, D = x1.shape
    E = eo.shape[1]
    tm = 512 if T % 512 == 0 else T
    assert E % SCATTER_GROUP == 0 and tm % GATHER_TILE == 0
    grid_spec = pltpu.PrefetchScalarGridSpec(
        num_scalar_prefetch=1,
        grid=(B, T // tm),
        in_specs=[pl.BlockSpec((1, tm, 128), lambda b, i, c: (b, i, 0)),
                  pl.BlockSpec((1, E, cap, D), lambda b, i, c: (b, 0, 0, 0)),
                  pl.BlockSpec((1, tm, D), lambda b, i, c: (b, i, 0)),
                  pl.BlockSpec((1, 1, D), lambda b, i, c: (b, 0, 0)),
                  pl.BlockSpec((1, D), lambda b, i, c: (0, 0))],
        out_specs=pl.BlockSpec((1, tm, D), lambda b, i, c: (b, i, 0)),
        scratch_shapes=[pltpu.VMEM((tm, D), f32)])
    return pl.pallas_call(
        functools.partial(_combine_kernel, cap=cap),
        grid_spec=grid_spec,
        out_shape=jax.ShapeDtypeStruct((B, T, D), f32),
        compiler_params=_params(("arbitrary", "arbitrary")),
        name="combine",
    )(cnt, rt, eo, x1, g2mod, ng3)


def _permute_columns(w):
    sk, sv, rk, rv, wd, ad = 0, 512, 1024, 1536, 2048, 2112
    q0 = 2176
    rq, rg, rr, gd, mg = q0, q0 + 512, q0 + 1024, q0 + 1536, q0 + 1664
    order = [(mg, 2048), (sk, 512), (sv, 512), (rq, 512), (rg, 512), (rk, 512), (rv, 512), (rr, 512),
             (wd, 64), (ad, 64), (gd, 128)]
    parts = [w[:, a:a + n] for a, n in order]
    parts.append(jnp.zeros((w.shape[0], IN_COLS - USED_COLS), w.dtype))
    return jnp.concatenate(parts, axis=1)


def _rope_tables(T, CT):
    t = jnp.arange(T)
    nfreq = RET_HEAD_DIM // 4
    inv = ROPE_BASE ** (-jnp.arange(nfreq, dtype=f32) / nfreq)
    ang = jnp.concatenate([(t // GRID_W).astype(f32)[:, None] * inv,
                           (t % GRID_W).astype(f32)[:, None] * inv], axis=-1)
    cos, sin = jnp.cos(ang), jnp.sin(ang)
    cosf = jnp.concatenate([cos, cos], axis=1)
    sinf = jnp.concatenate([-sin, sin], axis=1)
    return (jnp.concatenate([cosf, jnp.ones((CT, RET_HEAD_DIM), f32)], axis=0),
            jnp.concatenate([sinf, jnp.zeros((CT, RET_HEAD_DIM), f32)], axis=0))


def kernel(x, c, ctx, c_ctx, w_mod, b_mod, norm_g, w_in, ret_log_decay, ret_gn_g, rwkv_mu, rwkv_k_k, rwkv_k_a,
           rwkv_r_k, rwkv_w0, rwkv_w2, rwkv_a0, rwkv_a2, rwkv_g2, rwkv_ln_g, rwkv_ln_b, w_br_ret, w_br_rwkv,
           w_out, w_router, w_gate, w_up, w_down):
    B, T, D = x.shape
    CT = ctx.shape[1]
    assert w_mod.shape[0] == 1 and D == D_MODEL
    assert CT % ROWS == 0 and T % ROWS == 0 and T % GRID_W == 0
    cap = CAPACITY_FACTOR * T // N_EXPERTS
    assert cap % 8 == 0

    mrows = -(-(B + 1) // 8) * 8
    cc = jnp.zeros((mrows, D), f32).at[:B].set(c).at[B].set(c_ctx)
    mod = _modulation(cc, w_mod[0], b_mod[0])
    lat = mod[:B].reshape(B, N_MOD, D)
    cm = jnp.broadcast_to(mod[B].reshape(1, N_MOD, D), (B, N_MOD, D))
    modrows = jnp.concatenate([lat[:, 0:2], cm[:, 0:2]], axis=1)

    w_perm = _permute_columns(w_in[0]).astype(bf16)
    mu = rwkv_mu[0]
    ss = 2 * RWKV_WIDTH + DECAY_LORA + ICLR_LORA
    mu_full = jnp.zeros((2, IN_COLS), f32)
    mu_full = mu_full.at[:, C_RWK:C_RWK + 1024].set(mu[:, 0:1024])
    mu_full = mu_full.at[:, C_RWR:C_RWR + 512].set(mu[:, ss:ss + 512])
    mu_full = mu_full.at[:, C_LORA:C_LORA + 128].set(mu[:, 1024:ss])
    mu_full = mu_full.at[:, C_LORA + 128:USED_COLS].set(mu[:, ss + 512:])
    cosf, sinf = _rope_tables(T, CT)
    z = _inproj(x, ctx, modrows, norm_g[0, 0:1], w_perm, mu_full, cosf, sinf)

    lg = -jnp.exp(ret_log_decay[0].astype(f32))
    ret_o = _retention(lg, z, CT)

    G = 2 * RWKV_HEAD_DIM
    w2p = jnp.zeros((2, G, RWKV_WIDTH), f32).at[:, :DECAY_LORA].set(rwkv_w2[0]).astype(bf16)
    a2p = jnp.zeros((2, G, RWKV_WIDTH), f32).at[:, DECAY_LORA:].set(rwkv_a2[0]).astype(bf16)
    rw_o, bonus = _rwkv(z, rwkv_w0[0][:, None, :], rwkv_a0[0][:, None, :], w2p, a2p,
                        rwkv_k_k[0][None], rwkv_k_a[0][None], rwkv_r_k[0][None], CT)

    mod2 = jnp.stack([lat[:, 2], lat[:, 3], lat[:, 4]], axis=1)
    wr_pad = jnp.zeros((D, 128), f32).at[:, :N_EXPERTS].set(w_router[0])
    wr_hi = wr_pad.astype(bf16)
    wr_lo = (wr_pad - wr_hi.astype(f32)).astype(bf16)
    x1, h2, lt = _merge(x, ret_o, rw_o, bonus, z, mod2, norm_g[0, 1:3], ret_gn_g[0][None], rwkv_ln_g[0][None],
                        rwkv_ln_b[0][None], rwkv_g2[0].astype(bf16), w_br_ret[0].astype(bf16),
                        w_br_rwkv[0].astype(bf16), w_out[0].astype(bf16), wr_hi, wr_lo)

    slot, rt, cnt = _route(lt, cap)
    cnt = cnt[:, :, :T // GATHER_TILE + 1]
    eo = _ffn(cnt, slot.reshape(B, N_EXPERTS, T // GATHER_TILE, GATHER_TILE), h2, w_gate[0], w_up[0], w_down[0], cap)
    return _combine(cnt, rt, eo, x1, lat[:, 5:6], norm_g[0, 3:4], cap)
```

```python
import functools

import jax
import jax.numpy as jnp
from jax import lax
from jax.experimental import pallas as pl
from jax.experimental.pallas import tpu as pltpu

f32 = jnp.float32
bf16 = jnp.bfloat16

D_MODEL = 1024
GRID_W = 64
RET_HEAD_DIM = 128
RET_WIDTH = 512
RET_HEADS = 4
RET_CHUNK = 128
RET_EPS = 1e-5
ROPE_BASE = 10000.0
RWKV_HEAD_DIM = 64
RWKV_WIDTH = 512
RWKV_PAIRS = 4
RWKV_CHUNK = 64
RWKV_CHUNKS_PER_STEP = 4
RWKV_SAMPLES_PER_STEP = 8
DECAY_LORA = 64
ICLR_LORA = 64
GATE_LORA = 128
RWKV_EPS = 64e-5
N_EXPERTS = 16
EXPERT_FF = 1024
CAPACITY_FACTOR = 2
N_MOD = 6
NORM_EPS = 1e-6

C_MERGE, C_RETK, C_RETV, C_RETQ, C_RETG = 0, 2048, 2560, 3072, 3584
C_RWK, C_RWV, C_RWR, C_LORA = 4096, 4608, 5120, 5632
USED_COLS = 5888
TN = 512
IN_COLS = -(-USED_COLS // TN) * TN
ROWS = 256
STEP_ROWS = 768
POST_ROWS = 128
QUERY_ROWS = 1024
MERGE_ROWS = 512

ROUTE_MAX_EXP = 126.0
ROUTE_EXP_STEPS = 8
ROUTE_VAL_STEPS = 26

VMEM_LIMIT = 56 * 1024 * 1024
FFN_SAMPLES_PER_STEP = 2
GATHER_TILE = 256
GATHER_ALIGN_LOG2 = 4
GATHER_ALIGN = 1 << GATHER_ALIGN_LOG2
GATHER_WINDOW = 80
SCATTER_WINDOW = 64
SCATTER_GROUP = 4
FFN_VMEM_LIMIT = 62 * 1024 * 1024


def _dot(a, b):
    return jnp.dot(a, b, preferred_element_type=f32)


def _dot_nt(a, b):
    return lax.dot_general(a, b, (((1,), (1,)), ((), ())), preferred_element_type=f32)


def _dot_tn(a, b):
    return lax.dot_general(a, b, (((0,), (0,)), ((), ())), preferred_element_type=f32)


def _bdot(a, b):
    return _dot(a.astype(bf16), b.astype(bf16))


def _split_dot(x, w):
    hi = x.astype(bf16)
    lo = (x - hi.astype(f32)).astype(bf16)
    return _dot(jnp.concatenate([hi, lo], axis=1), jnp.concatenate([w, w], axis=0))


def _sigmoid(x):
    return 0.5 * jnp.tanh(0.5 * x) + 0.5


def _params(sem, limit=VMEM_LIMIT):
    return pltpu.CompilerParams(dimension_semantics=sem, vmem_limit_bytes=limit)


def _mod_kernel(c_ref, w_ref, b_ref, o_ref):
    c = c_ref[...]
    s = c * jax.nn.sigmoid(c)
    o_ref[...] = _bdot(s, w_ref[...]) + b_ref[...]


def _modulation(cc, w_mod, b_mod):
    m, d = cc.shape
    n = w_mod.shape[1]
    tn = 512
    return pl.pallas_call(
        _mod_kernel,
        grid=(n // tn,),
        in_specs=[pl.BlockSpec((m, d), lambda j: (0, 0)),
                  pl.BlockSpec((d, tn), lambda j: (0, j)),
                  pl.BlockSpec((1, tn), lambda j: (0, j))],
        out_specs=pl.BlockSpec((m, tn), lambda j: (0, j)),
        out_shape=jax.ShapeDtypeStruct((m, n), f32),
        compiler_params=_params(("arbitrary",)),
        name="mod",
    )(cc, w_mod, b_mod.reshape(1, n))


def _inproj_kernel(x_ref, c_ref, mod_ref, g_ref, w_ref, mu_ref, cos_ref, sin_ref, o_ref, h_ref, z_ref):
    n = pl.program_id(1)
    T = x_ref.shape[1]
    L = T + c_ref.shape[1]
    nlat = T // ROWS
    nchunk = L // ROWS
    PAD = 8

    SR = STEP_ROWS if L % STEP_ROWS == 0 else ROWS

    @pl.when(n == 0)
    def _():
        z_ref[0:PAD, :] = jnp.zeros((PAD, TN), f32)
        z_ref[PAD + L:PAD + L + PAD, :] = jnp.zeros((PAD, TN), f32)

        def norm_chunk(src_ref, s0, r0, o):
            xb = src_ref[0, pl.ds(s0, ROWS), :]
            y = xb * lax.rsqrt(jnp.mean(xb * xb, axis=-1, keepdims=True) + NORM_EPS) * g_ref[...]
            sh = mod_ref[0, o:o + 1, :]
            sc = mod_ref[0, o + 1:o + 2, :]
            h_ref[pl.ds(r0, ROWS), :] = (y * (1.0 + sc) + sh).astype(bf16)

        def lat_body(i, carry):
            r0 = pl.multiple_of(i * ROWS, ROWS)
            norm_chunk(x_ref, r0, r0, 0)
            return carry

        lax.fori_loop(0, nlat, lat_body, 0)
        for j in range(nlat, nchunk):
            norm_chunk(c_ref, j * ROWS - T, j * ROWS, 2)

    PR = POST_ROWS
    QR = QUERY_ROWS

    def rope(scale):
        def post(z, r0):
            if scale != 1.0:
                z = z * scale
            cs = cos_ref[pl.ds(r0, z.shape[0]), :]
            sn = sin_ref[pl.ds(r0, z.shape[0]), :]
            parts = []
            for hh in range(TN // RET_HEAD_DIM):
                zh = z[:, hh * RET_HEAD_DIM:(hh + 1) * RET_HEAD_DIM]
                parts.append(zh * cs + pltpu.roll(zh, RET_HEAD_DIM // 2, 1) * sn)
            return jnp.concatenate(parts, axis=1)
        return post

    def shifted(r0, at_boundary):
        win = z_ref[pl.ds(r0, PR + 2 * PAD), :]
        prev = win[PAD - 1:PAD - 1 + PR]
        z = win[PAD:PAD + PR]
        nxt = win[PAD + 1:PAD + 1 + PR]
        if at_boundary:
            rid = r0 + lax.broadcasted_iota(jnp.int32, (PR, 1), 0)
            prev = jnp.where(rid == T, 0.0, prev)
            nxt = jnp.where(rid == T - 1, 0.0, nxt)
        mu0 = mu_ref[0:1, :]
        mu1 = mu_ref[1:2, :]
        return (1.0 - mu0 - mu1) * z + mu0 * prev + mu1 * nxt

    def lora_act(zs):
        lane = lax.broadcasted_iota(jnp.int32, (1, TN), 1)
        return jnp.where(lane < DECAY_LORA, jnp.tanh(zs),
                         jnp.where(lane < DECAY_LORA + ICLR_LORA, zs, jax.nn.sigmoid(zs)))

    def product(r0, rows=SR):
        return _dot(h_ref[pl.ds(r0, rows), :], w_ref[...])

    def direct(post, latent_only=False):
        rows, total = (QR, T) if latent_only and T % QR == 0 else (SR, L)

        def body(i, carry):
            r0 = pl.multiple_of(i * rows, rows)
            o_ref[0, pl.ds(r0, rows), :] = post(product(r0, rows), r0).astype(bf16)
            return carry
        lax.fori_loop(0, total // rows, body, 0)
        if total < L:
            o_ref[0, total:L, :] = jnp.zeros((L - total, TN), bf16)

    def via_buffer(act):
        def mm_body(i, carry):
            r0 = pl.multiple_of(i * SR, SR)
            z_ref[pl.ds(PAD + r0, SR), :] = product(r0)
            return carry
        lax.fori_loop(0, L // SR, mm_body, 0)

        def post(r0, at_boundary):
            o_ref[0, pl.ds(r0, PR), :] = act(shifted(r0, at_boundary)).astype(bf16)

        def post_body(i, carry):
            post(pl.multiple_of(i * PR, PR), False)
            return carry

        seam = T // PR
        lax.fori_loop(0, seam - 1, post_body, 0)
        post((seam - 1) * PR, True)
        post(seam * PR, True)
        lax.fori_loop(seam + 1, L // PR, post_body, 0)

    t = lambda c: c // TN
    pl.when(n < t(C_RETK))(lambda: direct(lambda z, r0: _sigmoid(z), latent_only=True))
    pl.when((n >= t(C_RETK)) & (n < t(C_RETV)))(lambda: direct(rope(RET_HEAD_DIM ** -0.5)))
    pl.when((n >= t(C_RETV)) & (n < t(C_RETQ)))(lambda: direct(lambda z, r0: z))
    pl.when((n >= t(C_RETQ)) & (n < t(C_RETG)))(lambda: direct(rope(1.0), latent_only=True))
    pl.when((n >= t(C_RETG)) & (n < t(C_RWK)))(lambda: direct(lambda z, r0: z * _sigmoid(z), latent_only=True))
    pl.when((n >= t(C_RWK)) & (n < t(C_LORA)))(lambda: via_buffer(lambda zs: zs))
    pl.when(n >= t(C_LORA))(lambda: via_buffer(lora_act))


def _inproj(x, ctx, modrows, g0, w_perm, mu_full, cosf, sinf):
    B, T, D = x.shape
    CT = ctx.shape[1]
    L = T + CT
    nt = IN_COLS // TN
    return pl.pallas_call(
        _inproj_kernel,
        grid=(B, nt),
        in_specs=[pl.BlockSpec((1, T, D), lambda b, n: (b, 0, 0)),
                  pl.BlockSpec((1, CT, D), lambda b, n: (b, 0, 0)),
                  pl.BlockSpec((1, 4, D), lambda b, n: (b, 0, 0)),
                  pl.BlockSpec((1, D), lambda b, n: (0, 0)),
                  pl.BlockSpec((D, TN), lambda b, n: (0, n)),
                  pl.BlockSpec((2, TN), lambda b, n: (0, n)),
                  pl.BlockSpec((L, RET_HEAD_DIM), lambda b, n: (0, 0)),
                  pl.BlockSpec((L, RET_HEAD_DIM), lambda b, n: (0, 0))],
        out_specs=pl.BlockSpec((1, L, TN), lambda b, n: (b, 0, n)),
        out_shape=jax.ShapeDtypeStruct((B, L, IN_COLS), bf16),
        scratch_shapes=[pltpu.VMEM((L, D), bf16), pltpu.VMEM((L + 16, TN), f32)],
        compiler_params=_params(("arbitrary", "arbitrary")),
        name="inproj",
    )(x, ctx, modrows, g0, w_perm, mu_full, cosf, sinf)


def _ret_kernel(lg_ref, q_ref, k_ref, v_ref, o_ref, r_ref, tab_ref, *, ctx):
    L = q_ref.shape[1]
    Cc = RET_CHUNK
    hd = RET_HEAD_DIM
    lat = L - ctx
    nc = ctx // Cc
    nl = lat // Cc
    combos = [(h, d) for h in range(RET_HEADS) for d in (0, 1)]
    INTRA, CROSS, TAIL, DECAY = 0, 1, 2, 3

    @pl.when(pl.program_id(0) == 0)
    def _():
        ii = lax.broadcasted_iota(jnp.int32, (Cc, Cc), 0).astype(f32)
        jj = lax.broadcasted_iota(jnp.int32, (Cc, Cc), 1).astype(f32)
        for idx, (h, d) in enumerate(combos):
            lg = lg_ref[d, h]
            if d == 0:
                diff = ii - jj
                cross = jnp.exp(lg * (ii + 1.0))
                tailw = jnp.exp(lg * (Cc - 1.0 - ii))
            else:
                diff = jj - ii
                cross = jnp.exp(lg * (Cc - ii))
                tailw = jnp.exp(lg * ii)
            tab_ref[idx, INTRA] = jnp.where(diff >= 0, jnp.exp(lg * jnp.maximum(diff, 0.0)), 0.0)
            tab_ref[idx, CROSS] = cross
            tab_ref[idx, TAIL] = tailw
            tab_ref[idx, DECAY] = jnp.exp(jnp.zeros((Cc, Cc), f32) + lg * Cc)

    r_ref[...] = jnp.zeros(r_ref.shape, f32)
    o_ref[...] = jnp.zeros(o_ref.shape, f32)

    def cols(h):
        return slice(h * hd, (h + 1) * hd)

    def update(idx, kc, vc):
        ks = (kc.astype(f32) * tab_ref[idx, TAIL]).astype(bf16)
        r_ref[idx] = r_ref[idx] * tab_ref[idx, DECAY] + _dot_tn(ks, vc)

    def ctx_step(s, carry):
        for idx, (h, d) in enumerate(combos):
            row0 = pl.multiple_of(lat + (s * Cc if d == 0 else (nc - 1 - s) * Cc), Cc)
            update(idx, k_ref[0, pl.ds(row0, Cc), cols(h)], v_ref[0, pl.ds(row0, Cc), cols(h)])
        return carry

    def lat_step(s, carry):
        t0s, qs, ks, vs = [], [], [], []
        for h, d in combos:
            t0 = pl.multiple_of(s * Cc if d == 0 else (nl - 1 - s) * Cc, Cc)
            row0 = t0
            t0s.append(t0)
            qs.append(q_ref[0, pl.ds(row0, Cc), cols(h)])
            ks.append(k_ref[0, pl.ds(row0, Cc), cols(h)])
            vs.append(v_ref[0, pl.ds(row0, Cc), cols(h)])
        n = len(combos)
        sc = [(_dot_nt(qs[i], ks[i]) * tab_ref[i, INTRA]).astype(bf16) for i in range(n)]
        oc = [_dot(qs[i], r_ref[i].astype(bf16)) * tab_ref[i, CROSS] for i in range(n)]
        oi = [_dot(sc[i], vs[i]) for i in range(n)]
        for i, (h, d) in enumerate(combos):
            o_ref[0, pl.ds(t0s[i], Cc), cols(h)] += oi[i] + oc[i]
        for i in range(n):
            update(i, ks[i], vs[i])
        return carry

    lax.fori_loop(0, nc, ctx_step, 0)
    lax.fori_loop(0, nl, lat_step, 0)


def _retention(lg, z, ctx):
    B, L, _ = z.shape
    T = L - ctx
    W = RET_WIDTH
    Cc = RET_CHUNK
    blk = lambda c0: pl.BlockSpec((1, L, W), lambda b: (b, 0, c0 // W))
    nchain = 2 * RET_HEADS
    return pl.pallas_call(
        functools.partial(_ret_kernel, ctx=ctx),
        grid=(B,),
        in_specs=[pl.BlockSpec(memory_space=pltpu.SMEM), blk(C_RETQ), blk(C_RETK), blk(C_RETV)],
        out_specs=pl.BlockSpec((1, T, W), lambda b: (b, 0, 0)),
        out_shape=jax.ShapeDtypeStruct((B, T, W), f32),
        scratch_shapes=[pltpu.VMEM((nchain, Cc, Cc), f32), pltpu.VMEM((nchain, 4, Cc, Cc), f32)],
        compiler_params=_params(("arbitrary",)),
        name="ret",
    )(lg, z, z, z)


def _rwkv_kernel(k_ref, v_ref, r_ref, lo_ref, w0_ref, a0_ref, w2_ref, a2_ref, kk_ref, ka_ref, rk_ref,
                 o_ref, bo_ref, s_ref, *, nc, nb, nsub):
    d = pl.program_id(1)
    s = pl.program_id(2)
    C = RWKV_CHUNK

    @pl.when(s == 0)
    def _():
        s_ref[...] = jnp.zeros(s_ref.shape, f32)

    def sub_chunk(j, carry):
        r0 = pl.multiple_of(jnp.where(d == 1, nsub - 1 - j, j) * C, C)
        rows = lambda ref: ref.at[:, pl.ds(r0, C), :]
        _rwkv_chunk(rows(k_ref), rows(v_ref), rows(r_ref), rows(lo_ref), w0_ref, a0_ref, w2_ref, a2_ref,
                    kk_ref, ka_ref, rk_ref, o_ref.at[:, :, pl.ds(r0, C), :], bo_ref.at[:, :, pl.ds(r0, C), :],
                    s_ref, rev=d == 1, emit=s >= nc, nb=nb)
        return carry

    lax.fori_loop(0, nsub, sub_chunk, 0)


def _rwkv_chunk(k_ref, v_ref, r_ref, lo_ref, w0_ref, a0_ref, w2_ref, a2_ref, kk_ref, ka_ref, rk_ref,
                o_ref, bo_ref, s_ref, *, rev, emit, nb):
    C = RWKV_CHUNK
    G = 2 * RWKV_HEAD_DIM

    ii = lax.broadcasted_iota(jnp.int32, (C, G), 0)
    lane = lax.broadcasted_iota(jnp.int32, (C, G), 1)
    jj = lane & (RWKV_HEAD_DIM - 1)
    head0 = lane < RWKV_HEAD_DIM
    dlt = jnp.where(rev, ii - jj, jj - ii)
    strict = dlt < 0
    incl = dlt <= 0
    eye = (ii == jj).astype(f32)
    gi = lax.broadcasted_iota(jnp.int32, (G, G), 0)
    gj = lax.broadcasted_iota(jnp.int32, (G, G), 1)
    blockdiag = (gi >= RWKV_HEAD_DIM) == (gj >= RWKV_HEAD_DIM)
    ones_bd = blockdiag.astype(bf16)
    tri = incl[:, 0:C].astype(bf16)
    base = (ii >> 1) == (jj >> 1)
    offs = [((ii >> (lv + 1)) == (jj >> (lv + 1))) & ((ii >> lv) != (jj >> lv)) for lv in range(1, 6)]

    def segsum(x):
        return jnp.concatenate(
            [_bdot(x[:, p * G:(p + 1) * G], ones_bd) for p in range(RWKV_PAIRS)], axis=1)

    def stack(x):
        xb = x.astype(bf16)
        zero = jnp.zeros_like(xb)
        return jnp.concatenate([jnp.where(head0, xb, zero), jnp.where(head0, zero, xb)], axis=0)

    W = k_ref.shape[2]
    k_all = k_ref[...].astype(f32).reshape(nb * C, W)
    v_all = v_ref[...].astype(f32).reshape(nb * C, W)
    r_all = r_ref[...].astype(f32).reshape(nb * C, W)
    lo_all = lo_ref[:, :, 0:DECAY_LORA + ICLR_LORA].reshape(nb * C, DECAY_LORA + ICLR_LORA)
    u_all = w0_ref[0] + _dot(lo_all, w2_ref[0])
    softplus = jnp.maximum(-u_all, 0.0) + jnp.log1p(jnp.exp(-jnp.abs(u_all)))
    lw_all = -jnp.exp(-softplus - 0.5)
    a_all = jax.nn.sigmoid(a0_ref[0] + _dot(lo_all, a2_ref[0]))
    kkr = k_all * kk_ref[...]
    kk_all = kkr * lax.rsqrt(segsum(kkr * kkr) + 1e-12)
    kd_all = k_all * (1.0 + (a_all - 1.0) * ka_ref[...])
    be_all = kk_all * a_all
    bonus_all = segsum(r_all * kd_all * rk_ref[...]) * v_all
    pre = []
    for bb in range(nb):
        rows = slice(bb * C, (bb + 1) * C)
        vx, rx, lw, kk, kd, be, bonus = (arr[rows] for arr in (v_all, r_all, lw_all, kk_all, kd_all, be_all,
                                                                bonus_all))
        cum = _split_dot_left(tri, lw)
        tot = jnp.where(rev, cum[0:1, :], cum[C - 1:C, :])
        gneg = jnp.exp(-cum)
        gh = jnp.exp(tot - cum)
        pre.append(dict(alb=-kk * jnp.exp(cum - lw), rb=rx * jnp.exp(cum), beb=be * gneg, kb=kd * gneg,
                        beh=be * gh, kh=kd * gh, etot=jnp.exp(tot), v=vx, bonus=bonus))

    units = [(bb, p) for bb in range(nb) for p in range(RWKV_PAIRS)]
    sl = lambda p: slice(p * G, (p + 1) * G)
    part = lambda un, name: pre[un[0]][name][:, sl(un[1])]
    S = {un: s_ref[un[0] * RWKV_PAIRS + un[1]] for un in units}
    Sb = {un: S[un].astype(bf16) for un in units}
    X = {un: part(un, "alb").astype(bf16) for un in units}
    Rb = {un: part(un, "rb").astype(bf16) for un in units}
    Ybs = {un: stack(part(un, "beb")) for un in units}
    Yks = {un: stack(part(un, "kb")) for un in units}
    Vb = {un: part(un, "v").astype(bf16) for un in units}
    Vs = {un: stack(part(un, "v")) for un in units}

    XR = {un: jnp.concatenate([X[un], Rb[un]], axis=0) for un in units}
    ABO = {un: _dot_nt(XR[un], jnp.concatenate([Ybs[un], Yks[un], Sb[un]], axis=0)) for un in units}
    APb = {un: ABO[un][:, 0:G] for un in units}
    APk = {un: ABO[un][:, G:2 * G] for un in units}
    BO = {un: ABO[un][:, 2 * G:3 * G] for un in units}
    Aab = {un: jnp.where(strict, APb[un][0:C], 0.0) for un in units}
    Aak = {un: jnp.where(strict, APk[un][0:C], 0.0).astype(bf16) for un in units}
    Pab = {un: jnp.where(incl, APb[un][C:2 * C], 0.0).astype(bf16) for un in units}
    Pak = {un: jnp.where(incl, APk[un][C:2 * C], 0.0).astype(bf16) for un in units}

    Tm = {un: eye + jnp.where(base, Aab[un], 0.0) for un in units}
    for off in offs:
        Xs = {un: _dot(jnp.where(off, Aab[un], 0.0).astype(bf16), stack(Tm[un])) for un in units}
        Tm = {un: Tm[un] + _dot(Tm[un].astype(bf16), stack(Xs[un])) for un in units}

    AV = {un: _dot(jnp.concatenate([Aak[un], Pak[un]], axis=0), Vs[un]) for un in units}
    Bm = {un: BO[un][0:C] + AV[un][0:C] for un in units}
    U = {un: _dot(Tm[un].astype(bf16), stack(Bm[un])) for un in units}
    Om = {un: BO[un][C:2 * C] + AV[un][C:2 * C] + _dot(Pab[un], stack(U[un])) for un in units}
    for un in units:
        bb, p = un
        upd = _dot_tn(jnp.concatenate([U[un].astype(bf16), Vb[un]], axis=0),
                      jnp.concatenate([part(un, "beh").astype(bf16), part(un, "kh").astype(bf16)], axis=0))
        s_ref[bb * RWKV_PAIRS + p] = jnp.where(blockdiag, S[un] * part(un, "etot") + upd, 0.0)

    @pl.when(emit)
    def _():
        for bb in range(nb):
            o_ref[0, bb] = jnp.concatenate([Om[bb, p] for p in range(RWKV_PAIRS)], axis=1)
            bo_ref[0, bb] = pre[bb]["bonus"]


def _split_dot_left(w, x):
    hi = x.astype(bf16)
    lo = (x - hi.astype(f32)).astype(bf16)
    return _dot(jnp.concatenate([w, w], axis=1), jnp.concatenate([hi, lo], axis=0))


def _rwkv(z, w0, a0, w2p, a2p, k_k, k_a, r_k, ctx):
    B, L, _ = z.shape
    T = L - ctx
    nsub = RWKV_CHUNKS_PER_STEP
    while ctx % (nsub * RWKV_CHUNK) or T % (nsub * RWKV_CHUNK):
        nsub //= 2
    C = nsub * RWKV_CHUNK
    nc, nl = ctx // C, T // C
    W = RWKV_WIDTH

    def chunk(d, s):
        fwd = jnp.where(s < nc, nl + s, s - nc)
        bwd = jnp.where(s < nc, nl + nc - 1 - s, nl - 1 - (s - nc))
        return jnp.where(d == 0, fwd, bwd)

    def ochunk(d, s):
        sl = jnp.maximum(s - nc, 0)
        return jnp.where(d == 0, sl, nl - 1 - sl)

    nb = RWKV_SAMPLES_PER_STEP if B % RWKV_SAMPLES_PER_STEP == 0 else 1
    zblk = lambda c0, w: pl.BlockSpec((nb, C, w), lambda b, d, s: (b, chunk(d, s), c0 // w))
    dpar = lambda r: pl.BlockSpec((1, r, W), lambda b, d, s: (d, 0, 0))
    par = pl.BlockSpec((1, W), lambda b, d, s: (0, 0))
    oblk = pl.BlockSpec((1, nb, C, W), lambda b, d, s: (d, b, ochunk(d, s), 0))
    G = 2 * RWKV_HEAD_DIM
    return pl.pallas_call(
        functools.partial(_rwkv_kernel, nc=nc, nb=nb, nsub=nsub),
        grid=(B // nb, 2, nc + nl),
        in_specs=[zblk(C_RWK, W), zblk(C_RWV, W), zblk(C_RWR, W), zblk(C_LORA, 256),
                  dpar(1), dpar(1), dpar(G), dpar(G), par, par, par],
        out_specs=[oblk, oblk],
        out_shape=[jax.ShapeDtypeStruct((2, B, T, W), f32), jax.ShapeDtypeStruct((2, B, T, W), f32)],
        scratch_shapes=[pltpu.VMEM((nb * RWKV_PAIRS, G, G), f32)],
        compiler_params=_params(("arbitrary", "arbitrary", "arbitrary")),
        name="rwkv",
    )(z, z, z, z, w0, a0, w2p, a2p, k_k, k_a, r_k)


def _merge_kernel(x_ref, ret_ref, rw0_ref, rw1_ref, b0_ref, b1_ref, gate_ref, retg_ref, gd_ref, mod_ref,
                  ng_ref, gn_ref, lng_ref, lnb_ref, g2_ref, wbr_ref, wbw_ref, wout_ref, wrh_ref, wrl_ref,
                  x1_ref, h2_ref, lt_ref):
    hd = RET_HEAD_DIM
    ret = ret_ref[0]
    parts = []
    for hh in range(RET_HEADS):
        xh = ret[:, hh * hd:(hh + 1) * hd]
        mu = jnp.mean(xh, axis=-1, keepdims=True)
        dv = xh - mu
        var = jnp.mean(dv * dv, axis=-1, keepdims=True)
        parts.append(dv * lax.rsqrt(var + RET_EPS))
    yr = retg_ref[0].astype(f32) * (jnp.concatenate(parts, axis=1) * gn_ref[...])
    y_ret = _bdot(yr, wbr_ref[...])
    W = RWKV_WIDTH
    gi = lax.broadcasted_iota(jnp.int32, (W, W), 0)
    gj = lax.broadcasted_iota(jnp.int32, (W, W), 1)
    ones_bd = ((gi >> 6) == (gj >> 6)).astype(bf16)
    o = rw0_ref[0, 0] + rw1_ref[0, 0]
    mu = _split_dot(o, ones_bd) * (1.0 / RWKV_HEAD_DIM)
    dv = o - mu
    var = _bdot(dv * dv, ones_bd) * (1.0 / RWKV_HEAD_DIM)
    yw = dv * lax.rsqrt(var + RWKV_EPS) * lng_ref[...] + lnb_ref[...]
    gate = _dot(gd_ref[0], g2_ref[...])
    yw = (yw + b0_ref[0, 0] + b1_ref[0, 0]) * gate
    y_rw = _bdot(yw, wbw_ref[...])
    D = y_ret.shape[1]
    g = gate_ref[0].astype(f32)
    m = g[:, :D] * y_ret + g[:, D:] * y_rw
    y = _bdot(m, wout_ref[...])

    def rms(v, gg):
        return v * lax.rsqrt(jnp.mean(v * v, axis=-1, keepdims=True) + NORM_EPS) * gg

    x1 = x_ref[0] + mod_ref[0, 0:1, :] * rms(y, ng_ref[0:1, :])
    x1_ref[0] = x1
    h2 = rms(x1, ng_ref[1:2, :]) * (1.0 + mod_ref[0, 2:3, :]) + mod_ref[0, 1:2, :]
    h2b = h2.astype(bf16)
    h2_ref[0] = h2b
    h2l = (h2 - h2b.astype(f32)).astype(bf16)
    lgt = _dot(jnp.concatenate([h2b, h2l, h2b], axis=1),
               jnp.concatenate([wrh_ref[...], wrh_ref[...], wrl_ref[...]], axis=0))
    lt_ref[0] = lgt.T[0:N_EXPERTS, :]


def _merge(x, ret_o, rw_o, bonus, z, mod2, ng12, gn, lng, lnb, g2, wbr, wbw, wout, wrh, wrl):
    B, T, D = x.shape
    tm = MERGE_ROWS if T % MERGE_ROWS == 0 else ROWS
    W = RWKV_WIDTH
    row = lambda w: pl.BlockSpec((1, tm, w), lambda b, i: (b, i, 0))
    dblk = lambda dd: pl.BlockSpec((1, 1, tm, W), lambda b, i: (dd, b, i, 0))
    zblk = lambda c0, w: pl.BlockSpec((1, tm, w), lambda b, i: (b, i, c0 // w))
    full = lambda a: pl.BlockSpec(a.shape, lambda b, i: (0,) * a.ndim)
    return pl.pallas_call(
        _merge_kernel,
        grid=(B, T // tm),
        in_specs=[row(D), row(W), dblk(0), dblk(1), dblk(0), dblk(1),
                  zblk(C_MERGE, 2 * D), zblk(C_RETG, W), zblk(C_LORA + 128, 128),
                  pl.BlockSpec((1, 3, D), lambda b, i: (b, 0, 0)),
                  full(ng12), full(gn), full(lng), full(lnb), full(g2), full(wbr), full(wbw), full(wout),
                  full(wrh), full(wrl)],
        out_specs=[row(D), row(D), pl.BlockSpec((1, N_EXPERTS, tm), lambda b, i: (b, 0, i))],
        out_shape=[jax.ShapeDtypeStruct((B, T, D), f32), jax.ShapeDtypeStruct((B, T, D), bf16),
                   jax.ShapeDtypeStruct((B, N_EXPERTS, T), f32)],
        compiler_params=_params(("arbitrary", "arbitrary")),
        name="merge",
    )(x, ret_o, rw_o, rw_o, bonus, bonus, z, z, z, mod2, ng12, gn, lng, lnb, g2, wbr, wbw, wout, wrh, wrl)


def _route_kernel(lt_ref, slot_ref, rt_ref, cnt_ref, slotf_ref, gate_ref, *, cap):
    B, NE, T = lt_ref.shape
    lg = lt_ref[...]
    mx = jnp.max(lg, axis=1, keepdims=True)
    ex = jnp.exp(lg - mx)
    aff = (ex / jnp.sum(ex, axis=1, keepdims=True)).reshape(B * NE, T)
    E = B * NE

    def count_ge(cand):
        return jnp.sum((aff >= cand).astype(f32), axis=1, keepdims=True)

    def exp_step(_, kk):
        k_lo, k_hi = kk
        km = jnp.floor((k_lo + k_hi) * 0.5)
        ok = count_ge(jnp.exp2(-km)) >= cap
        return jnp.where(ok, k_lo, km), jnp.where(ok, km, k_hi)

    k_lo, k_hi = lax.fori_loop(0, ROUTE_EXP_STEPS, exp_step,
                               (jnp.full((E, 1), -1.0, f32), jnp.full((E, 1), ROUTE_MAX_EXP, f32)))
    lo0 = jnp.where(k_hi >= ROUTE_MAX_EXP, 0.0, jnp.exp2(-k_hi))
    hi0 = jnp.exp2(-k_lo)

    def val_step(_, lh):
        lo, hi = lh
        mid = (lo + hi) * 0.5
        ok = count_ge(mid) >= cap
        return jnp.where(ok, mid, lo), jnp.where(ok, hi, mid)

    lo, hi = lax.fori_loop(0, ROUTE_VAL_STEPS, val_step, (lo0, hi0))
    gt = aff >= hi
    eq = (aff >= lo) & (aff < hi)
    need = cap - jnp.sum(gt.astype(f32), axis=1, keepdims=True)
    tri = (lax.broadcasted_iota(jnp.int32, (T, T), 0) < lax.broadcasted_iota(jnp.int32, (T, T), 1)).astype(bf16)
    eq_before = _dot(eq.astype(bf16), tri)
    sel = gt | (eq & (eq_before < need))
    slot = _dot(sel.astype(bf16), tri)
    slot_f = jnp.where(sel, slot, -1.0)
    slot_ref[...] = slot_f.astype(jnp.int32).reshape(B, NE, T)
    before = (lax.broadcasted_iota(jnp.int32, (T, 128), 0)
              < lax.broadcasted_iota(jnp.int32, (T, 128), 1) * GATHER_TILE).astype(bf16)
    cnt_ref[...] = _dot(sel.astype(bf16), before).astype(jnp.int32).reshape(B, NE, 128)
    slotf_ref[...] = slot_f.reshape(B, NE, T)
    gate_ref[...] = jnp.where(sel, aff, 0.0).reshape(B, NE, T)

    def transpose_sample(b, carry):
        packed = jnp.concatenate([slotf_ref[b], gate_ref[b], jnp.zeros((128 - 2 * NE, T), f32)], axis=0)
        rt_ref[b] = packed.T.astype(bf16)
        return carry

    lax.fori_loop(0, B, transpose_sample, 0)


def _route(lt, cap):
    B, E, T = lt.shape
    assert cap <= 256 and T % GATHER_TILE == 0 and T // GATHER_TILE < 128
    return pl.pallas_call(
        functools.partial(_route_kernel, cap=cap),
        grid=(1,),
        in_specs=[pl.BlockSpec((B, E, T), lambda i: (0, 0, 0))],
        out_specs=[pl.BlockSpec((B, E, T), lambda i: (0, 0, 0)), pl.BlockSpec((B, T, 128), lambda i: (0, 0, 0)),
                   pl.BlockSpec((B, E, 128), lambda i: (0, 0, 0))],
        out_shape=[jax.ShapeDtypeStruct((B, E, T), jnp.int32), jax.ShapeDtypeStruct((B, T, 128), bf16),
                   jax.ShapeDtypeStruct((B, E, 128), jnp.int32)],
        scratch_shapes=[pltpu.VMEM((B, E, T), f32), pltpu.VMEM((B, E, T), f32)],
        compiler_params=_params(("arbitrary",)),
        name="route",
    )(lt)


def _ffn_kernel(cnt_ref, slot_ref, h_ref, wg_ref, wu_ref, wd_ref, o_ref, wgu_ref, wdb_ref, xg_ref, *, cap):
    T = h_ref.shape[1]
    F = wg_ref.shape[2]
    e = pl.program_id(0)

    @pl.when(pl.program_id(1) == 0)
    def _():
        wgu_ref[:, 0:F] = wg_ref[0].astype(bf16)
        wgu_ref[:, F:2 * F] = wu_ref[0].astype(bf16)
        wdb_ref[...] = wd_ref[0].astype(bf16)

    nb = h_ref.shape[0]
    GT, GW, GA = GATHER_TILE, GATHER_WINDOW, GATHER_ALIGN
    nt = T // GT
    for bb in range(nb):
        b = pl.program_id(1) * nb + bb
        base = [pl.multiple_of(lax.shift_left(lax.shift_right_logical(cnt_ref[b, e, j], GATHER_ALIGN_LOG2),
                                              GATHER_ALIGN_LOG2), GA) for j in range(nt)]
        fits = cnt_ref[b, e, 1] - base[0] <= GW
        for j in range(1, nt):
            fits = fits & (cnt_ref[b, e, j + 1] - base[j] <= GW)

        @pl.when(fits)
        def _():
            xg_ref[bb] = jnp.zeros(xg_ref.shape[1:], f32)
            for j in range(nt):
                rows = base[j] + lax.broadcasted_iota(jnp.int32, (GW, GT), 0)
                onehot = (slot_ref[bb, 0, j:j + 1, :] == rows).astype(bf16)
                xg_ref[bb, pl.ds(base[j], GW), :] += _dot(onehot, h_ref[bb, j * GT:(j + 1) * GT, :])

        @pl.when(jnp.logical_not(fits))
        def _():
            rows = lax.broadcasted_iota(jnp.int32, (cap, GT), 0)
            acc = jnp.zeros((cap, h_ref.shape[2]), f32)
            for j in range(nt):
                onehot = (slot_ref[bb, 0, j:j + 1, :] == rows).astype(bf16)
                acc = acc + _dot(onehot, h_ref[bb, j * GT:(j + 1) * GT, :])
            xg_ref[bb, 0:cap, :] = acc

    xg = jnp.concatenate([xg_ref[bb, 0:cap, :].astype(bf16) for bb in range(nb)], axis=0)
    hgu = _dot(xg, wgu_ref[...])
    hg = hgu[:, 0:F]
    hu = hgu[:, F:2 * F]
    hid = (hg * _sigmoid(hg) * hu).astype(bf16)
    out = _dot(hid, wdb_ref[...]).astype(bf16)
    for bb in range(nb):
        o_ref[bb, 0] = out[bb * cap:(bb + 1) * cap]


def _ffn(cnt, slot4, h2, wg, wu, wd, cap):
    B, T, D = h2.shape
    E, _, F = wg.shape
    nb = FFN_SAMPLES_PER_STEP if B % FFN_SAMPLES_PER_STEP == 0 else 1
    grid_spec = pltpu.PrefetchScalarGridSpec(
        num_scalar_prefetch=1,
        grid=(E, B // nb),
        in_specs=[pl.BlockSpec((nb, 1, T // GATHER_TILE, GATHER_TILE), lambda e, b, c: (b, e, 0, 0)),
                  pl.BlockSpec((nb, T, D), lambda e, b, c: (b, 0, 0)),
                  pl.BlockSpec((1, D, F), lambda e, b, c: (e, 0, 0)),
                  pl.BlockSpec((1, D, F), lambda e, b, c: (e, 0, 0)),
                  pl.BlockSpec((1, F, D), lambda e, b, c: (e, 0, 0))],
        out_specs=pl.BlockSpec((nb, 1, cap, D), lambda e, b, c: (b, e, 0, 0)),
        scratch_shapes=[pltpu.VMEM((D, 2 * F), bf16), pltpu.VMEM((F, D), bf16),
                        pltpu.VMEM((nb, cap + GATHER_WINDOW, D), f32)])
    return pl.pallas_call(
        functools.partial(_ffn_kernel, cap=cap),
        grid_spec=grid_spec,
        out_shape=jax.ShapeDtypeStruct((B, E, cap, D), bf16),
        compiler_params=_params(("arbitrary", "arbitrary"), FFN_VMEM_LIMIT),
        name="ffn",
    )(cnt, slot4, h2, wg, wu, wd)


def _combine_kernel(cnt_ref, rt_ref, eo_ref, x1_ref, mod_ref, ng_ref, o_ref, y_ref, *, cap):
    b = pl.program_id(0)
    i = pl.program_id(1)
    rt = rt_ref[0].astype(f32)
    tm = rt.shape[0]
    E = eo_ref.shape[1]
    W = min(SCATTER_WINDOW, cap)
    GS = SCATTER_GROUP
    ST = GATHER_TILE
    for st in range(tm // ST):
        t = i * (tm // ST) + st
        r = rt[st * ST:(st + 1) * ST]
        base, fits = [], None
        for e in range(E):
            lo = cnt_ref[b, e, t]
            be = jnp.minimum(lax.shift_left(lax.shift_right_logical(lo, GATHER_ALIGN_LOG2), GATHER_ALIGN_LOG2),
                             cap - W)
            base.append(pl.multiple_of(be, GATHER_ALIGN))
            ok = cnt_ref[b, e, t + 1] - be <= W
            fits = ok if fits is None else fits & ok

        def scatter_matrix(e, first, width, r=r):
            cols = (first + lax.broadcasted_iota(jnp.int32, (ST, width), 1)).astype(f32)
            return jnp.where(r[:, e:e + 1] == cols, r[:, E + e:E + e + 1], 0.0).astype(bf16)

        @pl.when(fits)
        def _():
            y = jnp.zeros((ST, y_ref.shape[1]), f32)
            for g in range(0, E, GS):
                p = jnp.concatenate([scatter_matrix(e, base[e], W) for e in range(g, g + GS)], axis=1)
                rows = jnp.concatenate([eo_ref[0, e, pl.ds(base[e], W), :] for e in range(g, g + GS)], axis=0)
                y = y + _dot(p, rows)
            y_ref[st * ST:(st + 1) * ST, :] = y

        @pl.when(jnp.logical_not(fits))
        def _():
            y = jnp.zeros((ST, y_ref.shape[1]), f32)
            for e in range(E):
                y = y + _dot(scatter_matrix(e, 0, cap), eo_ref[0, e])
            y_ref[st * ST:(st + 1) * ST, :] = y

    y = y_ref[...]
    yn = y * lax.rsqrt(jnp.mean(y * y, axis=-1, keepdims=True) + NORM_EPS) * ng_ref[...]
    o_ref[0] = x1_ref[0] + mod_ref[0] * yn


def _combine(cnt, rt, eo, x1, g2mod, ng3, cap):
    B, T, D = x1.shape
    E = eo.shape[1]
    tm = 512 if T % 512 == 0 else T
    assert E % SCATTER_GROUP == 0 and tm % GATHER_TILE == 0
    grid_spec = pltpu.PrefetchScalarGridSpec(
        num_scalar_prefetch=1,
        grid=(B, T // tm),
        in_specs=[pl.BlockSpec((1, tm, 128), lambda b, i, c: (b, i, 0)),
                  pl.BlockSpec((1, E, cap, D), lambda b, i, c: (b, 0, 0, 0)),
                  pl.BlockSpec((1, tm, D), lambda b, i, c: (b, i, 0)),
                  pl.BlockSpec((1, 1, D), lambda b, i, c: (b, 0, 0)),
                  pl.BlockSpec((1, D), lambda b, i, c: (0, 0))],
        out_specs=pl.BlockSpec((1, tm, D), lambda b, i, c: (b, i, 0)),
        scratch_shapes=[pltpu.VMEM((tm, D), f32)])
    return pl.pallas_call(
        functools.partial(_combine_kernel, cap=cap),
        grid_spec=grid_spec,
        out_shape=jax.ShapeDtypeStruct((B, T, D), f32),
        compiler_params=_params(("arbitrary", "arbitrary")),
        name="combine",
    )(cnt, rt, eo, x1, g2mod, ng3)


def _permute_columns(w):
    sk, sv, rk, rv, wd, ad = 0, 512, 1024, 1536, 2048, 2112
    q0 = 2176
    rq, rg, rr, gd, mg = q0, q0 + 512, q0 + 1024, q0 + 1536, q0 + 1664
    order = [(mg, 2048), (sk, 512), (sv, 512), (rq, 512), (rg, 512), (rk, 512), (rv, 512), (rr, 512),
             (wd, 64), (ad, 64), (gd, 128)]
    parts = [w[:, a:a + n] for a, n in order]
    parts.append(jnp.zeros((w.shape[0], IN_COLS - USED_COLS), w.dtype))
    return jnp.concatenate(parts, axis=1)


def _rope_tables(T, CT):
    t = jnp.arange(T)
    nfreq = RET_HEAD_DIM // 4
    inv = ROPE_BASE ** (-jnp.arange(nfreq, dtype=f32) / nfreq)
    ang = jnp.concatenate([(t // GRID_W).astype(f32)[:, None] * inv,
                           (t % GRID_W).astype(f32)[:, None] * inv], axis=-1)
    cos, sin = jnp.cos(ang), jnp.sin(ang)
    cosf = jnp.concatenate([cos, cos], axis=1)
    sinf = jnp.concatenate([-sin, sin], axis=1)
    return (jnp.concatenate([cosf, jnp.ones((CT, RET_HEAD_DIM), f32)], axis=0),
            jnp.concatenate([sinf, jnp.zeros((CT, RET_HEAD_DIM), f32)], axis=0))


def kernel(x, c, ctx, c_ctx, w_mod, b_mod, norm_g, w_in, ret_log_decay, ret_gn_g, rwkv_mu, rwkv_k_k, rwkv_k_a,
           rwkv_r_k, rwkv_w0, rwkv_w2, rwkv_a0, rwkv_a2, rwkv_g2, rwkv_ln_g, rwkv_ln_b, w_br_ret, w_br_rwkv,
           w_out, w_router, w_gate, w_up, w_down):
    B, T, D = x.shape
    CT = ctx.shape[1]
    assert w_mod.shape[0] == 1 and D == D_MODEL
    assert CT % ROWS == 0 and T % ROWS == 0 and T % GRID_W == 0
    cap = CAPACITY_FACTOR * T // N_EXPERTS
    assert cap % 8 == 0

    mrows = -(-(B + 1) // 8) * 8
    cc = jnp.zeros((mrows, D), f32).at[:B].set(c).at[B].set(c_ctx)
    mod = _modulation(cc, w_mod[0], b_mod[0])
    lat = mod[:B].reshape(B, N_MOD, D)
    cm = jnp.broadcast_to(mod[B].reshape(1, N_MOD, D), (B, N_MOD, D))
    modrows = jnp.concatenate([lat[:, 0:2], cm[:, 0:2]], axis=1)

    w_perm = _permute_columns(w_in[0]).astype(bf16)
    mu = rwkv_mu[0]
    ss = 2 * RWKV_WIDTH + DECAY_LORA + ICLR_LORA
    mu_full = jnp.zeros((2, IN_COLS), f32)
    mu_full = mu_full.at[:, C_RWK:C_RWK + 1024].set(mu[:, 0:1024])
    mu_full = mu_full.at[:, C_RWR:C_RWR + 512].set(mu[:, ss:ss + 512])
    mu_full = mu_full.at[:, C_LORA:C_LORA + 128].set(mu[:, 1024:ss])
    mu_full = mu_full.at[:, C_LORA + 128:USED_COLS].set(mu[:, ss + 512:])
    cosf, sinf = _rope_tables(T, CT)
    z = _inproj(x, ctx, modrows, norm_g[0, 0:1], w_perm, mu_full, cosf, sinf)

    lg = -jnp.exp(ret_log_decay[0].astype(f32))
    ret_o = _retention(lg, z, CT)

    G = 2 * RWKV_HEAD_DIM
    w2p = jnp.zeros((2, G, RWKV_WIDTH), f32).at[:, :DECAY_LORA].set(rwkv_w2[0]).astype(bf16)
    a2p = jnp.zeros((2, G, RWKV_WIDTH), f32).at[:, DECAY_LORA:].set(rwkv_a2[0]).astype(bf16)
    rw_o, bonus = _rwkv(z, rwkv_w0[0][:, None, :], rwkv_a0[0][:, None, :], w2p, a2p,
                        rwkv_k_k[0][None], rwkv_k_a[0][None], rwkv_r_k[0][None], CT)

    mod2 = jnp.stack([lat[:, 2], lat[:, 3], lat[:, 4]], axis=1)
    wr_pad = jnp.zeros((D, 128), f32).at[:, :N_EXPERTS].set(w_router[0])
    wr_hi = wr_pad.astype(bf16)
    wr_lo = (wr_pad - wr_hi.astype(f32)).astype(bf16)
    x1, h2, lt = _merge(x, ret_o, rw_o, bonus, z, mod2, norm_g[0, 1:3], ret_gn_g[0][None], rwkv_ln_g[0][None],
                        rwkv_ln_b[0][None], rwkv_g2[0].astype(bf16), w_br_ret[0].astype(bf16),
                        w_br_rwkv[0].astype(bf16), w_out[0].astype(bf16), wr_hi, wr_lo)

    slot, rt, cnt = _route(lt, cap)
    cnt = cnt[:, :, :T // GATHER_TILE + 1]
    eo = _ffn(cnt, slot.reshape(B, N_EXPERTS, T // GATHER_TILE, GATHER_TILE), h2, w_gate[0], w_up[0], w_down[0], cap)
    return _combine(cnt, rt, eo, x1, lat[:, 5:6], norm_g[0, 3:4], cap)
```

```python
import functools

import jax
import jax.numpy as jnp
from jax import lax
from jax.experimental import pallas as pl
from jax.experimental.pallas import tpu as pltpu

f32 = jnp.float32
bf16 = jnp.bfloat16

D_MODEL = 1024
GRID_W = 64
RET_HEAD_DIM = 128
RET_WIDTH = 512
RET_HEADS = 4
RET_CHUNK = 128
RET_EPS = 1e-5
ROPE_BASE = 10000.0
RWKV_HEAD_DIM = 64
RWKV_WIDTH = 512
RWKV_PAIRS = 4
RWKV_CHUNK = 64
RWKV_CHUNKS_PER_STEP = 4
RWKV_UNITS_PER_GROUP = 16
RWKV_SAMPLES_PER_STEP = 8
DECAY_LORA = 64
ICLR_LORA = 64
GATE_LORA = 128
RWKV_EPS = 64e-5
N_EXPERTS = 16
EXPERT_FF = 1024
CAPACITY_FACTOR = 2
N_MOD = 6
NORM_EPS = 1e-6

C_MERGE, C_RETK, C_RETV, C_RETQ, C_RETG = 0, 2048, 2560, 3072, 3584
C_RWK, C_RWV, C_RWR, C_LORA = 4096, 4608, 5120, 5632
USED_COLS = 5888
TN = 512
IN_COLS = -(-USED_COLS // TN) * TN
ROWS = 256
STEP_ROWS = 768
POST_ROWS = 128
QUERY_ROWS = 1024
MERGE_ROWS = 512

ROUTE_MAX_EXP = 126.0
ROUTE_EXP_STEPS = 8
ROUTE_VAL_STEPS = 26

VMEM_LIMIT = 56 * 1024 * 1024
FFN_SAMPLES_PER_STEP = 2
GATHER_TILE = 256
GATHER_ALIGN_LOG2 = 4
GATHER_ALIGN = 1 << GATHER_ALIGN_LOG2
GATHER_WINDOW = 80
SCATTER_WINDOW = 64
SCATTER_GROUP = 4
FFN_VMEM_LIMIT = 62 * 1024 * 1024


def _dot(a, b):
    return jnp.dot(a, b, preferred_element_type=f32)


def _dot_nt(a, b):
    return lax.dot_general(a, b, (((1,), (1,)), ((), ())), preferred_element_type=f32)


def _dot_tn(a, b):
    return lax.dot_general(a, b, (((0,), (0,)), ((), ())), preferred_element_type=f32)


def _bdot(a, b):
    return _dot(a.astype(bf16), b.astype(bf16))


def _split_dot(x, w):
    hi = x.astype(bf16)
    lo = (x - hi.astype(f32)).astype(bf16)
    return _dot(jnp.concatenate([hi, lo], axis=1), jnp.concatenate([w, w], axis=0))


def _sigmoid(x):
    return 0.5 * jnp.tanh(0.5 * x) + 0.5


def _params(sem, limit=VMEM_LIMIT):
    return pltpu.CompilerParams(dimension_semantics=sem, vmem_limit_bytes=limit)


def _mod_kernel(c_ref, w_ref, b_ref, o_ref):
    c = c_ref[...]
    s = c * jax.nn.sigmoid(c)
    o_ref[...] = _bdot(s, w_ref[...]) + b_ref[...]


def _modulation(cc, w_mod, b_mod):
    m, d = cc.shape
    n = w_mod.shape[1]
    tn = 512
    return pl.pallas_call(
        _mod_kernel,
        grid=(n // tn,),
        in_specs=[pl.BlockSpec((m, d), lambda j: (0, 0)),
                  pl.BlockSpec((d, tn), lambda j: (0, j)),
                  pl.BlockSpec((1, tn), lambda j: (0, j))],
        out_specs=pl.BlockSpec((m, tn), lambda j: (0, j)),
        out_shape=jax.ShapeDtypeStruct((m, n), f32),
        compiler_params=_params(("arbitrary",)),
        name="mod",
    )(cc, w_mod, b_mod.reshape(1, n))


def _inproj_kernel(x_ref, c_ref, mod_ref, g_ref, w_ref, mu_ref, cos_ref, sin_ref, o_ref, h_ref, z_ref):
    n = pl.program_id(1)
    T = x_ref.shape[1]
    L = T + c_ref.shape[1]
    nlat = T // ROWS
    nchunk = L // ROWS
    PAD = 8

    SR = STEP_ROWS if L % STEP_ROWS == 0 else ROWS

    @pl.when(n == 0)
    def _():
        z_ref[0:PAD, :] = jnp.zeros((PAD, TN), f32)
        z_ref[PAD + L:PAD + L + PAD, :] = jnp.zeros((PAD, TN), f32)

        def norm_chunk(src_ref, s0, r0, o):
            xb = src_ref[0, pl.ds(s0, ROWS), :]
            y = xb * lax.rsqrt(jnp.mean(xb * xb, axis=-1, keepdims=True) + NORM_EPS) * g_ref[...]
            sh = mod_ref[0, o:o + 1, :]
            sc = mod_ref[0, o + 1:o + 2, :]
            h_ref[pl.ds(r0, ROWS), :] = (y * (1.0 + sc) + sh).astype(bf16)

        def lat_body(i, carry):
            r0 = pl.multiple_of(i * ROWS, ROWS)
            norm_chunk(x_ref, r0, r0, 0)
            return carry

        lax.fori_loop(0, nlat, lat_body, 0)
        for j in range(nlat, nchunk):
            norm_chunk(c_ref, j * ROWS - T, j * ROWS, 2)

    PR = POST_ROWS
    QR = QUERY_ROWS

    def rope(scale):
        def post(z, r0):
            if scale != 1.0:
                z = z * scale
            cs = cos_ref[pl.ds(r0, z.shape[0]), :]
            sn = sin_ref[pl.ds(r0, z.shape[0]), :]
            parts = []
            for hh in range(TN // RET_HEAD_DIM):
                zh = z[:, hh * RET_HEAD_DIM:(hh + 1) * RET_HEAD_DIM]
                parts.append(zh * cs + pltpu.roll(zh, RET_HEAD_DIM // 2, 1) * sn)
            return jnp.concatenate(parts, axis=1)
        return post

    def shifted(r0, at_boundary):
        win = z_ref[pl.ds(r0, PR + 2 * PAD), :]
        prev = win[PAD - 1:PAD - 1 + PR]
        z = win[PAD:PAD + PR]
        nxt = win[PAD + 1:PAD + 1 + PR]
        if at_boundary:
            rid = r0 + lax.broadcasted_iota(jnp.int32, (PR, 1), 0)
            prev = jnp.where(rid == T, 0.0, prev)
            nxt = jnp.where(rid == T - 1, 0.0, nxt)
        mu0 = mu_ref[0:1, :]
        mu1 = mu_ref[1:2, :]
        return (1.0 - mu0 - mu1) * z + mu0 * prev + mu1 * nxt

    def lora_act(zs):
        lane = lax.broadcasted_iota(jnp.int32, (1, TN), 1)
        return jnp.where(lane < DECAY_LORA, jnp.tanh(zs),
                         jnp.where(lane < DECAY_LORA + ICLR_LORA, zs, jax.nn.sigmoid(zs)))

    def product(r0, rows=SR):
        return _dot(h_ref[pl.ds(r0, rows), :], w_ref[...])

    def direct(post, latent_only=False):
        rows, total = (QR, T) if latent_only and T % QR == 0 else (SR, L)

        def body(i, carry):
            r0 = pl.multiple_of(i * rows, rows)
            o_ref[0, pl.ds(r0, rows), :] = post(product(r0, rows), r0).astype(bf16)
            return carry
        lax.fori_loop(0, total // rows, body, 0)
        if total < L:
            o_ref[0, total:L, :] = jnp.zeros((L - total, TN), bf16)

    def via_buffer(act):
        def mm_body(i, carry):
            r0 = pl.multiple_of(i * SR, SR)
            z_ref[pl.ds(PAD + r0, SR), :] = product(r0)
            return carry
        lax.fori_loop(0, L // SR, mm_body, 0)

        def post(r0, at_boundary):
            o_ref[0, pl.ds(r0, PR), :] = act(shifted(r0, at_boundary)).astype(bf16)

        def post_body(i, carry):
            post(pl.multiple_of(i * PR, PR), False)
            return carry

        seam = T // PR
        lax.fori_loop(0, seam - 1, post_body, 0)
        post((seam - 1) * PR, True)
        post(seam * PR, True)
        lax.fori_loop(seam + 1, L // PR, post_body, 0)

    t = lambda c: c // TN
    pl.when(n < t(C_RETK))(lambda: direct(lambda z, r0: _sigmoid(z), latent_only=True))
    pl.when((n >= t(C_RETK)) & (n < t(C_RETV)))(lambda: direct(rope(RET_HEAD_DIM ** -0.5)))
    pl.when((n >= t(C_RETV)) & (n < t(C_RETQ)))(lambda: direct(lambda z, r0: z))
    pl.when((n >= t(C_RETQ)) & (n < t(C_RETG)))(lambda: direct(rope(1.0), latent_only=True))
    pl.when((n >= t(C_RETG)) & (n < t(C_RWK)))(lambda: direct(lambda z, r0: z * _sigmoid(z), latent_only=True))
    pl.when((n >= t(C_RWK)) & (n < t(C_LORA)))(lambda: via_buffer(lambda zs: zs))
    pl.when(n >= t(C_LORA))(lambda: via_buffer(lora_act))


def _inproj(x, ctx, modrows, g0, w_perm, mu_full, cosf, sinf):
    B, T, D = x.shape
    CT = ctx.shape[1]
    L = T + CT
    nt = IN_COLS // TN
    return pl.pallas_call(
        _inproj_kernel,
        grid=(B, nt),
        in_specs=[pl.BlockSpec((1, T, D), lambda b, n: (b, 0, 0)),
                  pl.BlockSpec((1, CT, D), lambda b, n: (b, 0, 0)),
                  pl.BlockSpec((1, 4, D), lambda b, n: (b, 0, 0)),
                  pl.BlockSpec((1, D), lambda b, n: (0, 0)),
                  pl.BlockSpec((D, TN), lambda b, n: (0, n)),
                  pl.BlockSpec((2, TN), lambda b, n: (0, n)),
                  pl.BlockSpec((L, RET_HEAD_DIM), lambda b, n: (0, 0)),
                  pl.BlockSpec((L, RET_HEAD_DIM), lambda b, n: (0, 0))],
        out_specs=pl.BlockSpec((1, L, TN), lambda b, n: (b, 0, n)),
        out_shape=jax.ShapeDtypeStruct((B, L, IN_COLS), bf16),
        scratch_shapes=[pltpu.VMEM((L, D), bf16), pltpu.VMEM((L + 16, TN), f32)],
        compiler_params=_params(("arbitrary", "arbitrary")),
        name="inproj",
    )(x, ctx, modrows, g0, w_perm, mu_full, cosf, sinf)


def _ret_kernel(lg_ref, q_ref, k_ref, v_ref, o_ref, r_ref, tab_ref, *, ctx):
    L = q_ref.shape[1]
    Cc = RET_CHUNK
    hd = RET_HEAD_DIM
    lat = L - ctx
    nc = ctx // Cc
    nl = lat // Cc
    combos = [(h, d) for h in range(RET_HEADS) for d in (0, 1)]
    INTRA, CROSS, TAIL, DECAY = 0, 1, 2, 3

    @pl.when(pl.program_id(0) == 0)
    def _():
        ii = lax.broadcasted_iota(jnp.int32, (Cc, Cc), 0).astype(f32)
        jj = lax.broadcasted_iota(jnp.int32, (Cc, Cc), 1).astype(f32)
        for idx, (h, d) in enumerate(combos):
            lg = lg_ref[d, h]
            if d == 0:
                diff = ii - jj
                cross = jnp.exp(lg * (ii + 1.0))
                tailw = jnp.exp(lg * (Cc - 1.0 - ii))
            else:
                diff = jj - ii
                cross = jnp.exp(lg * (Cc - ii))
                tailw = jnp.exp(lg * ii)
            tab_ref[idx, INTRA] = jnp.where(diff >= 0, jnp.exp(lg * jnp.maximum(diff, 0.0)), 0.0)
            tab_ref[idx, CROSS] = cross
            tab_ref[idx, TAIL] = tailw
            tab_ref[idx, DECAY] = jnp.exp(jnp.zeros((Cc, Cc), f32) + lg * Cc)

    r_ref[...] = jnp.zeros(r_ref.shape, f32)
    o_ref[...] = jnp.zeros(o_ref.shape, f32)

    def cols(h):
        return slice(h * hd, (h + 1) * hd)

    def update(idx, kc, vc):
        ks = (kc.astype(f32) * tab_ref[idx, TAIL]).astype(bf16)
        r_ref[idx] = r_ref[idx] * tab_ref[idx, DECAY] + _dot_tn(ks, vc)

    def ctx_step(s, carry):
        for idx, (h, d) in enumerate(combos):
            row0 = pl.multiple_of(lat + (s * Cc if d == 0 else (nc - 1 - s) * Cc), Cc)
            update(idx, k_ref[0, pl.ds(row0, Cc), cols(h)], v_ref[0, pl.ds(row0, Cc), cols(h)])
        return carry

    def lat_step(s, carry):
        t0s, qs, ks, vs = [], [], [], []
        for h, d in combos:
            t0 = pl.multiple_of(s * Cc if d == 0 else (nl - 1 - s) * Cc, Cc)
            row0 = t0
            t0s.append(t0)
            qs.append(q_ref[0, pl.ds(row0, Cc), cols(h)])
            ks.append(k_ref[0, pl.ds(row0, Cc), cols(h)])
            vs.append(v_ref[0, pl.ds(row0, Cc), cols(h)])
        n = len(combos)
        sc = [(_dot_nt(qs[i], ks[i]) * tab_ref[i, INTRA]).astype(bf16) for i in range(n)]
        oc = [_dot(qs[i], r_ref[i].astype(bf16)) * tab_ref[i, CROSS] for i in range(n)]
        oi = [_dot(sc[i], vs[i]) for i in range(n)]
        for i, (h, d) in enumerate(combos):
            o_ref[0, pl.ds(t0s[i], Cc), cols(h)] += oi[i] + oc[i]
        for i in range(n):
            update(i, ks[i], vs[i])
        return carry

    lax.fori_loop(0, nc, ctx_step, 0)
    lax.fori_loop(0, nl, lat_step, 0)


def _retention(lg, z, ctx):
    B, L, _ = z.shape
    T = L - ctx
    W = RET_WIDTH
    Cc = RET_CHUNK
    blk = lambda c0: pl.BlockSpec((1, L, W), lambda b: (b, 0, c0 // W))
    nchain = 2 * RET_HEADS
    return pl.pallas_call(
        functools.partial(_ret_kernel, ctx=ctx),
        grid=(B,),
        in_specs=[pl.BlockSpec(memory_space=pltpu.SMEM), blk(C_RETQ), blk(C_RETK), blk(C_RETV)],
        out_specs=pl.BlockSpec((1, T, W), lambda b: (b, 0, 0)),
        out_shape=jax.ShapeDtypeStruct((B, T, W), f32),
        scratch_shapes=[pltpu.VMEM((nchain, Cc, Cc), f32), pltpu.VMEM((nchain, 4, Cc, Cc), f32)],
        compiler_params=_params(("arbitrary",)),
        name="ret",
    )(lg, z, z, z)


def _rwkv_kernel(k_ref, v_ref, r_ref, lo_ref, w0_ref, a0_ref, w2_ref, a2_ref, kk_ref, ka_ref, rk_ref,
                 o_ref, bo_ref, s_ref, *, nc, nb, nsub):
    d = pl.program_id(1)
    s = pl.program_id(2)
    C = RWKV_CHUNK

    @pl.when(s == 0)
    def _():
        s_ref[...] = jnp.zeros(s_ref.shape, f32)

    def sub_chunk(j, carry):
        r0 = pl.multiple_of(jnp.where(d == 1, nsub - 1 - j, j) * C, C)
        rows = lambda ref: ref.at[:, pl.ds(r0, C), :]
        _rwkv_chunk(rows(k_ref), rows(v_ref), rows(r_ref), rows(lo_ref), w0_ref, a0_ref, w2_ref, a2_ref,
                    kk_ref, ka_ref, rk_ref, o_ref.at[:, :, pl.ds(r0, C), :], bo_ref.at[:, :, pl.ds(r0, C), :],
                    s_ref, rev=d == 1, emit=s >= nc, nb=nb)
        return carry

    lax.fori_loop(0, nsub, sub_chunk, 0)


def _rwkv_chunk(k_ref, v_ref, r_ref, lo_ref, w0_ref, a0_ref, w2_ref, a2_ref, kk_ref, ka_ref, rk_ref,
                o_ref, bo_ref, s_ref, *, rev, emit, nb):
    C = RWKV_CHUNK
    G = 2 * RWKV_HEAD_DIM

    ii = lax.broadcasted_iota(jnp.int32, (C, G), 0)
    lane = lax.broadcasted_iota(jnp.int32, (C, G), 1)
    jj = lane & (RWKV_HEAD_DIM - 1)
    head0 = lane < RWKV_HEAD_DIM
    dlt = jnp.where(rev, ii - jj, jj - ii)
    strict = dlt < 0
    incl = dlt <= 0
    eye = (ii == jj).astype(f32)
    gi = lax.broadcasted_iota(jnp.int32, (G, G), 0)
    gj = lax.broadcasted_iota(jnp.int32, (G, G), 1)
    blockdiag = (gi >= RWKV_HEAD_DIM) == (gj >= RWKV_HEAD_DIM)
    ones_bd = blockdiag.astype(bf16)
    tri = incl[:, 0:C].astype(bf16)
    base = (ii >> 1) == (jj >> 1)
    offs = [((ii >> (lv + 1)) == (jj >> (lv + 1))) & ((ii >> lv) != (jj >> lv)) for lv in range(1, 6)]

    def segsum(x):
        return jnp.concatenate(
            [_bdot(x[:, p * G:(p + 1) * G], ones_bd) for p in range(RWKV_PAIRS)], axis=1)

    def stack(x):
        xb = x.astype(bf16)
        zero = jnp.zeros_like(xb)
        return jnp.concatenate([jnp.where(head0, xb, zero), jnp.where(head0, zero, xb)], axis=0)

    W = k_ref.shape[2]
    k_all = k_ref[...].astype(f32).reshape(nb * C, W)
    v_all = v_ref[...].astype(f32).reshape(nb * C, W)
    r_all = r_ref[...].astype(f32).reshape(nb * C, W)
    lo_all = lo_ref[:, :, 0:DECAY_LORA + ICLR_LORA].reshape(nb * C, DECAY_LORA + ICLR_LORA)
    u_all = w0_ref[0] + _dot(lo_all, w2_ref[0])
    softplus = jnp.maximum(-u_all, 0.0) + jnp.log1p(jnp.exp(-jnp.abs(u_all)))
    lw_all = -jnp.exp(-softplus - 0.5)
    a_all = jax.nn.sigmoid(a0_ref[0] + _dot(lo_all, a2_ref[0]))
    kkr = k_all * kk_ref[...]
    kk_all = kkr * lax.rsqrt(segsum(kkr * kkr) + 1e-12)
    kd_all = k_all * (1.0 + (a_all - 1.0) * ka_ref[...])
    be_all = kk_all * a_all
    bonus_all = segsum(r_all * kd_all * rk_ref[...]) * v_all
    pre = []
    for bb in range(nb):
        rows = slice(bb * C, (bb + 1) * C)
        vx, rx, lw, kk, kd, be, bonus = (arr[rows] for arr in (v_all, r_all, lw_all, kk_all, kd_all, be_all,
                                                                bonus_all))
        cum = _split_dot_left(tri, lw)
        tot = jnp.where(rev, cum[0:1, :], cum[C - 1:C, :])
        gneg = jnp.exp(-cum)
        etot = jnp.exp(tot)
        gh = etot * gneg
        pre.append(dict(alb=-kk * jnp.exp(cum - lw), rb=rx * jnp.exp(cum), beb=be * gneg, kb=kd * gneg,
                        beh=be * gh, kh=kd * gh, etot=etot, v=vx, bonus=bonus))

    all_units = [(bb, p) for bb in range(nb) for p in range(RWKV_PAIRS)]
    sl = lambda p: slice(p * G, (p + 1) * G)
    part = lambda un, name: pre[un[0]][name][:, sl(un[1])]
    Om = {}
    for g0 in range(0, len(all_units), RWKV_UNITS_PER_GROUP):
        units = all_units[g0:g0 + RWKV_UNITS_PER_GROUP]
        S = {un: s_ref[un[0] * RWKV_PAIRS + un[1]] for un in units}
        Sb = {un: S[un].astype(bf16) for un in units}
        X = {un: part(un, "alb").astype(bf16) for un in units}
        Rb = {un: part(un, "rb").astype(bf16) for un in units}
        Ybs = {un: stack(part(un, "beb")) for un in units}
        Yks = {un: stack(part(un, "kb")) for un in units}
        Vb = {un: part(un, "v").astype(bf16) for un in units}
        Vs = {un: stack(part(un, "v")) for un in units}

        XR = {un: jnp.concatenate([X[un], Rb[un]], axis=0) for un in units}
        ABO = {un: _dot_nt(XR[un], jnp.concatenate([Ybs[un], Yks[un], Sb[un]], axis=0)) for un in units}
        APb = {un: ABO[un][:, 0:G] for un in units}
        APk = {un: ABO[un][:, G:2 * G] for un in units}
        BO = {un: ABO[un][:, 2 * G:3 * G] for un in units}
        Aab = {un: jnp.where(strict, APb[un][0:C], 0.0) for un in units}
        Aak = {un: jnp.where(strict, APk[un][0:C], 0.0).astype(bf16) for un in units}
        Pab = {un: jnp.where(incl, APb[un][C:2 * C], 0.0).astype(bf16) for un in units}
        Pak = {un: jnp.where(incl, APk[un][C:2 * C], 0.0).astype(bf16) for un in units}

        Tm = {un: eye + jnp.where(base, Aab[un], 0.0) for un in units}
        for off in offs:
            Xs = {un: _dot(jnp.where(off, Aab[un], 0.0).astype(bf16), stack(Tm[un])) for un in units}
            Tm = {un: Tm[un] + _dot(Tm[un].astype(bf16), stack(Xs[un])) for un in units}

        AV = {un: _dot(jnp.concatenate([Aak[un], Pak[un]], axis=0), Vs[un]) for un in units}
        Bm = {un: BO[un][0:C] + AV[un][0:C] for un in units}
        U = {un: _dot(Tm[un].astype(bf16), stack(Bm[un])) for un in units}
        for un in units:
            Om[un] = BO[un][C:2 * C] + AV[un][C:2 * C] + _dot(Pab[un], stack(U[un]))
        for un in units:
            bb, p = un
            upd = _dot_tn(jnp.concatenate([U[un].astype(bf16), Vb[un]], axis=0),
                          jnp.concatenate([part(un, "beh").astype(bf16), part(un, "kh").astype(bf16)], axis=0))
            s_ref[bb * RWKV_PAIRS + p] = jnp.where(blockdiag, S[un] * part(un, "etot") + upd, 0.0)

    @pl.when(emit)
    def _():
        for bb in range(nb):
            o_ref[0, bb] = jnp.concatenate([Om[bb, p] for p in range(RWKV_PAIRS)], axis=1)
            bo_ref[0, bb] = pre[bb]["bonus"]


def _split_dot_left(w, x):
    hi = x.astype(bf16)
    lo = (x - hi.astype(f32)).astype(bf16)
    return _dot(jnp.concatenate([w, w], axis=1), jnp.concatenate([hi, lo], axis=0))


def _rwkv(z, w0, a0, w2p, a2p, k_k, k_a, r_k, ctx):
    B, L, _ = z.shape
    T = L - ctx
    nsub = RWKV_CHUNKS_PER_STEP
    while ctx % (nsub * RWKV_CHUNK) or T % (nsub * RWKV_CHUNK):
        nsub //= 2
    C = nsub * RWKV_CHUNK
    nc, nl = ctx // C, T // C
    W = RWKV_WIDTH

    def chunk(d, s):
        fwd = jnp.where(s < nc, nl + s, s - nc)
        bwd = jnp.where(s < nc, nl + nc - 1 - s, nl - 1 - (s - nc))
        return jnp.where(d == 0, fwd, bwd)

    def ochunk(d, s):
        sl = jnp.maximum(s - nc, 0)
        return jnp.where(d == 0, sl, nl - 1 - sl)

    nb = RWKV_SAMPLES_PER_STEP if B % RWKV_SAMPLES_PER_STEP == 0 else 1
    zblk = lambda c0, w: pl.BlockSpec((nb, C, w), lambda b, d, s: (b, chunk(d, s), c0 // w))
    dpar = lambda r: pl.BlockSpec((1, r, W), lambda b, d, s: (d, 0, 0))
    par = pl.BlockSpec((1, W), lambda b, d, s: (0, 0))
    oblk = pl.BlockSpec((1, nb, C, W), lambda b, d, s: (d, b, ochunk(d, s), 0))
    G = 2 * RWKV_HEAD_DIM
    return pl.pallas_call(
        functools.partial(_rwkv_kernel, nc=nc, nb=nb, nsub=nsub),
        grid=(B // nb, 2, nc + nl),
        in_specs=[zblk(C_RWK, W), zblk(C_RWV, W), zblk(C_RWR, W), zblk(C_LORA, 256),
                  dpar(1), dpar(1), dpar(G), dpar(G), par, par, par],
        out_specs=[oblk, oblk],
        out_shape=[jax.ShapeDtypeStruct((2, B, T, W), f32), jax.ShapeDtypeStruct((2, B, T, W), f32)],
        scratch_shapes=[pltpu.VMEM((nb * RWKV_PAIRS, G, G), f32)],
        compiler_params=_params(("arbitrary", "arbitrary", "arbitrary")),
        name="rwkv",
    )(z, z, z, z, w0, a0, w2p, a2p, k_k, k_a, r_k)


def _merge_kernel(x_ref, ret_ref, rw0_ref, rw1_ref, b0_ref, b1_ref, gate_ref, retg_ref, gd_ref, mod_ref,
                  ng_ref, gn_ref, lng_ref, lnb_ref, g2_ref, wbr_ref, wbw_ref, wout_ref, wrh_ref, wrl_ref,
                  x1_ref, h2_ref, lt_ref):
    hd = RET_HEAD_DIM
    ret = ret_ref[0]
    parts = []
    for hh in range(RET_HEADS):
        xh = ret[:, hh * hd:(hh + 1) * hd]
        mu = jnp.mean(xh, axis=-1, keepdims=True)
        dv = xh - mu
        var = jnp.mean(dv * dv, axis=-1, keepdims=True)
        parts.append(dv * lax.rsqrt(var + RET_EPS))
    yr = retg_ref[0].astype(f32) * (jnp.concatenate(parts, axis=1) * gn_ref[...])
    y_ret = _bdot(yr, wbr_ref[...])
    W = RWKV_WIDTH
    gi = lax.broadcasted_iota(jnp.int32, (W, W), 0)
    gj = lax.broadcasted_iota(jnp.int32, (W, W), 1)
    ones_bd = ((gi >> 6) == (gj >> 6)).astype(bf16)
    o = rw0_ref[0, 0] + rw1_ref[0, 0]
    mu = _split_dot(o, ones_bd) * (1.0 / RWKV_HEAD_DIM)
    dv = o - mu
    var = _bdot(dv * dv, ones_bd) * (1.0 / RWKV_HEAD_DIM)
    yw = dv * lax.rsqrt(var + RWKV_EPS) * lng_ref[...] + lnb_ref[...]
    gate = _dot(gd_ref[0], g2_ref[...])
    yw = (yw + b0_ref[0, 0] + b1_ref[0, 0]) * gate
    y_rw = _bdot(yw, wbw_ref[...])
    D = y_ret.shape[1]
    g = gate_ref[0].astype(f32)
    m = g[:, :D] * y_ret + g[:, D:] * y_rw
    y = _bdot(m, wout_ref[...])

    def rms(v, gg):
        return v * lax.rsqrt(jnp.mean(v * v, axis=-1, keepdims=True) + NORM_EPS) * gg

    x1 = x_ref[0] + mod_ref[0, 0:1, :] * rms(y, ng_ref[0:1, :])
    x1_ref[0] = x1
    h2 = rms(x1, ng_ref[1:2, :]) * (1.0 + mod_ref[0, 2:3, :]) + mod_ref[0, 1:2, :]
    h2b = h2.astype(bf16)
    h2_ref[0] = h2b
    h2l = (h2 - h2b.astype(f32)).astype(bf16)
    lgt = _dot(jnp.concatenate([h2b, h2l, h2b], axis=1),
               jnp.concatenate([wrh_ref[...], wrh_ref[...], wrl_ref[...]], axis=0))
    lt_ref[0] = lgt.T[0:N_EXPERTS, :]


def _merge(x, ret_o, rw_o, bonus, z, mod2, ng12, gn, lng, lnb, g2, wbr, wbw, wout, wrh, wrl):
    B, T, D = x.shape
    tm = MERGE_ROWS if T % MERGE_ROWS == 0 else ROWS
    W = RWKV_WIDTH
    row = lambda w: pl.BlockSpec((1, tm, w), lambda b, i: (b, i, 0))
    dblk = lambda dd: pl.BlockSpec((1, 1, tm, W), lambda b, i: (dd, b, i, 0))
    zblk = lambda c0, w: pl.BlockSpec((1, tm, w), lambda b, i: (b, i, c0 // w))
    full = lambda a: pl.BlockSpec(a.shape, lambda b, i: (0,) * a.ndim)
    return pl.pallas_call(
        _merge_kernel,
        grid=(B, T // tm),
        in_specs=[row(D), row(W), dblk(0), dblk(1), dblk(0), dblk(1),
                  zblk(C_MERGE, 2 * D), zblk(C_RETG, W), zblk(C_LORA + 128, 128),
                  pl.BlockSpec((1, 3, D), lambda b, i: (b, 0, 0)),
                  full(ng12), full(gn), full(lng), full(lnb), full(g2), full(wbr), full(wbw), full(wout),
                  full(wrh), full(wrl)],
        out_specs=[row(D), row(D), pl.BlockSpec((1, N_EXPERTS, tm), lambda b, i: (b, 0, i))],
        out_shape=[jax.ShapeDtypeStruct((B, T, D), f32), jax.ShapeDtypeStruct((B, T, D), bf16),
                   jax.ShapeDtypeStruct((B, N_EXPERTS, T), f32)],
        compiler_params=_params(("arbitrary", "arbitrary")),
        name="merge",
    )(x, ret_o, rw_o, rw_o, bonus, bonus, z, z, z, mod2, ng12, gn, lng, lnb, g2, wbr, wbw, wout, wrh, wrl)


def _route_kernel(lt_ref, slot_ref, rt_ref, cnt_ref, slotf_ref, gate_ref, *, cap):
    B, NE, T = lt_ref.shape
    lg = lt_ref[...]
    mx = jnp.max(lg, axis=1, keepdims=True)
    ex = jnp.exp(lg - mx)
    aff = (ex / jnp.sum(ex, axis=1, keepdims=True)).reshape(B * NE, T)
    E = B * NE

    def count_ge(cand):
        return jnp.sum((aff >= cand).astype(f32), axis=1, keepdims=True)

    def exp_step(_, kk):
        k_lo, k_hi = kk
        km = jnp.floor((k_lo + k_hi) * 0.5)
        ok = count_ge(jnp.exp2(-km)) >= cap
        return jnp.where(ok, k_lo, km), jnp.where(ok, km, k_hi)

    k_lo, k_hi = lax.fori_loop(0, ROUTE_EXP_STEPS, exp_step,
                               (jnp.full((E, 1), -1.0, f32), jnp.full((E, 1), ROUTE_MAX_EXP, f32)))
    lo0 = jnp.where(k_hi >= ROUTE_MAX_EXP, 0.0, jnp.exp2(-k_hi))
    hi0 = jnp.exp2(-k_lo)

    def val_step(_, lh):
        lo, hi = lh
        mid = (lo + hi) * 0.5
        ok = count_ge(mid) >= cap
        return jnp.where(ok, mid, lo), jnp.where(ok, hi, mid)

    lo, hi = lax.fori_loop(0, ROUTE_VAL_STEPS, val_step, (lo0, hi0))
    gt = aff >= hi
    eq = (aff >= lo) & (aff < hi)
    need = cap - jnp.sum(gt.astype(f32), axis=1, keepdims=True)
    tri = (lax.broadcasted_iota(jnp.int32, (T, T), 0) < lax.broadcasted_iota(jnp.int32, (T, T), 1)).astype(bf16)
    eq_before = _dot(eq.astype(bf16), tri)
    sel = gt | (eq & (eq_before < need))
    slot = _dot(sel.astype(bf16), tri)
    slot_f = jnp.where(sel, slot, -1.0)
    slot_ref[...] = slot_f.astype(jnp.int32).reshape(B, NE, T)
    before = (lax.broadcasted_iota(jnp.int32, (T, 128), 0)
              < lax.broadcasted_iota(jnp.int32, (T, 128), 1) * GATHER_TILE).astype(bf16)
    cnt_ref[...] = _dot(sel.astype(bf16), before).astype(jnp.int32).reshape(B, NE, 128)
    slotf_ref[...] = slot_f.reshape(B, NE, T)
    gate_ref[...] = jnp.where(sel, aff, 0.0).reshape(B, NE, T)

    def transpose_sample(b, carry):
        packed = jnp.concatenate([slotf_ref[b], gate_ref[b], jnp.zeros((128 - 2 * NE, T), f32)], axis=0)
        rt_ref[b] = packed.T.astype(bf16)
        return carry

    lax.fori_loop(0, B, transpose_sample, 0)


def _route(lt, cap):
    B, E, T = lt.shape
    assert cap <= 256 and T % GATHER_TILE == 0 and T // GATHER_TILE < 128
    return pl.pallas_call(
        functools.partial(_route_kernel, cap=cap),
        grid=(1,),
        in_specs=[pl.BlockSpec((B, E, T), lambda i: (0, 0, 0))],
        out_specs=[pl.BlockSpec((B, E, T), lambda i: (0, 0, 0)), pl.BlockSpec((B, T, 128), lambda i: (0, 0, 0)),
                   pl.BlockSpec((B, E, 128), lambda i: (0, 0, 0))],
        out_shape=[jax.ShapeDtypeStruct((B, E, T), jnp.int32), jax.ShapeDtypeStruct((B, T, 128), bf16),
                   jax.ShapeDtypeStruct((B, E, 128), jnp.int32)],
        scratch_shapes=[pltpu.VMEM((B, E, T), f32), pltpu.VMEM((B, E, T), f32)],
        compiler_params=_params(("arbitrary",)),
        name="route",
    )(lt)


def _ffn_kernel(cnt_ref, slot_ref, h_ref, wg_ref, wu_ref, wd_ref, o_ref, wgu_ref, wdb_ref, xg_ref, *, cap):
    T = h_ref.shape[1]
    F = wg_ref.shape[2]
    e = pl.program_id(0)

    @pl.when(pl.program_id(1) == 0)
    def _():
        wgu_ref[:, 0:F] = wg_ref[0].astype(bf16)
        wgu_ref[:, F:2 * F] = wu_ref[0].astype(bf16)
        wdb_ref[...] = wd_ref[0].astype(bf16)

    nb = h_ref.shape[0]
    GT, GW, GA = GATHER_TILE, GATHER_WINDOW, GATHER_ALIGN
    nt = T // GT
    for bb in range(nb):
        b = pl.program_id(1) * nb + bb
        base = [pl.multiple_of(lax.shift_left(lax.shift_right_logical(cnt_ref[b, e, j], GATHER_ALIGN_LOG2),
                                              GATHER_ALIGN_LOG2), GA) for j in range(nt)]
        fits = cnt_ref[b, e, 1] - base[0] <= GW
        for j in range(1, nt):
            fits = fits & (cnt_ref[b, e, j + 1] - base[j] <= GW)

        @pl.when(fits)
        def _():
            xg_ref[bb] = jnp.zeros(xg_ref.shape[1:], f32)
            for j in range(nt):
                rows = base[j] + lax.broadcasted_iota(jnp.int32, (GW, GT), 0)
                onehot = (slot_ref[bb, 0, j:j + 1, :] == rows).astype(bf16)
                xg_ref[bb, pl.ds(base[j], GW), :] += _dot(onehot, h_ref[bb, j * GT:(j + 1) * GT, :])

        @pl.when(jnp.logical_not(fits))
        def _():
            rows = lax.broadcasted_iota(jnp.int32, (cap, GT), 0)
            acc = jnp.zeros((cap, h_ref.shape[2]), f32)
            for j in range(nt):
                onehot = (slot_ref[bb, 0, j:j + 1, :] == rows).astype(bf16)
                acc = acc + _dot(onehot, h_ref[bb, j * GT:(j + 1) * GT, :])
            xg_ref[bb, 0:cap, :] = acc

    xg = jnp.concatenate([xg_ref[bb, 0:cap, :].astype(bf16) for bb in range(nb)], axis=0)
    hgu = _dot(xg, wgu_ref[...])
    hg = hgu[:, 0:F]
    hu = hgu[:, F:2 * F]
    hid = (hg * _sigmoid(hg) * hu).astype(bf16)
    out = _dot(hid, wdb_ref[...]).astype(bf16)
    for bb in range(nb):
        o_ref[bb, 0] = out[bb * cap:(bb + 1) * cap]


def _ffn(cnt, slot4, h2, wg, wu, wd, cap):
    B, T, D = h2.shape
    E, _, F = wg.shape
    nb = FFN_SAMPLES_PER_STEP if B % FFN_SAMPLES_PER_STEP == 0 else 1
    grid_spec = pltpu.PrefetchScalarGridSpec(
        num_scalar_prefetch=1,
        grid=(E, B // nb),
        in_specs=[pl.BlockSpec((nb, 1, T // GATHER_TILE, GATHER_TILE), lambda e, b, c: (b, e, 0, 0)),
                  pl.BlockSpec((nb, T, D), lambda e, b, c: (b, 0, 0)),
                  pl.BlockSpec((1, D, F), lambda e, b, c: (e, 0, 0)),
                  pl.BlockSpec((1, D, F), lambda e, b, c: (e, 0, 0)),
                  pl.BlockSpec((1, F, D), lambda e, b, c: (e, 0, 0))],
        out_specs=pl.BlockSpec((nb, 1, cap, D), lambda e, b, c: (b, e, 0, 0)),
        scratch_shapes=[pltpu.VMEM((D, 2 * F), bf16), pltpu.VMEM((F, D), bf16),
                        pltpu.VMEM((nb, cap + GATHER_WINDOW, D), f32)])
    return pl.pallas_call(
        functools.partial(_ffn_kernel, cap=cap),
        grid_spec=grid_spec,
        out_shape=jax.ShapeDtypeStruct((B, E, cap, D), bf16),
        compiler_params=_params(("arbitrary", "arbitrary"), FFN_VMEM_LIMIT),
        name="ffn",
    )(cnt, slot4, h2, wg, wu, wd)


def _combine_kernel(cnt_ref, rt_ref, eo_ref, x1_ref, mod_ref, ng_ref, o_ref, y_ref, *, cap):
    b = pl.program_id(0)
    i = pl.program_id(1)
    rt = rt_ref[0].astype(f32)
    tm = rt.shape[0]
    E = eo_ref.shape[1]
    W = min(SCATTER_WINDOW, cap)
    GS = SCATTER_GROUP
    ST = GATHER_TILE
    for st in range(tm // ST):
        t = i * (tm // ST) + st
        r = rt[st * ST:(st + 1) * ST]
        base, fits = [], None
        for e in range(E):
            lo = cnt_ref[b, e, t]
            be = jnp.minimum(lax.shift_left(lax.shift_right_logical(lo, GATHER_ALIGN_LOG2), GATHER_ALIGN_LOG2),
                             cap - W)
            base.append(pl.multiple_of(be, GATHER_ALIGN))
            ok = cnt_ref[b, e, t + 1] - be <= W
            fits = ok if fits is None else fits & ok

        def scatter_matrix(e, first, width, r=r):
            cols = (first + lax.broadcasted_iota(jnp.int32, (ST, width), 1)).astype(f32)
            return jnp.where(r[:, e:e + 1] == cols, r[:, E + e:E + e + 1], 0.0).astype(bf16)

        @pl.when(fits)
        def _():
            y = jnp.zeros((ST, y_ref.shape[1]), f32)
            for g in range(0, E, GS):
                p = jnp.concatenate([scatter_matrix(e, base[e], W) for e in range(g, g + GS)], axis=1)
                rows = jnp.concatenate([eo_ref[0, e, pl.ds(base[e], W), :] for e in range(g, g + GS)], axis=0)
                y = y + _dot(p, rows)
            y_ref[st * ST:(st + 1) * ST, :] = y

        @pl.when(jnp.logical_not(fits))
        def _():
            y = jnp.zeros((ST, y_ref.shape[1]), f32)
            for e in range(E):
                y = y + _dot(scatter_matrix(e, 0, cap), eo_ref[0, e])
            y_ref[st * ST:(st + 1) * ST, :] = y

    y = y_ref[...]
    yn = y * lax.rsqrt(jnp.mean(y * y, axis=-1, keepdims=True) + NORM_EPS) * ng_ref[...]
    o_ref[0] = x1_ref[0] + mod_ref[0] * yn


def _combine(cnt, rt, eo, x1, g2mod, ng3, cap):
    B, T, D = x1.shape
    E = eo.shape[1]
    tm = 512 if T % 512 == 0 else T
    assert E % SCATTER_GROUP == 0 and tm % GATHER_TILE == 0
    grid_spec = pltpu.PrefetchScalarGridSpec(
        num_scalar_prefetch=1,
        grid=(B, T // tm),
        in_specs=[pl.BlockSpec((1, tm, 128), lambda b, i, c: (b, i, 0)),
                  pl.BlockSpec((1, E, cap, D), lambda b, i, c: (b, 0, 0, 0)),
                  pl.BlockSpec((1, tm, D), lambda b, i, c: (b, i, 0)),
                  pl.BlockSpec((1, 1, D), lambda b, i, c: (b, 0, 0)),
                  pl.BlockSpec((1, D), lambda b, i, c: (0, 0))],
        out_specs=pl.BlockSpec((1, tm, D), lambda b, i, c: (b, i, 0)),
        scratch_shapes=[pltpu.VMEM((tm, D), f32)])
    return pl.pallas_call(
        functools.partial(_combine_kernel, cap=cap),
        grid_spec=grid_spec,
        out_shape=jax.ShapeDtypeStruct((B, T, D), f32),
        compiler_params=_params(("arbitrary", "arbitrary")),
        name="combine",
    )(cnt, rt, eo, x1, g2mod, ng3)


def _permute_columns(w):
    sk, sv, rk, rv, wd, ad = 0, 512, 1024, 1536, 2048, 2112
    q0 = 2176
    rq, rg, rr, gd, mg = q0, q0 + 512, q0 + 1024, q0 + 1536, q0 + 1664
    order = [(mg, 2048), (sk, 512), (sv, 512), (rq, 512), (rg, 512), (rk, 512), (rv, 512), (rr, 512),
             (wd, 64), (ad, 64), (gd, 128)]
    parts = [w[:, a:a + n] for a, n in order]
    parts.append(jnp.zeros((w.shape[0], IN_COLS - USED_COLS), w.dtype))
    return jnp.concatenate(parts, axis=1)


def _rope_tables(T, CT):
    t = jnp.arange(T)
    nfreq = RET_HEAD_DIM // 4
    inv = ROPE_BASE ** (-jnp.arange(nfreq, dtype=f32) / nfreq)
    ang = jnp.concatenate([(t // GRID_W).astype(f32)[:, None] * inv,
                           (t % GRID_W).astype(f32)[:, None] * inv], axis=-1)
    cos, sin = jnp.cos(ang), jnp.sin(ang)
    cosf = jnp.concatenate([cos, cos], axis=1)
    sinf = jnp.concatenate([-sin, sin], axis=1)
    return (jnp.concatenate([cosf, jnp.ones((CT, RET_HEAD_DIM), f32)], axis=0),
            jnp.concatenate([sinf, jnp.zeros((CT, RET_HEAD_DIM), f32)], axis=0))


def kernel(x, c, ctx, c_ctx, w_mod, b_mod, norm_g, w_in, ret_log_decay, ret_gn_g, rwkv_mu, rwkv_k_k, rwkv_k_a,
           rwkv_r_k, rwkv_w0, rwkv_w2, rwkv_a0, rwkv_a2, rwkv_g2, rwkv_ln_g, rwkv_ln_b, w_br_ret, w_br_rwkv,
           w_out, w_router, w_gate, w_up, w_down):
    B, T, D = x.shape
    CT = ctx.shape[1]
    assert w_mod.shape[0] == 1 and D == D_MODEL
    assert CT % ROWS == 0 and T % ROWS == 0 and T % GRID_W == 0
    cap = CAPACITY_FACTOR * T // N_EXPERTS
    assert cap % 8 == 0

    mrows = -(-(B + 1) // 8) * 8
    cc = jnp.zeros((mrows, D), f32).at[:B].set(c).at[B].set(c_ctx)
    mod = _modulation(cc, w_mod[0], b_mod[0])
    lat = mod[:B].reshape(B, N_MOD, D)
    cm = jnp.broadcast_to(mod[B].reshape(1, N_MOD, D), (B, N_MOD, D))
    modrows = jnp.concatenate([lat[:, 0:2], cm[:, 0:2]], axis=1)

    w_perm = _permute_columns(w_in[0]).astype(bf16)
    mu = rwkv_mu[0]
    ss = 2 * RWKV_WIDTH + DECAY_LORA + ICLR_LORA
    mu_full = jnp.zeros((2, IN_COLS), f32)
    mu_full = mu_full.at[:, C_RWK:C_RWK + 1024].set(mu[:, 0:1024])
    mu_full = mu_full.at[:, C_RWR:C_RWR + 512].set(mu[:, ss:ss + 512])
    mu_full = mu_full.at[:, C_LORA:C_LORA + 128].set(mu[:, 1024:ss])
    mu_full = mu_full.at[:, C_LORA + 128:USED_COLS].set(mu[:, ss + 512:])
    cosf, sinf = _rope_tables(T, CT)
    z = _inproj(x, ctx, modrows, norm_g[0, 0:1], w_perm, mu_full, cosf, sinf)

    lg = -jnp.exp(ret_log_decay[0].astype(f32))
    ret_o = _retention(lg, z, CT)

    G = 2 * RWKV_HEAD_DIM
    w2p = jnp.zeros((2, G, RWKV_WIDTH), f32).at[:, :DECAY_LORA].set(rwkv_w2[0]).astype(bf16)
    a2p = jnp.zeros((2, G, RWKV_WIDTH), f32).at[:, DECAY_LORA:].set(rwkv_a2[0]).astype(bf16)
    rw_o, bonus = _rwkv(z, rwkv_w0[0][:, None, :], rwkv_a0[0][:, None, :], w2p, a2p,
                        rwkv_k_k[0][None], rwkv_k_a[0][None], rwkv_r_k[0][None], CT)

    mod2 = jnp.stack([lat[:, 2], lat[:, 3], lat[:, 4]], axis=1)
    wr_pad = jnp.zeros((D, 128), f32).at[:, :N_EXPERTS].set(w_router[0])
    wr_hi = wr_pad.astype(bf16)
    wr_lo = (wr_pad - wr_hi.astype(f32)).astype(bf16)
    x1, h2, lt = _merge(x, ret_o, rw_o, bonus, z, mod2, norm_g[0, 1:3], ret_gn_g[0][None], rwkv_ln_g[0][None],
                        rwkv_ln_b[0][None], rwkv_g2[0].astype(bf16), w_br_ret[0].astype(bf16),
                        w_br_rwkv[0].astype(bf16), w_out[0].astype(bf16), wr_hi, wr_lo)

    slot, rt, cnt = _route(lt, cap)
    cnt = cnt[:, :, :T // GATHER_TILE + 1]
    eo = _ffn(cnt, slot.reshape(B, N_EXPERTS, T // GATHER_TILE, GATHER_TILE), h2, w_gate[0], w_up[0], w_down[0], cap)
    return _combine(cnt, rt, eo, x1, lat[:, 5:6], norm_g[0, 3:4], cap)
```

```python
import functools

import jax
import jax.numpy as jnp
from jax import lax
from jax.experimental import pallas as pl
from jax.experimental.pallas import tpu as pltpu

f32 = jnp.float32
bf16 = jnp.bfloat16

D_MODEL = 1024
GRID_W = 64
RET_HEAD_DIM = 128
RET_WIDTH = 512
RET_HEADS = 4
RET_CHUNK = 128
RET_EPS = 1e-5
ROPE_BASE = 10000.0
RWKV_HEAD_DIM = 64
RWKV_WIDTH = 512
RWKV_PAIRS = 4
RWKV_CHUNK = 64
RWKV_CHUNKS_PER_STEP = 4
RWKV_UNITS_PER_GROUP = 16
RWKV_SAMPLES_PER_STEP = 8
DECAY_LORA = 64
ICLR_LORA = 64
GATE_LORA = 128
RWKV_EPS = 64e-5
N_EXPERTS = 16
EXPERT_FF = 1024
CAPACITY_FACTOR = 2
N_MOD = 6
NORM_EPS = 1e-6

C_MERGE, C_RETK, C_RETV, C_RETQ, C_RETG = 0, 2048, 2560, 3072, 3584
C_RWK, C_RWV, C_RWR, C_LORA = 4096, 4608, 5120, 5632
USED_COLS = 5888
TN = 512
IN_COLS = -(-USED_COLS // TN) * TN
ROWS = 256
STEP_ROWS = 768
POST_ROWS = 128
QUERY_ROWS = 1024
MERGE_ROWS = 512

ROUTE_MAX_EXP = 126.0
ROUTE_EXP_STEPS = 8
ROUTE_VAL_STEPS = 26

VMEM_LIMIT = 56 * 1024 * 1024
FFN_SAMPLES_PER_STEP = 2
GATHER_TILE = 256
GATHER_ALIGN_LOG2 = 4
GATHER_ALIGN = 1 << GATHER_ALIGN_LOG2
GATHER_WINDOW = 80
SCATTER_WINDOW = 64
SCATTER_GROUP = 4
FFN_VMEM_LIMIT = 62 * 1024 * 1024


def _dot(a, b):
    return jnp.dot(a, b, preferred_element_type=f32)


def _dot_nt(a, b):
    return lax.dot_general(a, b, (((1,), (1,)), ((), ())), preferred_element_type=f32)


def _dot_tn(a, b):
    return lax.dot_general(a, b, (((0,), (0,)), ((), ())), preferred_element_type=f32)


def _bdot(a, b):
    return _dot(a.astype(bf16), b.astype(bf16))


def _split_dot(x, w):
    hi = x.astype(bf16)
    lo = (x - hi.astype(f32)).astype(bf16)
    return _dot(jnp.concatenate([hi, lo], axis=1), jnp.concatenate([w, w], axis=0))


def _sigmoid(x):
    return 0.5 * jnp.tanh(0.5 * x) + 0.5


def _params(sem, limit=VMEM_LIMIT):
    return pltpu.CompilerParams(dimension_semantics=sem, vmem_limit_bytes=limit)


def _mod_kernel(c_ref, w_ref, b_ref, o_ref):
    c = c_ref[...]
    s = c * jax.nn.sigmoid(c)
    o_ref[...] = _bdot(s, w_ref[...]) + b_ref[...]


def _modulation(cc, w_mod, b_mod):
    m, d = cc.shape
    n = w_mod.shape[1]
    tn = 512
    return pl.pallas_call(
        _mod_kernel,
        grid=(n // tn,),
        in_specs=[pl.BlockSpec((m, d), lambda j: (0, 0)),
                  pl.BlockSpec((d, tn), lambda j: (0, j)),
                  pl.BlockSpec((1, tn), lambda j: (0, j))],
        out_specs=pl.BlockSpec((m, tn), lambda j: (0, j)),
        out_shape=jax.ShapeDtypeStruct((m, n), f32),
        compiler_params=_params(("arbitrary",)),
        name="mod",
    )(cc, w_mod, b_mod.reshape(1, n))


def _inproj_kernel(x_ref, c_ref, mod_ref, g_ref, w_ref, mu_ref, cos_ref, sin_ref, o_ref, h_ref, z_ref):
    n = pl.program_id(1)
    T = x_ref.shape[1]
    L = T + c_ref.shape[1]
    nlat = T // ROWS
    nchunk = L // ROWS
    PAD = 8

    SR = STEP_ROWS if L % STEP_ROWS == 0 else ROWS

    @pl.when(n == 0)
    def _():
        z_ref[0:PAD, :] = jnp.zeros((PAD, TN), f32)
        z_ref[PAD + L:PAD + L + PAD, :] = jnp.zeros((PAD, TN), f32)

        def norm_chunk(src_ref, s0, r0, o):
            xb = src_ref[0, pl.ds(s0, ROWS), :]
            y = xb * lax.rsqrt(jnp.mean(xb * xb, axis=-1, keepdims=True) + NORM_EPS) * g_ref[...]
            sh = mod_ref[0, o:o + 1, :]
            sc = mod_ref[0, o + 1:o + 2, :]
            h_ref[pl.ds(r0, ROWS), :] = (y * (1.0 + sc) + sh).astype(bf16)

        def lat_body(i, carry):
            r0 = pl.multiple_of(i * ROWS, ROWS)
            norm_chunk(x_ref, r0, r0, 0)
            return carry

        lax.fori_loop(0, nlat, lat_body, 0)
        for j in range(nlat, nchunk):
            norm_chunk(c_ref, j * ROWS - T, j * ROWS, 2)

    PR = POST_ROWS
    QR = QUERY_ROWS

    def rope(scale):
        def post(z, r0):
            if scale != 1.0:
                z = z * scale
            cs = cos_ref[pl.ds(r0, z.shape[0]), :]
            sn = sin_ref[pl.ds(r0, z.shape[0]), :]
            parts = []
            for hh in range(TN // RET_HEAD_DIM):
                zh = z[:, hh * RET_HEAD_DIM:(hh + 1) * RET_HEAD_DIM]
                parts.append(zh * cs + pltpu.roll(zh, RET_HEAD_DIM // 2, 1) * sn)
            return jnp.concatenate(parts, axis=1)
        return post

    def shifted(r0, at_boundary):
        win = z_ref[pl.ds(r0, PR + 2 * PAD), :]
        prev = win[PAD - 1:PAD - 1 + PR]
        z = win[PAD:PAD + PR]
        nxt = win[PAD + 1:PAD + 1 + PR]
        if at_boundary:
            rid = r0 + lax.broadcasted_iota(jnp.int32, (PR, 1), 0)
            prev = jnp.where(rid == T, 0.0, prev)
            nxt = jnp.where(rid == T - 1, 0.0, nxt)
        mu0 = mu_ref[0:1, :]
        mu1 = mu_ref[1:2, :]
        return (1.0 - mu0 - mu1) * z + mu0 * prev + mu1 * nxt

    def lora_act(zs):
        lane = lax.broadcasted_iota(jnp.int32, (1, TN), 1)
        return jnp.where(lane < DECAY_LORA, jnp.tanh(zs),
                         jnp.where(lane < DECAY_LORA + ICLR_LORA, zs, jax.nn.sigmoid(zs)))

    def product(r0, rows=SR):
        return _dot(h_ref[pl.ds(r0, rows), :], w_ref[...])

    def direct(post, latent_only=False):
        rows, total = (QR, T) if latent_only and T % QR == 0 else (SR, L)

        def body(i, carry):
            r0 = pl.multiple_of(i * rows, rows)
            o_ref[0, pl.ds(r0, rows), :] = post(product(r0, rows), r0).astype(bf16)
            return carry
        lax.fori_loop(0, total // rows, body, 0)
        if total < L:
            o_ref[0, total:L, :] = jnp.zeros((L - total, TN), bf16)

    def via_buffer(act):
        def mm_body(i, carry):
            r0 = pl.multiple_of(i * SR, SR)
            z_ref[pl.ds(PAD + r0, SR), :] = product(r0)
            return carry
        lax.fori_loop(0, L // SR, mm_body, 0)

        def post(r0, at_boundary):
            o_ref[0, pl.ds(r0, PR), :] = act(shifted(r0, at_boundary)).astype(bf16)

        def post_body(i, carry):
            post(pl.multiple_of(i * PR, PR), False)
            return carry

        seam = T // PR
        lax.fori_loop(0, seam - 1, post_body, 0)
        post((seam - 1) * PR, True)
        post(seam * PR, True)
        lax.fori_loop(seam + 1, L // PR, post_body, 0)

    t = lambda c: c // TN
    pl.when(n < t(C_RETK))(lambda: direct(lambda z, r0: _sigmoid(z), latent_only=True))
    pl.when((n >= t(C_RETK)) & (n < t(C_RETV)))(lambda: direct(rope(RET_HEAD_DIM ** -0.5)))
    pl.when((n >= t(C_RETV)) & (n < t(C_RETQ)))(lambda: direct(lambda z, r0: z))
    pl.when((n >= t(C_RETQ)) & (n < t(C_RETG)))(lambda: direct(rope(1.0), latent_only=True))
    pl.when((n >= t(C_RETG)) & (n < t(C_RWK)))(lambda: direct(lambda z, r0: z * _sigmoid(z), latent_only=True))
    pl.when((n >= t(C_RWK)) & (n < t(C_LORA)))(lambda: via_buffer(lambda zs: zs))
    pl.when(n >= t(C_LORA))(lambda: via_buffer(lora_act))


def _inproj(x, ctx, modrows, g0, w_perm, mu_full, cosf, sinf):
    B, T, D = x.shape
    CT = ctx.shape[1]
    L = T + CT
    nt = IN_COLS // TN
    return pl.pallas_call(
        _inproj_kernel,
        grid=(B, nt),
        in_specs=[pl.BlockSpec((1, T, D), lambda b, n: (b, 0, 0)),
                  pl.BlockSpec((1, CT, D), lambda b, n: (b, 0, 0)),
                  pl.BlockSpec((1, 4, D), lambda b, n: (b, 0, 0)),
                  pl.BlockSpec((1, D), lambda b, n: (0, 0)),
                  pl.BlockSpec((D, TN), lambda b, n: (0, n)),
                  pl.BlockSpec((2, TN), lambda b, n: (0, n)),
                  pl.BlockSpec((L, RET_HEAD_DIM), lambda b, n: (0, 0)),
                  pl.BlockSpec((L, RET_HEAD_DIM), lambda b, n: (0, 0))],
        out_specs=pl.BlockSpec((1, L, TN), lambda b, n: (b, 0, n)),
        out_shape=jax.ShapeDtypeStruct((B, L, IN_COLS), bf16),
        scratch_shapes=[pltpu.VMEM((L, D), bf16), pltpu.VMEM((L + 16, TN), f32)],
        compiler_params=_params(("arbitrary", "arbitrary")),
        name="inproj",
    )(x, ctx, modrows, g0, w_perm, mu_full, cosf, sinf)


def _ret_kernel(lg_ref, q_ref, k_ref, v_ref, o_ref, r_ref, tab_ref, *, ctx):
    L = q_ref.shape[1]
    Cc = RET_CHUNK
    hd = RET_HEAD_DIM
    lat = L - ctx
    nc = ctx // Cc
    nl = lat // Cc
    combos = [(h, d) for h in range(RET_HEADS) for d in (0, 1)]
    INTRA, CROSS, TAIL, DECAY = 0, 1, 2, 3

    @pl.when(pl.program_id(0) == 0)
    def _():
        ii = lax.broadcasted_iota(jnp.int32, (Cc, Cc), 0).astype(f32)
        jj = lax.broadcasted_iota(jnp.int32, (Cc, Cc), 1).astype(f32)
        for idx, (h, d) in enumerate(combos):
            lg = lg_ref[d, h]
            if d == 0:
                diff = ii - jj
                cross = jnp.exp(lg * (ii + 1.0))
                tailw = jnp.exp(lg * (Cc - 1.0 - ii))
            else:
                diff = jj - ii
                cross = jnp.exp(lg * (Cc - ii))
                tailw = jnp.exp(lg * ii)
            tab_ref[idx, INTRA] = jnp.where(diff >= 0, jnp.exp(lg * jnp.maximum(diff, 0.0)), 0.0)
            tab_ref[idx, CROSS] = cross
            tab_ref[idx, TAIL] = tailw
            tab_ref[idx, DECAY] = jnp.exp(jnp.zeros((Cc, Cc), f32) + lg * Cc)

    r_ref[...] = jnp.zeros(r_ref.shape, f32)
    o_ref[...] = jnp.zeros(o_ref.shape, f32)

    def cols(h):
        return slice(h * hd, (h + 1) * hd)

    def update(idx, kc, vc):
        ks = (kc.astype(f32) * tab_ref[idx, TAIL]).astype(bf16)
        r_ref[idx] = r_ref[idx] * tab_ref[idx, DECAY] + _dot_tn(ks, vc)

    def ctx_step(s, carry):
        for idx, (h, d) in enumerate(combos):
            row0 = pl.multiple_of(lat + (s * Cc if d == 0 else (nc - 1 - s) * Cc), Cc)
            update(idx, k_ref[0, pl.ds(row0, Cc), cols(h)], v_ref[0, pl.ds(row0, Cc), cols(h)])
        return carry

    def lat_step(s, carry):
        t0s, qs, ks, vs = [], [], [], []
        for h, d in combos:
            t0 = pl.multiple_of(s * Cc if d == 0 else (nl - 1 - s) * Cc, Cc)
            row0 = t0
            t0s.append(t0)
            qs.append(q_ref[0, pl.ds(row0, Cc), cols(h)])
            ks.append(k_ref[0, pl.ds(row0, Cc), cols(h)])
            vs.append(v_ref[0, pl.ds(row0, Cc), cols(h)])
        n = len(combos)
        sc = [(_dot_nt(qs[i], ks[i]) * tab_ref[i, INTRA]).astype(bf16) for i in range(n)]
        oc = [_dot(qs[i], r_ref[i].astype(bf16)) * tab_ref[i, CROSS] for i in range(n)]
        oi = [_dot(sc[i], vs[i]) for i in range(n)]
        for i, (h, d) in enumerate(combos):
            o_ref[0, pl.ds(t0s[i], Cc), cols(h)] += oi[i] + oc[i]
        for i in range(n):
            update(i, ks[i], vs[i])
        return carry

    lax.fori_loop(0, nc, ctx_step, 0)
    lax.fori_loop(0, nl, lat_step, 0)


def _retention(lg, z, ctx):
    B, L, _ = z.shape
    T = L - ctx
    W = RET_WIDTH
    Cc = RET_CHUNK
    blk = lambda c0: pl.BlockSpec((1, L, W), lambda b: (b, 0, c0 // W))
    nchain = 2 * RET_HEADS
    return pl.pallas_call(
        functools.partial(_ret_kernel, ctx=ctx),
        grid=(B,),
        in_specs=[pl.BlockSpec(memory_space=pltpu.SMEM), blk(C_RETQ), blk(C_RETK), blk(C_RETV)],
        out_specs=pl.BlockSpec((1, T, W), lambda b: (b, 0, 0)),
        out_shape=jax.ShapeDtypeStruct((B, T, W), f32),
        scratch_shapes=[pltpu.VMEM((nchain, Cc, Cc), f32), pltpu.VMEM((nchain, 4, Cc, Cc), f32)],
        compiler_params=_params(("arbitrary",)),
        name="ret",
    )(lg, z, z, z)


def _rwkv_kernel(k_ref, v_ref, r_ref, lo_ref, w0_ref, a0_ref, w2_ref, a2_ref, kk_ref, ka_ref, rk_ref,
                 o_ref, bo_ref, s_ref, *, nc, nb, nsub):
    d = pl.program_id(1)
    s = pl.program_id(2)
    C = RWKV_CHUNK

    @pl.when(s == 0)
    def _():
        s_ref[...] = jnp.zeros(s_ref.shape, f32)

    def sub_chunk(j, carry):
        r0 = pl.multiple_of(jnp.where(d == 1, nsub - 1 - j, j) * C, C)
        rows = lambda ref: ref.at[:, pl.ds(r0, C), :]
        _rwkv_chunk(rows(k_ref), rows(v_ref), rows(r_ref), rows(lo_ref), w0_ref, a0_ref, w2_ref, a2_ref,
                    kk_ref, ka_ref, rk_ref, o_ref.at[:, :, pl.ds(r0, C), :], bo_ref.at[:, :, pl.ds(r0, C), :],
                    s_ref, rev=d == 1, emit=s >= nc, nb=nb)
        return carry

    lax.fori_loop(0, nsub, sub_chunk, 0)


def _rwkv_chunk(k_ref, v_ref, r_ref, lo_ref, w0_ref, a0_ref, w2_ref, a2_ref, kk_ref, ka_ref, rk_ref,
                o_ref, bo_ref, s_ref, *, rev, emit, nb):
    C = RWKV_CHUNK
    G = 2 * RWKV_HEAD_DIM

    ii = lax.broadcasted_iota(jnp.int32, (C, G), 0)
    lane = lax.broadcasted_iota(jnp.int32, (C, G), 1)
    jj = lane & (RWKV_HEAD_DIM - 1)
    head0 = lane < RWKV_HEAD_DIM
    dlt = jnp.where(rev, ii - jj, jj - ii)
    strict = dlt < 0
    incl = dlt <= 0
    eye = (ii == jj).astype(f32)
    gi = lax.broadcasted_iota(jnp.int32, (G, G), 0)
    gj = lax.broadcasted_iota(jnp.int32, (G, G), 1)
    blockdiag = (gi >= RWKV_HEAD_DIM) == (gj >= RWKV_HEAD_DIM)
    ones_bd = blockdiag.astype(bf16)
    tri = incl[:, 0:C].astype(bf16)
    same4 = (ii >> 2) == (jj >> 2)
    lag = [same4 & (dlt == -k) for k in (1, 2, 3)]
    offs = [((ii >> (lv + 1)) == (jj >> (lv + 1))) & ((ii >> lv) != (jj >> lv)) for lv in range(2, 6)]

    def block4_inverse(a):
        l4 = jnp.where(same4, a, 0.0)
        c1 = jnp.where(rev, pltpu.roll(l4, 1, 1), pltpu.roll(l4, G - 1, 1))
        c2 = jnp.where(rev, pltpu.roll(l4, 2, 1), pltpu.roll(l4, G - 2, 1))
        d1 = jnp.sum(jnp.where(lag[0], l4, 0.0), axis=0, keepdims=True)
        d2 = jnp.sum(jnp.where(lag[1], l4, 0.0), axis=0, keepdims=True)
        d1b = jnp.broadcast_to(d1, (8, G))
        d1n = jnp.where(rev, pltpu.roll(d1b, 1, 1), pltpu.roll(d1b, G - 1, 1))[0:1]
        two = c1 * d1
        three = two + c2 * d2 + c2 * (d1n * d1)
        return eye + l4 + jnp.where(lag[1], two, 0.0) + jnp.where(lag[2], three, 0.0)

    def segsum(x):
        return jnp.concatenate(
            [_bdot(x[:, p * G:(p + 1) * G], ones_bd) for p in range(RWKV_PAIRS)], axis=1)

    def stack(x):
        xb = x.astype(bf16)
        zero = jnp.zeros_like(xb)
        return jnp.concatenate([jnp.where(head0, xb, zero), jnp.where(head0, zero, xb)], axis=0)

    W = k_ref.shape[2]
    k_all = k_ref[...].astype(f32).reshape(nb * C, W)
    v_all = v_ref[...].astype(f32).reshape(nb * C, W)
    r_all = r_ref[...].astype(f32).reshape(nb * C, W)
    lo_all = lo_ref[:, :, 0:DECAY_LORA + ICLR_LORA].reshape(nb * C, DECAY_LORA + ICLR_LORA)
    u_all = w0_ref[0] + _dot(lo_all, w2_ref[0])
    softplus = jnp.maximum(-u_all, 0.0) + jnp.log1p(jnp.exp(-jnp.abs(u_all)))
    lw_all = -jnp.exp(-softplus - 0.5)
    a_all = jax.nn.sigmoid(a0_ref[0] + _dot(lo_all, a2_ref[0]))
    kkr = k_all * kk_ref[...]
    kk_all = kkr * lax.rsqrt(segsum(kkr * kkr) + 1e-12)
    kd_all = k_all * (1.0 + (a_all - 1.0) * ka_ref[...])
    be_all = kk_all * a_all
    bonus_all = segsum(r_all * kd_all * rk_ref[...]) * v_all
    pre = []
    for bb in range(nb):
        rows = slice(bb * C, (bb + 1) * C)
        vx, rx, lw, kk, kd, be, bonus = (arr[rows] for arr in (v_all, r_all, lw_all, kk_all, kd_all, be_all,
                                                                bonus_all))
        cum = _split_dot_left(tri, lw)
        tot = jnp.where(rev, cum[0:1, :], cum[C - 1:C, :])
        gneg = jnp.exp(-cum)
        etot = jnp.exp(tot)
        gh = etot * gneg
        pre.append(dict(alb=-kk * jnp.exp(cum - lw), rb=rx * jnp.exp(cum), beb=be * gneg, kb=kd * gneg,
                        beh=be * gh, kh=kd * gh, etot=etot, v=vx, bonus=bonus))

    all_units = [(bb, p) for bb in range(nb) for p in range(RWKV_PAIRS)]
    sl = lambda p: slice(p * G, (p + 1) * G)
    part = lambda un, name: pre[un[0]][name][:, sl(un[1])]
    Om = {}
    for g0 in range(0, len(all_units), RWKV_UNITS_PER_GROUP):
        units = all_units[g0:g0 + RWKV_UNITS_PER_GROUP]
        S = {un: s_ref[un[0] * RWKV_PAIRS + un[1]] for un in units}
        Sb = {un: S[un].astype(bf16) for un in units}
        X = {un: part(un, "alb").astype(bf16) for un in units}
        Rb = {un: part(un, "rb").astype(bf16) for un in units}
        Ybs = {un: stack(part(un, "beb")) for un in units}
        Yks = {un: stack(part(un, "kb")) for un in units}
        Vb = {un: part(un, "v").astype(bf16) for un in units}
        Vs = {un: stack(part(un, "v")) for un in units}

        XR = {un: jnp.concatenate([X[un], Rb[un]], axis=0) for un in units}
        ABO = {un: _dot_nt(XR[un], jnp.concatenate([Ybs[un], Yks[un], Sb[un]], axis=0)) for un in units}
        APb = {un: ABO[un][:, 0:G] for un in units}
        APk = {un: ABO[un][:, G:2 * G] for un in units}
        BO = {un: ABO[un][:, 2 * G:3 * G] for un in units}
        Aab = {un: jnp.where(strict, APb[un][0:C], 0.0) for un in units}
        Aak = {un: jnp.where(strict, APk[un][0:C], 0.0).astype(bf16) for un in units}
        Pab = {un: jnp.where(incl, APb[un][C:2 * C], 0.0).astype(bf16) for un in units}
        Pak = {un: jnp.where(incl, APk[un][C:2 * C], 0.0).astype(bf16) for un in units}

        Tm = {un: block4_inverse(Aab[un]) for un in units}
        for off in offs:
            Xs = {un: _dot(jnp.where(off, Aab[un], 0.0).astype(bf16), stack(Tm[un])) for un in units}
            Tm = {un: Tm[un] + _dot(Tm[un].astype(bf16), stack(Xs[un])) for un in units}

        AV = {un: _dot(jnp.concatenate([Aak[un], Pak[un]], axis=0), Vs[un]) for un in units}
        Bm = {un: BO[un][0:C] + AV[un][0:C] for un in units}
        U = {un: _dot(Tm[un].astype(bf16), stack(Bm[un])) for un in units}
        for un in units:
            Om[un] = BO[un][C:2 * C] + AV[un][C:2 * C] + _dot(Pab[un], stack(U[un]))
        for un in units:
            bb, p = un
            upd = _dot_tn(jnp.concatenate([U[un].astype(bf16), Vb[un]], axis=0),
                          jnp.concatenate([part(un, "beh").astype(bf16), part(un, "kh").astype(bf16)], axis=0))
            s_ref[bb * RWKV_PAIRS + p] = jnp.where(blockdiag, S[un] * part(un, "etot") + upd, 0.0)

    @pl.when(emit)
    def _():
        for bb in range(nb):
            o_ref[0, bb] = jnp.concatenate([Om[bb, p] for p in range(RWKV_PAIRS)], axis=1)
            bo_ref[0, bb] = pre[bb]["bonus"]


def _split_dot_left(w, x):
    hi = x.astype(bf16)
    lo = (x - hi.astype(f32)).astype(bf16)
    return _dot(jnp.concatenate([w, w], axis=1), jnp.concatenate([hi, lo], axis=0))


def _rwkv(z, w0, a0, w2p, a2p, k_k, k_a, r_k, ctx):
    B, L, _ = z.shape
    T = L - ctx
    nsub = RWKV_CHUNKS_PER_STEP
    while ctx % (nsub * RWKV_CHUNK) or T % (nsub * RWKV_CHUNK):
        nsub //= 2
    C = nsub * RWKV_CHUNK
    nc, nl = ctx // C, T // C
    W = RWKV_WIDTH

    def chunk(d, s):
        fwd = jnp.where(s < nc, nl + s, s - nc)
        bwd = jnp.where(s < nc, nl + nc - 1 - s, nl - 1 - (s - nc))
        return jnp.where(d == 0, fwd, bwd)

    def ochunk(d, s):
        sl = jnp.maximum(s - nc, 0)
        return jnp.where(d == 0, sl, nl - 1 - sl)

    nb = RWKV_SAMPLES_PER_STEP if B % RWKV_SAMPLES_PER_STEP == 0 else 1
    zblk = lambda c0, w: pl.BlockSpec((nb, C, w), lambda b, d, s: (b, chunk(d, s), c0 // w))
    dpar = lambda r: pl.BlockSpec((1, r, W), lambda b, d, s: (d, 0, 0))
    par = pl.BlockSpec((1, W), lambda b, d, s: (0, 0))
    oblk = pl.BlockSpec((1, nb, C, W), lambda b, d, s: (d, b, ochunk(d, s), 0))
    G = 2 * RWKV_HEAD_DIM
    return pl.pallas_call(
        functools.partial(_rwkv_kernel, nc=nc, nb=nb, nsub=nsub),
        grid=(B // nb, 2, nc + nl),
        in_specs=[zblk(C_RWK, W), zblk(C_RWV, W), zblk(C_RWR, W), zblk(C_LORA, 256),
                  dpar(1), dpar(1), dpar(G), dpar(G), par, par, par],
        out_specs=[oblk, oblk],
        out_shape=[jax.ShapeDtypeStruct((2, B, T, W), f32), jax.ShapeDtypeStruct((2, B, T, W), f32)],
        scratch_shapes=[pltpu.VMEM((nb * RWKV_PAIRS, G, G), f32)],
        compiler_params=_params(("arbitrary", "arbitrary", "arbitrary")),
        name="rwkv",
    )(z, z, z, z, w0, a0, w2p, a2p, k_k, k_a, r_k)


def _merge_kernel(x_ref, ret_ref, rw0_ref, rw1_ref, b0_ref, b1_ref, gate_ref, retg_ref, gd_ref, mod_ref,
                  ng_ref, gn_ref, lng_ref, lnb_ref, g2_ref, wbr_ref, wbw_ref, wout_ref, wrh_ref, wrl_ref,
                  x1_ref, h2_ref, lt_ref):
    hd = RET_HEAD_DIM
    ret = ret_ref[0]
    parts = []
    for hh in range(RET_HEADS):
        xh = ret[:, hh * hd:(hh + 1) * hd]
        mu = jnp.mean(xh, axis=-1, keepdims=True)
        dv = xh - mu
        var = jnp.mean(dv * dv, axis=-1, keepdims=True)
        parts.append(dv * lax.rsqrt(var + RET_EPS))
    yr = retg_ref[0].astype(f32) * (jnp.concatenate(parts, axis=1) * gn_ref[...])
    y_ret = _bdot(yr, wbr_ref[...])
    W = RWKV_WIDTH
    gi = lax.broadcasted_iota(jnp.int32, (W, W), 0)
    gj = lax.broadcasted_iota(jnp.int32, (W, W), 1)
    ones_bd = ((gi >> 6) == (gj >> 6)).astype(bf16)
    o = rw0_ref[0, 0] + rw1_ref[0, 0]
    mu = _split_dot(o, ones_bd) * (1.0 / RWKV_HEAD_DIM)
    dv = o - mu
    var = _bdot(dv * dv, ones_bd) * (1.0 / RWKV_HEAD_DIM)
    yw = dv * lax.rsqrt(var + RWKV_EPS) * lng_ref[...] + lnb_ref[...]
    gate = _dot(gd_ref[0], g2_ref[...])
    yw = (yw + b0_ref[0, 0] + b1_ref[0, 0]) * gate
    y_rw = _bdot(yw, wbw_ref[...])
    D = y_ret.shape[1]
    g = gate_ref[0].astype(f32)
    m = g[:, :D] * y_ret + g[:, D:] * y_rw
    y = _bdot(m, wout_ref[...])

    def rms(v, gg):
        return v * lax.rsqrt(jnp.mean(v * v, axis=-1, keepdims=True) + NORM_EPS) * gg

    x1 = x_ref[0] + mod_ref[0, 0:1, :] * rms(y, ng_ref[0:1, :])
    x1_ref[0] = x1
    h2 = rms(x1, ng_ref[1:2, :]) * (1.0 + mod_ref[0, 2:3, :]) + mod_ref[0, 1:2, :]
    h2b = h2.astype(bf16)
    h2_ref[0] = h2b
    h2l = (h2 - h2b.astype(f32)).astype(bf16)
    lgt = _dot(jnp.concatenate([h2b, h2l, h2b], axis=1),
               jnp.concatenate([wrh_ref[...], wrh_ref[...], wrl_ref[...]], axis=0))
    lt_ref[0] = lgt.T[0:N_EXPERTS, :]


def _merge(x, ret_o, rw_o, bonus, z, mod2, ng12, gn, lng, lnb, g2, wbr, wbw, wout, wrh, wrl):
    B, T, D = x.shape
    tm = MERGE_ROWS if T % MERGE_ROWS == 0 else ROWS
    W = RWKV_WIDTH
    row = lambda w: pl.BlockSpec((1, tm, w), lambda b, i: (b, i, 0))
    dblk = lambda dd: pl.BlockSpec((1, 1, tm, W), lambda b, i: (dd, b, i, 0))
    zblk = lambda c0, w: pl.BlockSpec((1, tm, w), lambda b, i: (b, i, c0 // w))
    full = lambda a: pl.BlockSpec(a.shape, lambda b, i: (0,) * a.ndim)
    return pl.pallas_call(
        _merge_kernel,
        grid=(B, T // tm),
        in_specs=[row(D), row(W), dblk(0), dblk(1), dblk(0), dblk(1),
                  zblk(C_MERGE, 2 * D), zblk(C_RETG, W), zblk(C_LORA + 128, 128),
                  pl.BlockSpec((1, 3, D), lambda b, i: (b, 0, 0)),
                  full(ng12), full(gn), full(lng), full(lnb), full(g2), full(wbr), full(wbw), full(wout),
                  full(wrh), full(wrl)],
        out_specs=[row(D), row(D), pl.BlockSpec((1, N_EXPERTS, tm), lambda b, i: (b, 0, i))],
        out_shape=[jax.ShapeDtypeStruct((B, T, D), f32), jax.ShapeDtypeStruct((B, T, D), bf16),
                   jax.ShapeDtypeStruct((B, N_EXPERTS, T), f32)],
        compiler_params=_params(("arbitrary", "arbitrary")),
        name="merge",
    )(x, ret_o, rw_o, rw_o, bonus, bonus, z, z, z, mod2, ng12, gn, lng, lnb, g2, wbr, wbw, wout, wrh, wrl)


def _route_kernel(lt_ref, slot_ref, rt_ref, cnt_ref, slotf_ref, gate_ref, *, cap):
    B, NE, T = lt_ref.shape
    lg = lt_ref[...]
    mx = jnp.max(lg, axis=1, keepdims=True)
    ex = jnp.exp(lg - mx)
    aff = (ex / jnp.sum(ex, axis=1, keepdims=True)).reshape(B * NE, T)
    E = B * NE

    def count_ge(cand):
        return jnp.sum((aff >= cand).astype(f32), axis=1, keepdims=True)

    def exp_step(_, kk):
        k_lo, k_hi = kk
        km = jnp.floor((k_lo + k_hi) * 0.5)
        ok = count_ge(jnp.exp2(-km)) >= cap
        return jnp.where(ok, k_lo, km), jnp.where(ok, km, k_hi)

    k_lo, k_hi = lax.fori_loop(0, ROUTE_EXP_STEPS, exp_step,
                               (jnp.full((E, 1), -1.0, f32), jnp.full((E, 1), ROUTE_MAX_EXP, f32)))
    lo0 = jnp.where(k_hi >= ROUTE_MAX_EXP, 0.0, jnp.exp2(-k_hi))
    hi0 = jnp.exp2(-k_lo)

    def val_step(_, lh):
        lo, hi = lh
        mid = (lo + hi) * 0.5
        ok = count_ge(mid) >= cap
        return jnp.where(ok, mid, lo), jnp.where(ok, hi, mid)

    lo, hi = lax.fori_loop(0, ROUTE_VAL_STEPS, val_step, (lo0, hi0))
    gt = aff >= hi
    eq = (aff >= lo) & (aff < hi)
    need = cap - jnp.sum(gt.astype(f32), axis=1, keepdims=True)
    tri = (lax.broadcasted_iota(jnp.int32, (T, T), 0) < lax.broadcasted_iota(jnp.int32, (T, T), 1)).astype(bf16)
    eq_before = _dot(eq.astype(bf16), tri)
    sel = gt | (eq & (eq_before < need))
    slot = _dot(sel.astype(bf16), tri)
    slot_f = jnp.where(sel, slot, -1.0)
    slot_ref[...] = slot_f.astype(jnp.int32).reshape(B, NE, T)
    before = (lax.broadcasted_iota(jnp.int32, (T, 128), 0)
              < lax.broadcasted_iota(jnp.int32, (T, 128), 1) * GATHER_TILE).astype(bf16)
    cnt_ref[...] = _dot(sel.astype(bf16), before).astype(jnp.int32).reshape(B, NE, 128)
    slotf_ref[...] = slot_f.reshape(B, NE, T)
    gate_ref[...] = jnp.where(sel, aff, 0.0).reshape(B, NE, T)

    def transpose_sample(b, carry):
        packed = jnp.concatenate([slotf_ref[b], gate_ref[b], jnp.zeros((128 - 2 * NE, T), f32)], axis=0)
        rt_ref[b] = packed.T.astype(bf16)
        return carry

    lax.fori_loop(0, B, transpose_sample, 0)


def _route(lt, cap):
    B, E, T = lt.shape
    assert cap <= 256 and T % GATHER_TILE == 0 and T // GATHER_TILE < 128
    return pl.pallas_call(
        functools.partial(_route_kernel, cap=cap),
        grid=(1,),
        in_specs=[pl.BlockSpec((B, E, T), lambda i: (0, 0, 0))],
        out_specs=[pl.BlockSpec((B, E, T), lambda i: (0, 0, 0)), pl.BlockSpec((B, T, 128), lambda i: (0, 0, 0)),
                   pl.BlockSpec((B, E, 128), lambda i: (0, 0, 0))],
        out_shape=[jax.ShapeDtypeStruct((B, E, T), jnp.int32), jax.ShapeDtypeStruct((B, T, 128), bf16),
                   jax.ShapeDtypeStruct((B, E, 128), jnp.int32)],
        scratch_shapes=[pltpu.VMEM((B, E, T), f32), pltpu.VMEM((B, E, T), f32)],
        compiler_params=_params(("arbitrary",)),
        name="route",
    )(lt)


def _ffn_kernel(cnt_ref, slot_ref, h_ref, wg_ref, wu_ref, wd_ref, o_ref, wgu_ref, wdb_ref, xg_ref, *, cap):
    T = h_ref.shape[1]
    F = wg_ref.shape[2]
    e = pl.program_id(0)

    @pl.when(pl.program_id(1) == 0)
    def _():
        wgu_ref[:, 0:F] = wg_ref[0].astype(bf16)
        wgu_ref[:, F:2 * F] = wu_ref[0].astype(bf16)
        wdb_ref[...] = wd_ref[0].astype(bf16)

    nb = h_ref.shape[0]
    GT, GW, GA = GATHER_TILE, GATHER_WINDOW, GATHER_ALIGN
    nt = T // GT
    for bb in range(nb):
        b = pl.program_id(1) * nb + bb
        base = [pl.multiple_of(lax.shift_left(lax.shift_right_logical(cnt_ref[b, e, j], GATHER_ALIGN_LOG2),
                                              GATHER_ALIGN_LOG2), GA) for j in range(nt)]
        fits = cnt_ref[b, e, 1] - base[0] <= GW
        for j in range(1, nt):
            fits = fits & (cnt_ref[b, e, j + 1] - base[j] <= GW)

        @pl.when(fits)
        def _():
            xg_ref[bb] = jnp.zeros(xg_ref.shape[1:], f32)
            for j in range(nt):
                rows = base[j] + lax.broadcasted_iota(jnp.int32, (GW, GT), 0)
                onehot = (slot_ref[bb, 0, j:j + 1, :] == rows).astype(bf16)
                xg_ref[bb, pl.ds(base[j], GW), :] += _dot(onehot, h_ref[bb, j * GT:(j + 1) * GT, :])

        @pl.when(jnp.logical_not(fits))
        def _():
            rows = lax.broadcasted_iota(jnp.int32, (cap, GT), 0)
            acc = jnp.zeros((cap, h_ref.shape[2]), f32)
            for j in range(nt):
                onehot = (slot_ref[bb, 0, j:j + 1, :] == rows).astype(bf16)
                acc = acc + _dot(onehot, h_ref[bb, j * GT:(j + 1) * GT, :])
            xg_ref[bb, 0:cap, :] = acc

    xg = jnp.concatenate([xg_ref[bb, 0:cap, :].astype(bf16) for bb in range(nb)], axis=0)
    hgu = _dot(xg, wgu_ref[...])
    hg = hgu[:, 0:F]
    hu = hgu[:, F:2 * F]
    hid = (hg * _sigmoid(hg) * hu).astype(bf16)
    out = _dot(hid, wdb_ref[...]).astype(bf16)
    for bb in range(nb):
        o_ref[bb, 0] = out[bb * cap:(bb + 1) * cap]


def _ffn(cnt, slot4, h2, wg, wu, wd, cap):
    B, T, D = h2.shape
    E, _, F = wg.shape
    nb = FFN_SAMPLES_PER_STEP if B % FFN_SAMPLES_PER_STEP == 0 else 1
    grid_spec = pltpu.PrefetchScalarGridSpec(
        num_scalar_prefetch=1,
        grid=(E, B // nb),
        in_specs=[pl.BlockSpec((nb, 1, T // GATHER_TILE, GATHER_TILE), lambda e, b, c: (b, e, 0, 0)),
                  pl.BlockSpec((nb, T, D), lambda e, b, c: (b, 0, 0)),
                  pl.BlockSpec((1, D, F), lambda e, b, c: (e, 0, 0)),
                  pl.BlockSpec((1, D, F), lambda e, b, c: (e, 0, 0)),
                  pl.BlockSpec((1, F, D), lambda e, b, c: (e, 0, 0))],
        out_specs=pl.BlockSpec((nb, 1, cap, D), lambda e, b, c: (b, e, 0, 0)),
        scratch_shapes=[pltpu.VMEM((D, 2 * F), bf16), pltpu.VMEM((F, D), bf16),
                        pltpu.VMEM((nb, cap + GATHER_WINDOW, D), f32)])
    return pl.pallas_call(
        functools.partial(_ffn_kernel, cap=cap),
        grid_spec=grid_spec,
        out_shape=jax.ShapeDtypeStruct((B, E, cap, D), bf16),
        compiler_params=_params(("arbitrary", "arbitrary"), FFN_VMEM_LIMIT),
        name="ffn",
    )(cnt, slot4, h2, wg, wu, wd)


def _combine_kernel(cnt_ref, rt_ref, eo_ref, x1_ref, mod_ref, ng_ref, o_ref, y_ref, *, cap):
    b = pl.program_id(0)
    i = pl.program_id(1)
    rt = rt_ref[0].astype(f32)
    tm = rt.shape[0]
    E = eo_ref.shape[1]
    W = min(SCATTER_WINDOW, cap)
    GS = SCATTER_GROUP
    ST = GATHER_TILE
    for st in range(tm // ST):
        t = i * (tm // ST) + st
        r = rt[st * ST:(st + 1) * ST]
        base, fits = [], None
        for e in range(E):
            lo = cnt_ref[b, e, t]
            be = jnp.minimum(lax.shift_left(lax.shift_right_logical(lo, GATHER_ALIGN_LOG2), GATHER_ALIGN_LOG2),
                             cap - W)
            base.append(pl.multiple_of(be, GATHER_ALIGN))
            ok = cnt_ref[b, e, t + 1] - be <= W
            fits = ok if fits is None else fits & ok

        def scatter_matrix(e, first, width, r=r):
            cols = (first + lax.broadcasted_iota(jnp.int32, (ST, width), 1)).astype(f32)
            return jnp.where(r[:, e:e + 1] == cols, r[:, E + e:E + e + 1], 0.0).astype(bf16)

        @pl.when(fits)
        def _():
            y = jnp.zeros((ST, y_ref.shape[1]), f32)
            for g in range(0, E, GS):
                p = jnp.concatenate([scatter_matrix(e, base[e], W) for e in range(g, g + GS)], axis=1)
                rows = jnp.concatenate([eo_ref[0, e, pl.ds(base[e], W), :] for e in range(g, g + GS)], axis=0)
                y = y + _dot(p, rows)
            y_ref[st * ST:(st + 1) * ST, :] = y

        @pl.when(jnp.logical_not(fits))
        def _():
            y = jnp.zeros((ST, y_ref.shape[1]), f32)
            for e in range(E):
                y = y + _dot(scatter_matrix(e, 0, cap), eo_ref[0, e])
            y_ref[st * ST:(st + 1) * ST, :] = y

    y = y_ref[...]
    yn = y * lax.rsqrt(jnp.mean(y * y, axis=-1, keepdims=True) + NORM_EPS) * ng_ref[...]
    o_ref[0] = x1_ref[0] + mod_ref[0] * yn


def _combine(cnt, rt, eo, x1, g2mod, ng3, cap):
    B, T, D = x1.shape
    E = eo.shape[1]
    tm = 512 if T % 512 == 0 else T
    assert E % SCATTER_GROUP == 0 and tm % GATHER_TILE == 0
    grid_spec = pltpu.PrefetchScalarGridSpec(
        num_scalar_prefetch=1,
        grid=(B, T // tm),
        in_specs=[pl.BlockSpec((1, tm, 128), lambda b, i, c: (b, i, 0)),
                  pl.BlockSpec((1, E, cap, D), lambda b, i, c: (b, 0, 0, 0)),
                  pl.BlockSpec((1, tm, D), lambda b, i, c: (b, i, 0)),
                  pl.BlockSpec((1, 1, D), lambda b, i, c: (b, 0, 0)),
                  pl.BlockSpec((1, D), lambda b, i, c: (0, 0))],
        out_specs=pl.BlockSpec((1, tm, D), lambda b, i, c: (b, i, 0)),
        scratch_shapes=[pltpu.VMEM((tm, D), f32)])
    return pl.pallas_call(
        functools.partial(_combine_kernel, cap=cap),
        grid_spec=grid_spec,
        out_shape=jax.ShapeDtypeStruct((B, T, D), f32),
        compiler_params=_params(("arbitrary", "arbitrary")),
        name="combine",
    )(cnt, rt, eo, x1, g2mod, ng3)


def _permute_columns(w):
    sk, sv, rk, rv, wd, ad = 0, 512, 1024, 1536, 2048, 2112
    q0 = 2176
    rq, rg, rr, gd, mg = q0, q0 + 512, q0 + 1024, q0 + 1536, q0 + 1664
    order = [(mg, 2048), (sk, 512), (sv, 512), (rq, 512), (rg, 512), (rk, 512), (rv, 512), (rr, 512),
             (wd, 64), (ad, 64), (gd, 128)]
    parts = [w[:, a:a + n] for a, n in order]
    parts.append(jnp.zeros((w.shape[0], IN_COLS - USED_COLS), w.dtype))
    return jnp.concatenate(parts, axis=1)


def _rope_tables(T, CT):
    t = jnp.arange(T)
    nfreq = RET_HEAD_DIM // 4
    inv = ROPE_BASE ** (-jnp.arange(nfreq, dtype=f32) / nfreq)
    ang = jnp.concatenate([(t // GRID_W).astype(f32)[:, None] * inv,
                           (t % GRID_W).astype(f32)[:, None] * inv], axis=-1)
    cos, sin = jnp.cos(ang), jnp.sin(ang)
    cosf = jnp.concatenate([cos, cos], axis=1)
    sinf = jnp.concatenate([-sin, sin], axis=1)
    return (jnp.concatenate([cosf, jnp.ones((CT, RET_HEAD_DIM), f32)], axis=0),
            jnp.concatenate([sinf, jnp.zeros((CT, RET_HEAD_DIM), f32)], axis=0))


def kernel(x, c, ctx, c_ctx, w_mod, b_mod, norm_g, w_in, ret_log_decay, ret_gn_g, rwkv_mu, rwkv_k_k, rwkv_k_a,
           rwkv_r_k, rwkv_w0, rwkv_w2, rwkv_a0, rwkv_a2, rwkv_g2, rwkv_ln_g, rwkv_ln_b, w_br_ret, w_br_rwkv,
           w_out, w_router, w_gate, w_up, w_down):
    B, T, D = x.shape
    CT = ctx.shape[1]
    assert w_mod.shape[0] == 1 and D == D_MODEL
    assert CT % ROWS == 0 and T % ROWS == 0 and T % GRID_W == 0
    cap = CAPACITY_FACTOR * T // N_EXPERTS
    assert cap % 8 == 0

    mrows = -(-(B + 1) // 8) * 8
    cc = jnp.zeros((mrows, D), f32).at[:B].set(c).at[B].set(c_ctx)
    mod = _modulation(cc, w_mod[0], b_mod[0])
    lat = mod[:B].reshape(B, N_MOD, D)
    cm = jnp.broadcast_to(mod[B].reshape(1, N_MOD, D), (B, N_MOD, D))
    modrows = jnp.concatenate([lat[:, 0:2], cm[:, 0:2]], axis=1)

    w_perm = _permute_columns(w_in[0]).astype(bf16)
    mu = rwkv_mu[0]
    ss = 2 * RWKV_WIDTH + DECAY_LORA + ICLR_LORA
    mu_full = jnp.zeros((2, IN_COLS), f32)
    mu_full = mu_full.at[:, C_RWK:C_RWK + 1024].set(mu[:, 0:1024])
    mu_full = mu_full.at[:, C_RWR:C_RWR + 512].set(mu[:, ss:ss + 512])
    mu_full = mu_full.at[:, C_LORA:C_LORA + 128].set(mu[:, 1024:ss])
    mu_full = mu_full.at[:, C_LORA + 128:USED_COLS].set(mu[:, ss + 512:])
    cosf, sinf = _rope_tables(T, CT)
    z = _inproj(x, ctx, modrows, norm_g[0, 0:1], w_perm, mu_full, cosf, sinf)

    lg = -jnp.exp(ret_log_decay[0].astype(f32))
    ret_o = _retention(lg, z, CT)

    G = 2 * RWKV_HEAD_DIM
    w2p = jnp.zeros((2, G, RWKV_WIDTH), f32).at[:, :DECAY_LORA].set(rwkv_w2[0]).astype(bf16)
    a2p = jnp.zeros((2, G, RWKV_WIDTH), f32).at[:, DECAY_LORA:].set(rwkv_a2[0]).astype(bf16)
    rw_o, bonus = _rwkv(z, rwkv_w0[0][:, None, :], rwkv_a0[0][:, None, :], w2p, a2p,
                        rwkv_k_k[0][None], rwkv_k_a[0][None], rwkv_r_k[0][None], CT)

    mod2 = jnp.stack([lat[:, 2], lat[:, 3], lat[:, 4]], axis=1)
    wr_pad = jnp.zeros((D, 128), f32).at[:, :N_EXPERTS].set(w_router[0])
    wr_hi = wr_pad.astype(bf16)
    wr_lo = (wr_pad - wr_hi.astype(f32)).astype(bf16)
    x1, h2, lt = _merge(x, ret_o, rw_o, bonus, z, mod2, norm_g[0, 1:3], ret_gn_g[0][None], rwkv_ln_g[0][None],
                        rwkv_ln_b[0][None], rwkv_g2[0].astype(bf16), w_br_ret[0].astype(bf16),
                        w_br_rwkv[0].astype(bf16), w_out[0].astype(bf16), wr_hi, wr_lo)

    slot, rt, cnt = _route(lt, cap)
    cnt = cnt[:, :, :T // GATHER_TILE + 1]
    eo = _ffn(cnt, slot.reshape(B, N_EXPERTS, T // GATHER_TILE, GATHER_TILE), h2, w_gate[0], w_up[0], w_down[0], cap)
    return _combine(cnt, rt, eo, x1, lat[:, 5:6], norm_g[0, 3:4], cap)
```

```python
import functools

import jax
import jax.numpy as jnp
from jax import lax
from jax.experimental import pallas as pl
from jax.experimental.pallas import tpu as pltpu

f32 = jnp.float32
bf16 = jnp.bfloat16

D_MODEL = 1024
GRID_W = 64
RET_HEAD_DIM = 128
RET_WIDTH = 512
RET_HEADS = 4
RET_CHUNK = 128
RET_EPS = 1e-5
ROPE_BASE = 10000.0
RWKV_HEAD_DIM = 64
RWKV_WIDTH = 512
RWKV_PAIRS = 4
RWKV_CHUNK = 64
RWKV_CHUNKS_PER_STEP = 4
RWKV_UNITS_PER_GROUP = 16
RWKV_SAMPLES_PER_STEP = 8
DECAY_LORA = 64
ICLR_LORA = 64
GATE_LORA = 128
RWKV_EPS = 64e-5
N_EXPERTS = 16
EXPERT_FF = 1024
CAPACITY_FACTOR = 2
N_MOD = 6
NORM_EPS = 1e-6

C_MERGE, C_RETK, C_RETV, C_RETQ, C_RETG = 0, 2048, 2560, 3072, 3584
C_RWK, C_RWV, C_RWR, C_LORA = 4096, 4608, 5120, 5632
USED_COLS = 5888
TN = 512
IN_COLS = -(-USED_COLS // TN) * TN
ROWS = 256
STEP_ROWS = 768
POST_ROWS = 128
QUERY_ROWS = 1024
MERGE_ROWS = 512

ROUTE_MAX_EXP = 126.0
ROUTE_EXP_STEPS = 8
ROUTE_VAL_STEPS = 26

VMEM_LIMIT = 56 * 1024 * 1024
FFN_SAMPLES_PER_STEP = 2
GATHER_TILE = 256
GATHER_ALIGN_LOG2 = 4
GATHER_ALIGN = 1 << GATHER_ALIGN_LOG2
GATHER_WINDOW = 80
SCATTER_WINDOW = 64
SCATTER_GROUP = 4
FFN_VMEM_LIMIT = 62 * 1024 * 1024


def _dot(a, b):
    return jnp.dot(a, b, preferred_element_type=f32)


def _dot_nt(a, b):
    return lax.dot_general(a, b, (((1,), (1,)), ((), ())), preferred_element_type=f32)


def _dot_tn(a, b):
    return lax.dot_general(a, b, (((0,), (0,)), ((), ())), preferred_element_type=f32)


def _bdot(a, b):
    return _dot(a.astype(bf16), b.astype(bf16))


def _split_dot(x, w):
    hi = x.astype(bf16)
    lo = (x - hi.astype(f32)).astype(bf16)
    return _dot(jnp.concatenate([hi, lo], axis=1), jnp.concatenate([w, w], axis=0))


def _sigmoid(x):
    return 0.5 * jnp.tanh(0.5 * x) + 0.5


def _params(sem, limit=VMEM_LIMIT):
    return pltpu.CompilerParams(dimension_semantics=sem, vmem_limit_bytes=limit)


def _mod_kernel(c_ref, w_ref, b_ref, o_ref):
    c = c_ref[...]
    s = c * jax.nn.sigmoid(c)
    o_ref[...] = _bdot(s, w_ref[...]) + b_ref[...]


def _modulation(cc, w_mod, b_mod):
    m, d = cc.shape
    n = w_mod.shape[1]
    tn = 512
    return pl.pallas_call(
        _mod_kernel,
        grid=(n // tn,),
        in_specs=[pl.BlockSpec((m, d), lambda j: (0, 0)),
                  pl.BlockSpec((d, tn), lambda j: (0, j)),
                  pl.BlockSpec((1, tn), lambda j: (0, j))],
        out_specs=pl.BlockSpec((m, tn), lambda j: (0, j)),
        out_shape=jax.ShapeDtypeStruct((m, n), f32),
        compiler_params=_params(("arbitrary",)),
        name="mod",
    )(cc, w_mod, b_mod.reshape(1, n))


def _inproj_kernel(x_ref, c_ref, mod_ref, g_ref, w_ref, mu_ref, cos_ref, sin_ref, o_ref, h_ref, z_ref):
    n = pl.program_id(1)
    T = x_ref.shape[1]
    L = T + c_ref.shape[1]
    nlat = T // ROWS
    nchunk = L // ROWS
    PAD = 8

    SR = STEP_ROWS if L % STEP_ROWS == 0 else ROWS

    @pl.when(n == 0)
    def _():
        z_ref[0:PAD, :] = jnp.zeros((PAD, TN), f32)
        z_ref[PAD + L:PAD + L + PAD, :] = jnp.zeros((PAD, TN), f32)

        def norm_chunk(src_ref, s0, r0, o):
            xb = src_ref[0, pl.ds(s0, ROWS), :]
            y = xb * lax.rsqrt(jnp.mean(xb * xb, axis=-1, keepdims=True) + NORM_EPS) * g_ref[...]
            sh = mod_ref[0, o:o + 1, :]
            sc = mod_ref[0, o + 1:o + 2, :]
            h_ref[pl.ds(r0, ROWS), :] = (y * (1.0 + sc) + sh).astype(bf16)

        def lat_body(i, carry):
            r0 = pl.multiple_of(i * ROWS, ROWS)
            norm_chunk(x_ref, r0, r0, 0)
            return carry

        lax.fori_loop(0, nlat, lat_body, 0)
        for j in range(nlat, nchunk):
            norm_chunk(c_ref, j * ROWS - T, j * ROWS, 2)

    PR = POST_ROWS
    QR = QUERY_ROWS

    def rope(scale):
        def post(z, r0):
            if scale != 1.0:
                z = z * scale
            cs = cos_ref[pl.ds(r0, z.shape[0]), :]
            sn = sin_ref[pl.ds(r0, z.shape[0]), :]
            parts = []
            for hh in range(TN // RET_HEAD_DIM):
                zh = z[:, hh * RET_HEAD_DIM:(hh + 1) * RET_HEAD_DIM]
                parts.append(zh * cs + pltpu.roll(zh, RET_HEAD_DIM // 2, 1) * sn)
            return jnp.concatenate(parts, axis=1)
        return post

    def shifted(r0, at_boundary):
        win = z_ref[pl.ds(r0, PR + 2 * PAD), :]
        prev = win[PAD - 1:PAD - 1 + PR]
        z = win[PAD:PAD + PR]
        nxt = win[PAD + 1:PAD + 1 + PR]
        if at_boundary:
            rid = r0 + lax.broadcasted_iota(jnp.int32, (PR, 1), 0)
            prev = jnp.where(rid == T, 0.0, prev)
            nxt = jnp.where(rid == T - 1, 0.0, nxt)
        mu0 = mu_ref[0:1, :]
        mu1 = mu_ref[1:2, :]
        return (1.0 - mu0 - mu1) * z + mu0 * prev + mu1 * nxt

    def lora_act(zs):
        lane = lax.broadcasted_iota(jnp.int32, (1, TN), 1)
        return jnp.where(lane < DECAY_LORA, jnp.tanh(zs),
                         jnp.where(lane < DECAY_LORA + ICLR_LORA, zs, jax.nn.sigmoid(zs)))

    def product(r0, rows=SR):
        return _dot(h_ref[pl.ds(r0, rows), :], w_ref[...])

    def direct(post, latent_only=False):
        rows, total = (QR, T) if latent_only and T % QR == 0 else (SR, L)

        def body(i, carry):
            r0 = pl.multiple_of(i * rows, rows)
            o_ref[0, pl.ds(r0, rows), :] = post(product(r0, rows), r0).astype(bf16)
            return carry
        lax.fori_loop(0, total // rows, body, 0)
        if total < L:
            o_ref[0, total:L, :] = jnp.zeros((L - total, TN), bf16)

    def via_buffer(act):
        def mm_body(i, carry):
            r0 = pl.multiple_of(i * SR, SR)
            z_ref[pl.ds(PAD + r0, SR), :] = product(r0)
            return carry
        lax.fori_loop(0, L // SR, mm_body, 0)

        def post(r0, at_boundary):
            o_ref[0, pl.ds(r0, PR), :] = act(shifted(r0, at_boundary)).astype(bf16)

        def post_body(i, carry):
            post(pl.multiple_of(i * PR, PR), False)
            return carry

        seam = T // PR
        lax.fori_loop(0, seam - 1, post_body, 0)
        post((seam - 1) * PR, True)
        post(seam * PR, True)
        lax.fori_loop(seam + 1, L // PR, post_body, 0)

    t = lambda c: c // TN
    pl.when(n < t(C_RETK))(lambda: direct(lambda z, r0: _sigmoid(z), latent_only=True))
    pl.when((n >= t(C_RETK)) & (n < t(C_RETV)))(lambda: direct(rope(RET_HEAD_DIM ** -0.5)))
    pl.when((n >= t(C_RETV)) & (n < t(C_RETQ)))(lambda: direct(lambda z, r0: z))
    pl.when((n >= t(C_RETQ)) & (n < t(C_RETG)))(lambda: direct(rope(1.0), latent_only=True))
    pl.when((n >= t(C_RETG)) & (n < t(C_RWK)))(lambda: direct(lambda z, r0: z * _sigmoid(z), latent_only=True))
    pl.when((n >= t(C_RWK)) & (n < t(C_LORA)))(lambda: via_buffer(lambda zs: zs))
    pl.when(n >= t(C_LORA))(lambda: via_buffer(lora_act))


def _inproj(x, ctx, modrows, g0, w_perm, mu_full, cosf, sinf):
    B, T, D = x.shape
    CT = ctx.shape[1]
    L = T + CT
    nt = IN_COLS // TN
    return pl.pallas_call(
        _inproj_kernel,
        grid=(B, nt),
        in_specs=[pl.BlockSpec((1, T, D), lambda b, n: (b, 0, 0)),
                  pl.BlockSpec((1, CT, D), lambda b, n: (b, 0, 0)),
                  pl.BlockSpec((1, 4, D), lambda b, n: (b, 0, 0)),
                  pl.BlockSpec((1, D), lambda b, n: (0, 0)),
                  pl.BlockSpec((D, TN), lambda b, n: (0, n)),
                  pl.BlockSpec((2, TN), lambda b, n: (0, n)),
                  pl.BlockSpec((L, RET_HEAD_DIM), lambda b, n: (0, 0)),
                  pl.BlockSpec((L, RET_HEAD_DIM), lambda b, n: (0, 0))],
        out_specs=pl.BlockSpec((1, L, TN), lambda b, n: (b, 0, n)),
        out_shape=jax.ShapeDtypeStruct((B, L, IN_COLS), bf16),
        scratch_shapes=[pltpu.VMEM((L, D), bf16), pltpu.VMEM((L + 16, TN), f32)],
        compiler_params=_params(("arbitrary", "arbitrary")),
        name="inproj",
    )(x, ctx, modrows, g0, w_perm, mu_full, cosf, sinf)


def _ret_kernel(lg_ref, q_ref, k_ref, v_ref, o_ref, r_ref, tab_ref, *, ctx):
    L = q_ref.shape[1]
    Cc = RET_CHUNK
    hd = RET_HEAD_DIM
    lat = L - ctx
    nc = ctx // Cc
    nl = lat // Cc
    combos = [(h, d) for h in range(RET_HEADS) for d in (0, 1)]
    INTRA, CROSS, TAIL, DECAY = 0, 1, 2, 3

    @pl.when(pl.program_id(0) == 0)
    def _():
        ii = lax.broadcasted_iota(jnp.int32, (Cc, Cc), 0).astype(f32)
        jj = lax.broadcasted_iota(jnp.int32, (Cc, Cc), 1).astype(f32)
        for idx, (h, d) in enumerate(combos):
            lg = lg_ref[d, h]
            if d == 0:
                diff = ii - jj
                cross = jnp.exp(lg * (ii + 1.0))
                tailw = jnp.exp(lg * (Cc - 1.0 - ii))
            else:
                diff = jj - ii
                cross = jnp.exp(lg * (Cc - ii))
                tailw = jnp.exp(lg * ii)
            tab_ref[idx, INTRA] = jnp.where(diff >= 0, jnp.exp(lg * jnp.maximum(diff, 0.0)), 0.0)
            tab_ref[idx, CROSS] = cross
            tab_ref[idx, TAIL] = tailw
            tab_ref[idx, DECAY] = jnp.exp(jnp.zeros((Cc, Cc), f32) + lg * Cc)

    r_ref[...] = jnp.zeros(r_ref.shape, f32)
    o_ref[...] = jnp.zeros(o_ref.shape, f32)

    def cols(h):
        return slice(h * hd, (h + 1) * hd)

    def update(idx, kc, vc):
        ks = (kc.astype(f32) * tab_ref[idx, TAIL]).astype(bf16)
        r_ref[idx] = r_ref[idx] * tab_ref[idx, DECAY] + _dot_tn(ks, vc)

    def ctx_step(s, carry):
        for idx, (h, d) in enumerate(combos):
            row0 = pl.multiple_of(lat + (s * Cc if d == 0 else (nc - 1 - s) * Cc), Cc)
            update(idx, k_ref[0, pl.ds(row0, Cc), cols(h)], v_ref[0, pl.ds(row0, Cc), cols(h)])
        return carry

    def lat_step(s, carry):
        t0s, qs, ks, vs = [], [], [], []
        for h, d in combos:
            t0 = pl.multiple_of(s * Cc if d == 0 else (nl - 1 - s) * Cc, Cc)
            row0 = t0
            t0s.append(t0)
            qs.append(q_ref[0, pl.ds(row0, Cc), cols(h)])
            ks.append(k_ref[0, pl.ds(row0, Cc), cols(h)])
            vs.append(v_ref[0, pl.ds(row0, Cc), cols(h)])
        n = len(combos)
        sc = [(_dot_nt(qs[i], ks[i]) * tab_ref[i, INTRA]).astype(bf16) for i in range(n)]
        oc = [_dot(qs[i], r_ref[i].astype(bf16)) * tab_ref[i, CROSS] for i in range(n)]
        oi = [_dot(sc[i], vs[i]) for i in range(n)]
        for i, (h, d) in enumerate(combos):
            o_ref[0, pl.ds(t0s[i], Cc), cols(h)] += oi[i] + oc[i]
        for i in range(n):
            update(i, ks[i], vs[i])
        return carry

    lax.fori_loop(0, nc, ctx_step, 0)
    lax.fori_loop(0, nl, lat_step, 0)


def _retention(lg, z, ctx):
    B, L, _ = z.shape
    T = L - ctx
    W = RET_WIDTH
    Cc = RET_CHUNK
    blk = lambda c0: pl.BlockSpec((1, L, W), lambda b: (b, 0, c0 // W))
    nchain = 2 * RET_HEADS
    return pl.pallas_call(
        functools.partial(_ret_kernel, ctx=ctx),
        grid=(B,),
        in_specs=[pl.BlockSpec(memory_space=pltpu.SMEM), blk(C_RETQ), blk(C_RETK), blk(C_RETV)],
        out_specs=pl.BlockSpec((1, T, W), lambda b: (b, 0, 0)),
        out_shape=jax.ShapeDtypeStruct((B, T, W), f32),
        scratch_shapes=[pltpu.VMEM((nchain, Cc, Cc), f32), pltpu.VMEM((nchain, 4, Cc, Cc), f32)],
        compiler_params=_params(("arbitrary",)),
        name="ret",
    )(lg, z, z, z)


def _rwkv_kernel(k_ref, v_ref, r_ref, lo_ref, w0_ref, a0_ref, w2_ref, a2_ref, kk_ref, ka_ref, rk_ref,
                 o_ref, bo_ref, s_ref, *, nc, nb, nsub):
    d = pl.program_id(1)
    s = pl.program_id(2)
    C = RWKV_CHUNK

    @pl.when(s == 0)
    def _():
        s_ref[...] = jnp.zeros(s_ref.shape, f32)

    def sub_chunk(j, carry):
        r0 = pl.multiple_of(jnp.where(d == 1, nsub - 1 - j, j) * C, C)
        rows = lambda ref: ref.at[:, pl.ds(r0, C), :]
        _rwkv_chunk(rows(k_ref), rows(v_ref), rows(r_ref), rows(lo_ref), w0_ref, a0_ref, w2_ref, a2_ref,
                    kk_ref, ka_ref, rk_ref, o_ref.at[:, :, pl.ds(r0, C), :], bo_ref.at[:, :, pl.ds(r0, C), :],
                    s_ref, rev=d == 1, emit=s >= nc, nb=nb)
        return carry

    lax.fori_loop(0, nsub, sub_chunk, 0)


def _rwkv_chunk(k_ref, v_ref, r_ref, lo_ref, w0_ref, a0_ref, w2_ref, a2_ref, kk_ref, ka_ref, rk_ref,
                o_ref, bo_ref, s_ref, *, rev, emit, nb):
    C = RWKV_CHUNK
    G = 2 * RWKV_HEAD_DIM

    ii = lax.broadcasted_iota(jnp.int32, (C, G), 0)
    lane = lax.broadcasted_iota(jnp.int32, (C, G), 1)
    jj = lane & (RWKV_HEAD_DIM - 1)
    head0 = lane < RWKV_HEAD_DIM
    dlt = jnp.where(rev, ii - jj, jj - ii)
    strict = dlt < 0
    incl = dlt <= 0
    eye = (ii == jj).astype(f32)
    gi = lax.broadcasted_iota(jnp.int32, (G, G), 0)
    gj = lax.broadcasted_iota(jnp.int32, (G, G), 1)
    blockdiag = (gi >= RWKV_HEAD_DIM) == (gj >= RWKV_HEAD_DIM)
    ones_bd = blockdiag.astype(bf16)
    tri = incl[:, 0:C].astype(bf16)
    same4 = (ii >> 2) == (jj >> 2)
    lag = [same4 & (dlt == -k) for k in (1, 2, 3)]
    offs = [((ii >> (lv + 1)) == (jj >> (lv + 1))) & ((ii >> lv) != (jj >> lv)) for lv in range(2, 6)]

    def block4_inverse(a):
        l4 = jnp.where(same4, a, 0.0)
        c1 = jnp.where(rev, pltpu.roll(l4, 1, 1), pltpu.roll(l4, G - 1, 1))
        c2 = jnp.where(rev, pltpu.roll(l4, 2, 1), pltpu.roll(l4, G - 2, 1))
        d1 = jnp.sum(jnp.where(lag[0], l4, 0.0), axis=0, keepdims=True)
        d2 = jnp.sum(jnp.where(lag[1], l4, 0.0), axis=0, keepdims=True)
        d1b = jnp.broadcast_to(d1, (8, G))
        d1n = jnp.where(rev, pltpu.roll(d1b, 1, 1), pltpu.roll(d1b, G - 1, 1))[0:1]
        two = c1 * d1
        three = two + c2 * d2 + c2 * (d1n * d1)
        return eye + l4 + jnp.where(lag[1], two, 0.0) + jnp.where(lag[2], three, 0.0)

    def segsum(x):
        return jnp.concatenate(
            [_bdot(x[:, p * G:(p + 1) * G], ones_bd) for p in range(RWKV_PAIRS)], axis=1)

    def stack(x):
        xb = x.astype(bf16)
        zero = jnp.zeros_like(xb)
        return jnp.concatenate([jnp.where(head0, xb, zero), jnp.where(head0, zero, xb)], axis=0)

    W = k_ref.shape[2]
    k_all = k_ref[...].astype(f32).reshape(nb * C, W)
    v_all = v_ref[...].astype(f32).reshape(nb * C, W)
    r_all = r_ref[...].astype(f32).reshape(nb * C, W)
    lo_all = lo_ref[:, :, 0:DECAY_LORA + ICLR_LORA].reshape(nb * C, DECAY_LORA + ICLR_LORA)
    ua = _dot(lo_all, jnp.concatenate([w2_ref[0], a2_ref[0]], axis=1))
    u_all = w0_ref[0] + ua[:, 0:W]
    softplus = jnp.maximum(-u_all, 0.0) + jnp.log1p(jnp.exp(-jnp.abs(u_all)))
    lw_all = -jnp.exp(-softplus - 0.5)
    a_all = jax.nn.sigmoid(a0_ref[0] + ua[:, W:2 * W])
    kkr = k_all * kk_ref[...]
    kd_all = k_all * (1.0 + (a_all - 1.0) * ka_ref[...])
    sums = segsum(jnp.concatenate([kkr * kkr, r_all * kd_all * rk_ref[...]], axis=0))
    kk_all = kkr * lax.rsqrt(sums[0:nb * C] + 1e-12)
    be_all = kk_all * a_all
    bonus_all = sums[nb * C:2 * nb * C] * v_all
    pre = []
    for bb in range(nb):
        rows = slice(bb * C, (bb + 1) * C)
        vx, rx, lw, kk, kd, be, bonus = (arr[rows] for arr in (v_all, r_all, lw_all, kk_all, kd_all, be_all,
                                                                bonus_all))
        cum = _split_dot_left(tri, lw)
        tot = jnp.where(rev, cum[0:1, :], cum[C - 1:C, :])
        gneg = jnp.exp(-cum)
        etot = jnp.exp(tot)
        gh = etot * gneg
        pre.append(dict(alb=-kk * jnp.exp(cum - lw), rb=rx * jnp.exp(cum), beb=be * gneg, kb=kd * gneg,
                        beh=be * gh, kh=kd * gh, etot=etot, v=vx, bonus=bonus))

    all_units = [(bb, p) for bb in range(nb) for p in range(RWKV_PAIRS)]
    sl = lambda p: slice(p * G, (p + 1) * G)
    part = lambda un, name: pre[un[0]][name][:, sl(un[1])]
    Om = {}
    for g0 in range(0, len(all_units), RWKV_UNITS_PER_GROUP):
        units = all_units[g0:g0 + RWKV_UNITS_PER_GROUP]
        S = {un: s_ref[un[0] * RWKV_PAIRS + un[1]] for un in units}
        Sb = {un: S[un].astype(bf16) for un in units}
        X = {un: part(un, "alb").astype(bf16) for un in units}
        Rb = {un: part(un, "rb").astype(bf16) for un in units}
        Ybs = {un: stack(part(un, "beb")) for un in units}
        Yks = {un: stack(part(un, "kb")) for un in units}
        Vb = {un: part(un, "v").astype(bf16) for un in units}
        Vs = {un: stack(part(un, "v")) for un in units}

        XR = {un: jnp.concatenate([X[un], Rb[un]], axis=0) for un in units}
        ABO = {un: _dot_nt(XR[un], jnp.concatenate([Ybs[un], Yks[un], Sb[un]], axis=0)) for un in units}
        APb = {un: ABO[un][:, 0:G] for un in units}
        APk = {un: ABO[un][:, G:2 * G] for un in units}
        BO = {un: ABO[un][:, 2 * G:3 * G] for un in units}
        Aab = {un: jnp.where(strict, APb[un][0:C], 0.0) for un in units}
        Aak = {un: jnp.where(strict, APk[un][0:C], 0.0).astype(bf16) for un in units}
        Pab = {un: jnp.where(incl, APb[un][C:2 * C], 0.0).astype(bf16) for un in units}
        Pak = {un: jnp.where(incl, APk[un][C:2 * C], 0.0).astype(bf16) for un in units}

        Tm = {un: block4_inverse(Aab[un]) for un in units}
        for off in offs:
            Xs = {un: _dot(jnp.where(off, Aab[un], 0.0).astype(bf16), stack(Tm[un])) for un in units}
            Tm = {un: Tm[un] + _dot(Tm[un].astype(bf16), stack(Xs[un])) for un in units}

        AV = {un: _dot(jnp.concatenate([Aak[un], Pak[un]], axis=0), Vs[un]) for un in units}
        Bm = {un: BO[un][0:C] + AV[un][0:C] for un in units}
        U = {un: _dot(Tm[un].astype(bf16), stack(Bm[un])) for un in units}
        for un in units:
            Om[un] = BO[un][C:2 * C] + AV[un][C:2 * C] + _dot(Pab[un], stack(U[un]))
        for un in units:
            bb, p = un
            upd = _dot_tn(jnp.concatenate([U[un].astype(bf16), Vb[un]], axis=0),
                          jnp.concatenate([part(un, "beh").astype(bf16), part(un, "kh").astype(bf16)], axis=0))
            s_ref[bb * RWKV_PAIRS + p] = jnp.where(blockdiag, S[un] * part(un, "etot") + upd, 0.0)

    @pl.when(emit)
    def _():
        for bb in range(nb):
            o_ref[0, bb] = jnp.concatenate([Om[bb, p] for p in range(RWKV_PAIRS)], axis=1)
            bo_ref[0, bb] = pre[bb]["bonus"]


def _split_dot_left(w, x):
    hi = x.astype(bf16)
    lo = (x - hi.astype(f32)).astype(bf16)
    return _dot(jnp.concatenate([w, w], axis=1), jnp.concatenate([hi, lo], axis=0))


def _rwkv(z, w0, a0, w2p, a2p, k_k, k_a, r_k, ctx):
    B, L, _ = z.shape
    T = L - ctx
    nsub = RWKV_CHUNKS_PER_STEP
    while ctx % (nsub * RWKV_CHUNK) or T % (nsub * RWKV_CHUNK):
        nsub //= 2
    C = nsub * RWKV_CHUNK
    nc, nl = ctx // C, T // C
    W = RWKV_WIDTH

    def chunk(d, s):
        fwd = jnp.where(s < nc, nl + s, s - nc)
        bwd = jnp.where(s < nc, nl + nc - 1 - s, nl - 1 - (s - nc))
        return jnp.where(d == 0, fwd, bwd)

    def ochunk(d, s):
        sl = jnp.maximum(s - nc, 0)
        return jnp.where(d == 0, sl, nl - 1 - sl)

    nb = RWKV_SAMPLES_PER_STEP if B % RWKV_SAMPLES_PER_STEP == 0 else 1
    zblk = lambda c0, w: pl.BlockSpec((nb, C, w), lambda b, d, s: (b, chunk(d, s), c0 // w))
    dpar = lambda r: pl.BlockSpec((1, r, W), lambda b, d, s: (d, 0, 0))
    par = pl.BlockSpec((1, W), lambda b, d, s: (0, 0))
    oblk = pl.BlockSpec((1, nb, C, W), lambda b, d, s: (d, b, ochunk(d, s), 0))
    G = 2 * RWKV_HEAD_DIM
    return pl.pallas_call(
        functools.partial(_rwkv_kernel, nc=nc, nb=nb, nsub=nsub),
        grid=(B // nb, 2, nc + nl),
        in_specs=[zblk(C_RWK, W), zblk(C_RWV, W), zblk(C_RWR, W), zblk(C_LORA, 256),
                  dpar(1), dpar(1), dpar(G), dpar(G), par, par, par],
        out_specs=[oblk, oblk],
        out_shape=[jax.ShapeDtypeStruct((2, B, T, W), f32), jax.ShapeDtypeStruct((2, B, T, W), f32)],
        scratch_shapes=[pltpu.VMEM((nb * RWKV_PAIRS, G, G), f32)],
        compiler_params=_params(("arbitrary", "arbitrary", "arbitrary")),
        name="rwkv",
    )(z, z, z, z, w0, a0, w2p, a2p, k_k, k_a, r_k)


def _merge_kernel(x_ref, ret_ref, rw0_ref, rw1_ref, b0_ref, b1_ref, gate_ref, retg_ref, gd_ref, mod_ref,
                  ng_ref, gn_ref, lng_ref, lnb_ref, g2_ref, wbr_ref, wbw_ref, wout_ref, wrh_ref, wrl_ref,
                  x1_ref, h2_ref, lt_ref):
    hd = RET_HEAD_DIM
    ret = ret_ref[0]
    parts = []
    for hh in range(RET_HEADS):
        xh = ret[:, hh * hd:(hh + 1) * hd]
        mu = jnp.mean(xh, axis=-1, keepdims=True)
        dv = xh - mu
        var = jnp.mean(dv * dv, axis=-1, keepdims=True)
        parts.append(dv * lax.rsqrt(var + RET_EPS))
    yr = retg_ref[0].astype(f32) * (jnp.concatenate(parts, axis=1) * gn_ref[...])
    y_ret = _bdot(yr, wbr_ref[...])
    W = RWKV_WIDTH
    gi = lax.broadcasted_iota(jnp.int32, (W, W), 0)
    gj = lax.broadcasted_iota(jnp.int32, (W, W), 1)
    ones_bd = ((gi >> 6) == (gj >> 6)).astype(bf16)
    o = rw0_ref[0, 0] + rw1_ref[0, 0]
    mu = _split_dot(o, ones_bd) * (1.0 / RWKV_HEAD_DIM)
    dv = o - mu
    var = _bdot(dv * dv, ones_bd) * (1.0 / RWKV_HEAD_DIM)
    yw = dv * lax.rsqrt(var + RWKV_EPS) * lng_ref[...] + lnb_ref[...]
    gate = _dot(gd_ref[0], g2_ref[...])
    yw = (yw + b0_ref[0, 0] + b1_ref[0, 0]) * gate
    y_rw = _bdot(yw, wbw_ref[...])
    D = y_ret.shape[1]
    g = gate_ref[0].astype(f32)
    m = g[:, :D] * y_ret + g[:, D:] * y_rw
    y = _bdot(m, wout_ref[...])

    def rms(v, gg):
        return v * lax.rsqrt(jnp.mean(v * v, axis=-1, keepdims=True) + NORM_EPS) * gg

    x1 = x_ref[0] + mod_ref[0, 0:1, :] * rms(y, ng_ref[0:1, :])
    x1_ref[0] = x1
    h2 = rms(x1, ng_ref[1:2, :]) * (1.0 + mod_ref[0, 2:3, :]) + mod_ref[0, 1:2, :]
    h2b = h2.astype(bf16)
    h2_ref[0] = h2b
    h2l = (h2 - h2b.astype(f32)).astype(bf16)
    lgt = _dot(jnp.concatenate([h2b, h2l, h2b], axis=1),
               jnp.concatenate([wrh_ref[...], wrh_ref[...], wrl_ref[...]], axis=0))
    lt_ref[0] = lgt.T[0:N_EXPERTS, :]


def _merge(x, ret_o, rw_o, bonus, z, mod2, ng12, gn, lng, lnb, g2, wbr, wbw, wout, wrh, wrl):
    B, T, D = x.shape
    tm = MERGE_ROWS if T % MERGE_ROWS == 0 else ROWS
    W = RWKV_WIDTH
    row = lambda w: pl.BlockSpec((1, tm, w), lambda b, i: (b, i, 0))
    dblk = lambda dd: pl.BlockSpec((1, 1, tm, W), lambda b, i: (dd, b, i, 0))
    zblk = lambda c0, w: pl.BlockSpec((1, tm, w), lambda b, i: (b, i, c0 // w))
    full = lambda a: pl.BlockSpec(a.shape, lambda b, i: (0,) * a.ndim)
    return pl.pallas_call(
        _merge_kernel,
        grid=(B, T // tm),
        in_specs=[row(D), row(W), dblk(0), dblk(1), dblk(0), dblk(1),
                  zblk(C_MERGE, 2 * D), zblk(C_RETG, W), zblk(C_LORA + 128, 128),
                  pl.BlockSpec((1, 3, D), lambda b, i: (b, 0, 0)),
                  full(ng12), full(gn), full(lng), full(lnb), full(g2), full(wbr), full(wbw), full(wout),
                  full(wrh), full(wrl)],
        out_specs=[row(D), row(D), pl.BlockSpec((1, N_EXPERTS, tm), lambda b, i: (b, 0, i))],
        out_shape=[jax.ShapeDtypeStruct((B, T, D), f32), jax.ShapeDtypeStruct((B, T, D), bf16),
                   jax.ShapeDtypeStruct((B, N_EXPERTS, T), f32)],
        compiler_params=_params(("arbitrary", "arbitrary")),
        name="merge",
    )(x, ret_o, rw_o, rw_o, bonus, bonus, z, z, z, mod2, ng12, gn, lng, lnb, g2, wbr, wbw, wout, wrh, wrl)


def _route_kernel(lt_ref, slot_ref, rt_ref, cnt_ref, slotf_ref, gate_ref, *, cap):
    B, NE, T = lt_ref.shape
    lg = lt_ref[...]
    mx = jnp.max(lg, axis=1, keepdims=True)
    ex = jnp.exp(lg - mx)
    aff = (ex / jnp.sum(ex, axis=1, keepdims=True)).reshape(B * NE, T)
    E = B * NE

    def count_ge(cand):
        return jnp.sum((aff >= cand).astype(f32), axis=1, keepdims=True)

    def exp_step(_, kk):
        k_lo, k_hi = kk
        km = jnp.floor((k_lo + k_hi) * 0.5)
        ok = count_ge(jnp.exp2(-km)) >= cap
        return jnp.where(ok, k_lo, km), jnp.where(ok, km, k_hi)

    k_lo, k_hi = lax.fori_loop(0, ROUTE_EXP_STEPS, exp_step,
                               (jnp.full((E, 1), -1.0, f32), jnp.full((E, 1), ROUTE_MAX_EXP, f32)))
    lo0 = jnp.where(k_hi >= ROUTE_MAX_EXP, 0.0, jnp.exp2(-k_hi))
    hi0 = jnp.exp2(-k_lo)

    def val_step(_, lh):
        lo, hi = lh
        mid = (lo + hi) * 0.5
        ok = count_ge(mid) >= cap
        return jnp.where(ok, mid, lo), jnp.where(ok, hi, mid)

    lo, hi = lax.fori_loop(0, ROUTE_VAL_STEPS, val_step, (lo0, hi0))
    gt = aff >= hi
    eq = (aff >= lo) & (aff < hi)
    need = cap - jnp.sum(gt.astype(f32), axis=1, keepdims=True)
    tri = (lax.broadcasted_iota(jnp.int32, (T, T), 0) < lax.broadcasted_iota(jnp.int32, (T, T), 1)).astype(bf16)
    eq_before = _dot(eq.astype(bf16), tri)
    sel = gt | (eq & (eq_before < need))
    slot = _dot(sel.astype(bf16), tri)
    slot_f = jnp.where(sel, slot, -1.0)
    slot_ref[...] = slot_f.astype(jnp.int32).reshape(B, NE, T)
    before = (lax.broadcasted_iota(jnp.int32, (T, 128), 0)
              < lax.broadcasted_iota(jnp.int32, (T, 128), 1) * GATHER_TILE).astype(bf16)
    cnt_ref[...] = _dot(sel.astype(bf16), before).astype(jnp.int32).reshape(B, NE, 128)
    slotf_ref[...] = slot_f.reshape(B, NE, T)
    gate_ref[...] = jnp.where(sel, aff, 0.0).reshape(B, NE, T)

    def transpose_sample(b, carry):
        packed = jnp.concatenate([slotf_ref[b], gate_ref[b], jnp.zeros((128 - 2 * NE, T), f32)], axis=0)
        rt_ref[b] = packed.T.astype(bf16)
        return carry

    lax.fori_loop(0, B, transpose_sample, 0)


def _route(lt, cap):
    B, E, T = lt.shape
    assert cap <= 256 and T % GATHER_TILE == 0 and T // GATHER_TILE < 128
    return pl.pallas_call(
        functools.partial(_route_kernel, cap=cap),
        grid=(1,),
        in_specs=[pl.BlockSpec((B, E, T), lambda i: (0, 0, 0))],
        out_specs=[pl.BlockSpec((B, E, T), lambda i: (0, 0, 0)), pl.BlockSpec((B, T, 128), lambda i: (0, 0, 0)),
                   pl.BlockSpec((B, E, 128), lambda i: (0, 0, 0))],
        out_shape=[jax.ShapeDtypeStruct((B, E, T), jnp.int32), jax.ShapeDtypeStruct((B, T, 128), bf16),
                   jax.ShapeDtypeStruct((B, E, 128), jnp.int32)],
        scratch_shapes=[pltpu.VMEM((B, E, T), f32), pltpu.VMEM((B, E, T), f32)],
        compiler_params=_params(("arbitrary",)),
        name="route",
    )(lt)


def _ffn_kernel(cnt_ref, slot_ref, h_ref, wg_ref, wu_ref, wd_ref, o_ref, wgu_ref, wdb_ref, xg_ref, *, cap):
    T = h_ref.shape[1]
    F = wg_ref.shape[2]
    e = pl.program_id(0)

    @pl.when(pl.program_id(1) == 0)
    def _():
        wgu_ref[:, 0:F] = wg_ref[0].astype(bf16)
        wgu_ref[:, F:2 * F] = wu_ref[0].astype(bf16)
        wdb_ref[...] = wd_ref[0].astype(bf16)

    nb = h_ref.shape[0]
    GT, GW, GA = GATHER_TILE, GATHER_WINDOW, GATHER_ALIGN
    nt = T // GT
    for bb in range(nb):
        b = pl.program_id(1) * nb + bb
        base = [pl.multiple_of(lax.shift_left(lax.shift_right_logical(cnt_ref[b, e, j], GATHER_ALIGN_LOG2),
                                              GATHER_ALIGN_LOG2), GA) for j in range(nt)]
        fits = cnt_ref[b, e, 1] - base[0] <= GW
        for j in range(1, nt):
            fits = fits & (cnt_ref[b, e, j + 1] - base[j] <= GW)

        @pl.when(fits)
        def _():
            xg_ref[bb] = jnp.zeros(xg_ref.shape[1:], f32)
            for j in range(nt):
                rows = base[j] + lax.broadcasted_iota(jnp.int32, (GW, GT), 0)
                onehot = (slot_ref[bb, 0, j:j + 1, :] == rows).astype(bf16)
                xg_ref[bb, pl.ds(base[j], GW), :] += _dot(onehot, h_ref[bb, j * GT:(j + 1) * GT, :])

        @pl.when(jnp.logical_not(fits))
        def _():
            rows = lax.broadcasted_iota(jnp.int32, (cap, GT), 0)
            acc = jnp.zeros((cap, h_ref.shape[2]), f32)
            for j in range(nt):
                onehot = (slot_ref[bb, 0, j:j + 1, :] == rows).astype(bf16)
                acc = acc + _dot(onehot, h_ref[bb, j * GT:(j + 1) * GT, :])
            xg_ref[bb, 0:cap, :] = acc

    xg = jnp.concatenate([xg_ref[bb, 0:cap, :].astype(bf16) for bb in range(nb)], axis=0)
    hgu = _dot(xg, wgu_ref[...])
    hg = hgu[:, 0:F]
    hu = hgu[:, F:2 * F]
    hid = (hg * _sigmoid(hg) * hu).astype(bf16)
    out = _dot(hid, wdb_ref[...]).astype(bf16)
    for bb in range(nb):
        o_ref[bb, 0] = out[bb * cap:(bb + 1) * cap]


def _ffn(cnt, slot4, h2, wg, wu, wd, cap):
    B, T, D = h2.shape
    E, _, F = wg.shape
    nb = FFN_SAMPLES_PER_STEP if B % FFN_SAMPLES_PER_STEP == 0 else 1
    grid_spec = pltpu.PrefetchScalarGridSpec(
        num_scalar_prefetch=1,
        grid=(E, B // nb),
        in_specs=[pl.BlockSpec((nb, 1, T // GATHER_TILE, GATHER_TILE), lambda e, b, c: (b, e, 0, 0)),
                  pl.BlockSpec((nb, T, D), lambda e, b, c: (b, 0, 0)),
                  pl.BlockSpec((1, D, F), lambda e, b, c: (e, 0, 0)),
                  pl.BlockSpec((1, D, F), lambda e, b, c: (e, 0, 0)),
                  pl.BlockSpec((1, F, D), lambda e, b, c: (e, 0, 0))],
        out_specs=pl.BlockSpec((nb, 1, cap, D), lambda e, b, c: (b, e, 0, 0)),
        scratch_shapes=[pltpu.VMEM((D, 2 * F), bf16), pltpu.VMEM((F, D), bf16),
                        pltpu.VMEM((nb, cap + GATHER_WINDOW, D), f32)])
    return pl.pallas_call(
        functools.partial(_ffn_kernel, cap=cap),
        grid_spec=grid_spec,
        out_shape=jax.ShapeDtypeStruct((B, E, cap, D), bf16),
        compiler_params=_params(("arbitrary", "arbitrary"), FFN_VMEM_LIMIT),
        name="ffn",
    )(cnt, slot4, h2, wg, wu, wd)


def _combine_kernel(cnt_ref, rt_ref, eo_ref, x1_ref, mod_ref, ng_ref, o_ref, y_ref, *, cap):
    b = pl.program_id(0)
    i = pl.program_id(1)
    rt = rt_ref[0].astype(f32)
    tm = rt.shape[0]
    E = eo_ref.shape[1]
    W = min(SCATTER_WINDOW, cap)
    GS = SCATTER_GROUP
    ST = GATHER_TILE
    for st in range(tm // ST):
        t = i * (tm // ST) + st
        r = rt[st * ST:(st + 1) * ST]
        base, fits = [], None
        for e in range(E):
            lo = cnt_ref[b, e, t]
            be = jnp.minimum(lax.shift_left(lax.shift_right_logical(lo, GATHER_ALIGN_LOG2), GATHER_ALIGN_LOG2),
                             cap - W)
            base.append(pl.multiple_of(be, GATHER_ALIGN))
            ok = cnt_ref[b, e, t + 1] - be <= W
            fits = ok if fits is None else fits & ok

        def scatter_matrix(e, first, width, r=r):
            cols = (first + lax.broadcasted_iota(jnp.int32, (ST, width), 1)).astype(f32)
            return jnp.where(r[:, e:e + 1] == cols, r[:, E + e:E + e + 1], 0.0).astype(bf16)

        @pl.when(fits)
        def _():
            y = jnp.zeros((ST, y_ref.shape[1]), f32)
            for g in range(0, E, GS):
                p = jnp.concatenate([scatter_matrix(e, base[e], W) for e in range(g, g + GS)], axis=1)
                rows = jnp.concatenate([eo_ref[0, e, pl.ds(base[e], W), :] for e in range(g, g + GS)], axis=0)
                y = y + _dot(p, rows)
            y_ref[st * ST:(st + 1) * ST, :] = y

        @pl.when(jnp.logical_not(fits))
        def _():
            y = jnp.zeros((ST, y_ref.shape[1]), f32)
            for e in range(E):
                y = y + _dot(scatter_matrix(e, 0, cap), eo_ref[0, e])
            y_ref[st * ST:(st + 1) * ST, :] = y

    y = y_ref[...]
    yn = y * lax.rsqrt(jnp.mean(y * y, axis=-1, keepdims=True) + NORM_EPS) * ng_ref[...]
    o_ref[0] = x1_ref[0] + mod_ref[0] * yn


def _combine(cnt, rt, eo, x1, g2mod, ng3, cap):
    B, T, D = x1.shape
    E = eo.shape[1]
    tm = 512 if T % 512 == 0 else T
    assert E % SCATTER_GROUP == 0 and tm % GATHER_TILE == 0
    grid_spec = pltpu.PrefetchScalarGridSpec(
        num_scalar_prefetch=1,
        grid=(B, T // tm),
        in_specs=[pl.BlockSpec((1, tm, 128), lambda b, i, c: (b, i, 0)),
                  pl.BlockSpec((1, E, cap, D), lambda b, i, c: (b, 0, 0, 0)),
                  pl.BlockSpec((1, tm, D), lambda b, i, c: (b, i, 0)),
                  pl.BlockSpec((1, 1, D), lambda b, i, c: (b, 0, 0)),
                  pl.BlockSpec((1, D), lambda b, i, c: (0, 0))],
        out_specs=pl.BlockSpec((1, tm, D), lambda b, i, c: (b, i, 0)),
        scratch_shapes=[pltpu.VMEM((tm, D), f32)])
    return pl.pallas_call(
        functools.partial(_combine_kernel, cap=cap),
        grid_spec=grid_spec,
        out_shape=jax.ShapeDtypeStruct((B, T, D), f32),
        compiler_params=_params(("arbitrary", "arbitrary")),
        name="combine",
    )(cnt, rt, eo, x1, g2mod, ng3)


def _permute_columns(w):
    sk, sv, rk, rv, wd, ad = 0, 512, 1024, 1536, 2048, 2112
    q0 = 2176
    rq, rg, rr, gd, mg = q0, q0 + 512, q0 + 1024, q0 + 1536, q0 + 1664
    order = [(mg, 2048), (sk, 512), (sv, 512), (rq, 512), (rg, 512), (rk, 512), (rv, 512), (rr, 512),
             (wd, 64), (ad, 64), (gd, 128)]
    parts = [w[:, a:a + n] for a, n in order]
    parts.append(jnp.zeros((w.shape[0], IN_COLS - USED_COLS), w.dtype))
    return jnp.concatenate(parts, axis=1)


def _rope_tables(T, CT):
    t = jnp.arange(T)
    nfreq = RET_HEAD_DIM // 4
    inv = ROPE_BASE ** (-jnp.arange(nfreq, dtype=f32) / nfreq)
    ang = jnp.concatenate([(t // GRID_W).astype(f32)[:, None] * inv,
                           (t % GRID_W).astype(f32)[:, None] * inv], axis=-1)
    cos, sin = jnp.cos(ang), jnp.sin(ang)
    cosf = jnp.concatenate([cos, cos], axis=1)
    sinf = jnp.concatenate([-sin, sin], axis=1)
    return (jnp.concatenate([cosf, jnp.ones((CT, RET_HEAD_DIM), f32)], axis=0),
            jnp.concatenate([sinf, jnp.zeros((CT, RET_HEAD_DIM), f32)], axis=0))


def kernel(x, c, ctx, c_ctx, w_mod, b_mod, norm_g, w_in, ret_log_decay, ret_gn_g, rwkv_mu, rwkv_k_k, rwkv_k_a,
           rwkv_r_k, rwkv_w0, rwkv_w2, rwkv_a0, rwkv_a2, rwkv_g2, rwkv_ln_g, rwkv_ln_b, w_br_ret, w_br_rwkv,
           w_out, w_router, w_gate, w_up, w_down):
    B, T, D = x.shape
    CT = ctx.shape[1]
    assert w_mod.shape[0] == 1 and D == D_MODEL
    assert CT % ROWS == 0 and T % ROWS == 0 and T % GRID_W == 0
    cap = CAPACITY_FACTOR * T // N_EXPERTS
    assert cap % 8 == 0

    mrows = -(-(B + 1) // 8) * 8
    cc = jnp.zeros((mrows, D), f32).at[:B].set(c).at[B].set(c_ctx)
    mod = _modulation(cc, w_mod[0], b_mod[0])
    lat = mod[:B].reshape(B, N_MOD, D)
    cm = jnp.broadcast_to(mod[B].reshape(1, N_MOD, D), (B, N_MOD, D))
    modrows = jnp.concatenate([lat[:, 0:2], cm[:, 0:2]], axis=1)

    w_perm = _permute_columns(w_in[0]).astype(bf16)
    mu = rwkv_mu[0]
    ss = 2 * RWKV_WIDTH + DECAY_LORA + ICLR_LORA
    mu_full = jnp.zeros((2, IN_COLS), f32)
    mu_full = mu_full.at[:, C_RWK:C_RWK + 1024].set(mu[:, 0:1024])
    mu_full = mu_full.at[:, C_RWR:C_RWR + 512].set(mu[:, ss:ss + 512])
    mu_full = mu_full.at[:, C_LORA:C_LORA + 128].set(mu[:, 1024:ss])
    mu_full = mu_full.at[:, C_LORA + 128:USED_COLS].set(mu[:, ss + 512:])
    cosf, sinf = _rope_tables(T, CT)
    z = _inproj(x, ctx, modrows, norm_g[0, 0:1], w_perm, mu_full, cosf, sinf)

    lg = -jnp.exp(ret_log_decay[0].astype(f32))
    ret_o = _retention(lg, z, CT)

    G = 2 * RWKV_HEAD_DIM
    w2p = jnp.zeros((2, G, RWKV_WIDTH), f32).at[:, :DECAY_LORA].set(rwkv_w2[0]).astype(bf16)
    a2p = jnp.zeros((2, G, RWKV_WIDTH), f32).at[:, DECAY_LORA:].set(rwkv_a2[0]).astype(bf16)
    rw_o, bonus = _rwkv(z, rwkv_w0[0][:, None, :], rwkv_a0[0][:, None, :], w2p, a2p,
                        rwkv_k_k[0][None], rwkv_k_a[0][None], rwkv_r_k[0][None], CT)

    mod2 = jnp.stack([lat[:, 2], lat[:, 3], lat[:, 4]], axis=1)
    wr_pad = jnp.zeros((D, 128), f32).at[:, :N_EXPERTS].set(w_router[0])
    wr_hi = wr_pad.astype(bf16)
    wr_lo = (wr_pad - wr_hi.astype(f32)).astype(bf16)
    x1, h2, lt = _merge(x, ret_o, rw_o, bonus, z, mod2, norm_g[0, 1:3], ret_gn_g[0][None], rwkv_ln_g[0][None],
                        rwkv_ln_b[0][None], rwkv_g2[0].astype(bf16), w_br_ret[0].astype(bf16),
                        w_br_rwkv[0].astype(bf16), w_out[0].astype(bf16), wr_hi, wr_lo)

    slot, rt, cnt = _route(lt, cap)
    cnt = cnt[:, :, :T // GATHER_TILE + 1]
    eo = _ffn(cnt, slot.reshape(B, N_EXPERTS, T // GATHER_TILE, GATHER_TILE), h2, w_gate[0], w_up[0], w_down[0], cap)
    return _combine(cnt, rt, eo, x1, lat[:, 5:6], norm_g[0, 3:4], cap)
```

```python
import functools

import jax
import jax.numpy as jnp
from jax import lax
from jax.experimental import pallas as pl
from jax.experimental.pallas import tpu as pltpu

f32 = jnp.float32
bf16 = jnp.bfloat16

D_MODEL = 1024
GRID_W = 64
RET_HEAD_DIM = 128
RET_WIDTH = 512
RET_HEADS = 4
RET_CHUNK = 128
RET_EPS = 1e-5
ROPE_BASE = 10000.0
RWKV_HEAD_DIM = 64
RWKV_WIDTH = 512
RWKV_PAIRS = 4
RWKV_CHUNK = 64
RWKV_CHUNKS_PER_STEP = 4
RWKV_UNITS_PER_GROUP = 16
RWKV_SAMPLES_PER_STEP = 8
DECAY_LORA = 64
ICLR_LORA = 64
GATE_LORA = 128
RWKV_EPS = 64e-5
N_EXPERTS = 16
EXPERT_FF = 1024
CAPACITY_FACTOR = 2
N_MOD = 6
NORM_EPS = 1e-6

C_MERGE, C_RETK, C_RETV, C_RETQ, C_RETG = 0, 2048, 2560, 3072, 3584
C_RWK, C_RWV, C_RWR, C_LORA = 4096, 4608, 5120, 5632
USED_COLS = 5888
TN = 512
IN_COLS = -(-USED_COLS // TN) * TN
ROWS = 256
STEP_ROWS = 768
POST_ROWS = 128
QUERY_ROWS = 1024
MERGE_ROWS = 512

ROUTE_MAX_EXP = 126.0
ROUTE_EXP_STEPS = 8
ROUTE_VAL_STEPS = 26

VMEM_LIMIT = 56 * 1024 * 1024
FFN_SAMPLES_PER_STEP = 4
GATHER_TILE = 256
GATHER_ALIGN_LOG2 = 4
GATHER_ALIGN = 1 << GATHER_ALIGN_LOG2
GATHER_WINDOW = 80
SCATTER_WINDOW = 64
SCATTER_GROUP = 4


def _dot(a, b):
    return jnp.dot(a, b, preferred_element_type=f32)


def _dot_nt(a, b):
    return lax.dot_general(a, b, (((1,), (1,)), ((), ())), preferred_element_type=f32)


def _dot_tn(a, b):
    return lax.dot_general(a, b, (((0,), (0,)), ((), ())), preferred_element_type=f32)


def _bdot(a, b):
    return _dot(a.astype(bf16), b.astype(bf16))


def _split_dot(x, w):
    hi = x.astype(bf16)
    lo = (x - hi.astype(f32)).astype(bf16)
    return _dot(jnp.concatenate([hi, lo], axis=1), jnp.concatenate([w, w], axis=0))


def _sigmoid(x):
    return 0.5 * jnp.tanh(0.5 * x) + 0.5


def _params(sem, limit=VMEM_LIMIT):
    return pltpu.CompilerParams(dimension_semantics=sem, vmem_limit_bytes=limit)


def _mod_kernel(c_ref, w_ref, b_ref, o_ref):
    c = c_ref[...]
    s = c * jax.nn.sigmoid(c)
    o_ref[...] = _bdot(s, w_ref[...]) + b_ref[...]


def _modulation(cc, w_mod, b_mod):
    m, d = cc.shape
    n = w_mod.shape[1]
    tn = 512
    return pl.pallas_call(
        _mod_kernel,
        grid=(n // tn,),
        in_specs=[pl.BlockSpec((m, d), lambda j: (0, 0)),
                  pl.BlockSpec((d, tn), lambda j: (0, j)),
                  pl.BlockSpec((1, tn), lambda j: (0, j))],
        out_specs=pl.BlockSpec((m, tn), lambda j: (0, j)),
        out_shape=jax.ShapeDtypeStruct((m, n), f32),
        compiler_params=_params(("arbitrary",)),
        name="mod",
    )(cc, w_mod, b_mod.reshape(1, n))


def _inproj_kernel(x_ref, c_ref, mod_ref, g_ref, w_ref, mu_ref, cos_ref, sin_ref, o_ref, h_ref, z_ref):
    n = pl.program_id(1)
    T = x_ref.shape[1]
    L = T + c_ref.shape[1]
    nlat = T // ROWS
    nchunk = L // ROWS
    PAD = 8

    SR = STEP_ROWS if L % STEP_ROWS == 0 else ROWS

    @pl.when(n == 0)
    def _():
        z_ref[0:PAD, :] = jnp.zeros((PAD, TN), f32)
        z_ref[PAD + L:PAD + L + PAD, :] = jnp.zeros((PAD, TN), f32)

        def norm_chunk(src_ref, s0, r0, o):
            xb = src_ref[0, pl.ds(s0, ROWS), :]
            y = xb * lax.rsqrt(jnp.mean(xb * xb, axis=-1, keepdims=True) + NORM_EPS) * g_ref[...]
            sh = mod_ref[0, o:o + 1, :]
            sc = mod_ref[0, o + 1:o + 2, :]
            h_ref[pl.ds(r0, ROWS), :] = (y * (1.0 + sc) + sh).astype(bf16)

        def lat_body(i, carry):
            r0 = pl.multiple_of(i * ROWS, ROWS)
            norm_chunk(x_ref, r0, r0, 0)
            return carry

        lax.fori_loop(0, nlat, lat_body, 0)
        for j in range(nlat, nchunk):
            norm_chunk(c_ref, j * ROWS - T, j * ROWS, 2)

    PR = POST_ROWS
    QR = QUERY_ROWS

    def rope(scale):
        def post(z, r0):
            if scale != 1.0:
                z = z * scale
            cs = cos_ref[pl.ds(r0, z.shape[0]), :]
            sn = sin_ref[pl.ds(r0, z.shape[0]), :]
            parts = []
            for hh in range(TN // RET_HEAD_DIM):
                zh = z[:, hh * RET_HEAD_DIM:(hh + 1) * RET_HEAD_DIM]
                parts.append(zh * cs + pltpu.roll(zh, RET_HEAD_DIM // 2, 1) * sn)
            return jnp.concatenate(parts, axis=1)
        return post

    def shifted(r0, at_boundary):
        win = z_ref[pl.ds(r0, PR + 2 * PAD), :]
        prev = win[PAD - 1:PAD - 1 + PR]
        z = win[PAD:PAD + PR]
        nxt = win[PAD + 1:PAD + 1 + PR]
        if at_boundary:
            rid = r0 + lax.broadcasted_iota(jnp.int32, (PR, 1), 0)
            prev = jnp.where(rid == T, 0.0, prev)
            nxt = jnp.where(rid == T - 1, 0.0, nxt)
        mu0 = mu_ref[0:1, :]
        mu1 = mu_ref[1:2, :]
        return (1.0 - mu0 - mu1) * z + mu0 * prev + mu1 * nxt

    def lora_act(zs):
        lane = lax.broadcasted_iota(jnp.int32, (1, TN), 1)
        return jnp.where(lane < DECAY_LORA, jnp.tanh(zs),
                         jnp.where(lane < DECAY_LORA + ICLR_LORA, zs, jax.nn.sigmoid(zs)))

    def product(r0, rows=SR):
        return _dot(h_ref[pl.ds(r0, rows), :], w_ref[...])

    def direct(post, latent_only=False):
        rows, total = (QR, T) if latent_only and T % QR == 0 else (SR, L)

        def body(i, carry):
            r0 = pl.multiple_of(i * rows, rows)
            o_ref[0, pl.ds(r0, rows), :] = post(product(r0, rows), r0).astype(bf16)
            return carry
        lax.fori_loop(0, total // rows, body, 0)
        if total < L:
            o_ref[0, total:L, :] = jnp.zeros((L - total, TN), bf16)

    def via_buffer(act):
        def mm_body(i, carry):
            r0 = pl.multiple_of(i * SR, SR)
            z_ref[pl.ds(PAD + r0, SR), :] = product(r0)
            return carry
        lax.fori_loop(0, L // SR, mm_body, 0)

        def post(r0, at_boundary):
            o_ref[0, pl.ds(r0, PR), :] = act(shifted(r0, at_boundary)).astype(bf16)

        def post_body(i, carry):
            post(pl.multiple_of(i * PR, PR), False)
            return carry

        seam = T // PR
        lax.fori_loop(0, seam - 1, post_body, 0)
        post((seam - 1) * PR, True)
        post(seam * PR, True)
        lax.fori_loop(seam + 1, L // PR, post_body, 0)

    t = lambda c: c // TN
    pl.when(n < t(C_RETK))(lambda: direct(lambda z, r0: _sigmoid(z), latent_only=True))
    pl.when((n >= t(C_RETK)) & (n < t(C_RETV)))(lambda: direct(rope(RET_HEAD_DIM ** -0.5)))
    pl.when((n >= t(C_RETV)) & (n < t(C_RETQ)))(lambda: direct(lambda z, r0: z))
    pl.when((n >= t(C_RETQ)) & (n < t(C_RETG)))(lambda: direct(rope(1.0), latent_only=True))
    pl.when((n >= t(C_RETG)) & (n < t(C_RWK)))(lambda: direct(lambda z, r0: z * _sigmoid(z), latent_only=True))
    pl.when((n >= t(C_RWK)) & (n < t(C_LORA)))(lambda: via_buffer(lambda zs: zs))
    pl.when(n >= t(C_LORA))(lambda: via_buffer(lora_act))


def _inproj(x, ctx, modrows, g0, w_perm, mu_full, cosf, sinf):
    B, T, D = x.shape
    CT = ctx.shape[1]
    L = T + CT
    nt = IN_COLS // TN
    return pl.pallas_call(
        _inproj_kernel,
        grid=(B, nt),
        in_specs=[pl.BlockSpec((1, T, D), lambda b, n: (b, 0, 0)),
                  pl.BlockSpec((1, CT, D), lambda b, n: (b, 0, 0)),
                  pl.BlockSpec((1, 4, D), lambda b, n: (b, 0, 0)),
                  pl.BlockSpec((1, D), lambda b, n: (0, 0)),
                  pl.BlockSpec((D, TN), lambda b, n: (0, n)),
                  pl.BlockSpec((2, TN), lambda b, n: (0, n)),
                  pl.BlockSpec((L, RET_HEAD_DIM), lambda b, n: (0, 0)),
                  pl.BlockSpec((L, RET_HEAD_DIM), lambda b, n: (0, 0))],
        out_specs=pl.BlockSpec((1, L, TN), lambda b, n: (b, 0, n)),
        out_shape=jax.ShapeDtypeStruct((B, L, IN_COLS), bf16),
        scratch_shapes=[pltpu.VMEM((L, D), bf16), pltpu.VMEM((L + 16, TN), f32)],
        compiler_params=_params(("arbitrary", "arbitrary")),
        name="inproj",
    )(x, ctx, modrows, g0, w_perm, mu_full, cosf, sinf)


def _ret_kernel(lg_ref, q_ref, k_ref, v_ref, o_ref, r_ref, tab_ref, *, ctx):
    L = q_ref.shape[1]
    Cc = RET_CHUNK
    hd = RET_HEAD_DIM
    lat = L - ctx
    nc = ctx // Cc
    nl = lat // Cc
    combos = [(h, d) for h in range(RET_HEADS) for d in (0, 1)]
    INTRA, CROSS, TAIL, DECAY = 0, 1, 2, 3

    @pl.when(pl.program_id(0) == 0)
    def _():
        ii = lax.broadcasted_iota(jnp.int32, (Cc, Cc), 0).astype(f32)
        jj = lax.broadcasted_iota(jnp.int32, (Cc, Cc), 1).astype(f32)
        for idx, (h, d) in enumerate(combos):
            lg = lg_ref[d, h]
            if d == 0:
                diff = ii - jj
                cross = jnp.exp(lg * (ii + 1.0))
                tailw = jnp.exp(lg * (Cc - 1.0 - ii))
            else:
                diff = jj - ii
                cross = jnp.exp(lg * (Cc - ii))
                tailw = jnp.exp(lg * ii)
            tab_ref[idx, INTRA] = jnp.where(diff >= 0, jnp.exp(lg * jnp.maximum(diff, 0.0)), 0.0)
            tab_ref[idx, CROSS] = cross
            tab_ref[idx, TAIL] = tailw
            tab_ref[idx, DECAY] = jnp.exp(jnp.zeros((Cc, Cc), f32) + lg * Cc)

    r_ref[...] = jnp.zeros(r_ref.shape, f32)
    o_ref[...] = jnp.zeros(o_ref.shape, f32)

    def cols(h):
        return slice(h * hd, (h + 1) * hd)

    def update(idx, kc, vc):
        ks = (kc.astype(f32) * tab_ref[idx, TAIL]).astype(bf16)
        r_ref[idx] = r_ref[idx] * tab_ref[idx, DECAY] + _dot_tn(ks, vc)

    def ctx_step(s, carry):
        for idx, (h, d) in enumerate(combos):
            row0 = pl.multiple_of(lat + (s * Cc if d == 0 else (nc - 1 - s) * Cc), Cc)
            update(idx, k_ref[0, pl.ds(row0, Cc), cols(h)], v_ref[0, pl.ds(row0, Cc), cols(h)])
        return carry

    def lat_step(s, carry):
        t0s, qs, ks, vs = [], [], [], []
        for h, d in combos:
            t0 = pl.multiple_of(s * Cc if d == 0 else (nl - 1 - s) * Cc, Cc)
            row0 = t0
            t0s.append(t0)
            qs.append(q_ref[0, pl.ds(row0, Cc), cols(h)])
            ks.append(k_ref[0, pl.ds(row0, Cc), cols(h)])
            vs.append(v_ref[0, pl.ds(row0, Cc), cols(h)])
        n = len(combos)
        sc = [(_dot_nt(qs[i], ks[i]) * tab_ref[i, INTRA]).astype(bf16) for i in range(n)]
        oc = [_dot(qs[i], r_ref[i].astype(bf16)) * tab_ref[i, CROSS] for i in range(n)]
        oi = [_dot(sc[i], vs[i]) for i in range(n)]
        for i, (h, d) in enumerate(combos):
            o_ref[0, pl.ds(t0s[i], Cc), cols(h)] += oi[i] + oc[i]
        for i in range(n):
            update(i, ks[i], vs[i])
        return carry

    lax.fori_loop(0, nc, ctx_step, 0)
    lax.fori_loop(0, nl, lat_step, 0)


def _retention(lg, z, ctx):
    B, L, _ = z.shape
    T = L - ctx
    W = RET_WIDTH
    Cc = RET_CHUNK
    blk = lambda c0: pl.BlockSpec((1, L, W), lambda b: (b, 0, c0 // W))
    nchain = 2 * RET_HEADS
    return pl.pallas_call(
        functools.partial(_ret_kernel, ctx=ctx),
        grid=(B,),
        in_specs=[pl.BlockSpec(memory_space=pltpu.SMEM), blk(C_RETQ), blk(C_RETK), blk(C_RETV)],
        out_specs=pl.BlockSpec((1, T, W), lambda b: (b, 0, 0)),
        out_shape=jax.ShapeDtypeStruct((B, T, W), f32),
        scratch_shapes=[pltpu.VMEM((nchain, Cc, Cc), f32), pltpu.VMEM((nchain, 4, Cc, Cc), f32)],
        compiler_params=_params(("arbitrary",)),
        name="ret",
    )(lg, z, z, z)


def _rwkv_kernel(k_ref, v_ref, r_ref, lo_ref, w0_ref, a0_ref, w2_ref, a2_ref, kk_ref, ka_ref, rk_ref,
                 o_ref, bo_ref, s_ref, *, nc, nb, nsub):
    d = pl.program_id(1)
    s = pl.program_id(2)
    C = RWKV_CHUNK

    @pl.when(s == 0)
    def _():
        s_ref[...] = jnp.zeros(s_ref.shape, f32)

    def sub_chunk(j, carry):
        r0 = pl.multiple_of(jnp.where(d == 1, nsub - 1 - j, j) * C, C)
        rows = lambda ref: ref.at[:, pl.ds(r0, C), :]
        _rwkv_chunk(rows(k_ref), rows(v_ref), rows(r_ref), rows(lo_ref), w0_ref, a0_ref, w2_ref, a2_ref,
                    kk_ref, ka_ref, rk_ref, o_ref.at[:, :, pl.ds(r0, C), :], bo_ref.at[:, :, pl.ds(r0, C), :],
                    s_ref, rev=d == 1, emit=s >= nc, nb=nb)
        return carry

    lax.fori_loop(0, nsub, sub_chunk, 0)


def _rwkv_chunk(k_ref, v_ref, r_ref, lo_ref, w0_ref, a0_ref, w2_ref, a2_ref, kk_ref, ka_ref, rk_ref,
                o_ref, bo_ref, s_ref, *, rev, emit, nb):
    C = RWKV_CHUNK
    G = 2 * RWKV_HEAD_DIM

    ii = lax.broadcasted_iota(jnp.int32, (C, G), 0)
    lane = lax.broadcasted_iota(jnp.int32, (C, G), 1)
    jj = lane & (RWKV_HEAD_DIM - 1)
    head0 = lane < RWKV_HEAD_DIM
    dlt = jnp.where(rev, ii - jj, jj - ii)
    strict = dlt < 0
    incl = dlt <= 0
    eye = (ii == jj).astype(f32)
    gi = lax.broadcasted_iota(jnp.int32, (G, G), 0)
    gj = lax.broadcasted_iota(jnp.int32, (G, G), 1)
    blockdiag = (gi >= RWKV_HEAD_DIM) == (gj >= RWKV_HEAD_DIM)
    ones_bd = blockdiag.astype(bf16)
    tri = incl[:, 0:C].astype(bf16)
    same4 = (ii >> 2) == (jj >> 2)
    lag = [same4 & (dlt == -k) for k in (1, 2, 3)]
    offs = [((ii >> (lv + 1)) == (jj >> (lv + 1))) & ((ii >> lv) != (jj >> lv)) for lv in range(2, 6)]

    def block4_inverse(a):
        l4 = jnp.where(same4, a, 0.0)
        c1 = jnp.where(rev, pltpu.roll(l4, 1, 1), pltpu.roll(l4, G - 1, 1))
        c2 = jnp.where(rev, pltpu.roll(l4, 2, 1), pltpu.roll(l4, G - 2, 1))
        d1 = jnp.sum(jnp.where(lag[0], l4, 0.0), axis=0, keepdims=True)
        d2 = jnp.sum(jnp.where(lag[1], l4, 0.0), axis=0, keepdims=True)
        d1b = jnp.broadcast_to(d1, (8, G))
        d1n = jnp.where(rev, pltpu.roll(d1b, 1, 1), pltpu.roll(d1b, G - 1, 1))[0:1]
        two = c1 * d1
        three = two + c2 * d2 + c2 * (d1n * d1)
        return eye + l4 + jnp.where(lag[1], two, 0.0) + jnp.where(lag[2], three, 0.0)

    def segsum(x):
        return jnp.concatenate(
            [_bdot(x[:, p * G:(p + 1) * G], ones_bd) for p in range(RWKV_PAIRS)], axis=1)

    def stack(x):
        xb = x.astype(bf16)
        zero = jnp.zeros_like(xb)
        return jnp.concatenate([jnp.where(head0, xb, zero), jnp.where(head0, zero, xb)], axis=0)

    W = k_ref.shape[2]
    k_all = k_ref[...].astype(f32).reshape(nb * C, W)
    v_all = v_ref[...].astype(f32).reshape(nb * C, W)
    r_all = r_ref[...].astype(f32).reshape(nb * C, W)
    lo_all = lo_ref[:, :, 0:DECAY_LORA + ICLR_LORA].reshape(nb * C, DECAY_LORA + ICLR_LORA)
    ua = _dot(lo_all, jnp.concatenate([w2_ref[0], a2_ref[0]], axis=1))
    u_all = w0_ref[0] + ua[:, 0:W]
    softplus = jnp.maximum(-u_all, 0.0) + jnp.log1p(jnp.exp(-jnp.abs(u_all)))
    lw_all = -jnp.exp(-softplus - 0.5)
    a_all = jax.nn.sigmoid(a0_ref[0] + ua[:, W:2 * W])
    kkr = k_all * kk_ref[...]
    kd_all = k_all * (1.0 + (a_all - 1.0) * ka_ref[...])
    sums = segsum(jnp.concatenate([kkr * kkr, r_all * kd_all * rk_ref[...]], axis=0))
    kk_all = kkr * lax.rsqrt(sums[0:nb * C] + 1e-12)
    be_all = kk_all * a_all
    bonus_all = sums[nb * C:2 * nb * C] * v_all
    pre = []
    for bb in range(nb):
        rows = slice(bb * C, (bb + 1) * C)
        vx, rx, lw, kk, kd, be, bonus = (arr[rows] for arr in (v_all, r_all, lw_all, kk_all, kd_all, be_all,
                                                                bonus_all))
        cum = _split_dot_left(tri, lw)
        tot = jnp.where(rev, cum[0:1, :], cum[C - 1:C, :])
        gneg = jnp.exp(-cum)
        etot = jnp.exp(tot)
        gh = etot * gneg
        pre.append(dict(alb=-kk * jnp.exp(cum - lw), rb=rx * jnp.exp(cum), beb=be * gneg, kb=kd * gneg,
                        beh=be * gh, kh=kd * gh, etot=etot, v=vx, bonus=bonus))

    all_units = [(bb, p) for bb in range(nb) for p in range(RWKV_PAIRS)]
    sl = lambda p: slice(p * G, (p + 1) * G)
    part = lambda un, name: pre[un[0]][name][:, sl(un[1])]
    Om = {}
    for g0 in range(0, len(all_units), RWKV_UNITS_PER_GROUP):
        units = all_units[g0:g0 + RWKV_UNITS_PER_GROUP]
        S = {un: s_ref[un[0] * RWKV_PAIRS + un[1]] for un in units}
        Sb = {un: S[un].astype(bf16) for un in units}
        X = {un: part(un, "alb").astype(bf16) for un in units}
        Rb = {un: part(un, "rb").astype(bf16) for un in units}
        Ybs = {un: stack(part(un, "beb")) for un in units}
        Yks = {un: stack(part(un, "kb")) for un in units}
        Vb = {un: part(un, "v").astype(bf16) for un in units}
        Vs = {un: stack(part(un, "v")) for un in units}

        XR = {un: jnp.concatenate([X[un], Rb[un]], axis=0) for un in units}
        ABO = {un: _dot_nt(XR[un], jnp.concatenate([Ybs[un], Yks[un], Sb[un]], axis=0)) for un in units}
        APb = {un: ABO[un][:, 0:G] for un in units}
        APk = {un: ABO[un][:, G:2 * G] for un in units}
        BO = {un: ABO[un][:, 2 * G:3 * G] for un in units}
        Aab = {un: jnp.where(strict, APb[un][0:C], 0.0) for un in units}
        Aak = {un: jnp.where(strict, APk[un][0:C], 0.0).astype(bf16) for un in units}
        Pab = {un: jnp.where(incl, APb[un][C:2 * C], 0.0).astype(bf16) for un in units}
        Pak = {un: jnp.where(incl, APk[un][C:2 * C], 0.0).astype(bf16) for un in units}

        Tm = {un: block4_inverse(Aab[un]) for un in units}
        for off in offs:
            Xs = {un: _dot(jnp.where(off, Aab[un], 0.0).astype(bf16), stack(Tm[un])) for un in units}
            Tm = {un: Tm[un] + _dot(Tm[un].astype(bf16), stack(Xs[un])) for un in units}

        AV = {un: _dot(jnp.concatenate([Aak[un], Pak[un]], axis=0), Vs[un]) for un in units}
        Bm = {un: BO[un][0:C] + AV[un][0:C] for un in units}
        U = {un: _dot(Tm[un].astype(bf16), stack(Bm[un])) for un in units}
        for un in units:
            Om[un] = BO[un][C:2 * C] + AV[un][C:2 * C] + _dot(Pab[un], stack(U[un]))
        for un in units:
            bb, p = un
            upd = _dot_tn(jnp.concatenate([U[un].astype(bf16), Vb[un]], axis=0),
                          jnp.concatenate([part(un, "beh").astype(bf16), part(un, "kh").astype(bf16)], axis=0))
            s_ref[bb * RWKV_PAIRS + p] = jnp.where(blockdiag, S[un] * part(un, "etot") + upd, 0.0)

    @pl.when(emit)
    def _():
        for bb in range(nb):
            o_ref[0, bb] = jnp.concatenate([Om[bb, p] for p in range(RWKV_PAIRS)], axis=1)
            bo_ref[0, bb] = pre[bb]["bonus"]


def _split_dot_left(w, x):
    hi = x.astype(bf16)
    lo = (x - hi.astype(f32)).astype(bf16)
    return _dot(jnp.concatenate([w, w], axis=1), jnp.concatenate([hi, lo], axis=0))


def _rwkv(z, w0, a0, w2p, a2p, k_k, k_a, r_k, ctx):
    B, L, _ = z.shape
    T = L - ctx
    nsub = RWKV_CHUNKS_PER_STEP
    while ctx % (nsub * RWKV_CHUNK) or T % (nsub * RWKV_CHUNK):
        nsub //= 2
    C = nsub * RWKV_CHUNK
    nc, nl = ctx // C, T // C
    W = RWKV_WIDTH

    def chunk(d, s):
        fwd = jnp.where(s < nc, nl + s, s - nc)
        bwd = jnp.where(s < nc, nl + nc - 1 - s, nl - 1 - (s - nc))
        return jnp.where(d == 0, fwd, bwd)

    def ochunk(d, s):
        sl = jnp.maximum(s - nc, 0)
        return jnp.where(d == 0, sl, nl - 1 - sl)

    nb = RWKV_SAMPLES_PER_STEP if B % RWKV_SAMPLES_PER_STEP == 0 else 1
    zblk = lambda c0, w: pl.BlockSpec((nb, C, w), lambda b, d, s: (b, chunk(d, s), c0 // w))
    dpar = lambda r: pl.BlockSpec((1, r, W), lambda b, d, s: (d, 0, 0))
    par = pl.BlockSpec((1, W), lambda b, d, s: (0, 0))
    oblk = pl.BlockSpec((1, nb, C, W), lambda b, d, s: (d, b, ochunk(d, s), 0))
    G = 2 * RWKV_HEAD_DIM
    return pl.pallas_call(
        functools.partial(_rwkv_kernel, nc=nc, nb=nb, nsub=nsub),
        grid=(B // nb, 2, nc + nl),
        in_specs=[zblk(C_RWK, W), zblk(C_RWV, W), zblk(C_RWR, W), zblk(C_LORA, 256),
                  dpar(1), dpar(1), dpar(G), dpar(G), par, par, par],
        out_specs=[oblk, oblk],
        out_shape=[jax.ShapeDtypeStruct((2, B, T, W), f32), jax.ShapeDtypeStruct((2, B, T, W), f32)],
        scratch_shapes=[pltpu.VMEM((nb * RWKV_PAIRS, G, G), f32)],
        compiler_params=_params(("arbitrary", "arbitrary", "arbitrary")),
        name="rwkv",
    )(z, z, z, z, w0, a0, w2p, a2p, k_k, k_a, r_k)


def _merge_kernel(x_ref, ret_ref, rw0_ref, rw1_ref, b0_ref, b1_ref, gate_ref, retg_ref, gd_ref, mod_ref,
                  ng_ref, gn_ref, lng_ref, lnb_ref, g2_ref, wbr_ref, wbw_ref, wout_ref, wrh_ref, wrl_ref,
                  x1_ref, h2_ref, lt_ref):
    hd = RET_HEAD_DIM
    ret = ret_ref[0]
    parts = []
    for hh in range(RET_HEADS):
        xh = ret[:, hh * hd:(hh + 1) * hd]
        mu = jnp.mean(xh, axis=-1, keepdims=True)
        dv = xh - mu
        var = jnp.mean(dv * dv, axis=-1, keepdims=True)
        parts.append(dv * lax.rsqrt(var + RET_EPS))
    yr = retg_ref[0].astype(f32) * (jnp.concatenate(parts, axis=1) * gn_ref[...])
    y_ret = _bdot(yr, wbr_ref[...])
    W = RWKV_WIDTH
    gi = lax.broadcasted_iota(jnp.int32, (W, W), 0)
    gj = lax.broadcasted_iota(jnp.int32, (W, W), 1)
    ones_bd = ((gi >> 6) == (gj >> 6)).astype(bf16)
    o = rw0_ref[0, 0] + rw1_ref[0, 0]
    mu = _split_dot(o, ones_bd) * (1.0 / RWKV_HEAD_DIM)
    dv = o - mu
    var = _bdot(dv * dv, ones_bd) * (1.0 / RWKV_HEAD_DIM)
    yw = dv * lax.rsqrt(var + RWKV_EPS) * lng_ref[...] + lnb_ref[...]
    gate = _dot(gd_ref[0], g2_ref[...])
    yw = (yw + b0_ref[0, 0] + b1_ref[0, 0]) * gate
    y_rw = _bdot(yw, wbw_ref[...])
    D = y_ret.shape[1]
    g = gate_ref[0].astype(f32)
    m = g[:, :D] * y_ret + g[:, D:] * y_rw
    y = _bdot(m, wout_ref[...])

    def rms(v, gg):
        return v * lax.rsqrt(jnp.mean(v * v, axis=-1, keepdims=True) + NORM_EPS) * gg

    x1 = x_ref[0] + mod_ref[0, 0:1, :] * rms(y, ng_ref[0:1, :])
    x1_ref[0] = x1
    h2 = rms(x1, ng_ref[1:2, :]) * (1.0 + mod_ref[0, 2:3, :]) + mod_ref[0, 1:2, :]
    h2b = h2.astype(bf16)
    h2_ref[0] = h2b
    h2l = (h2 - h2b.astype(f32)).astype(bf16)
    lgt = _dot(jnp.concatenate([h2b, h2l, h2b], axis=1),
               jnp.concatenate([wrh_ref[...], wrh_ref[...], wrl_ref[...]], axis=0))
    lt_ref[0] = lgt.T[0:N_EXPERTS, :]


def _merge(x, ret_o, rw_o, bonus, z, mod2, ng12, gn, lng, lnb, g2, wbr, wbw, wout, wrh, wrl):
    B, T, D = x.shape
    tm = MERGE_ROWS if T % MERGE_ROWS == 0 else ROWS
    W = RWKV_WIDTH
    row = lambda w: pl.BlockSpec((1, tm, w), lambda b, i: (b, i, 0))
    dblk = lambda dd: pl.BlockSpec((1, 1, tm, W), lambda b, i: (dd, b, i, 0))
    zblk = lambda c0, w: pl.BlockSpec((1, tm, w), lambda b, i: (b, i, c0 // w))
    full = lambda a: pl.BlockSpec(a.shape, lambda b, i: (0,) * a.ndim)
    return pl.pallas_call(
        _merge_kernel,
        grid=(B, T // tm),
        in_specs=[row(D), row(W), dblk(0), dblk(1), dblk(0), dblk(1),
                  zblk(C_MERGE, 2 * D), zblk(C_RETG, W), zblk(C_LORA + 128, 128),
                  pl.BlockSpec((1, 3, D), lambda b, i: (b, 0, 0)),
                  full(ng12), full(gn), full(lng), full(lnb), full(g2), full(wbr), full(wbw), full(wout),
                  full(wrh), full(wrl)],
        out_specs=[row(D), row(D), pl.BlockSpec((1, N_EXPERTS, tm), lambda b, i: (b, 0, i))],
        out_shape=[jax.ShapeDtypeStruct((B, T, D), f32), jax.ShapeDtypeStruct((B, T, D), bf16),
                   jax.ShapeDtypeStruct((B, N_EXPERTS, T), f32)],
        compiler_params=_params(("arbitrary", "arbitrary")),
        name="merge",
    )(x, ret_o, rw_o, rw_o, bonus, bonus, z, z, z, mod2, ng12, gn, lng, lnb, g2, wbr, wbw, wout, wrh, wrl)


def _route_kernel(lt_ref, slot_ref, rt_ref, cnt_ref, slotf_ref, gate_ref, *, cap):
    B, NE, T = lt_ref.shape
    lg = lt_ref[...]
    mx = jnp.max(lg, axis=1, keepdims=True)
    ex = jnp.exp(lg - mx)
    aff = (ex / jnp.sum(ex, axis=1, keepdims=True)).reshape(B * NE, T)
    E = B * NE

    def count_ge(cand):
        return jnp.sum((aff >= cand).astype(f32), axis=1, keepdims=True)

    def exp_step(_, kk):
        k_lo, k_hi = kk
        km = jnp.floor((k_lo + k_hi) * 0.5)
        ok = count_ge(jnp.exp2(-km)) >= cap
        return jnp.where(ok, k_lo, km), jnp.where(ok, km, k_hi)

    k_lo, k_hi = lax.fori_loop(0, ROUTE_EXP_STEPS, exp_step,
                               (jnp.full((E, 1), -1.0, f32), jnp.full((E, 1), ROUTE_MAX_EXP, f32)))
    lo0 = jnp.where(k_hi >= ROUTE_MAX_EXP, 0.0, jnp.exp2(-k_hi))
    hi0 = jnp.exp2(-k_lo)

    def val_step(_, lh):
        lo, hi = lh
        mid = (lo + hi) * 0.5
        ok = count_ge(mid) >= cap
        return jnp.where(ok, mid, lo), jnp.where(ok, hi, mid)

    lo, hi = lax.fori_loop(0, ROUTE_VAL_STEPS, val_step, (lo0, hi0))
    gt = aff >= hi
    eq = (aff >= lo) & (aff < hi)
    need = cap - jnp.sum(gt.astype(f32), axis=1, keepdims=True)
    tri = (lax.broadcasted_iota(jnp.int32, (T, T), 0) < lax.broadcasted_iota(jnp.int32, (T, T), 1)).astype(bf16)
    eq_before = _dot(eq.astype(bf16), tri)
    sel = gt | (eq & (eq_before < need))
    slot = _dot(sel.astype(bf16), tri)
    slot_f = jnp.where(sel, slot, -1.0)
    slot_ref[...] = slot_f.astype(jnp.int32).reshape(B, NE, T)
    before = (lax.broadcasted_iota(jnp.int32, (T, 128), 0)
              < lax.broadcasted_iota(jnp.int32, (T, 128), 1) * GATHER_TILE).astype(bf16)
    cnt_ref[...] = _dot(sel.astype(bf16), before).astype(jnp.int32).reshape(B, NE, 128)
    slotf_ref[...] = slot_f.reshape(B, NE, T)
    gate_ref[...] = jnp.where(sel, aff, 0.0).reshape(B, NE, T)

    def transpose_sample(b, carry):
        packed = jnp.concatenate([slotf_ref[b], gate_ref[b], jnp.zeros((128 - 2 * NE, T), f32)], axis=0)
        rt_ref[b] = packed.T.astype(bf16)
        return carry

    lax.fori_loop(0, B, transpose_sample, 0)


def _route(lt, cap):
    B, E, T = lt.shape
    assert cap <= 256 and T % GATHER_TILE == 0 and T // GATHER_TILE < 128
    return pl.pallas_call(
        functools.partial(_route_kernel, cap=cap),
        grid=(1,),
        in_specs=[pl.BlockSpec((B, E, T), lambda i: (0, 0, 0))],
        out_specs=[pl.BlockSpec((B, E, T), lambda i: (0, 0, 0)), pl.BlockSpec((B, T, 128), lambda i: (0, 0, 0)),
                   pl.BlockSpec((B, E, 128), lambda i: (0, 0, 0))],
        out_shape=[jax.ShapeDtypeStruct((B, E, T), jnp.int32), jax.ShapeDtypeStruct((B, T, 128), bf16),
                   jax.ShapeDtypeStruct((B, E, 128), jnp.int32)],
        scratch_shapes=[pltpu.VMEM((B, E, T), f32), pltpu.VMEM((B, E, T), f32)],
        compiler_params=_params(("arbitrary",)),
        name="route",
    )(lt)


def _gather_kernel(cnt_ref, slot_ref, h_ref, o_ref, acc_ref, *, cap):
    b = pl.program_id(0)
    E = slot_ref.shape[1]
    T, D = h_ref.shape[1], h_ref.shape[2]
    GT, GW, GA = GATHER_TILE, GATHER_WINDOW, GATHER_ALIGN
    nt = T // GT
    base = [[pl.multiple_of(lax.shift_left(lax.shift_right_logical(cnt_ref[b, e, j], GATHER_ALIGN_LOG2),
                                           GATHER_ALIGN_LOG2), GA) for j in range(nt)] for e in range(E)]
    fits = None
    for e in range(E):
        for j in range(nt):
            ok = cnt_ref[b, e, j + 1] - base[e][j] <= GW
            fits = ok if fits is None else fits & ok

    @pl.when(fits)
    def _():
        acc_ref[...] = jnp.zeros(acc_ref.shape, bf16)
        win = lax.broadcasted_iota(jnp.int32, (GW, GT), 0)
        for j in range(nt):
            onehot = jnp.concatenate([(slot_ref[0, e, j:j + 1, :] == base[e][j] + win).astype(bf16)
                                      for e in range(E)], axis=0)
            res = _dot(onehot, h_ref[0, j * GT:(j + 1) * GT, :]).astype(bf16)
            for e in range(E):
                acc_ref[e, pl.ds(base[e][j], GW), :] += res[e * GW:(e + 1) * GW]
        for e in range(E):
            o_ref[0, e] = acc_ref[e, 0:cap, :]

    @pl.when(jnp.logical_not(fits))
    def _():
        rows = lax.broadcasted_iota(jnp.int32, (cap, GT), 0)

        def per_expert(e, carry):
            acc = jnp.zeros((cap, D), f32)
            for j in range(nt):
                onehot = (slot_ref[0, e, j:j + 1, :] == rows).astype(bf16)
                acc = acc + _dot(onehot, h_ref[0, j * GT:(j + 1) * GT, :])
            o_ref[0, e] = acc.astype(bf16)
            return carry

        lax.fori_loop(0, E, per_expert, 0)


def _gather(cnt, slot4, h2, cap):
    B, T, D = h2.shape
    E = slot4.shape[1]
    grid_spec = pltpu.PrefetchScalarGridSpec(
        num_scalar_prefetch=1,
        grid=(B,),
        in_specs=[pl.BlockSpec((1, E, T // GATHER_TILE, GATHER_TILE), lambda b, c: (b, 0, 0, 0)),
                  pl.BlockSpec((1, T, D), lambda b, c: (b, 0, 0))],
        out_specs=pl.BlockSpec((1, E, cap, D), lambda b, c: (b, 0, 0, 0)),
        scratch_shapes=[pltpu.VMEM((E, cap + GATHER_WINDOW, D), bf16)])
    return pl.pallas_call(
        functools.partial(_gather_kernel, cap=cap),
        grid_spec=grid_spec,
        out_shape=jax.ShapeDtypeStruct((B, E, cap, D), bf16),
        compiler_params=_params(("arbitrary",)),
        name="gather",
    )(cnt, slot4, h2)


def _ffn_kernel(x_ref, wg_ref, wu_ref, wd_ref, o_ref, wgu_ref, wdb_ref, *, cap):
    F = wg_ref.shape[2]

    @pl.when(pl.program_id(1) == 0)
    def _():
        wgu_ref[:, 0:F] = wg_ref[0].astype(bf16)
        wgu_ref[:, F:2 * F] = wu_ref[0].astype(bf16)
        wdb_ref[...] = wd_ref[0].astype(bf16)

    nb = x_ref.shape[0]
    xg = jnp.concatenate([x_ref[bb, 0] for bb in range(nb)], axis=0)
    hgu = _dot(xg, wgu_ref[...])
    hg = hgu[:, 0:F]
    hu = hgu[:, F:2 * F]
    hid = (hg * _sigmoid(hg) * hu).astype(bf16)
    out = _dot(hid, wdb_ref[...]).astype(bf16)
    for bb in range(nb):
        o_ref[bb, 0] = out[bb * cap:(bb + 1) * cap]


def _ffn(xg, wg, wu, wd, cap):
    B, E, _, D = xg.shape
    F = wg.shape[2]
    nb = FFN_SAMPLES_PER_STEP if B % FFN_SAMPLES_PER_STEP == 0 else 1
    return pl.pallas_call(
        functools.partial(_ffn_kernel, cap=cap),
        grid=(E, B // nb),
        in_specs=[pl.BlockSpec((nb, 1, cap, D), lambda e, b: (b, e, 0, 0)),
                  pl.BlockSpec((1, D, F), lambda e, b: (e, 0, 0)),
                  pl.BlockSpec((1, D, F), lambda e, b: (e, 0, 0)),
                  pl.BlockSpec((1, F, D), lambda e, b: (e, 0, 0))],
        out_specs=pl.BlockSpec((nb, 1, cap, D), lambda e, b: (b, e, 0, 0)),
        out_shape=jax.ShapeDtypeStruct((B, E, cap, D), bf16),
        scratch_shapes=[pltpu.VMEM((D, 2 * F), bf16), pltpu.VMEM((F, D), bf16)],
        compiler_params=_params(("arbitrary", "arbitrary")),
        name="ffn",
    )(xg, wg, wu, wd)


def _combine_kernel(cnt_ref, rt_ref, eo_ref, x1_ref, mod_ref, ng_ref, o_ref, y_ref, *, cap):
    b = pl.program_id(0)
    i = pl.program_id(1)
    rt = rt_ref[0].astype(f32)
    tm = rt.shape[0]
    E = eo_ref.shape[1]
    W = min(SCATTER_WINDOW, cap)
    GS = SCATTER_GROUP
    ST = GATHER_TILE
    for st in range(tm // ST):
        t = i * (tm // ST) + st
        r = rt[st * ST:(st + 1) * ST]
        base, fits = [], None
        for e in range(E):
            lo = cnt_ref[b, e, t]
            be = jnp.minimum(lax.shift_left(lax.shift_right_logical(lo, GATHER_ALIGN_LOG2), GATHER_ALIGN_LOG2),
                             cap - W)
            base.append(pl.multiple_of(be, GATHER_ALIGN))
            ok = cnt_ref[b, e, t + 1] - be <= W
            fits = ok if fits is None else fits & ok

        def scatter_matrix(e, first, width, r=r):
            cols = (first + lax.broadcasted_iota(jnp.int32, (ST, width), 1)).astype(f32)
            return jnp.where(r[:, e:e + 1] == cols, r[:, E + e:E + e + 1], 0.0).astype(bf16)

        @pl.when(fits)
        def _():
            y = jnp.zeros((ST, y_ref.shape[1]), f32)
            for g in range(0, E, GS):
                p = jnp.concatenate([scatter_matrix(e, base[e], W) for e in range(g, g + GS)], axis=1)
                rows = jnp.concatenate([eo_ref[0, e, pl.ds(base[e], W), :] for e in range(g, g + GS)], axis=0)
                y = y + _dot(p, rows)
            y_ref[st * ST:(st + 1) * ST, :] = y

        @pl.when(jnp.logical_not(fits))
        def _():
            y = jnp.zeros((ST, y_ref.shape[1]), f32)
            for e in range(E):
                y = y + _dot(scatter_matrix(e, 0, cap), eo_ref[0, e])
            y_ref[st * ST:(st + 1) * ST, :] = y

    y = y_ref[...]
    yn = y * lax.rsqrt(jnp.mean(y * y, axis=-1, keepdims=True) + NORM_EPS) * ng_ref[...]
    o_ref[0] = x1_ref[0] + mod_ref[0] * yn


def _combine(cnt, rt, eo, x1, g2mod, ng3, cap):
    B, T, D = x1.shape
    E = eo.shape[1]
    tm = 512 if T % 512 == 0 else T
    assert E % SCATTER_GROUP == 0 and tm % GATHER_TILE == 0
    grid_spec = pltpu.PrefetchScalarGridSpec(
        num_scalar_prefetch=1,
        grid=(B, T // tm),
        in_specs=[pl.BlockSpec((1, tm, 128), lambda b, i, c: (b, i, 0)),
                  pl.BlockSpec((1, E, cap, D), lambda b, i, c: (b, 0, 0, 0)),
                  pl.BlockSpec((1, tm, D), lambda b, i, c: (b, i, 0)),
                  pl.BlockSpec((1, 1, D), lambda b, i, c: (b, 0, 0)),
                  pl.BlockSpec((1, D), lambda b, i, c: (0, 0))],
        out_specs=pl.BlockSpec((1, tm, D), lambda b, i, c: (b, i, 0)),
        scratch_shapes=[pltpu.VMEM((tm, D), f32)])
    return pl.pallas_call(
        functools.partial(_combine_kernel, cap=cap),
        grid_spec=grid_spec,
        out_shape=jax.ShapeDtypeStruct((B, T, D), f32),
        compiler_params=_params(("arbitrary", "arbitrary")),
        name="combine",
    )(cnt, rt, eo, x1, g2mod, ng3)


def _permute_columns(w):
    sk, sv, rk, rv, wd, ad = 0, 512, 1024, 1536, 2048, 2112
    q0 = 2176
    rq, rg, rr, gd, mg = q0, q0 + 512, q0 + 1024, q0 + 1536, q0 + 1664
    order = [(mg, 2048), (sk, 512), (sv, 512), (rq, 512), (rg, 512), (rk, 512), (rv, 512), (rr, 512),
             (wd, 64), (ad, 64), (gd, 128)]
    parts = [w[:, a:a + n] for a, n in order]
    parts.append(jnp.zeros((w.shape[0], IN_COLS - USED_COLS), w.dtype))
    return jnp.concatenate(parts, axis=1)


def _rope_tables(T, CT):
    t = jnp.arange(T)
    nfreq = RET_HEAD_DIM // 4
    inv = ROPE_BASE ** (-jnp.arange(nfreq, dtype=f32) / nfreq)
    ang = jnp.concatenate([(t // GRID_W).astype(f32)[:, None] * inv,
                           (t % GRID_W).astype(f32)[:, None] * inv], axis=-1)
    cos, sin = jnp.cos(ang), jnp.sin(ang)
    cosf = jnp.concatenate([cos, cos], axis=1)
    sinf = jnp.concatenate([-sin, sin], axis=1)
    return (jnp.concatenate([cosf, jnp.ones((CT, RET_HEAD_DIM), f32)], axis=0),
            jnp.concatenate([sinf, jnp.zeros((CT, RET_HEAD_DIM), f32)], axis=0))


def kernel(x, c, ctx, c_ctx, w_mod, b_mod, norm_g, w_in, ret_log_decay, ret_gn_g, rwkv_mu, rwkv_k_k, rwkv_k_a,
           rwkv_r_k, rwkv_w0, rwkv_w2, rwkv_a0, rwkv_a2, rwkv_g2, rwkv_ln_g, rwkv_ln_b, w_br_ret, w_br_rwkv,
           w_out, w_router, w_gate, w_up, w_down):
    B, T, D = x.shape
    CT = ctx.shape[1]
    assert w_mod.shape[0] == 1 and D == D_MODEL
    assert CT % ROWS == 0 and T % ROWS == 0 and T % GRID_W == 0
    cap = CAPACITY_FACTOR * T // N_EXPERTS
    assert cap % 8 == 0

    mrows = -(-(B + 1) // 8) * 8
    cc = jnp.zeros((mrows, D), f32).at[:B].set(c).at[B].set(c_ctx)
    mod = _modulation(cc, w_mod[0], b_mod[0])
    lat = mod[:B].reshape(B, N_MOD, D)
    cm = jnp.broadcast_to(mod[B].reshape(1, N_MOD, D), (B, N_MOD, D))
    modrows = jnp.concatenate([lat[:, 0:2], cm[:, 0:2]], axis=1)

    w_perm = _permute_columns(w_in[0]).astype(bf16)
    mu = rwkv_mu[0]
    ss = 2 * RWKV_WIDTH + DECAY_LORA + ICLR_LORA
    mu_full = jnp.zeros((2, IN_COLS), f32)
    mu_full = mu_full.at[:, C_RWK:C_RWK + 1024].set(mu[:, 0:1024])
    mu_full = mu_full.at[:, C_RWR:C_RWR + 512].set(mu[:, ss:ss + 512])
    mu_full = mu_full.at[:, C_LORA:C_LORA + 128].set(mu[:, 1024:ss])
    mu_full = mu_full.at[:, C_LORA + 128:USED_COLS].set(mu[:, ss + 512:])
    cosf, sinf = _rope_tables(T, CT)
    z = _inproj(x, ctx, modrows, norm_g[0, 0:1], w_perm, mu_full, cosf, sinf)

    lg = -jnp.exp(ret_log_decay[0].astype(f32))
    ret_o = _retention(lg, z, CT)

    G = 2 * RWKV_HEAD_DIM
    w2p = jnp.zeros((2, G, RWKV_WIDTH), f32).at[:, :DECAY_LORA].set(rwkv_w2[0]).astype(bf16)
    a2p = jnp.zeros((2, G, RWKV_WIDTH), f32).at[:, DECAY_LORA:].set(rwkv_a2[0]).astype(bf16)
    rw_o, bonus = _rwkv(z, rwkv_w0[0][:, None, :], rwkv_a0[0][:, None, :], w2p, a2p,
                        rwkv_k_k[0][None], rwkv_k_a[0][None], rwkv_r_k[0][None], CT)

    mod2 = jnp.stack([lat[:, 2], lat[:, 3], lat[:, 4]], axis=1)
    wr_pad = jnp.zeros((D, 128), f32).at[:, :N_EXPERTS].set(w_router[0])
    wr_hi = wr_pad.astype(bf16)
    wr_lo = (wr_pad - wr_hi.astype(f32)).astype(bf16)
    x1, h2, lt = _merge(x, ret_o, rw_o, bonus, z, mod2, norm_g[0, 1:3], ret_gn_g[0][None], rwkv_ln_g[0][None],
                        rwkv_ln_b[0][None], rwkv_g2[0].astype(bf16), w_br_ret[0].astype(bf16),
                        w_br_rwkv[0].astype(bf16), w_out[0].astype(bf16), wr_hi, wr_lo)

    slot, rt, cnt = _route(lt, cap)
    cnt = cnt[:, :, :T // GATHER_TILE + 1]
    xg = _gather(cnt, slot.reshape(B, N_EXPERTS, T // GATHER_TILE, GATHER_TILE), h2, cap)
    eo = _ffn(xg, w_gate[0], w_up[0], w_down[0], cap)
    return _combine(cnt, rt, eo, x1, lat[:, 5:6], norm_g[0, 3:4], cap)
```

```python
import functools

import jax
import jax.numpy as jnp
from jax import lax
from jax.experimental import pallas as pl
from jax.experimental.pallas import tpu as pltpu

f32 = jnp.float32
bf16 = jnp.bfloat16

D_MODEL = 1024
GRID_W = 64
RET_HEAD_DIM = 128
RET_WIDTH = 512
RET_HEADS = 4
RET_CHUNK = 128
RET_EPS = 1e-5
ROPE_BASE = 10000.0
RWKV_HEAD_DIM = 64
RWKV_WIDTH = 512
RWKV_PAIRS = 4
RWKV_CHUNK = 64
RWKV_CHUNKS_PER_STEP = 4
RWKV_UNITS_PER_GROUP = 16
RWKV_SAMPLES_PER_STEP = 8
DECAY_LORA = 64
ICLR_LORA = 64
GATE_LORA = 128
RWKV_EPS = 64e-5
N_EXPERTS = 16
EXPERT_FF = 1024
CAPACITY_FACTOR = 2
N_MOD = 6
NORM_EPS = 1e-6

C_MERGE, C_RETK, C_RETV, C_RETQ, C_RETG = 0, 2048, 2560, 3072, 3584
C_RWK, C_RWV, C_RWR, C_LORA = 4096, 4608, 5120, 5632
USED_COLS = 5888
TN = 512
IN_COLS = -(-USED_COLS // TN) * TN
ROWS = 256
STEP_ROWS = 768
POST_ROWS = 128
QUERY_ROWS = 1024
MERGE_ROWS = 512

ROUTE_MAX_EXP = 126.0
ROUTE_EXP_STEPS = 8
ROUTE_VAL_STEPS = 26

VMEM_LIMIT = 56 * 1024 * 1024
FFN_SAMPLES_PER_STEP = 4
GATHER_TILE = 256
GATHER_ALIGN_LOG2 = 4
GATHER_ALIGN = 1 << GATHER_ALIGN_LOG2
GATHER_WINDOW = 80
SCATTER_WINDOW = 64
SCATTER_GROUP = 4


def _dot(a, b):
    return jnp.dot(a, b, preferred_element_type=f32)


def _dot_nt(a, b):
    return lax.dot_general(a, b, (((1,), (1,)), ((), ())), preferred_element_type=f32)


def _dot_tn(a, b):
    return lax.dot_general(a, b, (((0,), (0,)), ((), ())), preferred_element_type=f32)


def _bdot(a, b):
    return _dot(a.astype(bf16), b.astype(bf16))


def _split_dot(x, w):
    hi = x.astype(bf16)
    lo = (x - hi.astype(f32)).astype(bf16)
    return _dot(jnp.concatenate([hi, lo], axis=1), jnp.concatenate([w, w], axis=0))


def _sigmoid(x):
    return 0.5 * jnp.tanh(0.5 * x) + 0.5


def _params(sem, limit=VMEM_LIMIT):
    return pltpu.CompilerParams(dimension_semantics=sem, vmem_limit_bytes=limit)


def _mod_kernel(c_ref, w_ref, b_ref, o_ref):
    c = c_ref[...]
    s = c * jax.nn.sigmoid(c)
    o_ref[...] = _bdot(s, w_ref[...]) + b_ref[...]


def _modulation(cc, w_mod, b_mod):
    m, d = cc.shape
    n = w_mod.shape[1]
    tn = 512
    return pl.pallas_call(
        _mod_kernel,
        grid=(n // tn,),
        in_specs=[pl.BlockSpec((m, d), lambda j: (0, 0)),
                  pl.BlockSpec((d, tn), lambda j: (0, j)),
                  pl.BlockSpec((1, tn), lambda j: (0, j))],
        out_specs=pl.BlockSpec((m, tn), lambda j: (0, j)),
        out_shape=jax.ShapeDtypeStruct((m, n), f32),
        compiler_params=_params(("arbitrary",)),
        name="mod",
    )(cc, w_mod, b_mod.reshape(1, n))


def _inproj_kernel(x_ref, c_ref, mod_ref, g_ref, w_ref, mu_ref, cos_ref, sin_ref, o_ref, h_ref, z_ref):
    n = pl.program_id(1)
    T = x_ref.shape[1]
    L = T + c_ref.shape[1]
    nlat = T // ROWS
    nchunk = L // ROWS
    PAD = 8

    SR = STEP_ROWS if L % STEP_ROWS == 0 else ROWS

    @pl.when(n == 0)
    def _():
        z_ref[0:PAD, :] = jnp.zeros((PAD, TN), f32)
        z_ref[PAD + L:PAD + L + PAD, :] = jnp.zeros((PAD, TN), f32)

        def norm_chunk(src_ref, s0, r0, o):
            xb = src_ref[0, pl.ds(s0, ROWS), :]
            y = xb * lax.rsqrt(jnp.mean(xb * xb, axis=-1, keepdims=True) + NORM_EPS) * g_ref[...]
            sh = mod_ref[0, o:o + 1, :]
            sc = mod_ref[0, o + 1:o + 2, :]
            h_ref[pl.ds(r0, ROWS), :] = (y * (1.0 + sc) + sh).astype(bf16)

        def lat_body(i, carry):
            r0 = pl.multiple_of(i * ROWS, ROWS)
            norm_chunk(x_ref, r0, r0, 0)
            return carry

        lax.fori_loop(0, nlat, lat_body, 0)
        for j in range(nlat, nchunk):
            norm_chunk(c_ref, j * ROWS - T, j * ROWS, 2)

    PR = POST_ROWS
    QR = QUERY_ROWS

    def rope(scale):
        def post(z, r0):
            if scale != 1.0:
                z = z * scale
            cs = cos_ref[pl.ds(r0, z.shape[0]), :]
            sn = sin_ref[pl.ds(r0, z.shape[0]), :]
            parts = []
            for hh in range(TN // RET_HEAD_DIM):
                zh = z[:, hh * RET_HEAD_DIM:(hh + 1) * RET_HEAD_DIM]
                parts.append(zh * cs + pltpu.roll(zh, RET_HEAD_DIM // 2, 1) * sn)
            return jnp.concatenate(parts, axis=1)
        return post

    def shifted(r0, at_boundary):
        win = z_ref[pl.ds(r0, PR + 2 * PAD), :]
        prev = win[PAD - 1:PAD - 1 + PR]
        z = win[PAD:PAD + PR]
        nxt = win[PAD + 1:PAD + 1 + PR]
        if at_boundary:
            rid = r0 + lax.broadcasted_iota(jnp.int32, (PR, 1), 0)
            prev = jnp.where(rid == T, 0.0, prev)
            nxt = jnp.where(rid == T - 1, 0.0, nxt)
        mu0 = mu_ref[0:1, :]
        mu1 = mu_ref[1:2, :]
        return (1.0 - mu0 - mu1) * z + mu0 * prev + mu1 * nxt

    def lora_act(zs):
        lane = lax.broadcasted_iota(jnp.int32, (1, TN), 1)
        return jnp.where(lane < DECAY_LORA, jnp.tanh(zs),
                         jnp.where(lane < DECAY_LORA + ICLR_LORA, zs, jax.nn.sigmoid(zs)))

    def product(r0, rows=SR):
        return _dot(h_ref[pl.ds(r0, rows), :], w_ref[...])

    def direct(post, latent_only=False):
        rows, total = (QR, T) if latent_only and T % QR == 0 else (SR, L)

        def body(i, carry):
            r0 = pl.multiple_of(i * rows, rows)
            o_ref[0, pl.ds(r0, rows), :] = post(product(r0, rows), r0).astype(bf16)
            return carry
        lax.fori_loop(0, total // rows, body, 0)
        if total < L:
            o_ref[0, total:L, :] = jnp.zeros((L - total, TN), bf16)

    def via_buffer(act):
        def mm_body(i, carry):
            r0 = pl.multiple_of(i * SR, SR)
            z_ref[pl.ds(PAD + r0, SR), :] = product(r0)
            return carry
        lax.fori_loop(0, L // SR, mm_body, 0)

        def post(r0, at_boundary):
            o_ref[0, pl.ds(r0, PR), :] = act(shifted(r0, at_boundary)).astype(bf16)

        def post_body(i, carry):
            post(pl.multiple_of(i * PR, PR), False)
            return carry

        seam = T // PR
        lax.fori_loop(0, seam - 1, post_body, 0)
        post((seam - 1) * PR, True)
        post(seam * PR, True)
        lax.fori_loop(seam + 1, L // PR, post_body, 0)

    t = lambda c: c // TN
    pl.when(n < t(C_RETK))(lambda: direct(lambda z, r0: _sigmoid(z), latent_only=True))
    pl.when((n >= t(C_RETK)) & (n < t(C_RETV)))(lambda: direct(rope(RET_HEAD_DIM ** -0.5)))
    pl.when((n >= t(C_RETV)) & (n < t(C_RETQ)))(lambda: direct(lambda z, r0: z))
    pl.when((n >= t(C_RETQ)) & (n < t(C_RETG)))(lambda: direct(rope(1.0), latent_only=True))
    pl.when((n >= t(C_RETG)) & (n < t(C_RWK)))(lambda: direct(lambda z, r0: z * _sigmoid(z), latent_only=True))
    pl.when((n >= t(C_RWK)) & (n < t(C_LORA)))(lambda: via_buffer(lambda zs: zs))
    pl.when(n >= t(C_LORA))(lambda: via_buffer(lora_act))


def _inproj(x, ctx, modrows, g0, w_perm, mu_full, cosf, sinf):
    B, T, D = x.shape
    CT = ctx.shape[1]
    L = T + CT
    nt = IN_COLS // TN
    return pl.pallas_call(
        _inproj_kernel,
        grid=(B, nt),
        in_specs=[pl.BlockSpec((1, T, D), lambda b, n: (b, 0, 0)),
                  pl.BlockSpec((1, CT, D), lambda b, n: (b, 0, 0)),
                  pl.BlockSpec((1, 4, D), lambda b, n: (b, 0, 0)),
                  pl.BlockSpec((1, D), lambda b, n: (0, 0)),
                  pl.BlockSpec((D, TN), lambda b, n: (0, n)),
                  pl.BlockSpec((2, TN), lambda b, n: (0, n)),
                  pl.BlockSpec((L, RET_HEAD_DIM), lambda b, n: (0, 0)),
                  pl.BlockSpec((L, RET_HEAD_DIM), lambda b, n: (0, 0))],
        out_specs=pl.BlockSpec((1, L, TN), lambda b, n: (b, 0, n)),
        out_shape=jax.ShapeDtypeStruct((B, L, IN_COLS), bf16),
        scratch_shapes=[pltpu.VMEM((L, D), bf16), pltpu.VMEM((L + 16, TN), f32)],
        compiler_params=_params(("arbitrary", "arbitrary")),
        name="inproj",
    )(x, ctx, modrows, g0, w_perm, mu_full, cosf, sinf)


def _ret_kernel(lg_ref, q_ref, k_ref, v_ref, o_ref, r_ref, tab_ref, *, ctx):
    L = q_ref.shape[1]
    Cc = RET_CHUNK
    hd = RET_HEAD_DIM
    lat = L - ctx
    nc = ctx // Cc
    nl = lat // Cc
    combos = [(h, d) for h in range(RET_HEADS) for d in (0, 1)]
    INTRA, CROSS, TAIL, DECAY = 0, 1, 2, 3

    @pl.when(pl.program_id(0) == 0)
    def _():
        ii = lax.broadcasted_iota(jnp.int32, (Cc, Cc), 0).astype(f32)
        jj = lax.broadcasted_iota(jnp.int32, (Cc, Cc), 1).astype(f32)
        for idx, (h, d) in enumerate(combos):
            lg = lg_ref[d, h]
            if d == 0:
                diff = ii - jj
                cross = jnp.exp(lg * (ii + 1.0))
                tailw = jnp.exp(lg * (Cc - 1.0 - ii))
            else:
                diff = jj - ii
                cross = jnp.exp(lg * (Cc - ii))
                tailw = jnp.exp(lg * ii)
            tab_ref[idx, INTRA] = jnp.where(diff >= 0, jnp.exp(lg * jnp.maximum(diff, 0.0)), 0.0)
            tab_ref[idx, CROSS] = cross
            tab_ref[idx, TAIL] = tailw
            tab_ref[idx, DECAY] = jnp.exp(jnp.zeros((Cc, Cc), f32) + lg * Cc)

    r_ref[...] = jnp.zeros(r_ref.shape, f32)
    o_ref[...] = jnp.zeros(o_ref.shape, f32)

    def cols(h):
        return slice(h * hd, (h + 1) * hd)

    def update(idx, kc, vc):
        ks = (kc.astype(f32) * tab_ref[idx, TAIL]).astype(bf16)
        r_ref[idx] = r_ref[idx] * tab_ref[idx, DECAY] + _dot_tn(ks, vc)

    def ctx_step(s, carry):
        for idx, (h, d) in enumerate(combos):
            row0 = pl.multiple_of(lat + (s * Cc if d == 0 else (nc - 1 - s) * Cc), Cc)
            update(idx, k_ref[0, pl.ds(row0, Cc), cols(h)], v_ref[0, pl.ds(row0, Cc), cols(h)])
        return carry

    def lat_step(s, carry):
        t0s, qs, ks, vs = [], [], [], []
        for h, d in combos:
            t0 = pl.multiple_of(s * Cc if d == 0 else (nl - 1 - s) * Cc, Cc)
            row0 = t0
            t0s.append(t0)
            qs.append(q_ref[0, pl.ds(row0, Cc), cols(h)])
            ks.append(k_ref[0, pl.ds(row0, Cc), cols(h)])
            vs.append(v_ref[0, pl.ds(row0, Cc), cols(h)])
        n = len(combos)
        sc = [(_dot_nt(qs[i], ks[i]) * tab_ref[i, INTRA]).astype(bf16) for i in range(n)]
        oc = [_dot(qs[i], r_ref[i].astype(bf16)) * tab_ref[i, CROSS] for i in range(n)]
        oi = [_dot(sc[i], vs[i]) for i in range(n)]
        for i, (h, d) in enumerate(combos):
            o_ref[0, pl.ds(t0s[i], Cc), cols(h)] += oi[i] + oc[i]
        for i in range(n):
            update(i, ks[i], vs[i])
        return carry

    lax.fori_loop(0, nc, ctx_step, 0)
    lax.fori_loop(0, nl, lat_step, 0)


def _retention(lg, z, ctx):
    B, L, _ = z.shape
    T = L - ctx
    W = RET_WIDTH
    Cc = RET_CHUNK
    blk = lambda c0: pl.BlockSpec((1, L, W), lambda b: (b, 0, c0 // W))
    nchain = 2 * RET_HEADS
    return pl.pallas_call(
        functools.partial(_ret_kernel, ctx=ctx),
        grid=(B,),
        in_specs=[pl.BlockSpec(memory_space=pltpu.SMEM), blk(C_RETQ), blk(C_RETK), blk(C_RETV)],
        out_specs=pl.BlockSpec((1, T, W), lambda b: (b, 0, 0)),
        out_shape=jax.ShapeDtypeStruct((B, T, W), f32),
        scratch_shapes=[pltpu.VMEM((nchain, Cc, Cc), f32), pltpu.VMEM((nchain, 4, Cc, Cc), f32)],
        compiler_params=_params(("arbitrary",)),
        name="ret",
    )(lg, z, z, z)


def _rwkv_kernel(k_ref, v_ref, r_ref, lo_ref, w0_ref, a0_ref, w2_ref, a2_ref, kk_ref, ka_ref, rk_ref,
                 o_ref, bo_ref, s_ref, *, nc, nb, nsub):
    d = pl.program_id(1)
    s = pl.program_id(2)
    C = RWKV_CHUNK

    @pl.when(s == 0)
    def _():
        s_ref[...] = jnp.zeros(s_ref.shape, f32)

    def sub_chunk(j, carry):
        r0 = pl.multiple_of(jnp.where(d == 1, nsub - 1 - j, j) * C, C)
        rows = lambda ref: ref.at[:, pl.ds(r0, C), :]
        _rwkv_chunk(rows(k_ref), rows(v_ref), rows(r_ref), rows(lo_ref), w0_ref, a0_ref, w2_ref, a2_ref,
                    kk_ref, ka_ref, rk_ref, o_ref.at[:, :, pl.ds(r0, C), :], bo_ref.at[:, :, pl.ds(r0, C), :],
                    s_ref, rev=d == 1, emit=s >= nc, nb=nb)
        return carry

    lax.fori_loop(0, nsub, sub_chunk, 0)


def _rwkv_chunk(k_ref, v_ref, r_ref, lo_ref, w0_ref, a0_ref, w2_ref, a2_ref, kk_ref, ka_ref, rk_ref,
                o_ref, bo_ref, s_ref, *, rev, emit, nb):
    C = RWKV_CHUNK
    G = 2 * RWKV_HEAD_DIM

    ii = lax.broadcasted_iota(jnp.int32, (C, G), 0)
    lane = lax.broadcasted_iota(jnp.int32, (C, G), 1)
    jj = lane & (RWKV_HEAD_DIM - 1)
    head0 = lane < RWKV_HEAD_DIM
    dlt = jnp.where(rev, ii - jj, jj - ii)
    strict = dlt < 0
    incl = dlt <= 0
    eye = (ii == jj).astype(f32)
    gi = lax.broadcasted_iota(jnp.int32, (G, G), 0)
    gj = lax.broadcasted_iota(jnp.int32, (G, G), 1)
    blockdiag = (gi >= RWKV_HEAD_DIM) == (gj >= RWKV_HEAD_DIM)
    ones_bd = blockdiag.astype(bf16)
    tri = incl[:, 0:C].astype(bf16)
    same4 = (ii >> 2) == (jj >> 2)
    lag = [same4 & (dlt == -k) for k in (1, 2, 3)]
    offs = [((ii >> (lv + 1)) == (jj >> (lv + 1))) & ((ii >> lv) != (jj >> lv)) for lv in range(2, 6)]

    def block4_inverse(a):
        l4 = jnp.where(same4, a, 0.0)
        c1 = jnp.where(rev, pltpu.roll(l4, 1, 1), pltpu.roll(l4, G - 1, 1))
        c2 = jnp.where(rev, pltpu.roll(l4, 2, 1), pltpu.roll(l4, G - 2, 1))
        d1 = jnp.sum(jnp.where(lag[0], l4, 0.0), axis=0, keepdims=True)
        d2 = jnp.sum(jnp.where(lag[1], l4, 0.0), axis=0, keepdims=True)
        d1b = jnp.broadcast_to(d1, (8, G))
        d1n = jnp.where(rev, pltpu.roll(d1b, 1, 1), pltpu.roll(d1b, G - 1, 1))[0:1]
        two = c1 * d1
        three = two + c2 * d2 + c2 * (d1n * d1)
        return eye + l4 + jnp.where(lag[1], two, 0.0) + jnp.where(lag[2], three, 0.0)

    def segsum(x):
        return jnp.concatenate(
            [_bdot(x[:, p * G:(p + 1) * G], ones_bd) for p in range(RWKV_PAIRS)], axis=1)

    def stack(x):
        xb = x.astype(bf16)
        zero = jnp.zeros_like(xb)
        return jnp.concatenate([jnp.where(head0, xb, zero), jnp.where(head0, zero, xb)], axis=0)

    W = k_ref.shape[2]
    k_all = k_ref[...].astype(f32).reshape(nb * C, W)
    v_all = v_ref[...].astype(f32).reshape(nb * C, W)
    r_all = r_ref[...].astype(f32).reshape(nb * C, W)
    lo_all = lo_ref[:, :, 0:DECAY_LORA + ICLR_LORA].reshape(nb * C, DECAY_LORA + ICLR_LORA)
    ua = _dot(lo_all, jnp.concatenate([w2_ref[0], a2_ref[0]], axis=1))
    u_all = w0_ref[0] + ua[:, 0:W]
    softplus = jnp.maximum(-u_all, 0.0) + jnp.log1p(jnp.exp(-jnp.abs(u_all)))
    lw_all = -jnp.exp(-softplus - 0.5)
    a_all = jax.nn.sigmoid(a0_ref[0] + ua[:, W:2 * W])
    kkr = k_all * kk_ref[...]
    kd_all = k_all * (1.0 + (a_all - 1.0) * ka_ref[...])
    sums = segsum(jnp.concatenate([kkr * kkr, r_all * kd_all * rk_ref[...]], axis=0))
    kk_all = kkr * lax.rsqrt(sums[0:nb * C] + 1e-12)
    be_all = kk_all * a_all
    bonus_all = sums[nb * C:2 * nb * C] * v_all
    cum_all = _split_dot_left(tri, jnp.concatenate([lw_all[bb * C:(bb + 1) * C] for bb in range(nb)], axis=1))
    pre = []
    for bb in range(nb):
        rows = slice(bb * C, (bb + 1) * C)
        vx, rx, lw, kk, kd, be, bonus = (arr[rows] for arr in (v_all, r_all, lw_all, kk_all, kd_all, be_all,
                                                                bonus_all))
        cum = cum_all[:, bb * W:(bb + 1) * W]
        tot = jnp.where(rev, cum[0:1, :], cum[C - 1:C, :])
        gneg = jnp.exp(-cum)
        etot = jnp.exp(tot)
        gh = etot * gneg
        pre.append(dict(alb=-kk * jnp.exp(cum - lw), rb=rx * jnp.exp(cum), beb=be * gneg, kb=kd * gneg,
                        beh=be * gh, kh=kd * gh, etot=etot, v=vx, bonus=bonus))

    all_units = [(bb, p) for bb in range(nb) for p in range(RWKV_PAIRS)]
    sl = lambda p: slice(p * G, (p + 1) * G)
    part = lambda un, name: pre[un[0]][name][:, sl(un[1])]
    Om = {}
    for g0 in range(0, len(all_units), RWKV_UNITS_PER_GROUP):
        units = all_units[g0:g0 + RWKV_UNITS_PER_GROUP]
        S = {un: s_ref[un[0] * RWKV_PAIRS + un[1]] for un in units}
        Sb = {un: S[un].astype(bf16) for un in units}
        X = {un: part(un, "alb").astype(bf16) for un in units}
        Rb = {un: part(un, "rb").astype(bf16) for un in units}
        Ybs = {un: stack(part(un, "beb")) for un in units}
        Yks = {un: stack(part(un, "kb")) for un in units}
        Vb = {un: part(un, "v").astype(bf16) for un in units}
        Vs = {un: stack(part(un, "v")) for un in units}

        XR = {un: jnp.concatenate([X[un], Rb[un]], axis=0) for un in units}
        ABO = {un: _dot_nt(XR[un], jnp.concatenate([Ybs[un], Yks[un], Sb[un]], axis=0)) for un in units}
        APb = {un: ABO[un][:, 0:G] for un in units}
        APk = {un: ABO[un][:, G:2 * G] for un in units}
        BO = {un: ABO[un][:, 2 * G:3 * G] for un in units}
        Aab = {un: jnp.where(strict, APb[un][0:C], 0.0) for un in units}
        Aak = {un: jnp.where(strict, APk[un][0:C], 0.0).astype(bf16) for un in units}
        Pab = {un: jnp.where(incl, APb[un][C:2 * C], 0.0).astype(bf16) for un in units}
        Pak = {un: jnp.where(incl, APk[un][C:2 * C], 0.0).astype(bf16) for un in units}

        Tm = {un: block4_inverse(Aab[un]) for un in units}
        for off in offs:
            Xs = {un: _dot(jnp.where(off, Aab[un], 0.0).astype(bf16), stack(Tm[un])) for un in units}
            Tm = {un: Tm[un] + _dot(Tm[un].astype(bf16), stack(Xs[un])) for un in units}

        AV = {un: _dot(jnp.concatenate([Aak[un], Pak[un]], axis=0), Vs[un]) for un in units}
        Bm = {un: BO[un][0:C] + AV[un][0:C] for un in units}
        U = {un: _dot(Tm[un].astype(bf16), stack(Bm[un])) for un in units}
        for un in units:
            Om[un] = BO[un][C:2 * C] + AV[un][C:2 * C] + _dot(Pab[un], stack(U[un]))
        for un in units:
            bb, p = un
            upd = _dot_tn(jnp.concatenate([U[un].astype(bf16), Vb[un]], axis=0),
                          jnp.concatenate([part(un, "beh").astype(bf16), part(un, "kh").astype(bf16)], axis=0))
            s_ref[bb * RWKV_PAIRS + p] = jnp.where(blockdiag, S[un] * part(un, "etot") + upd, 0.0)

    @pl.when(emit)
    def _():
        for bb in range(nb):
            o_ref[0, bb] = jnp.concatenate([Om[bb, p] for p in range(RWKV_PAIRS)], axis=1)
            bo_ref[0, bb] = pre[bb]["bonus"]


def _split_dot_left(w, x):
    hi = x.astype(bf16)
    lo = (x - hi.astype(f32)).astype(bf16)
    return _dot(jnp.concatenate([w, w], axis=1), jnp.concatenate([hi, lo], axis=0))


def _rwkv(z, w0, a0, w2p, a2p, k_k, k_a, r_k, ctx):
    B, L, _ = z.shape
    T = L - ctx
    nsub = RWKV_CHUNKS_PER_STEP
    while ctx % (nsub * RWKV_CHUNK) or T % (nsub * RWKV_CHUNK):
        nsub //= 2
    C = nsub * RWKV_CHUNK
    nc, nl = ctx // C, T // C
    W = RWKV_WIDTH

    def chunk(d, s):
        fwd = jnp.where(s < nc, nl + s, s - nc)
        bwd = jnp.where(s < nc, nl + nc - 1 - s, nl - 1 - (s - nc))
        return jnp.where(d == 0, fwd, bwd)

    def ochunk(d, s):
        sl = jnp.maximum(s - nc, 0)
        return jnp.where(d == 0, sl, nl - 1 - sl)

    nb = RWKV_SAMPLES_PER_STEP if B % RWKV_SAMPLES_PER_STEP == 0 else 1
    zblk = lambda c0, w: pl.BlockSpec((nb, C, w), lambda b, d, s: (b, chunk(d, s), c0 // w))
    dpar = lambda r: pl.BlockSpec((1, r, W), lambda b, d, s: (d, 0, 0))
    par = pl.BlockSpec((1, W), lambda b, d, s: (0, 0))
    oblk = pl.BlockSpec((1, nb, C, W), lambda b, d, s: (d, b, ochunk(d, s), 0))
    G = 2 * RWKV_HEAD_DIM
    return pl.pallas_call(
        functools.partial(_rwkv_kernel, nc=nc, nb=nb, nsub=nsub),
        grid=(B // nb, 2, nc + nl),
        in_specs=[zblk(C_RWK, W), zblk(C_RWV, W), zblk(C_RWR, W), zblk(C_LORA, 256),
                  dpar(1), dpar(1), dpar(G), dpar(G), par, par, par],
        out_specs=[oblk, oblk],
        out_shape=[jax.ShapeDtypeStruct((2, B, T, W), f32), jax.ShapeDtypeStruct((2, B, T, W), f32)],
        scratch_shapes=[pltpu.VMEM((nb * RWKV_PAIRS, G, G), f32)],
        compiler_params=_params(("arbitrary", "arbitrary", "arbitrary")),
        name="rwkv",
    )(z, z, z, z, w0, a0, w2p, a2p, k_k, k_a, r_k)


def _merge_kernel(x_ref, ret_ref, rw0_ref, rw1_ref, b0_ref, b1_ref, gate_ref, retg_ref, gd_ref, mod_ref,
                  ng_ref, gn_ref, lng_ref, lnb_ref, g2_ref, wbr_ref, wbw_ref, wout_ref, wrh_ref, wrl_ref,
                  x1_ref, h2_ref, lt_ref):
    hd = RET_HEAD_DIM
    ret = ret_ref[0]
    parts = []
    for hh in range(RET_HEADS):
        xh = ret[:, hh * hd:(hh + 1) * hd]
        mu = jnp.mean(xh, axis=-1, keepdims=True)
        dv = xh - mu
        var = jnp.mean(dv * dv, axis=-1, keepdims=True)
        parts.append(dv * lax.rsqrt(var + RET_EPS))
    yr = retg_ref[0].astype(f32) * (jnp.concatenate(parts, axis=1) * gn_ref[...])
    y_ret = _bdot(yr, wbr_ref[...])
    W = RWKV_WIDTH
    gi = lax.broadcasted_iota(jnp.int32, (W, W), 0)
    gj = lax.broadcasted_iota(jnp.int32, (W, W), 1)
    ones_bd = ((gi >> 6) == (gj >> 6)).astype(bf16)
    o = rw0_ref[0, 0] + rw1_ref[0, 0]
    mu = _split_dot(o, ones_bd) * (1.0 / RWKV_HEAD_DIM)
    dv = o - mu
    var = _bdot(dv * dv, ones_bd) * (1.0 / RWKV_HEAD_DIM)
    yw = dv * lax.rsqrt(var + RWKV_EPS) * lng_ref[...] + lnb_ref[...]
    gate = _dot(gd_ref[0], g2_ref[...])
    yw = (yw + b0_ref[0, 0] + b1_ref[0, 0]) * gate
    y_rw = _bdot(yw, wbw_ref[...])
    D = y_ret.shape[1]
    g = gate_ref[0].astype(f32)
    m = g[:, :D] * y_ret + g[:, D:] * y_rw
    y = _bdot(m, wout_ref[...])

    def rms(v, gg):
        return v * lax.rsqrt(jnp.mean(v * v, axis=-1, keepdims=True) + NORM_EPS) * gg

    x1 = x_ref[0] + mod_ref[0, 0:1, :] * rms(y, ng_ref[0:1, :])
    x1_ref[0] = x1
    h2 = rms(x1, ng_ref[1:2, :]) * (1.0 + mod_ref[0, 2:3, :]) + mod_ref[0, 1:2, :]
    h2b = h2.astype(bf16)
    h2_ref[0] = h2b
    h2l = (h2 - h2b.astype(f32)).astype(bf16)
    lgt = _dot(jnp.concatenate([h2b, h2l, h2b], axis=1),
               jnp.concatenate([wrh_ref[...], wrh_ref[...], wrl_ref[...]], axis=0))
    lt_ref[0] = lgt.T[0:N_EXPERTS, :]


def _merge(x, ret_o, rw_o, bonus, z, mod2, ng12, gn, lng, lnb, g2, wbr, wbw, wout, wrh, wrl):
    B, T, D = x.shape
    tm = MERGE_ROWS if T % MERGE_ROWS == 0 else ROWS
    W = RWKV_WIDTH
    row = lambda w: pl.BlockSpec((1, tm, w), lambda b, i: (b, i, 0))
    dblk = lambda dd: pl.BlockSpec((1, 1, tm, W), lambda b, i: (dd, b, i, 0))
    zblk = lambda c0, w: pl.BlockSpec((1, tm, w), lambda b, i: (b, i, c0 // w))
    full = lambda a: pl.BlockSpec(a.shape, lambda b, i: (0,) * a.ndim)
    return pl.pallas_call(
        _merge_kernel,
        grid=(B, T // tm),
        in_specs=[row(D), row(W), dblk(0), dblk(1), dblk(0), dblk(1),
                  zblk(C_MERGE, 2 * D), zblk(C_RETG, W), zblk(C_LORA + 128, 128),
                  pl.BlockSpec((1, 3, D), lambda b, i: (b, 0, 0)),
                  full(ng12), full(gn), full(lng), full(lnb), full(g2), full(wbr), full(wbw), full(wout),
                  full(wrh), full(wrl)],
        out_specs=[row(D), row(D), pl.BlockSpec((1, N_EXPERTS, tm), lambda b, i: (b, 0, i))],
        out_shape=[jax.ShapeDtypeStruct((B, T, D), f32), jax.ShapeDtypeStruct((B, T, D), bf16),
                   jax.ShapeDtypeStruct((B, N_EXPERTS, T), f32)],
        compiler_params=_params(("arbitrary", "arbitrary")),
        name="merge",
    )(x, ret_o, rw_o, rw_o, bonus, bonus, z, z, z, mod2, ng12, gn, lng, lnb, g2, wbr, wbw, wout, wrh, wrl)


def _route_kernel(lt_ref, slot_ref, rt_ref, cnt_ref, slotf_ref, gate_ref, *, cap):
    B, NE, T = lt_ref.shape
    lg = lt_ref[...]
    mx = jnp.max(lg, axis=1, keepdims=True)
    ex = jnp.exp(lg - mx)
    aff = (ex / jnp.sum(ex, axis=1, keepdims=True)).reshape(B * NE, T)
    E = B * NE

    def count_ge(cand):
        return jnp.sum((aff >= cand).astype(f32), axis=1, keepdims=True)

    def exp_step(_, kk):
        k_lo, k_hi = kk
        km = jnp.floor((k_lo + k_hi) * 0.5)
        ok = count_ge(jnp.exp2(-km)) >= cap
        return jnp.where(ok, k_lo, km), jnp.where(ok, km, k_hi)

    k_lo, k_hi = lax.fori_loop(0, ROUTE_EXP_STEPS, exp_step,
                               (jnp.full((E, 1), -1.0, f32), jnp.full((E, 1), ROUTE_MAX_EXP, f32)))
    lo0 = jnp.where(k_hi >= ROUTE_MAX_EXP, 0.0, jnp.exp2(-k_hi))
    hi0 = jnp.exp2(-k_lo)

    def val_step(_, lh):
        lo, hi = lh
        mid = (lo + hi) * 0.5
        ok = count_ge(mid) >= cap
        return jnp.where(ok, mid, lo), jnp.where(ok, hi, mid)

    lo, hi = lax.fori_loop(0, ROUTE_VAL_STEPS, val_step, (lo0, hi0))
    gt = aff >= hi
    eq = (aff >= lo) & (aff < hi)
    need = cap - jnp.sum(gt.astype(f32), axis=1, keepdims=True)
    tri = (lax.broadcasted_iota(jnp.int32, (T, T), 0) < lax.broadcasted_iota(jnp.int32, (T, T), 1)).astype(bf16)
    eq_before = _dot(eq.astype(bf16), tri)
    sel = gt | (eq & (eq_before < need))
    slot = _dot(sel.astype(bf16), tri)
    slot_f = jnp.where(sel, slot, -1.0)
    slot_ref[...] = slot_f.astype(jnp.int32).reshape(B, NE, T)
    before = (lax.broadcasted_iota(jnp.int32, (T, 128), 0)
              < lax.broadcasted_iota(jnp.int32, (T, 128), 1) * GATHER_TILE).astype(bf16)
    cnt_ref[...] = _dot(sel.astype(bf16), before).astype(jnp.int32).reshape(B, NE, 128)
    slotf_ref[...] = slot_f.reshape(B, NE, T)
    gate_ref[...] = jnp.where(sel, aff, 0.0).reshape(B, NE, T)

    def transpose_sample(b, carry):
        packed = jnp.concatenate([slotf_ref[b], gate_ref[b], jnp.zeros((128 - 2 * NE, T), f32)], axis=0)
        rt_ref[b] = packed.T.astype(bf16)
        return carry

    lax.fori_loop(0, B, transpose_sample, 0)


def _route(lt, cap):
    B, E, T = lt.shape
    assert cap <= 256 and T % GATHER_TILE == 0 and T // GATHER_TILE < 128
    return pl.pallas_call(
        functools.partial(_route_kernel, cap=cap),
        grid=(1,),
        in_specs=[pl.BlockSpec((B, E, T), lambda i: (0, 0, 0))],
        out_specs=[pl.BlockSpec((B, E, T), lambda i: (0, 0, 0)), pl.BlockSpec((B, T, 128), lambda i: (0, 0, 0)),
                   pl.BlockSpec((B, E, 128), lambda i: (0, 0, 0))],
        out_shape=[jax.ShapeDtypeStruct((B, E, T), jnp.int32), jax.ShapeDtypeStruct((B, T, 128), bf16),
                   jax.ShapeDtypeStruct((B, E, 128), jnp.int32)],
        scratch_shapes=[pltpu.VMEM((B, E, T), f32), pltpu.VMEM((B, E, T), f32)],
        compiler_params=_params(("arbitrary",)),
        name="route",
    )(lt)


def _gather_kernel(cnt_ref, slot_ref, h_ref, o_ref, acc_ref, *, cap):
    b = pl.program_id(0)
    E = slot_ref.shape[1]
    T, D = h_ref.shape[1], h_ref.shape[2]
    GT, GW, GA = GATHER_TILE, GATHER_WINDOW, GATHER_ALIGN
    nt = T // GT
    base = [[pl.multiple_of(lax.shift_left(lax.shift_right_logical(cnt_ref[b, e, j], GATHER_ALIGN_LOG2),
                                           GATHER_ALIGN_LOG2), GA) for j in range(nt)] for e in range(E)]
    fits = None
    for e in range(E):
        for j in range(nt):
            ok = cnt_ref[b, e, j + 1] - base[e][j] <= GW
            fits = ok if fits is None else fits & ok

    @pl.when(fits)
    def _():
        acc_ref[...] = jnp.zeros(acc_ref.shape, bf16)
        win = lax.broadcasted_iota(jnp.int32, (GW, GT), 0)
        for j in range(nt):
            onehot = jnp.concatenate([(slot_ref[0, e, j:j + 1, :] == base[e][j] + win).astype(bf16)
                                      for e in range(E)], axis=0)
            res = _dot(onehot, h_ref[0, j * GT:(j + 1) * GT, :]).astype(bf16)
            for e in range(E):
                acc_ref[e, pl.ds(base[e][j], GW), :] += res[e * GW:(e + 1) * GW]
        for e in range(E):
            o_ref[0, e] = acc_ref[e, 0:cap, :]

    @pl.when(jnp.logical_not(fits))
    def _():
        rows = lax.broadcasted_iota(jnp.int32, (cap, GT), 0)

        def per_expert(e, carry):
            acc = jnp.zeros((cap, D), f32)
            for j in range(nt):
                onehot = (slot_ref[0, e, j:j + 1, :] == rows).astype(bf16)
                acc = acc + _dot(onehot, h_ref[0, j * GT:(j + 1) * GT, :])
            o_ref[0, e] = acc.astype(bf16)
            return carry

        lax.fori_loop(0, E, per_expert, 0)


def _gather(cnt, slot4, h2, cap):
    B, T, D = h2.shape
    E = slot4.shape[1]
    grid_spec = pltpu.PrefetchScalarGridSpec(
        num_scalar_prefetch=1,
        grid=(B,),
        in_specs=[pl.BlockSpec((1, E, T // GATHER_TILE, GATHER_TILE), lambda b, c: (b, 0, 0, 0)),
                  pl.BlockSpec((1, T, D), lambda b, c: (b, 0, 0))],
        out_specs=pl.BlockSpec((1, E, cap, D), lambda b, c: (b, 0, 0, 0)),
        scratch_shapes=[pltpu.VMEM((E, cap + GATHER_WINDOW, D), bf16)])
    return pl.pallas_call(
        functools.partial(_gather_kernel, cap=cap),
        grid_spec=grid_spec,
        out_shape=jax.ShapeDtypeStruct((B, E, cap, D), bf16),
        compiler_params=_params(("arbitrary",)),
        name="gather",
    )(cnt, slot4, h2)


def _ffn_kernel(x_ref, wg_ref, wu_ref, wd_ref, o_ref, wgu_ref, wdb_ref, *, cap):
    F = wg_ref.shape[2]

    @pl.when(pl.program_id(1) == 0)
    def _():
        wgu_ref[:, 0:F] = wg_ref[0].astype(bf16)
        wgu_ref[:, F:2 * F] = wu_ref[0].astype(bf16)
        wdb_ref[...] = wd_ref[0].astype(bf16)

    nb = x_ref.shape[0]
    xg = jnp.concatenate([x_ref[bb, 0] for bb in range(nb)], axis=0)
    hgu = _dot(xg, wgu_ref[...])
    hg = hgu[:, 0:F]
    hu = hgu[:, F:2 * F]
    hid = (hg * _sigmoid(hg) * hu).astype(bf16)
    out = _dot(hid, wdb_ref[...]).astype(bf16)
    for bb in range(nb):
        o_ref[bb, 0] = out[bb * cap:(bb + 1) * cap]


def _ffn(xg, wg, wu, wd, cap):
    B, E, _, D = xg.shape
    F = wg.shape[2]
    nb = FFN_SAMPLES_PER_STEP if B % FFN_SAMPLES_PER_STEP == 0 else 1
    return pl.pallas_call(
        functools.partial(_ffn_kernel, cap=cap),
        grid=(E, B // nb),
        in_specs=[pl.BlockSpec((nb, 1, cap, D), lambda e, b: (b, e, 0, 0)),
                  pl.BlockSpec((1, D, F), lambda e, b: (e, 0, 0)),
                  pl.BlockSpec((1, D, F), lambda e, b: (e, 0, 0)),
                  pl.BlockSpec((1, F, D), lambda e, b: (e, 0, 0))],
        out_specs=pl.BlockSpec((nb, 1, cap, D), lambda e, b: (b, e, 0, 0)),
        out_shape=jax.ShapeDtypeStruct((B, E, cap, D), bf16),
        scratch_shapes=[pltpu.VMEM((D, 2 * F), bf16), pltpu.VMEM((F, D), bf16)],
        compiler_params=_params(("arbitrary", "arbitrary")),
        name="ffn",
    )(xg, wg, wu, wd)


def _combine_kernel(cnt_ref, rt_ref, eo_ref, x1_ref, mod_ref, ng_ref, o_ref, y_ref, *, cap):
    b = pl.program_id(0)
    i = pl.program_id(1)
    rt = rt_ref[0].astype(f32)
    tm = rt.shape[0]
    E = eo_ref.shape[1]
    W = min(SCATTER_WINDOW, cap)
    GS = SCATTER_GROUP
    ST = GATHER_TILE
    for st in range(tm // ST):
        t = i * (tm // ST) + st
        r = rt[st * ST:(st + 1) * ST]
        base, fits = [], None
        for e in range(E):
            lo = cnt_ref[b, e, t]
            be = jnp.minimum(lax.shift_left(lax.shift_right_logical(lo, GATHER_ALIGN_LOG2), GATHER_ALIGN_LOG2),
                             cap - W)
            base.append(pl.multiple_of(be, GATHER_ALIGN))
            ok = cnt_ref[b, e, t + 1] - be <= W
            fits = ok if fits is None else fits & ok

        def scatter_matrix(e, first, width, r=r):
            cols = (first + lax.broadcasted_iota(jnp.int32, (ST, width), 1)).astype(f32)
            return jnp.where(r[:, e:e + 1] == cols, r[:, E + e:E + e + 1], 0.0).astype(bf16)

        @pl.when(fits)
        def _():
            y = jnp.zeros((ST, y_ref.shape[1]), f32)
            for g in range(0, E, GS):
                p = jnp.concatenate([scatter_matrix(e, base[e], W) for e in range(g, g + GS)], axis=1)
                rows = jnp.concatenate([eo_ref[0, e, pl.ds(base[e], W), :] for e in range(g, g + GS)], axis=0)
                y = y + _dot(p, rows)
            y_ref[st * ST:(st + 1) * ST, :] = y

        @pl.when(jnp.logical_not(fits))
        def _():
            y = jnp.zeros((ST, y_ref.shape[1]), f32)
            for e in range(E):
                y = y + _dot(scatter_matrix(e, 0, cap), eo_ref[0, e])
            y_ref[st * ST:(st + 1) * ST, :] = y

    y = y_ref[...]
    yn = y * lax.rsqrt(jnp.mean(y * y, axis=-1, keepdims=True) + NORM_EPS) * ng_ref[...]
    o_ref[0] = x1_ref[0] + mod_ref[0] * yn


def _combine(cnt, rt, eo, x1, g2mod, ng3, cap):
    B, T, D = x1.shape
    E = eo.shape[1]
    tm = 512 if T % 512 == 0 else T
    assert E % SCATTER_GROUP == 0 and tm % GATHER_TILE == 0
    grid_spec = pltpu.PrefetchScalarGridSpec(
        num_scalar_prefetch=1,
        grid=(B, T // tm),
        in_specs=[pl.BlockSpec((1, tm, 128), lambda b, i, c: (b, i, 0)),
                  pl.BlockSpec((1, E, cap, D), lambda b, i, c: (b, 0, 0, 0)),
                  pl.BlockSpec((1, tm, D), lambda b, i, c: (b, i, 0)),
                  pl.BlockSpec((1, 1, D), lambda b, i, c: (b, 0, 0)),
                  pl.BlockSpec((1, D), lambda b, i, c: (0, 0))],
        out_specs=pl.BlockSpec((1, tm, D), lambda b, i, c: (b, i, 0)),
        scratch_shapes=[pltpu.VMEM((tm, D), f32)])
    return pl.pallas_call(
        functools.partial(_combine_kernel, cap=cap),
        grid_spec=grid_spec,
        out_shape=jax.ShapeDtypeStruct((B, T, D), f32),
        compiler_params=_params(("arbitrary", "arbitrary")),
        name="combine",
    )(cnt, rt, eo, x1, g2mod, ng3)


def _permute_columns(w):
    sk, sv, rk, rv, wd, ad = 0, 512, 1024, 1536, 2048, 2112
    q0 = 2176
    rq, rg, rr, gd, mg = q0, q0 + 512, q0 + 1024, q0 + 1536, q0 + 1664
    order = [(mg, 2048), (sk, 512), (sv, 512), (rq, 512), (rg, 512), (rk, 512), (rv, 512), (rr, 512),
             (wd, 64), (ad, 64), (gd, 128)]
    parts = [w[:, a:a + n] for a, n in order]
    parts.append(jnp.zeros((w.shape[0], IN_COLS - USED_COLS), w.dtype))
    return jnp.concatenate(parts, axis=1)


def _rope_tables(T, CT):
    t = jnp.arange(T)
    nfreq = RET_HEAD_DIM // 4
    inv = ROPE_BASE ** (-jnp.arange(nfreq, dtype=f32) / nfreq)
    ang = jnp.concatenate([(t // GRID_W).astype(f32)[:, None] * inv,
                           (t % GRID_W).astype(f32)[:, None] * inv], axis=-1)
    cos, sin = jnp.cos(ang), jnp.sin(ang)
    cosf = jnp.concatenate([cos, cos], axis=1)
    sinf = jnp.concatenate([-sin, sin], axis=1)
    return (jnp.concatenate([cosf, jnp.ones((CT, RET_HEAD_DIM), f32)], axis=0),
            jnp.concatenate([sinf, jnp.zeros((CT, RET_HEAD_DIM), f32)], axis=0))


def kernel(x, c, ctx, c_ctx, w_mod, b_mod, norm_g, w_in, ret_log_decay, ret_gn_g, rwkv_mu, rwkv_k_k, rwkv_k_a,
           rwkv_r_k, rwkv_w0, rwkv_w2, rwkv_a0, rwkv_a2, rwkv_g2, rwkv_ln_g, rwkv_ln_b, w_br_ret, w_br_rwkv,
           w_out, w_router, w_gate, w_up, w_down):
    B, T, D = x.shape
    CT = ctx.shape[1]
    assert w_mod.shape[0] == 1 and D == D_MODEL
    assert CT % ROWS == 0 and T % ROWS == 0 and T % GRID_W == 0
    cap = CAPACITY_FACTOR * T // N_EXPERTS
    assert cap % 8 == 0

    mrows = -(-(B + 1) // 8) * 8
    cc = jnp.zeros((mrows, D), f32).at[:B].set(c).at[B].set(c_ctx)
    mod = _modulation(cc, w_mod[0], b_mod[0])
    lat = mod[:B].reshape(B, N_MOD, D)
    cm = jnp.broadcast_to(mod[B].reshape(1, N_MOD, D), (B, N_MOD, D))
    modrows = jnp.concatenate([lat[:, 0:2], cm[:, 0:2]], axis=1)

    w_perm = _permute_columns(w_in[0]).astype(bf16)
    mu = rwkv_mu[0]
    ss = 2 * RWKV_WIDTH + DECAY_LORA + ICLR_LORA
    mu_full = jnp.zeros((2, IN_COLS), f32)
    mu_full = mu_full.at[:, C_RWK:C_RWK + 1024].set(mu[:, 0:1024])
    mu_full = mu_full.at[:, C_RWR:C_RWR + 512].set(mu[:, ss:ss + 512])
    mu_full = mu_full.at[:, C_LORA:C_LORA + 128].set(mu[:, 1024:ss])
    mu_full = mu_full.at[:, C_LORA + 128:USED_COLS].set(mu[:, ss + 512:])
    cosf, sinf = _rope_tables(T, CT)
    z = _inproj(x, ctx, modrows, norm_g[0, 0:1], w_perm, mu_full, cosf, sinf)

    lg = -jnp.exp(ret_log_decay[0].astype(f32))
    ret_o = _retention(lg, z, CT)

    G = 2 * RWKV_HEAD_DIM
    w2p = jnp.zeros((2, G, RWKV_WIDTH), f32).at[:, :DECAY_LORA].set(rwkv_w2[0]).astype(bf16)
    a2p = jnp.zeros((2, G, RWKV_WIDTH), f32).at[:, DECAY_LORA:].set(rwkv_a2[0]).astype(bf16)
    rw_o, bonus = _rwkv(z, rwkv_w0[0][:, None, :], rwkv_a0[0][:, None, :], w2p, a2p,
                        rwkv_k_k[0][None], rwkv_k_a[0][None], rwkv_r_k[0][None], CT)

    mod2 = jnp.stack([lat[:, 2], lat[:, 3], lat[:, 4]], axis=1)
    wr_pad = jnp.zeros((D, 128), f32).at[:, :N_EXPERTS].set(w_router[0])
    wr_hi = wr_pad.astype(bf16)
    wr_lo = (wr_pad - wr_hi.astype(f32)).astype(bf16)
    x1, h2, lt = _merge(x, ret_o, rw_o, bonus, z, mod2, norm_g[0, 1:3], ret_gn_g[0][None], rwkv_ln_g[0][None],
                        rwkv_ln_b[0][None], rwkv_g2[0].astype(bf16), w_br_ret[0].astype(bf16),
                        w_br_rwkv[0].astype(bf16), w_out[0].astype(bf16), wr_hi, wr_lo)

    slot, rt, cnt = _route(lt, cap)
    cnt = cnt[:, :, :T // GATHER_TILE + 1]
    xg = _gather(cnt, slot.reshape(B, N_EXPERTS, T // GATHER_TILE, GATHER_TILE), h2, cap)
    eo = _ffn(xg, w_gate[0], w_up[0], w_down[0], cap)
    return _combine(cnt, rt, eo, x1, lat[:, 5:6], norm_g[0, 3:4], cap)
```

```python
import functools

import jax
import jax.numpy as jnp
from jax import lax
from jax.experimental import pallas as pl
from jax.experimental.pallas import tpu as pltpu

f32 = jnp.float32
bf16 = jnp.bfloat16

D_MODEL = 1024
GRID_W = 64
RET_HEAD_DIM = 128
RET_WIDTH = 512
RET_HEADS = 4
RET_CHUNK = 128
RET_EPS = 1e-5
ROPE_BASE = 10000.0
RWKV_HEAD_DIM = 64
RWKV_WIDTH = 512
RWKV_PAIRS = 4
RWKV_CHUNK = 64
RWKV_CHUNKS_PER_STEP = 4
RWKV_UNITS_PER_GROUP = 16
RWKV_SAMPLES_PER_STEP = 8
DECAY_LORA = 64
ICLR_LORA = 64
GATE_LORA = 128
RWKV_EPS = 64e-5
N_EXPERTS = 16
EXPERT_FF = 1024
CAPACITY_FACTOR = 2
N_MOD = 6
NORM_EPS = 1e-6

C_MERGE, C_RETK, C_RETV, C_RETQ, C_RETG = 0, 2048, 2560, 3072, 3584
C_RWK, C_RWV, C_RWR, C_LORA = 4096, 4608, 5120, 5632
USED_COLS = 5888
TN = 512
IN_COLS = -(-USED_COLS // TN) * TN
ROWS = 256
STEP_ROWS = 768
POST_ROWS = 128
QUERY_ROWS = 1024
MERGE_ROWS = 512

ROUTE_MAX_EXP = 126.0
ROUTE_EXP_STEPS = 8
ROUTE_VAL_STEPS = 26

VMEM_LIMIT = 56 * 1024 * 1024
FFN_SAMPLES_PER_STEP = 4
GATHER_TILE = 256
GATHER_ALIGN_LOG2 = 4
GATHER_ALIGN = 1 << GATHER_ALIGN_LOG2
GATHER_WINDOW = 80
SCATTER_WINDOW = 64
SCATTER_GROUP = 4


def _dot(a, b):
    return jnp.dot(a, b, preferred_element_type=f32)


def _dot_nt(a, b):
    return lax.dot_general(a, b, (((1,), (1,)), ((), ())), preferred_element_type=f32)


def _dot_tn(a, b):
    return lax.dot_general(a, b, (((0,), (0,)), ((), ())), preferred_element_type=f32)


def _bdot(a, b):
    return _dot(a.astype(bf16), b.astype(bf16))


def _split_dot(x, w):
    hi = x.astype(bf16)
    lo = (x - hi.astype(f32)).astype(bf16)
    return _dot(jnp.concatenate([hi, lo], axis=1), jnp.concatenate([w, w], axis=0))


def _sigmoid(x):
    return 0.5 * jnp.tanh(0.5 * x) + 0.5


def _params(sem, limit=VMEM_LIMIT):
    return pltpu.CompilerParams(dimension_semantics=sem, vmem_limit_bytes=limit)


def _mod_kernel(c_ref, w_ref, b_ref, o_ref):
    c = c_ref[...]
    s = c * jax.nn.sigmoid(c)
    o_ref[...] = _bdot(s, w_ref[...]) + b_ref[...]


def _modulation(cc, w_mod, b_mod):
    m, d = cc.shape
    n = w_mod.shape[1]
    tn = 512
    return pl.pallas_call(
        _mod_kernel,
        grid=(n // tn,),
        in_specs=[pl.BlockSpec((m, d), lambda j: (0, 0)),
                  pl.BlockSpec((d, tn), lambda j: (0, j)),
                  pl.BlockSpec((1, tn), lambda j: (0, j))],
        out_specs=pl.BlockSpec((m, tn), lambda j: (0, j)),
        out_shape=jax.ShapeDtypeStruct((m, n), f32),
        compiler_params=_params(("arbitrary",)),
        name="mod",
    )(cc, w_mod, b_mod.reshape(1, n))


def _inproj_kernel(x_ref, c_ref, mod_ref, g_ref, w_ref, mu_ref, cos_ref, sin_ref, o_ref, h_ref, z_ref):
    n = pl.program_id(1)
    T = x_ref.shape[1]
    L = T + c_ref.shape[1]
    nlat = T // ROWS
    nchunk = L // ROWS
    PAD = 8

    SR = STEP_ROWS if L % STEP_ROWS == 0 else ROWS

    @pl.when(n == 0)
    def _():
        z_ref[0:PAD, :] = jnp.zeros((PAD, TN), f32)
        z_ref[PAD + L:PAD + L + PAD, :] = jnp.zeros((PAD, TN), f32)

        def norm_chunk(src_ref, s0, r0, o):
            xb = src_ref[0, pl.ds(s0, ROWS), :]
            y = xb * lax.rsqrt(jnp.mean(xb * xb, axis=-1, keepdims=True) + NORM_EPS) * g_ref[...]
            sh = mod_ref[0, o:o + 1, :]
            sc = mod_ref[0, o + 1:o + 2, :]
            h_ref[pl.ds(r0, ROWS), :] = (y * (1.0 + sc) + sh).astype(bf16)

        def lat_body(i, carry):
            r0 = pl.multiple_of(i * ROWS, ROWS)
            norm_chunk(x_ref, r0, r0, 0)
            return carry

        lax.fori_loop(0, nlat, lat_body, 0)
        for j in range(nlat, nchunk):
            norm_chunk(c_ref, j * ROWS - T, j * ROWS, 2)

    PR = POST_ROWS
    QR = QUERY_ROWS

    def rope(scale):
        def post(z, r0):
            if scale != 1.0:
                z = z * scale
            cs = cos_ref[pl.ds(r0, z.shape[0]), :]
            sn = sin_ref[pl.ds(r0, z.shape[0]), :]
            parts = []
            for hh in range(TN // RET_HEAD_DIM):
                zh = z[:, hh * RET_HEAD_DIM:(hh + 1) * RET_HEAD_DIM]
                parts.append(zh * cs + pltpu.roll(zh, RET_HEAD_DIM // 2, 1) * sn)
            return jnp.concatenate(parts, axis=1)
        return post

    def shifted(r0, at_boundary):
        win = z_ref[pl.ds(r0, PR + 2 * PAD), :]
        prev = win[PAD - 1:PAD - 1 + PR]
        z = win[PAD:PAD + PR]
        nxt = win[PAD + 1:PAD + 1 + PR]
        if at_boundary:
            rid = r0 + lax.broadcasted_iota(jnp.int32, (PR, 1), 0)
            prev = jnp.where(rid == T, 0.0, prev)
            nxt = jnp.where(rid == T - 1, 0.0, nxt)
        mu0 = mu_ref[0:1, :]
        mu1 = mu_ref[1:2, :]
        return (1.0 - mu0 - mu1) * z + mu0 * prev + mu1 * nxt

    def lora_act(zs):
        lane = lax.broadcasted_iota(jnp.int32, (1, TN), 1)
        return jnp.where(lane < DECAY_LORA, jnp.tanh(zs),
                         jnp.where(lane < DECAY_LORA + ICLR_LORA, zs, jax.nn.sigmoid(zs)))

    def product(r0, rows=SR):
        return _dot(h_ref[pl.ds(r0, rows), :], w_ref[...])

    def direct(post, latent_only=False):
        rows, total = (QR, T) if latent_only and T % QR == 0 else (SR, L)

        def body(i, carry):
            r0 = pl.multiple_of(i * rows, rows)
            o_ref[0, pl.ds(r0, rows), :] = post(product(r0, rows), r0).astype(bf16)
            return carry
        lax.fori_loop(0, total // rows, body, 0)
        if total < L:
            o_ref[0, total:L, :] = jnp.zeros((L - total, TN), bf16)

    def via_buffer(act):
        def mm_body(i, carry):
            r0 = pl.multiple_of(i * SR, SR)
            z_ref[pl.ds(PAD + r0, SR), :] = product(r0)
            return carry
        lax.fori_loop(0, L // SR, mm_body, 0)

        def post(r0, at_boundary):
            o_ref[0, pl.ds(r0, PR), :] = act(shifted(r0, at_boundary)).astype(bf16)

        def post_body(i, carry):
            post(pl.multiple_of(i * PR, PR), False)
            return carry

        seam = T // PR
        lax.fori_loop(0, seam - 1, post_body, 0)
        post((seam - 1) * PR, True)
        post(seam * PR, True)
        lax.fori_loop(seam + 1, L // PR, post_body, 0)

    t = lambda c: c // TN
    pl.when(n < t(C_RETK))(lambda: direct(lambda z, r0: _sigmoid(z), latent_only=True))
    pl.when((n >= t(C_RETK)) & (n < t(C_RETV)))(lambda: direct(rope(RET_HEAD_DIM ** -0.5)))
    pl.when((n >= t(C_RETV)) & (n < t(C_RETQ)))(lambda: direct(lambda z, r0: z))
    pl.when((n >= t(C_RETQ)) & (n < t(C_RETG)))(lambda: direct(rope(1.0), latent_only=True))
    pl.when((n >= t(C_RETG)) & (n < t(C_RWK)))(lambda: direct(lambda z, r0: z * _sigmoid(z), latent_only=True))
    pl.when((n >= t(C_RWK)) & (n < t(C_LORA)))(lambda: via_buffer(lambda zs: zs))
    pl.when(n >= t(C_LORA))(lambda: via_buffer(lora_act))


def _inproj(x, ctx, modrows, g0, w_perm, mu_full, cosf, sinf):
    B, T, D = x.shape
    CT = ctx.shape[1]
    L = T + CT
    nt = IN_COLS // TN
    return pl.pallas_call(
        _inproj_kernel,
        grid=(B, nt),
        in_specs=[pl.BlockSpec((1, T, D), lambda b, n: (b, 0, 0)),
                  pl.BlockSpec((1, CT, D), lambda b, n: (b, 0, 0)),
                  pl.BlockSpec((1, 4, D), lambda b, n: (b, 0, 0)),
                  pl.BlockSpec((1, D), lambda b, n: (0, 0)),
                  pl.BlockSpec((D, TN), lambda b, n: (0, n)),
                  pl.BlockSpec((2, TN), lambda b, n: (0, n)),
                  pl.BlockSpec((L, RET_HEAD_DIM), lambda b, n: (0, 0)),
                  pl.BlockSpec((L, RET_HEAD_DIM), lambda b, n: (0, 0))],
        out_specs=pl.BlockSpec((1, L, TN), lambda b, n: (b, 0, n)),
        out_shape=jax.ShapeDtypeStruct((B, L, IN_COLS), bf16),
        scratch_shapes=[pltpu.VMEM((L, D), bf16), pltpu.VMEM((L + 16, TN), f32)],
        compiler_params=_params(("arbitrary", "arbitrary")),
        name="inproj",
    )(x, ctx, modrows, g0, w_perm, mu_full, cosf, sinf)


def _ret_kernel(lg_ref, q_ref, k_ref, v_ref, o_ref, r_ref, tab_ref, *, ctx):
    L = q_ref.shape[1]
    Cc = RET_CHUNK
    hd = RET_HEAD_DIM
    lat = L - ctx
    nc = ctx // Cc
    nl = lat // Cc
    combos = [(h, d) for h in range(RET_HEADS) for d in (0, 1)]
    INTRA, CROSS, TAIL, DECAY = 0, 1, 2, 3

    @pl.when(pl.program_id(0) == 0)
    def _():
        ii = lax.broadcasted_iota(jnp.int32, (Cc, Cc), 0).astype(f32)
        jj = lax.broadcasted_iota(jnp.int32, (Cc, Cc), 1).astype(f32)
        for idx, (h, d) in enumerate(combos):
            lg = lg_ref[d, h]
            if d == 0:
                diff = ii - jj
                cross = jnp.exp(lg * (ii + 1.0))
                tailw = jnp.exp(lg * (Cc - 1.0 - ii))
            else:
                diff = jj - ii
                cross = jnp.exp(lg * (Cc - ii))
                tailw = jnp.exp(lg * ii)
            tab_ref[idx, INTRA] = jnp.where(diff >= 0, jnp.exp(lg * jnp.maximum(diff, 0.0)), 0.0)
            tab_ref[idx, CROSS] = cross
            tab_ref[idx, TAIL] = tailw
            tab_ref[idx, DECAY] = jnp.exp(jnp.zeros((Cc, Cc), f32) + lg * Cc)

    r_ref[...] = jnp.zeros(r_ref.shape, f32)
    o_ref[...] = jnp.zeros(o_ref.shape, f32)

    def cols(h):
        return slice(h * hd, (h + 1) * hd)

    def update(idx, kc, vc):
        ks = (kc.astype(f32) * tab_ref[idx, TAIL]).astype(bf16)
        r_ref[idx] = r_ref[idx] * tab_ref[idx, DECAY] + _dot_tn(ks, vc)

    def ctx_step(s, carry):
        for idx, (h, d) in enumerate(combos):
            row0 = pl.multiple_of(lat + (s * Cc if d == 0 else (nc - 1 - s) * Cc), Cc)
            update(idx, k_ref[0, pl.ds(row0, Cc), cols(h)], v_ref[0, pl.ds(row0, Cc), cols(h)])
        return carry

    def lat_step(s, carry):
        t0s, qs, ks, vs = [], [], [], []
        for h, d in combos:
            t0 = pl.multiple_of(s * Cc if d == 0 else (nl - 1 - s) * Cc, Cc)
            row0 = t0
            t0s.append(t0)
            qs.append(q_ref[0, pl.ds(row0, Cc), cols(h)])
            ks.append(k_ref[0, pl.ds(row0, Cc), cols(h)])
            vs.append(v_ref[0, pl.ds(row0, Cc), cols(h)])
        n = len(combos)
        sc = [(_dot_nt(qs[i], ks[i]) * tab_ref[i, INTRA]).astype(bf16) for i in range(n)]
        oc = [_dot(qs[i], r_ref[i].astype(bf16)) * tab_ref[i, CROSS] for i in range(n)]
        oi = [_dot(sc[i], vs[i]) for i in range(n)]
        for i, (h, d) in enumerate(combos):
            o_ref[0, pl.ds(t0s[i], Cc), cols(h)] += oi[i] + oc[i]
        for i in range(n):
            update(i, ks[i], vs[i])
        return carry

    lax.fori_loop(0, nc, ctx_step, 0)
    lax.fori_loop(0, nl, lat_step, 0)


def _retention(lg, z, ctx):
    B, L, _ = z.shape
    T = L - ctx
    W = RET_WIDTH
    Cc = RET_CHUNK
    blk = lambda c0: pl.BlockSpec((1, L, W), lambda b: (b, 0, c0 // W))
    nchain = 2 * RET_HEADS
    return pl.pallas_call(
        functools.partial(_ret_kernel, ctx=ctx),
        grid=(B,),
        in_specs=[pl.BlockSpec(memory_space=pltpu.SMEM), blk(C_RETQ), blk(C_RETK), blk(C_RETV)],
        out_specs=pl.BlockSpec((1, T, W), lambda b: (b, 0, 0)),
        out_shape=jax.ShapeDtypeStruct((B, T, W), f32),
        scratch_shapes=[pltpu.VMEM((nchain, Cc, Cc), f32), pltpu.VMEM((nchain, 4, Cc, Cc), f32)],
        compiler_params=_params(("arbitrary",)),
        name="ret",
    )(lg, z, z, z)


def _rwkv_kernel(k_ref, v_ref, r_ref, lo_ref, w0_ref, a0_ref, w2_ref, a2_ref, kk_ref, ka_ref, rk_ref,
                 o_ref, bo_ref, s_ref, *, nc, nb, nsub):
    d = pl.program_id(1)
    s = pl.program_id(2)
    C = RWKV_CHUNK

    @pl.when(s == 0)
    def _():
        s_ref[...] = jnp.zeros(s_ref.shape, f32)

    def sub_chunk(j, carry):
        r0 = pl.multiple_of(jnp.where(d == 1, nsub - 1 - j, j) * C, C)
        rows = lambda ref: ref.at[:, pl.ds(r0, C), :]
        _rwkv_chunk(rows(k_ref), rows(v_ref), rows(r_ref), rows(lo_ref), w0_ref, a0_ref, w2_ref, a2_ref,
                    kk_ref, ka_ref, rk_ref, o_ref.at[:, :, pl.ds(r0, C), :], bo_ref.at[:, :, pl.ds(r0, C), :],
                    s_ref, rev=d == 1, emit=s >= nc, nb=nb)
        return carry

    lax.fori_loop(0, nsub, sub_chunk, 0)


def _rwkv_chunk(k_ref, v_ref, r_ref, lo_ref, w0_ref, a0_ref, w2_ref, a2_ref, kk_ref, ka_ref, rk_ref,
                o_ref, bo_ref, s_ref, *, rev, emit, nb):
    C = RWKV_CHUNK
    G = 2 * RWKV_HEAD_DIM

    ii = lax.broadcasted_iota(jnp.int32, (C, G), 0)
    lane = lax.broadcasted_iota(jnp.int32, (C, G), 1)
    jj = lane & (RWKV_HEAD_DIM - 1)
    head0 = lane < RWKV_HEAD_DIM
    dlt = jnp.where(rev, ii - jj, jj - ii)
    strict = dlt < 0
    incl = dlt <= 0
    eye = (ii == jj).astype(f32)
    gi = lax.broadcasted_iota(jnp.int32, (G, G), 0)
    gj = lax.broadcasted_iota(jnp.int32, (G, G), 1)
    blockdiag = (gi >= RWKV_HEAD_DIM) == (gj >= RWKV_HEAD_DIM)
    ones_bd = blockdiag.astype(bf16)
    tri = incl[:, 0:C].astype(bf16)
    same4 = (ii >> 2) == (jj >> 2)
    lag = [same4 & (dlt == -k) for k in (1, 2, 3)]
    offs = [((ii >> (lv + 1)) == (jj >> (lv + 1))) & ((ii >> lv) != (jj >> lv)) for lv in range(2, 6)]

    def block4_inverse(a):
        l4 = jnp.where(same4, a, 0.0)
        c1 = jnp.where(rev, pltpu.roll(l4, 1, 1), pltpu.roll(l4, G - 1, 1))
        c2 = jnp.where(rev, pltpu.roll(l4, 2, 1), pltpu.roll(l4, G - 2, 1))
        d1 = jnp.sum(jnp.where(lag[0], l4, 0.0), axis=0, keepdims=True)
        d2 = jnp.sum(jnp.where(lag[1], l4, 0.0), axis=0, keepdims=True)
        d1b = jnp.broadcast_to(d1, (8, G))
        d1n = jnp.where(rev, pltpu.roll(d1b, 1, 1), pltpu.roll(d1b, G - 1, 1))[0:1]
        two = c1 * d1
        three = two + c2 * d2 + c2 * (d1n * d1)
        return eye + l4 + jnp.where(lag[1], two, 0.0) + jnp.where(lag[2], three, 0.0)

    def segsum(x):
        return jnp.concatenate(
            [_bdot(x[:, p * G:(p + 1) * G], ones_bd) for p in range(RWKV_PAIRS)], axis=1)

    def stack(x):
        xb = x.astype(bf16)
        zero = jnp.zeros_like(xb)
        return jnp.concatenate([jnp.where(head0, xb, zero), jnp.where(head0, zero, xb)], axis=0)

    W = k_ref.shape[2]
    k_all = k_ref[...].astype(f32).reshape(nb * C, W)
    v_all = v_ref[...].astype(f32).reshape(nb * C, W)
    r_all = r_ref[...].astype(f32).reshape(nb * C, W)
    lo_all = lo_ref[:, :, 0:DECAY_LORA + ICLR_LORA].reshape(nb * C, DECAY_LORA + ICLR_LORA)
    ua = _dot(lo_all, jnp.concatenate([w2_ref[0], a2_ref[0]], axis=1))
    u_all = w0_ref[0] + ua[:, 0:W]
    softplus = jnp.maximum(-u_all, 0.0) + jnp.log1p(jnp.exp(-jnp.abs(u_all)))
    lw_all = -jnp.exp(-softplus - 0.5)
    a_all = jax.nn.sigmoid(a0_ref[0] + ua[:, W:2 * W])
    kkr = k_all * kk_ref[...]
    kd_all = k_all * (1.0 + (a_all - 1.0) * ka_ref[...])
    sums = segsum(jnp.concatenate([kkr * kkr, r_all * kd_all * rk_ref[...]], axis=0))
    kk_all = kkr * lax.rsqrt(sums[0:nb * C] + 1e-12)
    be_all = kk_all * a_all
    bonus_all = sums[nb * C:2 * nb * C] * v_all
    pre = []
    for bb in range(nb):
        rows = slice(bb * C, (bb + 1) * C)
        vx, rx, lw, kk, kd, be, bonus = (arr[rows] for arr in (v_all, r_all, lw_all, kk_all, kd_all, be_all,
                                                                bonus_all))
        cum = _split_dot_left(tri, lw)
        tot = jnp.where(rev, cum[0:1, :], cum[C - 1:C, :])
        gneg = jnp.exp(-cum)
        etot = jnp.exp(tot)
        gh = etot * gneg
        pre.append(dict(alb=-kk * jnp.exp(cum - lw), rb=rx * jnp.exp(cum), beb=be * gneg, kb=kd * gneg,
                        beh=be * gh, kh=kd * gh, etot=etot, v=vx, bonus=bonus))

    all_units = [(bb, p) for bb in range(nb) for p in range(RWKV_PAIRS)]
    sl = lambda p: slice(p * G, (p + 1) * G)
    part = lambda un, name: pre[un[0]][name][:, sl(un[1])]
    Om = {}
    for g0 in range(0, len(all_units), RWKV_UNITS_PER_GROUP):
        units = all_units[g0:g0 + RWKV_UNITS_PER_GROUP]
        S = {un: s_ref[un[0] * RWKV_PAIRS + un[1]] for un in units}
        Sb = {un: S[un].astype(bf16) for un in units}
        X = {un: part(un, "alb").astype(bf16) for un in units}
        Rb = {un: part(un, "rb").astype(bf16) for un in units}
        Ybs = {un: stack(part(un, "beb")) for un in units}
        Yks = {un: stack(part(un, "kb")) for un in units}
        Vb = {un: part(un, "v").astype(bf16) for un in units}
        Vs = {un: stack(part(un, "v")) for un in units}

        XR = {un: jnp.concatenate([X[un], Rb[un]], axis=0) for un in units}
        ABO = {un: _dot_nt(XR[un], jnp.concatenate([Ybs[un], Yks[un], Sb[un]], axis=0)) for un in units}
        APb = {un: ABO[un][:, 0:G] for un in units}
        APk = {un: ABO[un][:, G:2 * G] for un in units}
        BO = {un: ABO[un][:, 2 * G:3 * G] for un in units}
        Aab = {un: jnp.where(strict, APb[un][0:C], 0.0) for un in units}
        Aak = {un: jnp.where(strict, APk[un][0:C], 0.0).astype(bf16) for un in units}
        Pab = {un: jnp.where(incl, APb[un][C:2 * C], 0.0).astype(bf16) for un in units}
        Pak = {un: jnp.where(incl, APk[un][C:2 * C], 0.0).astype(bf16) for un in units}

        Tm = {un: block4_inverse(Aab[un]) for un in units}
        for off in offs:
            Xs = {un: _dot(jnp.where(off, Aab[un], 0.0).astype(bf16), stack(Tm[un])) for un in units}
            Tm = {un: Tm[un] + _dot(Tm[un].astype(bf16), stack(Xs[un])) for un in units}

        AV = {un: _dot(jnp.concatenate([Aak[un], Pak[un]], axis=0), Vs[un]) for un in units}
        Bm = {un: BO[un][0:C] + AV[un][0:C] for un in units}
        U = {un: _dot(Tm[un].astype(bf16), stack(Bm[un])) for un in units}
        for un in units:
            Om[un] = BO[un][C:2 * C] + AV[un][C:2 * C] + _dot(Pab[un], stack(U[un]))
        for un in units:
            bb, p = un
            upd = _dot_tn(jnp.concatenate([U[un].astype(bf16), Vb[un]], axis=0),
                          jnp.concatenate([part(un, "beh").astype(bf16), part(un, "kh").astype(bf16)], axis=0))
            s_ref[bb * RWKV_PAIRS + p] = jnp.where(blockdiag, S[un] * part(un, "etot") + upd, 0.0)

    @pl.when(emit)
    def _():
        for bb in range(nb):
            o_ref[0, bb] = jnp.concatenate([Om[bb, p] for p in range(RWKV_PAIRS)], axis=1).astype(bf16)
            bo_ref[0, bb] = pre[bb]["bonus"].astype(bf16)


def _split_dot_left(w, x):
    hi = x.astype(bf16)
    lo = (x - hi.astype(f32)).astype(bf16)
    return _dot(jnp.concatenate([w, w], axis=1), jnp.concatenate([hi, lo], axis=0))


def _rwkv(z, w0, a0, w2p, a2p, k_k, k_a, r_k, ctx):
    B, L, _ = z.shape
    T = L - ctx
    nsub = RWKV_CHUNKS_PER_STEP
    while ctx % (nsub * RWKV_CHUNK) or T % (nsub * RWKV_CHUNK):
        nsub //= 2
    C = nsub * RWKV_CHUNK
    nc, nl = ctx // C, T // C
    W = RWKV_WIDTH

    def chunk(d, s):
        fwd = jnp.where(s < nc, nl + s, s - nc)
        bwd = jnp.where(s < nc, nl + nc - 1 - s, nl - 1 - (s - nc))
        return jnp.where(d == 0, fwd, bwd)

    def ochunk(d, s):
        sl = jnp.maximum(s - nc, 0)
        return jnp.where(d == 0, sl, nl - 1 - sl)

    nb = RWKV_SAMPLES_PER_STEP if B % RWKV_SAMPLES_PER_STEP == 0 else 1
    zblk = lambda c0, w: pl.BlockSpec((nb, C, w), lambda b, d, s: (b, chunk(d, s), c0 // w))
    dpar = lambda r: pl.BlockSpec((1, r, W), lambda b, d, s: (d, 0, 0))
    par = pl.BlockSpec((1, W), lambda b, d, s: (0, 0))
    oblk = pl.BlockSpec((1, nb, C, W), lambda b, d, s: (d, b, ochunk(d, s), 0))
    G = 2 * RWKV_HEAD_DIM
    return pl.pallas_call(
        functools.partial(_rwkv_kernel, nc=nc, nb=nb, nsub=nsub),
        grid=(B // nb, 2, nc + nl),
        in_specs=[zblk(C_RWK, W), zblk(C_RWV, W), zblk(C_RWR, W), zblk(C_LORA, 256),
                  dpar(1), dpar(1), dpar(G), dpar(G), par, par, par],
        out_specs=[oblk, oblk],
        out_shape=[jax.ShapeDtypeStruct((2, B, T, W), bf16), jax.ShapeDtypeStruct((2, B, T, W), bf16)],
        scratch_shapes=[pltpu.VMEM((nb * RWKV_PAIRS, G, G), f32)],
        compiler_params=_params(("arbitrary", "arbitrary", "arbitrary")),
        name="rwkv",
    )(z, z, z, z, w0, a0, w2p, a2p, k_k, k_a, r_k)


def _merge_kernel(x_ref, ret_ref, rw0_ref, rw1_ref, b0_ref, b1_ref, gate_ref, retg_ref, gd_ref, mod_ref,
                  ng_ref, gn_ref, lng_ref, lnb_ref, g2_ref, wbr_ref, wbw_ref, wout_ref, wrh_ref, wrl_ref,
                  x1_ref, h2_ref, lt_ref):
    hd = RET_HEAD_DIM
    ret = ret_ref[0]
    parts = []
    for hh in range(RET_HEADS):
        xh = ret[:, hh * hd:(hh + 1) * hd]
        mu = jnp.mean(xh, axis=-1, keepdims=True)
        dv = xh - mu
        var = jnp.mean(dv * dv, axis=-1, keepdims=True)
        parts.append(dv * lax.rsqrt(var + RET_EPS))
    yr = retg_ref[0].astype(f32) * (jnp.concatenate(parts, axis=1) * gn_ref[...])
    y_ret = _bdot(yr, wbr_ref[...])
    W = RWKV_WIDTH
    gi = lax.broadcasted_iota(jnp.int32, (W, W), 0)
    gj = lax.broadcasted_iota(jnp.int32, (W, W), 1)
    ones_bd = ((gi >> 6) == (gj >> 6)).astype(bf16)
    o = rw0_ref[0, 0].astype(f32) + rw1_ref[0, 0].astype(f32)
    mu = _split_dot(o, ones_bd) * (1.0 / RWKV_HEAD_DIM)
    dv = o - mu
    var = _bdot(dv * dv, ones_bd) * (1.0 / RWKV_HEAD_DIM)
    yw = dv * lax.rsqrt(var + RWKV_EPS) * lng_ref[...] + lnb_ref[...]
    gate = _dot(gd_ref[0], g2_ref[...])
    yw = (yw + b0_ref[0, 0].astype(f32) + b1_ref[0, 0].astype(f32)) * gate
    y_rw = _bdot(yw, wbw_ref[...])
    D = y_ret.shape[1]
    g = gate_ref[0].astype(f32)
    m = g[:, :D] * y_ret + g[:, D:] * y_rw
    y = _bdot(m, wout_ref[...])

    def rms(v, gg):
        return v * lax.rsqrt(jnp.mean(v * v, axis=-1, keepdims=True) + NORM_EPS) * gg

    x1 = x_ref[0] + mod_ref[0, 0:1, :] * rms(y, ng_ref[0:1, :])
    x1_ref[0] = x1
    h2 = rms(x1, ng_ref[1:2, :]) * (1.0 + mod_ref[0, 2:3, :]) + mod_ref[0, 1:2, :]
    h2b = h2.astype(bf16)
    h2_ref[0] = h2b
    h2l = (h2 - h2b.astype(f32)).astype(bf16)
    lgt = _dot(jnp.concatenate([h2b, h2l, h2b], axis=1),
               jnp.concatenate([wrh_ref[...], wrh_ref[...], wrl_ref[...]], axis=0))
    lt_ref[0] = lgt.T[0:N_EXPERTS, :]


def _merge(x, ret_o, rw_o, bonus, z, mod2, ng12, gn, lng, lnb, g2, wbr, wbw, wout, wrh, wrl):
    B, T, D = x.shape
    tm = MERGE_ROWS if T % MERGE_ROWS == 0 else ROWS
    W = RWKV_WIDTH
    row = lambda w: pl.BlockSpec((1, tm, w), lambda b, i: (b, i, 0))
    dblk = lambda dd: pl.BlockSpec((1, 1, tm, W), lambda b, i: (dd, b, i, 0))
    zblk = lambda c0, w: pl.BlockSpec((1, tm, w), lambda b, i: (b, i, c0 // w))
    full = lambda a: pl.BlockSpec(a.shape, lambda b, i: (0,) * a.ndim)
    return pl.pallas_call(
        _merge_kernel,
        grid=(B, T // tm),
        in_specs=[row(D), row(W), dblk(0), dblk(1), dblk(0), dblk(1),
                  zblk(C_MERGE, 2 * D), zblk(C_RETG, W), zblk(C_LORA + 128, 128),
                  pl.BlockSpec((1, 3, D), lambda b, i: (b, 0, 0)),
                  full(ng12), full(gn), full(lng), full(lnb), full(g2), full(wbr), full(wbw), full(wout),
                  full(wrh), full(wrl)],
        out_specs=[row(D), row(D), pl.BlockSpec((1, N_EXPERTS, tm), lambda b, i: (b, 0, i))],
        out_shape=[jax.ShapeDtypeStruct((B, T, D), f32), jax.ShapeDtypeStruct((B, T, D), bf16),
                   jax.ShapeDtypeStruct((B, N_EXPERTS, T), f32)],
        compiler_params=_params(("arbitrary", "arbitrary")),
        name="merge",
    )(x, ret_o, rw_o, rw_o, bonus, bonus, z, z, z, mod2, ng12, gn, lng, lnb, g2, wbr, wbw, wout, wrh, wrl)


def _route_kernel(lt_ref, slot_ref, rt_ref, cnt_ref, slotf_ref, gate_ref, *, cap):
    B, NE, T = lt_ref.shape
    lg = lt_ref[...]
    mx = jnp.max(lg, axis=1, keepdims=True)
    ex = jnp.exp(lg - mx)
    aff = (ex / jnp.sum(ex, axis=1, keepdims=True)).reshape(B * NE, T)
    E = B * NE

    def count_ge(cand):
        return jnp.sum((aff >= cand).astype(f32), axis=1, keepdims=True)

    def exp_step(_, kk):
        k_lo, k_hi = kk
        km = jnp.floor((k_lo + k_hi) * 0.5)
        ok = count_ge(jnp.exp2(-km)) >= cap
        return jnp.where(ok, k_lo, km), jnp.where(ok, km, k_hi)

    k_lo, k_hi = lax.fori_loop(0, ROUTE_EXP_STEPS, exp_step,
                               (jnp.full((E, 1), -1.0, f32), jnp.full((E, 1), ROUTE_MAX_EXP, f32)))
    lo0 = jnp.where(k_hi >= ROUTE_MAX_EXP, 0.0, jnp.exp2(-k_hi))
    hi0 = jnp.exp2(-k_lo)

    def val_step(_, lh):
        lo, hi = lh
        mid = (lo + hi) * 0.5
        ok = count_ge(mid) >= cap
        return jnp.where(ok, mid, lo), jnp.where(ok, hi, mid)

    lo, hi = lax.fori_loop(0, ROUTE_VAL_STEPS, val_step, (lo0, hi0))
    gt = aff >= hi
    eq = (aff >= lo) & (aff < hi)
    need = cap - jnp.sum(gt.astype(f32), axis=1, keepdims=True)
    tri = (lax.broadcasted_iota(jnp.int32, (T, T), 0) < lax.broadcasted_iota(jnp.int32, (T, T), 1)).astype(bf16)
    eq_before = _dot(eq.astype(bf16), tri)
    sel = gt | (eq & (eq_before < need))
    slot = _dot(sel.astype(bf16), tri)
    slot_f = jnp.where(sel, slot, -1.0)
    slot_ref[...] = slot_f.astype(jnp.int32).reshape(B, NE, T)
    before = (lax.broadcasted_iota(jnp.int32, (T, 128), 0)
              < lax.broadcasted_iota(jnp.int32, (T, 128), 1) * GATHER_TILE).astype(bf16)
    cnt_ref[...] = _dot(sel.astype(bf16), before).astype(jnp.int32).reshape(B, NE, 128)
    slotf_ref[...] = slot_f.reshape(B, NE, T)
    gate_ref[...] = jnp.where(sel, aff, 0.0).reshape(B, NE, T)

    def transpose_sample(b, carry):
        packed = jnp.concatenate([slotf_ref[b], gate_ref[b], jnp.zeros((128 - 2 * NE, T), f32)], axis=0)
        rt_ref[b] = packed.T.astype(bf16)
        return carry

    lax.fori_loop(0, B, transpose_sample, 0)


def _route(lt, cap):
    B, E, T = lt.shape
    assert cap <= 256 and T % GATHER_TILE == 0 and T // GATHER_TILE < 128
    return pl.pallas_call(
        functools.partial(_route_kernel, cap=cap),
        grid=(1,),
        in_specs=[pl.BlockSpec((B, E, T), lambda i: (0, 0, 0))],
        out_specs=[pl.BlockSpec((B, E, T), lambda i: (0, 0, 0)), pl.BlockSpec((B, T, 128), lambda i: (0, 0, 0)),
                   pl.BlockSpec((B, E, 128), lambda i: (0, 0, 0))],
        out_shape=[jax.ShapeDtypeStruct((B, E, T), jnp.int32), jax.ShapeDtypeStruct((B, T, 128), bf16),
                   jax.ShapeDtypeStruct((B, E, 128), jnp.int32)],
        scratch_shapes=[pltpu.VMEM((B, E, T), f32), pltpu.VMEM((B, E, T), f32)],
        compiler_params=_params(("arbitrary",)),
        name="route",
    )(lt)


def _gather_kernel(cnt_ref, slot_ref, h_ref, o_ref, acc_ref, *, cap):
    b = pl.program_id(0)
    E = slot_ref.shape[1]
    T, D = h_ref.shape[1], h_ref.shape[2]
    GT, GW, GA = GATHER_TILE, GATHER_WINDOW, GATHER_ALIGN
    nt = T // GT
    base = [[pl.multiple_of(lax.shift_left(lax.shift_right_logical(cnt_ref[b, e, j], GATHER_ALIGN_LOG2),
                                           GATHER_ALIGN_LOG2), GA) for j in range(nt)] for e in range(E)]
    fits = None
    for e in range(E):
        for j in range(nt):
            ok = cnt_ref[b, e, j + 1] - base[e][j] <= GW
            fits = ok if fits is None else fits & ok

    @pl.when(fits)
    def _():
        acc_ref[...] = jnp.zeros(acc_ref.shape, bf16)
        win = lax.broadcasted_iota(jnp.int32, (GW, GT), 0)
        for j in range(nt):
            onehot = jnp.concatenate([(slot_ref[0, e, j:j + 1, :] == base[e][j] + win).astype(bf16)
                                      for e in range(E)], axis=0)
            res = _dot(onehot, h_ref[0, j * GT:(j + 1) * GT, :]).astype(bf16)
            for e in range(E):
                acc_ref[e, pl.ds(base[e][j], GW), :] += res[e * GW:(e + 1) * GW]
        for e in range(E):
            o_ref[0, e] = acc_ref[e, 0:cap, :]

    @pl.when(jnp.logical_not(fits))
    def _():
        rows = lax.broadcasted_iota(jnp.int32, (cap, GT), 0)

        def per_expert(e, carry):
            acc = jnp.zeros((cap, D), f32)
            for j in range(nt):
                onehot = (slot_ref[0, e, j:j + 1, :] == rows).astype(bf16)
                acc = acc + _dot(onehot, h_ref[0, j * GT:(j + 1) * GT, :])
            o_ref[0, e] = acc.astype(bf16)
            return carry

        lax.fori_loop(0, E, per_expert, 0)


def _gather(cnt, slot4, h2, cap):
    B, T, D = h2.shape
    E = slot4.shape[1]
    grid_spec = pltpu.PrefetchScalarGridSpec(
        num_scalar_prefetch=1,
        grid=(B,),
        in_specs=[pl.BlockSpec((1, E, T // GATHER_TILE, GATHER_TILE), lambda b, c: (b, 0, 0, 0)),
                  pl.BlockSpec((1, T, D), lambda b, c: (b, 0, 0))],
        out_specs=pl.BlockSpec((1, E, cap, D), lambda b, c: (b, 0, 0, 0)),
        scratch_shapes=[pltpu.VMEM((E, cap + GATHER_WINDOW, D), bf16)])
    return pl.pallas_call(
        functools.partial(_gather_kernel, cap=cap),
        grid_spec=grid_spec,
        out_shape=jax.ShapeDtypeStruct((B, E, cap, D), bf16),
        compiler_params=_params(("arbitrary",)),
        name="gather",
    )(cnt, slot4, h2)


def _ffn_kernel(x_ref, wg_ref, wu_ref, wd_ref, o_ref, wgu_ref, wdb_ref, *, cap):
    F = wg_ref.shape[2]

    @pl.when(pl.program_id(1) == 0)
    def _():
        wgu_ref[:, 0:F] = wg_ref[0].astype(bf16)
        wgu_ref[:, F:2 * F] = wu_ref[0].astype(bf16)
        wdb_ref[...] = wd_ref[0].astype(bf16)

    nb = x_ref.shape[0]
    xg = jnp.concatenate([x_ref[bb, 0] for bb in range(nb)], axis=0)
    hgu = _dot(xg, wgu_ref[...])
    hg = hgu[:, 0:F]
    hu = hgu[:, F:2 * F]
    hid = (hg * _sigmoid(hg) * hu).astype(bf16)
    out = _dot(hid, wdb_ref[...]).astype(bf16)
    for bb in range(nb):
        o_ref[bb, 0] = out[bb * cap:(bb + 1) * cap]


def _ffn(xg, wg, wu, wd, cap):
    B, E, _, D = xg.shape
    F = wg.shape[2]
    nb = FFN_SAMPLES_PER_STEP if B % FFN_SAMPLES_PER_STEP == 0 else 1
    return pl.pallas_call(
        functools.partial(_ffn_kernel, cap=cap),
        grid=(E, B // nb),
        in_specs=[pl.BlockSpec((nb, 1, cap, D), lambda e, b: (b, e, 0, 0)),
                  pl.BlockSpec((1, D, F), lambda e, b: (e, 0, 0)),
                  pl.BlockSpec((1, D, F), lambda e, b: (e, 0, 0)),
                  pl.BlockSpec((1, F, D), lambda e, b: (e, 0, 0))],
        out_specs=pl.BlockSpec((nb, 1, cap, D), lambda e, b: (b, e, 0, 0)),
        out_shape=jax.ShapeDtypeStruct((B, E, cap, D), bf16),
        scratch_shapes=[pltpu.VMEM((D, 2 * F), bf16), pltpu.VMEM((F, D), bf16)],
        compiler_params=_params(("arbitrary", "arbitrary")),
        name="ffn",
    )(xg, wg, wu, wd)


def _combine_kernel(cnt_ref, rt_ref, eo_ref, x1_ref, mod_ref, ng_ref, o_ref, y_ref, *, cap):
    b = pl.program_id(0)
    i = pl.program_id(1)
    rt = rt_ref[0].astype(f32)
    tm = rt.shape[0]
    E = eo_ref.shape[1]
    W = min(SCATTER_WINDOW, cap)
    GS = SCATTER_GROUP
    ST = GATHER_TILE
    for st in range(tm // ST):
        t = i * (tm // ST) + st
        r = rt[st * ST:(st + 1) * ST]
        base, fits = [], None
        for e in range(E):
            lo = cnt_ref[b, e, t]
            be = jnp.minimum(lax.shift_left(lax.shift_right_logical(lo, GATHER_ALIGN_LOG2), GATHER_ALIGN_LOG2),
                             cap - W)
            base.append(pl.multiple_of(be, GATHER_ALIGN))
            ok = cnt_ref[b, e, t + 1] - be <= W
            fits = ok if fits is None else fits & ok

        def scatter_matrix(e, first, width, r=r):
            cols = (first + lax.broadcasted_iota(jnp.int32, (ST, width), 1)).astype(f32)
            return jnp.where(r[:, e:e + 1] == cols, r[:, E + e:E + e + 1], 0.0).astype(bf16)

        @pl.when(fits)
        def _():
            y = jnp.zeros((ST, y_ref.shape[1]), f32)
            for g in range(0, E, GS):
                p = jnp.concatenate([scatter_matrix(e, base[e], W) for e in range(g, g + GS)], axis=1)
                rows = jnp.concatenate([eo_ref[0, e, pl.ds(base[e], W), :] for e in range(g, g + GS)], axis=0)
                y = y + _dot(p, rows)
            y_ref[st * ST:(st + 1) * ST, :] = y

        @pl.when(jnp.logical_not(fits))
        def _():
            y = jnp.zeros((ST, y_ref.shape[1]), f32)
            for e in range(E):
                y = y + _dot(scatter_matrix(e, 0, cap), eo_ref[0, e])
            y_ref[st * ST:(st + 1) * ST, :] = y

    y = y_ref[...]
    yn = y * lax.rsqrt(jnp.mean(y * y, axis=-1, keepdims=True) + NORM_EPS) * ng_ref[...]
    o_ref[0] = x1_ref[0] + mod_ref[0] * yn


def _combine(cnt, rt, eo, x1, g2mod, ng3, cap):
    B, T, D = x1.shape
    E = eo.shape[1]
    tm = 512 if T % 512 == 0 else T
    assert E % SCATTER_GROUP == 0 and tm % GATHER_TILE == 0
    grid_spec = pltpu.PrefetchScalarGridSpec(
        num_scalar_prefetch=1,
        grid=(B, T // tm),
        in_specs=[pl.BlockSpec((1, tm, 128), lambda b, i, c: (b, i, 0)),
                  pl.BlockSpec((1, E, cap, D), lambda b, i, c: (b, 0, 0, 0)),
                  pl.BlockSpec((1, tm, D), lambda b, i, c: (b, i, 0)),
                  pl.BlockSpec((1, 1, D), lambda b, i, c: (b, 0, 0)),
                  pl.BlockSpec((1, D), lambda b, i, c: (0, 0))],
        out_specs=pl.BlockSpec((1, tm, D), lambda b, i, c: (b, i, 0)),
        scratch_shapes=[pltpu.VMEM((tm, D), f32)])
    return pl.pallas_call(
        functools.partial(_combine_kernel, cap=cap),
        grid_spec=grid_spec,
        out_shape=jax.ShapeDtypeStruct((B, T, D), f32),
        compiler_params=_params(("arbitrary", "arbitrary")),
        name="combine",
    )(cnt, rt, eo, x1, g2mod, ng3)


def _permute_columns(w):
    sk, sv, rk, rv, wd, ad = 0, 512, 1024, 1536, 2048, 2112
    q0 = 2176
    rq, rg, rr, gd, mg = q0, q0 + 512, q0 + 1024, q0 + 1536, q0 + 1664
    order = [(mg, 2048), (sk, 512), (sv, 512), (rq, 512), (rg, 512), (rk, 512), (rv, 512), (rr, 512),
             (wd, 64), (ad, 64), (gd, 128)]
    parts = [w[:, a:a + n] for a, n in order]
    parts.append(jnp.zeros((w.shape[0], IN_COLS - USED_COLS), w.dtype))
    return jnp.concatenate(parts, axis=1)


def _rope_tables(T, CT):
    t = jnp.arange(T)
    nfreq = RET_HEAD_DIM // 4
    inv = ROPE_BASE ** (-jnp.arange(nfreq, dtype=f32) / nfreq)
    ang = jnp.concatenate([(t // GRID_W).astype(f32)[:, None] * inv,
                           (t % GRID_W).astype(f32)[:, None] * inv], axis=-1)
    cos, sin = jnp.cos(ang), jnp.sin(ang)
    cosf = jnp.concatenate([cos, cos], axis=1)
    sinf = jnp.concatenate([-sin, sin], axis=1)
    return (jnp.concatenate([cosf, jnp.ones((CT, RET_HEAD_DIM), f32)], axis=0),
            jnp.concatenate([sinf, jnp.zeros((CT, RET_HEAD_DIM), f32)], axis=0))


def kernel(x, c, ctx, c_ctx, w_mod, b_mod, norm_g, w_in, ret_log_decay, ret_gn_g, rwkv_mu, rwkv_k_k, rwkv_k_a,
           rwkv_r_k, rwkv_w0, rwkv_w2, rwkv_a0, rwkv_a2, rwkv_g2, rwkv_ln_g, rwkv_ln_b, w_br_ret, w_br_rwkv,
           w_out, w_router, w_gate, w_up, w_down):
    B, T, D = x.shape
    CT = ctx.shape[1]
    assert w_mod.shape[0] == 1 and D == D_MODEL
    assert CT % ROWS == 0 and T % ROWS == 0 and T % GRID_W == 0
    cap = CAPACITY_FACTOR * T // N_EXPERTS
    assert cap % 8 == 0

    mrows = -(-(B + 1) // 8) * 8
    cc = jnp.zeros((mrows, D), f32).at[:B].set(c).at[B].set(c_ctx)
    mod = _modulation(cc, w_mod[0], b_mod[0])
    lat = mod[:B].reshape(B, N_MOD, D)
    cm = jnp.broadcast_to(mod[B].reshape(1, N_MOD, D), (B, N_MOD, D))
    modrows = jnp.concatenate([lat[:, 0:2], cm[:, 0:2]], axis=1)

    w_perm = _permute_columns(w_in[0]).astype(bf16)
    mu = rwkv_mu[0]
    ss = 2 * RWKV_WIDTH + DECAY_LORA + ICLR_LORA
    mu_full = jnp.zeros((2, IN_COLS), f32)
    mu_full = mu_full.at[:, C_RWK:C_RWK + 1024].set(mu[:, 0:1024])
    mu_full = mu_full.at[:, C_RWR:C_RWR + 512].set(mu[:, ss:ss + 512])
    mu_full = mu_full.at[:, C_LORA:C_LORA + 128].set(mu[:, 1024:ss])
    mu_full = mu_full.at[:, C_LORA + 128:USED_COLS].set(mu[:, ss + 512:])
    cosf, sinf = _rope_tables(T, CT)
    z = _inproj(x, ctx, modrows, norm_g[0, 0:1], w_perm, mu_full, cosf, sinf)

    lg = -jnp.exp(ret_log_decay[0].astype(f32))
    ret_o = _retention(lg, z, CT)

    G = 2 * RWKV_HEAD_DIM
    w2p = jnp.zeros((2, G, RWKV_WIDTH), f32).at[:, :DECAY_LORA].set(rwkv_w2[0]).astype(bf16)
    a2p = jnp.zeros((2, G, RWKV_WIDTH), f32).at[:, DECAY_LORA:].set(rwkv_a2[0]).astype(bf16)
    rw_o, bonus = _rwkv(z, rwkv_w0[0][:, None, :], rwkv_a0[0][:, None, :], w2p, a2p,
                        rwkv_k_k[0][None], rwkv_k_a[0][None], rwkv_r_k[0][None], CT)

    mod2 = jnp.stack([lat[:, 2], lat[:, 3], lat[:, 4]], axis=1)
    wr_pad = jnp.zeros((D, 128), f32).at[:, :N_EXPERTS].set(w_router[0])
    wr_hi = wr_pad.astype(bf16)
    wr_lo = (wr_pad - wr_hi.astype(f32)).astype(bf16)
    x1, h2, lt = _merge(x, ret_o, rw_o, bonus, z, mod2, norm_g[0, 1:3], ret_gn_g[0][None], rwkv_ln_g[0][None],
                        rwkv_ln_b[0][None], rwkv_g2[0].astype(bf16), w_br_ret[0].astype(bf16),
                        w_br_rwkv[0].astype(bf16), w_out[0].astype(bf16), wr_hi, wr_lo)

    slot, rt, cnt = _route(lt, cap)
    cnt = cnt[:, :, :T // GATHER_TILE + 1]
    xg = _gather(cnt, slot.reshape(B, N_EXPERTS, T // GATHER_TILE, GATHER_TILE), h2, cap)
    eo = _ffn(xg, w_gate[0], w_up[0], w_down[0], cap)
    return _combine(cnt, rt, eo, x1, lat[:, 5:6], norm_g[0, 3:4], cap)
```
